```python
import math
import jax, jax.numpy as jnp
from jax import lax
import numpy as np

D_MODEL = 1024
BATCH = 32
SEQ = 2048
DEPTH = 2

N_MEM = 256
EPS = 1e-6
NEG = -1e30
F_MIN = 1e-12
ATT_HEADS = 8
ATT_HEAD_DIM = 64
ATT_W = ATT_HEADS * ATT_HEAD_DIM
DILATED_PATTERNS = ((128, 1), (512, 4), (2048, 16))
BLK = 128
HG_HEADS = 4
HG_HEAD_DIM = 128
HG_W = HG_HEADS * HG_HEAD_DIM
HG_CHUNK = 32
MIX_W = ATT_W + HG_W
IN_W = 3 * ATT_W + 4 * HG_W
MEM_HEADS = 4
MEM_HEAD_DIM = D_MODEL // MEM_HEADS
D_FF = 4 * D_MODEL

kernel_name = "hybrid_dilated_attn_hgrn2_block"


def rms_norm(x, g):
    xf = x.astype(jnp.float32)
    y = xf * lax.rsqrt(jnp.mean(xf * xf, axis=-1, keepdims=True) + EPS)
    return (y * g.astype(jnp.float32)).astype(x.dtype)


def dilated_branch(q, k, v, window, dilation, slopes):
    B_, S_, H_, E_ = q.shape
    steps = window // dilation
    L = S_ // dilation
    nb = -(-L // BLK)
    Lp = nb * BLK

    def split(t):
        t = t.reshape(B_, L, dilation, H_, E_).transpose(0, 2, 3, 1, 4)
        t = jnp.pad(t, ((0, 0), (0, 0), (0, 0), (0, Lp - L), (0, 0)))
        return t.reshape(B_, dilation, H_, nb, BLK, E_)

    def with_prev(t):
        prev = jnp.pad(t, ((0, 0), (0, 0), (0, 0), (1, 0), (0, 0), (0, 0)))[:, :, :, :-1]
        return jnp.concatenate([prev, t], axis=-2)

    qb = split(q)
    kx = with_prev(split(k))
    vx = with_prev(split(v))
    s = jnp.einsum('bdhnqe,bdhnke->bdhnqk', qb, kx) * (1.0 / math.sqrt(E_))
    qi = jnp.arange(BLK)[:, None]
    kj = jnp.arange(2 * BLK)[None, :]
    step = qi + BLK - kj
    blk = jnp.arange(nb)[:, None, None]
    valid = (step >= 0) & (step <= steps) & (blk * BLK + kj - BLK >= 0)
    bias = -slopes[:, None, None, None] * (step * dilation).astype(jnp.float32)[None, None]
    s = jnp.where(valid, s + bias, NEG)
    lse = jax.nn.logsumexp(s, axis=-1)
    p = jnp.where(valid, jnp.exp(s - lse[..., None]), 0.0)
    o = jnp.einsum('bdhnqk,bdhnke->bdhnqe', p, vx)
    o = o.reshape(B_, dilation, H_, Lp, E_)[:, :, :, :L].transpose(0, 3, 1, 2, 4).reshape(B_, S_, H_, E_)
    lse = lse.reshape(B_, dilation, H_, Lp)[..., :L].transpose(0, 3, 1, 2).reshape(B_, S_, H_)
    return o, lse


def hgrn2_recurrence(q, f_logit, i, lb):
    B_, S_, H_, K_ = q.shape
    q = jax.nn.silu(q.astype(jnp.float32))
    i = i.astype(jnp.float32)
    fl = f_logit.astype(jnp.float32)
    f = lb + (1.0 - lb) * jax.nn.sigmoid(fl)
    log_f = jnp.log(jnp.maximum(f, F_MIN))
    k = (1.0 - lb) * jax.nn.sigmoid(-fl)
    n = S_ // HG_CHUNK

    def chunks(t):
        return t.reshape(B_, n, HG_CHUNK, H_, K_).transpose(1, 0, 3, 2, 4)

    qc, kc, vc, gc = chunks(q), chunks(k), chunks(i), chunks(log_f)
    bc = jnp.cumsum(gc, axis=-2)
    causal = jnp.tril(jnp.ones((HG_CHUNK, HG_CHUNK), dtype=bool))[:, :, None]

    def step(state, inp):
        qt, kt, vt, bt = inp
        diff = bt[..., :, None, :] - bt[..., None, :, :]
        decay = jnp.where(causal, jnp.exp(jnp.where(causal, diff, 0.0)), 0.0)
        scores = jnp.einsum('bhtk,bhsk,bhtsk->bhts', qt, kt, decay)
        o = jnp.einsum('bhts,bhsv->bhtv', scores, vt) + jnp.einsum('bhtk,bhkv->bhtv', qt * jnp.exp(bt), state)
        b_last = bt[..., -1:, :]
        state = jnp.exp(b_last[..., 0, :])[..., None] * state + jnp.einsum('bhsk,bhsv->bhkv', kt * jnp.exp(b_last - bt), vt)
        return state, o

    state0 = jnp.zeros((B_, H_, K_, K_), jnp.float32)
    _, o = lax.scan(step, state0, (qc, kc, vc, bc))
    return o.transpose(1, 0, 3, 2, 4).reshape(B_, S_, H_, K_)


def _fwd_setup_inputs(seed: int = 0) -> dict:
    key = jax.random.key(seed)
    ks = jax.random.split(key, 20)
    f32 = jnp.float32

    def w(k, shape, fan_in):
        return jax.random.normal(k, shape, f32) * fan_in ** -0.5

    def gain(k, shape):
        return 1.0 + 0.02 * jax.random.normal(k, shape, f32)

    return {
        "x": jax.random.normal(ks[0], (BATCH, SEQ, D_MODEL), f32),
        "mem": jax.random.normal(ks[1], (BATCH, N_MEM, D_MODEL), f32),
        "norm1_g": gain(ks[2], (DEPTH, D_MODEL)),
        "w_in": w(ks[3], (DEPTH, D_MODEL, IN_W), D_MODEL),
        "attn_qn_g": gain(ks[4], (DEPTH, ATT_HEAD_DIM)),
        "attn_kn_g": gain(ks[5], (DEPTH, ATT_HEAD_DIM)),
        "hg_lb": 0.1 * jax.random.normal(ks[6], (DEPTH, HG_W), f32),
        "hg_onorm_g": gain(ks[7], (DEPTH, HG_W)),
        "w_out": w(ks[8], (DEPTH, MIX_W, D_MODEL), MIX_W),
        "norm2_g": gain(ks[9], (DEPTH, D_MODEL)),
        "mem_norm_g": gain(ks[10], (DEPTH, D_MODEL)),
        "w_mq": w(ks[11], (DEPTH, D_MODEL, D_MODEL), D_MODEL),
        "w_mkv": w(ks[12], (DEPTH, D_MODEL, 2 * D_MODEL), D_MODEL),
        "mq_norm_g": gain(ks[13], (DEPTH, MEM_HEAD_DIM)),
        "mk_norm_g": gain(ks[14], (DEPTH, MEM_HEAD_DIM)),
        "w_mo": w(ks[15], (DEPTH, D_MODEL, D_MODEL), D_MODEL),
        "norm3_g": gain(ks[16], (DEPTH, D_MODEL)),
        "w_ff1": w(ks[17], (DEPTH, D_MODEL, D_FF), D_MODEL),
        "w_ff2": w(ks[18], (DEPTH, D_FF, D_MODEL), D_FF),
    }


def _fwd_reference(x, mem, norm1_g, w_in, attn_qn_g, attn_kn_g, hg_lb, hg_onorm_g, w_out,
              norm2_g, mem_norm_g, w_mq, w_mkv, mq_norm_g, mk_norm_g, w_mo,
              norm3_g, w_ff1, w_ff2):
    B_, S_, _ = x.shape
    f32 = jnp.float32
    slopes = jnp.exp2(-8.0 / ATT_HEADS * jnp.arange(1, ATT_HEADS + 1, dtype=f32))
    p_lb = jax.nn.softmax(hg_lb.astype(f32), axis=0)
    lower_bounds = jnp.cumsum(p_lb, axis=0) - p_lb[0:1]
    cuts = np.cumsum([ATT_W, ATT_W, ATT_W, HG_W, HG_W, HG_W]).tolist()

    for l in range(DEPTH):
        h = rms_norm(x, norm1_g[l])
        z = h @ w_in[l]
        qa, ka, va, qh, fh, ih, gh = jnp.split(z, cuts, axis=-1)
        qa = rms_norm(qa.reshape(B_, S_, ATT_HEADS, ATT_HEAD_DIM), attn_qn_g[l]).astype(f32)
        ka = rms_norm(ka.reshape(B_, S_, ATT_HEADS, ATT_HEAD_DIM), attn_kn_g[l]).astype(f32)
        va = va.reshape(B_, S_, ATT_HEADS, ATT_HEAD_DIM).astype(f32)
        outs, lses = [], []
        for window, dilation in DILATED_PATTERNS:
            o_b, lse_b = dilated_branch(qa, ka, va, window, dilation, slopes)
            outs.append(o_b)
            lses.append(lse_b)
        wts = jax.nn.softmax(jnp.stack(lses, axis=0), axis=0)
        ya = jnp.sum(wts[..., None] * jnp.stack(outs, axis=0), axis=0).reshape(B_, S_, ATT_W)

        hs = (B_, S_, HG_HEADS, HG_HEAD_DIM)
        oh = hgrn2_recurrence(qh.reshape(hs), fh.reshape(hs), ih.reshape(hs),
                              lower_bounds[l].reshape(HG_HEADS, HG_HEAD_DIM))
        oh = rms_norm(oh, hg_onorm_g[l].reshape(HG_HEADS, HG_HEAD_DIM)).reshape(B_, S_, HG_W)
        yh = oh * jax.nn.silu(gh.astype(f32))

        y = jnp.concatenate([ya, yh], axis=-1).astype(x.dtype) @ w_out[l]
        x = x + y

        h = rms_norm(x, norm2_g[l])
        mn = rms_norm(mem, mem_norm_g[l])
        qm = rms_norm((h @ w_mq[l]).reshape(B_, S_, MEM_HEADS, MEM_HEAD_DIM), mq_norm_g[l]).astype(f32)
        km, vm = jnp.split(mn @ w_mkv[l], 2, axis=-1)
        km = rms_norm(km.reshape(B_, -1, MEM_HEADS, MEM_HEAD_DIM), mk_norm_g[l]).astype(f32)
        vm = vm.reshape(B_, -1, MEM_HEADS, MEM_HEAD_DIM).astype(f32)
        sm = jnp.einsum('bshe,bmhe->bhsm', qm, km) * (1.0 / math.sqrt(MEM_HEAD_DIM))
        pm = jax.nn.softmax(sm, axis=-1)
        om = jnp.einsum('bhsm,bmhe->bshe', pm, vm).reshape(B_, S_, D_MODEL)
        x = x + om.astype(x.dtype) @ w_mo[l]

        h = rms_norm(x, norm3_g[l])
        u = jnp.square(jax.nn.relu(h @ w_ff1[l]))
        x = x + u @ w_ff2[l]
    return x


import jax as _jax
import jax.numpy as _jnp

TWIN_FORMAT = 'train_step'
FWD_PARAMS = ['x', 'mem', 'norm1_g', 'w_in', 'attn_qn_g', 'attn_kn_g', 'hg_lb', 'hg_onorm_g', 'w_out', 'norm2_g', 'mem_norm_g', 'w_mq', 'w_mkv', 'mq_norm_g', 'mk_norm_g', 'w_mo', 'norm3_g', 'w_ff1', 'w_ff2']
TWIN_WEIGHTS = ['norm1_g', 'w_in', 'attn_qn_g', 'attn_kn_g', 'hg_lb', 'hg_onorm_g', 'w_out', 'norm2_g', 'mem_norm_g', 'w_mq', 'w_mkv', 'mq_norm_g', 'mk_norm_g', 'w_mo', 'norm3_g', 'w_ff1', 'w_ff2']
TWIN_DIFF_INPUT = 'x'
TWIN_INPUTS = ['x', 'mem', 'norm1_g', 'w_in', 'attn_qn_g', 'attn_kn_g', 'hg_lb', 'hg_onorm_g', 'w_out', 'norm2_g', 'mem_norm_g', 'w_mq', 'w_mkv', 'mq_norm_g', 'mk_norm_g', 'w_mo', 'norm3_g', 'w_ff1', 'w_ff2', 'loss_target', 'm_norm1_g', 'm_w_in', 'm_attn_qn_g', 'm_attn_kn_g', 'm_hg_lb', 'm_hg_onorm_g', 'm_w_out', 'm_norm2_g', 'm_mem_norm_g', 'm_w_mq', 'm_w_mkv', 'm_mq_norm_g', 'm_mk_norm_g', 'm_w_mo', 'm_norm3_g', 'm_w_ff1', 'm_w_ff2', 'v_norm1_g', 'v_w_in', 'v_attn_qn_g', 'v_attn_kn_g', 'v_hg_lb', 'v_hg_onorm_g', 'v_w_out', 'v_norm2_g', 'v_mem_norm_g', 'v_w_mq', 'v_w_mkv', 'v_mq_norm_g', 'v_mk_norm_g', 'v_w_mo', 'v_norm3_g', 'v_w_ff1', 'v_w_ff2']
TWIN_OUTPUTS = ['loss', 'grad_x', 'grad_norm1_g', 'grad_w_in', 'grad_attn_qn_g', 'grad_attn_kn_g', 'grad_hg_lb', 'grad_hg_onorm_g', 'grad_w_out', 'grad_norm2_g', 'grad_mem_norm_g', 'grad_w_mq', 'grad_w_mkv', 'grad_mq_norm_g', 'grad_mk_norm_g', 'grad_w_mo', 'grad_norm3_g', 'grad_w_ff1', 'grad_w_ff2', 'delta_norm1_g', 'delta_w_in', 'delta_attn_qn_g', 'delta_attn_kn_g', 'delta_hg_lb', 'delta_hg_onorm_g', 'delta_w_out', 'delta_norm2_g', 'delta_mem_norm_g', 'delta_w_mq', 'delta_w_mkv', 'delta_mq_norm_g', 'delta_mk_norm_g', 'delta_w_mo', 'delta_norm3_g', 'delta_w_ff1', 'delta_w_ff2', 'new_m_norm1_g', 'new_m_w_in', 'new_m_attn_qn_g', 'new_m_attn_kn_g', 'new_m_hg_lb', 'new_m_hg_onorm_g', 'new_m_w_out', 'new_m_norm2_g', 'new_m_mem_norm_g', 'new_m_w_mq', 'new_m_w_mkv', 'new_m_mq_norm_g', 'new_m_mk_norm_g', 'new_m_w_mo', 'new_m_norm3_g', 'new_m_w_ff1', 'new_m_w_ff2', 'new_v_norm1_g', 'new_v_w_in', 'new_v_attn_qn_g', 'new_v_attn_kn_g', 'new_v_hg_lb', 'new_v_hg_onorm_g', 'new_v_w_out', 'new_v_norm2_g', 'new_v_mem_norm_g', 'new_v_w_mq', 'new_v_w_mkv', 'new_v_mq_norm_g', 'new_v_mk_norm_g', 'new_v_w_mo', 'new_v_norm3_g', 'new_v_w_ff1', 'new_v_w_ff2']
TWIN_LEAF_KINDS = {'loss': 'loss', 'grad_x': 'grad_x', 'grad_norm1_g': 'grad_w', 'grad_w_in': 'grad_w', 'grad_attn_qn_g': 'grad_w', 'grad_attn_kn_g': 'grad_w', 'grad_hg_lb': 'grad_w', 'grad_hg_onorm_g': 'grad_w', 'grad_w_out': 'grad_w', 'grad_norm2_g': 'grad_w', 'grad_mem_norm_g': 'grad_w', 'grad_w_mq': 'grad_w', 'grad_w_mkv': 'grad_w', 'grad_mq_norm_g': 'grad_w', 'grad_mk_norm_g': 'grad_w', 'grad_w_mo': 'grad_w', 'grad_norm3_g': 'grad_w', 'grad_w_ff1': 'grad_w', 'grad_w_ff2': 'grad_w', 'delta_norm1_g': 'delta_w', 'delta_w_in': 'delta_w', 'delta_attn_qn_g': 'delta_w', 'delta_attn_kn_g': 'delta_w', 'delta_hg_lb': 'delta_w', 'delta_hg_onorm_g': 'delta_w', 'delta_w_out': 'delta_w', 'delta_norm2_g': 'delta_w', 'delta_mem_norm_g': 'delta_w', 'delta_w_mq': 'delta_w', 'delta_w_mkv': 'delta_w', 'delta_mq_norm_g': 'delta_w', 'delta_mk_norm_g': 'delta_w', 'delta_w_mo': 'delta_w', 'delta_norm3_g': 'delta_w', 'delta_w_ff1': 'delta_w', 'delta_w_ff2': 'delta_w', 'new_m_norm1_g': 'new_m', 'new_m_w_in': 'new_m', 'new_m_attn_qn_g': 'new_m', 'new_m_attn_kn_g': 'new_m', 'new_m_hg_lb': 'new_m', 'new_m_hg_onorm_g': 'new_m', 'new_m_w_out': 'new_m', 'new_m_norm2_g': 'new_m', 'new_m_mem_norm_g': 'new_m', 'new_m_w_mq': 'new_m', 'new_m_w_mkv': 'new_m', 'new_m_mq_norm_g': 'new_m', 'new_m_mk_norm_g': 'new_m', 'new_m_w_mo': 'new_m', 'new_m_norm3_g': 'new_m', 'new_m_w_ff1': 'new_m', 'new_m_w_ff2': 'new_m', 'new_v_norm1_g': 'new_v', 'new_v_w_in': 'new_v', 'new_v_attn_qn_g': 'new_v', 'new_v_attn_kn_g': 'new_v', 'new_v_hg_lb': 'new_v', 'new_v_hg_onorm_g': 'new_v', 'new_v_w_out': 'new_v', 'new_v_norm2_g': 'new_v', 'new_v_mem_norm_g': 'new_v', 'new_v_w_mq': 'new_v', 'new_v_w_mkv': 'new_v', 'new_v_mq_norm_g': 'new_v', 'new_v_mk_norm_g': 'new_v', 'new_v_w_mo': 'new_v', 'new_v_norm3_g': 'new_v', 'new_v_w_ff1': 'new_v', 'new_v_w_ff2': 'new_v'}


def _forward(args):
    return _fwd_reference(*[args[k] for k in FWD_PARAMS])


def _output_shape():
    out = _jax.eval_shape(lambda: _forward(_fwd_setup_inputs(0)))
    return out.shape, out.dtype

N_MICROBATCH = 1
ADAM_LR = 0.001
ADAM_B1 = 0.9
ADAM_B2 = 0.999
ADAM_EPS = 1e-08
ADAM_WD = 0.01
ADAM_STEP = 10
PER_EXAMPLE_BATCH_AXIS = {'x': 0, 'mem': 0, 'loss_target': 0}
SHARED_INPUTS = []
_WEIGHT_DTYPES = {'norm1_g': _jnp.float32, 'w_in': _jnp.float32, 'attn_qn_g': _jnp.float32, 'attn_kn_g': _jnp.float32, 'hg_lb': _jnp.float32, 'hg_onorm_g': _jnp.float32, 'w_out': _jnp.float32, 'norm2_g': _jnp.float32, 'mem_norm_g': _jnp.float32, 'w_mq': _jnp.float32, 'w_mkv': _jnp.float32, 'mq_norm_g': _jnp.float32, 'mk_norm_g': _jnp.float32, 'w_mo': _jnp.float32, 'norm3_g': _jnp.float32, 'w_ff1': _jnp.float32, 'w_ff2': _jnp.float32}
MOMENT_SCALE = {'norm1_g': 2.032518e+01, 'w_in': 5.620463e+00, 'attn_qn_g': 8.743312e+00, 'attn_kn_g': 8.713198e+00, 'hg_lb': 8.387900e-02, 'hg_onorm_g': 2.401532e+01, 'w_out': 1.050397e+01, 'norm2_g': 4.746429e-01, 'mem_norm_g': 2.632128e+00, 'w_mq': 4.761019e-01, 'w_mkv': 1.770096e+00, 'mq_norm_g': 2.354652e+00, 'mk_norm_g': 2.350821e+00, 'w_mo': 2.549136e+00, 'norm3_g': 1.966845e+02, 'w_ff1': 7.472431e+00, 'w_ff2': 3.361202e+01}


def _to_microbatches(a, axis):
    t = _jnp.moveaxis(a, axis, 0)
    t = t.reshape((N_MICROBATCH, t.shape[0] // N_MICROBATCH) + t.shape[1:])
    return _jnp.moveaxis(t, 1, axis + 1)


def setup_inputs(seed: int = 0) -> dict:
    inp = _fwd_setup_inputs(seed)
    key = _jax.random.fold_in(_jax.random.key(seed), 7919)
    shape, _ = _output_shape()
    out = dict(inp)
    out["loss_target"] = _jax.random.normal(_jax.random.fold_in(key, 0), shape, _jnp.float32)
    for i, name in enumerate(TWIN_WEIGHTS):
        w = inp[name].astype(_jnp.float32)
        if MOMENT_SCALE is None:
            s = _jnp.sqrt(_jnp.mean(_jnp.square(w)) + 1e-30)
        else:
            s = MOMENT_SCALE[name]
        km, kv = _jax.random.split(_jax.random.fold_in(key, i + 1))
        out[name] = w
        out["m_" + name] = s * _jax.random.normal(km, w.shape, _jnp.float32)
        out["v_" + name] = (s * s) * _jax.random.uniform(kv, w.shape, _jnp.float32, 0.5, 1.5)
    if N_MICROBATCH > 1:
        for name, axis in PER_EXAMPLE_BATCH_AXIS.items():
            out[name] = _to_microbatches(out[name], axis)
    return {'x': out['x'], 'mem': out['mem'], 'norm1_g': out['norm1_g'], 'w_in': out['w_in'], 'attn_qn_g': out['attn_qn_g'], 'attn_kn_g': out['attn_kn_g'], 'hg_lb': out['hg_lb'], 'hg_onorm_g': out['hg_onorm_g'], 'w_out': out['w_out'], 'norm2_g': out['norm2_g'], 'mem_norm_g': out['mem_norm_g'], 'w_mq': out['w_mq'], 'w_mkv': out['w_mkv'], 'mq_norm_g': out['mq_norm_g'], 'mk_norm_g': out['mk_norm_g'], 'w_mo': out['w_mo'], 'norm3_g': out['norm3_g'], 'w_ff1': out['w_ff1'], 'w_ff2': out['w_ff2'], 'loss_target': out['loss_target'], 'm_norm1_g': out['m_norm1_g'], 'm_w_in': out['m_w_in'], 'm_attn_qn_g': out['m_attn_qn_g'], 'm_attn_kn_g': out['m_attn_kn_g'], 'm_hg_lb': out['m_hg_lb'], 'm_hg_onorm_g': out['m_hg_onorm_g'], 'm_w_out': out['m_w_out'], 'm_norm2_g': out['m_norm2_g'], 'm_mem_norm_g': out['m_mem_norm_g'], 'm_w_mq': out['m_w_mq'], 'm_w_mkv': out['m_w_mkv'], 'm_mq_norm_g': out['m_mq_norm_g'], 'm_mk_norm_g': out['m_mk_norm_g'], 'm_w_mo': out['m_w_mo'], 'm_norm3_g': out['m_norm3_g'], 'm_w_ff1': out['m_w_ff1'], 'm_w_ff2': out['m_w_ff2'], 'v_norm1_g': out['v_norm1_g'], 'v_w_in': out['v_w_in'], 'v_attn_qn_g': out['v_attn_qn_g'], 'v_attn_kn_g': out['v_attn_kn_g'], 'v_hg_lb': out['v_hg_lb'], 'v_hg_onorm_g': out['v_hg_onorm_g'], 'v_w_out': out['v_w_out'], 'v_norm2_g': out['v_norm2_g'], 'v_mem_norm_g': out['v_mem_norm_g'], 'v_w_mq': out['v_w_mq'], 'v_w_mkv': out['v_w_mkv'], 'v_mq_norm_g': out['v_mq_norm_g'], 'v_mk_norm_g': out['v_mk_norm_g'], 'v_w_mo': out['v_w_mo'], 'v_norm3_g': out['v_norm3_g'], 'v_w_ff1': out['v_w_ff1'], 'v_w_ff2': out['v_w_ff2']}


def _loss(weights, diff, rest, loss_target):
    with _jax.named_scope("forward"):
        args = {**rest, TWIN_DIFF_INPUT: diff, **{k: w.astype(_WEIGHT_DTYPES[k]) for k, w in weights.items()}}
        y = _forward(args)
    with _jax.named_scope("loss_head"):
        err = _jnp.square(y.astype(_jnp.float32) - loss_target)
        return 0.5 * _jnp.sum(_jnp.mean(err, axis=-1)) if err.ndim else 0.5 * err


def _adamw(w, g, m, v):
    m = ADAM_B1 * m + (1.0 - ADAM_B1) * g
    v = ADAM_B2 * v + (1.0 - ADAM_B2) * _jnp.square(g)
    m_hat = m / (1.0 - ADAM_B1 ** ADAM_STEP)
    v_hat = v / (1.0 - ADAM_B2 ** ADAM_STEP)
    delta = -ADAM_LR * (m_hat / (_jnp.sqrt(v_hat) + ADAM_EPS) + ADAM_WD * w)
    return delta, m, v


def reference(x, mem, norm1_g, w_in, attn_qn_g, attn_kn_g, hg_lb, hg_onorm_g, w_out, norm2_g, mem_norm_g, w_mq, w_mkv, mq_norm_g, mk_norm_g, w_mo, norm3_g, w_ff1, w_ff2, loss_target, m_norm1_g, m_w_in, m_attn_qn_g, m_attn_kn_g, m_hg_lb, m_hg_onorm_g, m_w_out, m_norm2_g, m_mem_norm_g, m_w_mq, m_w_mkv, m_mq_norm_g, m_mk_norm_g, m_w_mo, m_norm3_g, m_w_ff1, m_w_ff2, v_norm1_g, v_w_in, v_attn_qn_g, v_attn_kn_g, v_hg_lb, v_hg_onorm_g, v_w_out, v_norm2_g, v_mem_norm_g, v_w_mq, v_w_mkv, v_mq_norm_g, v_mk_norm_g, v_w_mo, v_norm3_g, v_w_ff1, v_w_ff2):
    given = dict(x=x, mem=mem, norm1_g=norm1_g, w_in=w_in, attn_qn_g=attn_qn_g, attn_kn_g=attn_kn_g, hg_lb=hg_lb, hg_onorm_g=hg_onorm_g, w_out=w_out, norm2_g=norm2_g, mem_norm_g=mem_norm_g, w_mq=w_mq, w_mkv=w_mkv, mq_norm_g=mq_norm_g, mk_norm_g=mk_norm_g, w_mo=w_mo, norm3_g=norm3_g, w_ff1=w_ff1, w_ff2=w_ff2, loss_target=loss_target, m_norm1_g=m_norm1_g, m_w_in=m_w_in, m_attn_qn_g=m_attn_qn_g, m_attn_kn_g=m_attn_kn_g, m_hg_lb=m_hg_lb, m_hg_onorm_g=m_hg_onorm_g, m_w_out=m_w_out, m_norm2_g=m_norm2_g, m_mem_norm_g=m_mem_norm_g, m_w_mq=m_w_mq, m_w_mkv=m_w_mkv, m_mq_norm_g=m_mq_norm_g, m_mk_norm_g=m_mk_norm_g, m_w_mo=m_w_mo, m_norm3_g=m_norm3_g, m_w_ff1=m_w_ff1, m_w_ff2=m_w_ff2, v_norm1_g=v_norm1_g, v_w_in=v_w_in, v_attn_qn_g=v_attn_qn_g, v_attn_kn_g=v_attn_kn_g, v_hg_lb=v_hg_lb, v_hg_onorm_g=v_hg_onorm_g, v_w_out=v_w_out, v_norm2_g=v_norm2_g, v_mem_norm_g=v_mem_norm_g, v_w_mq=v_w_mq, v_w_mkv=v_w_mkv, v_mq_norm_g=v_mq_norm_g, v_mk_norm_g=v_mk_norm_g, v_w_mo=v_w_mo, v_norm3_g=v_norm3_g, v_w_ff1=v_w_ff1, v_w_ff2=v_w_ff2)
    weights = {n: given[n] for n in TWIN_WEIGHTS}
    shared = {n: given[n] for n in SHARED_INPUTS}
    per_example = {n: given[n] for n in ['x', 'mem']}
    grad_fn = _jax.value_and_grad(_loss, argnums=(0, 1))

    def one_microbatch(ex, loss_target):
        ex = dict(ex)
        diff = ex.pop(TWIN_DIFF_INPUT)
        return grad_fn(weights, diff, {**shared, **ex}, loss_target)

    if N_MICROBATCH == 1:
        loss, (grad_w, grad_x) = one_microbatch(per_example, given["loss_target"])
    else:
        def body(carry, xs):
            loss_sum, grad_sum = carry
            l_k, (gw_k, gx_k) = one_microbatch(xs[0], xs[1])
            with _jax.named_scope("update"):
                return (loss_sum + l_k, _jax.tree.map(_jnp.add, grad_sum, gw_k)), gx_k

        init = (_jnp.zeros((), _jnp.float32), _jax.tree.map(_jnp.zeros_like, weights))
        (loss, grad_w), grad_x = _jax.lax.scan(body, init, (per_example, given["loss_target"]))
    with _jax.named_scope("update"):
        delta_w, new_m, new_v = {}, {}, {}
        for n in TWIN_WEIGHTS:
            delta_w[n], new_m[n], new_v[n] = _adamw(weights[n], grad_w[n], given["m_" + n], given["v_" + n])
    return (loss, grad_x, *[grad_w[n] for n in TWIN_WEIGHTS], *[delta_w[n] for n in TWIN_WEIGHTS],
            *[new_m[n] for n in TWIN_WEIGHTS], *[new_v[n] for n in TWIN_WEIGHTS])
```

```python
import functools
import math

import jax
import jax.numpy as jnp
from jax import lax
from jax.experimental import pallas as pl
from jax.experimental.pallas import tpu as pltpu

F32 = jnp.float32
MXU_DTYPE = jnp.bfloat16
VMEM_LIMIT = 48 * 1024 * 1024

N_DEV = 8
EPS = 1e-6
NEG = -1e30
F_MIN = 1e-12
ATT_HEADS = 8
ATT_HEAD_DIM = 64
ATT_W = ATT_HEADS * ATT_HEAD_DIM
DILATIONS = (1, 4, 16)
DIL_STEPS = 128
HG_HEADS = 4
HG_HEAD_DIM = 128
HG_W = HG_HEADS * HG_HEAD_DIM
HG_CHUNK = 32
MEM_HEADS = 4
LANES = 128

ADAM_LR = 0.001
ADAM_B1 = 0.9
ADAM_B2 = 0.999
ADAM_EPS = 1e-08
ADAM_WD = 0.01
ADAM_STEP = 10

BIG = ("w_in", "w_out", "w_mq", "w_mkv", "w_mo", "w_ff1", "w_ff2")
BIG_SHARD_AXIS = {"w_in": 2, "w_out": 1, "w_mq": 1, "w_mkv": 2, "w_mo": 1, "w_ff1": 2, "w_ff2": 1}
SMALL = ("norm1_g", "attn_qn_g", "attn_kn_g", "hg_lb", "hg_onorm_g", "norm2_g", "mem_norm_g",
         "mq_norm_g", "mk_norm_g", "norm3_g")
WEIGHTS = ("norm1_g", "w_in", "attn_qn_g", "attn_kn_g", "hg_lb", "hg_onorm_g", "w_out", "norm2_g",
           "mem_norm_g", "w_mq", "w_mkv", "mq_norm_g", "mk_norm_g", "w_mo", "norm3_g", "w_ff1", "w_ff2")


def _pcall(body, **kw):
    return pl.pallas_call(body, **kw)


def _params():
    return pltpu.CompilerParams(vmem_limit_bytes=VMEM_LIMIT)


def _tile(n, pref, mult):
    t = (min(n, pref) // mult) * mult
    while t >= mult:
        if n % t == 0:
            return t
        t -= mult
    return n


def _sds(shape, dtype):
    return jax.ShapeDtypeStruct(shape, dtype)


def _matmul(a, b, *, ta=False, tb=False, pro=None, epi=None, extra=None, out_dtype=F32, name):
    M, K = (a.shape[1], a.shape[0]) if ta else a.shape
    N = b.shape[0] if tb else b.shape[1]
    tm = _tile(M, 512, 8 if not ta else LANES)
    tn = _tile(N, 1024, LANES)
    tk = _tile(K, 512 if ta else 1024, LANES if not ta else 8)
    nk = K // tk
    dims = (((0 if ta else 1,), (1 if tb else 0,)), ((), ()))

    def body(a_ref, b_ref, *rest):
        if extra is not None:
            e_ref, o_ref, acc_ref = rest
        else:
            o_ref, acc_ref = rest
        k = pl.program_id(2)

        @pl.when(k == 0)
        def _():
            acc_ref[...] = jnp.zeros_like(acc_ref)

        av = a_ref[...]
        if pro == "relu2":
            av = jnp.square(jnp.maximum(av, 0.0))
        acc_ref[...] += lax.dot_general(av.astype(MXU_DTYPE), b_ref[...].astype(MXU_DTYPE), dims,
                                        preferred_element_type=F32)

        @pl.when(k == nk - 1)
        def _():
            r = acc_ref[...]
            if epi == "add":
                r = r + e_ref[...]
            elif epi == "relu2grad":
                r = r * (2.0 * jnp.maximum(e_ref[...], 0.0))
            o_ref[...] = r.astype(out_dtype)

    a_spec = pl.BlockSpec((tk, tm), lambda i, j, k: (k, i)) if ta else pl.BlockSpec((tm, tk), lambda i, j, k: (i, k))
    b_spec = pl.BlockSpec((tn, tk), lambda i, j, k: (j, k)) if tb else pl.BlockSpec((tk, tn), lambda i, j, k: (k, j))
    o_spec = pl.BlockSpec((tm, tn), lambda i, j, k: (i, j))
    in_specs = [a_spec, b_spec] + ([o_spec] if extra is not None else [])
    args = (a, b) + ((extra,) if extra is not None else ())
    return _pcall(body, out_shape=_sds((M, N), out_dtype), grid=(M // tm, N // tn, nk), in_specs=in_specs,
                  out_specs=o_spec, scratch_shapes=[pltpu.VMEM((tm, tn), F32)], compiler_params=_params(),
                  name=name)(*args)


def _group_mean(v, E):
    rows, W = v.shape
    if E == W:
        return jnp.mean(v, axis=-1, keepdims=True)
    pieces = []
    if E % LANES == 0:
        for g0 in range(0, W, E):
            m = jnp.mean(v[:, g0:g0 + E], axis=-1, keepdims=True)
            pieces.append(jnp.broadcast_to(m, (rows, E)))
    else:
        lane = lax.broadcasted_iota(jnp.int32, (rows, LANES), 1)
        for c0 in range(0, W, LANES):
            vc = v[:, c0:c0 + LANES]
            acc = jnp.zeros((rows, LANES), F32)
            for s0 in range(0, LANES, E):
                msk = (lane >= s0) & (lane < s0 + E)
                m = jnp.sum(jnp.where(msk, vc, 0.0), axis=-1, keepdims=True) * (1.0 / E)
                acc = jnp.where(msk, m, acc)
            pieces.append(acc)
    return jnp.concatenate(pieces, axis=-1)


def _fold_groups(t, E):
    W = t.shape[1]
    step = max(E, LANES)
    acc = t[:, 0:step]
    for c0 in range(step, W, step):
        acc = acc + t[:, c0:c0 + step]
    sh = LANES // 2
    while sh >= E:
        acc = acc + pltpu.roll(acc, sh, 1)
        sh //= 2
    return acc


def _norm_fwd(x, g, *, E, W, cb=0, out_dtype, name):
    M = x.shape[0]
    tm = _tile(M, 512, 8)

    def body(x_ref, g_ref, o_ref):
        xv = x_ref[...]
        r = lax.rsqrt(_group_mean(xv * xv, E) + EPS)
        o_ref[...] = (xv * r * g_ref[...]).astype(out_dtype)

    return _pcall(body, out_shape=_sds((M, W), out_dtype), grid=(M // tm,),
                  in_specs=[pl.BlockSpec((tm, W), lambda i: (i, cb)), pl.BlockSpec((1, W), lambda i: (0, 0))],
                  out_specs=pl.BlockSpec((tm, W), lambda i: (i, 0)), compiler_params=_params(), name=name)(x, g)


def _norm_bwd(x, g, dy, *, E, W, cb=0, res=None, fold=None, out_dtype=F32, name):
    M = x.shape[0]
    tm = _tile(M, 512, 8)
    n = M // tm
    gw = W if fold is None else max(fold, LANES)

    def body(x_ref, g_ref, dy_ref, *rest):
        if res is not None:
            r_ref, dx_ref, dg_ref, acc_ref = rest
        else:
            dx_ref, dg_ref, acc_ref = rest
        i = pl.program_id(0)
        xv = x_ref[...]
        r = lax.rsqrt(_group_mean(xv * xv, E) + EPS)
        xh = xv * r
        dyv = dy_ref[...].astype(F32)
        dyg = dyv * g_ref[...]
        dx = r * (dyg - xh * _group_mean(dyg * xh, E))
        if res is not None:
            dx = dx + r_ref[...]
        dx_ref[...] = dx.astype(out_dtype)
        part = jnp.sum(dyv * xh, axis=0, keepdims=True)

        @pl.when(i == 0)
        def _():
            acc_ref[...] = part

        @pl.when(i > 0)
        def _():
            acc_ref[...] += part

        @pl.when(i == n - 1)
        def _():
            t = acc_ref[...]
            dg_ref[...] = t if fold is None else _fold_groups(t, fold)

    blk = pl.BlockSpec((tm, W), lambda i: (i, 0))
    in_specs = [pl.BlockSpec((tm, W), lambda i: (i, cb)), pl.BlockSpec((1, W), lambda i: (0, 0)), blk]
    args = [x, g, dy]
    if res is not None:
        in_specs.append(blk)
        args.append(res)
    return _pcall(body, out_shape=(_sds((M, W), out_dtype), _sds((1, gw), F32)), grid=(n,), in_specs=in_specs,
                  out_specs=(blk, pl.BlockSpec((1, gw), lambda i: (0, 0))),
                  scratch_shapes=[pltpu.VMEM((1, W), F32)], compiler_params=_params(), name=name)(*args)


def _dilated_bias(i, tq, S):
    d = (i * tq + lax.broadcasted_iota(jnp.int32, (tq, S), 0)) - lax.broadcasted_iota(jnp.int32, (tq, S), 1)
    cnt = jnp.zeros((tq, S), jnp.int32)
    for dil in DILATIONS:
        hit = (d <= DIL_STEPS * dil) if dil == 1 else (((d & (dil - 1)) == 0) & (d <= DIL_STEPS * dil))
        cnt = cnt + hit.astype(jnp.int32)
    ok = (d >= 0) & (cnt > 0)
    logm = jnp.where(cnt == 3, math.log(3.0), jnp.where(cnt == 2, math.log(2.0), 0.0))
    return jnp.where(ok, logm, NEG).astype(F32), d.astype(F32)


def _alibi_slope(h):
    return 2.0 ** (-8.0 / ATT_HEADS * (h + 1))


def _attn_fwd(qn, kn, z, *, B, S, name):
    T = B * S
    tq = _tile(S, 256, 8)
    nq = S // tq
    scale = 1.0 / math.sqrt(ATT_HEAD_DIM)

    def body(q_ref, k_ref, v_ref, y_ref, lse_ref):
        i = pl.program_id(1)
        base, df = _dilated_bias(i, tq, S)
        lane = lax.broadcasted_iota(jnp.int32, (1, LANES), 1)
        lo = lane < ATT_HEAD_DIM
        lse_blk = jnp.zeros((tq, LANES), F32)
        for p in range(ATT_HEADS // 2):
            cs = slice(p * LANES, (p + 1) * LANES)
            q2 = q_ref[:, cs].astype(MXU_DTYPE)
            k2 = k_ref[:, cs].astype(MXU_DTYPE)
            v2 = v_ref[:, cs].astype(MXU_DTYPE)
            outs = []
            for sub in range(2):
                h = 2 * p + sub
                sel = lo if sub == 0 else jnp.logical_not(lo)
                qm = jnp.where(sel, q2, jnp.zeros_like(q2))
                s = lax.dot_general(qm, k2, (((1,), (1,)), ((), ())), preferred_element_type=F32)
                s = s * scale + base - _alibi_slope(h) * df
                mx = jnp.max(s, axis=1, keepdims=True)
                e = jnp.exp(s - mx)
                l = jnp.sum(e, axis=1, keepdims=True)
                o = jnp.dot(e.astype(MXU_DTYPE), v2, preferred_element_type=F32) / l
                outs.append(o)
                lse_blk = jnp.where(lane == h, mx + jnp.log(l), lse_blk)
            y_ref[:, cs] = jnp.where(lo, outs[0], outs[1])
        lse_ref[...] = lse_blk

    return _pcall(
        body, out_shape=(_sds((T, ATT_W), F32), _sds((T, LANES), F32)), grid=(B, nq),
        in_specs=[pl.BlockSpec((tq, ATT_W), lambda b, i: (b * nq + i, 0)),
                  pl.BlockSpec((S, ATT_W), lambda b, i: (b, 0)),
                  pl.BlockSpec((S, ATT_W), lambda b, i: (b, 2))],
        out_specs=(pl.BlockSpec((tq, ATT_W), lambda b, i: (b * nq + i, 0)),
                   pl.BlockSpec((tq, LANES), lambda b, i: (b * nq + i, 0))),
        compiler_params=_params(), name=name)(qn, kn, z)


def _attn_bwd(qn, kn, z, y, lse, dycat, *, B, S, name):
    T = B * S
    tq = _tile(S, 256, 8)
    nq = S // tq
    scale = 1.0 / math.sqrt(ATT_HEAD_DIM)

    def body(q_ref, k_ref, v_ref, y_ref, lse_ref, dy_ref, dq_ref, dk_ref, dv_ref):
        i = pl.program_id(1)

        @pl.when(i == 0)
        def _():
            dk_ref[...] = jnp.zeros_like(dk_ref)
            dv_ref[...] = jnp.zeros_like(dv_ref)

        base, df = _dilated_bias(i, tq, S)
        lane = lax.broadcasted_iota(jnp.int32, (1, LANES), 1)
        lo = lane < ATT_HEAD_DIM
        lse_blk = lse_ref[...]
        for p in range(ATT_HEADS // 2):
            cs = slice(p * LANES, (p + 1) * LANES)
            q2 = q_ref[:, cs].astype(MXU_DTYPE)
            k2 = k_ref[:, cs].astype(MXU_DTYPE)
            v2 = v_ref[:, cs].astype(MXU_DTYPE)
            do2 = dy_ref[:, cs]
            doy = do2 * y_ref[:, cs]
            do2m = do2.astype(MXU_DTYPE)
            dqs, dks, dvs = [], [], []
            for sub in range(2):
                h = 2 * p + sub
                sel = lo if sub == 0 else jnp.logical_not(lo)
                qm = jnp.where(sel, q2, jnp.zeros_like(q2))
                s = lax.dot_general(qm, k2, (((1,), (1,)), ((), ())), preferred_element_type=F32)
                s = s * scale + base - _alibi_slope(h) * df
                lse_h = jnp.sum(jnp.where(lane == h, lse_blk, 0.0), axis=1, keepdims=True)
                pr = jnp.exp(s - lse_h)
                dsum = jnp.sum(jnp.where(sel, doy, 0.0), axis=1, keepdims=True)
                dom = jnp.where(sel, do2m, jnp.zeros_like(do2m))
                dp = lax.dot_general(dom, v2, (((1,), (1,)), ((), ())), preferred_element_type=F32)
                ds = (pr * (dp - dsum)).astype(MXU_DTYPE)
                dqs.append(jnp.dot(ds, k2, preferred_element_type=F32) * scale)
                dks.append(lax.dot_general(ds, q2, (((0,), (0,)), ((), ())), preferred_element_type=F32) * scale)
                dvs.append(lax.dot_general(pr.astype(MXU_DTYPE), do2m, (((0,), (0,)), ((), ())),
                                           preferred_element_type=F32))
            dq_ref[:, cs] = jnp.where(lo, dqs[0], dqs[1])
            dk_ref[:, cs] += jnp.where(lo, dks[0], dks[1])
            dv_ref[:, cs] += jnp.where(lo, dvs[0], dvs[1])

    qblk = pl.BlockSpec((tq, ATT_W), lambda b, i: (b * nq + i, 0))
    sblk = pl.BlockSpec((S, ATT_W), lambda b, i: (b, 0))
    return _pcall(
        body, out_shape=(_sds((T, ATT_W), F32),) * 3, grid=(B, nq),
        in_specs=[qblk, sblk, pl.BlockSpec((S, ATT_W), lambda b, i: (b, 2)), qblk,
                  pl.BlockSpec((tq, LANES), lambda b, i: (b * nq + i, 0)), qblk],
        out_specs=(qblk, sblk, sblk), compiler_params=_params(), name=name)(qn, kn, z, y, lse, dycat)


def _sigmoid_pair(x):
    en = jnp.exp(-jnp.abs(x))
    big = 1.0 / (1.0 + en)
    small = en * big
    pos = x >= 0
    return jnp.where(pos, big, small), jnp.where(pos, small, big)


def _chunk_scan(x, pos, reverse):
    n = x.shape[0]
    sh = 1
    while sh < HG_CHUNK:
        if reverse:
            x = x + jnp.where(pos < HG_CHUNK - sh, pltpu.roll(x, n - sh, 0), 0.0)
        else:
            x = x + jnp.where(pos >= sh, pltpu.roll(x, sh, 0), 0.0)
        sh *= 2
    return x


def _hgrn_gates(qh, fh, lb):
    sq, _ = _sigmoid_pair(qh)
    sg, sgn = _sigmoid_pair(fh)
    f = lb + (1.0 - lb) * sg
    g = jnp.log(jnp.maximum(f, F_MIN))
    kk = (1.0 - lb) * sgn
    return qh * sq, sq, sg, sgn, f, g, kk


def _hgrn_specs(B, S, tb):
    nb = S // tb

    def zcol(first):
        return lambda h, b, i: (b * nb + i, first + h)

    return nb, zcol


def _hgrn_fwd(z, lb, *, B, S, name):
    T = B * S
    tb = _tile(S, 256, HG_CHUNK)
    nc = tb // HG_CHUNK
    nb, zcol = _hgrn_specs(B, S, tb)
    dn_t = (((1,), (1,)), ((), ()))
    dn_o = (((0,), (0,)), ((), ()))

    def body(q_ref, f_ref, v_ref, lb_ref, o_ref, st_ref, state):
        i = pl.program_id(2)

        @pl.when(i == 0)
        def _():
            state[...] = jnp.zeros_like(state)

        q, _, _, _, _, g, kk = _hgrn_gates(q_ref[...], f_ref[...], lb_ref[...])
        v = v_ref[...]
        pos = lax.broadcasted_iota(jnp.int32, (tb, LANES), 0) % HG_CHUNK
        b = _chunk_scan(g, pos, False)
        o = jnp.sum(q * kk, axis=1, keepdims=True) * v
        for d in range(1, HG_CHUNK):
            qd = pltpu.roll(q, tb - d, 0)
            bd = pltpu.roll(b, tb - d, 0)
            e = jnp.where(pos < HG_CHUNK - d, jnp.exp(jnp.minimum(bd - b, 0.0)), 0.0)
            a = jnp.sum(qd * kk * e, axis=1, keepdims=True)
            o = o + pltpu.roll(a * v, d, 0)
        qe = q * jnp.exp(b)
        st = state[...]
        for c in range(nc):
            rs = slice(c * HG_CHUNK, (c + 1) * HG_CHUNK)
            st_ref[c * LANES:(c + 1) * LANES, :] = st
            b_last = jnp.sum(jnp.where(pos[rs] == HG_CHUNK - 1, b[rs], 0.0), axis=0, keepdims=True)
            ke = kk[rs] * jnp.exp(b_last - b[rs])
            o_ref[rs, :] = o[rs] + lax.dot_general(qe[rs].astype(MXU_DTYPE), st.astype(MXU_DTYPE), dn_t,
                                                   preferred_element_type=F32)
            st = st * jnp.exp(b_last) + lax.dot_general(v[rs].astype(MXU_DTYPE), ke.astype(MXU_DTYPE), dn_o,
                                                        preferred_element_type=F32)
        state[...] = st

    blk = (tb, LANES)
    return _pcall(
        body, out_shape=(_sds((T, HG_W), F32), _sds((T // HG_CHUNK * LANES, HG_W), F32)), grid=(HG_HEADS, B, nb),
        in_specs=[pl.BlockSpec(blk, zcol(12)), pl.BlockSpec(blk, zcol(16)), pl.BlockSpec(blk, zcol(20)),
                  pl.BlockSpec((1, LANES), lambda h, b, i: (0, h))],
        out_specs=(pl.BlockSpec(blk, lambda h, b, i: (b * nb + i, h)),
                   pl.BlockSpec((nc * LANES, LANES), lambda h, b, i: (b * nb + i, h))),
        scratch_shapes=[pltpu.VMEM((LANES, LANES), F32)], compiler_params=_params(), name=name)(z, z, z, lb)


def _hgrn_bwd(z, lb, states, do, *, B, S, name):
    T = B * S
    tb = _tile(S, 256, HG_CHUNK)
    nc = tb // HG_CHUNK
    nb = S // tb
    dn_t = (((1,), (1,)), ((), ()))
    dn_o = (((0,), (0,)), ((), ()))

    def body(q_ref, f_ref, v_ref, lb_ref, st_ref, do_ref, dq_ref, df_ref, di_ref, dlb_ref, dstate):
        bi = pl.program_id(1)
        i = pl.program_id(2)

        @pl.when(i == 0)
        def _():
            dstate[...] = jnp.zeros_like(dstate)

        qh = q_ref[...]
        lbv = lb_ref[...]
        q, sq, sg, sgn, f, g, kk = _hgrn_gates(qh, f_ref[...], lbv)
        v = v_ref[...]
        dov = do_ref[...]
        pos = lax.broadcasted_iota(jnp.int32, (tb, LANES), 0) % HG_CHUNK
        b = _chunk_scan(g, pos, False)

        a0 = jnp.sum(q * kk, axis=1, keepdims=True)
        da0 = jnp.sum(dov * v, axis=1, keepdims=True)
        dv = a0 * dov
        dkk = da0 * q
        dq = da0 * kk
        db = jnp.zeros((tb, LANES), F32)
        for d in range(1, HG_CHUNK):
            qd = pltpu.roll(q, tb - d, 0)
            bd = pltpu.roll(b, tb - d, 0)
            dod = pltpu.roll(dov, tb - d, 0)
            e = jnp.where(pos < HG_CHUNK - d, jnp.exp(jnp.minimum(bd - b, 0.0)), 0.0)
            a = jnp.sum(qd * kk * e, axis=1, keepdims=True)
            da = jnp.sum(dod * v, axis=1, keepdims=True)
            dv = dv + a * dod
            xe = da * e
            dkk = dkk + xe * qd
            t1 = xe * kk
            wt = t1 * qd
            dq = dq + pltpu.roll(t1, d, 0)
            db = db + pltpu.roll(wt, d, 0) - wt

        eb = jnp.exp(b)
        qe = q * eb
        dst = dstate[...]
        dq_c, dkk_c, dv_c, db_c = [None] * nc, [None] * nc, [None] * nc, [None] * nc
        for c in range(nc - 1, -1, -1):
            rs = slice(c * HG_CHUNK, (c + 1) * HG_CHUNK)
            st0 = st_ref[c * LANES:(c + 1) * LANES, :]
            b_last = jnp.sum(jnp.where(pos[rs] == HG_CHUNK - 1, b[rs], 0.0), axis=0, keepdims=True)
            eb_last = jnp.exp(b_last)
            er = jnp.exp(b_last - b[rs])
            ke = kk[rs] * er
            do_c = dov[rs].astype(MXU_DTYPE)
            dstm = dst.astype(MXU_DTYPE)
            dqe = jnp.dot(do_c, st0.astype(MXU_DTYPE), preferred_element_type=F32)
            dv_c[c] = dv[rs] + lax.dot_general(ke.astype(MXU_DTYPE), dstm, dn_t, preferred_element_type=F32)
            dke = jnp.dot(v[rs].astype(MXU_DTYPE), dstm, preferred_element_type=F32)
            d_eb_last = jnp.sum(dst * st0, axis=0, keepdims=True)
            dst = dst * eb_last + lax.dot_general(do_c, qe[rs].astype(MXU_DTYPE), dn_o, preferred_element_type=F32)
            dq_c[c] = dq[rs] + dqe * eb[rs]
            dkk_c[c] = dkk[rs] + dke * er
            dkeke = dke * ke
            db_last = jnp.sum(dkeke, axis=0, keepdims=True) + d_eb_last * eb_last
            db_c[c] = db[rs] + dqe * qe[rs] - dkeke + jnp.where(pos[rs] == HG_CHUNK - 1, db_last, 0.0)
        dstate[...] = dst
        dq = jnp.concatenate(dq_c, axis=0)
        dkk = jnp.concatenate(dkk_c, axis=0)
        dv = jnp.concatenate(dv_c, axis=0)
        dg = _chunk_scan(jnp.concatenate(db_c, axis=0), pos, True)

        dfv = dg * jnp.where(f > F_MIN, 1.0 / f, 0.0)
        dsg = (dfv - dkk) * (1.0 - lbv)
        dq_ref[...] = dq * (sq * (1.0 + qh * (1.0 - sq)))
        df_ref[...] = dsg * sg * sgn
        di_ref[...] = dv
        part = jnp.sum(dfv * sgn - dkk * sgn, axis=0, keepdims=True)

        @pl.when((bi == 0) & (i == 0))
        def _():
            dlb_ref[...] = part

        @pl.when((bi > 0) | (i > 0))
        def _():
            dlb_ref[...] += part

    def rev(first):
        return lambda h, b, i: (b * nb + (nb - 1 - i), first + h)

    blk = (tb, LANES)
    oblk = pl.BlockSpec(blk, rev(0))
    return _pcall(
        body, out_shape=(_sds((T, HG_W), F32),) * 3 + (_sds((1, HG_W), F32),), grid=(HG_HEADS, B, nb),
        in_specs=[pl.BlockSpec(blk, rev(12)), pl.BlockSpec(blk, rev(16)), pl.BlockSpec(blk, rev(20)),
                  pl.BlockSpec((1, LANES), lambda h, b, i: (0, h)),
                  pl.BlockSpec((nc * LANES, LANES), rev(0)), oblk],
        out_specs=(oblk, oblk, oblk, pl.BlockSpec((1, LANES), lambda h, b, i: (0, h))),
        scratch_shapes=[pltpu.VMEM((LANES, LANES), F32)], compiler_params=_params(), name=name)(z, z, z, lb, states, do)


def _hgout_fwd(o, z, g, *, name):
    T = o.shape[0]
    tm = _tile(T, 512, 8)

    def body(o_ref, gh_ref, g_ref, y_ref):
        ov = o_ref[...]
        gh = gh_ref[...]
        r = lax.rsqrt(_group_mean(ov * ov, HG_HEAD_DIM) + EPS)
        sg, _ = _sigmoid_pair(gh)
        y_ref[...] = ov * r * g_ref[...] * (gh * sg)

    blk = pl.BlockSpec((tm, HG_W), lambda i: (i, 0))
    return _pcall(body, out_shape=_sds((T, HG_W), F32), grid=(T // tm,),
                  in_specs=[blk, pl.BlockSpec((tm, HG_W), lambda i: (i, 6)), pl.BlockSpec((1, HG_W), lambda i: (0, 0))],
                  out_specs=blk, compiler_params=_params(), name=name)(o, z, g)


def _hgout_bwd(o, z, g, dycat, *, name):
    T = o.shape[0]
    tm = _tile(T, 512, 8)

    def body(o_ref, gh_ref, g_ref, dy_ref, do_ref, dgh_ref, dg_ref):
        i = pl.program_id(0)
        ov = o_ref[...]
        gh = gh_ref[...]
        gv = g_ref[...]
        dy = dy_ref[...]
        r = lax.rsqrt(_group_mean(ov * ov, HG_HEAD_DIM) + EPS)
        xh = ov * r
        sg, _ = _sigmoid_pair(gh)
        dn = dy * (gh * sg)
        dgh_ref[...] = dy * xh * gv * (sg * (1.0 + gh * (1.0 - sg)))
        dng = dn * gv
        do_ref[...] = r * (dng - xh * _group_mean(dng * xh, HG_HEAD_DIM))
        part = jnp.sum(dn * xh, axis=0, keepdims=True)

        @pl.when(i == 0)
        def _():
            dg_ref[...] = part

        @pl.when(i > 0)
        def _():
            dg_ref[...] += part

    blk = pl.BlockSpec((tm, HG_W), lambda i: (i, 0))
    row = pl.BlockSpec((1, HG_W), lambda i: (0, 0))
    return _pcall(body, out_shape=(_sds((T, HG_W), F32), _sds((T, HG_W), F32), _sds((1, HG_W), F32)), grid=(T // tm,),
                  in_specs=[blk, pl.BlockSpec((tm, HG_W), lambda i: (i, 6)), row,
                            pl.BlockSpec((tm, HG_W), lambda i: (i, 1))],
                  out_specs=(blk, blk, row), compiler_params=_params(), name=name)(o, z, g, dycat)


def _lower_bounds_fwd(hg_lb):
    L = hg_lb.shape[0]

    def body(x_ref, o_ref):
        rows = [x_ref[l:l + 1, :] for l in range(L)]
        mx = functools.reduce(jnp.maximum, rows)
        es = [jnp.exp(r - mx) for r in rows]
        tot = functools.reduce(jnp.add, es)
        acc = jnp.zeros_like(tot)
        for l in range(L):
            acc = acc + es[l] / tot
            o_ref[l:l + 1, :] = acc - es[0] / tot

    return _pcall(body, out_shape=_sds(hg_lb.shape, F32), name="lower_bounds_fwd")(hg_lb)


def _lower_bounds_bwd(hg_lb, dlower):
    L = hg_lb.shape[0]

    def body(x_ref, d_ref, o_ref):
        rows = [x_ref[l:l + 1, :] for l in range(L)]
        ds = [d_ref[l:l + 1, :] for l in range(L)]
        mx = functools.reduce(jnp.maximum, rows)
        es = [jnp.exp(r - mx) for r in rows]
        tot = functools.reduce(jnp.add, es)
        ps = [e / tot for e in es]
        dps = []
        for j in range(L):
            t = functools.reduce(jnp.add, ds[j:])
            if j == 0:
                t = t - functools.reduce(jnp.add, ds)
            dps.append(t)
        inner = functools.reduce(jnp.add, [p * dp for p, dp in zip(ps, dps)])
        for j in range(L):
            o_ref[j:j + 1, :] = ps[j] * (dps[j] - inner)

    return _pcall(body, out_shape=_sds(hg_lb.shape, F32), name="lower_bounds_bwd")(hg_lb, dlower)


def _memattn_fwd(qm, km, kv, *, B, S, NM, name):
    T, D = qm.shape
    E = D // MEM_HEADS
    tq = _tile(S, 512, 8)
    nq = S // tq
    scale = 1.0 / math.sqrt(E)

    def body(q_ref, k_ref, v_ref, o_ref):
        for h in range(MEM_HEADS):
            cs = slice(h * E, (h + 1) * E)
            s = lax.dot_general(q_ref[:, cs].astype(MXU_DTYPE), k_ref[:, cs].astype(MXU_DTYPE),
                                (((1,), (1,)), ((), ())), preferred_element_type=F32) * scale
            e = jnp.exp(s - jnp.max(s, axis=1, keepdims=True))
            pr = e / jnp.sum(e, axis=1, keepdims=True)
            o_ref[:, cs] = jnp.dot(pr.astype(MXU_DTYPE), v_ref[:, cs].astype(MXU_DTYPE),
                                   preferred_element_type=F32).astype(o_ref.dtype)

    return _pcall(
        body, out_shape=_sds((T, D), MXU_DTYPE), grid=(B, nq),
        in_specs=[pl.BlockSpec((tq, D), lambda b, i: (b * nq + i, 0)), pl.BlockSpec((NM, D), lambda b, i: (b, 0)),
                  pl.BlockSpec((NM, D), lambda b, i: (b, 1))],
        out_specs=pl.BlockSpec((tq, D), lambda b, i: (b * nq + i, 0)), compiler_params=_params(), name=name)(qm, km, kv)


def _memattn_bwd(qm, km, kv, do, *, B, S, NM, name):
    T, D = qm.shape
    E = D // MEM_HEADS
    tq = _tile(S, 512, 8)
    nq = S // tq
    scale = 1.0 / math.sqrt(E)

    def body(q_ref, k_ref, v_ref, do_ref, dq_ref, dk_ref, dv_ref):
        i = pl.program_id(1)

        @pl.when(i == 0)
        def _():
            dk_ref[...] = jnp.zeros_like(dk_ref)
            dv_ref[...] = jnp.zeros_like(dv_ref)

        for h in range(MEM_HEADS):
            cs = slice(h * E, (h + 1) * E)
            qh = q_ref[:, cs].astype(MXU_DTYPE)
            kh = k_ref[:, cs].astype(MXU_DTYPE)
            vh = v_ref[:, cs].astype(MXU_DTYPE)
            doh = do_ref[:, cs].astype(MXU_DTYPE)
            s = lax.dot_general(qh, kh, (((1,), (1,)), ((), ())), preferred_element_type=F32) * scale
            e = jnp.exp(s - jnp.max(s, axis=1, keepdims=True))
            pr = e / jnp.sum(e, axis=1, keepdims=True)
            dp = lax.dot_general(doh, vh, (((1,), (1,)), ((), ())), preferred_element_type=F32)
            ds = (pr * (dp - jnp.sum(dp * pr, axis=1, keepdims=True))).astype(MXU_DTYPE)
            dq_ref[:, cs] = jnp.dot(ds, kh, preferred_element_type=F32) * scale
            dk_ref[:, cs] += lax.dot_general(ds, qh, (((0,), (0,)), ((), ())), preferred_element_type=F32) * scale
            dv_ref[:, cs] += lax.dot_general(pr.astype(MXU_DTYPE), doh, (((0,), (0,)), ((), ())),
                                             preferred_element_type=F32)

    qblk = pl.BlockSpec((tq, D), lambda b, i: (b * nq + i, 0))
    mblk = pl.BlockSpec((NM, D), lambda b, i: (b, 0))
    return _pcall(
        body, out_shape=(_sds((T, D), F32), _sds((B * NM, D), F32), _sds((B * NM, D), F32)), grid=(B, nq),
        in_specs=[qblk, mblk, pl.BlockSpec((NM, D), lambda b, i: (b, 1)), qblk],
        out_specs=(qblk, mblk, mblk), compiler_params=_params(), name=name)(qm, km, kv, do)


def _loss_head(y, tgt):
    T, D = y.shape
    tm = _tile(T, 512, 8)

    def body(y_ref, t_ref, dy_ref, l_ref):
        i = pl.program_id(0)
        err = y_ref[...] - t_ref[...]
        dy_ref[...] = err * (1.0 / D)
        part = jnp.sum(jnp.sum(err * err, axis=1, keepdims=True), axis=0, keepdims=True) * (0.5 / D)

        @pl.when(i == 0)
        def _():
            l_ref[...] = part

        @pl.when(i > 0)
        def _():
            l_ref[...] += part

    blk = pl.BlockSpec((tm, D), lambda i: (i, 0))
    return _pcall(body, out_shape=(_sds((T, D), F32), _sds((1, 1), F32)), grid=(T // tm,), in_specs=[blk, blk],
                  out_specs=(blk, pl.BlockSpec((1, 1), lambda i: (0, 0))), compiler_params=_params(),
                  name="loss_head")(y, tgt)


def _mesh_pos():
    return lax.axis_index("x"), lax.axis_index("y"), lax.axis_index("c")


def _flip(v, bit):
    return 1 - v if bit else v


def _all_gather(xs):
    R, C = xs.shape
    MESH = pl.DeviceIdType.MESH

    def body(x_ref, out_ref, send_sems, recv_sems, local_sem):
        x, y, c = _mesh_pos()
        me, sibling = (x, y, c), (x, y, 1 - c)
        chips = [(1 - x, y), (x, 1 - y), (1 - x, 1 - y)]

        def rows(px, py, pc):
            return out_ref.at[4 * px + 2 * py + pc]

        def copy(k, block, to, src=None):
            return pltpu.make_async_remote_copy(
                src_ref=rows(*block) if src is None else src, dst_ref=rows(*block), send_sem=send_sems.at[k],
                recv_sem=recv_sems.at[k], device_id=to, device_id_type=MESH)

        mine = pltpu.make_async_copy(x_ref, rows(*me), local_sem)
        mine.start()
        first = [copy(0, me, sibling, src=x_ref)]
        first += [copy(1 + j, me, (*chip, c), src=x_ref) for j, chip in enumerate(chips)]
        for cp in first:
            cp.start()
        passed = [copy(4 + j, (*chip, c), sibling) for j, chip in enumerate(chips)]
        for j, chip in enumerate(chips):
            copy(1 + j, (*chip, c), me).wait_recv()
            passed[j].start()
        copy(0, sibling, me).wait_recv()
        for j, chip in enumerate(chips):
            copy(4 + j, (*chip, 1 - c), me).wait_recv()
        for cp in first + passed:
            cp.wait_send()
        mine.wait()

    return _pcall(body, out_shape=_sds((N_DEV, R, C), xs.dtype), in_specs=[pl.BlockSpec(memory_space=pl.ANY)],
                  out_specs=pl.BlockSpec(memory_space=pl.ANY),
                  scratch_shapes=[pltpu.SemaphoreType.DMA((7,)), pltpu.SemaphoreType.DMA((7,)),
                                  pltpu.SemaphoreType.DMA], name="weights_all_gather")(xs)


def _all_to_all(send):
    _, R, C = send.shape
    MESH = pl.DeviceIdType.MESH

    def body(s_ref, r_ref, send_sems, recv_sems, local_sem):
        x, y, c = _mesh_pos()
        me = 4 * x + 2 * y + c
        mine = pltpu.make_async_copy(s_ref.at[me], r_ref.at[me], local_sem)
        mine.start()
        copies = []
        for m in range(1, N_DEV):
            px, py, pc = _flip(x, m & 4), _flip(y, m & 2), _flip(c, m & 1)
            cp = pltpu.make_async_remote_copy(
                src_ref=s_ref.at[4 * px + 2 * py + pc], dst_ref=r_ref.at[me], send_sem=send_sems.at[m - 1],
                recv_sem=recv_sems.at[m - 1], device_id=(px, py, pc), device_id_type=MESH)
            cp.start()
            copies.append(cp)
        for cp in copies:
            cp.wait()
        mine.wait()

    return _pcall(body, out_shape=_sds(send.shape, send.dtype), in_specs=[pl.BlockSpec(memory_space=pl.ANY)],
                  out_specs=pl.BlockSpec(memory_space=pl.ANY),
                  scratch_shapes=[pltpu.SemaphoreType.DMA((7,)), pltpu.SemaphoreType.DMA((7,)),
                                  pltpu.SemaphoreType.DMA], name="grads_all_to_all")(send)


def _adamw(w, g, m, v):
    m = ADAM_B1 * m + (1.0 - ADAM_B1) * g
    v = ADAM_B2 * v + (1.0 - ADAM_B2) * jnp.square(g)
    m_hat = m / (1.0 - ADAM_B1 ** ADAM_STEP)
    v_hat = v / (1.0 - ADAM_B2 ** ADAM_STEP)
    delta = -ADAM_LR * (m_hat / (jnp.sqrt(v_hat) + ADAM_EPS) + ADAM_WD * w)
    return delta, m, v


def _sum_adamw(recv, w, m, v):
    _, R, C = recv.shape
    tr = _tile(R, 384, 8)

    def body(r_ref, w_ref, m_ref, v_ref, g_ref, d_ref, nm_ref, nv_ref):
        g = r_ref[0].astype(F32)
        for j in range(1, N_DEV):
            g = g + r_ref[j].astype(F32)
        g_ref[...] = g
        d_ref[...], nm_ref[...], nv_ref[...] = _adamw(w_ref[...], g, m_ref[...], v_ref[...])

    blk = pl.BlockSpec((tr, C), lambda i: (i, 0))
    return _pcall(body, out_shape=(_sds((R, C), F32),) * 4, grid=(R // tr,),
                  in_specs=[pl.BlockSpec((N_DEV, tr, C), lambda i: (0, i, 0)), blk, blk, blk],
                  out_specs=(blk,) * 4, compiler_params=_params(), name="grad_sum_adamw")(recv, w, m, v)


def _small_sum_adamw(gp, w, m, v):
    NR, C = gp.shape
    MESH = pl.DeviceIdType.MESH

    def body(gp_ref, w_ref, m_ref, v_ref, g_ref, d_ref, nm_ref, nv_ref, gath, send_sems, recv_sems):
        x, y, c = _mesh_pos()
        me = 4 * x + 2 * y + c
        gath[me] = gp_ref[...]
        copies = []
        for k in range(1, N_DEV):
            peer = (_flip(x, k & 4), _flip(y, k & 2), _flip(c, k & 1))
            cp = pltpu.make_async_remote_copy(src_ref=gp_ref, dst_ref=gath.at[me], send_sem=send_sems.at[k - 1],
                                              recv_sem=recv_sems.at[k - 1], device_id=peer, device_id_type=MESH)
            cp.start()
            copies.append(cp)
        for cp in copies:
            cp.wait()
        g = gath[0]
        for j in range(1, N_DEV):
            g = g + gath[j]
        g_ref[...] = g
        d_ref[...], nm_ref[...], nv_ref[...] = _adamw(w_ref[...], g, m_ref[...], v_ref[...])

    vm = pl.BlockSpec(memory_space=pltpu.VMEM)
    return _pcall(body, out_shape=(_sds((NR, C), F32),) * 4, in_specs=[vm] * 4, out_specs=(vm,) * 4,
                  scratch_shapes=[pltpu.VMEM((N_DEV, NR, C), F32), pltpu.SemaphoreType.DMA((7,)),
                                  pltpu.SemaphoreType.DMA((7,))], name="small_sum_adamw")(gp, w, m, v)


def _flat_big(d, prefix=""):
    return jnp.concatenate([d[prefix + n].reshape(-1, 1024) for n in BIG], axis=0)


def _unflat_big(flat, shapes):
    out, off = {}, 0
    for n in BIG:
        rows = math.prod(shapes[n]) // 1024
        out[n] = flat[off:off + rows].reshape(shapes[n])
        off += rows
    return out


def _gathered_weights(gath, shapes):
    out, off = {}, 0
    for n in BIG:
        L, r, c = shapes[n]
        rows = L * r * c // 1024
        blk = gath[:, off:off + rows].reshape(N_DEV, L, r, c)
        if BIG_SHARD_AXIS[n] == 2:
            out[n] = blk.transpose(1, 2, 0, 3).reshape(L, r, N_DEV * c)
        else:
            out[n] = blk.transpose(1, 0, 2, 3).reshape(L, N_DEV * r, c)
        off += rows
    return out


def _scatter_blocks(grads, shapes, dtype):
    parts = []
    for n in BIG:
        L, r, c = shapes[n]
        g = grads[n]
        if BIG_SHARD_AXIS[n] == 2:
            g = g.reshape(L, r, N_DEV, c).transpose(2, 0, 1, 3)
        else:
            g = g.reshape(L, N_DEV, r, c).transpose(1, 0, 2, 3)
        parts.append(g.reshape(N_DEV, -1, 1024).astype(dtype))
    return jnp.concatenate(parts, axis=1)


def _pack_small(d, prefix, loss=None):
    parts = []
    for n in SMALL:
        a = d[prefix + n].reshape(-1).astype(F32)
        pad = (-a.shape[0]) % LANES
        parts.append(jnp.pad(a, (0, pad)) if pad else a)
    tail = jnp.zeros((LANES,), F32)
    if loss is not None:
        tail = tail.at[0].set(loss)
    flat = jnp.concatenate(parts + [tail])
    pad = (-flat.shape[0]) % (8 * LANES)
    return jnp.pad(flat, (0, pad)).reshape(-1, LANES)


def _unpack_small(packed, shapes):
    flat = packed.reshape(-1)
    out, off = {}, 0
    for n in SMALL:
        size = math.prod(shapes[n])
        out[n] = flat[off:off + size].reshape(shapes[n])
        off += size + (-size) % LANES
    return out, flat[off]


def _row(a):
    return a.reshape(1, -1).astype(F32)


def _layer_fwd(x, memf, W, sp, lb, dims):
    B, S, NM = dims
    D = x.shape[1]
    h1 = _norm_fwd(x, sp["norm1_g"], E=D, W=D, out_dtype=MXU_DTYPE, name="norm1_fwd")
    z = _matmul(h1, W["w_in"], name="in_proj")
    qn = _norm_fwd(z, sp["attn_qn_g"], E=ATT_HEAD_DIM, W=ATT_W, cb=0, out_dtype=MXU_DTYPE, name="qnorm_fwd")
    kn = _norm_fwd(z, sp["attn_kn_g"], E=ATT_HEAD_DIM, W=ATT_W, cb=1, out_dtype=MXU_DTYPE, name="knorm_fwd")
    ya, lse = _attn_fwd(qn, kn, z, B=B, S=S, name="dilated_attn_fwd")
    o, states = _hgrn_fwd(z, lb, B=B, S=S, name="hgrn_fwd")
    yh = _hgout_fwd(o, z, sp["hg_onorm_g"], name="hgrn_out_fwd")
    ycat = jnp.concatenate([ya, yh], axis=1).astype(MXU_DTYPE)
    x1 = _matmul(ycat, W["w_out"], epi="add", extra=x, name="out_proj")
    h2 = _norm_fwd(x1, sp["norm2_g"], E=D, W=D, out_dtype=MXU_DTYPE, name="norm2_fwd")
    qmp = _matmul(h2, W["w_mq"], name="mq_proj")
    qm = _norm_fwd(qmp, sp["mq_norm_g"], E=D // MEM_HEADS, W=D, out_dtype=MXU_DTYPE, name="mqnorm_fwd")
    mn = _norm_fwd(memf, sp["mem_norm_g"], E=D, W=D, out_dtype=MXU_DTYPE, name="memnorm_fwd")
    kv = _matmul(mn, W["w_mkv"], name="mkv_proj")
    km = _norm_fwd(kv, sp["mk_norm_g"], E=D // MEM_HEADS, W=D, cb=0, out_dtype=MXU_DTYPE, name="mknorm_fwd")
    om = _memattn_fwd(qm, km, kv, B=B, S=S, NM=NM, name="mem_attn_fwd")
    x2 = _matmul(om, W["w_mo"], epi="add", extra=x1, name="mo_proj")
    h3 = _norm_fwd(x2, sp["norm3_g"], E=D, W=D, out_dtype=MXU_DTYPE, name="norm3_fwd")
    a = _matmul(h3, W["w_ff1"], name="ff1")
    x3 = _matmul(a, W["w_ff2"], pro="relu2", epi="add", extra=x2, name="ff2")
    saved = dict(x=x, h1=h1, z=z, qn=qn, kn=kn, ya=ya, lse=lse, o=o, states=states, ycat=ycat, x1=x1, h2=h2,
                 qmp=qmp, qm=qm, mn=mn, kv=kv, km=km, om=om, x2=x2, h3=h3, a=a)
    return x3, saved


def _layer_bwd(dx3, s, memf, W, sp, lb, dims):
    B, S, NM = dims
    D = dx3.shape[1]
    E_M = D // MEM_HEADS
    gw, gs = {}, {}
    da = _matmul(dx3, W["w_ff2"], tb=True, epi="relu2grad", extra=s["a"], out_dtype=MXU_DTYPE, name="ff2_dgrad")
    gw["w_ff2"] = _matmul(s["a"], dx3, ta=True, pro="relu2", name="ff2_wgrad")
    dh3 = _matmul(da, W["w_ff1"], tb=True, name="ff1_dgrad")
    gw["w_ff1"] = _matmul(s["h3"], da, ta=True, name="ff1_wgrad")
    dx2, gs["norm3_g"] = _norm_bwd(s["x2"], sp["norm3_g"], dh3, E=D, W=D, res=dx3, name="norm3_bwd")
    dom = _matmul(dx2, W["w_mo"], tb=True, name="mo_dgrad")
    gw["w_mo"] = _matmul(s["om"], dx2, ta=True, name="mo_wgrad")
    dqm, dkm, dvm = _memattn_bwd(s["qm"], s["km"], s["kv"], dom, B=B, S=S, NM=NM, name="mem_attn_bwd")
    dqmp, gs["mq_norm_g"] = _norm_bwd(s["qmp"], sp["mq_norm_g"], dqm, E=E_M, W=D, fold=E_M, out_dtype=MXU_DTYPE,
                                      name="mqnorm_bwd")
    dkmp, gs["mk_norm_g"] = _norm_bwd(s["kv"], sp["mk_norm_g"], dkm, E=E_M, W=D, cb=0, fold=E_M,
                                      out_dtype=MXU_DTYPE, name="mknorm_bwd")
    dkv = jnp.concatenate([dkmp, dvm.astype(MXU_DTYPE)], axis=1)
    dh2 = _matmul(dqmp, W["w_mq"], tb=True, name="mq_dgrad")
    gw["w_mq"] = _matmul(s["h2"], dqmp, ta=True, name="mq_wgrad")
    dmn = _matmul(dkv, W["w_mkv"], tb=True, name="mkv_dgrad")
    gw["w_mkv"] = _matmul(s["mn"], dkv, ta=True, name="mkv_wgrad")
    _, gs["mem_norm_g"] = _norm_bwd(memf, sp["mem_norm_g"], dmn, E=D, W=D, name="memnorm_bwd")
    dx1, gs["norm2_g"] = _norm_bwd(s["x1"], sp["norm2_g"], dh2, E=D, W=D, res=dx2, name="norm2_bwd")
    dycat = _matmul(dx1, W["w_out"], tb=True, name="out_dgrad")
    gw["w_out"] = _matmul(s["ycat"], dx1, ta=True, name="out_wgrad")
    do_hg, dgh, gs["hg_onorm_g"] = _hgout_bwd(s["o"], s["z"], sp["hg_onorm_g"], dycat, name="hgrn_out_bwd")
    dqh, dfh, dih, dlb = _hgrn_bwd(s["z"], lb, s["states"], do_hg, B=B, S=S, name="hgrn_bwd")
    dqn, dkn, dv = _attn_bwd(s["qn"], s["kn"], s["z"], s["ya"], s["lse"], dycat, B=B, S=S, name="dilated_attn_bwd")
    dq, gs["attn_qn_g"] = _norm_bwd(s["z"], sp["attn_qn_g"], dqn, E=ATT_HEAD_DIM, W=ATT_W, cb=0, fold=ATT_HEAD_DIM,
                                    out_dtype=MXU_DTYPE, name="qnorm_bwd")
    dk, gs["attn_kn_g"] = _norm_bwd(s["z"], sp["attn_kn_g"], dkn, E=ATT_HEAD_DIM, W=ATT_W, cb=1, fold=ATT_HEAD_DIM,
                                    out_dtype=MXU_DTYPE, name="knorm_bwd")
    dz = jnp.concatenate([dq, dk] + [t.astype(MXU_DTYPE) for t in (dv, dqh, dfh, dih, dgh)], axis=1)
    dh1 = _matmul(dz, W["w_in"], tb=True, name="in_dgrad")
    gw["w_in"] = _matmul(s["h1"], dz, ta=True, name="in_wgrad")
    dx0, gs["norm1_g"] = _norm_bwd(s["x"], sp["norm1_g"], dh1, E=D, W=D, res=dx1, name="norm1_bwd")
    gs["attn_qn_g"] = gs["attn_qn_g"][:, :ATT_HEAD_DIM]
    gs["attn_kn_g"] = gs["attn_kn_g"][:, :ATT_HEAD_DIM]
    return dx0, gw, gs, dlb


def kernel(x, mem, norm1_g, w_in, attn_qn_g, attn_kn_g, hg_lb, hg_onorm_g, w_out, norm2_g, mem_norm_g, w_mq, w_mkv, mq_norm_g, mk_norm_g, w_mo, norm3_g, w_ff1, w_ff2, loss_target, m_norm1_g, m_w_in, m_attn_qn_g, m_attn_kn_g, m_hg_lb, m_hg_onorm_g, m_w_out, m_norm2_g, m_mem_norm_g, m_w_mq, m_w_mkv, m_mq_norm_g, m_mk_norm_g, m_w_mo, m_norm3_g, m_w_ff1, m_w_ff2, v_norm1_g, v_w_in, v_attn_qn_g, v_attn_kn_g, v_hg_lb, v_hg_onorm_g, v_w_out, v_norm2_g, v_mem_norm_g, v_w_mq, v_w_mkv, v_mq_norm_g, v_mk_norm_g, v_w_mo, v_norm3_g, v_w_ff1, v_w_ff2):
    given = dict(norm1_g=norm1_g, w_in=w_in, attn_qn_g=attn_qn_g, attn_kn_g=attn_kn_g, hg_lb=hg_lb, hg_onorm_g=hg_onorm_g, w_out=w_out, norm2_g=norm2_g, mem_norm_g=mem_norm_g, w_mq=w_mq, w_mkv=w_mkv, mq_norm_g=mq_norm_g, mk_norm_g=mk_norm_g, w_mo=w_mo, norm3_g=norm3_g, w_ff1=w_ff1, w_ff2=w_ff2, m_norm1_g=m_norm1_g, m_w_in=m_w_in, m_attn_qn_g=m_attn_qn_g, m_attn_kn_g=m_attn_kn_g, m_hg_lb=m_hg_lb, m_hg_onorm_g=m_hg_onorm_g, m_w_out=m_w_out, m_norm2_g=m_norm2_g, m_mem_norm_g=m_mem_norm_g, m_w_mq=m_w_mq, m_w_mkv=m_w_mkv, m_mq_norm_g=m_mq_norm_g, m_mk_norm_g=m_mk_norm_g, m_w_mo=m_w_mo, m_norm3_g=m_norm3_g, m_w_ff1=m_w_ff1, m_w_ff2=m_w_ff2, v_norm1_g=v_norm1_g, v_w_in=v_w_in, v_attn_qn_g=v_attn_qn_g, v_attn_kn_g=v_attn_kn_g, v_hg_lb=v_hg_lb, v_hg_onorm_g=v_hg_onorm_g, v_w_out=v_w_out, v_norm2_g=v_norm2_g, v_mem_norm_g=v_mem_norm_g, v_w_mq=v_w_mq, v_w_mkv=v_w_mkv, v_mq_norm_g=v_mq_norm_g, v_mk_norm_g=v_mk_norm_g, v_w_mo=v_w_mo, v_norm3_g=v_norm3_g, v_w_ff1=v_w_ff1, v_w_ff2=v_w_ff2)
    B, S, D = x.shape
    NM = mem.shape[1]
    L = w_in.shape[0]
    dims = (B, S, NM)
    big_shapes = {n: given[n].shape for n in BIG}
    small_shapes = {n: given[n].shape for n in SMALL}

    w_flat = _flat_big(given)
    gathered = _gathered_weights(_all_gather(w_flat.astype(MXU_DTYPE)), big_shapes)

    lower = _lower_bounds_fwd(hg_lb)
    xf = x.reshape(B * S, D)
    memf = mem.reshape(B * NM, D)

    def layer_params(l):
        W = {n: gathered[n][l] for n in BIG}
        sp = {"norm1_g": _row(norm1_g[l]), "norm2_g": _row(norm2_g[l]), "norm3_g": _row(norm3_g[l]),
              "mem_norm_g": _row(mem_norm_g[l]), "hg_onorm_g": _row(hg_onorm_g[l]),
              "attn_qn_g": _row(jnp.tile(attn_qn_g[l], ATT_HEADS)), "attn_kn_g": _row(jnp.tile(attn_kn_g[l], ATT_HEADS)),
              "mq_norm_g": _row(jnp.tile(mq_norm_g[l], MEM_HEADS)), "mk_norm_g": _row(jnp.tile(mk_norm_g[l], MEM_HEADS))}
        return W, sp, _row(lower[l])

    saved = []
    h = xf
    for l in range(L):
        W, sp, lb = layer_params(l)
        h, s = _layer_fwd(h, memf, W, sp, lb, dims)
        saved.append(s)

    dh, loss_part = _loss_head(h, loss_target.reshape(B * S, D))

    gw_layers, gs_layers, dlb_layers = [None] * L, [None] * L, [None] * L
    for l in range(L - 1, -1, -1):
        W, sp, lb = layer_params(l)
        dh, gw_layers[l], gs_layers[l], dlb_layers[l] = _layer_bwd(dh, saved[l], memf, W, sp, lb, dims)
    grad_x = dh.reshape(B, S, D)

    gw_full = {n: jnp.stack([gw_layers[l][n] for l in range(L)]) for n in BIG}
    full_shapes = {n: gw_full[n].shape for n in BIG}
    shard_like = {n: (L, full_shapes[n][1] // (N_DEV if BIG_SHARD_AXIS[n] == 1 else 1),
                      full_shapes[n][2] // (N_DEV if BIG_SHARD_AXIS[n] == 2 else 1)) for n in BIG}
    recv = _all_to_all(_scatter_blocks(gw_full, shard_like, MXU_DTYPE))
    g_flat, d_flat, nm_flat, nv_flat = _sum_adamw(recv, w_flat, _flat_big(given, "m_"), _flat_big(given, "v_"))
    big_out = [_unflat_big(t, big_shapes) for t in (g_flat, d_flat, nm_flat, nv_flat)]

    gs = {n: jnp.stack([gs_layers[l][n].reshape(small_shapes[n][1:]) for l in range(L)]) for n in SMALL if n != "hg_lb"}
    gs["hg_lb"] = _lower_bounds_bwd(hg_lb, jnp.concatenate(dlb_layers, axis=0))
    packed = _small_sum_adamw(_pack_small(gs, "", loss_part[0, 0]), _pack_small(given, ""), _pack_small(given, "m_"),
                              _pack_small(given, "v_"))
    small_out, loss = [], None
    for t in packed:
        d, tail = _unpack_small(t, small_shapes)
        small_out.append(d)
        loss = tail if loss is None else loss

    outs = [loss, grad_x]
    for k in range(4):
        outs += [small_out[k][n] if n in small_shapes else big_out[k][n] for n in WEIGHTS]
    return tuple(outs)
```

```python
import functools
import math

import jax
import jax.numpy as jnp
from jax import lax
from jax.experimental import pallas as pl
from jax.experimental.pallas import tpu as pltpu

F32 = jnp.float32
MXU_DTYPE = jnp.bfloat16
VMEM_LIMIT = 48 * 1024 * 1024

N_DEV = 8
EPS = 1e-6
NEG = -1e30
F_MIN = 1e-12
ATT_HEADS = 8
ATT_HEAD_DIM = 64
ATT_W = ATT_HEADS * ATT_HEAD_DIM
DILATIONS = (1, 4, 16)
DIL_STEPS = 128
HG_HEADS = 4
HG_HEAD_DIM = 128
HG_W = HG_HEADS * HG_HEAD_DIM
HG_CHUNK = 32
MEM_HEADS = 4
LANES = 128
FLAT_W = 1024

ADAM_LR = 0.001
ADAM_B1 = 0.9
ADAM_B2 = 0.999
ADAM_EPS = 1e-08
ADAM_WD = 0.01
ADAM_STEP = 10

BIG = ("w_in", "w_out", "w_mq", "w_mkv", "w_mo", "w_ff1", "w_ff2")
COL_SHARDED = ("w_in", "w_mkv", "w_ff1")
SMALL = ("norm1_g", "attn_qn_g", "attn_kn_g", "hg_lb", "hg_onorm_g", "norm2_g", "mem_norm_g",
         "mq_norm_g", "mk_norm_g", "norm3_g")
WEIGHTS = ("norm1_g", "w_in", "attn_qn_g", "attn_kn_g", "hg_lb", "hg_onorm_g", "w_out", "norm2_g",
           "mem_norm_g", "w_mq", "w_mkv", "mq_norm_g", "mk_norm_g", "w_mo", "norm3_g", "w_ff1", "w_ff2")
GATHER_IN = ("w_in",)
GATHER_MIX = ("w_out", "w_mq", "w_mkv", "w_mo")
GATHER_FF = ("w_ff1", "w_ff2")
SCATTER_A = ("w_ff2",)
SCATTER_B = ("w_ff1", "w_mo", "w_mq", "w_mkv", "w_out")
SCATTER_C = ("w_in",)
SCATTERS = (SCATTER_A, SCATTER_B, SCATTER_C)


def _pcall(body, **kw):
    return pl.pallas_call(body, **kw)


def _params():
    return pltpu.CompilerParams(vmem_limit_bytes=VMEM_LIMIT)


def _tile(n, pref, mult):
    t = (min(n, pref) // mult) * mult
    while t >= mult:
        if n % t == 0:
            return t
        t -= mult
    return n


def _sds(shape, dtype):
    return jax.ShapeDtypeStruct(shape, dtype)


def _mesh_pos():
    return lax.axis_index("x"), lax.axis_index("y"), lax.axis_index("c")


def _flip(v, bit):
    return 1 - v if bit else v


def _gather_hooks(x_ref, out_ref, send_sems, recv_sems, local_sem):
    MESH = pl.DeviceIdType.MESH
    x, y, c = _mesh_pos()
    me, sibling = (x, y, c), (x, y, 1 - c)
    chips = [(1 - x, y), (x, 1 - y), (1 - x, 1 - y)]

    def rows(px, py, pc):
        return out_ref.at[4 * px + 2 * py + pc]

    def copy(k, block, to, src=None):
        return pltpu.make_async_remote_copy(
            src_ref=rows(*block) if src is None else src, dst_ref=rows(*block), send_sem=send_sems.at[k],
            recv_sem=recv_sems.at[k], device_id=to, device_id_type=MESH)

    mine = pltpu.make_async_copy(x_ref, rows(*me), local_sem)
    first = [copy(0, me, sibling, src=x_ref)]
    first += [copy(1 + j, me, (*chip, c), src=x_ref) for j, chip in enumerate(chips)]
    passed = [copy(4 + j, (*chip, c), sibling) for j, chip in enumerate(chips)]

    def start():
        mine.start()
        for cp in first:
            cp.start()

    def forward():
        for j, chip in enumerate(chips):
            copy(1 + j, (*chip, c), me).wait_recv()
            passed[j].start()

    def finish():
        copy(0, sibling, me).wait_recv()
        for j, chip in enumerate(chips):
            copy(4 + j, (*chip, 1 - c), me).wait_recv()
        for cp in first + passed:
            cp.wait_send()
        mine.wait()

    return start, forward, finish


def _scatter_hooks(s_ref, r_ref, send_sems, recv_sems, local_sem):
    MESH = pl.DeviceIdType.MESH
    x, y, c = _mesh_pos()
    me = 4 * x + 2 * y + c
    mine = pltpu.make_async_copy(s_ref.at[me], r_ref.at[me], local_sem)
    copies = []
    for m in range(1, N_DEV):
        px, py, pc = _flip(x, m & 4), _flip(y, m & 2), _flip(c, m & 1)
        copies.append(pltpu.make_async_remote_copy(
            src_ref=s_ref.at[4 * px + 2 * py + pc], dst_ref=r_ref.at[me], send_sem=send_sems.at[m - 1],
            recv_sem=recv_sems.at[m - 1], device_id=(px, py, pc), device_id_type=MESH))

    def start():
        mine.start()
        for cp in copies:
            cp.start()

    def finish():
        for cp in copies:
            cp.wait()
        mine.wait()

    return start, None, finish


_EXCHANGE_SEMS = [pltpu.SemaphoreType.DMA((N_DEV - 1,)), pltpu.SemaphoreType.DMA((N_DEV - 1,)),
                  pltpu.SemaphoreType.DMA]


def _exchange_out(kind, xs):
    return _sds((N_DEV,) + xs.shape, xs.dtype) if kind == "gather" else _sds(xs.shape, xs.dtype)


def _run(body, args, *, out_shape, grid, in_specs, out_specs, scratch_shapes=(), name, carry=None):
    single = not isinstance(out_shape, (tuple, list))
    outs = (out_shape,) if single else tuple(out_shape)
    ospecs = (out_specs,) if single else tuple(out_specs)
    if carry is None:
        res = _pcall(body, out_shape=outs, grid=grid, in_specs=list(in_specs), out_specs=ospecs,
                     scratch_shapes=list(scratch_shapes), compiler_params=_params(), name=name)(*args)
        return res[0] if single else tuple(res)
    kind, xs = carry
    n_in, n_out, n_scr = len(args), len(outs), len(scratch_shapes)
    total = math.prod(grid)

    def wrapped(*refs):
        ins, x_ref = refs[:n_in], refs[n_in]
        o, xo_ref = refs[n_in + 1:n_in + 1 + n_out], refs[n_in + 1 + n_out]
        scr = refs[n_in + 2 + n_out:n_in + 2 + n_out + n_scr]
        sems = refs[n_in + 2 + n_out + n_scr:]
        step = pl.program_id(0)
        for ax in range(1, len(grid)):
            step = step * grid[ax] + pl.program_id(ax)
        start, forward, finish = (_gather_hooks if kind == "gather" else _scatter_hooks)(x_ref, xo_ref, *sems)
        pl.when(step == 0)(start)
        body(*ins, *o, *scr)
        if forward is not None:
            pl.when(step == (3 * total) // 4)(forward)
        pl.when(step == total - 1)(finish)

    hbm = pl.BlockSpec(memory_space=pl.ANY)
    res = _pcall(wrapped, out_shape=outs + (_exchange_out(kind, xs),), grid=grid, in_specs=list(in_specs) + [hbm],
                 out_specs=ospecs + (hbm,), scratch_shapes=list(scratch_shapes) + _EXCHANGE_SEMS,
                 compiler_params=_params(), name=name)(*args, xs)
    main = tuple(res[:-1])
    return (main[0] if single else main), res[-1]


def _all_gather(xs):
    def body(x_ref, out_ref, send_sems, recv_sems, local_sem):
        start, forward, finish = _gather_hooks(x_ref, out_ref, send_sems, recv_sems, local_sem)
        start()
        forward()
        finish()

    hbm = pl.BlockSpec(memory_space=pl.ANY)
    return _pcall(body, out_shape=_exchange_out("gather", xs), in_specs=[hbm], out_specs=hbm,
                  scratch_shapes=_EXCHANGE_SEMS, name="weights_all_gather")(xs)


def _matmul(a, b, *, ta=False, tb=False, pro=None, epi=None, extra=None, out_dtype=F32, carry=None, name):
    M, K = (a.shape[1], a.shape[0]) if ta else a.shape
    N = b.shape[0] if tb else b.shape[1]
    tm = _tile(M, 512, 8 if not ta else LANES)
    tn = _tile(N, 1024, LANES)
    tk = _tile(K, 512 if ta else 1024, LANES if not ta else 8)
    nk = K // tk
    dims = (((0 if ta else 1,), (1 if tb else 0,)), ((), ()))

    def body(a_ref, b_ref, *rest):
        if extra is not None:
            e_ref, o_ref, acc_ref = rest
        else:
            o_ref, acc_ref = rest
        k = pl.program_id(2)

        @pl.when(k == 0)
        def _():
            acc_ref[...] = jnp.zeros_like(acc_ref)

        av = a_ref[...]
        if pro == "relu2":
            av = jnp.square(jnp.maximum(av, 0.0))
        acc_ref[...] += lax.dot_general(av.astype(MXU_DTYPE), b_ref[...].astype(MXU_DTYPE), dims,
                                        preferred_element_type=F32)

        @pl.when(k == nk - 1)
        def _():
            r = acc_ref[...]
            if epi == "add":
                r = r + e_ref[...]
            elif epi == "relu2grad":
                r = r * (2.0 * jnp.maximum(e_ref[...], 0.0))
            o_ref[...] = r.astype(out_dtype)

    a_spec = pl.BlockSpec((tk, tm), lambda i, j, k: (k, i)) if ta else pl.BlockSpec((tm, tk), lambda i, j, k: (i, k))
    b_spec = pl.BlockSpec((tn, tk), lambda i, j, k: (j, k)) if tb else pl.BlockSpec((tk, tn), lambda i, j, k: (k, j))
    o_spec = pl.BlockSpec((tm, tn), lambda i, j, k: (i, j))
    in_specs = [a_spec, b_spec] + ([o_spec] if extra is not None else [])
    args = (a, b) + ((extra,) if extra is not None else ())
    return _run(body, args, out_shape=_sds((M, N), out_dtype), grid=(M // tm, N // tn, nk), in_specs=in_specs,
                out_specs=o_spec, scratch_shapes=[pltpu.VMEM((tm, tn), F32)], name=name, carry=carry)


def _group_mean(v, E):
    rows, W = v.shape
    if E == W:
        return jnp.mean(v, axis=-1, keepdims=True)
    pieces = []
    if E % LANES == 0:
        for g0 in range(0, W, E):
            m = jnp.mean(v[:, g0:g0 + E], axis=-1, keepdims=True)
            pieces.append(jnp.broadcast_to(m, (rows, E)))
    else:
        lane = lax.broadcasted_iota(jnp.int32, (rows, LANES), 1)
        for c0 in range(0, W, LANES):
            vc = v[:, c0:c0 + LANES]
            acc = jnp.zeros((rows, LANES), F32)
            for s0 in range(0, LANES, E):
                msk = (lane >= s0) & (lane < s0 + E)
                m = jnp.sum(jnp.where(msk, vc, 0.0), axis=-1, keepdims=True) * (1.0 / E)
                acc = jnp.where(msk, m, acc)
            pieces.append(acc)
    return jnp.concatenate(pieces, axis=-1)


def _fold_groups(t, E):
    W = t.shape[1]
    step = max(E, LANES)
    acc = t[:, 0:step]
    for c0 in range(step, W, step):
        acc = acc + t[:, c0:c0 + step]
    sh = LANES // 2
    while sh >= E:
        acc = acc + pltpu.roll(acc, sh, 1)
        sh //= 2
    return acc


def _norm_fwd(x, g, *, E, W, cb=0, out_dtype, name):
    M = x.shape[0]
    tm = _tile(M, 512, 8)

    def body(x_ref, g_ref, o_ref):
        xv = x_ref[...]
        r = lax.rsqrt(_group_mean(xv * xv, E) + EPS)
        o_ref[...] = (xv * r * g_ref[...]).astype(out_dtype)

    return _run(body, (x, g), out_shape=_sds((M, W), out_dtype), grid=(M // tm,),
                in_specs=[pl.BlockSpec((tm, W), lambda i: (i, cb)), pl.BlockSpec((1, W), lambda i: (0, 0))],
                out_specs=pl.BlockSpec((tm, W), lambda i: (i, 0)), name=name)


def _norm_bwd(x, g, dy, *, E, W, cb=0, res=None, fold=None, out_dtype=F32, name):
    M = x.shape[0]
    tm = _tile(M, 512, 8)
    n = M // tm
    gw = W if fold is None else max(fold, LANES)

    def body(x_ref, g_ref, dy_ref, *rest):
        if res is not None:
            r_ref, dx_ref, dg_ref, acc_ref = rest
        else:
            dx_ref, dg_ref, acc_ref = rest
        i = pl.program_id(0)
        xv = x_ref[...]
        r = lax.rsqrt(_group_mean(xv * xv, E) + EPS)
        xh = xv * r
        dyv = dy_ref[...].astype(F32)
        dyg = dyv * g_ref[...]
        dx = r * (dyg - xh * _group_mean(dyg * xh, E))
        if res is not None:
            dx = dx + r_ref[...]
        dx_ref[...] = dx.astype(out_dtype)
        part = jnp.sum(dyv * xh, axis=0, keepdims=True)

        @pl.when(i == 0)
        def _():
            acc_ref[...] = part

        @pl.when(i > 0)
        def _():
            acc_ref[...] += part

        @pl.when(i == n - 1)
        def _():
            t = acc_ref[...]
            dg_ref[...] = t if fold is None else _fold_groups(t, fold)

    blk = pl.BlockSpec((tm, W), lambda i: (i, 0))
    in_specs = [pl.BlockSpec((tm, W), lambda i: (i, cb)), pl.BlockSpec((1, W), lambda i: (0, 0)), blk]
    args = [x, g, dy]
    if res is not None:
        in_specs.append(blk)
        args.append(res)
    return _run(body, tuple(args), out_shape=(_sds((M, W), out_dtype), _sds((1, gw), F32)), grid=(n,),
                in_specs=in_specs, out_specs=(blk, pl.BlockSpec((1, gw), lambda i: (0, 0))),
                scratch_shapes=[pltpu.VMEM((1, W), F32)], name=name)


def _attn_tables(S, tq):
    nb = S // tq

    def body(o_ref):
        n = pl.program_id(0)
        rc = lax.broadcasted_iota(jnp.int32, (tq, tq), 0) - lax.broadcasted_iota(jnp.int32, (tq, tq), 1)
        d = rc + n * tq
        cnt = jnp.zeros((tq, tq), jnp.int32)
        for dil in DILATIONS:
            hit = (d <= DIL_STEPS * dil) if dil == 1 else (((d & (dil - 1)) == 0) & (d <= DIL_STEPS * dil))
            cnt = cnt + hit.astype(jnp.int32)
        ok = (d >= 0) & (cnt > 0)
        logm = jnp.where(cnt == 3, math.log(3.0), jnp.where(cnt == 2, math.log(2.0), 0.0))
        o_ref[0] = jnp.where(n == nb, rc.astype(F32), jnp.where(ok, logm, NEG).astype(F32))

    return _run(body, (), out_shape=_sds((nb + 1, tq, tq), F32), grid=(nb + 1,), in_specs=[],
                out_specs=pl.BlockSpec((1, tq, tq), lambda n: (n, 0, 0)), name="dilated_attn_tables")


def _alibi_slope(h):
    return 2.0 ** (-8.0 / ATT_HEADS * (h + 1))


def _attn_fwd(tab, qn, kn, z, *, B, S, carry=None, name):
    T = B * S
    tq = tab.shape[1]
    nq = S // tq
    scale = 1.0 / math.sqrt(ATT_HEAD_DIM)
    dn_t = (((1,), (1,)), ((), ()))

    def body(tab_ref, q_ref, k_ref, v_ref, y_ref, lse_ref):
        i = pl.program_id(1)
        rc = tab_ref[nq]
        lane = lax.broadcasted_iota(jnp.int32, (1, LANES), 1)
        lo = lane < ATT_HEAD_DIM
        lse_blk = jnp.zeros((tq, LANES), F32)
        for p in range(ATT_HEADS // 2):
            cs = slice(p * LANES, (p + 1) * LANES)
            q2 = q_ref[:, cs].astype(MXU_DTYPE) * scale
            qms = (jnp.where(lo, q2, jnp.zeros_like(q2)), jnp.where(lo, jnp.zeros_like(q2), q2))

            def chunk(jj, carried):
                off = pl.multiple_of((i - jj) * tq, tq)
                k2 = k_ref[pl.ds(off, tq), cs].astype(MXU_DTYPE)
                v2 = v_ref[pl.ds(off, tq), cs].astype(MXU_DTYPE)
                base = tab_ref[jj]
                dist = rc + (jj * tq).astype(F32)
                out = []
                for sub in range(2):
                    m, l, acc = carried[3 * sub:3 * sub + 3]
                    s = lax.dot_general(qms[sub], k2, dn_t, preferred_element_type=F32)
                    s = s + (base - _alibi_slope(2 * p + sub) * dist)
                    mn = jnp.maximum(m, jnp.max(s, axis=1, keepdims=True))
                    al = jnp.exp(m - mn)
                    e = jnp.exp(s - mn)
                    out += [mn, l * al + jnp.sum(e, axis=1, keepdims=True),
                            acc * al + jnp.dot(e.astype(MXU_DTYPE), v2, preferred_element_type=F32)]
                return tuple(out)

            init = (jnp.full((tq, 1), NEG, F32), jnp.zeros((tq, 1), F32), jnp.zeros((tq, LANES), F32)) * 2
            m0, l0, a0, m1, l1, a1 = lax.fori_loop(0, i + 1, chunk, init)
            y_ref[:, cs] = jnp.where(lo, a0 / l0, a1 / l1)
            lse_blk = jnp.where(lane == 2 * p, m0 + jnp.log(l0), lse_blk)
            lse_blk = jnp.where(lane == 2 * p + 1, m1 + jnp.log(l1), lse_blk)
        lse_ref[...] = lse_blk

    return _run(
        body, (tab, qn, kn, z), out_shape=(_sds((T, ATT_W), F32), _sds((T, LANES), F32)), grid=(B, nq),
        in_specs=[pl.BlockSpec((nq + 1, tq, tq), lambda b, i: (0, 0, 0)),
                  pl.BlockSpec((tq, ATT_W), lambda b, i: (b * nq + i, 0)),
                  pl.BlockSpec((S, ATT_W), lambda b, i: (b, 0)),
                  pl.BlockSpec((S, ATT_W), lambda b, i: (b, 2))],
        out_specs=(pl.BlockSpec((tq, ATT_W), lambda b, i: (b * nq + i, 0)),
                   pl.BlockSpec((tq, LANES), lambda b, i: (b * nq + i, 0))),
        name=name, carry=carry)


def _attn_bwd(tab, qn, kn, z, y, lse, dycat, *, B, S, carry=None, name):
    T = B * S
    tq = tab.shape[1]
    nq = S // tq
    scale = 1.0 / math.sqrt(ATT_HEAD_DIM)
    dn_t = (((1,), (1,)), ((), ()))
    dn_o = (((0,), (0,)), ((), ()))

    def body(tab_ref, q_ref, k_ref, v_ref, y_ref, lse_ref, dy_ref, dq_ref, dk_ref, dv_ref):
        i = pl.program_id(1)

        @pl.when(i == 0)
        def _():
            dk_ref[...] = jnp.zeros_like(dk_ref)
            dv_ref[...] = jnp.zeros_like(dv_ref)

        rc = tab_ref[nq]
        lane = lax.broadcasted_iota(jnp.int32, (1, LANES), 1)
        lo = lane < ATT_HEAD_DIM
        lse_blk = lse_ref[...]
        for p in range(ATT_HEADS // 2):
            cs = slice(p * LANES, (p + 1) * LANES)
            q2 = q_ref[:, cs].astype(MXU_DTYPE)
            q2s = q2 * scale
            qms = (jnp.where(lo, q2s, jnp.zeros_like(q2)), jnp.where(lo, jnp.zeros_like(q2), q2s))
            do2 = dy_ref[:, cs]
            doy = do2 * y_ref[:, cs]
            do2m = do2.astype(MXU_DTYPE)
            doms = (jnp.where(lo, do2m, jnp.zeros_like(do2m)), jnp.where(lo, jnp.zeros_like(do2m), do2m))
            lses = [jnp.sum(jnp.where(lane == 2 * p + sub, lse_blk, 0.0), axis=1, keepdims=True) for sub in range(2)]
            dsums = [jnp.sum(jnp.where(lo, doy, 0.0), axis=1, keepdims=True),
                     jnp.sum(jnp.where(lo, 0.0, doy), axis=1, keepdims=True)]

            def chunk(jj, carried):
                off = pl.multiple_of((i - jj) * tq, tq)
                k2 = k_ref[pl.ds(off, tq), cs].astype(MXU_DTYPE)
                v2 = v_ref[pl.ds(off, tq), cs].astype(MXU_DTYPE)
                base = tab_ref[jj]
                dist = rc + (jj * tq).astype(F32)
                dqs, dks, dvs = [], [], []
                for sub in range(2):
                    s = lax.dot_general(qms[sub], k2, dn_t, preferred_element_type=F32)
                    pr = jnp.exp(s + (base - _alibi_slope(2 * p + sub) * dist) - lses[sub])
                    dp = lax.dot_general(doms[sub], v2, dn_t, preferred_element_type=F32)
                    ds = (pr * (dp - dsums[sub])).astype(MXU_DTYPE)
                    dqs.append(carried[sub] + jnp.dot(ds, k2, preferred_element_type=F32))
                    dks.append(lax.dot_general(ds, q2, dn_o, preferred_element_type=F32))
                    dvs.append(lax.dot_general(pr.astype(MXU_DTYPE), do2m, dn_o, preferred_element_type=F32))
                dk_ref[pl.ds(off, tq), cs] += jnp.where(lo, dks[0], dks[1]) * scale
                dv_ref[pl.ds(off, tq), cs] += jnp.where(lo, dvs[0], dvs[1])
                return tuple(dqs)

            z0 = jnp.zeros((tq, LANES), F32)
            dq0, dq1 = lax.fori_loop(0, i + 1, chunk, (z0, z0))
            dq_ref[:, cs] = jnp.where(lo, dq0, dq1) * scale

    qblk = pl.BlockSpec((tq, ATT_W), lambda b, i: (b * nq + i, 0))
    sblk = pl.BlockSpec((S, ATT_W), lambda b, i: (b, 0))
    return _run(
        body, (tab, qn, kn, z, y, lse, dycat), out_shape=(_sds((T, ATT_W), F32),) * 3, grid=(B, nq),
        in_specs=[pl.BlockSpec((nq + 1, tq, tq), lambda b, i: (0, 0, 0)), qblk, sblk,
                  pl.BlockSpec((S, ATT_W), lambda b, i: (b, 2)), qblk,
                  pl.BlockSpec((tq, LANES), lambda b, i: (b * nq + i, 0)), qblk],
        out_specs=(qblk, sblk, sblk), name=name, carry=carry)


def _sigmoid_pair(x):
    en = jnp.exp(-jnp.abs(x))
    big = 1.0 / (1.0 + en)
    small = en * big
    pos = x >= 0
    return jnp.where(pos, big, small), jnp.where(pos, small, big)


def _chunk_scan(x, pos, reverse):
    n = x.shape[0]
    sh = 1
    while sh < HG_CHUNK:
        if reverse:
            x = x + jnp.where(pos < HG_CHUNK - sh, pltpu.roll(x, n - sh, 0), 0.0)
        else:
            x = x + jnp.where(pos >= sh, pltpu.roll(x, sh, 0), 0.0)
        sh *= 2
    return x


def _hgrn_gates(qh, fh, lb):
    sq, _ = _sigmoid_pair(qh)
    sg, sgn = _sigmoid_pair(fh)
    f = lb + (1.0 - lb) * sg
    g = jnp.log(jnp.maximum(f, F_MIN))
    kk = (1.0 - lb) * sgn
    return qh * sq, sq, sg, sgn, f, g, kk


def _hgrn_fwd(z, lb, *, B, S, name):
    T = B * S
    tb = _tile(S, 256, HG_CHUNK)
    nc = tb // HG_CHUNK
    nb = S // tb
    dn_t = (((1,), (1,)), ((), ()))
    dn_o = (((0,), (0,)), ((), ()))

    def zcol(first):
        return lambda h, b, i: (b * nb + i, first + h)

    def body(q_ref, f_ref, v_ref, lb_ref, o_ref, st_ref, state):
        i = pl.program_id(2)

        @pl.when(i == 0)
        def _():
            state[...] = jnp.zeros_like(state)

        q, _, _, _, _, g, kk = _hgrn_gates(q_ref[...], f_ref[...], lb_ref[...])
        v = v_ref[...]
        pos = lax.broadcasted_iota(jnp.int32, (tb, LANES), 0) % HG_CHUNK
        b = _chunk_scan(g, pos, False)
        o = jnp.sum(q * kk, axis=1, keepdims=True) * v
        for d in range(1, HG_CHUNK):
            qd = pltpu.roll(q, tb - d, 0)
            bd = pltpu.roll(b, tb - d, 0)
            e = jnp.where(pos < HG_CHUNK - d, jnp.exp(jnp.minimum(bd - b, 0.0)), 0.0)
            a = jnp.sum(qd * kk * e, axis=1, keepdims=True)
            o = o + pltpu.roll(a * v, d, 0)
        qe = q * jnp.exp(b)
        st = state[...]
        for c in range(nc):
            rs = slice(c * HG_CHUNK, (c + 1) * HG_CHUNK)
            st_ref[c * LANES:(c + 1) * LANES, :] = st
            b_last = jnp.sum(jnp.where(pos[rs] == HG_CHUNK - 1, b[rs], 0.0), axis=0, keepdims=True)
            ke = kk[rs] * jnp.exp(b_last - b[rs])
            o_ref[rs, :] = o[rs] + lax.dot_general(qe[rs].astype(MXU_DTYPE), st.astype(MXU_DTYPE), dn_t,
                                                   preferred_element_type=F32)
            st = st * jnp.exp(b_last) + lax.dot_general(v[rs].astype(MXU_DTYPE), ke.astype(MXU_DTYPE), dn_o,
                                                        preferred_element_type=F32)
        state[...] = st

    blk = (tb, LANES)
    return _run(
        body, (z, z, z, lb), out_shape=(_sds((T, HG_W), F32), _sds((T // HG_CHUNK * LANES, HG_W), F32)),
        grid=(HG_HEADS, B, nb),
        in_specs=[pl.BlockSpec(blk, zcol(12)), pl.BlockSpec(blk, zcol(16)), pl.BlockSpec(blk, zcol(20)),
                  pl.BlockSpec((1, LANES), lambda h, b, i: (0, h))],
        out_specs=(pl.BlockSpec(blk, lambda h, b, i: (b * nb + i, h)),
                   pl.BlockSpec((nc * LANES, LANES), lambda h, b, i: (b * nb + i, h))),
        scratch_shapes=[pltpu.VMEM((LANES, LANES), F32)], name=name)


def _hgrn_bwd(z, lb, states, do, *, B, S, name):
    T = B * S
    tb = _tile(S, 256, HG_CHUNK)
    nc = tb // HG_CHUNK
    nb = S // tb
    dn_t = (((1,), (1,)), ((), ()))
    dn_o = (((0,), (0,)), ((), ()))

    def body(q_ref, f_ref, v_ref, lb_ref, st_ref, do_ref, dq_ref, df_ref, di_ref, dlb_ref, dstate):
        bi = pl.program_id(1)
        i = pl.program_id(2)

        @pl.when(i == 0)
        def _():
            dstate[...] = jnp.zeros_like(dstate)

        qh = q_ref[...]
        lbv = lb_ref[...]
        q, sq, sg, sgn, f, g, kk = _hgrn_gates(qh, f_ref[...], lbv)
        v = v_ref[...]
        dov = do_ref[...]
        pos = lax.broadcasted_iota(jnp.int32, (tb, LANES), 0) % HG_CHUNK
        b = _chunk_scan(g, pos, False)

        a0 = jnp.sum(q * kk, axis=1, keepdims=True)
        da0 = jnp.sum(dov * v, axis=1, keepdims=True)
        dv = a0 * dov
        dkk = da0 * q
        dq = da0 * kk
        db = jnp.zeros((tb, LANES), F32)
        for d in range(1, HG_CHUNK):
            qd = pltpu.roll(q, tb - d, 0)
            bd = pltpu.roll(b, tb - d, 0)
            dod = pltpu.roll(dov, tb - d, 0)
            e = jnp.where(pos < HG_CHUNK - d, jnp.exp(jnp.minimum(bd - b, 0.0)), 0.0)
            a = jnp.sum(qd * kk * e, axis=1, keepdims=True)
            da = jnp.sum(dod * v, axis=1, keepdims=True)
            dv = dv + a * dod
            xe = da * e
            dkk = dkk + xe * qd
            t1 = xe * kk
            wt = t1 * qd
            dq = dq + pltpu.roll(t1, d, 0)
            db = db + pltpu.roll(wt, d, 0) - wt

        eb = jnp.exp(b)
        qe = q * eb
        dst = dstate[...]
        dq_c, dkk_c, dv_c, db_c = [None] * nc, [None] * nc, [None] * nc, [None] * nc
        for c in range(nc - 1, -1, -1):
            rs = slice(c * HG_CHUNK, (c + 1) * HG_CHUNK)
            st0 = st_ref[c * LANES:(c + 1) * LANES, :]
            b_last = jnp.sum(jnp.where(pos[rs] == HG_CHUNK - 1, b[rs], 0.0), axis=0, keepdims=True)
            eb_last = jnp.exp(b_last)
            er = jnp.exp(b_last - b[rs])
            ke = kk[rs] * er
            do_c = dov[rs].astype(MXU_DTYPE)
            dstm = dst.astype(MXU_DTYPE)
            dqe = jnp.dot(do_c, st0.astype(MXU_DTYPE), preferred_element_type=F32)
            dv_c[c] = dv[rs] + lax.dot_general(ke.astype(MXU_DTYPE), dstm, dn_t, preferred_element_type=F32)
            dke = jnp.dot(v[rs].astype(MXU_DTYPE), dstm, preferred_element_type=F32)
            d_eb_last = jnp.sum(dst * st0, axis=0, keepdims=True)
            dst = dst * eb_last + lax.dot_general(do_c, qe[rs].astype(MXU_DTYPE), dn_o, preferred_element_type=F32)
            dq_c[c] = dq[rs] + dqe * eb[rs]
            dkk_c[c] = dkk[rs] + dke * er
            dkeke = dke * ke
            db_last = jnp.sum(dkeke, axis=0, keepdims=True) + d_eb_last * eb_last
            db_c[c] = db[rs] + dqe * qe[rs] - dkeke + jnp.where(pos[rs] == HG_CHUNK - 1, db_last, 0.0)
        dstate[...] = dst
        dq = jnp.concatenate(dq_c, axis=0)
        dkk = jnp.concatenate(dkk_c, axis=0)
        dv = jnp.concatenate(dv_c, axis=0)
        dg = _chunk_scan(jnp.concatenate(db_c, axis=0), pos, True)

        dfv = dg * jnp.where(f > F_MIN, 1.0 / f, 0.0)
        dsg = (dfv - dkk) * (1.0 - lbv)
        dq_ref[...] = dq * (sq * (1.0 + qh * (1.0 - sq)))
        df_ref[...] = dsg * sg * sgn
        di_ref[...] = dv
        part = jnp.sum(dfv * sgn - dkk * sgn, axis=0, keepdims=True)

        @pl.when((bi == 0) & (i == 0))
        def _():
            dlb_ref[...] = part

        @pl.when((bi > 0) | (i > 0))
        def _():
            dlb_ref[...] += part

    def rev(first):
        return lambda h, b, i: (b * nb + (nb - 1 - i), first + h)

    blk = (tb, LANES)
    oblk = pl.BlockSpec(blk, rev(0))
    return _run(
        body, (z, z, z, lb, states, do), out_shape=(_sds((T, HG_W), F32),) * 3 + (_sds((1, HG_W), F32),),
        grid=(HG_HEADS, B, nb),
        in_specs=[pl.BlockSpec(blk, rev(12)), pl.BlockSpec(blk, rev(16)), pl.BlockSpec(blk, rev(20)),
                  pl.BlockSpec((1, LANES), lambda h, b, i: (0, h)),
                  pl.BlockSpec((nc * LANES, LANES), rev(0)), oblk],
        out_specs=(oblk, oblk, oblk, pl.BlockSpec((1, LANES), lambda h, b, i: (0, h))),
        scratch_shapes=[pltpu.VMEM((LANES, LANES), F32)], name=name)


def _hgout_fwd(o, z, g, *, name):
    T = o.shape[0]
    tm = _tile(T, 512, 8)

    def body(o_ref, gh_ref, g_ref, y_ref):
        ov = o_ref[...]
        gh = gh_ref[...]
        r = lax.rsqrt(_group_mean(ov * ov, HG_HEAD_DIM) + EPS)
        sg, _ = _sigmoid_pair(gh)
        y_ref[...] = ov * r * g_ref[...] * (gh * sg)

    blk = pl.BlockSpec((tm, HG_W), lambda i: (i, 0))
    return _run(body, (o, z, g), out_shape=_sds((T, HG_W), F32), grid=(T // tm,),
                in_specs=[blk, pl.BlockSpec((tm, HG_W), lambda i: (i, 6)), pl.BlockSpec((1, HG_W), lambda i: (0, 0))],
                out_specs=blk, name=name)


def _hgout_bwd(o, z, g, dycat, *, name):
    T = o.shape[0]
    tm = _tile(T, 512, 8)

    def body(o_ref, gh_ref, g_ref, dy_ref, do_ref, dgh_ref, dg_ref):
        i = pl.program_id(0)
        ov = o_ref[...]
        gh = gh_ref[...]
        gv = g_ref[...]
        dy = dy_ref[...]
        r = lax.rsqrt(_group_mean(ov * ov, HG_HEAD_DIM) + EPS)
        xh = ov * r
        sg, _ = _sigmoid_pair(gh)
        dn = dy * (gh * sg)
        dgh_ref[...] = dy * xh * gv * (sg * (1.0 + gh * (1.0 - sg)))
        dng = dn * gv
        do_ref[...] = r * (dng - xh * _group_mean(dng * xh, HG_HEAD_DIM))
        part = jnp.sum(dn * xh, axis=0, keepdims=True)

        @pl.when(i == 0)
        def _():
            dg_ref[...] = part

        @pl.when(i > 0)
        def _():
            dg_ref[...] += part

    blk = pl.BlockSpec((tm, HG_W), lambda i: (i, 0))
    row = pl.BlockSpec((1, HG_W), lambda i: (0, 0))
    return _run(body, (o, z, g, dycat),
                out_shape=(_sds((T, HG_W), F32), _sds((T, HG_W), F32), _sds((1, HG_W), F32)), grid=(T // tm,),
                in_specs=[blk, pl.BlockSpec((tm, HG_W), lambda i: (i, 6)), row,
                          pl.BlockSpec((tm, HG_W), lambda i: (i, 1))],
                out_specs=(blk, blk, row), name=name)


def _lower_bounds_fwd(hg_lb):
    L = hg_lb.shape[0]

    def body(x_ref, o_ref):
        rows = [x_ref[l:l + 1, :] for l in range(L)]
        mx = functools.reduce(jnp.maximum, rows)
        es = [jnp.exp(r - mx) for r in rows]
        tot = functools.reduce(jnp.add, es)
        acc = jnp.zeros_like(tot)
        for l in range(L):
            acc = acc + es[l] / tot
            o_ref[l:l + 1, :] = acc - es[0] / tot

    return _pcall(body, out_shape=_sds(hg_lb.shape, F32), name="lower_bounds_fwd")(hg_lb)


def _lower_bounds_bwd(hg_lb, dlower):
    L = hg_lb.shape[0]

    def body(x_ref, d_ref, o_ref):
        rows = [x_ref[l:l + 1, :] for l in range(L)]
        ds = [d_ref[l:l + 1, :] for l in range(L)]
        mx = functools.reduce(jnp.maximum, rows)
        es = [jnp.exp(r - mx) for r in rows]
        tot = functools.reduce(jnp.add, es)
        ps = [e / tot for e in es]
        dps = []
        for j in range(L):
            t = functools.reduce(jnp.add, ds[j:])
            if j == 0:
                t = t - functools.reduce(jnp.add, ds)
            dps.append(t)
        inner = functools.reduce(jnp.add, [p * dp for p, dp in zip(ps, dps)])
        for j in range(L):
            o_ref[j:j + 1, :] = ps[j] * (dps[j] - inner)

    return _pcall(body, out_shape=_sds(hg_lb.shape, F32), name="lower_bounds_bwd")(hg_lb, dlower)


def _memattn_fwd(qm, km, kv, *, B, S, NM, name):
    T, D = qm.shape
    E = D // MEM_HEADS
    tq = _tile(S, 512, 8)
    nq = S // tq
    scale = 1.0 / math.sqrt(E)

    def body(q_ref, k_ref, v_ref, o_ref):
        for h in range(MEM_HEADS):
            cs = slice(h * E, (h + 1) * E)
            s = lax.dot_general(q_ref[:, cs].astype(MXU_DTYPE), k_ref[:, cs].astype(MXU_DTYPE),
                                (((1,), (1,)), ((), ())), preferred_element_type=F32) * scale
            e = jnp.exp(s - jnp.max(s, axis=1, keepdims=True))
            pr = e / jnp.sum(e, axis=1, keepdims=True)
            o_ref[:, cs] = jnp.dot(pr.astype(MXU_DTYPE), v_ref[:, cs].astype(MXU_DTYPE),
                                   preferred_element_type=F32).astype(o_ref.dtype)

    return _run(
        body, (qm, km, kv), out_shape=_sds((T, D), MXU_DTYPE), grid=(B, nq),
        in_specs=[pl.BlockSpec((tq, D), lambda b, i: (b * nq + i, 0)), pl.BlockSpec((NM, D), lambda b, i: (b, 0)),
                  pl.BlockSpec((NM, D), lambda b, i: (b, 1))],
        out_specs=pl.BlockSpec((tq, D), lambda b, i: (b * nq + i, 0)), name=name)


def _memattn_bwd(qm, km, kv, do, *, B, S, NM, name):
    T, D = qm.shape
    E = D // MEM_HEADS
    tq = _tile(S, 512, 8)
    nq = S // tq
    scale = 1.0 / math.sqrt(E)

    def body(q_ref, k_ref, v_ref, do_ref, dq_ref, dk_ref, dv_ref):
        i = pl.program_id(1)

        @pl.when(i == 0)
        def _():
            dk_ref[...] = jnp.zeros_like(dk_ref)
            dv_ref[...] = jnp.zeros_like(dv_ref)

        for h in range(MEM_HEADS):
            cs = slice(h * E, (h + 1) * E)
            qh = q_ref[:, cs].astype(MXU_DTYPE)
            kh = k_ref[:, cs].astype(MXU_DTYPE)
            vh = v_ref[:, cs].astype(MXU_DTYPE)
            doh = do_ref[:, cs].astype(MXU_DTYPE)
            s = lax.dot_general(qh, kh, (((1,), (1,)), ((), ())), preferred_element_type=F32) * scale
            e = jnp.exp(s - jnp.max(s, axis=1, keepdims=True))
            pr = e / jnp.sum(e, axis=1, keepdims=True)
            dp = lax.dot_general(doh, vh, (((1,), (1,)), ((), ())), preferred_element_type=F32)
            ds = (pr * (dp - jnp.sum(dp * pr, axis=1, keepdims=True))).astype(MXU_DTYPE)
            dq_ref[:, cs] = jnp.dot(ds, kh, preferred_element_type=F32) * scale
            dk_ref[:, cs] += lax.dot_general(ds, qh, (((0,), (0,)), ((), ())), preferred_element_type=F32) * scale
            dv_ref[:, cs] += lax.dot_general(pr.astype(MXU_DTYPE), doh, (((0,), (0,)), ((), ())),
                                             preferred_element_type=F32)

    qblk = pl.BlockSpec((tq, D), lambda b, i: (b * nq + i, 0))
    mblk = pl.BlockSpec((NM, D), lambda b, i: (b, 0))
    return _run(
        body, (qm, km, kv, do), out_shape=(_sds((T, D), F32), _sds((B * NM, D), F32), _sds((B * NM, D), F32)),
        grid=(B, nq), in_specs=[qblk, mblk, pl.BlockSpec((NM, D), lambda b, i: (b, 1)), qblk],
        out_specs=(qblk, mblk, mblk), name=name)


def _loss_head(y, tgt):
    T, D = y.shape
    tm = _tile(T, 512, 8)

    def body(y_ref, t_ref, dy_ref, l_ref):
        i = pl.program_id(0)
        err = y_ref[...] - t_ref[...]
        dy_ref[...] = err * (1.0 / D)
        part = jnp.sum(jnp.sum(err * err, axis=1, keepdims=True), axis=0, keepdims=True) * (0.5 / D)

        @pl.when(i == 0)
        def _():
            l_ref[...] = part

        @pl.when(i > 0)
        def _():
            l_ref[...] += part

    blk = pl.BlockSpec((tm, D), lambda i: (i, 0))
    return _run(body, (y, tgt), out_shape=(_sds((T, D), F32), _sds((1, 1), F32)), grid=(T // tm,),
                in_specs=[blk, blk], out_specs=(blk, pl.BlockSpec((1, 1), lambda i: (0, 0))), name="loss_head")


def _adamw(w, g, m, v):
    m = ADAM_B1 * m + (1.0 - ADAM_B1) * g
    v = ADAM_B2 * v + (1.0 - ADAM_B2) * jnp.square(g)
    m_hat = m / (1.0 - ADAM_B1 ** ADAM_STEP)
    v_hat = v / (1.0 - ADAM_B2 ** ADAM_STEP)
    delta = -ADAM_LR * (m_hat / (jnp.sqrt(v_hat) + ADAM_EPS) + ADAM_WD * w)
    return delta, m, v


def _sum_adamw(recv, w, m, v):
    _, R, C = recv.shape
    tr = _tile(R, 384, 8)

    def body(r_ref, w_ref, m_ref, v_ref, g_ref, d_ref, nm_ref, nv_ref):
        g = r_ref[0].astype(F32)
        for j in range(1, N_DEV):
            g = g + r_ref[j].astype(F32)
        g_ref[...] = g
        d_ref[...], nm_ref[...], nv_ref[...] = _adamw(w_ref[...], g, m_ref[...], v_ref[...])

    blk = pl.BlockSpec((tr, C), lambda i: (i, 0))
    return _run(body, (recv, w, m, v), out_shape=(_sds((R, C), F32),) * 4, grid=(R // tr,),
                in_specs=[pl.BlockSpec((N_DEV, tr, C), lambda i: (0, i, 0)), blk, blk, blk],
                out_specs=(blk,) * 4, name="grad_sum_adamw")


def _small_sum_adamw(gp, w, m, v):
    NR, C = gp.shape
    MESH = pl.DeviceIdType.MESH

    def body(gp_ref, w_ref, m_ref, v_ref, g_ref, d_ref, nm_ref, nv_ref, gath, send_sems, recv_sems):
        x, y, c = _mesh_pos()
        me = 4 * x + 2 * y + c
        gath[me] = gp_ref[...]
        copies = []
        for k in range(1, N_DEV):
            peer = (_flip(x, k & 4), _flip(y, k & 2), _flip(c, k & 1))
            cp = pltpu.make_async_remote_copy(src_ref=gp_ref, dst_ref=gath.at[me], send_sem=send_sems.at[k - 1],
                                              recv_sem=recv_sems.at[k - 1], device_id=peer, device_id_type=MESH)
            cp.start()
            copies.append(cp)
        for cp in copies:
            cp.wait()
        g = gath[0]
        for j in range(1, N_DEV):
            g = g + gath[j]
        g_ref[...] = g
        d_ref[...], nm_ref[...], nv_ref[...] = _adamw(w_ref[...], g, m_ref[...], v_ref[...])

    vm = pl.BlockSpec(memory_space=pltpu.VMEM)
    return _pcall(body, out_shape=(_sds((NR, C), F32),) * 4, in_specs=[vm] * 4, out_specs=(vm,) * 4,
                  scratch_shapes=[pltpu.VMEM((N_DEV, NR, C), F32), pltpu.SemaphoreType.DMA((7,)),
                                  pltpu.SemaphoreType.DMA((7,))], name="small_sum_adamw")(gp, w, m, v)


def _pack(shards, names, dtype=F32):
    return jnp.concatenate([shards[n].reshape(-1, FLAT_W) for n in names], axis=0).astype(dtype)


def _unpack(flat, names, shapes):
    out, off = {}, 0
    for n in names:
        rows = math.prod(shapes[n]) // FLAT_W
        out[n] = flat[off:off + rows].reshape(shapes[n])
        off += rows
    return out


def _gathered(gath, names, shapes):
    out, off = {}, 0
    for n in names:
        r, c = shapes[n]
        rows = r * c // FLAT_W
        blk = gath[:, off:off + rows].reshape(N_DEV, r, c)
        out[n] = blk.transpose(1, 0, 2).reshape(r, N_DEV * c) if n in COL_SHARDED else blk.reshape(N_DEV * r, c)
        off += rows
    return out


def _scatter_blocks(grads, names, shapes, dtype):
    parts = []
    for n in names:
        r, c = shapes[n]
        g = grads[n]
        g = g.reshape(r, N_DEV, c).transpose(1, 0, 2) if n in COL_SHARDED else g.reshape(N_DEV, r, c)
        parts.append(g.reshape(N_DEV, -1, FLAT_W).astype(dtype))
    return jnp.concatenate(parts, axis=1)


def _pack_small(d, prefix, loss=None):
    parts = []
    for n in SMALL:
        a = d[prefix + n].reshape(-1).astype(F32)
        pad = (-a.shape[0]) % LANES
        parts.append(jnp.pad(a, (0, pad)) if pad else a)
    tail = jnp.zeros((LANES,), F32)
    if loss is not None:
        tail = tail.at[0].set(loss)
    flat = jnp.concatenate(parts + [tail])
    pad = (-flat.shape[0]) % (8 * LANES)
    return jnp.pad(flat, (0, pad)).reshape(-1, LANES)


def _unpack_small(packed, shapes):
    flat = packed.reshape(-1)
    out, off = {}, 0
    for n in SMALL:
        size = math.prod(shapes[n])
        out[n] = flat[off:off + size].reshape(shapes[n])
        off += size + (-size) % LANES
    return out, flat[off]


def _row(a):
    return a.reshape(1, -1).astype(F32)


def _layer_fwd(x, memf, w_in, shards, next_in, shapes, sp, lb, tab, dims):
    B, S, NM = dims
    D = x.shape[1]
    W = {"w_in": w_in}
    h1 = _norm_fwd(x, sp["norm1_g"], E=D, W=D, out_dtype=MXU_DTYPE, name="norm1_fwd")
    z, gath = _matmul(h1, w_in, carry=("gather", _pack(shards, GATHER_MIX, MXU_DTYPE)), name="in_proj")
    W.update(_gathered(gath, GATHER_MIX, shapes))
    qn = _norm_fwd(z, sp["attn_qn_g"], E=ATT_HEAD_DIM, W=ATT_W, cb=0, out_dtype=MXU_DTYPE, name="qnorm_fwd")
    kn = _norm_fwd(z, sp["attn_kn_g"], E=ATT_HEAD_DIM, W=ATT_W, cb=1, out_dtype=MXU_DTYPE, name="knorm_fwd")
    (ya, lse), gath = _attn_fwd(tab, qn, kn, z, B=B, S=S, carry=("gather", _pack(shards, GATHER_FF, MXU_DTYPE)),
                                name="dilated_attn_fwd")
    W.update(_gathered(gath, GATHER_FF, shapes))
    o, states = _hgrn_fwd(z, lb, B=B, S=S, name="hgrn_fwd")
    yh = _hgout_fwd(o, z, sp["hg_onorm_g"], name="hgrn_out_fwd")
    ycat = jnp.concatenate([ya, yh], axis=1).astype(MXU_DTYPE)
    x1 = _matmul(ycat, W["w_out"], epi="add", extra=x, name="out_proj")
    h2 = _norm_fwd(x1, sp["norm2_g"], E=D, W=D, out_dtype=MXU_DTYPE, name="norm2_fwd")
    qmp = _matmul(h2, W["w_mq"], name="mq_proj")
    qm = _norm_fwd(qmp, sp["mq_norm_g"], E=D // MEM_HEADS, W=D, out_dtype=MXU_DTYPE, name="mqnorm_fwd")
    mn = _norm_fwd(memf, sp["mem_norm_g"], E=D, W=D, out_dtype=MXU_DTYPE, name="memnorm_fwd")
    kv = _matmul(mn, W["w_mkv"], name="mkv_proj")
    km = _norm_fwd(kv, sp["mk_norm_g"], E=D // MEM_HEADS, W=D, cb=0, out_dtype=MXU_DTYPE, name="mknorm_fwd")
    om = _memattn_fwd(qm, km, kv, B=B, S=S, NM=NM, name="mem_attn_fwd")
    x2 = _matmul(om, W["w_mo"], epi="add", extra=x1, name="mo_proj")
    h3 = _norm_fwd(x2, sp["norm3_g"], E=D, W=D, out_dtype=MXU_DTYPE, name="norm3_fwd")
    if next_in is None:
        a, w_in_next = _matmul(h3, W["w_ff1"], name="ff1"), None
    else:
        a, gath = _matmul(h3, W["w_ff1"], carry=("gather", _pack({"w_in": next_in}, GATHER_IN, MXU_DTYPE)), name="ff1")
        w_in_next = _gathered(gath, GATHER_IN, shapes)["w_in"]
    x3 = _matmul(a, W["w_ff2"], pro="relu2", epi="add", extra=x2, name="ff2")
    saved = dict(x=x, h1=h1, z=z, qn=qn, kn=kn, ya=ya, lse=lse, o=o, states=states, ycat=ycat, x1=x1, h2=h2,
                 qmp=qmp, qm=qm, mn=mn, kv=kv, km=km, om=om, x2=x2, h3=h3, a=a)
    return x3, saved, W, w_in_next


def _layer_bwd(dx3, s, memf, W, shapes, sp, lb, tab, dims):
    B, S, NM = dims
    D = dx3.shape[1]
    E_M = D // MEM_HEADS
    gw, gs, recv = {}, {}, {}

    def blocks(names):
        return ("scatter", _scatter_blocks(gw, names, shapes, MXU_DTYPE))

    da = _matmul(dx3, W["w_ff2"], tb=True, epi="relu2grad", extra=s["a"], out_dtype=MXU_DTYPE, name="ff2_dgrad")
    gw["w_ff2"] = _matmul(s["a"], dx3, ta=True, pro="relu2", name="ff2_wgrad")
    dh3, recv[SCATTER_A] = _matmul(da, W["w_ff1"], tb=True, carry=blocks(SCATTER_A), name="ff1_dgrad")
    gw["w_ff1"] = _matmul(s["h3"], da, ta=True, name="ff1_wgrad")
    dx2, gs["norm3_g"] = _norm_bwd(s["x2"], sp["norm3_g"], dh3, E=D, W=D, res=dx3, name="norm3_bwd")
    dom = _matmul(dx2, W["w_mo"], tb=True, name="mo_dgrad")
    gw["w_mo"] = _matmul(s["om"], dx2, ta=True, name="mo_wgrad")
    dqm, dkm, dvm = _memattn_bwd(s["qm"], s["km"], s["kv"], dom, B=B, S=S, NM=NM, name="mem_attn_bwd")
    dqmp, gs["mq_norm_g"] = _norm_bwd(s["qmp"], sp["mq_norm_g"], dqm, E=E_M, W=D, fold=E_M, out_dtype=MXU_DTYPE,
                                      name="mqnorm_bwd")
    dkmp, gs["mk_norm_g"] = _norm_bwd(s["kv"], sp["mk_norm_g"], dkm, E=E_M, W=D, cb=0, fold=E_M,
                                      out_dtype=MXU_DTYPE, name="mknorm_bwd")
    dkv = jnp.concatenate([dkmp, dvm.astype(MXU_DTYPE)], axis=1)
    dh2 = _matmul(dqmp, W["w_mq"], tb=True, name="mq_dgrad")
    gw["w_mq"] = _matmul(s["h2"], dqmp, ta=True, name="mq_wgrad")
    dmn = _matmul(dkv, W["w_mkv"], tb=True, name="mkv_dgrad")
    gw["w_mkv"] = _matmul(s["mn"], dkv, ta=True, name="mkv_wgrad")
    _, gs["mem_norm_g"] = _norm_bwd(memf, sp["mem_norm_g"], dmn, E=D, W=D, name="memnorm_bwd")
    dx1, gs["norm2_g"] = _norm_bwd(s["x1"], sp["norm2_g"], dh2, E=D, W=D, res=dx2, name="norm2_bwd")
    dycat = _matmul(dx1, W["w_out"], tb=True, name="out_dgrad")
    gw["w_out"] = _matmul(s["ycat"], dx1, ta=True, name="out_wgrad")
    do_hg, dgh, gs["hg_onorm_g"] = _hgout_bwd(s["o"], s["z"], sp["hg_onorm_g"], dycat, name="hgrn_out_bwd")
    dqh, dfh, dih, dlb = _hgrn_bwd(s["z"], lb, s["states"], do_hg, B=B, S=S, name="hgrn_bwd")
    (dqn, dkn, dv), recv[SCATTER_B] = _attn_bwd(tab, s["qn"], s["kn"], s["z"], s["ya"], s["lse"], dycat, B=B, S=S,
                                                 carry=blocks(SCATTER_B), name="dilated_attn_bwd")
    dq, gs["attn_qn_g"] = _norm_bwd(s["z"], sp["attn_qn_g"], dqn, E=ATT_HEAD_DIM, W=ATT_W, cb=0, fold=ATT_HEAD_DIM,
                                    out_dtype=MXU_DTYPE, name="qnorm_bwd")
    dk, gs["attn_kn_g"] = _norm_bwd(s["z"], sp["attn_kn_g"], dkn, E=ATT_HEAD_DIM, W=ATT_W, cb=1, fold=ATT_HEAD_DIM,
                                    out_dtype=MXU_DTYPE, name="knorm_bwd")
    dz = jnp.concatenate([dq, dk] + [t.astype(MXU_DTYPE) for t in (dv, dqh, dfh, dih, dgh)], axis=1)
    gw["w_in"] = _matmul(s["h1"], dz, ta=True, name="in_wgrad")
    dh1, recv[SCATTER_C] = _matmul(dz, W["w_in"], tb=True, carry=blocks(SCATTER_C), name="in_dgrad")
    dx0, gs["norm1_g"] = _norm_bwd(s["x"], sp["norm1_g"], dh1, E=D, W=D, res=dx1, name="norm1_bwd")
    gs["attn_qn_g"] = gs["attn_qn_g"][:, :ATT_HEAD_DIM]
    gs["attn_kn_g"] = gs["attn_kn_g"][:, :ATT_HEAD_DIM]
    return dx0, recv, gs, dlb


def kernel(x, mem, norm1_g, w_in, attn_qn_g, attn_kn_g, hg_lb, hg_onorm_g, w_out, norm2_g, mem_norm_g, w_mq, w_mkv, mq_norm_g, mk_norm_g, w_mo, norm3_g, w_ff1, w_ff2, loss_target, m_norm1_g, m_w_in, m_attn_qn_g, m_attn_kn_g, m_hg_lb, m_hg_onorm_g, m_w_out, m_norm2_g, m_mem_norm_g, m_w_mq, m_w_mkv, m_mq_norm_g, m_mk_norm_g, m_w_mo, m_norm3_g, m_w_ff1, m_w_ff2, v_norm1_g, v_w_in, v_attn_qn_g, v_attn_kn_g, v_hg_lb, v_hg_onorm_g, v_w_out, v_norm2_g, v_mem_norm_g, v_w_mq, v_w_mkv, v_mq_norm_g, v_mk_norm_g, v_w_mo, v_norm3_g, v_w_ff1, v_w_ff2):
    given = dict(norm1_g=norm1_g, w_in=w_in, attn_qn_g=attn_qn_g, attn_kn_g=attn_kn_g, hg_lb=hg_lb, hg_onorm_g=hg_onorm_g, w_out=w_out, norm2_g=norm2_g, mem_norm_g=mem_norm_g, w_mq=w_mq, w_mkv=w_mkv, mq_norm_g=mq_norm_g, mk_norm_g=mk_norm_g, w_mo=w_mo, norm3_g=norm3_g, w_ff1=w_ff1, w_ff2=w_ff2, m_norm1_g=m_norm1_g, m_w_in=m_w_in, m_attn_qn_g=m_attn_qn_g, m_attn_kn_g=m_attn_kn_g, m_hg_lb=m_hg_lb, m_hg_onorm_g=m_hg_onorm_g, m_w_out=m_w_out, m_norm2_g=m_norm2_g, m_mem_norm_g=m_mem_norm_g, m_w_mq=m_w_mq, m_w_mkv=m_w_mkv, m_mq_norm_g=m_mq_norm_g, m_mk_norm_g=m_mk_norm_g, m_w_mo=m_w_mo, m_norm3_g=m_norm3_g, m_w_ff1=m_w_ff1, m_w_ff2=m_w_ff2, v_norm1_g=v_norm1_g, v_w_in=v_w_in, v_attn_qn_g=v_attn_qn_g, v_attn_kn_g=v_attn_kn_g, v_hg_lb=v_hg_lb, v_hg_onorm_g=v_hg_onorm_g, v_w_out=v_w_out, v_norm2_g=v_norm2_g, v_mem_norm_g=v_mem_norm_g, v_w_mq=v_w_mq, v_w_mkv=v_w_mkv, v_mq_norm_g=v_mq_norm_g, v_mk_norm_g=v_mk_norm_g, v_w_mo=v_w_mo, v_norm3_g=v_norm3_g, v_w_ff1=v_w_ff1, v_w_ff2=v_w_ff2)
    B, S, D = x.shape
    NM = mem.shape[1]
    L = w_in.shape[0]
    dims = (B, S, NM)
    shapes = {n: given[n].shape[1:] for n in BIG}
    small_shapes = {n: given[n].shape for n in SMALL}

    def shards(prefix, l):
        return {n: given[prefix + n][l] for n in BIG}

    lower = _lower_bounds_fwd(hg_lb)
    tab = _attn_tables(S, _tile(S, 256, 8))
    xf = x.reshape(B * S, D)
    memf = mem.reshape(B * NM, D)

    def small_params(l):
        sp = {"norm1_g": _row(norm1_g[l]), "norm2_g": _row(norm2_g[l]), "norm3_g": _row(norm3_g[l]),
              "mem_norm_g": _row(mem_norm_g[l]), "hg_onorm_g": _row(hg_onorm_g[l]),
              "attn_qn_g": _row(jnp.tile(attn_qn_g[l], ATT_HEADS)), "attn_kn_g": _row(jnp.tile(attn_kn_g[l], ATT_HEADS)),
              "mq_norm_g": _row(jnp.tile(mq_norm_g[l], MEM_HEADS)), "mk_norm_g": _row(jnp.tile(mk_norm_g[l], MEM_HEADS))}
        return sp, _row(lower[l])

    w_in_full = _gathered(_all_gather(_pack(shards("", 0), GATHER_IN, MXU_DTYPE)), GATHER_IN, shapes)["w_in"]
    saved, weights = [], []
    h = xf
    for l in range(L):
        sp, lb = small_params(l)
        next_in = given["w_in"][l + 1] if l + 1 < L else None
        h, s, W, w_in_full = _layer_fwd(h, memf, w_in_full, shards("", l), next_in, shapes, sp, lb, tab, dims)
        saved.append(s)
        weights.append(W)

    dh, loss_part = _loss_head(h, loss_target.reshape(B * S, D))

    recv_layers, gs_layers, dlb_layers = [None] * L, [None] * L, [None] * L
    for l in range(L - 1, -1, -1):
        sp, lb = small_params(l)
        dh, recv_layers[l], gs_layers[l], dlb_layers[l] = _layer_bwd(dh, saved[l], memf, weights[l], shapes, sp, lb,
                                                                      tab, dims)
    grad_x = dh.reshape(B, S, D)

    big_out = [{n: [None] * L for n in BIG} for _ in range(4)]
    for l in range(L):
        for names in SCATTERS:
            flats = _sum_adamw(recv_layers[l][names], _pack(shards("", l), names), _pack(shards("m_", l), names),
                               _pack(shards("v_", l), names))
            for k in range(4):
                for n, t in _unpack(flats[k], names, shapes).items():
                    big_out[k][n][l] = t
    big_out = [{n: jnp.stack(d[n]) for n in BIG} for d in big_out]

    gs = {n: jnp.stack([gs_layers[l][n].reshape(small_shapes[n][1:]) for l in range(L)]) for n in SMALL if n != "hg_lb"}
    gs["hg_lb"] = _lower_bounds_bwd(hg_lb, jnp.concatenate(dlb_layers, axis=0))
    packed = _small_sum_adamw(_pack_small(gs, "", loss_part[0, 0]), _pack_small(given, ""), _pack_small(given, "m_"),
                              _pack_small(given, "v_"))
    small_out, loss = [], None
    for t in packed:
        d, tail = _unpack_small(t, small_shapes)
        small_out.append(d)
        loss = tail if loss is None else loss

    outs = [loss, grad_x]
    for k in range(4):
        outs += [small_out[k][n] if n in small_shapes else big_out[k][n] for n in WEIGHTS]
    return tuple(outs)
```

```python
import functools
import math

import jax
import jax.numpy as jnp
from jax import lax
from jax.experimental import pallas as pl
from jax.experimental.pallas import tpu as pltpu

F32 = jnp.float32
MXU_DTYPE = jnp.bfloat16
VMEM_LIMIT = 48 * 1024 * 1024

N_DEV = 8
EPS = 1e-6
NEG = -1e30
F_MIN = 1e-12
ATT_HEADS = 8
ATT_HEAD_DIM = 64
ATT_W = ATT_HEADS * ATT_HEAD_DIM
DILATIONS = (1, 4, 16)
DIL_STEPS = 128
ATT_Q_BLOCK = 256
ATT_KEY_EXTENT = 512
HG_HEADS = 4
HG_HEAD_DIM = 128
HG_W = HG_HEADS * HG_HEAD_DIM
HG_CHUNK = 32
MEM_HEADS = 4
LANES = 128
FLAT_W = 1024

ADAM_LR = 0.001
ADAM_B1 = 0.9
ADAM_B2 = 0.999
ADAM_EPS = 1e-08
ADAM_WD = 0.01
ADAM_STEP = 10

BIG = ("w_in", "w_out", "w_mq", "w_mkv", "w_mo", "w_ff1", "w_ff2")
COL_SHARDED = ("w_in", "w_mkv", "w_ff1")
SMALL = ("norm1_g", "attn_qn_g", "attn_kn_g", "hg_lb", "hg_onorm_g", "norm2_g", "mem_norm_g",
         "mq_norm_g", "mk_norm_g", "norm3_g")
WEIGHTS = ("norm1_g", "w_in", "attn_qn_g", "attn_kn_g", "hg_lb", "hg_onorm_g", "w_out", "norm2_g",
           "mem_norm_g", "w_mq", "w_mkv", "mq_norm_g", "mk_norm_g", "w_mo", "norm3_g", "w_ff1", "w_ff2")
GATHER_IN = ("w_in",)
GATHER_MIX = ("w_out", "w_mq", "w_mkv", "w_mo")
GATHER_FF = ("w_ff1", "w_ff2")
SCATTER_A = ("w_ff2",)
SCATTER_B = ("w_ff1", "w_mo", "w_mq", "w_mkv", "w_out")
SCATTER_C = ("w_in",)
SCATTERS = (SCATTER_A, SCATTER_B, SCATTER_C)


def _pcall(body, **kw):
    return pl.pallas_call(body, **kw)


def _params():
    return pltpu.CompilerParams(vmem_limit_bytes=VMEM_LIMIT)


def _tile(n, pref, mult):
    t = (min(n, pref) // mult) * mult
    while t >= mult:
        if n % t == 0:
            return t
        t -= mult
    return n


def _sds(shape, dtype):
    return jax.ShapeDtypeStruct(shape, dtype)


def _mesh_pos():
    return lax.axis_index("x"), lax.axis_index("y"), lax.axis_index("c")


def _flip(v, bit):
    return 1 - v if bit else v


def _gather_hooks(x_ref, out_ref, send_sems, recv_sems, local_sem):
    MESH = pl.DeviceIdType.MESH
    x, y, c = _mesh_pos()
    me, sibling = (x, y, c), (x, y, 1 - c)
    chips = [(1 - x, y), (x, 1 - y), (1 - x, 1 - y)]

    def rows(px, py, pc):
        return out_ref.at[4 * px + 2 * py + pc]

    def copy(k, block, to, src=None):
        return pltpu.make_async_remote_copy(
            src_ref=rows(*block) if src is None else src, dst_ref=rows(*block), send_sem=send_sems.at[k],
            recv_sem=recv_sems.at[k], device_id=to, device_id_type=MESH)

    mine = pltpu.make_async_copy(x_ref, rows(*me), local_sem)
    first = [copy(0, me, sibling, src=x_ref)]
    first += [copy(1 + j, me, (*chip, c), src=x_ref) for j, chip in enumerate(chips)]
    passed = [copy(4 + j, (*chip, c), sibling) for j, chip in enumerate(chips)]

    def start():
        mine.start()
        for cp in first:
            cp.start()

    def forward():
        for j, chip in enumerate(chips):
            copy(1 + j, (*chip, c), me).wait_recv()
            passed[j].start()

    def finish():
        copy(0, sibling, me).wait_recv()
        for j, chip in enumerate(chips):
            copy(4 + j, (*chip, 1 - c), me).wait_recv()
        for cp in first + passed:
            cp.wait_send()
        mine.wait()

    return start, forward, finish


def _scatter_hooks(s_ref, r_ref, send_sems, recv_sems, local_sem):
    MESH = pl.DeviceIdType.MESH
    x, y, c = _mesh_pos()
    me = 4 * x + 2 * y + c
    mine = pltpu.make_async_copy(s_ref.at[me], r_ref.at[me], local_sem)
    copies = []
    for m in range(1, N_DEV):
        px, py, pc = _flip(x, m & 4), _flip(y, m & 2), _flip(c, m & 1)
        copies.append(pltpu.make_async_remote_copy(
            src_ref=s_ref.at[4 * px + 2 * py + pc], dst_ref=r_ref.at[me], send_sem=send_sems.at[m - 1],
            recv_sem=recv_sems.at[m - 1], device_id=(px, py, pc), device_id_type=MESH))

    def start():
        mine.start()
        for cp in copies:
            cp.start()

    def finish():
        for cp in copies:
            cp.wait()
        mine.wait()

    return start, None, finish


_EXCHANGE_SEMS = [pltpu.SemaphoreType.DMA((N_DEV - 1,)), pltpu.SemaphoreType.DMA((N_DEV - 1,)),
                  pltpu.SemaphoreType.DMA]


def _exchange_out(kind, xs):
    return _sds((N_DEV,) + xs.shape, xs.dtype) if kind == "gather" else _sds(xs.shape, xs.dtype)


def _run(body, args, *, out_shape, grid, in_specs, out_specs, scratch_shapes=(), name, carry=None):
    single = not isinstance(out_shape, (tuple, list))
    outs = (out_shape,) if single else tuple(out_shape)
    ospecs = (out_specs,) if single else tuple(out_specs)
    if carry is None:
        res = _pcall(body, out_shape=outs, grid=grid, in_specs=list(in_specs), out_specs=ospecs,
                     scratch_shapes=list(scratch_shapes), compiler_params=_params(), name=name)(*args)
        return res[0] if single else tuple(res)
    kind, xs = carry
    n_in, n_out, n_scr = len(args), len(outs), len(scratch_shapes)
    total = math.prod(grid)

    def wrapped(*refs):
        ins, x_ref = refs[:n_in], refs[n_in]
        o, xo_ref = refs[n_in + 1:n_in + 1 + n_out], refs[n_in + 1 + n_out]
        scr = refs[n_in + 2 + n_out:n_in + 2 + n_out + n_scr]
        sems = refs[n_in + 2 + n_out + n_scr:]
        step = pl.program_id(0)
        for ax in range(1, len(grid)):
            step = step * grid[ax] + pl.program_id(ax)
        start, forward, finish = (_gather_hooks if kind == "gather" else _scatter_hooks)(x_ref, xo_ref, *sems)
        pl.when(step == 0)(start)
        body(*ins, *o, *scr)
        if forward is not None:
            pl.when(step == (3 * total) // 4)(forward)
        pl.when(step == total - 1)(finish)

    hbm = pl.BlockSpec(memory_space=pl.ANY)
    res = _pcall(wrapped, out_shape=outs + (_exchange_out(kind, xs),), grid=grid, in_specs=list(in_specs) + [hbm],
                 out_specs=ospecs + (hbm,), scratch_shapes=list(scratch_shapes) + _EXCHANGE_SEMS,
                 compiler_params=_params(), name=name)(*args, xs)
    main = tuple(res[:-1])
    return (main[0] if single else main), res[-1]


def _all_gather(xs):
    def body(x_ref, out_ref, send_sems, recv_sems, local_sem):
        start, forward, finish = _gather_hooks(x_ref, out_ref, send_sems, recv_sems, local_sem)
        start()
        forward()
        finish()

    hbm = pl.BlockSpec(memory_space=pl.ANY)
    return _pcall(body, out_shape=_exchange_out("gather", xs), in_specs=[hbm], out_specs=hbm,
                  scratch_shapes=_EXCHANGE_SEMS, name="weights_all_gather")(xs)


def _matmul(a, b, *, ta=False, tb=False, pro=None, epi=None, extra=None, out_dtype=F32, carry=None, name):
    M, K = (a.shape[1], a.shape[0]) if ta else a.shape
    N = b.shape[0] if tb else b.shape[1]
    tm = _tile(M, 512, 8 if not ta else LANES)
    tn = _tile(N, 1024, LANES)
    tk = _tile(K, 512 if ta else 1024, LANES if not ta else 8)
    nk = K // tk
    dims = (((0 if ta else 1,), (1 if tb else 0,)), ((), ()))

    def body(a_ref, b_ref, *rest):
        if extra is not None:
            e_ref, o_ref, acc_ref = rest
        else:
            o_ref, acc_ref = rest
        k = pl.program_id(2)

        @pl.when(k == 0)
        def _():
            acc_ref[...] = jnp.zeros_like(acc_ref)

        av = a_ref[...]
        if pro == "relu2":
            av = jnp.square(jnp.maximum(av, 0.0))
        acc_ref[...] += lax.dot_general(av.astype(MXU_DTYPE), b_ref[...].astype(MXU_DTYPE), dims,
                                        preferred_element_type=F32)

        @pl.when(k == nk - 1)
        def _():
            r = acc_ref[...]
            if epi == "add":
                r = r + e_ref[...]
            elif epi == "relu2grad":
                r = r * (2.0 * jnp.maximum(e_ref[...], 0.0))
            o_ref[...] = r.astype(out_dtype)

    a_spec = pl.BlockSpec((tk, tm), lambda i, j, k: (k, i)) if ta else pl.BlockSpec((tm, tk), lambda i, j, k: (i, k))
    b_spec = pl.BlockSpec((tn, tk), lambda i, j, k: (j, k)) if tb else pl.BlockSpec((tk, tn), lambda i, j, k: (k, j))
    o_spec = pl.BlockSpec((tm, tn), lambda i, j, k: (i, j))
    in_specs = [a_spec, b_spec] + ([o_spec] if extra is not None else [])
    args = (a, b) + ((extra,) if extra is not None else ())
    return _run(body, args, out_shape=_sds((M, N), out_dtype), grid=(M // tm, N // tn, nk), in_specs=in_specs,
                out_specs=o_spec, scratch_shapes=[pltpu.VMEM((tm, tn), F32)], name=name, carry=carry)


def _group_mean(v, E):
    rows, W = v.shape
    if E == W:
        return jnp.mean(v, axis=-1, keepdims=True)
    pieces = []
    if E % LANES == 0:
        for g0 in range(0, W, E):
            m = jnp.mean(v[:, g0:g0 + E], axis=-1, keepdims=True)
            pieces.append(jnp.broadcast_to(m, (rows, E)))
    else:
        lane = lax.broadcasted_iota(jnp.int32, (rows, LANES), 1)
        for c0 in range(0, W, LANES):
            vc = v[:, c0:c0 + LANES]
            acc = jnp.zeros((rows, LANES), F32)
            for s0 in range(0, LANES, E):
                msk = (lane >= s0) & (lane < s0 + E)
                m = jnp.sum(jnp.where(msk, vc, 0.0), axis=-1, keepdims=True) * (1.0 / E)
                acc = jnp.where(msk, m, acc)
            pieces.append(acc)
    return jnp.concatenate(pieces, axis=-1)


def _fold_groups(t, E):
    W = t.shape[1]
    step = max(E, LANES)
    acc = t[:, 0:step]
    for c0 in range(step, W, step):
        acc = acc + t[:, c0:c0 + step]
    sh = LANES // 2
    while sh >= E:
        acc = acc + pltpu.roll(acc, sh, 1)
        sh //= 2
    return acc


def _norm_fwd(x, g, *, E, W, cb=0, out_dtype, name):
    M = x.shape[0]
    tm = _tile(M, 512, 8)

    def body(x_ref, g_ref, o_ref):
        xv = x_ref[...]
        r = lax.rsqrt(_group_mean(xv * xv, E) + EPS)
        o_ref[...] = (xv * r * g_ref[...]).astype(out_dtype)

    return _run(body, (x, g), out_shape=_sds((M, W), out_dtype), grid=(M // tm,),
                in_specs=[pl.BlockSpec((tm, W), lambda i: (i, cb)), pl.BlockSpec((1, W), lambda i: (0, 0))],
                out_specs=pl.BlockSpec((tm, W), lambda i: (i, 0)), name=name)


def _norm_bwd(x, g, dy, *, E, W, cb=0, res=None, fold=None, out_dtype=F32, name):
    M = x.shape[0]
    tm = _tile(M, 512, 8)
    n = M // tm
    gw = W if fold is None else max(fold, LANES)

    def body(x_ref, g_ref, dy_ref, *rest):
        if res is not None:
            r_ref, dx_ref, dg_ref, acc_ref = rest
        else:
            dx_ref, dg_ref, acc_ref = rest
        i = pl.program_id(0)
        xv = x_ref[...]
        r = lax.rsqrt(_group_mean(xv * xv, E) + EPS)
        xh = xv * r
        dyv = dy_ref[...].astype(F32)
        dyg = dyv * g_ref[...]
        dx = r * (dyg - xh * _group_mean(dyg * xh, E))
        if res is not None:
            dx = dx + r_ref[...]
        dx_ref[...] = dx.astype(out_dtype)
        part = jnp.sum(dyv * xh, axis=0, keepdims=True)

        @pl.when(i == 0)
        def _():
            acc_ref[...] = part

        @pl.when(i > 0)
        def _():
            acc_ref[...] += part

        @pl.when(i == n - 1)
        def _():
            t = acc_ref[...]
            dg_ref[...] = t if fold is None else _fold_groups(t, fold)

    blk = pl.BlockSpec((tm, W), lambda i: (i, 0))
    in_specs = [pl.BlockSpec((tm, W), lambda i: (i, cb)), pl.BlockSpec((1, W), lambda i: (0, 0)), blk]
    args = [x, g, dy]
    if res is not None:
        in_specs.append(blk)
        args.append(res)
    return _run(body, tuple(args), out_shape=(_sds((M, W), out_dtype), _sds((1, gw), F32)), grid=(n,),
                in_specs=in_specs, out_specs=(blk, pl.BlockSpec((1, gw), lambda i: (0, 0))),
                scratch_shapes=[pltpu.VMEM((1, W), F32)], name=name)


def _attn_tables(S, tq):
    nq = S // tq

    def body(o_ref):
        i = pl.program_id(0)
        d = (i * tq + lax.broadcasted_iota(jnp.int32, (tq, S), 0)) - lax.broadcasted_iota(jnp.int32, (tq, S), 1)
        cnt = jnp.zeros((tq, S), jnp.int32)
        for dil in DILATIONS:
            hit = (d <= DIL_STEPS * dil) if dil == 1 else (((d & (dil - 1)) == 0) & (d <= DIL_STEPS * dil))
            cnt = cnt + hit.astype(jnp.int32)
        ok = (d >= 0) & (cnt > 0)
        logm = jnp.where(cnt == 3, math.log(3.0), jnp.where(cnt == 2, math.log(2.0), 0.0))
        o_ref[0] = jnp.where(ok, logm, NEG).astype(F32)

    return _run(body, (), out_shape=_sds((nq, tq, S), F32), grid=(nq,), in_specs=[],
                out_specs=pl.BlockSpec((1, tq, S), lambda i: (i, 0, 0)), name="dilated_attn_tables")


def _alibi_slope(h):
    return 2.0 ** (-8.0 / ATT_HEADS * (h + 1))


def _key_positions(Sk):
    kpos = lax.broadcasted_iota(jnp.int32, (Sk, LANES), 0)
    return (kpos >> 8).astype(F32).astype(MXU_DTYPE), (kpos & 255).astype(F32).astype(MXU_DTYPE)


def _score_operands(q2s, k2, pos_hi, pos_lo, lane, sub, h):
    own = (lane < ATT_HEAD_DIM) if sub == 0 else (lane >= ATT_HEAD_DIM)
    spare = ATT_HEAD_DIM if sub == 0 else 0
    slope = _alibi_slope(h)
    terms = jnp.where(lane == spare, slope * 256.0, jnp.where(lane == spare + 1, slope, 0.0)).astype(q2s.dtype)
    qa = jnp.where(own, q2s, terms)
    ka = jnp.where(lane == spare, pos_hi, jnp.where(lane == spare + 1, pos_lo, k2))
    return own, qa, ka


def _key_extents(S, tq):
    ext = min(ATT_KEY_EXTENT, S)
    return ext, ext // tq, S // ext


def _attn_fwd(tab, qn, kn, z, *, B, S, carry=None, name):
    T = B * S
    tq = tab.shape[1]
    nq = S // tq
    ext, per, n_ext = _key_extents(S, tq)
    scale = 1.0 / math.sqrt(ATT_HEAD_DIM)
    dn_t = (((1,), (1,)), ((), ()))

    def body(tab_ref, q_ref, k_ref, v_ref, y_ref, lse_ref):
        i = pl.program_id(1)
        lane = lax.broadcasted_iota(jnp.int32, (1, LANES), 1)
        lo = lane < ATT_HEAD_DIM

        def visit(Sk):
            base = tab_ref[0, :, :Sk]
            pos_hi, pos_lo = _key_positions(Sk)
            lse_blk = jnp.zeros((tq, LANES), F32)
            for p in range(ATT_HEADS // 2):
                cs = slice(p * LANES, (p + 1) * LANES)
                q2s = q_ref[:, cs].astype(MXU_DTYPE) * scale
                k2 = k_ref[:Sk, cs].astype(MXU_DTYPE)
                v2 = v_ref[:Sk, cs].astype(MXU_DTYPE)
                outs = []
                for sub in range(2):
                    h = 2 * p + sub
                    _, qa, ka = _score_operands(q2s, k2, pos_hi, pos_lo, lane, sub, h)
                    s = lax.dot_general(qa, ka, dn_t, preferred_element_type=F32) + base
                    mx = jnp.max(s, axis=1, keepdims=True)
                    e = jnp.exp(s - mx)
                    l = jnp.sum(e, axis=1, keepdims=True)
                    outs.append(jnp.dot(e.astype(MXU_DTYPE), v2, preferred_element_type=F32) / l)
                    lse_blk = jnp.where(lane == h, mx + jnp.log(l), lse_blk)
                y_ref[:, cs] = jnp.where(lo, outs[0], outs[1])
            lse_ref[...] = lse_blk

        for e in range(n_ext):
            pl.when(i // per == e)(functools.partial(visit, (e + 1) * ext))

    return _run(
        body, (tab, qn, kn, z), out_shape=(_sds((T, ATT_W), F32), _sds((T, LANES), F32)), grid=(B, nq),
        in_specs=[pl.BlockSpec((1, tq, S), lambda b, i: (i, 0, 0)),
                  pl.BlockSpec((tq, ATT_W), lambda b, i: (b * nq + i, 0)),
                  pl.BlockSpec((S, ATT_W), lambda b, i: (b, 0)),
                  pl.BlockSpec((S, ATT_W), lambda b, i: (b, 2))],
        out_specs=(pl.BlockSpec((tq, ATT_W), lambda b, i: (b * nq + i, 0)),
                   pl.BlockSpec((tq, LANES), lambda b, i: (b * nq + i, 0))),
        name=name, carry=carry)


def _attn_bwd(tab, qn, kn, z, y, lse, dycat, *, B, S, carry=None, name):
    T = B * S
    tq = tab.shape[1]
    nq = S // tq
    ext, per, n_ext = _key_extents(S, tq)
    scale = 1.0 / math.sqrt(ATT_HEAD_DIM)
    dn_t = (((1,), (1,)), ((), ()))
    dn_o = (((0,), (0,)), ((), ()))

    def body(tab_ref, q_ref, k_ref, v_ref, y_ref, lse_ref, dy_ref, dq_ref, dk_ref, dv_ref):
        i = pl.program_id(1)

        @pl.when(i == 0)
        def _():
            dk_ref[...] = jnp.zeros_like(dk_ref)
            dv_ref[...] = jnp.zeros_like(dv_ref)

        lane = lax.broadcasted_iota(jnp.int32, (1, LANES), 1)
        lo = lane < ATT_HEAD_DIM

        def visit(Sk):
            base = tab_ref[0, :, :Sk]
            pos_hi, pos_lo = _key_positions(Sk)
            lse_blk = lse_ref[...]
            for p in range(ATT_HEADS // 2):
                cs = slice(p * LANES, (p + 1) * LANES)
                q2 = q_ref[:, cs].astype(MXU_DTYPE)
                q2s = q2 * scale
                k2 = k_ref[:Sk, cs].astype(MXU_DTYPE)
                v2 = v_ref[:Sk, cs].astype(MXU_DTYPE)
                do2 = dy_ref[:, cs]
                doy = do2 * y_ref[:, cs]
                do2m = do2.astype(MXU_DTYPE)
                dqs, dks, dvs = [], [], []
                for sub in range(2):
                    h = 2 * p + sub
                    own, qa, ka = _score_operands(q2s, k2, pos_hi, pos_lo, lane, sub, h)
                    s = lax.dot_general(qa, ka, dn_t, preferred_element_type=F32) + base
                    lse_h = jnp.sum(jnp.where(lane == h, lse_blk, 0.0), axis=1, keepdims=True)
                    pr = jnp.exp(s - lse_h)
                    dsum = jnp.sum(jnp.where(own, doy, 0.0), axis=1, keepdims=True)
                    dom = jnp.where(own, do2m, jnp.zeros_like(do2m))
                    dp = lax.dot_general(dom, v2, dn_t, preferred_element_type=F32)
                    ds = (pr * (dp - dsum)).astype(MXU_DTYPE)
                    dqs.append(jnp.dot(ds, k2, preferred_element_type=F32))
                    dks.append(lax.dot_general(ds, q2, dn_o, preferred_element_type=F32))
                    dvs.append(lax.dot_general(pr.astype(MXU_DTYPE), do2m, dn_o, preferred_element_type=F32))
                dq_ref[:, cs] = jnp.where(lo, dqs[0], dqs[1]) * scale
                dk_ref[:Sk, cs] += jnp.where(lo, dks[0], dks[1]) * scale
                dv_ref[:Sk, cs] += jnp.where(lo, dvs[0], dvs[1])

        for e in range(n_ext):
            pl.when(i // per == e)(functools.partial(visit, (e + 1) * ext))

    qblk = pl.BlockSpec((tq, ATT_W), lambda b, i: (b * nq + i, 0))
    sblk = pl.BlockSpec((S, ATT_W), lambda b, i: (b, 0))
    return _run(
        body, (tab, qn, kn, z, y, lse, dycat), out_shape=(_sds((T, ATT_W), F32),) * 3, grid=(B, nq),
        in_specs=[pl.BlockSpec((1, tq, S), lambda b, i: (i, 0, 0)), qblk, sblk,
                  pl.BlockSpec((S, ATT_W), lambda b, i: (b, 2)), qblk,
                  pl.BlockSpec((tq, LANES), lambda b, i: (b * nq + i, 0)), qblk],
        out_specs=(qblk, sblk, sblk), name=name, carry=carry)


def _sigmoid_pair(x):
    en = jnp.exp(-jnp.abs(x))
    big = 1.0 / (1.0 + en)
    small = en * big
    pos = x >= 0
    return jnp.where(pos, big, small), jnp.where(pos, small, big)


def _chunk_scan(x, pos, reverse):
    n = x.shape[0]
    sh = 1
    while sh < HG_CHUNK:
        if reverse:
            x = x + jnp.where(pos < HG_CHUNK - sh, pltpu.roll(x, n - sh, 0), 0.0)
        else:
            x = x + jnp.where(pos >= sh, pltpu.roll(x, sh, 0), 0.0)
        sh *= 2
    return x


def _hgrn_gates(qh, fh, lb):
    sq, _ = _sigmoid_pair(qh)
    sg, sgn = _sigmoid_pair(fh)
    f = lb + (1.0 - lb) * sg
    g = jnp.log(jnp.maximum(f, F_MIN))
    kk = (1.0 - lb) * sgn
    return qh * sq, sq, sg, sgn, f, g, kk


def _hgrn_fwd(z, lb, *, B, S, name):
    T = B * S
    tb = _tile(S, 256, HG_CHUNK)
    nc = tb // HG_CHUNK
    nb = S // tb
    dn_t = (((1,), (1,)), ((), ()))
    dn_o = (((0,), (0,)), ((), ()))

    def zcol(first):
        return lambda h, b, i: (b * nb + i, first + h)

    def body(q_ref, f_ref, v_ref, lb_ref, o_ref, st_ref, state):
        i = pl.program_id(2)

        @pl.when(i == 0)
        def _():
            state[...] = jnp.zeros_like(state)

        q, _, _, _, _, g, kk = _hgrn_gates(q_ref[...], f_ref[...], lb_ref[...])
        v = v_ref[...]
        pos = lax.broadcasted_iota(jnp.int32, (tb, LANES), 0) % HG_CHUNK
        b = _chunk_scan(g, pos, False)
        o = jnp.sum(q * kk, axis=1, keepdims=True) * v
        for d in range(1, HG_CHUNK):
            qd = pltpu.roll(q, tb - d, 0)
            bd = pltpu.roll(b, tb - d, 0)
            e = jnp.where(pos < HG_CHUNK - d, jnp.exp(jnp.minimum(bd - b, 0.0)), 0.0)
            a = jnp.sum(qd * kk * e, axis=1, keepdims=True)
            o = o + pltpu.roll(a * v, d, 0)
        qe = q * jnp.exp(b)
        st = state[...]
        for c in range(nc):
            rs = slice(c * HG_CHUNK, (c + 1) * HG_CHUNK)
            st_ref[c * LANES:(c + 1) * LANES, :] = st
            b_last = jnp.sum(jnp.where(pos[rs] == HG_CHUNK - 1, b[rs], 0.0), axis=0, keepdims=True)
            ke = kk[rs] * jnp.exp(b_last - b[rs])
            o_ref[rs, :] = o[rs] + lax.dot_general(qe[rs].astype(MXU_DTYPE), st.astype(MXU_DTYPE), dn_t,
                                                   preferred_element_type=F32)
            st = st * jnp.exp(b_last) + lax.dot_general(v[rs].astype(MXU_DTYPE), ke.astype(MXU_DTYPE), dn_o,
                                                        preferred_element_type=F32)
        state[...] = st

    blk = (tb, LANES)
    return _run(
        body, (z, z, z, lb), out_shape=(_sds((T, HG_W), F32), _sds((T // HG_CHUNK * LANES, HG_W), F32)),
        grid=(HG_HEADS, B, nb),
        in_specs=[pl.BlockSpec(blk, zcol(12)), pl.BlockSpec(blk, zcol(16)), pl.BlockSpec(blk, zcol(20)),
                  pl.BlockSpec((1, LANES), lambda h, b, i: (0, h))],
        out_specs=(pl.BlockSpec(blk, lambda h, b, i: (b * nb + i, h)),
                   pl.BlockSpec((nc * LANES, LANES), lambda h, b, i: (b * nb + i, h))),
        scratch_shapes=[pltpu.VMEM((LANES, LANES), F32)], name=name)


def _hgrn_bwd(z, lb, states, do, *, B, S, name):
    T = B * S
    tb = _tile(S, 256, HG_CHUNK)
    nc = tb // HG_CHUNK
    nb = S // tb
    dn_t = (((1,), (1,)), ((), ()))
    dn_o = (((0,), (0,)), ((), ()))

    def body(q_ref, f_ref, v_ref, lb_ref, st_ref, do_ref, dq_ref, df_ref, di_ref, dlb_ref, dstate):
        bi = pl.program_id(1)
        i = pl.program_id(2)

        @pl.when(i == 0)
        def _():
            dstate[...] = jnp.zeros_like(dstate)

        qh = q_ref[...]
        lbv = lb_ref[...]
        q, sq, sg, sgn, f, g, kk = _hgrn_gates(qh, f_ref[...], lbv)
        v = v_ref[...]
        dov = do_ref[...]
        pos = lax.broadcasted_iota(jnp.int32, (tb, LANES), 0) % HG_CHUNK
        b = _chunk_scan(g, pos, False)

        a0 = jnp.sum(q * kk, axis=1, keepdims=True)
        da0 = jnp.sum(dov * v, axis=1, keepdims=True)
        dv = a0 * dov
        dkk = da0 * q
        dq = da0 * kk
        db = jnp.zeros((tb, LANES), F32)
        for d in range(1, HG_CHUNK):
            qd = pltpu.roll(q, tb - d, 0)
            bd = pltpu.roll(b, tb - d, 0)
            dod = pltpu.roll(dov, tb - d, 0)
            e = jnp.where(pos < HG_CHUNK - d, jnp.exp(jnp.minimum(bd - b, 0.0)), 0.0)
            a = jnp.sum(qd * kk * e, axis=1, keepdims=True)
            da = jnp.sum(dod * v, axis=1, keepdims=True)
            dv = dv + a * dod
            xe = da * e
            dkk = dkk + xe * qd
            t1 = xe * kk
            wt = t1 * qd
            dq = dq + pltpu.roll(t1, d, 0)
            db = db + pltpu.roll(wt, d, 0) - wt

        eb = jnp.exp(b)
        qe = q * eb
        dst = dstate[...]
        dq_c, dkk_c, dv_c, db_c = [None] * nc, [None] * nc, [None] * nc, [None] * nc
        for c in range(nc - 1, -1, -1):
            rs = slice(c * HG_CHUNK, (c + 1) * HG_CHUNK)
            st0 = st_ref[c * LANES:(c + 1) * LANES, :]
            b_last = jnp.sum(jnp.where(pos[rs] == HG_CHUNK - 1, b[rs], 0.0), axis=0, keepdims=True)
            eb_last = jnp.exp(b_last)
            er = jnp.exp(b_last - b[rs])
            ke = kk[rs] * er
            do_c = dov[rs].astype(MXU_DTYPE)
            dstm = dst.astype(MXU_DTYPE)
            dqe = jnp.dot(do_c, st0.astype(MXU_DTYPE), preferred_element_type=F32)
            dv_c[c] = dv[rs] + lax.dot_general(ke.astype(MXU_DTYPE), dstm, dn_t, preferred_element_type=F32)
            dke = jnp.dot(v[rs].astype(MXU_DTYPE), dstm, preferred_element_type=F32)
            d_eb_last = jnp.sum(dst * st0, axis=0, keepdims=True)
            dst = dst * eb_last + lax.dot_general(do_c, qe[rs].astype(MXU_DTYPE), dn_o, preferred_element_type=F32)
            dq_c[c] = dq[rs] + dqe * eb[rs]
            dkk_c[c] = dkk[rs] + dke * er
            dkeke = dke * ke
            db_last = jnp.sum(dkeke, axis=0, keepdims=True) + d_eb_last * eb_last
            db_c[c] = db[rs] + dqe * qe[rs] - dkeke + jnp.where(pos[rs] == HG_CHUNK - 1, db_last, 0.0)
        dstate[...] = dst
        dq = jnp.concatenate(dq_c, axis=0)
        dkk = jnp.concatenate(dkk_c, axis=0)
        dv = jnp.concatenate(dv_c, axis=0)
        dg = _chunk_scan(jnp.concatenate(db_c, axis=0), pos, True)

        dfv = dg * jnp.where(f > F_MIN, 1.0 / f, 0.0)
        dsg = (dfv - dkk) * (1.0 - lbv)
        dq_ref[...] = dq * (sq * (1.0 + qh * (1.0 - sq)))
        df_ref[...] = dsg * sg * sgn
        di_ref[...] = dv
        part = jnp.sum(dfv * sgn - dkk * sgn, axis=0, keepdims=True)

        @pl.when((bi == 0) & (i == 0))
        def _():
            dlb_ref[...] = part

        @pl.when((bi > 0) | (i > 0))
        def _():
            dlb_ref[...] += part

    def rev(first):
        return lambda h, b, i: (b * nb + (nb - 1 - i), first + h)

    blk = (tb, LANES)
    oblk = pl.BlockSpec(blk, rev(0))
    return _run(
        body, (z, z, z, lb, states, do), out_shape=(_sds((T, HG_W), F32),) * 3 + (_sds((1, HG_W), F32),),
        grid=(HG_HEADS, B, nb),
        in_specs=[pl.BlockSpec(blk, rev(12)), pl.BlockSpec(blk, rev(16)), pl.BlockSpec(blk, rev(20)),
                  pl.BlockSpec((1, LANES), lambda h, b, i: (0, h)),
                  pl.BlockSpec((nc * LANES, LANES), rev(0)), oblk],
        out_specs=(oblk, oblk, oblk, pl.BlockSpec((1, LANES), lambda h, b, i: (0, h))),
        scratch_shapes=[pltpu.VMEM((LANES, LANES), F32)], name=name)


def _hgout_fwd(o, z, g, *, name):
    T = o.shape[0]
    tm = _tile(T, 512, 8)

    def body(o_ref, gh_ref, g_ref, y_ref):
        ov = o_ref[...]
        gh = gh_ref[...]
        r = lax.rsqrt(_group_mean(ov * ov, HG_HEAD_DIM) + EPS)
        sg, _ = _sigmoid_pair(gh)
        y_ref[...] = ov * r * g_ref[...] * (gh * sg)

    blk = pl.BlockSpec((tm, HG_W), lambda i: (i, 0))
    return _run(body, (o, z, g), out_shape=_sds((T, HG_W), F32), grid=(T // tm,),
                in_specs=[blk, pl.BlockSpec((tm, HG_W), lambda i: (i, 6)), pl.BlockSpec((1, HG_W), lambda i: (0, 0))],
                out_specs=blk, name=name)


def _hgout_bwd(o, z, g, dycat, *, name):
    T = o.shape[0]
    tm = _tile(T, 512, 8)

    def body(o_ref, gh_ref, g_ref, dy_ref, do_ref, dgh_ref, dg_ref):
        i = pl.program_id(0)
        ov = o_ref[...]
        gh = gh_ref[...]
        gv = g_ref[...]
        dy = dy_ref[...]
        r = lax.rsqrt(_group_mean(ov * ov, HG_HEAD_DIM) + EPS)
        xh = ov * r
        sg, _ = _sigmoid_pair(gh)
        dn = dy * (gh * sg)
        dgh_ref[...] = dy * xh * gv * (sg * (1.0 + gh * (1.0 - sg)))
        dng = dn * gv
        do_ref[...] = r * (dng - xh * _group_mean(dng * xh, HG_HEAD_DIM))
        part = jnp.sum(dn * xh, axis=0, keepdims=True)

        @pl.when(i == 0)
        def _():
            dg_ref[...] = part

        @pl.when(i > 0)
        def _():
            dg_ref[...] += part

    blk = pl.BlockSpec((tm, HG_W), lambda i: (i, 0))
    row = pl.BlockSpec((1, HG_W), lambda i: (0, 0))
    return _run(body, (o, z, g, dycat),
                out_shape=(_sds((T, HG_W), F32), _sds((T, HG_W), F32), _sds((1, HG_W), F32)), grid=(T // tm,),
                in_specs=[blk, pl.BlockSpec((tm, HG_W), lambda i: (i, 6)), row,
                          pl.BlockSpec((tm, HG_W), lambda i: (i, 1))],
                out_specs=(blk, blk, row), name=name)


def _lower_bounds_fwd(hg_lb):
    L = hg_lb.shape[0]

    def body(x_ref, o_ref):
        rows = [x_ref[l:l + 1, :] for l in range(L)]
        mx = functools.reduce(jnp.maximum, rows)
        es = [jnp.exp(r - mx) for r in rows]
        tot = functools.reduce(jnp.add, es)
        acc = jnp.zeros_like(tot)
        for l in range(L):
            acc = acc + es[l] / tot
            o_ref[l:l + 1, :] = acc - es[0] / tot

    return _pcall(body, out_shape=_sds(hg_lb.shape, F32), name="lower_bounds_fwd")(hg_lb)


def _lower_bounds_bwd(hg_lb, dlower):
    L = hg_lb.shape[0]

    def body(x_ref, d_ref, o_ref):
        rows = [x_ref[l:l + 1, :] for l in range(L)]
        ds = [d_ref[l:l + 1, :] for l in range(L)]
        mx = functools.reduce(jnp.maximum, rows)
        es = [jnp.exp(r - mx) for r in rows]
        tot = functools.reduce(jnp.add, es)
        ps = [e / tot for e in es]
        dps = []
        for j in range(L):
            t = functools.reduce(jnp.add, ds[j:])
            if j == 0:
                t = t - functools.reduce(jnp.add, ds)
            dps.append(t)
        inner = functools.reduce(jnp.add, [p * dp for p, dp in zip(ps, dps)])
        for j in range(L):
            o_ref[j:j + 1, :] = ps[j] * (dps[j] - inner)

    return _pcall(body, out_shape=_sds(hg_lb.shape, F32), name="lower_bounds_bwd")(hg_lb, dlower)


def _memattn_fwd(qm, km, kv, *, B, S, NM, name):
    T, D = qm.shape
    E = D // MEM_HEADS
    tq = _tile(S, 512, 8)
    nq = S // tq
    scale = 1.0 / math.sqrt(E)

    def body(q_ref, k_ref, v_ref, o_ref):
        for h in range(MEM_HEADS):
            cs = slice(h * E, (h + 1) * E)
            s = lax.dot_general(q_ref[:, cs].astype(MXU_DTYPE), k_ref[:, cs].astype(MXU_DTYPE),
                                (((1,), (1,)), ((), ())), preferred_element_type=F32) * scale
            e = jnp.exp(s - jnp.max(s, axis=1, keepdims=True))
            pr = e / jnp.sum(e, axis=1, keepdims=True)
            o_ref[:, cs] = jnp.dot(pr.astype(MXU_DTYPE), v_ref[:, cs].astype(MXU_DTYPE),
                                   preferred_element_type=F32).astype(o_ref.dtype)

    return _run(
        body, (qm, km, kv), out_shape=_sds((T, D), MXU_DTYPE), grid=(B, nq),
        in_specs=[pl.BlockSpec((tq, D), lambda b, i: (b * nq + i, 0)), pl.BlockSpec((NM, D), lambda b, i: (b, 0)),
                  pl.BlockSpec((NM, D), lambda b, i: (b, 1))],
        out_specs=pl.BlockSpec((tq, D), lambda b, i: (b * nq + i, 0)), name=name)


def _memattn_bwd(qm, km, kv, do, *, B, S, NM, name):
    T, D = qm.shape
    E = D // MEM_HEADS
    tq = _tile(S, 512, 8)
    nq = S // tq
    scale = 1.0 / math.sqrt(E)

    def body(q_ref, k_ref, v_ref, do_ref, dq_ref, dk_ref, dv_ref):
        i = pl.program_id(1)

        @pl.when(i == 0)
        def _():
            dk_ref[...] = jnp.zeros_like(dk_ref)
            dv_ref[...] = jnp.zeros_like(dv_ref)

        for h in range(MEM_HEADS):
            cs = slice(h * E, (h + 1) * E)
            qh = q_ref[:, cs].astype(MXU_DTYPE)
            kh = k_ref[:, cs].astype(MXU_DTYPE)
            vh = v_ref[:, cs].astype(MXU_DTYPE)
            doh = do_ref[:, cs].astype(MXU_DTYPE)
            s = lax.dot_general(qh, kh, (((1,), (1,)), ((), ())), preferred_element_type=F32) * scale
            e = jnp.exp(s - jnp.max(s, axis=1, keepdims=True))
            pr = e / jnp.sum(e, axis=1, keepdims=True)
            dp = lax.dot_general(doh, vh, (((1,), (1,)), ((), ())), preferred_element_type=F32)
            ds = (pr * (dp - jnp.sum(dp * pr, axis=1, keepdims=True))).astype(MXU_DTYPE)
            dq_ref[:, cs] = jnp.dot(ds, kh, preferred_element_type=F32) * scale
            dk_ref[:, cs] += lax.dot_general(ds, qh, (((0,), (0,)), ((), ())), preferred_element_type=F32) * scale
            dv_ref[:, cs] += lax.dot_general(pr.astype(MXU_DTYPE), doh, (((0,), (0,)), ((), ())),
                                             preferred_element_type=F32)

    qblk = pl.BlockSpec((tq, D), lambda b, i: (b * nq + i, 0))
    mblk = pl.BlockSpec((NM, D), lambda b, i: (b, 0))
    return _run(
        body, (qm, km, kv, do), out_shape=(_sds((T, D), F32), _sds((B * NM, D), F32), _sds((B * NM, D), F32)),
        grid=(B, nq), in_specs=[qblk, mblk, pl.BlockSpec((NM, D), lambda b, i: (b, 1)), qblk],
        out_specs=(qblk, mblk, mblk), name=name)


def _loss_head(y, tgt):
    T, D = y.shape
    tm = _tile(T, 512, 8)

    def body(y_ref, t_ref, dy_ref, l_ref):
        i = pl.program_id(0)
        err = y_ref[...] - t_ref[...]
        dy_ref[...] = err * (1.0 / D)
        part = jnp.sum(jnp.sum(err * err, axis=1, keepdims=True), axis=0, keepdims=True) * (0.5 / D)

        @pl.when(i == 0)
        def _():
            l_ref[...] = part

        @pl.when(i > 0)
        def _():
            l_ref[...] += part

    blk = pl.BlockSpec((tm, D), lambda i: (i, 0))
    return _run(body, (y, tgt), out_shape=(_sds((T, D), F32), _sds((1, 1), F32)), grid=(T // tm,),
                in_specs=[blk, blk], out_specs=(blk, pl.BlockSpec((1, 1), lambda i: (0, 0))), name="loss_head")


def _adamw(w, g, m, v):
    m = ADAM_B1 * m + (1.0 - ADAM_B1) * g
    v = ADAM_B2 * v + (1.0 - ADAM_B2) * jnp.square(g)
    m_hat = m / (1.0 - ADAM_B1 ** ADAM_STEP)
    v_hat = v / (1.0 - ADAM_B2 ** ADAM_STEP)
    delta = -ADAM_LR * (m_hat / (jnp.sqrt(v_hat) + ADAM_EPS) + ADAM_WD * w)
    return delta, m, v


def _sum_adamw(recv, w, m, v):
    _, R, C = recv.shape
    tr = _tile(R, 384, 8)

    def body(r_ref, w_ref, m_ref, v_ref, g_ref, d_ref, nm_ref, nv_ref):
        g = r_ref[0].astype(F32)
        for j in range(1, N_DEV):
            g = g + r_ref[j].astype(F32)
        g_ref[...] = g
        d_ref[...], nm_ref[...], nv_ref[...] = _adamw(w_ref[...], g, m_ref[...], v_ref[...])

    blk = pl.BlockSpec((tr, C), lambda i: (i, 0))
    return _run(body, (recv, w, m, v), out_shape=(_sds((R, C), F32),) * 4, grid=(R // tr,),
                in_specs=[pl.BlockSpec((N_DEV, tr, C), lambda i: (0, i, 0)), blk, blk, blk],
                out_specs=(blk,) * 4, name="grad_sum_adamw")


def _small_sum_adamw(gp, w, m, v):
    NR, C = gp.shape
    MESH = pl.DeviceIdType.MESH

    def body(gp_ref, w_ref, m_ref, v_ref, g_ref, d_ref, nm_ref, nv_ref, gath, send_sems, recv_sems):
        x, y, c = _mesh_pos()
        me = 4 * x + 2 * y + c
        gath[me] = gp_ref[...]
        copies = []
        for k in range(1, N_DEV):
            peer = (_flip(x, k & 4), _flip(y, k & 2), _flip(c, k & 1))
            cp = pltpu.make_async_remote_copy(src_ref=gp_ref, dst_ref=gath.at[me], send_sem=send_sems.at[k - 1],
                                              recv_sem=recv_sems.at[k - 1], device_id=peer, device_id_type=MESH)
            cp.start()
            copies.append(cp)
        for cp in copies:
            cp.wait()
        g = gath[0]
        for j in range(1, N_DEV):
            g = g + gath[j]
        g_ref[...] = g
        d_ref[...], nm_ref[...], nv_ref[...] = _adamw(w_ref[...], g, m_ref[...], v_ref[...])

    vm = pl.BlockSpec(memory_space=pltpu.VMEM)
    return _pcall(body, out_shape=(_sds((NR, C), F32),) * 4, in_specs=[vm] * 4, out_specs=(vm,) * 4,
                  scratch_shapes=[pltpu.VMEM((N_DEV, NR, C), F32), pltpu.SemaphoreType.DMA((7,)),
                                  pltpu.SemaphoreType.DMA((7,))], name="small_sum_adamw")(gp, w, m, v)


def _pack(shards, names, dtype=F32):
    return jnp.concatenate([shards[n].reshape(-1, FLAT_W) for n in names], axis=0).astype(dtype)


def _unpack(flat, names, shapes):
    out, off = {}, 0
    for n in names:
        rows = math.prod(shapes[n]) // FLAT_W
        out[n] = flat[off:off + rows].reshape(shapes[n])
        off += rows
    return out


def _gathered(gath, names, shapes):
    out, off = {}, 0
    for n in names:
        r, c = shapes[n]
        rows = r * c // FLAT_W
        blk = gath[:, off:off + rows].reshape(N_DEV, r, c)
        out[n] = blk.transpose(1, 0, 2).reshape(r, N_DEV * c) if n in COL_SHARDED else blk.reshape(N_DEV * r, c)
        off += rows
    return out


def _scatter_blocks(grads, names, shapes, dtype):
    parts = []
    for n in names:
        r, c = shapes[n]
        g = grads[n]
        g = g.reshape(r, N_DEV, c).transpose(1, 0, 2) if n in COL_SHARDED else g.reshape(N_DEV, r, c)
        parts.append(g.reshape(N_DEV, -1, FLAT_W).astype(dtype))
    return jnp.concatenate(parts, axis=1)


def _pack_small(d, prefix, loss=None):
    parts = []
    for n in SMALL:
        a = d[prefix + n].reshape(-1).astype(F32)
        pad = (-a.shape[0]) % LANES
        parts.append(jnp.pad(a, (0, pad)) if pad else a)
    tail = jnp.zeros((LANES,), F32)
    if loss is not None:
        tail = tail.at[0].set(loss)
    flat = jnp.concatenate(parts + [tail])
    pad = (-flat.shape[0]) % (8 * LANES)
    return jnp.pad(flat, (0, pad)).reshape(-1, LANES)


def _unpack_small(packed, shapes):
    flat = packed.reshape(-1)
    out, off = {}, 0
    for n in SMALL:
        size = math.prod(shapes[n])
        out[n] = flat[off:off + size].reshape(shapes[n])
        off += size + (-size) % LANES
    return out, flat[off]


def _row(a):
    return a.reshape(1, -1).astype(F32)


def _layer_fwd(x, memf, w_in, shards, next_in, shapes, sp, lb, tab, dims):
    B, S, NM = dims
    D = x.shape[1]
    W = {"w_in": w_in}
    h1 = _norm_fwd(x, sp["norm1_g"], E=D, W=D, out_dtype=MXU_DTYPE, name="norm1_fwd")
    z, gath = _matmul(h1, w_in, carry=("gather", _pack(shards, GATHER_MIX, MXU_DTYPE)), name="in_proj")
    W.update(_gathered(gath, GATHER_MIX, shapes))
    qn = _norm_fwd(z, sp["attn_qn_g"], E=ATT_HEAD_DIM, W=ATT_W, cb=0, out_dtype=MXU_DTYPE, name="qnorm_fwd")
    kn = _norm_fwd(z, sp["attn_kn_g"], E=ATT_HEAD_DIM, W=ATT_W, cb=1, out_dtype=MXU_DTYPE, name="knorm_fwd")
    (ya, lse), gath = _attn_fwd(tab, qn, kn, z, B=B, S=S, carry=("gather", _pack(shards, GATHER_FF, MXU_DTYPE)),
                                name="dilated_attn_fwd")
    W.update(_gathered(gath, GATHER_FF, shapes))
    o, states = _hgrn_fwd(z, lb, B=B, S=S, name="hgrn_fwd")
    yh = _hgout_fwd(o, z, sp["hg_onorm_g"], name="hgrn_out_fwd")
    ycat = jnp.concatenate([ya, yh], axis=1).astype(MXU_DTYPE)
    x1 = _matmul(ycat, W["w_out"], epi="add", extra=x, name="out_proj")
    h2 = _norm_fwd(x1, sp["norm2_g"], E=D, W=D, out_dtype=MXU_DTYPE, name="norm2_fwd")
    qmp = _matmul(h2, W["w_mq"], name="mq_proj")
    qm = _norm_fwd(qmp, sp["mq_norm_g"], E=D // MEM_HEADS, W=D, out_dtype=MXU_DTYPE, name="mqnorm_fwd")
    mn = _norm_fwd(memf, sp["mem_norm_g"], E=D, W=D, out_dtype=MXU_DTYPE, name="memnorm_fwd")
    kv = _matmul(mn, W["w_mkv"], name="mkv_proj")
    km = _norm_fwd(kv, sp["mk_norm_g"], E=D // MEM_HEADS, W=D, cb=0, out_dtype=MXU_DTYPE, name="mknorm_fwd")
    om = _memattn_fwd(qm, km, kv, B=B, S=S, NM=NM, name="mem_attn_fwd")
    x2 = _matmul(om, W["w_mo"], epi="add", extra=x1, name="mo_proj")
    h3 = _norm_fwd(x2, sp["norm3_g"], E=D, W=D, out_dtype=MXU_DTYPE, name="norm3_fwd")
    if next_in is None:
        a, w_in_next = _matmul(h3, W["w_ff1"], name="ff1"), None
    else:
        a, gath = _matmul(h3, W["w_ff1"], carry=("gather", _pack({"w_in": next_in}, GATHER_IN, MXU_DTYPE)), name="ff1")
        w_in_next = _gathered(gath, GATHER_IN, shapes)["w_in"]
    x3 = _matmul(a, W["w_ff2"], pro="relu2", epi="add", extra=x2, name="ff2")
    saved = dict(x=x, h1=h1, z=z, qn=qn, kn=kn, ya=ya, lse=lse, o=o, states=states, ycat=ycat, x1=x1, h2=h2,
                 qmp=qmp, qm=qm, mn=mn, kv=kv, km=km, om=om, x2=x2, h3=h3, a=a)
    return x3, saved, W, w_in_next


def _layer_bwd(dx3, s, memf, W, shapes, sp, lb, tab, dims):
    B, S, NM = dims
    D = dx3.shape[1]
    E_M = D // MEM_HEADS
    gw, gs, recv = {}, {}, {}

    def blocks(names):
        return ("scatter", _scatter_blocks(gw, names, shapes, MXU_DTYPE))

    da = _matmul(dx3, W["w_ff2"], tb=True, epi="relu2grad", extra=s["a"], out_dtype=MXU_DTYPE, name="ff2_dgrad")
    gw["w_ff2"] = _matmul(s["a"], dx3, ta=True, out_dtype=MXU_DTYPE, pro="relu2", name="ff2_wgrad")
    dh3, recv[SCATTER_A] = _matmul(da, W["w_ff1"], tb=True, carry=blocks(SCATTER_A), name="ff1_dgrad")
    gw["w_ff1"] = _matmul(s["h3"], da, ta=True, out_dtype=MXU_DTYPE, name="ff1_wgrad")
    dx2, gs["norm3_g"] = _norm_bwd(s["x2"], sp["norm3_g"], dh3, E=D, W=D, res=dx3, name="norm3_bwd")
    dom = _matmul(dx2, W["w_mo"], tb=True, name="mo_dgrad")
    gw["w_mo"] = _matmul(s["om"], dx2, ta=True, out_dtype=MXU_DTYPE, name="mo_wgrad")
    dqm, dkm, dvm = _memattn_bwd(s["qm"], s["km"], s["kv"], dom, B=B, S=S, NM=NM, name="mem_attn_bwd")
    dqmp, gs["mq_norm_g"] = _norm_bwd(s["qmp"], sp["mq_norm_g"], dqm, E=E_M, W=D, fold=E_M, out_dtype=MXU_DTYPE,
                                      name="mqnorm_bwd")
    dkmp, gs["mk_norm_g"] = _norm_bwd(s["kv"], sp["mk_norm_g"], dkm, E=E_M, W=D, cb=0, fold=E_M,
                                      out_dtype=MXU_DTYPE, name="mknorm_bwd")
    dkv = jnp.concatenate([dkmp, dvm.astype(MXU_DTYPE)], axis=1)
    dh2 = _matmul(dqmp, W["w_mq"], tb=True, name="mq_dgrad")
    gw["w_mq"] = _matmul(s["h2"], dqmp, ta=True, out_dtype=MXU_DTYPE, name="mq_wgrad")
    dmn = _matmul(dkv, W["w_mkv"], tb=True, name="mkv_dgrad")
    gw["w_mkv"] = _matmul(s["mn"], dkv, ta=True, out_dtype=MXU_DTYPE, name="mkv_wgrad")
    _, gs["mem_norm_g"] = _norm_bwd(memf, sp["mem_norm_g"], dmn, E=D, W=D, name="memnorm_bwd")
    dx1, gs["norm2_g"] = _norm_bwd(s["x1"], sp["norm2_g"], dh2, E=D, W=D, res=dx2, name="norm2_bwd")
    dycat = _matmul(dx1, W["w_out"], tb=True, name="out_dgrad")
    gw["w_out"] = _matmul(s["ycat"], dx1, ta=True, out_dtype=MXU_DTYPE, name="out_wgrad")
    do_hg, dgh, gs["hg_onorm_g"] = _hgout_bwd(s["o"], s["z"], sp["hg_onorm_g"], dycat, name="hgrn_out_bwd")
    dqh, dfh, dih, dlb = _hgrn_bwd(s["z"], lb, s["states"], do_hg, B=B, S=S, name="hgrn_bwd")
    (dqn, dkn, dv), recv[SCATTER_B] = _attn_bwd(tab, s["qn"], s["kn"], s["z"], s["ya"], s["lse"], dycat, B=B, S=S,
                                                 carry=blocks(SCATTER_B), name="dilated_attn_bwd")
    dq, gs["attn_qn_g"] = _norm_bwd(s["z"], sp["attn_qn_g"], dqn, E=ATT_HEAD_DIM, W=ATT_W, cb=0, fold=ATT_HEAD_DIM,
                                    out_dtype=MXU_DTYPE, name="qnorm_bwd")
    dk, gs["attn_kn_g"] = _norm_bwd(s["z"], sp["attn_kn_g"], dkn, E=ATT_HEAD_DIM, W=ATT_W, cb=1, fold=ATT_HEAD_DIM,
                                    out_dtype=MXU_DTYPE, name="knorm_bwd")
    dz = jnp.concatenate([dq, dk] + [t.astype(MXU_DTYPE) for t in (dv, dqh, dfh, dih, dgh)], axis=1)
    gw["w_in"] = _matmul(s["h1"], dz, ta=True, out_dtype=MXU_DTYPE, name="in_wgrad")
    dh1, recv[SCATTER_C] = _matmul(dz, W["w_in"], tb=True, carry=blocks(SCATTER_C), name="in_dgrad")
    dx0, gs["norm1_g"] = _norm_bwd(s["x"], sp["norm1_g"], dh1, E=D, W=D, res=dx1, name="norm1_bwd")
    gs["attn_qn_g"] = gs["attn_qn_g"][:, :ATT_HEAD_DIM]
    gs["attn_kn_g"] = gs["attn_kn_g"][:, :ATT_HEAD_DIM]
    return dx0, recv, gs, dlb


def kernel(x, mem, norm1_g, w_in, attn_qn_g, attn_kn_g, hg_lb, hg_onorm_g, w_out, norm2_g, mem_norm_g, w_mq, w_mkv, mq_norm_g, mk_norm_g, w_mo, norm3_g, w_ff1, w_ff2, loss_target, m_norm1_g, m_w_in, m_attn_qn_g, m_attn_kn_g, m_hg_lb, m_hg_onorm_g, m_w_out, m_norm2_g, m_mem_norm_g, m_w_mq, m_w_mkv, m_mq_norm_g, m_mk_norm_g, m_w_mo, m_norm3_g, m_w_ff1, m_w_ff2, v_norm1_g, v_w_in, v_attn_qn_g, v_attn_kn_g, v_hg_lb, v_hg_onorm_g, v_w_out, v_norm2_g, v_mem_norm_g, v_w_mq, v_w_mkv, v_mq_norm_g, v_mk_norm_g, v_w_mo, v_norm3_g, v_w_ff1, v_w_ff2):
    given = dict(norm1_g=norm1_g, w_in=w_in, attn_qn_g=attn_qn_g, attn_kn_g=attn_kn_g, hg_lb=hg_lb, hg_onorm_g=hg_onorm_g, w_out=w_out, norm2_g=norm2_g, mem_norm_g=mem_norm_g, w_mq=w_mq, w_mkv=w_mkv, mq_norm_g=mq_norm_g, mk_norm_g=mk_norm_g, w_mo=w_mo, norm3_g=norm3_g, w_ff1=w_ff1, w_ff2=w_ff2, m_norm1_g=m_norm1_g, m_w_in=m_w_in, m_attn_qn_g=m_attn_qn_g, m_attn_kn_g=m_attn_kn_g, m_hg_lb=m_hg_lb, m_hg_onorm_g=m_hg_onorm_g, m_w_out=m_w_out, m_norm2_g=m_norm2_g, m_mem_norm_g=m_mem_norm_g, m_w_mq=m_w_mq, m_w_mkv=m_w_mkv, m_mq_norm_g=m_mq_norm_g, m_mk_norm_g=m_mk_norm_g, m_w_mo=m_w_mo, m_norm3_g=m_norm3_g, m_w_ff1=m_w_ff1, m_w_ff2=m_w_ff2, v_norm1_g=v_norm1_g, v_w_in=v_w_in, v_attn_qn_g=v_attn_qn_g, v_attn_kn_g=v_attn_kn_g, v_hg_lb=v_hg_lb, v_hg_onorm_g=v_hg_onorm_g, v_w_out=v_w_out, v_norm2_g=v_norm2_g, v_mem_norm_g=v_mem_norm_g, v_w_mq=v_w_mq, v_w_mkv=v_w_mkv, v_mq_norm_g=v_mq_norm_g, v_mk_norm_g=v_mk_norm_g, v_w_mo=v_w_mo, v_norm3_g=v_norm3_g, v_w_ff1=v_w_ff1, v_w_ff2=v_w_ff2)
    B, S, D = x.shape
    NM = mem.shape[1]
    L = w_in.shape[0]
    dims = (B, S, NM)
    shapes = {n: given[n].shape[1:] for n in BIG}
    small_shapes = {n: given[n].shape for n in SMALL}

    def shards(prefix, l):
        return {n: given[prefix + n][l] for n in BIG}

    lower = _lower_bounds_fwd(hg_lb)
    tab = _attn_tables(S, _tile(S, ATT_Q_BLOCK, 8))
    xf = x.reshape(B * S, D)
    memf = mem.reshape(B * NM, D)

    def small_params(l):
        sp = {"norm1_g": _row(norm1_g[l]), "norm2_g": _row(norm2_g[l]), "norm3_g": _row(norm3_g[l]),
              "mem_norm_g": _row(mem_norm_g[l]), "hg_onorm_g": _row(hg_onorm_g[l]),
              "attn_qn_g": _row(jnp.tile(attn_qn_g[l], ATT_HEADS)), "attn_kn_g": _row(jnp.tile(attn_kn_g[l], ATT_HEADS)),
              "mq_norm_g": _row(jnp.tile(mq_norm_g[l], MEM_HEADS)), "mk_norm_g": _row(jnp.tile(mk_norm_g[l], MEM_HEADS))}
        return sp, _row(lower[l])

    w_in_full = _gathered(_all_gather(_pack(shards("", 0), GATHER_IN, MXU_DTYPE)), GATHER_IN, shapes)["w_in"]
    saved, weights = [], []
    h = xf
    for l in range(L):
        sp, lb = small_params(l)
        next_in = given["w_in"][l + 1] if l + 1 < L else None
        h, s, W, w_in_full = _layer_fwd(h, memf, w_in_full, shards("", l), next_in, shapes, sp, lb, tab, dims)
        saved.append(s)
        weights.append(W)

    dh, loss_part = _loss_head(h, loss_target.reshape(B * S, D))

    recv_layers, gs_layers, dlb_layers = [None] * L, [None] * L, [None] * L
    for l in range(L - 1, -1, -1):
        sp, lb = small_params(l)
        dh, recv_layers[l], gs_layers[l], dlb_layers[l] = _layer_bwd(dh, saved[l], memf, weights[l], shapes, sp, lb,
                                                                      tab, dims)
    grad_x = dh.reshape(B, S, D)

    big_out = [{n: [None] * L for n in BIG} for _ in range(4)]
    for l in range(L):
        for names in SCATTERS:
            flats = _sum_adamw(recv_layers[l][names], _pack(shards("", l), names), _pack(shards("m_", l), names),
                               _pack(shards("v_", l), names))
            for k in range(4):
                for n, t in _unpack(flats[k], names, shapes).items():
                    big_out[k][n][l] = t
    big_out = [{n: jnp.stack(d[n]) for n in BIG} for d in big_out]

    gs = {n: jnp.stack([gs_layers[l][n].reshape(small_shapes[n][1:]) for l in range(L)]) for n in SMALL if n != "hg_lb"}
    gs["hg_lb"] = _lower_bounds_bwd(hg_lb, jnp.concatenate(dlb_layers, axis=0))
    packed = _small_sum_adamw(_pack_small(gs, "", loss_part[0, 0]), _pack_small(given, ""), _pack_small(given, "m_"),
                              _pack_small(given, "v_"))
    small_out, loss = [], None
    for t in packed:
        d, tail = _unpack_small(t, small_shapes)
        small_out.append(d)
        loss = tail if loss is None else loss

    outs = [loss, grad_x]
    for k in range(4):
        outs += [small_out[k][n] if n in small_shapes else big_out[k][n] for n in WEIGHTS]
    return tuple(outs)
```

```python
import functools
import math

import jax
import jax.numpy as jnp
from jax import lax
from jax.experimental import pallas as pl
from jax.experimental.pallas import tpu as pltpu

F32 = jnp.float32
MXU_DTYPE = jnp.bfloat16
VMEM_LIMIT = 48 * 1024 * 1024

N_DEV = 8
EPS = 1e-6
NEG = -1e30
F_MIN = 1e-12
ATT_HEADS = 8
ATT_HEAD_DIM = 64
ATT_W = ATT_HEADS * ATT_HEAD_DIM
DILATIONS = (1, 4, 16)
DIL_STEPS = 128
ATT_Q_BLOCK = 256
ATT_KEY_EXTENT = 512
HG_HEADS = 4
HG_HEAD_DIM = 128
HG_W = HG_HEADS * HG_HEAD_DIM
HG_CHUNK = 32
HG_SUB = 8
MEM_HEADS = 4
LANES = 128
FLAT_W = 1024

ADAM_LR = 0.001
ADAM_B1 = 0.9
ADAM_B2 = 0.999
ADAM_EPS = 1e-08
ADAM_WD = 0.01
ADAM_STEP = 10

BIG = ("w_in", "w_out", "w_mq", "w_mkv", "w_mo", "w_ff1", "w_ff2")
COL_SHARDED = ("w_in", "w_mkv", "w_ff1")
SMALL = ("norm1_g", "attn_qn_g", "attn_kn_g", "hg_lb", "hg_onorm_g", "norm2_g", "mem_norm_g",
         "mq_norm_g", "mk_norm_g", "norm3_g")
WEIGHTS = ("norm1_g", "w_in", "attn_qn_g", "attn_kn_g", "hg_lb", "hg_onorm_g", "w_out", "norm2_g",
           "mem_norm_g", "w_mq", "w_mkv", "mq_norm_g", "mk_norm_g", "w_mo", "norm3_g", "w_ff1", "w_ff2")
GATHER_IN = ("w_in",)
GATHER_MIX = ("w_out", "w_mq", "w_mkv", "w_mo")
GATHER_FF = ("w_ff1", "w_ff2")
SCATTER_A = ("w_ff2",)
SCATTER_B = ("w_ff1", "w_mo", "w_mq", "w_mkv", "w_out")
SCATTER_C = ("w_in",)
SCATTERS = (SCATTER_A, SCATTER_B, SCATTER_C)


def _pcall(body, **kw):
    return pl.pallas_call(body, **kw)


def _params():
    return pltpu.CompilerParams(vmem_limit_bytes=VMEM_LIMIT)


def _tile(n, pref, mult):
    t = (min(n, pref) // mult) * mult
    while t >= mult:
        if n % t == 0:
            return t
        t -= mult
    return n


def _sds(shape, dtype):
    return jax.ShapeDtypeStruct(shape, dtype)


def _mesh_pos():
    return lax.axis_index("x"), lax.axis_index("y"), lax.axis_index("c")


def _flip(v, bit):
    return 1 - v if bit else v


def _gather_hooks(x_ref, out_ref, send_sems, recv_sems, local_sem):
    MESH = pl.DeviceIdType.MESH
    x, y, c = _mesh_pos()
    me, sibling = (x, y, c), (x, y, 1 - c)
    chips = [(1 - x, y), (x, 1 - y), (1 - x, 1 - y)]

    def rows(px, py, pc):
        return out_ref.at[4 * px + 2 * py + pc]

    def copy(k, block, to, src=None):
        return pltpu.make_async_remote_copy(
            src_ref=rows(*block) if src is None else src, dst_ref=rows(*block), send_sem=send_sems.at[k],
            recv_sem=recv_sems.at[k], device_id=to, device_id_type=MESH)

    mine = pltpu.make_async_copy(x_ref, rows(*me), local_sem)
    first = [copy(0, me, sibling, src=x_ref)]
    first += [copy(1 + j, me, (*chip, c), src=x_ref) for j, chip in enumerate(chips)]
    passed = [copy(4 + j, (*chip, c), sibling) for j, chip in enumerate(chips)]

    def start():
        mine.start()
        for cp in first:
            cp.start()

    def forward():
        for j, chip in enumerate(chips):
            copy(1 + j, (*chip, c), me).wait_recv()
            passed[j].start()

    def finish():
        copy(0, sibling, me).wait_recv()
        for j, chip in enumerate(chips):
            copy(4 + j, (*chip, 1 - c), me).wait_recv()
        for cp in first + passed:
            cp.wait_send()
        mine.wait()

    return start, forward, finish


def _scatter_hooks(s_ref, r_ref, send_sems, recv_sems, local_sem):
    MESH = pl.DeviceIdType.MESH
    x, y, c = _mesh_pos()
    me = 4 * x + 2 * y + c
    mine = pltpu.make_async_copy(s_ref.at[me], r_ref.at[me], local_sem)
    copies = []
    for m in range(1, N_DEV):
        px, py, pc = _flip(x, m & 4), _flip(y, m & 2), _flip(c, m & 1)
        copies.append(pltpu.make_async_remote_copy(
            src_ref=s_ref.at[4 * px + 2 * py + pc], dst_ref=r_ref.at[me], send_sem=send_sems.at[m - 1],
            recv_sem=recv_sems.at[m - 1], device_id=(px, py, pc), device_id_type=MESH))

    def start():
        mine.start()
        for cp in copies:
            cp.start()

    def finish():
        for cp in copies:
            cp.wait()
        mine.wait()

    return start, None, finish


_EXCHANGE_SEMS = [pltpu.SemaphoreType.DMA((N_DEV - 1,)), pltpu.SemaphoreType.DMA((N_DEV - 1,)),
                  pltpu.SemaphoreType.DMA]


def _exchange_out(kind, xs):
    return _sds((N_DEV,) + xs.shape, xs.dtype) if kind == "gather" else _sds(xs.shape, xs.dtype)


def _run(body, args, *, out_shape, grid, in_specs, out_specs, scratch_shapes=(), name, carry=None):
    single = not isinstance(out_shape, (tuple, list))
    outs = (out_shape,) if single else tuple(out_shape)
    ospecs = (out_specs,) if single else tuple(out_specs)
    if carry is None:
        res = _pcall(body, out_shape=outs, grid=grid, in_specs=list(in_specs), out_specs=ospecs,
                     scratch_shapes=list(scratch_shapes), compiler_params=_params(), name=name)(*args)
        return res[0] if single else tuple(res)
    kind, xs = carry
    n_in, n_out, n_scr = len(args), len(outs), len(scratch_shapes)
    total = math.prod(grid)

    def wrapped(*refs):
        ins, x_ref = refs[:n_in], refs[n_in]
        o, xo_ref = refs[n_in + 1:n_in + 1 + n_out], refs[n_in + 1 + n_out]
        scr = refs[n_in + 2 + n_out:n_in + 2 + n_out + n_scr]
        sems = refs[n_in + 2 + n_out + n_scr:]
        step = pl.program_id(0)
        for ax in range(1, len(grid)):
            step = step * grid[ax] + pl.program_id(ax)
        start, forward, finish = (_gather_hooks if kind == "gather" else _scatter_hooks)(x_ref, xo_ref, *sems)
        pl.when(step == 0)(start)
        body(*ins, *o, *scr)
        if forward is not None:
            pl.when(step == (3 * total) // 4)(forward)
        pl.when(step == total - 1)(finish)

    hbm = pl.BlockSpec(memory_space=pl.ANY)
    res = _pcall(wrapped, out_shape=outs + (_exchange_out(kind, xs),), grid=grid, in_specs=list(in_specs) + [hbm],
                 out_specs=ospecs + (hbm,), scratch_shapes=list(scratch_shapes) + _EXCHANGE_SEMS,
                 compiler_params=_params(), name=name)(*args, xs)
    main = tuple(res[:-1])
    return (main[0] if single else main), res[-1]


def _all_gather(xs):
    def body(x_ref, out_ref, send_sems, recv_sems, local_sem):
        start, forward, finish = _gather_hooks(x_ref, out_ref, send_sems, recv_sems, local_sem)
        start()
        forward()
        finish()

    hbm = pl.BlockSpec(memory_space=pl.ANY)
    return _pcall(body, out_shape=_exchange_out("gather", xs), in_specs=[hbm], out_specs=hbm,
                  scratch_shapes=_EXCHANGE_SEMS, name="weights_all_gather")(xs)


def _matmul(a, b, *, ta=False, tb=False, pro=None, epi=None, extra=None, out_dtype=F32, carry=None, name):
    M, K = (a.shape[1], a.shape[0]) if ta else a.shape
    N = b.shape[0] if tb else b.shape[1]
    tm = _tile(M, 512, 8 if not ta else LANES)
    tn = _tile(N, 1024, LANES)
    tk = _tile(K, 512 if ta else 1024, LANES if not ta else 8)
    nk = K // tk
    dims = (((0 if ta else 1,), (1 if tb else 0,)), ((), ()))

    def body(a_ref, b_ref, *rest):
        if extra is not None:
            e_ref, o_ref, acc_ref = rest
        else:
            o_ref, acc_ref = rest
        k = pl.program_id(2)

        @pl.when(k == 0)
        def _():
            acc_ref[...] = jnp.zeros_like(acc_ref)

        av = a_ref[...]
        if pro == "relu2":
            av = jnp.square(jnp.maximum(av, 0.0))
        acc_ref[...] += lax.dot_general(av.astype(MXU_DTYPE), b_ref[...].astype(MXU_DTYPE), dims,
                                        preferred_element_type=F32)

        @pl.when(k == nk - 1)
        def _():
            r = acc_ref[...]
            if epi == "add":
                r = r + e_ref[...]
            elif epi == "relu2grad":
                r = r * (2.0 * jnp.maximum(e_ref[...], 0.0))
            o_ref[...] = r.astype(out_dtype)

    a_spec = pl.BlockSpec((tk, tm), lambda i, j, k: (k, i)) if ta else pl.BlockSpec((tm, tk), lambda i, j, k: (i, k))
    b_spec = pl.BlockSpec((tn, tk), lambda i, j, k: (j, k)) if tb else pl.BlockSpec((tk, tn), lambda i, j, k: (k, j))
    o_spec = pl.BlockSpec((tm, tn), lambda i, j, k: (i, j))
    in_specs = [a_spec, b_spec] + ([o_spec] if extra is not None else [])
    args = (a, b) + ((extra,) if extra is not None else ())
    return _run(body, args, out_shape=_sds((M, N), out_dtype), grid=(M // tm, N // tn, nk), in_specs=in_specs,
                out_specs=o_spec, scratch_shapes=[pltpu.VMEM((tm, tn), F32)], name=name, carry=carry)


def _group_mean(v, E):
    rows, W = v.shape
    if E == W:
        return jnp.mean(v, axis=-1, keepdims=True)
    pieces = []
    if E % LANES == 0:
        for g0 in range(0, W, E):
            m = jnp.mean(v[:, g0:g0 + E], axis=-1, keepdims=True)
            pieces.append(jnp.broadcast_to(m, (rows, E)))
    else:
        lane = lax.broadcasted_iota(jnp.int32, (rows, LANES), 1)
        for c0 in range(0, W, LANES):
            vc = v[:, c0:c0 + LANES]
            acc = jnp.zeros((rows, LANES), F32)
            for s0 in range(0, LANES, E):
                msk = (lane >= s0) & (lane < s0 + E)
                m = jnp.sum(jnp.where(msk, vc, 0.0), axis=-1, keepdims=True) * (1.0 / E)
                acc = jnp.where(msk, m, acc)
            pieces.append(acc)
    return jnp.concatenate(pieces, axis=-1)


def _fold_groups(t, E):
    W = t.shape[1]
    step = max(E, LANES)
    acc = t[:, 0:step]
    for c0 in range(step, W, step):
        acc = acc + t[:, c0:c0 + step]
    sh = LANES // 2
    while sh >= E:
        acc = acc + pltpu.roll(acc, sh, 1)
        sh //= 2
    return acc


def _norm_fwd(x, g, *, E, W, cb=0, out_dtype, name):
    M = x.shape[0]
    tm = _tile(M, 512, 8)

    def body(x_ref, g_ref, o_ref):
        xv = x_ref[...]
        r = lax.rsqrt(_group_mean(xv * xv, E) + EPS)
        o_ref[...] = (xv * r * g_ref[...]).astype(out_dtype)

    return _run(body, (x, g), out_shape=_sds((M, W), out_dtype), grid=(M // tm,),
                in_specs=[pl.BlockSpec((tm, W), lambda i: (i, cb)), pl.BlockSpec((1, W), lambda i: (0, 0))],
                out_specs=pl.BlockSpec((tm, W), lambda i: (i, 0)), name=name)


def _norm_bwd(x, g, dy, *, E, W, cb=0, res=None, fold=None, out_dtype=F32, name):
    M = x.shape[0]
    tm = _tile(M, 512, 8)
    n = M // tm
    gw = W if fold is None else max(fold, LANES)

    def body(x_ref, g_ref, dy_ref, *rest):
        if res is not None:
            r_ref, dx_ref, dg_ref, acc_ref = rest
        else:
            dx_ref, dg_ref, acc_ref = rest
        i = pl.program_id(0)
        xv = x_ref[...]
        r = lax.rsqrt(_group_mean(xv * xv, E) + EPS)
        xh = xv * r
        dyv = dy_ref[...].astype(F32)
        dyg = dyv * g_ref[...]
        dx = r * (dyg - xh * _group_mean(dyg * xh, E))
        if res is not None:
            dx = dx + r_ref[...]
        dx_ref[...] = dx.astype(out_dtype)
        part = jnp.sum(dyv * xh, axis=0, keepdims=True)

        @pl.when(i == 0)
        def _():
            acc_ref[...] = part

        @pl.when(i > 0)
        def _():
            acc_ref[...] += part

        @pl.when(i == n - 1)
        def _():
            t = acc_ref[...]
            dg_ref[...] = t if fold is None else _fold_groups(t, fold)

    blk = pl.BlockSpec((tm, W), lambda i: (i, 0))
    in_specs = [pl.BlockSpec((tm, W), lambda i: (i, cb)), pl.BlockSpec((1, W), lambda i: (0, 0)), blk]
    args = [x, g, dy]
    if res is not None:
        in_specs.append(blk)
        args.append(res)
    return _run(body, tuple(args), out_shape=(_sds((M, W), out_dtype), _sds((1, gw), F32)), grid=(n,),
                in_specs=in_specs, out_specs=(blk, pl.BlockSpec((1, gw), lambda i: (0, 0))),
                scratch_shapes=[pltpu.VMEM((1, W), F32)], name=name)


def _attn_tables(S, tq):
    nq = S // tq

    def body(o_ref):
        i = pl.program_id(0)
        d = (i * tq + lax.broadcasted_iota(jnp.int32, (tq, S), 0)) - lax.broadcasted_iota(jnp.int32, (tq, S), 1)
        cnt = jnp.zeros((tq, S), jnp.int32)
        for dil in DILATIONS:
            hit = (d <= DIL_STEPS * dil) if dil == 1 else (((d & (dil - 1)) == 0) & (d <= DIL_STEPS * dil))
            cnt = cnt + hit.astype(jnp.int32)
        ok = (d >= 0) & (cnt > 0)
        logm = jnp.where(cnt == 3, math.log(3.0), jnp.where(cnt == 2, math.log(2.0), 0.0))
        o_ref[0] = jnp.where(ok, logm, NEG).astype(F32)

    return _run(body, (), out_shape=_sds((nq, tq, S), F32), grid=(nq,), in_specs=[],
                out_specs=pl.BlockSpec((1, tq, S), lambda i: (i, 0, 0)), name="dilated_attn_tables")


def _alibi_slope(h):
    return 2.0 ** (-8.0 / ATT_HEADS * (h + 1))


def _key_positions(Sk):
    kpos = lax.broadcasted_iota(jnp.int32, (Sk, LANES), 0)
    return (kpos >> 8).astype(F32).astype(MXU_DTYPE), (kpos & 255).astype(F32).astype(MXU_DTYPE)


def _score_operands(q2s, k2, pos_hi, pos_lo, lane, sub, h):
    own = (lane < ATT_HEAD_DIM) if sub == 0 else (lane >= ATT_HEAD_DIM)
    spare = ATT_HEAD_DIM if sub == 0 else 0
    slope = _alibi_slope(h)
    terms = jnp.where(lane == spare, slope * 256.0, jnp.where(lane == spare + 1, slope, 0.0)).astype(q2s.dtype)
    qa = jnp.where(own, q2s, terms)
    ka = jnp.where(lane == spare, pos_hi, jnp.where(lane == spare + 1, pos_lo, k2))
    return own, qa, ka


def _key_extents(S, tq):
    ext = min(ATT_KEY_EXTENT, S)
    return ext, ext // tq, S // ext


def _attn_fwd(tab, qn, kn, z, *, B, S, carry=None, name):
    T = B * S
    tq = tab.shape[1]
    nq = S // tq
    ext, per, n_ext = _key_extents(S, tq)
    scale = 1.0 / math.sqrt(ATT_HEAD_DIM)
    dn_t = (((1,), (1,)), ((), ()))

    def body(tab_ref, q_ref, k_ref, v_ref, y_ref, lse_ref):
        i = pl.program_id(1)
        lane = lax.broadcasted_iota(jnp.int32, (1, LANES), 1)
        lo = lane < ATT_HEAD_DIM

        def visit(Sk):
            base = tab_ref[0, :, :Sk]
            pos_hi, pos_lo = _key_positions(Sk)
            lse_blk = jnp.zeros((tq, LANES), F32)
            for p in range(ATT_HEADS // 2):
                cs = slice(p * LANES, (p + 1) * LANES)
                q2s = q_ref[:, cs].astype(MXU_DTYPE) * scale
                k2 = k_ref[:Sk, cs].astype(MXU_DTYPE)
                v2 = v_ref[:Sk, cs].astype(MXU_DTYPE)
                outs = []
                for sub in range(2):
                    h = 2 * p + sub
                    _, qa, ka = _score_operands(q2s, k2, pos_hi, pos_lo, lane, sub, h)
                    s = lax.dot_general(qa, ka, dn_t, preferred_element_type=F32) + base
                    mx = jnp.max(s, axis=1, keepdims=True)
                    e = jnp.exp(s - mx)
                    l = jnp.sum(e, axis=1, keepdims=True)
                    outs.append(jnp.dot(e.astype(MXU_DTYPE), v2, preferred_element_type=F32) / l)
                    lse_blk = jnp.where(lane == h, mx + jnp.log(l), lse_blk)
                y_ref[:, cs] = jnp.where(lo, outs[0], outs[1])
            lse_ref[...] = lse_blk

        for e in range(n_ext):
            pl.when(i // per == e)(functools.partial(visit, (e + 1) * ext))

    return _run(
        body, (tab, qn, kn, z), out_shape=(_sds((T, ATT_W), F32), _sds((T, LANES), F32)), grid=(B, nq),
        in_specs=[pl.BlockSpec((1, tq, S), lambda b, i: (i, 0, 0)),
                  pl.BlockSpec((tq, ATT_W), lambda b, i: (b * nq + i, 0)),
                  pl.BlockSpec((S, ATT_W), lambda b, i: (b, 0)),
                  pl.BlockSpec((S, ATT_W), lambda b, i: (b, 2))],
        out_specs=(pl.BlockSpec((tq, ATT_W), lambda b, i: (b * nq + i, 0)),
                   pl.BlockSpec((tq, LANES), lambda b, i: (b * nq + i, 0))),
        name=name, carry=carry)


def _attn_bwd(tab, qn, kn, z, y, lse, dycat, *, B, S, carry=None, name):
    T = B * S
    tq = tab.shape[1]
    nq = S // tq
    ext, per, n_ext = _key_extents(S, tq)
    scale = 1.0 / math.sqrt(ATT_HEAD_DIM)
    dn_t = (((1,), (1,)), ((), ()))
    dn_o = (((0,), (0,)), ((), ()))

    def body(tab_ref, q_ref, k_ref, v_ref, y_ref, lse_ref, dy_ref, dq_ref, dk_ref, dv_ref):
        i = pl.program_id(1)

        @pl.when(i == 0)
        def _():
            dk_ref[...] = jnp.zeros_like(dk_ref)
            dv_ref[...] = jnp.zeros_like(dv_ref)

        lane = lax.broadcasted_iota(jnp.int32, (1, LANES), 1)
        lo = lane < ATT_HEAD_DIM

        def visit(Sk):
            base = tab_ref[0, :, :Sk]
            pos_hi, pos_lo = _key_positions(Sk)
            lse_blk = lse_ref[...]
            for p in range(ATT_HEADS // 2):
                cs = slice(p * LANES, (p + 1) * LANES)
                q2 = q_ref[:, cs].astype(MXU_DTYPE)
                q2s = q2 * scale
                k2 = k_ref[:Sk, cs].astype(MXU_DTYPE)
                v2 = v_ref[:Sk, cs].astype(MXU_DTYPE)
                do2 = dy_ref[:, cs]
                doy = do2 * y_ref[:, cs]
                do2m = do2.astype(MXU_DTYPE)
                dqs, dks, dvs = [], [], []
                for sub in range(2):
                    h = 2 * p + sub
                    own, qa, ka = _score_operands(q2s, k2, pos_hi, pos_lo, lane, sub, h)
                    s = lax.dot_general(qa, ka, dn_t, preferred_element_type=F32) + base
                    lse_h = jnp.sum(jnp.where(lane == h, lse_blk, 0.0), axis=1, keepdims=True)
                    pr = jnp.exp(s - lse_h)
                    dsum = jnp.sum(jnp.where(own, doy, 0.0), axis=1, keepdims=True)
                    dom = jnp.where(own, do2m, jnp.zeros_like(do2m))
                    dp = lax.dot_general(dom, v2, dn_t, preferred_element_type=F32)
                    ds = (pr * (dp - dsum)).astype(MXU_DTYPE)
                    dqs.append(jnp.dot(ds, k2, preferred_element_type=F32))
                    dks.append(lax.dot_general(ds, q2, dn_o, preferred_element_type=F32))
                    dvs.append(lax.dot_general(pr.astype(MXU_DTYPE), do2m, dn_o, preferred_element_type=F32))
                dq_ref[:, cs] = jnp.where(lo, dqs[0], dqs[1]) * scale
                dk_ref[:Sk, cs] += jnp.where(lo, dks[0], dks[1]) * scale
                dv_ref[:Sk, cs] += jnp.where(lo, dvs[0], dvs[1])

        for e in range(n_ext):
            pl.when(i // per == e)(functools.partial(visit, (e + 1) * ext))

    qblk = pl.BlockSpec((tq, ATT_W), lambda b, i: (b * nq + i, 0))
    sblk = pl.BlockSpec((S, ATT_W), lambda b, i: (b, 0))
    return _run(
        body, (tab, qn, kn, z, y, lse, dycat), out_shape=(_sds((T, ATT_W), F32),) * 3, grid=(B, nq),
        in_specs=[pl.BlockSpec((1, tq, S), lambda b, i: (i, 0, 0)), qblk, sblk,
                  pl.BlockSpec((S, ATT_W), lambda b, i: (b, 2)), qblk,
                  pl.BlockSpec((tq, LANES), lambda b, i: (b * nq + i, 0)), qblk],
        out_specs=(qblk, sblk, sblk), name=name, carry=carry)


def _sigmoid_pair(x):
    en = jnp.exp(-jnp.abs(x))
    big = 1.0 / (1.0 + en)
    small = en * big
    pos = x >= 0
    return jnp.where(pos, big, small), jnp.where(pos, small, big)


def _chunk_scan(x, pos, reverse):
    n = x.shape[0]
    sh = 1
    while sh < HG_CHUNK:
        if reverse:
            x = x + jnp.where(pos < HG_CHUNK - sh, pltpu.roll(x, n - sh, 0), 0.0)
        else:
            x = x + jnp.where(pos >= sh, pltpu.roll(x, sh, 0), 0.0)
        sh *= 2
    return x


def _hgrn_pair_table(tb):
    def body(o_ref):
        t = lax.broadcasted_iota(jnp.int32, (tb, tb), 0)
        s = lax.broadcasted_iota(jnp.int32, (tb, tb), 1)
        o_ref[...] = jnp.where((t // HG_CHUNK) == (s // HG_CHUNK), (t // HG_SUB) - (s // HG_SUB), 0)

    return _pcall(body, out_shape=_sds((tb, tb), jnp.int32), name="hgrn_pair_table")()


def _hgrn_cross_decays(b, pos):
    n = b.shape[0]
    pos8 = pos % HG_SUB
    end = b
    sh = 1
    while sh < HG_SUB:
        end = jnp.where((pos8 & sh) == 0, pltpu.roll(end, n - sh, 0), end)
        sh *= 2
    x = jnp.where(pos >= HG_SUB, jnp.exp(jnp.minimum(b - pltpu.roll(end, HG_SUB, 0), 0.0)), 0.0)
    ys = []
    for j in range(1, HG_CHUNK // HG_SUB):
        end_j = end if j == 1 else pltpu.roll(end, n - HG_SUB * (j - 1), 0)
        ys.append(jnp.where(pos < HG_CHUNK - HG_SUB * j, jnp.exp(jnp.minimum(end_j - b, 0.0)), 0.0))
    return x, ys


def _hgrn_cross_scores(qt, kts, pair):
    n = qt.shape[0]
    kcat = jnp.concatenate([k.astype(MXU_DTYPE) for k in kts], axis=0)
    p = lax.dot_general(qt.astype(MXU_DTYPE), kcat, (((1,), (1,)), ((), ())), preferred_element_type=F32)
    a = jnp.zeros((n, n), F32)
    for j in range(1, len(kts) + 1):
        a = jnp.where(pair == j, p[:, (j - 1) * n:j * n], a)
    return a, kcat


def _hgrn_gates(qh, fh, lb):
    sq, _ = _sigmoid_pair(qh)
    sg, sgn = _sigmoid_pair(fh)
    f = lb + (1.0 - lb) * sg
    g = jnp.log(jnp.maximum(f, F_MIN))
    kk = (1.0 - lb) * sgn
    return qh * sq, sq, sg, sgn, f, g, kk


def _hgrn_fwd(z, lb, pair, *, B, S, name):
    T = B * S
    tb = _tile(S, 256, HG_CHUNK)
    nc = tb // HG_CHUNK
    nb = S // tb
    dn_t = (((1,), (1,)), ((), ()))
    dn_o = (((0,), (0,)), ((), ()))

    def zcol(first):
        return lambda h, b, i: (b * nb + i, first + h)

    def body(q_ref, f_ref, v_ref, lb_ref, pair_ref, o_ref, st_ref, state):
        i = pl.program_id(2)

        @pl.when(i == 0)
        def _():
            state[...] = jnp.zeros_like(state)

        q, _, _, _, _, g, kk = _hgrn_gates(q_ref[...], f_ref[...], lb_ref[...])
        v = v_ref[...]
        pos = lax.broadcasted_iota(jnp.int32, (tb, LANES), 0) % HG_CHUNK
        pos8 = pos % HG_SUB
        b = _chunk_scan(g, pos, False)
        o = jnp.sum(q * kk, axis=1, keepdims=True) * v
        for d in range(1, HG_SUB):
            qd = pltpu.roll(q, tb - d, 0)
            bd = pltpu.roll(b, tb - d, 0)
            e = jnp.where(pos8 < HG_SUB - d, jnp.exp(jnp.minimum(bd - b, 0.0)), 0.0)
            a = jnp.sum(qd * kk * e, axis=1, keepdims=True)
            o = o + pltpu.roll(a * v, d, 0)
        x, ys = _hgrn_cross_decays(b, pos)
        a_cross, _ = _hgrn_cross_scores(q * x, [kk * y for y in ys], pair_ref[...])
        o = o + jnp.dot(a_cross.astype(MXU_DTYPE), v.astype(MXU_DTYPE), preferred_element_type=F32)
        qe = q * jnp.exp(b)
        chunks = [slice(c * HG_CHUNK, (c + 1) * HG_CHUNK) for c in range(nc)]
        b_last = [jnp.sum(jnp.where(pos[rs] == HG_CHUNK - 1, b[rs], 0.0), axis=0, keepdims=True) for rs in chunks]
        own = [lax.dot_general(v[rs].astype(MXU_DTYPE), (kk[rs] * jnp.exp(bl - b[rs])).astype(MXU_DTYPE), dn_o,
                               preferred_element_type=F32) for rs, bl in zip(chunks, b_last)]
        st = state[...]
        before = []
        for c in range(nc):
            before.append(st)
            st = st * jnp.exp(b_last[c]) + own[c]
        state[...] = st
        for c, rs in enumerate(chunks):
            st_ref[c * LANES:(c + 1) * LANES, :] = before[c]
            o_ref[rs, :] = o[rs] + lax.dot_general(qe[rs].astype(MXU_DTYPE), before[c].astype(MXU_DTYPE), dn_t,
                                                   preferred_element_type=F32)

    blk = (tb, LANES)
    return _run(
        body, (z, z, z, lb, pair), out_shape=(_sds((T, HG_W), F32), _sds((T // HG_CHUNK * LANES, HG_W), F32)),
        grid=(HG_HEADS, B, nb),
        in_specs=[pl.BlockSpec(blk, zcol(12)), pl.BlockSpec(blk, zcol(16)), pl.BlockSpec(blk, zcol(20)),
                  pl.BlockSpec((1, LANES), lambda h, b, i: (0, h)), pl.BlockSpec((tb, tb), lambda h, b, i: (0, 0))],
        out_specs=(pl.BlockSpec(blk, lambda h, b, i: (b * nb + i, h)),
                   pl.BlockSpec((nc * LANES, LANES), lambda h, b, i: (b * nb + i, h))),
        scratch_shapes=[pltpu.VMEM((LANES, LANES), F32)], name=name)


def _hgrn_bwd(z, lb, pair, states, do, *, B, S, name):
    T = B * S
    tb = _tile(S, 256, HG_CHUNK)
    nc = tb // HG_CHUNK
    nb = S // tb
    dn_t = (((1,), (1,)), ((), ()))
    dn_o = (((0,), (0,)), ((), ()))

    def body(q_ref, f_ref, v_ref, lb_ref, pair_ref, st_ref, do_ref, dq_ref, df_ref, di_ref, dlb_ref, dstate):
        bi = pl.program_id(1)
        i = pl.program_id(2)

        @pl.when(i == 0)
        def _():
            dstate[...] = jnp.zeros_like(dstate)

        qh = q_ref[...]
        lbv = lb_ref[...]
        q, sq, sg, sgn, f, g, kk = _hgrn_gates(qh, f_ref[...], lbv)
        v = v_ref[...]
        dov = do_ref[...]
        pos = lax.broadcasted_iota(jnp.int32, (tb, LANES), 0) % HG_CHUNK
        pos8 = pos % HG_SUB
        b = _chunk_scan(g, pos, False)

        a0 = jnp.sum(q * kk, axis=1, keepdims=True)
        da0 = jnp.sum(dov * v, axis=1, keepdims=True)
        dv = a0 * dov
        dkk = da0 * q
        dq = da0 * kk
        db = jnp.zeros((tb, LANES), F32)
        for d in range(1, HG_SUB):
            qd = pltpu.roll(q, tb - d, 0)
            bd = pltpu.roll(b, tb - d, 0)
            dod = pltpu.roll(dov, tb - d, 0)
            e = jnp.where(pos8 < HG_SUB - d, jnp.exp(jnp.minimum(bd - b, 0.0)), 0.0)
            a = jnp.sum(qd * kk * e, axis=1, keepdims=True)
            da = jnp.sum(dod * v, axis=1, keepdims=True)
            dv = dv + a * dod
            xe = da * e
            dkk = dkk + xe * qd
            t1 = xe * kk
            wt = t1 * qd
            dq = dq + pltpu.roll(t1, d, 0)
            db = db + pltpu.roll(wt, d, 0) - wt

        n_cross = HG_CHUNK // HG_SUB - 1
        pair = pair_ref[...]
        x, ys = _hgrn_cross_decays(b, pos)
        qt = q * x
        kts = [kk * y for y in ys]
        a_cross, kcat = _hgrn_cross_scores(qt, kts, pair)
        dom = dov.astype(MXU_DTYPE)
        da_all = lax.dot_general(dom, v.astype(MXU_DTYPE), dn_t, preferred_element_type=F32)
        dv = dv + lax.dot_general(a_cross.astype(MXU_DTYPE), dom, dn_o, preferred_element_type=F32)
        gcat = jnp.concatenate([jnp.where(pair == j, da_all, 0.0).astype(MXU_DTYPE) for j in range(1, n_cross + 1)],
                               axis=1)
        dqt = jnp.dot(gcat, kcat, preferred_element_type=F32)
        dkcat = lax.dot_general(gcat, qt.astype(MXU_DTYPE), dn_o, preferred_element_type=F32)
        dq = dq + dqt * x
        u = dqt * qt
        db = db + u
        d_end = -pltpu.roll(u, tb - HG_SUB, 0)
        for j in range(1, n_cross + 1):
            dkt = dkcat[(j - 1) * tb:j * tb]
            dkk = dkk + dkt * ys[j - 1]
            w = dkt * kts[j - 1]
            db = db - w
            d_end = d_end + (w if j == 1 else pltpu.roll(w, HG_SUB * (j - 1), 0))
        sh = 1
        while sh < HG_SUB:
            d_end = d_end + jnp.where(pos8 >= sh, pltpu.roll(d_end, sh, 0), 0.0)
            sh *= 2
        db = db + jnp.where(pos8 == HG_SUB - 1, d_end, 0.0)

        eb = jnp.exp(b)
        qe = q * eb
        chunks = [slice(c * HG_CHUNK, (c + 1) * HG_CHUNK) for c in range(nc)]
        b_lasts = [jnp.sum(jnp.where(pos[rs] == HG_CHUNK - 1, b[rs], 0.0), axis=0, keepdims=True) for rs in chunks]
        own = [lax.dot_general(dom[rs], qe[rs].astype(MXU_DTYPE), dn_o, preferred_element_type=F32) for rs in chunks]
        dst = dstate[...]
        after = [None] * nc
        for c in range(nc - 1, -1, -1):
            after[c] = dst
            dst = dst * jnp.exp(b_lasts[c]) + own[c]
        dstate[...] = dst
        dq_c, dkk_c, dv_c, db_c = [None] * nc, [None] * nc, [None] * nc, [None] * nc
        for c, rs in enumerate(chunks):
            st0 = st_ref[c * LANES:(c + 1) * LANES, :]
            eb_last = jnp.exp(b_lasts[c])
            er = jnp.exp(b_lasts[c] - b[rs])
            ke = kk[rs] * er
            dstm = after[c].astype(MXU_DTYPE)
            dqe = jnp.dot(dom[rs], st0.astype(MXU_DTYPE), preferred_element_type=F32)
            dv_c[c] = dv[rs] + lax.dot_general(ke.astype(MXU_DTYPE), dstm, dn_t, preferred_element_type=F32)
            dke = jnp.dot(v[rs].astype(MXU_DTYPE), dstm, preferred_element_type=F32)
            d_eb_last = jnp.sum(after[c] * st0, axis=0, keepdims=True)
            dq_c[c] = dq[rs] + dqe * eb[rs]
            dkk_c[c] = dkk[rs] + dke * er
            dkeke = dke * ke
            db_last = jnp.sum(dkeke, axis=0, keepdims=True) + d_eb_last * eb_last
            db_c[c] = db[rs] + dqe * qe[rs] - dkeke + jnp.where(pos[rs] == HG_CHUNK - 1, db_last, 0.0)
        dq = jnp.concatenate(dq_c, axis=0)
        dkk = jnp.concatenate(dkk_c, axis=0)
        dv = jnp.concatenate(dv_c, axis=0)
        dg = _chunk_scan(jnp.concatenate(db_c, axis=0), pos, True)

        dfv = dg * jnp.where(f > F_MIN, 1.0 / f, 0.0)
        dsg = (dfv - dkk) * (1.0 - lbv)
        dq_ref[...] = dq * (sq * (1.0 + qh * (1.0 - sq)))
        df_ref[...] = dsg * sg * sgn
        di_ref[...] = dv
        part = jnp.sum(dfv * sgn - dkk * sgn, axis=0, keepdims=True)

        @pl.when((bi == 0) & (i == 0))
        def _():
            dlb_ref[...] = part

        @pl.when((bi > 0) | (i > 0))
        def _():
            dlb_ref[...] += part

    def rev(first):
        return lambda h, b, i: (b * nb + (nb - 1 - i), first + h)

    blk = (tb, LANES)
    oblk = pl.BlockSpec(blk, rev(0))
    return _run(
        body, (z, z, z, lb, pair, states, do), out_shape=(_sds((T, HG_W), F32),) * 3 + (_sds((1, HG_W), F32),),
        grid=(HG_HEADS, B, nb),
        in_specs=[pl.BlockSpec(blk, rev(12)), pl.BlockSpec(blk, rev(16)), pl.BlockSpec(blk, rev(20)),
                  pl.BlockSpec((1, LANES), lambda h, b, i: (0, h)), pl.BlockSpec((tb, tb), lambda h, b, i: (0, 0)),
                  pl.BlockSpec((nc * LANES, LANES), rev(0)), oblk],
        out_specs=(oblk, oblk, oblk, pl.BlockSpec((1, LANES), lambda h, b, i: (0, h))),
        scratch_shapes=[pltpu.VMEM((LANES, LANES), F32)], name=name)


def _hgout_fwd(o, z, g, *, name):
    T = o.shape[0]
    tm = _tile(T, 512, 8)

    def body(o_ref, gh_ref, g_ref, y_ref):
        ov = o_ref[...]
        gh = gh_ref[...]
        r = lax.rsqrt(_group_mean(ov * ov, HG_HEAD_DIM) + EPS)
        sg, _ = _sigmoid_pair(gh)
        y_ref[...] = ov * r * g_ref[...] * (gh * sg)

    blk = pl.BlockSpec((tm, HG_W), lambda i: (i, 0))
    return _run(body, (o, z, g), out_shape=_sds((T, HG_W), F32), grid=(T // tm,),
                in_specs=[blk, pl.BlockSpec((tm, HG_W), lambda i: (i, 6)), pl.BlockSpec((1, HG_W), lambda i: (0, 0))],
                out_specs=blk, name=name)


def _hgout_bwd(o, z, g, dycat, *, name):
    T = o.shape[0]
    tm = _tile(T, 512, 8)

    def body(o_ref, gh_ref, g_ref, dy_ref, do_ref, dgh_ref, dg_ref):
        i = pl.program_id(0)
        ov = o_ref[...]
        gh = gh_ref[...]
        gv = g_ref[...]
        dy = dy_ref[...]
        r = lax.rsqrt(_group_mean(ov * ov, HG_HEAD_DIM) + EPS)
        xh = ov * r
        sg, _ = _sigmoid_pair(gh)
        dn = dy * (gh * sg)
        dgh_ref[...] = dy * xh * gv * (sg * (1.0 + gh * (1.0 - sg)))
        dng = dn * gv
        do_ref[...] = r * (dng - xh * _group_mean(dng * xh, HG_HEAD_DIM))
        part = jnp.sum(dn * xh, axis=0, keepdims=True)

        @pl.when(i == 0)
        def _():
            dg_ref[...] = part

        @pl.when(i > 0)
        def _():
            dg_ref[...] += part

    blk = pl.BlockSpec((tm, HG_W), lambda i: (i, 0))
    row = pl.BlockSpec((1, HG_W), lambda i: (0, 0))
    return _run(body, (o, z, g, dycat),
                out_shape=(_sds((T, HG_W), F32), _sds((T, HG_W), F32), _sds((1, HG_W), F32)), grid=(T // tm,),
                in_specs=[blk, pl.BlockSpec((tm, HG_W), lambda i: (i, 6)), row,
                          pl.BlockSpec((tm, HG_W), lambda i: (i, 1))],
                out_specs=(blk, blk, row), name=name)


def _lower_bounds_fwd(hg_lb):
    L = hg_lb.shape[0]

    def body(x_ref, o_ref):
        rows = [x_ref[l:l + 1, :] for l in range(L)]
        mx = functools.reduce(jnp.maximum, rows)
        es = [jnp.exp(r - mx) for r in rows]
        tot = functools.reduce(jnp.add, es)
        acc = jnp.zeros_like(tot)
        for l in range(L):
            acc = acc + es[l] / tot
            o_ref[l:l + 1, :] = acc - es[0] / tot

    return _pcall(body, out_shape=_sds(hg_lb.shape, F32), name="lower_bounds_fwd")(hg_lb)


def _lower_bounds_bwd(hg_lb, dlower):
    L = hg_lb.shape[0]

    def body(x_ref, d_ref, o_ref):
        rows = [x_ref[l:l + 1, :] for l in range(L)]
        ds = [d_ref[l:l + 1, :] for l in range(L)]
        mx = functools.reduce(jnp.maximum, rows)
        es = [jnp.exp(r - mx) for r in rows]
        tot = functools.reduce(jnp.add, es)
        ps = [e / tot for e in es]
        dps = []
        for j in range(L):
            t = functools.reduce(jnp.add, ds[j:])
            if j == 0:
                t = t - functools.reduce(jnp.add, ds)
            dps.append(t)
        inner = functools.reduce(jnp.add, [p * dp for p, dp in zip(ps, dps)])
        for j in range(L):
            o_ref[j:j + 1, :] = ps[j] * (dps[j] - inner)

    return _pcall(body, out_shape=_sds(hg_lb.shape, F32), name="lower_bounds_bwd")(hg_lb, dlower)


def _memattn_fwd(qm, km, kv, *, B, S, NM, name):
    T, D = qm.shape
    E = D // MEM_HEADS
    tq = _tile(S, 512, 8)
    nq = S // tq
    scale = 1.0 / math.sqrt(E)

    def body(q_ref, k_ref, v_ref, o_ref):
        for h in range(MEM_HEADS):
            cs = slice(h * E, (h + 1) * E)
            s = lax.dot_general(q_ref[:, cs].astype(MXU_DTYPE), k_ref[:, cs].astype(MXU_DTYPE),
                                (((1,), (1,)), ((), ())), preferred_element_type=F32) * scale
            e = jnp.exp(s - jnp.max(s, axis=1, keepdims=True))
            pr = e / jnp.sum(e, axis=1, keepdims=True)
            o_ref[:, cs] = jnp.dot(pr.astype(MXU_DTYPE), v_ref[:, cs].astype(MXU_DTYPE),
                                   preferred_element_type=F32).astype(o_ref.dtype)

    return _run(
        body, (qm, km, kv), out_shape=_sds((T, D), MXU_DTYPE), grid=(B, nq),
        in_specs=[pl.BlockSpec((tq, D), lambda b, i: (b * nq + i, 0)), pl.BlockSpec((NM, D), lambda b, i: (b, 0)),
                  pl.BlockSpec((NM, D), lambda b, i: (b, 1))],
        out_specs=pl.BlockSpec((tq, D), lambda b, i: (b * nq + i, 0)), name=name)


def _memattn_bwd(qm, km, kv, do, *, B, S, NM, name):
    T, D = qm.shape
    E = D // MEM_HEADS
    tq = _tile(S, 512, 8)
    nq = S // tq
    scale = 1.0 / math.sqrt(E)

    def body(q_ref, k_ref, v_ref, do_ref, dq_ref, dk_ref, dv_ref):
        i = pl.program_id(1)

        @pl.when(i == 0)
        def _():
            dk_ref[...] = jnp.zeros_like(dk_ref)
            dv_ref[...] = jnp.zeros_like(dv_ref)

        for h in range(MEM_HEADS):
            cs = slice(h * E, (h + 1) * E)
            qh = q_ref[:, cs].astype(MXU_DTYPE)
            kh = k_ref[:, cs].astype(MXU_DTYPE)
            vh = v_ref[:, cs].astype(MXU_DTYPE)
            doh = do_ref[:, cs].astype(MXU_DTYPE)
            s = lax.dot_general(qh, kh, (((1,), (1,)), ((), ())), preferred_element_type=F32) * scale
            e = jnp.exp(s - jnp.max(s, axis=1, keepdims=True))
            pr = e / jnp.sum(e, axis=1, keepdims=True)
            dp = lax.dot_general(doh, vh, (((1,), (1,)), ((), ())), preferred_element_type=F32)
            ds = (pr * (dp - jnp.sum(dp * pr, axis=1, keepdims=True))).astype(MXU_DTYPE)
            dq_ref[:, cs] = jnp.dot(ds, kh, preferred_element_type=F32) * scale
            dk_ref[:, cs] += lax.dot_general(ds, qh, (((0,), (0,)), ((), ())), preferred_element_type=F32) * scale
            dv_ref[:, cs] += lax.dot_general(pr.astype(MXU_DTYPE), doh, (((0,), (0,)), ((), ())),
                                             preferred_element_type=F32)

    qblk = pl.BlockSpec((tq, D), lambda b, i: (b * nq + i, 0))
    mblk = pl.BlockSpec((NM, D), lambda b, i: (b, 0))
    return _run(
        body, (qm, km, kv, do), out_shape=(_sds((T, D), F32), _sds((B * NM, D), F32), _sds((B * NM, D), F32)),
        grid=(B, nq), in_specs=[qblk, mblk, pl.BlockSpec((NM, D), lambda b, i: (b, 1)), qblk],
        out_specs=(qblk, mblk, mblk), name=name)


def _loss_head(y, tgt):
    T, D = y.shape
    tm = _tile(T, 512, 8)

    def body(y_ref, t_ref, dy_ref, l_ref):
        i = pl.program_id(0)
        err = y_ref[...] - t_ref[...]
        dy_ref[...] = err * (1.0 / D)
        part = jnp.sum(jnp.sum(err * err, axis=1, keepdims=True), axis=0, keepdims=True) * (0.5 / D)

        @pl.when(i == 0)
        def _():
            l_ref[...] = part

        @pl.when(i > 0)
        def _():
            l_ref[...] += part

    blk = pl.BlockSpec((tm, D), lambda i: (i, 0))
    return _run(body, (y, tgt), out_shape=(_sds((T, D), F32), _sds((1, 1), F32)), grid=(T // tm,),
                in_specs=[blk, blk], out_specs=(blk, pl.BlockSpec((1, 1), lambda i: (0, 0))), name="loss_head")


def _adamw(w, g, m, v):
    m = ADAM_B1 * m + (1.0 - ADAM_B1) * g
    v = ADAM_B2 * v + (1.0 - ADAM_B2) * jnp.square(g)
    m_hat = m / (1.0 - ADAM_B1 ** ADAM_STEP)
    v_hat = v / (1.0 - ADAM_B2 ** ADAM_STEP)
    delta = -ADAM_LR * (m_hat / (jnp.sqrt(v_hat) + ADAM_EPS) + ADAM_WD * w)
    return delta, m, v


def _sum_adamw(recv, w, m, v):
    _, R, C = recv.shape
    tr = _tile(R, 384, 8)

    def body(r_ref, w_ref, m_ref, v_ref, g_ref, d_ref, nm_ref, nv_ref):
        g = r_ref[0].astype(F32)
        for j in range(1, N_DEV):
            g = g + r_ref[j].astype(F32)
        g_ref[...] = g
        d_ref[...], nm_ref[...], nv_ref[...] = _adamw(w_ref[...], g, m_ref[...], v_ref[...])

    blk = pl.BlockSpec((tr, C), lambda i: (i, 0))
    return _run(body, (recv, w, m, v), out_shape=(_sds((R, C), F32),) * 4, grid=(R // tr,),
                in_specs=[pl.BlockSpec((N_DEV, tr, C), lambda i: (0, i, 0)), blk, blk, blk],
                out_specs=(blk,) * 4, name="grad_sum_adamw")


def _small_sum_adamw(gp, w, m, v):
    NR, C = gp.shape
    MESH = pl.DeviceIdType.MESH

    def body(gp_ref, w_ref, m_ref, v_ref, g_ref, d_ref, nm_ref, nv_ref, gath, send_sems, recv_sems):
        x, y, c = _mesh_pos()
        me = 4 * x + 2 * y + c
        gath[me] = gp_ref[...]
        copies = []
        for k in range(1, N_DEV):
            peer = (_flip(x, k & 4), _flip(y, k & 2), _flip(c, k & 1))
            cp = pltpu.make_async_remote_copy(src_ref=gp_ref, dst_ref=gath.at[me], send_sem=send_sems.at[k - 1],
                                              recv_sem=recv_sems.at[k - 1], device_id=peer, device_id_type=MESH)
            cp.start()
            copies.append(cp)
        for cp in copies:
            cp.wait()
        g = gath[0]
        for j in range(1, N_DEV):
            g = g + gath[j]
        g_ref[...] = g
        d_ref[...], nm_ref[...], nv_ref[...] = _adamw(w_ref[...], g, m_ref[...], v_ref[...])

    vm = pl.BlockSpec(memory_space=pltpu.VMEM)
    return _pcall(body, out_shape=(_sds((NR, C), F32),) * 4, in_specs=[vm] * 4, out_specs=(vm,) * 4,
                  scratch_shapes=[pltpu.VMEM((N_DEV, NR, C), F32), pltpu.SemaphoreType.DMA((7,)),
                                  pltpu.SemaphoreType.DMA((7,))], name="small_sum_adamw")(gp, w, m, v)


def _pack(shards, names, dtype=F32):
    return jnp.concatenate([shards[n].reshape(-1, FLAT_W) for n in names], axis=0).astype(dtype)


def _unpack(flat, names, shapes):
    out, off = {}, 0
    for n in names:
        rows = math.prod(shapes[n]) // FLAT_W
        out[n] = flat[off:off + rows].reshape(shapes[n])
        off += rows
    return out


def _gathered(gath, names, shapes):
    out, off = {}, 0
    for n in names:
        r, c = shapes[n]
        rows = r * c // FLAT_W
        blk = gath[:, off:off + rows].reshape(N_DEV, r, c)
        out[n] = blk.transpose(1, 0, 2).reshape(r, N_DEV * c) if n in COL_SHARDED else blk.reshape(N_DEV * r, c)
        off += rows
    return out


def _scatter_blocks(grads, names, shapes, dtype):
    parts = []
    for n in names:
        r, c = shapes[n]
        g = grads[n]
        g = g.reshape(r, N_DEV, c).transpose(1, 0, 2) if n in COL_SHARDED else g.reshape(N_DEV, r, c)
        parts.append(g.reshape(N_DEV, -1, FLAT_W).astype(dtype))
    return jnp.concatenate(parts, axis=1)


def _pack_small(d, prefix, loss=None):
    parts = []
    for n in SMALL:
        a = d[prefix + n].reshape(-1).astype(F32)
        pad = (-a.shape[0]) % LANES
        parts.append(jnp.pad(a, (0, pad)) if pad else a)
    tail = jnp.zeros((LANES,), F32)
    if loss is not None:
        tail = tail.at[0].set(loss)
    flat = jnp.concatenate(parts + [tail])
    pad = (-flat.shape[0]) % (8 * LANES)
    return jnp.pad(flat, (0, pad)).reshape(-1, LANES)


def _unpack_small(packed, shapes):
    flat = packed.reshape(-1)
    out, off = {}, 0
    for n in SMALL:
        size = math.prod(shapes[n])
        out[n] = flat[off:off + size].reshape(shapes[n])
        off += size + (-size) % LANES
    return out, flat[off]


def _row(a):
    return a.reshape(1, -1).astype(F32)


def _layer_fwd(x, memf, w_in, shards, next_in, shapes, sp, lb, tab, dims):
    B, S, NM = dims
    D = x.shape[1]
    W = {"w_in": w_in}
    h1 = _norm_fwd(x, sp["norm1_g"], E=D, W=D, out_dtype=MXU_DTYPE, name="norm1_fwd")
    z, gath = _matmul(h1, w_in, carry=("gather", _pack(shards, GATHER_MIX, MXU_DTYPE)), name="in_proj")
    W.update(_gathered(gath, GATHER_MIX, shapes))
    qn = _norm_fwd(z, sp["attn_qn_g"], E=ATT_HEAD_DIM, W=ATT_W, cb=0, out_dtype=MXU_DTYPE, name="qnorm_fwd")
    kn = _norm_fwd(z, sp["attn_kn_g"], E=ATT_HEAD_DIM, W=ATT_W, cb=1, out_dtype=MXU_DTYPE, name="knorm_fwd")
    (ya, lse), gath = _attn_fwd(tab["attn"], qn, kn, z, B=B, S=S, carry=("gather", _pack(shards, GATHER_FF, MXU_DTYPE)),
                                name="dilated_attn_fwd")
    W.update(_gathered(gath, GATHER_FF, shapes))
    o, states = _hgrn_fwd(z, lb, tab["hgrn"], B=B, S=S, name="hgrn_fwd")
    yh = _hgout_fwd(o, z, sp["hg_onorm_g"], name="hgrn_out_fwd")
    ycat = jnp.concatenate([ya, yh], axis=1).astype(MXU_DTYPE)
    x1 = _matmul(ycat, W["w_out"], epi="add", extra=x, name="out_proj")
    h2 = _norm_fwd(x1, sp["norm2_g"], E=D, W=D, out_dtype=MXU_DTYPE, name="norm2_fwd")
    qmp = _matmul(h2, W["w_mq"], name="mq_proj")
    qm = _norm_fwd(qmp, sp["mq_norm_g"], E=D // MEM_HEADS, W=D, out_dtype=MXU_DTYPE, name="mqnorm_fwd")
    mn = _norm_fwd(memf, sp["mem_norm_g"], E=D, W=D, out_dtype=MXU_DTYPE, name="memnorm_fwd")
    kv = _matmul(mn, W["w_mkv"], name="mkv_proj")
    km = _norm_fwd(kv, sp["mk_norm_g"], E=D // MEM_HEADS, W=D, cb=0, out_dtype=MXU_DTYPE, name="mknorm_fwd")
    om = _memattn_fwd(qm, km, kv, B=B, S=S, NM=NM, name="mem_attn_fwd")
    x2 = _matmul(om, W["w_mo"], epi="add", extra=x1, name="mo_proj")
    h3 = _norm_fwd(x2, sp["norm3_g"], E=D, W=D, out_dtype=MXU_DTYPE, name="norm3_fwd")
    if next_in is None:
        a, w_in_next = _matmul(h3, W["w_ff1"], name="ff1"), None
    else:
        a, gath = _matmul(h3, W["w_ff1"], carry=("gather", _pack({"w_in": next_in}, GATHER_IN, MXU_DTYPE)), name="ff1")
        w_in_next = _gathered(gath, GATHER_IN, shapes)["w_in"]
    x3 = _matmul(a, W["w_ff2"], pro="relu2", epi="add", extra=x2, name="ff2")
    saved = dict(x=x, h1=h1, z=z, qn=qn, kn=kn, ya=ya, lse=lse, o=o, states=states, ycat=ycat, x1=x1, h2=h2,
                 qmp=qmp, qm=qm, mn=mn, kv=kv, km=km, om=om, x2=x2, h3=h3, a=a)
    return x3, saved, W, w_in_next


def _layer_bwd(dx3, s, memf, W, shapes, sp, lb, tab, dims):
    B, S, NM = dims
    D = dx3.shape[1]
    E_M = D // MEM_HEADS
    gw, gs, recv = {}, {}, {}

    def blocks(names):
        return ("scatter", _scatter_blocks(gw, names, shapes, MXU_DTYPE))

    da = _matmul(dx3, W["w_ff2"], tb=True, epi="relu2grad", extra=s["a"], out_dtype=MXU_DTYPE, name="ff2_dgrad")
    gw["w_ff2"] = _matmul(s["a"], dx3, ta=True, out_dtype=MXU_DTYPE, pro="relu2", name="ff2_wgrad")
    dh3, recv[SCATTER_A] = _matmul(da, W["w_ff1"], tb=True, carry=blocks(SCATTER_A), name="ff1_dgrad")
    gw["w_ff1"] = _matmul(s["h3"], da, ta=True, out_dtype=MXU_DTYPE, name="ff1_wgrad")
    dx2, gs["norm3_g"] = _norm_bwd(s["x2"], sp["norm3_g"], dh3, E=D, W=D, res=dx3, name="norm3_bwd")
    dom = _matmul(dx2, W["w_mo"], tb=True, name="mo_dgrad")
    gw["w_mo"] = _matmul(s["om"], dx2, ta=True, out_dtype=MXU_DTYPE, name="mo_wgrad")
    dqm, dkm, dvm = _memattn_bwd(s["qm"], s["km"], s["kv"], dom, B=B, S=S, NM=NM, name="mem_attn_bwd")
    dqmp, gs["mq_norm_g"] = _norm_bwd(s["qmp"], sp["mq_norm_g"], dqm, E=E_M, W=D, fold=E_M, out_dtype=MXU_DTYPE,
                                      name="mqnorm_bwd")
    dkmp, gs["mk_norm_g"] = _norm_bwd(s["kv"], sp["mk_norm_g"], dkm, E=E_M, W=D, cb=0, fold=E_M,
                                      out_dtype=MXU_DTYPE, name="mknorm_bwd")
    dkv = jnp.concatenate([dkmp, dvm.astype(MXU_DTYPE)], axis=1)
    dh2 = _matmul(dqmp, W["w_mq"], tb=True, name="mq_dgrad")
    gw["w_mq"] = _matmul(s["h2"], dqmp, ta=True, out_dtype=MXU_DTYPE, name="mq_wgrad")
    dmn = _matmul(dkv, W["w_mkv"], tb=True, name="mkv_dgrad")
    gw["w_mkv"] = _matmul(s["mn"], dkv, ta=True, out_dtype=MXU_DTYPE, name="mkv_wgrad")
    _, gs["mem_norm_g"] = _norm_bwd(memf, sp["mem_norm_g"], dmn, E=D, W=D, name="memnorm_bwd")
    dx1, gs["norm2_g"] = _norm_bwd(s["x1"], sp["norm2_g"], dh2, E=D, W=D, res=dx2, name="norm2_bwd")
    dycat = _matmul(dx1, W["w_out"], tb=True, name="out_dgrad")
    gw["w_out"] = _matmul(s["ycat"], dx1, ta=True, out_dtype=MXU_DTYPE, name="out_wgrad")
    do_hg, dgh, gs["hg_onorm_g"] = _hgout_bwd(s["o"], s["z"], sp["hg_onorm_g"], dycat, name="hgrn_out_bwd")
    dqh, dfh, dih, dlb = _hgrn_bwd(s["z"], lb, tab["hgrn"], s["states"], do_hg, B=B, S=S, name="hgrn_bwd")
    (dqn, dkn, dv), recv[SCATTER_B] = _attn_bwd(tab["attn"], s["qn"], s["kn"], s["z"], s["ya"], s["lse"], dycat, B=B, S=S,
                                                 carry=blocks(SCATTER_B), name="dilated_attn_bwd")
    dq, gs["attn_qn_g"] = _norm_bwd(s["z"], sp["attn_qn_g"], dqn, E=ATT_HEAD_DIM, W=ATT_W, cb=0, fold=ATT_HEAD_DIM,
                                    out_dtype=MXU_DTYPE, name="qnorm_bwd")
    dk, gs["attn_kn_g"] = _norm_bwd(s["z"], sp["attn_kn_g"], dkn, E=ATT_HEAD_DIM, W=ATT_W, cb=1, fold=ATT_HEAD_DIM,
                                    out_dtype=MXU_DTYPE, name="knorm_bwd")
    dz = jnp.concatenate([dq, dk] + [t.astype(MXU_DTYPE) for t in (dv, dqh, dfh, dih, dgh)], axis=1)
    gw["w_in"] = _matmul(s["h1"], dz, ta=True, out_dtype=MXU_DTYPE, name="in_wgrad")
    dh1, recv[SCATTER_C] = _matmul(dz, W["w_in"], tb=True, carry=blocks(SCATTER_C), name="in_dgrad")
    dx0, gs["norm1_g"] = _norm_bwd(s["x"], sp["norm1_g"], dh1, E=D, W=D, res=dx1, name="norm1_bwd")
    gs["attn_qn_g"] = gs["attn_qn_g"][:, :ATT_HEAD_DIM]
    gs["attn_kn_g"] = gs["attn_kn_g"][:, :ATT_HEAD_DIM]
    return dx0, recv, gs, dlb


def kernel(x, mem, norm1_g, w_in, attn_qn_g, attn_kn_g, hg_lb, hg_onorm_g, w_out, norm2_g, mem_norm_g, w_mq, w_mkv, mq_norm_g, mk_norm_g, w_mo, norm3_g, w_ff1, w_ff2, loss_target, m_norm1_g, m_w_in, m_attn_qn_g, m_attn_kn_g, m_hg_lb, m_hg_onorm_g, m_w_out, m_norm2_g, m_mem_norm_g, m_w_mq, m_w_mkv, m_mq_norm_g, m_mk_norm_g, m_w_mo, m_norm3_g, m_w_ff1, m_w_ff2, v_norm1_g, v_w_in, v_attn_qn_g, v_attn_kn_g, v_hg_lb, v_hg_onorm_g, v_w_out, v_norm2_g, v_mem_norm_g, v_w_mq, v_w_mkv, v_mq_norm_g, v_mk_norm_g, v_w_mo, v_norm3_g, v_w_ff1, v_w_ff2):
    given = dict(norm1_g=norm1_g, w_in=w_in, attn_qn_g=attn_qn_g, attn_kn_g=attn_kn_g, hg_lb=hg_lb, hg_onorm_g=hg_onorm_g, w_out=w_out, norm2_g=norm2_g, mem_norm_g=mem_norm_g, w_mq=w_mq, w_mkv=w_mkv, mq_norm_g=mq_norm_g, mk_norm_g=mk_norm_g, w_mo=w_mo, norm3_g=norm3_g, w_ff1=w_ff1, w_ff2=w_ff2, m_norm1_g=m_norm1_g, m_w_in=m_w_in, m_attn_qn_g=m_attn_qn_g, m_attn_kn_g=m_attn_kn_g, m_hg_lb=m_hg_lb, m_hg_onorm_g=m_hg_onorm_g, m_w_out=m_w_out, m_norm2_g=m_norm2_g, m_mem_norm_g=m_mem_norm_g, m_w_mq=m_w_mq, m_w_mkv=m_w_mkv, m_mq_norm_g=m_mq_norm_g, m_mk_norm_g=m_mk_norm_g, m_w_mo=m_w_mo, m_norm3_g=m_norm3_g, m_w_ff1=m_w_ff1, m_w_ff2=m_w_ff2, v_norm1_g=v_norm1_g, v_w_in=v_w_in, v_attn_qn_g=v_attn_qn_g, v_attn_kn_g=v_attn_kn_g, v_hg_lb=v_hg_lb, v_hg_onorm_g=v_hg_onorm_g, v_w_out=v_w_out, v_norm2_g=v_norm2_g, v_mem_norm_g=v_mem_norm_g, v_w_mq=v_w_mq, v_w_mkv=v_w_mkv, v_mq_norm_g=v_mq_norm_g, v_mk_norm_g=v_mk_norm_g, v_w_mo=v_w_mo, v_norm3_g=v_norm3_g, v_w_ff1=v_w_ff1, v_w_ff2=v_w_ff2)
    B, S, D = x.shape
    NM = mem.shape[1]
    L = w_in.shape[0]
    dims = (B, S, NM)
    shapes = {n: given[n].shape[1:] for n in BIG}
    small_shapes = {n: given[n].shape for n in SMALL}

    def shards(prefix, l):
        return {n: given[prefix + n][l] for n in BIG}

    lower = _lower_bounds_fwd(hg_lb)
    tab = {"attn": _attn_tables(S, _tile(S, ATT_Q_BLOCK, 8)), "hgrn": _hgrn_pair_table(_tile(S, 256, HG_CHUNK))}
    xf = x.reshape(B * S, D)
    memf = mem.reshape(B * NM, D)

    def small_params(l):
        sp = {"norm1_g": _row(norm1_g[l]), "norm2_g": _row(norm2_g[l]), "norm3_g": _row(norm3_g[l]),
              "mem_norm_g": _row(mem_norm_g[l]), "hg_onorm_g": _row(hg_onorm_g[l]),
              "attn_qn_g": _row(jnp.tile(attn_qn_g[l], ATT_HEADS)), "attn_kn_g": _row(jnp.tile(attn_kn_g[l], ATT_HEADS)),
              "mq_norm_g": _row(jnp.tile(mq_norm_g[l], MEM_HEADS)), "mk_norm_g": _row(jnp.tile(mk_norm_g[l], MEM_HEADS))}
        return sp, _row(lower[l])

    w_in_full = _gathered(_all_gather(_pack(shards("", 0), GATHER_IN, MXU_DTYPE)), GATHER_IN, shapes)["w_in"]
    saved, weights = [], []
    h = xf
    for l in range(L):
        sp, lb = small_params(l)
        next_in = given["w_in"][l + 1] if l + 1 < L else None
        h, s, W, w_in_full = _layer_fwd(h, memf, w_in_full, shards("", l), next_in, shapes, sp, lb, tab, dims)
        saved.append(s)
        weights.append(W)

    dh, loss_part = _loss_head(h, loss_target.reshape(B * S, D))

    recv_layers, gs_layers, dlb_layers = [None] * L, [None] * L, [None] * L
    for l in range(L - 1, -1, -1):
        sp, lb = small_params(l)
        dh, recv_layers[l], gs_layers[l], dlb_layers[l] = _layer_bwd(dh, saved[l], memf, weights[l], shapes, sp, lb,
                                                                      tab, dims)
    grad_x = dh.reshape(B, S, D)

    big_out = [{n: [None] * L for n in BIG} for _ in range(4)]
    for l in range(L):
        for names in SCATTERS:
            flats = _sum_adamw(recv_layers[l][names], _pack(shards("", l), names), _pack(shards("m_", l), names),
                               _pack(shards("v_", l), names))
            for k in range(4):
                for n, t in _unpack(flats[k], names, shapes).items():
                    big_out[k][n][l] = t
    big_out = [{n: jnp.stack(d[n]) for n in BIG} for d in big_out]

    gs = {n: jnp.stack([gs_layers[l][n].reshape(small_shapes[n][1:]) for l in range(L)]) for n in SMALL if n != "hg_lb"}
    gs["hg_lb"] = _lower_bounds_bwd(hg_lb, jnp.concatenate(dlb_layers, axis=0))
    packed = _small_sum_adamw(_pack_small(gs, "", loss_part[0, 0]), _pack_small(given, ""), _pack_small(given, "m_"),
                              _pack_small(given, "v_"))
    small_out, loss = [], None
    for t in packed:
        d, tail = _unpack_small(t, small_shapes)
        small_out.append(d)
        loss = tail if loss is None else loss

    outs = [loss, grad_x]
    for k in range(4):
        outs += [small_out[k][n] if n in small_shapes else big_out[k][n] for n in WEIGHTS]
    return tuple(outs)
```

```python
import functools
import math

import jax
import jax.numpy as jnp
from jax import lax
from jax.experimental import pallas as pl
from jax.experimental.pallas import tpu as pltpu

F32 = jnp.float32
MXU_DTYPE = jnp.bfloat16
VMEM_LIMIT = 48 * 1024 * 1024

N_DEV = 8
EPS = 1e-6
NEG = -1e30
F_MIN = 1e-12
ATT_HEADS = 8
ATT_HEAD_DIM = 64
ATT_W = ATT_HEADS * ATT_HEAD_DIM
DILATIONS = (1, 4, 16)
DIL_STEPS = 128
ATT_Q_BLOCK = 256
ATT_KEY_EXTENT = 512
HG_HEADS = 4
HG_HEAD_DIM = 128
HG_W = HG_HEADS * HG_HEAD_DIM
HG_CHUNK = 32
HG_SUB = 8
MEM_HEADS = 4
LANES = 128
FLAT_W = 1024

ADAM_LR = 0.001
ADAM_B1 = 0.9
ADAM_B2 = 0.999
ADAM_EPS = 1e-08
ADAM_WD = 0.01
ADAM_STEP = 10

BIG = ("w_in", "w_out", "w_mq", "w_mkv", "w_mo", "w_ff1", "w_ff2")
COL_SHARDED = ("w_in", "w_mkv", "w_ff1")
SMALL = ("norm1_g", "attn_qn_g", "attn_kn_g", "hg_lb", "hg_onorm_g", "norm2_g", "mem_norm_g",
         "mq_norm_g", "mk_norm_g", "norm3_g")
WEIGHTS = ("norm1_g", "w_in", "attn_qn_g", "attn_kn_g", "hg_lb", "hg_onorm_g", "w_out", "norm2_g",
           "mem_norm_g", "w_mq", "w_mkv", "mq_norm_g", "mk_norm_g", "w_mo", "norm3_g", "w_ff1", "w_ff2")
GATHER_IN = ("w_in",)
GATHER_MIX = ("w_out", "w_mq", "w_mkv", "w_mo")
GATHER_FF = ("w_ff1", "w_ff2")
SCATTER_A = ("w_ff2",)
SCATTER_B = ("w_ff1", "w_mo", "w_mq", "w_mkv", "w_out")
SCATTER_C = ("w_in",)
SCATTERS = (SCATTER_A, SCATTER_B, SCATTER_C)


def _pcall(body, **kw):
    return pl.pallas_call(body, **kw)


def _params():
    return pltpu.CompilerParams(vmem_limit_bytes=VMEM_LIMIT)


def _tile(n, pref, mult):
    t = (min(n, pref) // mult) * mult
    while t >= mult:
        if n % t == 0:
            return t
        t -= mult
    return n


def _sds(shape, dtype):
    return jax.ShapeDtypeStruct(shape, dtype)


def _mesh_pos():
    return lax.axis_index("x"), lax.axis_index("y"), lax.axis_index("c")


def _flip(v, bit):
    return 1 - v if bit else v


def _gather_hooks(x_ref, out_ref, send_sems, recv_sems, local_sem):
    MESH = pl.DeviceIdType.MESH
    x, y, c = _mesh_pos()
    me, sibling = (x, y, c), (x, y, 1 - c)
    chips = [(1 - x, y), (x, 1 - y), (1 - x, 1 - y)]

    def rows(px, py, pc):
        return out_ref.at[4 * px + 2 * py + pc]

    def copy(k, block, to, src=None):
        return pltpu.make_async_remote_copy(
            src_ref=rows(*block) if src is None else src, dst_ref=rows(*block), send_sem=send_sems.at[k],
            recv_sem=recv_sems.at[k], device_id=to, device_id_type=MESH)

    mine = pltpu.make_async_copy(x_ref, rows(*me), local_sem)
    first = [copy(0, me, sibling, src=x_ref)]
    first += [copy(1 + j, me, (*chip, c), src=x_ref) for j, chip in enumerate(chips)]
    passed = [copy(4 + j, (*chip, c), sibling) for j, chip in enumerate(chips)]

    def start():
        mine.start()
        for cp in first:
            cp.start()

    def forward():
        for j, chip in enumerate(chips):
            copy(1 + j, (*chip, c), me).wait_recv()
            passed[j].start()

    def finish():
        copy(0, sibling, me).wait_recv()
        for j, chip in enumerate(chips):
            copy(4 + j, (*chip, 1 - c), me).wait_recv()
        for cp in first + passed:
            cp.wait_send()
        mine.wait()

    return start, forward, finish


def _scatter_hooks(s_ref, r_ref, send_sems, recv_sems, local_sem):
    MESH = pl.DeviceIdType.MESH
    x, y, c = _mesh_pos()
    me = 4 * x + 2 * y + c
    mine = pltpu.make_async_copy(s_ref.at[me], r_ref.at[me], local_sem)
    copies = []
    for m in range(1, N_DEV):
        px, py, pc = _flip(x, m & 4), _flip(y, m & 2), _flip(c, m & 1)
        copies.append(pltpu.make_async_remote_copy(
            src_ref=s_ref.at[4 * px + 2 * py + pc], dst_ref=r_ref.at[me], send_sem=send_sems.at[m - 1],
            recv_sem=recv_sems.at[m - 1], device_id=(px, py, pc), device_id_type=MESH))

    def start():
        mine.start()
        for cp in copies:
            cp.start()

    def finish():
        for cp in copies:
            cp.wait()
        mine.wait()

    return start, None, finish


_EXCHANGE_SEMS = [pltpu.SemaphoreType.DMA((N_DEV - 1,)), pltpu.SemaphoreType.DMA((N_DEV - 1,)),
                  pltpu.SemaphoreType.DMA]


def _exchange_out(kind, xs):
    return _sds((N_DEV,) + xs.shape, xs.dtype) if kind == "gather" else _sds(xs.shape, xs.dtype)


def _run(body, args, *, out_shape, grid, in_specs, out_specs, scratch_shapes=(), name, carry=None):
    single = not isinstance(out_shape, (tuple, list))
    outs = (out_shape,) if single else tuple(out_shape)
    ospecs = (out_specs,) if single else tuple(out_specs)
    if carry is None:
        res = _pcall(body, out_shape=outs, grid=grid, in_specs=list(in_specs), out_specs=ospecs,
                     scratch_shapes=list(scratch_shapes), compiler_params=_params(), name=name)(*args)
        return res[0] if single else tuple(res)
    kind, xs = carry
    n_in, n_out, n_scr = len(args), len(outs), len(scratch_shapes)
    total = math.prod(grid)

    def wrapped(*refs):
        ins, x_ref = refs[:n_in], refs[n_in]
        o, xo_ref = refs[n_in + 1:n_in + 1 + n_out], refs[n_in + 1 + n_out]
        scr = refs[n_in + 2 + n_out:n_in + 2 + n_out + n_scr]
        sems = refs[n_in + 2 + n_out + n_scr:]
        step = pl.program_id(0)
        for ax in range(1, len(grid)):
            step = step * grid[ax] + pl.program_id(ax)
        start, forward, finish = (_gather_hooks if kind == "gather" else _scatter_hooks)(x_ref, xo_ref, *sems)
        pl.when(step == 0)(start)
        body(*ins, *o, *scr)
        if forward is not None:
            pl.when(step == (3 * total) // 4)(forward)
        pl.when(step == total - 1)(finish)

    hbm = pl.BlockSpec(memory_space=pl.ANY)
    res = _pcall(wrapped, out_shape=outs + (_exchange_out(kind, xs),), grid=grid, in_specs=list(in_specs) + [hbm],
                 out_specs=ospecs + (hbm,), scratch_shapes=list(scratch_shapes) + _EXCHANGE_SEMS,
                 compiler_params=_params(), name=name)(*args, xs)
    main = tuple(res[:-1])
    return (main[0] if single else main), res[-1]


def _all_gather(xs):
    def body(x_ref, out_ref, send_sems, recv_sems, local_sem):
        start, forward, finish = _gather_hooks(x_ref, out_ref, send_sems, recv_sems, local_sem)
        start()
        forward()
        finish()

    hbm = pl.BlockSpec(memory_space=pl.ANY)
    return _pcall(body, out_shape=_exchange_out("gather", xs), in_specs=[hbm], out_specs=hbm,
                  scratch_shapes=_EXCHANGE_SEMS, name="weights_all_gather")(xs)


def _matmul(a, b, *, ta=False, tb=False, epi=None, extra=None, out_dtype=F32, carry=None, name):
    M, K = (a.shape[1], a.shape[0]) if ta else a.shape
    N = b.shape[0] if tb else b.shape[1]
    tm = _tile(M, 1024, 8 if not ta else LANES)
    tn = _tile(N, 1024, LANES)
    tk = _tile(K, 512 if ta else 1024, LANES if not ta else 8)
    nk = K // tk
    dims = (((0 if ta else 1,), (1 if tb else 0,)), ((), ()))

    def body(a_ref, b_ref, *rest):
        e_ref = rest[0] if extra is not None else None
        o_ref = rest[1] if extra is not None else rest[0]

        def finish(r):
            if epi == "add":
                r = r + e_ref[...]
            elif epi == "relu2":
                r = jnp.square(jnp.maximum(r, 0.0))
            elif epi == "relu2grad":
                r = r * (2.0 * jnp.sqrt(e_ref[...].astype(F32)))
            o_ref[...] = r.astype(out_dtype)

        part = lax.dot_general(a_ref[...].astype(MXU_DTYPE), b_ref[...].astype(MXU_DTYPE), dims,
                               preferred_element_type=F32)
        if nk == 1:
            finish(part)
        else:
            acc_ref = rest[-1]
            k = pl.program_id(2)

            @pl.when(k == 0)
            def _():
                acc_ref[...] = part

            @pl.when(k > 0)
            def _():
                acc_ref[...] += part

            @pl.when(k == nk - 1)
            def _():
                finish(acc_ref[...])

    a_spec = pl.BlockSpec((tk, tm), lambda i, j, k: (k, i)) if ta else pl.BlockSpec((tm, tk), lambda i, j, k: (i, k))
    b_spec = pl.BlockSpec((tn, tk), lambda i, j, k: (j, k)) if tb else pl.BlockSpec((tk, tn), lambda i, j, k: (k, j))
    o_spec = pl.BlockSpec((tm, tn), lambda i, j, k: (i, j))
    in_specs = [a_spec, b_spec] + ([o_spec] if extra is not None else [])
    args = (a, b) + ((extra,) if extra is not None else ())
    return _run(body, args, out_shape=_sds((M, N), out_dtype), grid=(M // tm, N // tn, nk), in_specs=in_specs,
                out_specs=o_spec, scratch_shapes=[pltpu.VMEM((tm, tn), F32)] if nk > 1 else [], name=name, carry=carry)


def _group_mean(v, E):
    rows, W = v.shape
    if E == W:
        return jnp.mean(v, axis=-1, keepdims=True)
    pieces = []
    if E % LANES == 0:
        for g0 in range(0, W, E):
            m = jnp.mean(v[:, g0:g0 + E], axis=-1, keepdims=True)
            pieces.append(jnp.broadcast_to(m, (rows, E)))
    else:
        lane = lax.broadcasted_iota(jnp.int32, (rows, LANES), 1)
        for c0 in range(0, W, LANES):
            vc = v[:, c0:c0 + LANES]
            acc = jnp.zeros((rows, LANES), F32)
            for s0 in range(0, LANES, E):
                msk = (lane >= s0) & (lane < s0 + E)
                m = jnp.sum(jnp.where(msk, vc, 0.0), axis=-1, keepdims=True) * (1.0 / E)
                acc = jnp.where(msk, m, acc)
            pieces.append(acc)
    return jnp.concatenate(pieces, axis=-1)


def _fold_groups(t, E):
    W = t.shape[1]
    step = max(E, LANES)
    acc = t[:, 0:step]
    for c0 in range(step, W, step):
        acc = acc + t[:, c0:c0 + step]
    sh = LANES // 2
    while sh >= E:
        acc = acc + pltpu.roll(acc, sh, 1)
        sh //= 2
    return acc


def _norm_fwd(x, g, *, E, W, cb=0, out_dtype, name):
    M = x.shape[0]
    tm = _tile(M, 512, 8)

    def body(x_ref, g_ref, o_ref):
        xv = x_ref[...]
        r = lax.rsqrt(_group_mean(xv * xv, E) + EPS)
        o_ref[...] = (xv * r * g_ref[...]).astype(out_dtype)

    return _run(body, (x, g), out_shape=_sds((M, W), out_dtype), grid=(M // tm,),
                in_specs=[pl.BlockSpec((tm, W), lambda i: (i, cb)), pl.BlockSpec((1, W), lambda i: (0, 0))],
                out_specs=pl.BlockSpec((tm, W), lambda i: (i, 0)), name=name)


def _norm_bwd(x, g, dy, *, E, W, cb=0, res=None, fold=None, out_dtype=F32, low=False, name):
    M = x.shape[0]
    tm = _tile(M, 512, 8)
    n = M // tm
    gw = W if fold is None else max(fold, LANES)

    def body(x_ref, g_ref, dy_ref, *rest):
        rest = list(rest)
        r_ref = rest.pop(0) if res is not None else None
        dx_ref = rest.pop(0)
        dxl_ref = rest.pop(0) if low else None
        dg_ref, acc_ref = rest
        i = pl.program_id(0)
        xv = x_ref[...]
        r = lax.rsqrt(_group_mean(xv * xv, E) + EPS)
        xh = xv * r
        dyv = dy_ref[...].astype(F32)
        dyg = dyv * g_ref[...]
        dx = r * (dyg - xh * _group_mean(dyg * xh, E))
        if res is not None:
            dx = dx + r_ref[...]
        dx_ref[...] = dx.astype(out_dtype)
        if low:
            dxl_ref[...] = dx.astype(MXU_DTYPE)
        part = jnp.sum(dyv * xh, axis=0, keepdims=True)

        @pl.when(i == 0)
        def _():
            acc_ref[...] = part

        @pl.when(i > 0)
        def _():
            acc_ref[...] += part

        @pl.when(i == n - 1)
        def _():
            t = acc_ref[...]
            dg_ref[...] = t if fold is None else _fold_groups(t, fold)

    blk = pl.BlockSpec((tm, W), lambda i: (i, 0))
    in_specs = [pl.BlockSpec((tm, W), lambda i: (i, cb)), pl.BlockSpec((1, W), lambda i: (0, 0)), blk]
    args = [x, g, dy]
    if res is not None:
        in_specs.append(blk)
        args.append(res)
    lows = ((_sds((M, W), MXU_DTYPE),), (blk,)) if low else ((), ())
    return _run(body, tuple(args), out_shape=(_sds((M, W), out_dtype),) + lows[0] + (_sds((1, gw), F32),), grid=(n,),
                in_specs=in_specs, out_specs=(blk,) + lows[1] + (pl.BlockSpec((1, gw), lambda i: (0, 0)),),
                scratch_shapes=[pltpu.VMEM((1, W), F32)], name=name)


def _attn_tables(S, tq):
    nq = S // tq

    def body(o_ref):
        i = pl.program_id(0)
        d = (i * tq + lax.broadcasted_iota(jnp.int32, (tq, S), 0)) - lax.broadcasted_iota(jnp.int32, (tq, S), 1)
        cnt = jnp.zeros((tq, S), jnp.int32)
        for dil in DILATIONS:
            hit = (d <= DIL_STEPS * dil) if dil == 1 else (((d & (dil - 1)) == 0) & (d <= DIL_STEPS * dil))
            cnt = cnt + hit.astype(jnp.int32)
        ok = (d >= 0) & (cnt > 0)
        logm = jnp.where(cnt == 3, math.log(3.0), jnp.where(cnt == 2, math.log(2.0), 0.0))
        o_ref[0] = jnp.where(ok, logm, NEG).astype(F32)

    return _run(body, (), out_shape=_sds((nq, tq, S), F32), grid=(nq,), in_specs=[],
                out_specs=pl.BlockSpec((1, tq, S), lambda i: (i, 0, 0)), name="dilated_attn_tables")


def _alibi_slope(h):
    return 2.0 ** (-8.0 / ATT_HEADS * (h + 1))


def _key_positions(Sk):
    kpos = lax.broadcasted_iota(jnp.int32, (Sk, LANES), 0)
    return (kpos >> 8).astype(F32).astype(MXU_DTYPE), (kpos & 255).astype(F32).astype(MXU_DTYPE)


def _score_operands(q2s, k2, pos_hi, pos_lo, lane, sub, h):
    own = (lane < ATT_HEAD_DIM) if sub == 0 else (lane >= ATT_HEAD_DIM)
    spare = ATT_HEAD_DIM if sub == 0 else 0
    slope = _alibi_slope(h)
    terms = jnp.where(lane == spare, slope * 256.0, jnp.where(lane == spare + 1, slope, 0.0)).astype(q2s.dtype)
    qa = jnp.where(own, q2s, terms)
    ka = jnp.where(lane == spare, pos_hi, jnp.where(lane == spare + 1, pos_lo, k2))
    return own, qa, ka


def _key_extents(S, tq):
    ext = min(ATT_KEY_EXTENT, S)
    return ext, ext // tq, S // ext


def _attn_fwd(tab, qn, kn, z, *, B, S, carry=None, name):
    T = B * S
    tq = tab.shape[1]
    nq = S // tq
    ext, per, n_ext = _key_extents(S, tq)
    scale = 1.0 / math.sqrt(ATT_HEAD_DIM)
    dn_t = (((1,), (1,)), ((), ()))

    def body(tab_ref, q_ref, k_ref, v_ref, y_ref, lse_ref):
        i = pl.program_id(1)
        lane = lax.broadcasted_iota(jnp.int32, (1, LANES), 1)
        lo = lane < ATT_HEAD_DIM

        def visit(Sk):
            base = tab_ref[0, :, :Sk]
            pos_hi, pos_lo = _key_positions(Sk)
            lse_blk = jnp.zeros((tq, LANES), F32)
            for p in range(ATT_HEADS // 2):
                cs = slice(p * LANES, (p + 1) * LANES)
                q2s = q_ref[:, cs].astype(MXU_DTYPE) * scale
                k2 = k_ref[:Sk, cs].astype(MXU_DTYPE)
                v2 = v_ref[:Sk, cs].astype(MXU_DTYPE)
                outs = []
                for sub in range(2):
                    h = 2 * p + sub
                    _, qa, ka = _score_operands(q2s, k2, pos_hi, pos_lo, lane, sub, h)
                    s = lax.dot_general(qa, ka, dn_t, preferred_element_type=F32) + base
                    mx = jnp.max(s, axis=1, keepdims=True)
                    e = jnp.exp(s - mx)
                    l = jnp.sum(e, axis=1, keepdims=True)
                    outs.append(jnp.dot(e.astype(MXU_DTYPE), v2, preferred_element_type=F32) / l)
                    lse_blk = jnp.where(lane == h, mx + jnp.log(l), lse_blk)
                y_ref[:, cs] = jnp.where(lo, outs[0], outs[1])
            lse_ref[...] = lse_blk

        for e in range(n_ext):
            pl.when(i // per == e)(functools.partial(visit, (e + 1) * ext))

    return _run(
        body, (tab, qn, kn, z), out_shape=(_sds((T, ATT_W), F32), _sds((T, LANES), F32)), grid=(B, nq),
        in_specs=[pl.BlockSpec((1, tq, S), lambda b, i: (i, 0, 0)),
                  pl.BlockSpec((tq, ATT_W), lambda b, i: (b * nq + i, 0)),
                  pl.BlockSpec((S, ATT_W), lambda b, i: (b, 0)),
                  pl.BlockSpec((S, ATT_W), lambda b, i: (b, 2))],
        out_specs=(pl.BlockSpec((tq, ATT_W), lambda b, i: (b * nq + i, 0)),
                   pl.BlockSpec((tq, LANES), lambda b, i: (b * nq + i, 0))),
        name=name, carry=carry)


def _attn_bwd(tab, qn, kn, z, y, lse, dycat, *, B, S, carry=None, name):
    T = B * S
    tq = tab.shape[1]
    nq = S // tq
    ext, per, n_ext = _key_extents(S, tq)
    scale = 1.0 / math.sqrt(ATT_HEAD_DIM)
    dn_t = (((1,), (1,)), ((), ()))
    dn_o = (((0,), (0,)), ((), ()))

    def body(tab_ref, q_ref, k_ref, v_ref, y_ref, lse_ref, dy_ref, dq_ref, dk_ref, dv_ref):
        i = pl.program_id(1)

        @pl.when(i == 0)
        def _():
            dk_ref[...] = jnp.zeros_like(dk_ref)
            dv_ref[...] = jnp.zeros_like(dv_ref)

        lane = lax.broadcasted_iota(jnp.int32, (1, LANES), 1)
        lo = lane < ATT_HEAD_DIM

        def visit(Sk):
            base = tab_ref[0, :, :Sk]
            pos_hi, pos_lo = _key_positions(Sk)
            lse_blk = lse_ref[...]
            for p in range(ATT_HEADS // 2):
                cs = slice(p * LANES, (p + 1) * LANES)
                q2 = q_ref[:, cs].astype(MXU_DTYPE)
                q2s = q2 * scale
                k2 = k_ref[:Sk, cs].astype(MXU_DTYPE)
                v2 = v_ref[:Sk, cs].astype(MXU_DTYPE)
                do2 = dy_ref[:, cs]
                doy = do2 * y_ref[:, cs]
                do2m = do2.astype(MXU_DTYPE)
                dqs, dks, dvs = [], [], []
                for sub in range(2):
                    h = 2 * p + sub
                    own, qa, ka = _score_operands(q2s, k2, pos_hi, pos_lo, lane, sub, h)
                    s = lax.dot_general(qa, ka, dn_t, preferred_element_type=F32) + base
                    lse_h = jnp.sum(jnp.where(lane == h, lse_blk, 0.0), axis=1, keepdims=True)
                    pr = jnp.exp(s - lse_h)
                    dsum = jnp.sum(jnp.where(own, doy, 0.0), axis=1, keepdims=True)
                    dom = jnp.where(own, do2m, jnp.zeros_like(do2m))
                    dp = lax.dot_general(dom, v2, dn_t, preferred_element_type=F32)
                    ds = (pr * (dp - dsum)).astype(MXU_DTYPE)
                    dqs.append(jnp.dot(ds, k2, preferred_element_type=F32))
                    dks.append(lax.dot_general(ds, q2, dn_o, preferred_element_type=F32))
                    dvs.append(lax.dot_general(pr.astype(MXU_DTYPE), do2m, dn_o, preferred_element_type=F32))
                dq_ref[:, cs] = jnp.where(lo, dqs[0], dqs[1]) * scale
                dk_ref[:Sk, cs] += jnp.where(lo, dks[0], dks[1]) * scale
                dv_ref[:Sk, cs] += jnp.where(lo, dvs[0], dvs[1])

        for e in range(n_ext):
            pl.when(i // per == e)(functools.partial(visit, (e + 1) * ext))

    qblk = pl.BlockSpec((tq, ATT_W), lambda b, i: (b * nq + i, 0))
    sblk = pl.BlockSpec((S, ATT_W), lambda b, i: (b, 0))
    return _run(
        body, (tab, qn, kn, z, y, lse, dycat), out_shape=(_sds((T, ATT_W), F32),) * 3, grid=(B, nq),
        in_specs=[pl.BlockSpec((1, tq, S), lambda b, i: (i, 0, 0)), qblk, sblk,
                  pl.BlockSpec((S, ATT_W), lambda b, i: (b, 2)), qblk,
                  pl.BlockSpec((tq, LANES), lambda b, i: (b * nq + i, 0)), qblk],
        out_specs=(qblk, sblk, sblk), name=name, carry=carry)


def _sigmoid_pair(x):
    en = jnp.exp(-jnp.abs(x))
    big = 1.0 / (1.0 + en)
    small = en * big
    pos = x >= 0
    return jnp.where(pos, big, small), jnp.where(pos, small, big)


def _chunk_scan(x, pos, reverse):
    n = x.shape[0]
    sh = 1
    while sh < HG_CHUNK:
        if reverse:
            x = x + jnp.where(pos < HG_CHUNK - sh, pltpu.roll(x, n - sh, 0), 0.0)
        else:
            x = x + jnp.where(pos >= sh, pltpu.roll(x, sh, 0), 0.0)
        sh *= 2
    return x


def _hgrn_pair_table(tb):
    def body(o_ref):
        t = lax.broadcasted_iota(jnp.int32, (tb, tb), 0)
        s = lax.broadcasted_iota(jnp.int32, (tb, tb), 1)
        o_ref[...] = jnp.where((t // HG_CHUNK) == (s // HG_CHUNK), (t // HG_SUB) - (s // HG_SUB), 0).astype(F32)

    return _pcall(body, out_shape=_sds((tb, tb), F32), name="hgrn_pair_table")()


def _hgrn_cross_decays(b, pos):
    n = b.shape[0]
    pos8 = pos % HG_SUB
    end = b
    sh = 1
    while sh < HG_SUB:
        end = jnp.where((pos8 & sh) == 0, pltpu.roll(end, n - sh, 0), end)
        sh *= 2
    x = jnp.where(pos >= HG_SUB, jnp.exp(jnp.minimum(b - pltpu.roll(end, HG_SUB, 0), 0.0)), 0.0)
    ys = []
    for j in range(1, HG_CHUNK // HG_SUB):
        end_j = end if j == 1 else pltpu.roll(end, n - HG_SUB * (j - 1), 0)
        ys.append(jnp.where(pos < HG_CHUNK - HG_SUB * j, jnp.exp(jnp.minimum(end_j - b, 0.0)), 0.0))
    return x, ys


def _hgrn_cross_scores(qt, kts, pair):
    n = qt.shape[0]
    kcat = jnp.concatenate([k.astype(MXU_DTYPE) for k in kts], axis=0)
    p = lax.dot_general(qt.astype(MXU_DTYPE), kcat, (((1,), (1,)), ((), ())), preferred_element_type=F32)
    a = jnp.zeros((n, n), F32)
    for j in range(1, len(kts) + 1):
        a = jnp.where(pair == j, p[:, (j - 1) * n:j * n], a)
    return a, kcat


def _hgrn_gates(qh, fh, lb):
    sq, _ = _sigmoid_pair(qh)
    sg, sgn = _sigmoid_pair(fh)
    f = lb + (1.0 - lb) * sg
    g = jnp.log(jnp.maximum(f, F_MIN))
    kk = (1.0 - lb) * sgn
    return qh * sq, sq, sg, sgn, f, g, kk


def _hgrn_fwd(z, lb, pair, *, B, S, name):
    T = B * S
    tb = _tile(S, 256, HG_CHUNK)
    nc = tb // HG_CHUNK
    nb = S // tb
    dn_t = (((1,), (1,)), ((), ()))
    dn_o = (((0,), (0,)), ((), ()))

    def zcol(first):
        return lambda h, b, i: (b * nb + i, first + h)

    def body(q_ref, f_ref, v_ref, lb_ref, pair_ref, o_ref, st_ref, state):
        i = pl.program_id(2)

        @pl.when(i == 0)
        def _():
            state[...] = jnp.zeros_like(state)

        q, _, _, _, _, g, kk = _hgrn_gates(q_ref[...], f_ref[...], lb_ref[...])
        v = v_ref[...]
        pos = lax.broadcasted_iota(jnp.int32, (tb, LANES), 0) % HG_CHUNK
        pos8 = pos % HG_SUB
        b = _chunk_scan(g, pos, False)
        o = jnp.sum(q * kk, axis=1, keepdims=True) * v
        for d in range(1, HG_SUB):
            qd = pltpu.roll(q, tb - d, 0)
            bd = pltpu.roll(b, tb - d, 0)
            e = jnp.where(pos8 < HG_SUB - d, jnp.exp(jnp.minimum(bd - b, 0.0)), 0.0)
            a = jnp.sum(qd * kk * e, axis=1, keepdims=True)
            o = o + pltpu.roll(a * v, d, 0)
        x, ys = _hgrn_cross_decays(b, pos)
        a_cross, _ = _hgrn_cross_scores(q * x, [kk * y for y in ys], pair_ref[...])
        o = o + jnp.dot(a_cross.astype(MXU_DTYPE), v.astype(MXU_DTYPE), preferred_element_type=F32)
        qe = q * jnp.exp(b)
        chunks = [slice(c * HG_CHUNK, (c + 1) * HG_CHUNK) for c in range(nc)]
        b_last = [jnp.sum(jnp.where(pos[rs] == HG_CHUNK - 1, b[rs], 0.0), axis=0, keepdims=True) for rs in chunks]
        own = [lax.dot_general(v[rs].astype(MXU_DTYPE), (kk[rs] * jnp.exp(bl - b[rs])).astype(MXU_DTYPE), dn_o,
                               preferred_element_type=F32) for rs, bl in zip(chunks, b_last)]
        st = state[...]
        before = []
        for c in range(nc):
            before.append(st)
            st = st * jnp.exp(b_last[c]) + own[c]
        state[...] = st
        for c, rs in enumerate(chunks):
            st_ref[c * LANES:(c + 1) * LANES, :] = before[c]
            o_ref[rs, :] = o[rs] + lax.dot_general(qe[rs].astype(MXU_DTYPE), before[c].astype(MXU_DTYPE), dn_t,
                                                   preferred_element_type=F32)

    blk = (tb, LANES)
    return _run(
        body, (z, z, z, lb, pair), out_shape=(_sds((T, HG_W), F32), _sds((T // HG_CHUNK * LANES, HG_W), F32)),
        grid=(HG_HEADS, B, nb),
        in_specs=[pl.BlockSpec(blk, zcol(12)), pl.BlockSpec(blk, zcol(16)), pl.BlockSpec(blk, zcol(20)),
                  pl.BlockSpec((1, LANES), lambda h, b, i: (0, h)), pl.BlockSpec((tb, tb), lambda h, b, i: (0, 0))],
        out_specs=(pl.BlockSpec(blk, lambda h, b, i: (b * nb + i, h)),
                   pl.BlockSpec((nc * LANES, LANES), lambda h, b, i: (b * nb + i, h))),
        scratch_shapes=[pltpu.VMEM((LANES, LANES), F32)], name=name)


def _hgrn_bwd(z, lb, pair, states, do, *, B, S, name):
    T = B * S
    tb = _tile(S, 256, HG_CHUNK)
    nc = tb // HG_CHUNK
    nb = S // tb
    dn_t = (((1,), (1,)), ((), ()))
    dn_o = (((0,), (0,)), ((), ()))

    def body(q_ref, f_ref, v_ref, lb_ref, pair_ref, st_ref, do_ref, dq_ref, df_ref, di_ref, dlb_ref, dstate):
        bi = pl.program_id(1)
        i = pl.program_id(2)

        @pl.when(i == 0)
        def _():
            dstate[...] = jnp.zeros_like(dstate)

        qh = q_ref[...]
        lbv = lb_ref[...]
        q, sq, sg, sgn, f, g, kk = _hgrn_gates(qh, f_ref[...], lbv)
        v = v_ref[...]
        dov = do_ref[...]
        pos = lax.broadcasted_iota(jnp.int32, (tb, LANES), 0) % HG_CHUNK
        pos8 = pos % HG_SUB
        b = _chunk_scan(g, pos, False)

        a0 = jnp.sum(q * kk, axis=1, keepdims=True)
        da0 = jnp.sum(dov * v, axis=1, keepdims=True)
        dv = a0 * dov
        dkk = da0 * q
        dq = da0 * kk
        db = jnp.zeros((tb, LANES), F32)
        for d in range(1, HG_SUB):
            qd = pltpu.roll(q, tb - d, 0)
            bd = pltpu.roll(b, tb - d, 0)
            dod = pltpu.roll(dov, tb - d, 0)
            e = jnp.where(pos8 < HG_SUB - d, jnp.exp(jnp.minimum(bd - b, 0.0)), 0.0)
            a = jnp.sum(qd * kk * e, axis=1, keepdims=True)
            da = jnp.sum(dod * v, axis=1, keepdims=True)
            dv = dv + a * dod
            xe = da * e
            dkk = dkk + xe * qd
            t1 = xe * kk
            wt = t1 * qd
            dq = dq + pltpu.roll(t1, d, 0)
            db = db + pltpu.roll(wt, d, 0) - wt

        n_cross = HG_CHUNK // HG_SUB - 1
        pair = pair_ref[...]
        x, ys = _hgrn_cross_decays(b, pos)
        qt = q * x
        kts = [kk * y for y in ys]
        a_cross, kcat = _hgrn_cross_scores(qt, kts, pair)
        dom = dov.astype(MXU_DTYPE)
        da_all = lax.dot_general(dom, v.astype(MXU_DTYPE), dn_t, preferred_element_type=F32)
        dv = dv + lax.dot_general(a_cross.astype(MXU_DTYPE), dom, dn_o, preferred_element_type=F32)
        gcat = jnp.concatenate([jnp.where(pair == j, da_all, 0.0).astype(MXU_DTYPE) for j in range(1, n_cross + 1)],
                               axis=1)
        dqt = jnp.dot(gcat, kcat, preferred_element_type=F32)
        dkcat = lax.dot_general(gcat, qt.astype(MXU_DTYPE), dn_o, preferred_element_type=F32)
        dq = dq + dqt * x
        u = dqt * qt
        db = db + u
        d_end = -pltpu.roll(u, tb - HG_SUB, 0)
        for j in range(1, n_cross + 1):
            dkt = dkcat[(j - 1) * tb:j * tb]
            dkk = dkk + dkt * ys[j - 1]
            w = dkt * kts[j - 1]
            db = db - w
            d_end = d_end + (w if j == 1 else pltpu.roll(w, HG_SUB * (j - 1), 0))
        sh = 1
        while sh < HG_SUB:
            d_end = d_end + jnp.where(pos8 >= sh, pltpu.roll(d_end, sh, 0), 0.0)
            sh *= 2
        db = db + jnp.where(pos8 == HG_SUB - 1, d_end, 0.0)

        eb = jnp.exp(b)
        qe = q * eb
        chunks = [slice(c * HG_CHUNK, (c + 1) * HG_CHUNK) for c in range(nc)]
        b_lasts = [jnp.sum(jnp.where(pos[rs] == HG_CHUNK - 1, b[rs], 0.0), axis=0, keepdims=True) for rs in chunks]
        own = [lax.dot_general(dom[rs], qe[rs].astype(MXU_DTYPE), dn_o, preferred_element_type=F32) for rs in chunks]
        dst = dstate[...]
        after = [None] * nc
        for c in range(nc - 1, -1, -1):
            after[c] = dst
            dst = dst * jnp.exp(b_lasts[c]) + own[c]
        dstate[...] = dst
        dq_c, dkk_c, dv_c, db_c = [None] * nc, [None] * nc, [None] * nc, [None] * nc
        for c, rs in enumerate(chunks):
            st0 = st_ref[c * LANES:(c + 1) * LANES, :]
            eb_last = jnp.exp(b_lasts[c])
            er = jnp.exp(b_lasts[c] - b[rs])
            ke = kk[rs] * er
            dstm = after[c].astype(MXU_DTYPE)
            dqe = jnp.dot(dom[rs], st0.astype(MXU_DTYPE), preferred_element_type=F32)
            dv_c[c] = dv[rs] + lax.dot_general(ke.astype(MXU_DTYPE), dstm, dn_t, preferred_element_type=F32)
            dke = jnp.dot(v[rs].astype(MXU_DTYPE), dstm, preferred_element_type=F32)
            d_eb_last = jnp.sum(after[c] * st0, axis=0, keepdims=True)
            dq_c[c] = dq[rs] + dqe * eb[rs]
            dkk_c[c] = dkk[rs] + dke * er
            dkeke = dke * ke
            db_last = jnp.sum(dkeke, axis=0, keepdims=True) + d_eb_last * eb_last
            db_c[c] = db[rs] + dqe * qe[rs] - dkeke + jnp.where(pos[rs] == HG_CHUNK - 1, db_last, 0.0)
        dq = jnp.concatenate(dq_c, axis=0)
        dkk = jnp.concatenate(dkk_c, axis=0)
        dv = jnp.concatenate(dv_c, axis=0)
        dg = _chunk_scan(jnp.concatenate(db_c, axis=0), pos, True)

        dfv = dg * jnp.where(f > F_MIN, 1.0 / f, 0.0)
        dsg = (dfv - dkk) * (1.0 - lbv)
        dq_ref[...] = dq * (sq * (1.0 + qh * (1.0 - sq)))
        df_ref[...] = dsg * sg * sgn
        di_ref[...] = dv
        part = jnp.sum(dfv * sgn - dkk * sgn, axis=0, keepdims=True)

        @pl.when((bi == 0) & (i == 0))
        def _():
            dlb_ref[...] = part

        @pl.when((bi > 0) | (i > 0))
        def _():
            dlb_ref[...] += part

    def rev(first):
        return lambda h, b, i: (b * nb + (nb - 1 - i), first + h)

    blk = (tb, LANES)
    oblk = pl.BlockSpec(blk, rev(0))
    return _run(
        body, (z, z, z, lb, pair, states, do), out_shape=(_sds((T, HG_W), F32),) * 3 + (_sds((1, HG_W), F32),),
        grid=(HG_HEADS, B, nb),
        in_specs=[pl.BlockSpec(blk, rev(12)), pl.BlockSpec(blk, rev(16)), pl.BlockSpec(blk, rev(20)),
                  pl.BlockSpec((1, LANES), lambda h, b, i: (0, h)), pl.BlockSpec((tb, tb), lambda h, b, i: (0, 0)),
                  pl.BlockSpec((nc * LANES, LANES), rev(0)), oblk],
        out_specs=(oblk, oblk, oblk, pl.BlockSpec((1, LANES), lambda h, b, i: (0, h))),
        scratch_shapes=[pltpu.VMEM((LANES, LANES), F32)], name=name)


def _hgout_fwd(o, z, g, *, name):
    T = o.shape[0]
    tm = _tile(T, 512, 8)

    def body(o_ref, gh_ref, g_ref, y_ref):
        ov = o_ref[...]
        gh = gh_ref[...]
        r = lax.rsqrt(_group_mean(ov * ov, HG_HEAD_DIM) + EPS)
        sg, _ = _sigmoid_pair(gh)
        y_ref[...] = ov * r * g_ref[...] * (gh * sg)

    blk = pl.BlockSpec((tm, HG_W), lambda i: (i, 0))
    return _run(body, (o, z, g), out_shape=_sds((T, HG_W), F32), grid=(T // tm,),
                in_specs=[blk, pl.BlockSpec((tm, HG_W), lambda i: (i, 6)), pl.BlockSpec((1, HG_W), lambda i: (0, 0))],
                out_specs=blk, name=name)


def _hgout_bwd(o, z, g, dycat, *, name):
    T = o.shape[0]
    tm = _tile(T, 512, 8)

    def body(o_ref, gh_ref, g_ref, dy_ref, do_ref, dgh_ref, dg_ref):
        i = pl.program_id(0)
        ov = o_ref[...]
        gh = gh_ref[...]
        gv = g_ref[...]
        dy = dy_ref[...]
        r = lax.rsqrt(_group_mean(ov * ov, HG_HEAD_DIM) + EPS)
        xh = ov * r
        sg, _ = _sigmoid_pair(gh)
        dn = dy * (gh * sg)
        dgh_ref[...] = dy * xh * gv * (sg * (1.0 + gh * (1.0 - sg)))
        dng = dn * gv
        do_ref[...] = r * (dng - xh * _group_mean(dng * xh, HG_HEAD_DIM))
        part = jnp.sum(dn * xh, axis=0, keepdims=True)

        @pl.when(i == 0)
        def _():
            dg_ref[...] = part

        @pl.when(i > 0)
        def _():
            dg_ref[...] += part

    blk = pl.BlockSpec((tm, HG_W), lambda i: (i, 0))
    row = pl.BlockSpec((1, HG_W), lambda i: (0, 0))
    return _run(body, (o, z, g, dycat),
                out_shape=(_sds((T, HG_W), F32), _sds((T, HG_W), F32), _sds((1, HG_W), F32)), grid=(T // tm,),
                in_specs=[blk, pl.BlockSpec((tm, HG_W), lambda i: (i, 6)), row,
                          pl.BlockSpec((tm, HG_W), lambda i: (i, 1))],
                out_specs=(blk, blk, row), name=name)


def _lower_bounds_fwd(hg_lb):
    L = hg_lb.shape[0]

    def body(x_ref, o_ref):
        rows = [x_ref[l:l + 1, :] for l in range(L)]
        mx = functools.reduce(jnp.maximum, rows)
        es = [jnp.exp(r - mx) for r in rows]
        tot = functools.reduce(jnp.add, es)
        acc = jnp.zeros_like(tot)
        for l in range(L):
            acc = acc + es[l] / tot
            o_ref[l:l + 1, :] = acc - es[0] / tot

    return _pcall(body, out_shape=_sds(hg_lb.shape, F32), name="lower_bounds_fwd")(hg_lb)


def _lower_bounds_bwd(hg_lb, dlower):
    L = hg_lb.shape[0]

    def body(x_ref, d_ref, o_ref):
        rows = [x_ref[l:l + 1, :] for l in range(L)]
        ds = [d_ref[l:l + 1, :] for l in range(L)]
        mx = functools.reduce(jnp.maximum, rows)
        es = [jnp.exp(r - mx) for r in rows]
        tot = functools.reduce(jnp.add, es)
        ps = [e / tot for e in es]
        dps = []
        for j in range(L):
            t = functools.reduce(jnp.add, ds[j:])
            if j == 0:
                t = t - functools.reduce(jnp.add, ds)
            dps.append(t)
        inner = functools.reduce(jnp.add, [p * dp for p, dp in zip(ps, dps)])
        for j in range(L):
            o_ref[j:j + 1, :] = ps[j] * (dps[j] - inner)

    return _pcall(body, out_shape=_sds(hg_lb.shape, F32), name="lower_bounds_bwd")(hg_lb, dlower)


def _memattn_fwd(qm, km, kv, *, B, S, NM, name):
    T, D = qm.shape
    E = D // MEM_HEADS
    tq = _tile(S, 512, 8)
    nq = S // tq
    scale = 1.0 / math.sqrt(E)

    def body(q_ref, k_ref, v_ref, o_ref):
        for h in range(MEM_HEADS):
            cs = slice(h * E, (h + 1) * E)
            s = lax.dot_general(q_ref[:, cs].astype(MXU_DTYPE), k_ref[:, cs].astype(MXU_DTYPE),
                                (((1,), (1,)), ((), ())), preferred_element_type=F32) * scale
            e = jnp.exp(s - jnp.max(s, axis=1, keepdims=True))
            pr = e / jnp.sum(e, axis=1, keepdims=True)
            o_ref[:, cs] = jnp.dot(pr.astype(MXU_DTYPE), v_ref[:, cs].astype(MXU_DTYPE),
                                   preferred_element_type=F32).astype(o_ref.dtype)

    return _run(
        body, (qm, km, kv), out_shape=_sds((T, D), MXU_DTYPE), grid=(B, nq),
        in_specs=[pl.BlockSpec((tq, D), lambda b, i: (b * nq + i, 0)), pl.BlockSpec((NM, D), lambda b, i: (b, 0)),
                  pl.BlockSpec((NM, D), lambda b, i: (b, 1))],
        out_specs=pl.BlockSpec((tq, D), lambda b, i: (b * nq + i, 0)), name=name)


def _memattn_bwd(qm, km, kv, do, *, B, S, NM, name):
    T, D = qm.shape
    E = D // MEM_HEADS
    tq = _tile(S, 512, 8)
    nq = S // tq
    scale = 1.0 / math.sqrt(E)

    def body(q_ref, k_ref, v_ref, do_ref, dq_ref, dk_ref, dv_ref):
        i = pl.program_id(1)

        @pl.when(i == 0)
        def _():
            dk_ref[...] = jnp.zeros_like(dk_ref)
            dv_ref[...] = jnp.zeros_like(dv_ref)

        for h in range(MEM_HEADS):
            cs = slice(h * E, (h + 1) * E)
            qh = q_ref[:, cs].astype(MXU_DTYPE)
            kh = k_ref[:, cs].astype(MXU_DTYPE)
            vh = v_ref[:, cs].astype(MXU_DTYPE)
            doh = do_ref[:, cs].astype(MXU_DTYPE)
            s = lax.dot_general(qh, kh, (((1,), (1,)), ((), ())), preferred_element_type=F32) * scale
            e = jnp.exp(s - jnp.max(s, axis=1, keepdims=True))
            pr = e / jnp.sum(e, axis=1, keepdims=True)
            dp = lax.dot_general(doh, vh, (((1,), (1,)), ((), ())), preferred_element_type=F32)
            ds = (pr * (dp - jnp.sum(dp * pr, axis=1, keepdims=True))).astype(MXU_DTYPE)
            dq_ref[:, cs] = jnp.dot(ds, kh, preferred_element_type=F32) * scale
            dk_ref[:, cs] += lax.dot_general(ds, qh, (((0,), (0,)), ((), ())), preferred_element_type=F32) * scale
            dv_ref[:, cs] += lax.dot_general(pr.astype(MXU_DTYPE), doh, (((0,), (0,)), ((), ())),
                                             preferred_element_type=F32)

    qblk = pl.BlockSpec((tq, D), lambda b, i: (b * nq + i, 0))
    mblk = pl.BlockSpec((NM, D), lambda b, i: (b, 0))
    return _run(
        body, (qm, km, kv, do), out_shape=(_sds((T, D), F32), _sds((B * NM, D), F32), _sds((B * NM, D), F32)),
        grid=(B, nq), in_specs=[qblk, mblk, pl.BlockSpec((NM, D), lambda b, i: (b, 1)), qblk],
        out_specs=(qblk, mblk, mblk), name=name)


def _loss_head(y, tgt):
    T, D = y.shape
    tm = _tile(T, 512, 8)

    def body(y_ref, t_ref, dy_ref, dyl_ref, l_ref):
        i = pl.program_id(0)
        err = y_ref[...] - t_ref[...]
        dy_ref[...] = err * (1.0 / D)
        dyl_ref[...] = (err * (1.0 / D)).astype(MXU_DTYPE)
        part = jnp.sum(jnp.sum(err * err, axis=1, keepdims=True), axis=0, keepdims=True) * (0.5 / D)

        @pl.when(i == 0)
        def _():
            l_ref[...] = part

        @pl.when(i > 0)
        def _():
            l_ref[...] += part

    blk = pl.BlockSpec((tm, D), lambda i: (i, 0))
    return _run(body, (y, tgt), out_shape=(_sds((T, D), F32), _sds((T, D), MXU_DTYPE), _sds((1, 1), F32)),
                grid=(T // tm,), in_specs=[blk, blk], out_specs=(blk, blk, pl.BlockSpec((1, 1), lambda i: (0, 0))),
                name="loss_head")


def _adamw(w, g, m, v):
    m = ADAM_B1 * m + (1.0 - ADAM_B1) * g
    v = ADAM_B2 * v + (1.0 - ADAM_B2) * jnp.square(g)
    m_hat = m / (1.0 - ADAM_B1 ** ADAM_STEP)
    v_hat = v / (1.0 - ADAM_B2 ** ADAM_STEP)
    delta = -ADAM_LR * (m_hat / (jnp.sqrt(v_hat) + ADAM_EPS) + ADAM_WD * w)
    return delta, m, v


def _sum_adamw(recv, w, m, v):
    _, R, C = recv.shape
    tr = _tile(R, 384, 8)

    def body(r_ref, w_ref, m_ref, v_ref, g_ref, d_ref, nm_ref, nv_ref):
        g = r_ref[0].astype(F32)
        for j in range(1, N_DEV):
            g = g + r_ref[j].astype(F32)
        g_ref[...] = g
        d_ref[...], nm_ref[...], nv_ref[...] = _adamw(w_ref[...], g, m_ref[...], v_ref[...])

    blk = pl.BlockSpec((tr, C), lambda i: (i, 0))
    return _run(body, (recv, w, m, v), out_shape=(_sds((R, C), F32),) * 4, grid=(R // tr,),
                in_specs=[pl.BlockSpec((N_DEV, tr, C), lambda i: (0, i, 0)), blk, blk, blk],
                out_specs=(blk,) * 4, name="grad_sum_adamw")


def _small_sum_adamw(gp, w, m, v):
    NR, C = gp.shape
    MESH = pl.DeviceIdType.MESH

    def body(gp_ref, w_ref, m_ref, v_ref, g_ref, d_ref, nm_ref, nv_ref, gath, send_sems, recv_sems):
        x, y, c = _mesh_pos()
        me = 4 * x + 2 * y + c
        gath[me] = gp_ref[...]
        copies = []
        for k in range(1, N_DEV):
            peer = (_flip(x, k & 4), _flip(y, k & 2), _flip(c, k & 1))
            cp = pltpu.make_async_remote_copy(src_ref=gp_ref, dst_ref=gath.at[me], send_sem=send_sems.at[k - 1],
                                              recv_sem=recv_sems.at[k - 1], device_id=peer, device_id_type=MESH)
            cp.start()
            copies.append(cp)
        for cp in copies:
            cp.wait()
        g = gath[0]
        for j in range(1, N_DEV):
            g = g + gath[j]
        g_ref[...] = g
        d_ref[...], nm_ref[...], nv_ref[...] = _adamw(w_ref[...], g, m_ref[...], v_ref[...])

    vm = pl.BlockSpec(memory_space=pltpu.VMEM)
    return _pcall(body, out_shape=(_sds((NR, C), F32),) * 4, in_specs=[vm] * 4, out_specs=(vm,) * 4,
                  scratch_shapes=[pltpu.VMEM((N_DEV, NR, C), F32), pltpu.SemaphoreType.DMA((7,)),
                                  pltpu.SemaphoreType.DMA((7,))], name="small_sum_adamw")(gp, w, m, v)


def _pack(shards, names, dtype=F32):
    return jnp.concatenate([shards[n].reshape(-1, FLAT_W) for n in names], axis=0).astype(dtype)


def _unpack(flat, names, shapes):
    out, off = {}, 0
    for n in names:
        rows = math.prod(shapes[n]) // FLAT_W
        out[n] = flat[off:off + rows].reshape(shapes[n])
        off += rows
    return out


def _gathered(gath, names, shapes):
    out, off = {}, 0
    for n in names:
        r, c = shapes[n]
        rows = r * c // FLAT_W
        blk = gath[:, off:off + rows].reshape(N_DEV, r, c)
        out[n] = blk.transpose(1, 0, 2).reshape(r, N_DEV * c) if n in COL_SHARDED else blk.reshape(N_DEV * r, c)
        off += rows
    return out


def _scatter_blocks(grads, names, shapes, dtype):
    parts = []
    for n in names:
        r, c = shapes[n]
        g = grads[n]
        g = g.reshape(r, N_DEV, c).transpose(1, 0, 2) if n in COL_SHARDED else g.reshape(N_DEV, r, c)
        parts.append(g.reshape(N_DEV, -1, FLAT_W).astype(dtype))
    return jnp.concatenate(parts, axis=1)


def _pack_small(d, prefix, loss=None):
    parts = []
    for n in SMALL:
        a = d[prefix + n].reshape(-1).astype(F32)
        pad = (-a.shape[0]) % LANES
        parts.append(jnp.pad(a, (0, pad)) if pad else a)
    tail = jnp.zeros((LANES,), F32)
    if loss is not None:
        tail = tail.at[0].set(loss)
    flat = jnp.concatenate(parts + [tail])
    pad = (-flat.shape[0]) % (8 * LANES)
    return jnp.pad(flat, (0, pad)).reshape(-1, LANES)


def _unpack_small(packed, shapes):
    flat = packed.reshape(-1)
    out, off = {}, 0
    for n in SMALL:
        size = math.prod(shapes[n])
        out[n] = flat[off:off + size].reshape(shapes[n])
        off += size + (-size) % LANES
    return out, flat[off]


def _row(a):
    return a.reshape(1, -1).astype(F32)


def _layer_fwd(x, memf, w_in, shards, next_in, shapes, sp, lb, tab, dims):
    B, S, NM = dims
    D = x.shape[1]
    W = {"w_in": w_in}
    h1 = _norm_fwd(x, sp["norm1_g"], E=D, W=D, out_dtype=MXU_DTYPE, name="norm1_fwd")
    z, gath = _matmul(h1, w_in, carry=("gather", _pack(shards, GATHER_MIX, MXU_DTYPE)), name="in_proj")
    W.update(_gathered(gath, GATHER_MIX, shapes))
    qn = _norm_fwd(z, sp["attn_qn_g"], E=ATT_HEAD_DIM, W=ATT_W, cb=0, out_dtype=MXU_DTYPE, name="qnorm_fwd")
    kn = _norm_fwd(z, sp["attn_kn_g"], E=ATT_HEAD_DIM, W=ATT_W, cb=1, out_dtype=MXU_DTYPE, name="knorm_fwd")
    (ya, lse), gath = _attn_fwd(tab["attn"], qn, kn, z, B=B, S=S, carry=("gather", _pack(shards, GATHER_FF, MXU_DTYPE)),
                                name="dilated_attn_fwd")
    W.update(_gathered(gath, GATHER_FF, shapes))
    o, states = _hgrn_fwd(z, lb, tab["hgrn"], B=B, S=S, name="hgrn_fwd")
    yh = _hgout_fwd(o, z, sp["hg_onorm_g"], name="hgrn_out_fwd")
    ycat = jnp.concatenate([ya, yh], axis=1).astype(MXU_DTYPE)
    x1 = _matmul(ycat, W["w_out"], epi="add", extra=x, name="out_proj")
    h2 = _norm_fwd(x1, sp["norm2_g"], E=D, W=D, out_dtype=MXU_DTYPE, name="norm2_fwd")
    qmp = _matmul(h2, W["w_mq"], name="mq_proj")
    qm = _norm_fwd(qmp, sp["mq_norm_g"], E=D // MEM_HEADS, W=D, out_dtype=MXU_DTYPE, name="mqnorm_fwd")
    mn = _norm_fwd(memf, sp["mem_norm_g"], E=D, W=D, out_dtype=MXU_DTYPE, name="memnorm_fwd")
    kv = _matmul(mn, W["w_mkv"], name="mkv_proj")
    km = _norm_fwd(kv, sp["mk_norm_g"], E=D // MEM_HEADS, W=D, cb=0, out_dtype=MXU_DTYPE, name="mknorm_fwd")
    om = _memattn_fwd(qm, km, kv, B=B, S=S, NM=NM, name="mem_attn_fwd")
    x2 = _matmul(om, W["w_mo"], epi="add", extra=x1, name="mo_proj")
    h3 = _norm_fwd(x2, sp["norm3_g"], E=D, W=D, out_dtype=MXU_DTYPE, name="norm3_fwd")
    if next_in is None:
        u, w_in_next = _matmul(h3, W["w_ff1"], epi="relu2", out_dtype=MXU_DTYPE, name="ff1"), None
    else:
        u, gath = _matmul(h3, W["w_ff1"], epi="relu2", out_dtype=MXU_DTYPE,
                          carry=("gather", _pack({"w_in": next_in}, GATHER_IN, MXU_DTYPE)), name="ff1")
        w_in_next = _gathered(gath, GATHER_IN, shapes)["w_in"]
    x3 = _matmul(u, W["w_ff2"], epi="add", extra=x2, name="ff2")
    saved = dict(x=x, h1=h1, z=z, qn=qn, kn=kn, ya=ya, lse=lse, o=o, states=states, ycat=ycat, x1=x1, h2=h2,
                 qmp=qmp, qm=qm, mn=mn, kv=kv, km=km, om=om, x2=x2, h3=h3, u=u)
    return x3, saved, W, w_in_next


def _layer_bwd(dx3, dx3_low, s, memf, W, shapes, sp, lb, tab, dims):
    B, S, NM = dims
    D = dx3.shape[1]
    E_M = D // MEM_HEADS
    gw, gs, recv = {}, {}, {}

    def blocks(names):
        return ("scatter", _scatter_blocks(gw, names, shapes, MXU_DTYPE))

    da = _matmul(dx3_low, W["w_ff2"], tb=True, epi="relu2grad", extra=s["u"], out_dtype=MXU_DTYPE, name="ff2_dgrad")
    gw["w_ff2"] = _matmul(s["u"], dx3_low, ta=True, out_dtype=MXU_DTYPE, name="ff2_wgrad")
    dh3, recv[SCATTER_A] = _matmul(da, W["w_ff1"], tb=True, out_dtype=MXU_DTYPE, carry=blocks(SCATTER_A),
                                   name="ff1_dgrad")
    gw["w_ff1"] = _matmul(s["h3"], da, ta=True, out_dtype=MXU_DTYPE, name="ff1_wgrad")
    dx2, dx2_low, gs["norm3_g"] = _norm_bwd(s["x2"], sp["norm3_g"], dh3, E=D, W=D, res=dx3, low=True, name="norm3_bwd")
    dom = _matmul(dx2_low, W["w_mo"], tb=True, out_dtype=MXU_DTYPE, name="mo_dgrad")
    gw["w_mo"] = _matmul(s["om"], dx2_low, ta=True, out_dtype=MXU_DTYPE, name="mo_wgrad")
    dqm, dkm, dvm = _memattn_bwd(s["qm"], s["km"], s["kv"], dom, B=B, S=S, NM=NM, name="mem_attn_bwd")
    dqmp, gs["mq_norm_g"] = _norm_bwd(s["qmp"], sp["mq_norm_g"], dqm, E=E_M, W=D, fold=E_M, out_dtype=MXU_DTYPE,
                                      name="mqnorm_bwd")
    dkmp, gs["mk_norm_g"] = _norm_bwd(s["kv"], sp["mk_norm_g"], dkm, E=E_M, W=D, cb=0, fold=E_M,
                                      out_dtype=MXU_DTYPE, name="mknorm_bwd")
    dkv = jnp.concatenate([dkmp, dvm.astype(MXU_DTYPE)], axis=1)
    dh2 = _matmul(dqmp, W["w_mq"], tb=True, out_dtype=MXU_DTYPE, name="mq_dgrad")
    gw["w_mq"] = _matmul(s["h2"], dqmp, ta=True, out_dtype=MXU_DTYPE, name="mq_wgrad")
    dmn = _matmul(dkv, W["w_mkv"], tb=True, out_dtype=MXU_DTYPE, name="mkv_dgrad")
    gw["w_mkv"] = _matmul(s["mn"], dkv, ta=True, out_dtype=MXU_DTYPE, name="mkv_wgrad")
    _, gs["mem_norm_g"] = _norm_bwd(memf, sp["mem_norm_g"], dmn, E=D, W=D, name="memnorm_bwd")
    dx1, dx1_low, gs["norm2_g"] = _norm_bwd(s["x1"], sp["norm2_g"], dh2, E=D, W=D, res=dx2, low=True, name="norm2_bwd")
    dycat = _matmul(dx1_low, W["w_out"], tb=True, out_dtype=MXU_DTYPE, name="out_dgrad")
    gw["w_out"] = _matmul(s["ycat"], dx1_low, ta=True, out_dtype=MXU_DTYPE, name="out_wgrad")
    do_hg, dgh, gs["hg_onorm_g"] = _hgout_bwd(s["o"], s["z"], sp["hg_onorm_g"], dycat, name="hgrn_out_bwd")
    dqh, dfh, dih, dlb = _hgrn_bwd(s["z"], lb, tab["hgrn"], s["states"], do_hg, B=B, S=S, name="hgrn_bwd")
    (dqn, dkn, dv), recv[SCATTER_B] = _attn_bwd(tab["attn"], s["qn"], s["kn"], s["z"], s["ya"], s["lse"], dycat, B=B, S=S,
                                                 carry=blocks(SCATTER_B), name="dilated_attn_bwd")
    dq, gs["attn_qn_g"] = _norm_bwd(s["z"], sp["attn_qn_g"], dqn, E=ATT_HEAD_DIM, W=ATT_W, cb=0, fold=ATT_HEAD_DIM,
                                    out_dtype=MXU_DTYPE, name="qnorm_bwd")
    dk, gs["attn_kn_g"] = _norm_bwd(s["z"], sp["attn_kn_g"], dkn, E=ATT_HEAD_DIM, W=ATT_W, cb=1, fold=ATT_HEAD_DIM,
                                    out_dtype=MXU_DTYPE, name="knorm_bwd")
    dz = jnp.concatenate([dq, dk] + [t.astype(MXU_DTYPE) for t in (dv, dqh, dfh, dih, dgh)], axis=1)
    gw["w_in"] = _matmul(s["h1"], dz, ta=True, out_dtype=MXU_DTYPE, name="in_wgrad")
    dh1, recv[SCATTER_C] = _matmul(dz, W["w_in"], tb=True, out_dtype=MXU_DTYPE, carry=blocks(SCATTER_C), name="in_dgrad")
    dx0, dx0_low, gs["norm1_g"] = _norm_bwd(s["x"], sp["norm1_g"], dh1, E=D, W=D, res=dx1, low=True, name="norm1_bwd")
    gs["attn_qn_g"] = gs["attn_qn_g"][:, :ATT_HEAD_DIM]
    gs["attn_kn_g"] = gs["attn_kn_g"][:, :ATT_HEAD_DIM]
    return dx0, dx0_low, recv, gs, dlb


def kernel(x, mem, norm1_g, w_in, attn_qn_g, attn_kn_g, hg_lb, hg_onorm_g, w_out, norm2_g, mem_norm_g, w_mq, w_mkv, mq_norm_g, mk_norm_g, w_mo, norm3_g, w_ff1, w_ff2, loss_target, m_norm1_g, m_w_in, m_attn_qn_g, m_attn_kn_g, m_hg_lb, m_hg_onorm_g, m_w_out, m_norm2_g, m_mem_norm_g, m_w_mq, m_w_mkv, m_mq_norm_g, m_mk_norm_g, m_w_mo, m_norm3_g, m_w_ff1, m_w_ff2, v_norm1_g, v_w_in, v_attn_qn_g, v_attn_kn_g, v_hg_lb, v_hg_onorm_g, v_w_out, v_norm2_g, v_mem_norm_g, v_w_mq, v_w_mkv, v_mq_norm_g, v_mk_norm_g, v_w_mo, v_norm3_g, v_w_ff1, v_w_ff2):
    given = dict(norm1_g=norm1_g, w_in=w_in, attn_qn_g=attn_qn_g, attn_kn_g=attn_kn_g, hg_lb=hg_lb, hg_onorm_g=hg_onorm_g, w_out=w_out, norm2_g=norm2_g, mem_norm_g=mem_norm_g, w_mq=w_mq, w_mkv=w_mkv, mq_norm_g=mq_norm_g, mk_norm_g=mk_norm_g, w_mo=w_mo, norm3_g=norm3_g, w_ff1=w_ff1, w_ff2=w_ff2, m_norm1_g=m_norm1_g, m_w_in=m_w_in, m_attn_qn_g=m_attn_qn_g, m_attn_kn_g=m_attn_kn_g, m_hg_lb=m_hg_lb, m_hg_onorm_g=m_hg_onorm_g, m_w_out=m_w_out, m_norm2_g=m_norm2_g, m_mem_norm_g=m_mem_norm_g, m_w_mq=m_w_mq, m_w_mkv=m_w_mkv, m_mq_norm_g=m_mq_norm_g, m_mk_norm_g=m_mk_norm_g, m_w_mo=m_w_mo, m_norm3_g=m_norm3_g, m_w_ff1=m_w_ff1, m_w_ff2=m_w_ff2, v_norm1_g=v_norm1_g, v_w_in=v_w_in, v_attn_qn_g=v_attn_qn_g, v_attn_kn_g=v_attn_kn_g, v_hg_lb=v_hg_lb, v_hg_onorm_g=v_hg_onorm_g, v_w_out=v_w_out, v_norm2_g=v_norm2_g, v_mem_norm_g=v_mem_norm_g, v_w_mq=v_w_mq, v_w_mkv=v_w_mkv, v_mq_norm_g=v_mq_norm_g, v_mk_norm_g=v_mk_norm_g, v_w_mo=v_w_mo, v_norm3_g=v_norm3_g, v_w_ff1=v_w_ff1, v_w_ff2=v_w_ff2)
    B, S, D = x.shape
    NM = mem.shape[1]
    L = w_in.shape[0]
    dims = (B, S, NM)
    shapes = {n: given[n].shape[1:] for n in BIG}
    small_shapes = {n: given[n].shape for n in SMALL}

    def shards(prefix, l):
        return {n: given[prefix + n][l] for n in BIG}

    lower = _lower_bounds_fwd(hg_lb)
    tab = {"attn": _attn_tables(S, _tile(S, ATT_Q_BLOCK, 8)), "hgrn": _hgrn_pair_table(_tile(S, 256, HG_CHUNK))}
    xf = x.reshape(B * S, D)
    memf = mem.reshape(B * NM, D)

    def small_params(l):
        sp = {"norm1_g": _row(norm1_g[l]), "norm2_g": _row(norm2_g[l]), "norm3_g": _row(norm3_g[l]),
              "mem_norm_g": _row(mem_norm_g[l]), "hg_onorm_g": _row(hg_onorm_g[l]),
              "attn_qn_g": _row(jnp.tile(attn_qn_g[l], ATT_HEADS)), "attn_kn_g": _row(jnp.tile(attn_kn_g[l], ATT_HEADS)),
              "mq_norm_g": _row(jnp.tile(mq_norm_g[l], MEM_HEADS)), "mk_norm_g": _row(jnp.tile(mk_norm_g[l], MEM_HEADS))}
        return sp, _row(lower[l])

    w_in_full = _gathered(_all_gather(_pack(shards("", 0), GATHER_IN, MXU_DTYPE)), GATHER_IN, shapes)["w_in"]
    saved, weights = [], []
    h = xf
    for l in range(L):
        sp, lb = small_params(l)
        next_in = given["w_in"][l + 1] if l + 1 < L else None
        h, s, W, w_in_full = _layer_fwd(h, memf, w_in_full, shards("", l), next_in, shapes, sp, lb, tab, dims)
        saved.append(s)
        weights.append(W)

    dh, dh_low, loss_part = _loss_head(h, loss_target.reshape(B * S, D))

    recv_layers, gs_layers, dlb_layers = [None] * L, [None] * L, [None] * L
    for l in range(L - 1, -1, -1):
        sp, lb = small_params(l)
        dh, dh_low, recv_layers[l], gs_layers[l], dlb_layers[l] = _layer_bwd(dh, dh_low, saved[l], memf, weights[l],
                                                                              shapes, sp, lb, tab, dims)
    grad_x = dh.reshape(B, S, D)

    big_out = [{n: [None] * L for n in BIG} for _ in range(4)]
    for l in range(L):
        for names in SCATTERS:
            flats = _sum_adamw(recv_layers[l][names], _pack(shards("", l), names), _pack(shards("m_", l), names),
                               _pack(shards("v_", l), names))
            for k in range(4):
                for n, t in _unpack(flats[k], names, shapes).items():
                    big_out[k][n][l] = t
    big_out = [{n: jnp.stack(d[n]) for n in BIG} for d in big_out]

    gs = {n: jnp.stack([gs_layers[l][n].reshape(small_shapes[n][1:]) for l in range(L)]) for n in SMALL if n != "hg_lb"}
    gs["hg_lb"] = _lower_bounds_bwd(hg_lb, jnp.concatenate(dlb_layers, axis=0))
    packed = _small_sum_adamw(_pack_small(gs, "", loss_part[0, 0]), _pack_small(given, ""), _pack_small(given, "m_"),
                              _pack_small(given, "v_"))
    small_out, loss = [], None
    for t in packed:
        d, tail = _unpack_small(t, small_shapes)
        small_out.append(d)
        loss = tail if loss is None else loss

    outs = [loss, grad_x]
    for k in range(4):
        outs += [small_out[k][n] if n in small_shapes else big_out[k][n] for n in WEIGHTS]
    return tuple(outs)
```

```python
import functools
import math

import jax
import jax.numpy as jnp
from jax import lax
from jax.experimental import pallas as pl
from jax.experimental.pallas import tpu as pltpu

F32 = jnp.float32
MXU_DTYPE = jnp.bfloat16
VMEM_LIMIT = 48 * 1024 * 1024

N_DEV = 8
EPS = 1e-6
NEG = -1e30
F_MIN = 1e-12
ATT_HEADS = 8
ATT_HEAD_DIM = 64
ATT_W = ATT_HEADS * ATT_HEAD_DIM
DILATIONS = (1, 4, 16)
DIL_STEPS = 128
ATT_Q_BLOCK = 256
ATT_KEY_EXTENT = 512
HG_HEADS = 4
HG_HEAD_DIM = 128
HG_W = HG_HEADS * HG_HEAD_DIM
HG_CHUNK = 32
HG_SUB = 8
MEM_HEADS = 4
LANES = 128

ADAM_LR = 0.001
ADAM_B1 = 0.9
ADAM_B2 = 0.999
ADAM_EPS = 1e-08
ADAM_WD = 0.01
ADAM_STEP = 10

BIG = ("w_in", "w_out", "w_mq", "w_mkv", "w_mo", "w_ff1", "w_ff2")
COL_SHARDED = ("w_in", "w_mkv", "w_ff1")
SMALL = ("norm1_g", "attn_qn_g", "attn_kn_g", "hg_lb", "hg_onorm_g", "norm2_g", "mem_norm_g",
         "mq_norm_g", "mk_norm_g", "norm3_g")
WEIGHTS = ("norm1_g", "w_in", "attn_qn_g", "attn_kn_g", "hg_lb", "hg_onorm_g", "w_out", "norm2_g",
           "mem_norm_g", "w_mq", "w_mkv", "mq_norm_g", "mk_norm_g", "w_mo", "norm3_g", "w_ff1", "w_ff2")
GATHER_IN = ("w_in",)
GATHER_MIX = ("w_out", "w_mq", "w_mkv", "w_mo")
GATHER_FF = ("w_ff1", "w_ff2")
SCATTER_A = ("w_ff2",)
SCATTER_B = ("w_ff1", "w_mo", "w_mq", "w_mkv", "w_out")
SCATTER_C = ("w_in",)


def _pcall(body, **kw):
    return pl.pallas_call(body, **kw)


def _params():
    return pltpu.CompilerParams(vmem_limit_bytes=VMEM_LIMIT)


def _tile(n, pref, mult):
    t = (min(n, pref) // mult) * mult
    while t >= mult:
        if n % t == 0:
            return t
        t -= mult
    return n


def _sds(shape, dtype):
    return jax.ShapeDtypeStruct(shape, dtype)


def _mesh_pos():
    return lax.axis_index("x"), lax.axis_index("y"), lax.axis_index("c")


def _flip(v, bit):
    return 1 - v if bit else v


def _gather_hooks(x_refs, out_refs, send_sems, recv_sems, local_sems):
    MESH = pl.DeviceIdType.MESH
    n = len(x_refs)
    x, y, c = _mesh_pos()
    me, sibling = (x, y, c), (x, y, 1 - c)
    chips = [(1 - x, y), (x, 1 - y), (1 - x, 1 - y)]

    def rows(a, px, py, pc):
        return out_refs[a].at[4 * px + 2 * py + pc]

    def copy(a, k, block, to, own=False):
        return pltpu.make_async_remote_copy(
            src_ref=x_refs[a] if own else rows(a, *block), dst_ref=rows(a, *block), send_sem=send_sems.at[n * k + a],
            recv_sem=recv_sems.at[n * k + a], device_id=to, device_id_type=MESH)

    mine = [pltpu.make_async_copy(x_refs[a], rows(a, *me), local_sems.at[a]) for a in range(n)]
    first = [copy(a, 0, me, sibling, own=True) for a in range(n)]
    first += [copy(a, 1 + j, me, (*chip, c), own=True) for j, chip in enumerate(chips) for a in range(n)]
    passed = [[copy(a, 4 + j, (*chip, c), sibling) for a in range(n)] for j, chip in enumerate(chips)]

    def start():
        for cp in mine + first:
            cp.start()

    def forward():
        for j, chip in enumerate(chips):
            for a in range(n):
                copy(a, 1 + j, (*chip, c), me).wait_recv()
                passed[j][a].start()

    def finish():
        for a in range(n):
            copy(a, 0, sibling, me).wait_recv()
        for j, chip in enumerate(chips):
            for a in range(n):
                copy(a, 4 + j, (*chip, 1 - c), me).wait_recv()
        for cp in first + [cp for group in passed for cp in group]:
            cp.wait_send()
        for cp in mine:
            cp.wait()

    return start, forward, finish


def _scatter_hooks(s_refs, r_refs, send_sems, recv_sems, local_sems):
    MESH = pl.DeviceIdType.MESH
    n = len(s_refs)
    x, y, c = _mesh_pos()
    me = 4 * x + 2 * y + c
    mine = [pltpu.make_async_copy(s_refs[a].at[me], r_refs[a].at[me], local_sems.at[a]) for a in range(n)]
    copies = []
    for m in range(1, N_DEV):
        px, py, pc = _flip(x, m & 4), _flip(y, m & 2), _flip(c, m & 1)
        for a in range(n):
            copies.append(pltpu.make_async_remote_copy(
                src_ref=s_refs[a].at[4 * px + 2 * py + pc], dst_ref=r_refs[a].at[me],
                send_sem=send_sems.at[n * (m - 1) + a], recv_sem=recv_sems.at[n * (m - 1) + a],
                device_id=(px, py, pc), device_id_type=MESH))

    def start():
        for cp in mine + copies:
            cp.start()

    def finish():
        for cp in copies:
            cp.wait()
        for cp in mine:
            cp.wait()

    return start, None, finish


def _exchange_sems(n):
    return [pltpu.SemaphoreType.DMA(((N_DEV - 1) * n,)), pltpu.SemaphoreType.DMA(((N_DEV - 1) * n,)),
            pltpu.SemaphoreType.DMA((n,))]


def _exchange_out(kind, xs):
    return _sds((N_DEV,) + xs.shape, xs.dtype) if kind == "gather" else _sds(xs.shape, xs.dtype)


def _run(body, args, *, out_shape, grid, in_specs, out_specs, scratch_shapes=(), name, carry=None, aliases=None):
    single = not isinstance(out_shape, (tuple, list))
    outs = (out_shape,) if single else tuple(out_shape)
    ospecs = (out_specs,) if single else tuple(out_specs)
    aliases = dict(aliases or {})
    if carry is None:
        res = _pcall(body, out_shape=outs, grid=grid, in_specs=list(in_specs), out_specs=ospecs,
                     scratch_shapes=list(scratch_shapes), input_output_aliases=aliases, compiler_params=_params(),
                     name=name)(*args)
        return res[0] if single else tuple(res)
    kind, xs = carry
    n_in, n_out, n_scr, n_x = len(args), len(outs), len(scratch_shapes), len(xs)
    total = math.prod(grid)

    def wrapped(*refs):
        refs = list(refs)
        ins, x_refs = refs[:n_in], refs[n_in:n_in + n_x]
        o, xo_refs = refs[n_in + n_x:n_in + n_x + n_out], refs[n_in + n_x + n_out:n_in + 2 * n_x + n_out]
        scr = refs[n_in + 2 * n_x + n_out:n_in + 2 * n_x + n_out + n_scr]
        sems = refs[n_in + 2 * n_x + n_out + n_scr:]
        step = pl.program_id(0)
        for ax in range(1, len(grid)):
            step = step * grid[ax] + pl.program_id(ax)
        start, forward, finish = (_gather_hooks if kind == "gather" else _scatter_hooks)(x_refs, xo_refs, *sems)
        pl.when(step == 0)(start)
        body(*ins, *o, *scr)
        if forward is not None:
            pl.when(step == (3 * total) // 4)(forward)
        pl.when(step == total - 1)(finish)

    hbm = pl.BlockSpec(memory_space=pl.ANY)
    res = _pcall(wrapped, out_shape=outs + tuple(_exchange_out(kind, t) for t in xs), grid=grid,
                 in_specs=list(in_specs) + [hbm] * n_x, out_specs=ospecs + (hbm,) * n_x,
                 scratch_shapes=list(scratch_shapes) + _exchange_sems(n_x), input_output_aliases=aliases,
                 compiler_params=_params(), name=name)(*args, *xs)
    main = tuple(res[:n_out])
    return (main[0] if single else main), list(res[n_out:])


def _all_gather(xs):
    n = len(xs)

    def body(*refs):
        start, forward, finish = _gather_hooks(refs[:n], refs[n:2 * n], *refs[2 * n:])
        start()
        forward()
        finish()

    hbm = pl.BlockSpec(memory_space=pl.ANY)
    return _pcall(body, out_shape=tuple(_exchange_out("gather", t) for t in xs), in_specs=[hbm] * n,
                  out_specs=(hbm,) * n, scratch_shapes=_exchange_sems(n), name="weights_all_gather")(*xs)


def _matmul(a, b, *, ta=False, tb=False, epi=None, extra=None, out_dtype=F32, carry=None, name):
    M, K = (a.shape[1], a.shape[0]) if ta else a.shape
    N = b.shape[0] if tb else b.shape[1]
    tm = _tile(M, 1024, 8 if not ta else LANES)
    tn = _tile(N, 1024, LANES)
    tk = _tile(K, 512 if ta else 1024, LANES if not ta else 8)
    nk = K // tk
    dims = (((0 if ta else 1,), (1 if tb else 0,)), ((), ()))

    def body(a_ref, b_ref, *rest):
        e_ref = rest[0] if extra is not None else None
        o_ref = rest[1] if extra is not None else rest[0]

        def finish(r):
            if epi == "add":
                r = r + e_ref[...]
            elif epi == "relu2":
                r = jnp.square(jnp.maximum(r, 0.0))
            elif epi == "relu2grad":
                r = r * (2.0 * jnp.sqrt(e_ref[...].astype(F32)))
            o_ref[...] = r.astype(out_dtype)

        part = lax.dot_general(a_ref[...].astype(MXU_DTYPE), b_ref[...].astype(MXU_DTYPE), dims,
                               preferred_element_type=F32)
        if nk == 1:
            finish(part)
        else:
            acc_ref = rest[-1]
            k = pl.program_id(2)

            @pl.when(k == 0)
            def _():
                acc_ref[...] = part

            @pl.when(k > 0)
            def _():
                acc_ref[...] += part

            @pl.when(k == nk - 1)
            def _():
                finish(acc_ref[...])

    a_spec = pl.BlockSpec((tk, tm), lambda i, j, k: (k, i)) if ta else pl.BlockSpec((tm, tk), lambda i, j, k: (i, k))
    b_spec = pl.BlockSpec((tn, tk), lambda i, j, k: (j, k)) if tb else pl.BlockSpec((tk, tn), lambda i, j, k: (k, j))
    o_spec = pl.BlockSpec((tm, tn), lambda i, j, k: (i, j))
    in_specs = [a_spec, b_spec] + ([o_spec] if extra is not None else [])
    args = (a, b) + ((extra,) if extra is not None else ())
    return _run(body, args, out_shape=_sds((M, N), out_dtype), grid=(M // tm, N // tn, nk), in_specs=in_specs,
                out_specs=o_spec, scratch_shapes=[pltpu.VMEM((tm, tn), F32)] if nk > 1 else [], name=name, carry=carry)


def _group_mean(v, E):
    rows, W = v.shape
    if E == W:
        return jnp.mean(v, axis=-1, keepdims=True)
    pieces = []
    if E % LANES == 0:
        for g0 in range(0, W, E):
            m = jnp.mean(v[:, g0:g0 + E], axis=-1, keepdims=True)
            pieces.append(jnp.broadcast_to(m, (rows, E)))
    else:
        lane = lax.broadcasted_iota(jnp.int32, (rows, LANES), 1)
        for c0 in range(0, W, LANES):
            vc = v[:, c0:c0 + LANES]
            acc = jnp.zeros((rows, LANES), F32)
            for s0 in range(0, LANES, E):
                msk = (lane >= s0) & (lane < s0 + E)
                m = jnp.sum(jnp.where(msk, vc, 0.0), axis=-1, keepdims=True) * (1.0 / E)
                acc = jnp.where(msk, m, acc)
            pieces.append(acc)
    return jnp.concatenate(pieces, axis=-1)


def _fold_groups(t, E):
    W = t.shape[1]
    step = max(E, LANES)
    acc = t[:, 0:step]
    for c0 in range(step, W, step):
        acc = acc + t[:, c0:c0 + step]
    sh = LANES // 2
    while sh >= E:
        acc = acc + pltpu.roll(acc, sh, 1)
        sh //= 2
    return acc


def _norm_fwd(x, g, *, E, W, cb=0, out_dtype, name):
    M = x.shape[0]
    tm = _tile(M, 512, 8)

    def body(x_ref, g_ref, o_ref):
        xv = x_ref[...]
        r = lax.rsqrt(_group_mean(xv * xv, E) + EPS)
        o_ref[...] = (xv * r * g_ref[...]).astype(out_dtype)

    return _run(body, (x, g), out_shape=_sds((M, W), out_dtype), grid=(M // tm,),
                in_specs=[pl.BlockSpec((tm, W), lambda i: (i, cb)), pl.BlockSpec((1, W), lambda i: (0, 0))],
                out_specs=pl.BlockSpec((tm, W), lambda i: (i, 0)), name=name)


def _norm_bwd(x, g, dy, *, E, W, cb=0, res=None, fold=None, out_dtype=F32, low=False, name):
    M = x.shape[0]
    tm = _tile(M, 512, 8)
    n = M // tm
    gw = W if fold is None else max(fold, LANES)

    def body(x_ref, g_ref, dy_ref, *rest):
        rest = list(rest)
        r_ref = rest.pop(0) if res is not None else None
        dx_ref = rest.pop(0)
        dxl_ref = rest.pop(0) if low else None
        dg_ref, acc_ref = rest
        i = pl.program_id(0)
        xv = x_ref[...]
        r = lax.rsqrt(_group_mean(xv * xv, E) + EPS)
        xh = xv * r
        dyv = dy_ref[...].astype(F32)
        dyg = dyv * g_ref[...]
        dx = r * (dyg - xh * _group_mean(dyg * xh, E))
        if res is not None:
            dx = dx + r_ref[...]
        dx_ref[...] = dx.astype(out_dtype)
        if low:
            dxl_ref[...] = dx.astype(MXU_DTYPE)
        part = jnp.sum(dyv * xh, axis=0, keepdims=True)

        @pl.when(i == 0)
        def _():
            acc_ref[...] = part

        @pl.when(i > 0)
        def _():
            acc_ref[...] += part

        @pl.when(i == n - 1)
        def _():
            t = acc_ref[...]
            dg_ref[...] = t if fold is None else _fold_groups(t, fold)

    blk = pl.BlockSpec((tm, W), lambda i: (i, 0))
    in_specs = [pl.BlockSpec((tm, W), lambda i: (i, cb)), pl.BlockSpec((1, W), lambda i: (0, 0)), blk]
    args = [x, g, dy]
    if res is not None:
        in_specs.append(blk)
        args.append(res)
    lows = ((_sds((M, W), MXU_DTYPE),), (blk,)) if low else ((), ())
    return _run(body, tuple(args), out_shape=(_sds((M, W), out_dtype),) + lows[0] + (_sds((1, gw), F32),), grid=(n,),
                in_specs=in_specs, out_specs=(blk,) + lows[1] + (pl.BlockSpec((1, gw), lambda i: (0, 0)),),
                scratch_shapes=[pltpu.VMEM((1, W), F32)], name=name)


def _attn_tables(S, tq):
    nq = S // tq

    def body(o_ref):
        i = pl.program_id(0)
        d = (i * tq + lax.broadcasted_iota(jnp.int32, (tq, S), 0)) - lax.broadcasted_iota(jnp.int32, (tq, S), 1)
        cnt = jnp.zeros((tq, S), jnp.int32)
        for dil in DILATIONS:
            hit = (d <= DIL_STEPS * dil) if dil == 1 else (((d & (dil - 1)) == 0) & (d <= DIL_STEPS * dil))
            cnt = cnt + hit.astype(jnp.int32)
        ok = (d >= 0) & (cnt > 0)
        logm = jnp.where(cnt == 3, math.log(3.0), jnp.where(cnt == 2, math.log(2.0), 0.0))
        o_ref[0] = jnp.where(ok, logm, NEG).astype(F32)

    return _run(body, (), out_shape=_sds((nq, tq, S), F32), grid=(nq,), in_specs=[],
                out_specs=pl.BlockSpec((1, tq, S), lambda i: (i, 0, 0)), name="dilated_attn_tables")


def _alibi_slope(h):
    return 2.0 ** (-8.0 / ATT_HEADS * (h + 1))


def _key_positions(Sk):
    kpos = lax.broadcasted_iota(jnp.int32, (Sk, LANES), 0)
    return (kpos >> 8).astype(F32).astype(MXU_DTYPE), (kpos & 255).astype(F32).astype(MXU_DTYPE)


def _score_operands(q2s, k2, pos_hi, pos_lo, lane, sub, h):
    own = (lane < ATT_HEAD_DIM) if sub == 0 else (lane >= ATT_HEAD_DIM)
    spare = ATT_HEAD_DIM if sub == 0 else 0
    slope = _alibi_slope(h)
    terms = jnp.where(lane == spare, slope * 256.0, jnp.where(lane == spare + 1, slope, 0.0)).astype(q2s.dtype)
    qa = jnp.where(own, q2s, terms)
    ka = jnp.where(lane == spare, pos_hi, jnp.where(lane == spare + 1, pos_lo, k2))
    return own, qa, ka


def _key_extents(S, tq):
    ext = min(ATT_KEY_EXTENT, S)
    return ext, ext // tq, S // ext


def _attn_fwd(tab, qn, kn, z, *, B, S, carry=None, name):
    T = B * S
    tq = tab.shape[1]
    nq = S // tq
    ext, per, n_ext = _key_extents(S, tq)
    scale = 1.0 / math.sqrt(ATT_HEAD_DIM)
    dn_t = (((1,), (1,)), ((), ()))

    def body(tab_ref, q_ref, k_ref, v_ref, y_ref, lse_ref, ycat_ref):
        i = pl.program_id(1)
        lane = lax.broadcasted_iota(jnp.int32, (1, LANES), 1)
        lo = lane < ATT_HEAD_DIM

        def visit(Sk):
            base = tab_ref[0, :, :Sk]
            pos_hi, pos_lo = _key_positions(Sk)
            lse_blk = jnp.zeros((tq, LANES), F32)
            for p in range(ATT_HEADS // 2):
                cs = slice(p * LANES, (p + 1) * LANES)
                q2s = q_ref[:, cs].astype(MXU_DTYPE) * scale
                k2 = k_ref[:Sk, cs].astype(MXU_DTYPE)
                v2 = v_ref[:Sk, cs].astype(MXU_DTYPE)
                outs = []
                for sub in range(2):
                    h = 2 * p + sub
                    _, qa, ka = _score_operands(q2s, k2, pos_hi, pos_lo, lane, sub, h)
                    s = lax.dot_general(qa, ka, dn_t, preferred_element_type=F32) + base
                    mx = jnp.max(s, axis=1, keepdims=True)
                    e = jnp.exp(s - mx)
                    l = jnp.sum(e, axis=1, keepdims=True)
                    outs.append(jnp.dot(e.astype(MXU_DTYPE), v2, preferred_element_type=F32) / l)
                    lse_blk = jnp.where(lane == h, mx + jnp.log(l), lse_blk)
                y = jnp.where(lo, outs[0], outs[1])
                y_ref[:, cs] = y
                ycat_ref[:, cs] = y.astype(MXU_DTYPE)
            lse_ref[...] = lse_blk

        for e in range(n_ext):
            pl.when(i // per == e)(functools.partial(visit, (e + 1) * ext))

    return _run(
        body, (tab, qn, kn, z), grid=(B, nq),
        out_shape=(_sds((T, ATT_W), F32), _sds((T, LANES), F32), _sds((T, ATT_W + HG_W), MXU_DTYPE)),
        in_specs=[pl.BlockSpec((1, tq, S), lambda b, i: (i, 0, 0)),
                  pl.BlockSpec((tq, ATT_W), lambda b, i: (b * nq + i, 0)),
                  pl.BlockSpec((S, ATT_W), lambda b, i: (b, 0)),
                  pl.BlockSpec((S, ATT_W), lambda b, i: (b, 2))],
        out_specs=(pl.BlockSpec((tq, ATT_W), lambda b, i: (b * nq + i, 0)),
                   pl.BlockSpec((tq, LANES), lambda b, i: (b * nq + i, 0)),
                   pl.BlockSpec((tq, ATT_W), lambda b, i: (b * nq + i, 0))),
        name=name, carry=carry)


def _attn_bwd(tab, qn, kn, z, y, lse, dycat, *, B, S, carry=None, name):
    T = B * S
    tq = tab.shape[1]
    nq = S // tq
    ext, per, n_ext = _key_extents(S, tq)
    scale = 1.0 / math.sqrt(ATT_HEAD_DIM)
    dn_t = (((1,), (1,)), ((), ()))
    dn_o = (((0,), (0,)), ((), ()))

    def body(tab_ref, q_ref, k_ref, v_ref, y_ref, lse_ref, dy_ref, dq_ref, dk_ref, dv_ref):
        i = pl.program_id(1)

        @pl.when(i == 0)
        def _():
            dk_ref[...] = jnp.zeros_like(dk_ref)
            dv_ref[...] = jnp.zeros_like(dv_ref)

        lane = lax.broadcasted_iota(jnp.int32, (1, LANES), 1)
        lo = lane < ATT_HEAD_DIM

        def visit(Sk):
            base = tab_ref[0, :, :Sk]
            pos_hi, pos_lo = _key_positions(Sk)
            lse_blk = lse_ref[...]
            for p in range(ATT_HEADS // 2):
                cs = slice(p * LANES, (p + 1) * LANES)
                q2 = q_ref[:, cs].astype(MXU_DTYPE)
                q2s = q2 * scale
                k2 = k_ref[:Sk, cs].astype(MXU_DTYPE)
                v2 = v_ref[:Sk, cs].astype(MXU_DTYPE)
                do2 = dy_ref[:, cs]
                doy = do2 * y_ref[:, cs]
                do2m = do2.astype(MXU_DTYPE)
                dqs, dks, dvs = [], [], []
                for sub in range(2):
                    h = 2 * p + sub
                    own, qa, ka = _score_operands(q2s, k2, pos_hi, pos_lo, lane, sub, h)
                    s = lax.dot_general(qa, ka, dn_t, preferred_element_type=F32) + base
                    lse_h = jnp.sum(jnp.where(lane == h, lse_blk, 0.0), axis=1, keepdims=True)
                    pr = jnp.exp(s - lse_h)
                    dsum = jnp.sum(jnp.where(own, doy, 0.0), axis=1, keepdims=True)
                    dom = jnp.where(own, do2m, jnp.zeros_like(do2m))
                    dp = lax.dot_general(dom, v2, dn_t, preferred_element_type=F32)
                    ds = (pr * (dp - dsum)).astype(MXU_DTYPE)
                    dqs.append(jnp.dot(ds, k2, preferred_element_type=F32))
                    dks.append(lax.dot_general(ds, q2, dn_o, preferred_element_type=F32))
                    dvs.append(lax.dot_general(pr.astype(MXU_DTYPE), do2m, dn_o, preferred_element_type=F32))
                dq_ref[:, cs] = jnp.where(lo, dqs[0], dqs[1]) * scale
                dk_ref[:Sk, cs] += jnp.where(lo, dks[0], dks[1]) * scale
                dv_ref[:Sk, cs] += jnp.where(lo, dvs[0], dvs[1])

        for e in range(n_ext):
            pl.when(i // per == e)(functools.partial(visit, (e + 1) * ext))

    qblk = pl.BlockSpec((tq, ATT_W), lambda b, i: (b * nq + i, 0))
    sblk = pl.BlockSpec((S, ATT_W), lambda b, i: (b, 0))
    return _run(
        body, (tab, qn, kn, z, y, lse, dycat), out_shape=(_sds((T, ATT_W), F32),) * 3, grid=(B, nq),
        in_specs=[pl.BlockSpec((1, tq, S), lambda b, i: (i, 0, 0)), qblk, sblk,
                  pl.BlockSpec((S, ATT_W), lambda b, i: (b, 2)), qblk,
                  pl.BlockSpec((tq, LANES), lambda b, i: (b * nq + i, 0)), qblk],
        out_specs=(qblk, sblk, sblk), name=name, carry=carry)


def _sigmoid_pair(x):
    en = jnp.exp(-jnp.abs(x))
    big = 1.0 / (1.0 + en)
    small = en * big
    pos = x >= 0
    return jnp.where(pos, big, small), jnp.where(pos, small, big)


def _chunk_scan(x, pos, reverse):
    n = x.shape[0]
    sh = 1
    while sh < HG_CHUNK:
        if reverse:
            x = x + jnp.where(pos < HG_CHUNK - sh, pltpu.roll(x, n - sh, 0), 0.0)
        else:
            x = x + jnp.where(pos >= sh, pltpu.roll(x, sh, 0), 0.0)
        sh *= 2
    return x


def _hgrn_pair_table(tb):
    def body(o_ref):
        t = lax.broadcasted_iota(jnp.int32, (tb, tb), 0)
        s = lax.broadcasted_iota(jnp.int32, (tb, tb), 1)
        o_ref[...] = jnp.where((t // HG_CHUNK) == (s // HG_CHUNK), (t // HG_SUB) - (s // HG_SUB), 0).astype(F32)

    return _pcall(body, out_shape=_sds((tb, tb), F32), name="hgrn_pair_table")()


def _hgrn_cross_decays(b, pos):
    n = b.shape[0]
    pos8 = pos % HG_SUB
    end = b
    sh = 1
    while sh < HG_SUB:
        end = jnp.where((pos8 & sh) == 0, pltpu.roll(end, n - sh, 0), end)
        sh *= 2
    x = jnp.where(pos >= HG_SUB, jnp.exp(jnp.minimum(b - pltpu.roll(end, HG_SUB, 0), 0.0)), 0.0)
    ys = []
    for j in range(1, HG_CHUNK // HG_SUB):
        end_j = end if j == 1 else pltpu.roll(end, n - HG_SUB * (j - 1), 0)
        ys.append(jnp.where(pos < HG_CHUNK - HG_SUB * j, jnp.exp(jnp.minimum(end_j - b, 0.0)), 0.0))
    return x, ys


def _hgrn_cross_scores(qt, kts, pair):
    n = qt.shape[0]
    kcat = jnp.concatenate([k.astype(MXU_DTYPE) for k in kts], axis=0)
    p = lax.dot_general(qt.astype(MXU_DTYPE), kcat, (((1,), (1,)), ((), ())), preferred_element_type=F32)
    a = jnp.zeros((n, n), F32)
    for j in range(1, len(kts) + 1):
        a = jnp.where(pair == j, p[:, (j - 1) * n:j * n], a)
    return a, kcat


def _hgrn_gates(qh, fh, lb):
    sq, _ = _sigmoid_pair(qh)
    sg, sgn = _sigmoid_pair(fh)
    f = lb + (1.0 - lb) * sg
    g = jnp.log(jnp.maximum(f, F_MIN))
    kk = (1.0 - lb) * sgn
    return qh * sq, sq, sg, sgn, f, g, kk


def _hgrn_fwd(z, lb, pair, *, B, S, name):
    T = B * S
    tb = _tile(S, 256, HG_CHUNK)
    nc = tb // HG_CHUNK
    nb = S // tb
    dn_t = (((1,), (1,)), ((), ()))
    dn_o = (((0,), (0,)), ((), ()))

    def zcol(first):
        return lambda h, b, i: (b * nb + i, first + h)

    def body(q_ref, f_ref, v_ref, lb_ref, pair_ref, o_ref, st_ref, state):
        i = pl.program_id(2)

        @pl.when(i == 0)
        def _():
            state[...] = jnp.zeros_like(state)

        q, _, _, _, _, g, kk = _hgrn_gates(q_ref[...], f_ref[...], lb_ref[...])
        v = v_ref[...]
        pos = lax.broadcasted_iota(jnp.int32, (tb, LANES), 0) % HG_CHUNK
        pos8 = pos % HG_SUB
        b = _chunk_scan(g, pos, False)
        o = jnp.sum(q * kk, axis=1, keepdims=True) * v
        for d in range(1, HG_SUB):
            qd = pltpu.roll(q, tb - d, 0)
            bd = pltpu.roll(b, tb - d, 0)
            e = jnp.where(pos8 < HG_SUB - d, jnp.exp(jnp.minimum(bd - b, 0.0)), 0.0)
            a = jnp.sum(qd * kk * e, axis=1, keepdims=True)
            o = o + pltpu.roll(a * v, d, 0)
        x, ys = _hgrn_cross_decays(b, pos)
        a_cross, _ = _hgrn_cross_scores(q * x, [kk * y for y in ys], pair_ref[...])
        o = o + jnp.dot(a_cross.astype(MXU_DTYPE), v.astype(MXU_DTYPE), preferred_element_type=F32)
        qe = q * jnp.exp(b)
        chunks = [slice(c * HG_CHUNK, (c + 1) * HG_CHUNK) for c in range(nc)]
        b_last = [jnp.sum(jnp.where(pos[rs] == HG_CHUNK - 1, b[rs], 0.0), axis=0, keepdims=True) for rs in chunks]
        own = [lax.dot_general(v[rs].astype(MXU_DTYPE), (kk[rs] * jnp.exp(bl - b[rs])).astype(MXU_DTYPE), dn_o,
                               preferred_element_type=F32) for rs, bl in zip(chunks, b_last)]
        st = state[...]
        before = []
        for c in range(nc):
            before.append(st)
            st = st * jnp.exp(b_last[c]) + own[c]
        state[...] = st
        for c, rs in enumerate(chunks):
            st_ref[c * LANES:(c + 1) * LANES, :] = before[c]
            o_ref[rs, :] = o[rs] + lax.dot_general(qe[rs].astype(MXU_DTYPE), before[c].astype(MXU_DTYPE), dn_t,
                                                   preferred_element_type=F32)

    blk = (tb, LANES)
    return _run(
        body, (z, z, z, lb, pair), out_shape=(_sds((T, HG_W), F32), _sds((T // HG_CHUNK * LANES, HG_W), F32)),
        grid=(HG_HEADS, B, nb),
        in_specs=[pl.BlockSpec(blk, zcol(12)), pl.BlockSpec(blk, zcol(16)), pl.BlockSpec(blk, zcol(20)),
                  pl.BlockSpec((1, LANES), lambda h, b, i: (0, h)), pl.BlockSpec((tb, tb), lambda h, b, i: (0, 0))],
        out_specs=(pl.BlockSpec(blk, lambda h, b, i: (b * nb + i, h)),
                   pl.BlockSpec((nc * LANES, LANES), lambda h, b, i: (b * nb + i, h))),
        scratch_shapes=[pltpu.VMEM((LANES, LANES), F32)], name=name)


def _hgrn_bwd(z, lb, pair, states, do, *, B, S, name):
    T = B * S
    tb = _tile(S, 256, HG_CHUNK)
    nc = tb // HG_CHUNK
    nb = S // tb
    dn_t = (((1,), (1,)), ((), ()))
    dn_o = (((0,), (0,)), ((), ()))

    def body(q_ref, f_ref, v_ref, lb_ref, pair_ref, st_ref, do_ref, dq_ref, df_ref, di_ref, dlb_ref, dstate):
        bi = pl.program_id(1)
        i = pl.program_id(2)

        @pl.when(i == 0)
        def _():
            dstate[...] = jnp.zeros_like(dstate)

        qh = q_ref[...]
        lbv = lb_ref[...]
        q, sq, sg, sgn, f, g, kk = _hgrn_gates(qh, f_ref[...], lbv)
        v = v_ref[...]
        dov = do_ref[...]
        pos = lax.broadcasted_iota(jnp.int32, (tb, LANES), 0) % HG_CHUNK
        pos8 = pos % HG_SUB
        b = _chunk_scan(g, pos, False)

        a0 = jnp.sum(q * kk, axis=1, keepdims=True)
        da0 = jnp.sum(dov * v, axis=1, keepdims=True)
        dv = a0 * dov
        dkk = da0 * q
        dq = da0 * kk
        db = jnp.zeros((tb, LANES), F32)
        for d in range(1, HG_SUB):
            qd = pltpu.roll(q, tb - d, 0)
            bd = pltpu.roll(b, tb - d, 0)
            dod = pltpu.roll(dov, tb - d, 0)
            e = jnp.where(pos8 < HG_SUB - d, jnp.exp(jnp.minimum(bd - b, 0.0)), 0.0)
            a = jnp.sum(qd * kk * e, axis=1, keepdims=True)
            da = jnp.sum(dod * v, axis=1, keepdims=True)
            dv = dv + a * dod
            xe = da * e
            dkk = dkk + xe * qd
            t1 = xe * kk
            wt = t1 * qd
            dq = dq + pltpu.roll(t1, d, 0)
            db = db + pltpu.roll(wt, d, 0) - wt

        n_cross = HG_CHUNK // HG_SUB - 1
        pair = pair_ref[...]
        x, ys = _hgrn_cross_decays(b, pos)
        qt = q * x
        kts = [kk * y for y in ys]
        a_cross, kcat = _hgrn_cross_scores(qt, kts, pair)
        dom = dov.astype(MXU_DTYPE)
        da_all = lax.dot_general(dom, v.astype(MXU_DTYPE), dn_t, preferred_element_type=F32)
        dv = dv + lax.dot_general(a_cross.astype(MXU_DTYPE), dom, dn_o, preferred_element_type=F32)
        gcat = jnp.concatenate([jnp.where(pair == j, da_all, 0.0).astype(MXU_DTYPE) for j in range(1, n_cross + 1)],
                               axis=1)
        dqt = jnp.dot(gcat, kcat, preferred_element_type=F32)
        dkcat = lax.dot_general(gcat, qt.astype(MXU_DTYPE), dn_o, preferred_element_type=F32)
        dq = dq + dqt * x
        u = dqt * qt
        db = db + u
        d_end = -pltpu.roll(u, tb - HG_SUB, 0)
        for j in range(1, n_cross + 1):
            dkt = dkcat[(j - 1) * tb:j * tb]
            dkk = dkk + dkt * ys[j - 1]
            w = dkt * kts[j - 1]
            db = db - w
            d_end = d_end + (w if j == 1 else pltpu.roll(w, HG_SUB * (j - 1), 0))
        sh = 1
        while sh < HG_SUB:
            d_end = d_end + jnp.where(pos8 >= sh, pltpu.roll(d_end, sh, 0), 0.0)
            sh *= 2
        db = db + jnp.where(pos8 == HG_SUB - 1, d_end, 0.0)

        eb = jnp.exp(b)
        qe = q * eb
        chunks = [slice(c * HG_CHUNK, (c + 1) * HG_CHUNK) for c in range(nc)]
        b_lasts = [jnp.sum(jnp.where(pos[rs] == HG_CHUNK - 1, b[rs], 0.0), axis=0, keepdims=True) for rs in chunks]
        own = [lax.dot_general(dom[rs], qe[rs].astype(MXU_DTYPE), dn_o, preferred_element_type=F32) for rs in chunks]
        dst = dstate[...]
        after = [None] * nc
        for c in range(nc - 1, -1, -1):
            after[c] = dst
            dst = dst * jnp.exp(b_lasts[c]) + own[c]
        dstate[...] = dst
        dq_c, dkk_c, dv_c, db_c = [None] * nc, [None] * nc, [None] * nc, [None] * nc
        for c, rs in enumerate(chunks):
            st0 = st_ref[c * LANES:(c + 1) * LANES, :]
            eb_last = jnp.exp(b_lasts[c])
            er = jnp.exp(b_lasts[c] - b[rs])
            ke = kk[rs] * er
            dstm = after[c].astype(MXU_DTYPE)
            dqe = jnp.dot(dom[rs], st0.astype(MXU_DTYPE), preferred_element_type=F32)
            dv_c[c] = dv[rs] + lax.dot_general(ke.astype(MXU_DTYPE), dstm, dn_t, preferred_element_type=F32)
            dke = jnp.dot(v[rs].astype(MXU_DTYPE), dstm, preferred_element_type=F32)
            d_eb_last = jnp.sum(after[c] * st0, axis=0, keepdims=True)
            dq_c[c] = dq[rs] + dqe * eb[rs]
            dkk_c[c] = dkk[rs] + dke * er
            dkeke = dke * ke
            db_last = jnp.sum(dkeke, axis=0, keepdims=True) + d_eb_last * eb_last
            db_c[c] = db[rs] + dqe * qe[rs] - dkeke + jnp.where(pos[rs] == HG_CHUNK - 1, db_last, 0.0)
        dq = jnp.concatenate(dq_c, axis=0)
        dkk = jnp.concatenate(dkk_c, axis=0)
        dv = jnp.concatenate(dv_c, axis=0)
        dg = _chunk_scan(jnp.concatenate(db_c, axis=0), pos, True)

        dfv = dg * jnp.where(f > F_MIN, 1.0 / f, 0.0)
        dsg = (dfv - dkk) * (1.0 - lbv)
        dq_ref[...] = dq * (sq * (1.0 + qh * (1.0 - sq)))
        df_ref[...] = dsg * sg * sgn
        di_ref[...] = dv
        part = jnp.sum(dfv * sgn - dkk * sgn, axis=0, keepdims=True)

        @pl.when((bi == 0) & (i == 0))
        def _():
            dlb_ref[...] = part

        @pl.when((bi > 0) | (i > 0))
        def _():
            dlb_ref[...] += part

    def rev(first):
        return lambda h, b, i: (b * nb + (nb - 1 - i), first + h)

    blk = (tb, LANES)
    oblk = pl.BlockSpec(blk, rev(0))
    return _run(
        body, (z, z, z, lb, pair, states, do), out_shape=(_sds((T, HG_W), F32),) * 3 + (_sds((1, HG_W), F32),),
        grid=(HG_HEADS, B, nb),
        in_specs=[pl.BlockSpec(blk, rev(12)), pl.BlockSpec(blk, rev(16)), pl.BlockSpec(blk, rev(20)),
                  pl.BlockSpec((1, LANES), lambda h, b, i: (0, h)), pl.BlockSpec((tb, tb), lambda h, b, i: (0, 0)),
                  pl.BlockSpec((nc * LANES, LANES), rev(0)), oblk],
        out_specs=(oblk, oblk, oblk, pl.BlockSpec((1, LANES), lambda h, b, i: (0, h))),
        scratch_shapes=[pltpu.VMEM((LANES, LANES), F32)], name=name)


def _hgout_fwd(o, z, g, ycat, *, name):
    T = o.shape[0]
    tm = _tile(T, 512, 8)

    def body(o_ref, gh_ref, g_ref, _, y_ref):
        ov = o_ref[...]
        gh = gh_ref[...]
        r = lax.rsqrt(_group_mean(ov * ov, HG_HEAD_DIM) + EPS)
        sg, _ = _sigmoid_pair(gh)
        y_ref[...] = (ov * r * g_ref[...] * (gh * sg)).astype(y_ref.dtype)

    blk = pl.BlockSpec((tm, HG_W), lambda i: (i, 0))
    return _run(body, (o, z, g, ycat), out_shape=_sds(ycat.shape, ycat.dtype), grid=(T // tm,),
                in_specs=[blk, pl.BlockSpec((tm, HG_W), lambda i: (i, 6)), pl.BlockSpec((1, HG_W), lambda i: (0, 0)),
                          pl.BlockSpec(memory_space=pl.ANY)],
                out_specs=pl.BlockSpec((tm, HG_W), lambda i: (i, ATT_W // HG_W)), name=name, aliases={3: 0})


def _hgout_bwd(o, z, g, dycat, *, name):
    T = o.shape[0]
    tm = _tile(T, 512, 8)

    def body(o_ref, gh_ref, g_ref, dy_ref, do_ref, dgh_ref, dg_ref):
        i = pl.program_id(0)
        ov = o_ref[...]
        gh = gh_ref[...]
        gv = g_ref[...]
        dy = dy_ref[...]
        r = lax.rsqrt(_group_mean(ov * ov, HG_HEAD_DIM) + EPS)
        xh = ov * r
        sg, _ = _sigmoid_pair(gh)
        dn = dy * (gh * sg)
        dgh_ref[...] = dy * xh * gv * (sg * (1.0 + gh * (1.0 - sg)))
        dng = dn * gv
        do_ref[...] = r * (dng - xh * _group_mean(dng * xh, HG_HEAD_DIM))
        part = jnp.sum(dn * xh, axis=0, keepdims=True)

        @pl.when(i == 0)
        def _():
            dg_ref[...] = part

        @pl.when(i > 0)
        def _():
            dg_ref[...] += part

    blk = pl.BlockSpec((tm, HG_W), lambda i: (i, 0))
    row = pl.BlockSpec((1, HG_W), lambda i: (0, 0))
    return _run(body, (o, z, g, dycat),
                out_shape=(_sds((T, HG_W), F32), _sds((T, HG_W), F32), _sds((1, HG_W), F32)), grid=(T // tm,),
                in_specs=[blk, pl.BlockSpec((tm, HG_W), lambda i: (i, 6)), row,
                          pl.BlockSpec((tm, HG_W), lambda i: (i, 1))],
                out_specs=(blk, blk, row), name=name)


def _lower_bounds_fwd(hg_lb):
    L = hg_lb.shape[0]

    def body(x_ref, o_ref):
        rows = [x_ref[l:l + 1, :] for l in range(L)]
        mx = functools.reduce(jnp.maximum, rows)
        es = [jnp.exp(r - mx) for r in rows]
        tot = functools.reduce(jnp.add, es)
        acc = jnp.zeros_like(tot)
        for l in range(L):
            acc = acc + es[l] / tot
            o_ref[l:l + 1, :] = acc - es[0] / tot

    return _pcall(body, out_shape=_sds(hg_lb.shape, F32), name="lower_bounds_fwd")(hg_lb)


def _lower_bounds_bwd(hg_lb, dlower):
    L = hg_lb.shape[0]

    def body(x_ref, d_ref, o_ref):
        rows = [x_ref[l:l + 1, :] for l in range(L)]
        ds = [d_ref[l:l + 1, :] for l in range(L)]
        mx = functools.reduce(jnp.maximum, rows)
        es = [jnp.exp(r - mx) for r in rows]
        tot = functools.reduce(jnp.add, es)
        ps = [e / tot for e in es]
        dps = []
        for j in range(L):
            t = functools.reduce(jnp.add, ds[j:])
            if j == 0:
                t = t - functools.reduce(jnp.add, ds)
            dps.append(t)
        inner = functools.reduce(jnp.add, [p * dp for p, dp in zip(ps, dps)])
        for j in range(L):
            o_ref[j:j + 1, :] = ps[j] * (dps[j] - inner)

    return _pcall(body, out_shape=_sds(hg_lb.shape, F32), name="lower_bounds_bwd")(hg_lb, dlower)


def _memattn_fwd(qm, km, kv, *, B, S, NM, name):
    T, D = qm.shape
    E = D // MEM_HEADS
    tq = _tile(S, 512, 8)
    nq = S // tq
    scale = 1.0 / math.sqrt(E)

    def body(q_ref, k_ref, v_ref, o_ref):
        for h in range(MEM_HEADS):
            cs = slice(h * E, (h + 1) * E)
            s = lax.dot_general(q_ref[:, cs].astype(MXU_DTYPE), k_ref[:, cs].astype(MXU_DTYPE),
                                (((1,), (1,)), ((), ())), preferred_element_type=F32) * scale
            e = jnp.exp(s - jnp.max(s, axis=1, keepdims=True))
            pr = e / jnp.sum(e, axis=1, keepdims=True)
            o_ref[:, cs] = jnp.dot(pr.astype(MXU_DTYPE), v_ref[:, cs].astype(MXU_DTYPE),
                                   preferred_element_type=F32).astype(o_ref.dtype)

    return _run(
        body, (qm, km, kv), out_shape=_sds((T, D), MXU_DTYPE), grid=(B, nq),
        in_specs=[pl.BlockSpec((tq, D), lambda b, i: (b * nq + i, 0)), pl.BlockSpec((NM, D), lambda b, i: (b, 0)),
                  pl.BlockSpec((NM, D), lambda b, i: (b, 1))],
        out_specs=pl.BlockSpec((tq, D), lambda b, i: (b * nq + i, 0)), name=name)


def _memattn_bwd(qm, km, kv, do, *, B, S, NM, name):
    T, D = qm.shape
    E = D // MEM_HEADS
    tq = _tile(S, 512, 8)
    nq = S // tq
    scale = 1.0 / math.sqrt(E)

    def body(q_ref, k_ref, v_ref, do_ref, dq_ref, dk_ref, dv_ref):
        i = pl.program_id(1)

        @pl.when(i == 0)
        def _():
            dk_ref[...] = jnp.zeros_like(dk_ref)
            dv_ref[...] = jnp.zeros_like(dv_ref)

        for h in range(MEM_HEADS):
            cs = slice(h * E, (h + 1) * E)
            qh = q_ref[:, cs].astype(MXU_DTYPE)
            kh = k_ref[:, cs].astype(MXU_DTYPE)
            vh = v_ref[:, cs].astype(MXU_DTYPE)
            doh = do_ref[:, cs].astype(MXU_DTYPE)
            s = lax.dot_general(qh, kh, (((1,), (1,)), ((), ())), preferred_element_type=F32) * scale
            e = jnp.exp(s - jnp.max(s, axis=1, keepdims=True))
            pr = e / jnp.sum(e, axis=1, keepdims=True)
            dp = lax.dot_general(doh, vh, (((1,), (1,)), ((), ())), preferred_element_type=F32)
            ds = (pr * (dp - jnp.sum(dp * pr, axis=1, keepdims=True))).astype(MXU_DTYPE)
            dq_ref[:, cs] = jnp.dot(ds, kh, preferred_element_type=F32) * scale
            dk_ref[:, cs] += lax.dot_general(ds, qh, (((0,), (0,)), ((), ())), preferred_element_type=F32) * scale
            dv_ref[:, cs] += lax.dot_general(pr.astype(MXU_DTYPE), doh, (((0,), (0,)), ((), ())),
                                             preferred_element_type=F32)

    qblk = pl.BlockSpec((tq, D), lambda b, i: (b * nq + i, 0))
    mblk = pl.BlockSpec((NM, D), lambda b, i: (b, 0))
    return _run(
        body, (qm, km, kv, do), out_shape=(_sds((T, D), F32), _sds((B * NM, D), F32), _sds((B * NM, D), F32)),
        grid=(B, nq), in_specs=[qblk, mblk, pl.BlockSpec((NM, D), lambda b, i: (b, 1)), qblk],
        out_specs=(qblk, mblk, mblk), name=name)


def _loss_head(y, tgt):
    T, D = y.shape
    tm = _tile(T, 512, 8)

    def body(y_ref, t_ref, dy_ref, dyl_ref, l_ref):
        i = pl.program_id(0)
        err = y_ref[...] - t_ref[...]
        dy_ref[...] = err * (1.0 / D)
        dyl_ref[...] = (err * (1.0 / D)).astype(MXU_DTYPE)
        part = jnp.sum(jnp.sum(err * err, axis=1, keepdims=True), axis=0, keepdims=True) * (0.5 / D)

        @pl.when(i == 0)
        def _():
            l_ref[...] = part

        @pl.when(i > 0)
        def _():
            l_ref[...] += part

    blk = pl.BlockSpec((tm, D), lambda i: (i, 0))
    return _run(body, (y, tgt), out_shape=(_sds((T, D), F32), _sds((T, D), MXU_DTYPE), _sds((1, 1), F32)),
                grid=(T // tm,), in_specs=[blk, blk], out_specs=(blk, blk, pl.BlockSpec((1, 1), lambda i: (0, 0))),
                name="loss_head")


def _adamw(w, g, m, v):
    m = ADAM_B1 * m + (1.0 - ADAM_B1) * g
    v = ADAM_B2 * v + (1.0 - ADAM_B2) * jnp.square(g)
    m_hat = m / (1.0 - ADAM_B1 ** ADAM_STEP)
    v_hat = v / (1.0 - ADAM_B2 ** ADAM_STEP)
    delta = -ADAM_LR * (m_hat / (jnp.sqrt(v_hat) + ADAM_EPS) + ADAM_WD * w)
    return delta, m, v


def _sum_adamw(recv, w, m, v, layer, so_far=None):
    L, r, c = w.shape
    tr = _tile(r, 256, 8)

    def body(r_ref, w_ref, m_ref, v_ref, *rest):
        g_ref, d_ref, nm_ref, nv_ref = rest[-4:]
        g = r_ref[0].astype(F32)
        for j in range(1, N_DEV):
            g = g + r_ref[j].astype(F32)
        g_ref[...] = g
        d_ref[...], nm_ref[...], nv_ref[...] = _adamw(w_ref[...], g, m_ref[...], v_ref[...])

    blk = pl.BlockSpec((None, tr, c), lambda i: (layer, i, 0))
    hbm = pl.BlockSpec(memory_space=pl.ANY)
    args, in_specs, aliases = (recv, w, m, v), [pl.BlockSpec((N_DEV, tr, c), lambda i: (0, i, 0)), blk, blk, blk], {}
    if so_far is not None:
        args, in_specs, aliases = args + tuple(so_far), in_specs + [hbm] * 4, {4 + k: k for k in range(4)}
    return _run(body, args, out_shape=(_sds((L, r, c), F32),) * 4, grid=(r // tr,), in_specs=in_specs,
                out_specs=(blk,) * 4, name="grad_sum_adamw", aliases=aliases)


def _small_sum_adamw(gp, w, m, v):
    NR, C = gp.shape
    MESH = pl.DeviceIdType.MESH

    def body(gp_ref, w_ref, m_ref, v_ref, g_ref, d_ref, nm_ref, nv_ref, gath, send_sems, recv_sems):
        x, y, c = _mesh_pos()
        me = 4 * x + 2 * y + c
        gath[me] = gp_ref[...]
        copies = []
        for k in range(1, N_DEV):
            peer = (_flip(x, k & 4), _flip(y, k & 2), _flip(c, k & 1))
            cp = pltpu.make_async_remote_copy(src_ref=gp_ref, dst_ref=gath.at[me], send_sem=send_sems.at[k - 1],
                                              recv_sem=recv_sems.at[k - 1], device_id=peer, device_id_type=MESH)
            cp.start()
            copies.append(cp)
        for cp in copies:
            cp.wait()
        g = gath[0]
        for j in range(1, N_DEV):
            g = g + gath[j]
        g_ref[...] = g
        d_ref[...], nm_ref[...], nv_ref[...] = _adamw(w_ref[...], g, m_ref[...], v_ref[...])

    vm = pl.BlockSpec(memory_space=pltpu.VMEM)
    return _pcall(body, out_shape=(_sds((NR, C), F32),) * 4, in_specs=[vm] * 4, out_specs=(vm,) * 4,
                  scratch_shapes=[pltpu.VMEM((N_DEV, NR, C), F32), pltpu.SemaphoreType.DMA((7,)),
                                  pltpu.SemaphoreType.DMA((7,))], name="small_sum_adamw")(gp, w, m, v)


def _gathered(gath, name):
    _, r, c = gath.shape
    return gath.transpose(1, 0, 2).reshape(r, N_DEV * c) if name in COL_SHARDED else gath.reshape(N_DEV * r, c)


def _scatter_blocks(grad, name):
    rows, cols = grad.shape
    if name in COL_SHARDED:
        return grad.reshape(rows, N_DEV, cols // N_DEV).transpose(1, 0, 2)
    return grad.reshape(N_DEV, rows // N_DEV, cols)


def _pack_small(d, prefix, loss=None):
    parts = []
    for n in SMALL:
        a = d[prefix + n].reshape(-1).astype(F32)
        pad = (-a.shape[0]) % LANES
        parts.append(jnp.pad(a, (0, pad)) if pad else a)
    tail = jnp.zeros((LANES,), F32)
    if loss is not None:
        tail = tail.at[0].set(loss)
    flat = jnp.concatenate(parts + [tail])
    pad = (-flat.shape[0]) % (8 * LANES)
    return jnp.pad(flat, (0, pad)).reshape(-1, LANES)


def _unpack_small(packed, shapes):
    flat = packed.reshape(-1)
    out, off = {}, 0
    for n in SMALL:
        size = math.prod(shapes[n])
        out[n] = flat[off:off + size].reshape(shapes[n])
        off += size + (-size) % LANES
    return out, flat[off]


def _row(a):
    return a.reshape(1, -1).astype(F32)


def _layer_fwd(x, memf, w_in, shards, next_in, sp, lb, tab, dims):
    B, S, NM = dims
    D = x.shape[1]
    W = {"w_in": w_in}
    h1 = _norm_fwd(x, sp["norm1_g"], E=D, W=D, out_dtype=MXU_DTYPE, name="norm1_fwd")
    z, gath = _matmul(h1, w_in, carry=("gather", [shards[n].astype(MXU_DTYPE) for n in GATHER_MIX]), name="in_proj")
    W.update({n: _gathered(t, n) for n, t in zip(GATHER_MIX, gath)})
    qn = _norm_fwd(z, sp["attn_qn_g"], E=ATT_HEAD_DIM, W=ATT_W, cb=0, out_dtype=MXU_DTYPE, name="qnorm_fwd")
    kn = _norm_fwd(z, sp["attn_kn_g"], E=ATT_HEAD_DIM, W=ATT_W, cb=1, out_dtype=MXU_DTYPE, name="knorm_fwd")
    (ya, lse, ycat), gath = _attn_fwd(tab["attn"], qn, kn, z, B=B, S=S, name="dilated_attn_fwd",
                                carry=("gather", [shards[n].astype(MXU_DTYPE) for n in GATHER_FF]))
    W.update({n: _gathered(t, n) for n, t in zip(GATHER_FF, gath)})
    o, states = _hgrn_fwd(z, lb, tab["hgrn"], B=B, S=S, name="hgrn_fwd")
    ycat = _hgout_fwd(o, z, sp["hg_onorm_g"], ycat, name="hgrn_out_fwd")
    x1 = _matmul(ycat, W["w_out"], epi="add", extra=x, name="out_proj")
    h2 = _norm_fwd(x1, sp["norm2_g"], E=D, W=D, out_dtype=MXU_DTYPE, name="norm2_fwd")
    qmp = _matmul(h2, W["w_mq"], name="mq_proj")
    qm = _norm_fwd(qmp, sp["mq_norm_g"], E=D // MEM_HEADS, W=D, out_dtype=MXU_DTYPE, name="mqnorm_fwd")
    mn = _norm_fwd(memf, sp["mem_norm_g"], E=D, W=D, out_dtype=MXU_DTYPE, name="memnorm_fwd")
    kv = _matmul(mn, W["w_mkv"], name="mkv_proj")
    km = _norm_fwd(kv, sp["mk_norm_g"], E=D // MEM_HEADS, W=D, cb=0, out_dtype=MXU_DTYPE, name="mknorm_fwd")
    om = _memattn_fwd(qm, km, kv, B=B, S=S, NM=NM, name="mem_attn_fwd")
    x2 = _matmul(om, W["w_mo"], epi="add", extra=x1, name="mo_proj")
    h3 = _norm_fwd(x2, sp["norm3_g"], E=D, W=D, out_dtype=MXU_DTYPE, name="norm3_fwd")
    if next_in is None:
        u, w_in_next = _matmul(h3, W["w_ff1"], epi="relu2", out_dtype=MXU_DTYPE, name="ff1"), None
    else:
        u, gath = _matmul(h3, W["w_ff1"], epi="relu2", out_dtype=MXU_DTYPE,
                          carry=("gather", [next_in.astype(MXU_DTYPE)]), name="ff1")
        w_in_next = _gathered(gath[0], "w_in")
    x3 = _matmul(u, W["w_ff2"], epi="add", extra=x2, name="ff2")
    saved = dict(x=x, h1=h1, z=z, qn=qn, kn=kn, ya=ya, lse=lse, o=o, states=states, ycat=ycat, x1=x1, h2=h2,
                 qmp=qmp, qm=qm, mn=mn, kv=kv, km=km, om=om, x2=x2, h3=h3, u=u)
    return x3, saved, W, w_in_next


def _layer_bwd(dx3, dx3_low, s, memf, W, sp, lb, tab, dims):
    B, S, NM = dims
    D = dx3.shape[1]
    E_M = D // MEM_HEADS
    gw, gs, recv = {}, {}, {}

    def blocks(names):
        return ("scatter", [_scatter_blocks(gw[n], n) for n in names])

    def received(names, got):
        recv.update(dict(zip(names, got)))

    da = _matmul(dx3_low, W["w_ff2"], tb=True, epi="relu2grad", extra=s["u"], out_dtype=MXU_DTYPE, name="ff2_dgrad")
    gw["w_ff2"] = _matmul(s["u"], dx3_low, ta=True, out_dtype=MXU_DTYPE, name="ff2_wgrad")
    dh3, got = _matmul(da, W["w_ff1"], tb=True, out_dtype=MXU_DTYPE, carry=blocks(SCATTER_A), name="ff1_dgrad")
    received(SCATTER_A, got)
    gw["w_ff1"] = _matmul(s["h3"], da, ta=True, out_dtype=MXU_DTYPE, name="ff1_wgrad")
    dx2, dx2_low, gs["norm3_g"] = _norm_bwd(s["x2"], sp["norm3_g"], dh3, E=D, W=D, res=dx3, low=True, name="norm3_bwd")
    dom = _matmul(dx2_low, W["w_mo"], tb=True, out_dtype=MXU_DTYPE, name="mo_dgrad")
    gw["w_mo"] = _matmul(s["om"], dx2_low, ta=True, out_dtype=MXU_DTYPE, name="mo_wgrad")
    dqm, dkm, dvm = _memattn_bwd(s["qm"], s["km"], s["kv"], dom, B=B, S=S, NM=NM, name="mem_attn_bwd")
    dqmp, gs["mq_norm_g"] = _norm_bwd(s["qmp"], sp["mq_norm_g"], dqm, E=E_M, W=D, fold=E_M, out_dtype=MXU_DTYPE,
                                      name="mqnorm_bwd")
    dkmp, gs["mk_norm_g"] = _norm_bwd(s["kv"], sp["mk_norm_g"], dkm, E=E_M, W=D, cb=0, fold=E_M,
                                      out_dtype=MXU_DTYPE, name="mknorm_bwd")
    dkv = jnp.concatenate([dkmp, dvm.astype(MXU_DTYPE)], axis=1)
    dh2 = _matmul(dqmp, W["w_mq"], tb=True, out_dtype=MXU_DTYPE, name="mq_dgrad")
    gw["w_mq"] = _matmul(s["h2"], dqmp, ta=True, out_dtype=MXU_DTYPE, name="mq_wgrad")
    dmn = _matmul(dkv, W["w_mkv"], tb=True, out_dtype=MXU_DTYPE, name="mkv_dgrad")
    gw["w_mkv"] = _matmul(s["mn"], dkv, ta=True, out_dtype=MXU_DTYPE, name="mkv_wgrad")
    _, gs["mem_norm_g"] = _norm_bwd(memf, sp["mem_norm_g"], dmn, E=D, W=D, name="memnorm_bwd")
    dx1, dx1_low, gs["norm2_g"] = _norm_bwd(s["x1"], sp["norm2_g"], dh2, E=D, W=D, res=dx2, low=True, name="norm2_bwd")
    dycat = _matmul(dx1_low, W["w_out"], tb=True, out_dtype=MXU_DTYPE, name="out_dgrad")
    gw["w_out"] = _matmul(s["ycat"], dx1_low, ta=True, out_dtype=MXU_DTYPE, name="out_wgrad")
    do_hg, dgh, gs["hg_onorm_g"] = _hgout_bwd(s["o"], s["z"], sp["hg_onorm_g"], dycat, name="hgrn_out_bwd")
    dqh, dfh, dih, dlb = _hgrn_bwd(s["z"], lb, tab["hgrn"], s["states"], do_hg, B=B, S=S, name="hgrn_bwd")
    (dqn, dkn, dv), got = _attn_bwd(tab["attn"], s["qn"], s["kn"], s["z"], s["ya"], s["lse"], dycat, B=B, S=S,
                                     carry=blocks(SCATTER_B), name="dilated_attn_bwd")
    received(SCATTER_B, got)
    dq, gs["attn_qn_g"] = _norm_bwd(s["z"], sp["attn_qn_g"], dqn, E=ATT_HEAD_DIM, W=ATT_W, cb=0, fold=ATT_HEAD_DIM,
                                    out_dtype=MXU_DTYPE, name="qnorm_bwd")
    dk, gs["attn_kn_g"] = _norm_bwd(s["z"], sp["attn_kn_g"], dkn, E=ATT_HEAD_DIM, W=ATT_W, cb=1, fold=ATT_HEAD_DIM,
                                    out_dtype=MXU_DTYPE, name="knorm_bwd")
    dz = jnp.concatenate([dq, dk] + [t.astype(MXU_DTYPE) for t in (dv, dqh, dfh, dih, dgh)], axis=1)
    gw["w_in"] = _matmul(s["h1"], dz, ta=True, out_dtype=MXU_DTYPE, name="in_wgrad")
    dh1, got = _matmul(dz, W["w_in"], tb=True, out_dtype=MXU_DTYPE, carry=blocks(SCATTER_C), name="in_dgrad")
    received(SCATTER_C, got)
    dx0, dx0_low, gs["norm1_g"] = _norm_bwd(s["x"], sp["norm1_g"], dh1, E=D, W=D, res=dx1, low=True, name="norm1_bwd")
    gs["attn_qn_g"] = gs["attn_qn_g"][:, :ATT_HEAD_DIM]
    gs["attn_kn_g"] = gs["attn_kn_g"][:, :ATT_HEAD_DIM]
    return dx0, dx0_low, recv, gs, dlb


def kernel(x, mem, norm1_g, w_in, attn_qn_g, attn_kn_g, hg_lb, hg_onorm_g, w_out, norm2_g, mem_norm_g, w_mq, w_mkv, mq_norm_g, mk_norm_g, w_mo, norm3_g, w_ff1, w_ff2, loss_target, m_norm1_g, m_w_in, m_attn_qn_g, m_attn_kn_g, m_hg_lb, m_hg_onorm_g, m_w_out, m_norm2_g, m_mem_norm_g, m_w_mq, m_w_mkv, m_mq_norm_g, m_mk_norm_g, m_w_mo, m_norm3_g, m_w_ff1, m_w_ff2, v_norm1_g, v_w_in, v_attn_qn_g, v_attn_kn_g, v_hg_lb, v_hg_onorm_g, v_w_out, v_norm2_g, v_mem_norm_g, v_w_mq, v_w_mkv, v_mq_norm_g, v_mk_norm_g, v_w_mo, v_norm3_g, v_w_ff1, v_w_ff2):
    given = dict(norm1_g=norm1_g, w_in=w_in, attn_qn_g=attn_qn_g, attn_kn_g=attn_kn_g, hg_lb=hg_lb, hg_onorm_g=hg_onorm_g, w_out=w_out, norm2_g=norm2_g, mem_norm_g=mem_norm_g, w_mq=w_mq, w_mkv=w_mkv, mq_norm_g=mq_norm_g, mk_norm_g=mk_norm_g, w_mo=w_mo, norm3_g=norm3_g, w_ff1=w_ff1, w_ff2=w_ff2, m_norm1_g=m_norm1_g, m_w_in=m_w_in, m_attn_qn_g=m_attn_qn_g, m_attn_kn_g=m_attn_kn_g, m_hg_lb=m_hg_lb, m_hg_onorm_g=m_hg_onorm_g, m_w_out=m_w_out, m_norm2_g=m_norm2_g, m_mem_norm_g=m_mem_norm_g, m_w_mq=m_w_mq, m_w_mkv=m_w_mkv, m_mq_norm_g=m_mq_norm_g, m_mk_norm_g=m_mk_norm_g, m_w_mo=m_w_mo, m_norm3_g=m_norm3_g, m_w_ff1=m_w_ff1, m_w_ff2=m_w_ff2, v_norm1_g=v_norm1_g, v_w_in=v_w_in, v_attn_qn_g=v_attn_qn_g, v_attn_kn_g=v_attn_kn_g, v_hg_lb=v_hg_lb, v_hg_onorm_g=v_hg_onorm_g, v_w_out=v_w_out, v_norm2_g=v_norm2_g, v_mem_norm_g=v_mem_norm_g, v_w_mq=v_w_mq, v_w_mkv=v_w_mkv, v_mq_norm_g=v_mq_norm_g, v_mk_norm_g=v_mk_norm_g, v_w_mo=v_w_mo, v_norm3_g=v_norm3_g, v_w_ff1=v_w_ff1, v_w_ff2=v_w_ff2)
    B, S, D = x.shape
    NM = mem.shape[1]
    L = w_in.shape[0]
    dims = (B, S, NM)
    small_shapes = {n: given[n].shape for n in SMALL}

    def shards(prefix, l):
        return {n: given[prefix + n][l] for n in BIG}

    lower = _lower_bounds_fwd(hg_lb)
    tab = {"attn": _attn_tables(S, _tile(S, ATT_Q_BLOCK, 8)), "hgrn": _hgrn_pair_table(_tile(S, 256, HG_CHUNK))}
    xf = x.reshape(B * S, D)
    memf = mem.reshape(B * NM, D)

    def small_params(l):
        sp = {"norm1_g": _row(norm1_g[l]), "norm2_g": _row(norm2_g[l]), "norm3_g": _row(norm3_g[l]),
              "mem_norm_g": _row(mem_norm_g[l]), "hg_onorm_g": _row(hg_onorm_g[l]),
              "attn_qn_g": _row(jnp.tile(attn_qn_g[l], ATT_HEADS)), "attn_kn_g": _row(jnp.tile(attn_kn_g[l], ATT_HEADS)),
              "mq_norm_g": _row(jnp.tile(mq_norm_g[l], MEM_HEADS)), "mk_norm_g": _row(jnp.tile(mk_norm_g[l], MEM_HEADS))}
        return sp, _row(lower[l])

    w_in_full = _gathered(_all_gather([given["w_in"][0].astype(MXU_DTYPE)])[0], "w_in")
    saved, weights = [], []
    h = xf
    for l in range(L):
        sp, lb = small_params(l)
        next_in = given["w_in"][l + 1] if l + 1 < L else None
        h, s, W, w_in_full = _layer_fwd(h, memf, w_in_full, shards("", l), next_in, sp, lb, tab, dims)
        saved.append(s)
        weights.append(W)

    dh, dh_low, loss_part = _loss_head(h, loss_target.reshape(B * S, D))

    recv_layers, gs_layers, dlb_layers = [None] * L, [None] * L, [None] * L
    for l in range(L - 1, -1, -1):
        sp, lb = small_params(l)
        dh, dh_low, recv_layers[l], gs_layers[l], dlb_layers[l] = _layer_bwd(dh, dh_low, saved[l], memf, weights[l],
                                                                              sp, lb, tab, dims)
    grad_x = dh.reshape(B, S, D)

    big_out = {}
    for n in BIG:
        outs = None
        for l in range(L):
            outs = _sum_adamw(recv_layers[l][n], given[n], given["m_" + n], given["v_" + n], l, outs)
        big_out[n] = outs

    gs = {n: jnp.stack([gs_layers[l][n].reshape(small_shapes[n][1:]) for l in range(L)]) for n in SMALL if n != "hg_lb"}
    gs["hg_lb"] = _lower_bounds_bwd(hg_lb, jnp.concatenate(dlb_layers, axis=0))
    packed = _small_sum_adamw(_pack_small(gs, "", loss_part[0, 0]), _pack_small(given, ""), _pack_small(given, "m_"),
                              _pack_small(given, "v_"))
    small_out, loss = [], None
    for t in packed:
        d, tail = _unpack_small(t, small_shapes)
        small_out.append(d)
        loss = tail if loss is None else loss

    outs = [loss, grad_x]
    for k in range(4):
        outs += [small_out[k][n] if n in small_shapes else big_out[n][k] for n in WEIGHTS]
    return tuple(outs)
```

```python
import functools
import math

import jax
import jax.numpy as jnp
from jax import lax
from jax.experimental import pallas as pl
from jax.experimental.pallas import tpu as pltpu

F32 = jnp.float32
MXU_DTYPE = jnp.bfloat16
VMEM_LIMIT = 48 * 1024 * 1024

N_DEV = 8
EPS = 1e-6
NEG = -1e30
F_MIN = 1e-12
ATT_HEADS = 8
ATT_HEAD_DIM = 64
ATT_W = ATT_HEADS * ATT_HEAD_DIM
DILATIONS = (1, 4, 16)
DIL_STEPS = 128
ATT_Q_BLOCK = 256
ATT_KEY_EXTENT = 256
HG_HEADS = 4
HG_HEAD_DIM = 128
HG_W = HG_HEADS * HG_HEAD_DIM
HG_CHUNK = 32
HG_SUB = 8
MEM_HEADS = 4
LANES = 128

ADAM_LR = 0.001
ADAM_B1 = 0.9
ADAM_B2 = 0.999
ADAM_EPS = 1e-08
ADAM_WD = 0.01
ADAM_STEP = 10

BIG = ("w_in", "w_out", "w_mq", "w_mkv", "w_mo", "w_ff1", "w_ff2")
COL_SHARDED = ("w_in", "w_mkv", "w_ff1")
SMALL = ("norm1_g", "attn_qn_g", "attn_kn_g", "hg_lb", "hg_onorm_g", "norm2_g", "mem_norm_g",
         "mq_norm_g", "mk_norm_g", "norm3_g")
WEIGHTS = ("norm1_g", "w_in", "attn_qn_g", "attn_kn_g", "hg_lb", "hg_onorm_g", "w_out", "norm2_g",
           "mem_norm_g", "w_mq", "w_mkv", "mq_norm_g", "mk_norm_g", "w_mo", "norm3_g", "w_ff1", "w_ff2")
GATHER_IN = ("w_in",)
GATHER_MIX = ("w_out", "w_mq", "w_mkv", "w_mo")
GATHER_FF = ("w_ff1", "w_ff2")
SCATTER_A = ("w_ff2",)
SCATTER_B = ("w_ff1", "w_mo", "w_mq", "w_mkv", "w_out")
SCATTER_C = ("w_in",)


def _pcall(body, **kw):
    return pl.pallas_call(body, **kw)


def _params():
    return pltpu.CompilerParams(vmem_limit_bytes=VMEM_LIMIT)


def _tile(n, pref, mult):
    t = (min(n, pref) // mult) * mult
    while t >= mult:
        if n % t == 0:
            return t
        t -= mult
    return n


def _sds(shape, dtype):
    return jax.ShapeDtypeStruct(shape, dtype)


def _mesh_pos():
    return lax.axis_index("x"), lax.axis_index("y"), lax.axis_index("c")


def _flip(v, bit):
    return 1 - v if bit else v


def _gather_hooks(x_refs, out_refs, send_sems, recv_sems, local_sems):
    MESH = pl.DeviceIdType.MESH
    n = len(x_refs)
    x, y, c = _mesh_pos()
    me, sibling = (x, y, c), (x, y, 1 - c)
    chips = [(1 - x, y), (x, 1 - y), (1 - x, 1 - y)]

    def rows(a, px, py, pc):
        return out_refs[a].at[4 * px + 2 * py + pc]

    def copy(a, k, block, to, own=False):
        return pltpu.make_async_remote_copy(
            src_ref=x_refs[a] if own else rows(a, *block), dst_ref=rows(a, *block), send_sem=send_sems.at[n * k + a],
            recv_sem=recv_sems.at[n * k + a], device_id=to, device_id_type=MESH)

    mine = [pltpu.make_async_copy(x_refs[a], rows(a, *me), local_sems.at[a]) for a in range(n)]
    first = [copy(a, 0, me, sibling, own=True) for a in range(n)]
    first += [copy(a, 1 + j, me, (*chip, c), own=True) for j, chip in enumerate(chips) for a in range(n)]
    passed = [[copy(a, 4 + j, (*chip, c), sibling) for a in range(n)] for j, chip in enumerate(chips)]

    def start():
        for cp in mine + first:
            cp.start()

    def forward():
        for j, chip in enumerate(chips):
            for a in range(n):
                copy(a, 1 + j, (*chip, c), me).wait_recv()
                passed[j][a].start()

    def finish():
        for a in range(n):
            copy(a, 0, sibling, me).wait_recv()
        for j, chip in enumerate(chips):
            for a in range(n):
                copy(a, 4 + j, (*chip, 1 - c), me).wait_recv()
        for cp in first + [cp for group in passed for cp in group]:
            cp.wait_send()
        for cp in mine:
            cp.wait()

    return start, forward, finish


def _scatter_hooks(s_refs, r_refs, send_sems, recv_sems, local_sems):
    MESH = pl.DeviceIdType.MESH
    n = len(s_refs)
    x, y, c = _mesh_pos()
    me = 4 * x + 2 * y + c
    mine = [pltpu.make_async_copy(s_refs[a].at[me], r_refs[a].at[me], local_sems.at[a]) for a in range(n)]
    copies = []
    for m in range(1, N_DEV):
        px, py, pc = _flip(x, m & 4), _flip(y, m & 2), _flip(c, m & 1)
        for a in range(n):
            copies.append(pltpu.make_async_remote_copy(
                src_ref=s_refs[a].at[4 * px + 2 * py + pc], dst_ref=r_refs[a].at[me],
                send_sem=send_sems.at[n * (m - 1) + a], recv_sem=recv_sems.at[n * (m - 1) + a],
                device_id=(px, py, pc), device_id_type=MESH))

    def start():
        for cp in mine + copies:
            cp.start()

    def finish():
        for cp in copies:
            cp.wait()
        for cp in mine:
            cp.wait()

    return start, None, finish


def _exchange_sems(n):
    return [pltpu.SemaphoreType.DMA(((N_DEV - 1) * n,)), pltpu.SemaphoreType.DMA(((N_DEV - 1) * n,)),
            pltpu.SemaphoreType.DMA((n,))]


def _exchange_out(kind, xs):
    return _sds((N_DEV,) + xs.shape, xs.dtype) if kind == "gather" else _sds(xs.shape, xs.dtype)


def _run(body, args, *, out_shape, grid, in_specs, out_specs, scratch_shapes=(), name, carry=None, aliases=None):
    single = not isinstance(out_shape, (tuple, list))
    outs = (out_shape,) if single else tuple(out_shape)
    ospecs = (out_specs,) if single else tuple(out_specs)
    aliases = dict(aliases or {})
    if carry is None:
        res = _pcall(body, out_shape=outs, grid=grid, in_specs=list(in_specs), out_specs=ospecs,
                     scratch_shapes=list(scratch_shapes), input_output_aliases=aliases, compiler_params=_params(),
                     name=name)(*args)
        return res[0] if single else tuple(res)
    kind, xs = carry
    n_in, n_out, n_scr, n_x = len(args), len(outs), len(scratch_shapes), len(xs)
    total = math.prod(grid)

    def wrapped(*refs):
        refs = list(refs)
        ins, x_refs = refs[:n_in], refs[n_in:n_in + n_x]
        o, xo_refs = refs[n_in + n_x:n_in + n_x + n_out], refs[n_in + n_x + n_out:n_in + 2 * n_x + n_out]
        scr = refs[n_in + 2 * n_x + n_out:n_in + 2 * n_x + n_out + n_scr]
        sems = refs[n_in + 2 * n_x + n_out + n_scr:]
        step = pl.program_id(0)
        for ax in range(1, len(grid)):
            step = step * grid[ax] + pl.program_id(ax)
        start, forward, finish = (_gather_hooks if kind == "gather" else _scatter_hooks)(x_refs, xo_refs, *sems)
        pl.when(step == 0)(start)
        body(*ins, *o, *scr)
        if forward is not None:
            pl.when(step == (3 * total) // 4)(forward)
        pl.when(step == total - 1)(finish)

    hbm = pl.BlockSpec(memory_space=pl.ANY)
    res = _pcall(wrapped, out_shape=outs + tuple(_exchange_out(kind, t) for t in xs), grid=grid,
                 in_specs=list(in_specs) + [hbm] * n_x, out_specs=ospecs + (hbm,) * n_x,
                 scratch_shapes=list(scratch_shapes) + _exchange_sems(n_x), input_output_aliases=aliases,
                 compiler_params=_params(), name=name)(*args, *xs)
    main = tuple(res[:n_out])
    return (main[0] if single else main), list(res[n_out:])


def _all_gather(xs):
    n = len(xs)

    def body(*refs):
        start, forward, finish = _gather_hooks(refs[:n], refs[n:2 * n], *refs[2 * n:])
        start()
        forward()
        finish()

    hbm = pl.BlockSpec(memory_space=pl.ANY)
    return _pcall(body, out_shape=tuple(_exchange_out("gather", t) for t in xs), in_specs=[hbm] * n,
                  out_specs=(hbm,) * n, scratch_shapes=_exchange_sems(n), name="weights_all_gather")(*xs)


def _matmul(a, b, *, ta=False, tb=False, epi=None, extra=None, out_dtype=F32, carry=None, name):
    M, K = (a.shape[1], a.shape[0]) if ta else a.shape
    N = b.shape[0] if tb else b.shape[1]
    tm = _tile(M, 1024, 8 if not ta else LANES)
    tn = _tile(N, 1024, LANES)
    tk = _tile(K, 1024 if ta else 2048, LANES if not ta else 8)
    nk = K // tk
    dims = (((0 if ta else 1,), (1 if tb else 0,)), ((), ()))

    def body(a_ref, b_ref, *rest):
        e_ref = rest[0] if extra is not None else None
        o_ref = rest[1] if extra is not None else rest[0]

        def finish(r):
            if epi == "add":
                r = r + e_ref[...]
            elif epi == "relu2":
                r = jnp.square(jnp.maximum(r, 0.0))
            elif epi == "relu2grad":
                r = r * (2.0 * jnp.sqrt(e_ref[...].astype(F32)))
            o_ref[...] = r.astype(out_dtype)

        part = lax.dot_general(a_ref[...].astype(MXU_DTYPE), b_ref[...].astype(MXU_DTYPE), dims,
                               preferred_element_type=F32)
        if nk == 1:
            finish(part)
        else:
            acc_ref = rest[-1]
            k = pl.program_id(2)

            @pl.when(k == 0)
            def _():
                acc_ref[...] = part

            @pl.when(k > 0)
            def _():
                acc_ref[...] += part

            @pl.when(k == nk - 1)
            def _():
                finish(acc_ref[...])

    a_spec = pl.BlockSpec((tk, tm), lambda i, j, k: (k, i)) if ta else pl.BlockSpec((tm, tk), lambda i, j, k: (i, k))
    b_spec = pl.BlockSpec((tn, tk), lambda i, j, k: (j, k)) if tb else pl.BlockSpec((tk, tn), lambda i, j, k: (k, j))
    o_spec = pl.BlockSpec((tm, tn), lambda i, j, k: (i, j))
    in_specs = [a_spec, b_spec] + ([o_spec] if extra is not None else [])
    args = (a, b) + ((extra,) if extra is not None else ())
    return _run(body, args, out_shape=_sds((M, N), out_dtype), grid=(M // tm, N // tn, nk), in_specs=in_specs,
                out_specs=o_spec, scratch_shapes=[pltpu.VMEM((tm, tn), F32)] if nk > 1 else [], name=name, carry=carry)


def _group_mean(v, E):
    rows, W = v.shape
    if E == W:
        return jnp.mean(v, axis=-1, keepdims=True)
    pieces = []
    if E % LANES == 0:
        for g0 in range(0, W, E):
            m = jnp.mean(v[:, g0:g0 + E], axis=-1, keepdims=True)
            pieces.append(jnp.broadcast_to(m, (rows, E)))
    else:
        lane = lax.broadcasted_iota(jnp.int32, (rows, LANES), 1)
        for c0 in range(0, W, LANES):
            vc = v[:, c0:c0 + LANES]
            acc = jnp.zeros((rows, LANES), F32)
            for s0 in range(0, LANES, E):
                msk = (lane >= s0) & (lane < s0 + E)
                m = jnp.sum(jnp.where(msk, vc, 0.0), axis=-1, keepdims=True) * (1.0 / E)
                acc = jnp.where(msk, m, acc)
            pieces.append(acc)
    return jnp.concatenate(pieces, axis=-1)


def _fold_groups(t, E):
    W = t.shape[1]
    step = max(E, LANES)
    acc = t[:, 0:step]
    for c0 in range(step, W, step):
        acc = acc + t[:, c0:c0 + step]
    sh = LANES // 2
    while sh >= E:
        acc = acc + pltpu.roll(acc, sh, 1)
        sh //= 2
    return acc


def _norm_fwd(x, g, *, E, W, cb=0, out_dtype, name):
    M = x.shape[0]
    tm = _tile(M, 512, 8)

    def body(x_ref, g_ref, o_ref):
        xv = x_ref[...]
        r = lax.rsqrt(_group_mean(xv * xv, E) + EPS)
        o_ref[...] = (xv * r * g_ref[...]).astype(out_dtype)

    return _run(body, (x, g), out_shape=_sds((M, W), out_dtype), grid=(M // tm,),
                in_specs=[pl.BlockSpec((tm, W), lambda i: (i, cb)), pl.BlockSpec((1, W), lambda i: (0, 0))],
                out_specs=pl.BlockSpec((tm, W), lambda i: (i, 0)), name=name)


def _norm_bwd(x, g, dy, *, E, W, cb=0, res=None, fold=None, out_dtype=F32, low=False, name):
    M = x.shape[0]
    tm = _tile(M, 512, 8)
    n = M // tm
    gw = W if fold is None else max(fold, LANES)

    def body(x_ref, g_ref, dy_ref, *rest):
        rest = list(rest)
        r_ref = rest.pop(0) if res is not None else None
        dx_ref = rest.pop(0)
        dxl_ref = rest.pop(0) if low else None
        dg_ref, acc_ref = rest
        i = pl.program_id(0)
        xv = x_ref[...]
        r = lax.rsqrt(_group_mean(xv * xv, E) + EPS)
        xh = xv * r
        dyv = dy_ref[...].astype(F32)
        dyg = dyv * g_ref[...]
        dx = r * (dyg - xh * _group_mean(dyg * xh, E))
        if res is not None:
            dx = dx + r_ref[...]
        dx_ref[...] = dx.astype(out_dtype)
        if low:
            dxl_ref[...] = dx.astype(MXU_DTYPE)
        part = jnp.sum(dyv * xh, axis=0, keepdims=True)

        @pl.when(i == 0)
        def _():
            acc_ref[...] = part

        @pl.when(i > 0)
        def _():
            acc_ref[...] += part

        @pl.when(i == n - 1)
        def _():
            t = acc_ref[...]
            dg_ref[...] = t if fold is None else _fold_groups(t, fold)

    blk = pl.BlockSpec((tm, W), lambda i: (i, 0))
    in_specs = [pl.BlockSpec((tm, W), lambda i: (i, cb)), pl.BlockSpec((1, W), lambda i: (0, 0)), blk]
    args = [x, g, dy]
    if res is not None:
        in_specs.append(blk)
        args.append(res)
    lows = ((_sds((M, W), MXU_DTYPE),), (blk,)) if low else ((), ())
    return _run(body, tuple(args), out_shape=(_sds((M, W), out_dtype),) + lows[0] + (_sds((1, gw), F32),), grid=(n,),
                in_specs=in_specs, out_specs=(blk,) + lows[1] + (pl.BlockSpec((1, gw), lambda i: (0, 0)),),
                scratch_shapes=[pltpu.VMEM((1, W), F32)], name=name)


def _attn_tables(S, tq):
    nq = S // tq

    def body(o_ref):
        i = pl.program_id(0)
        d = (i * tq + lax.broadcasted_iota(jnp.int32, (tq, S), 0)) - lax.broadcasted_iota(jnp.int32, (tq, S), 1)
        cnt = jnp.zeros((tq, S), jnp.int32)
        for dil in DILATIONS:
            hit = (d <= DIL_STEPS * dil) if dil == 1 else (((d & (dil - 1)) == 0) & (d <= DIL_STEPS * dil))
            cnt = cnt + hit.astype(jnp.int32)
        ok = (d >= 0) & (cnt > 0)
        logm = jnp.where(cnt == 3, math.log(3.0), jnp.where(cnt == 2, math.log(2.0), 0.0))
        o_ref[0] = jnp.where(ok, logm, NEG).astype(F32)

    return _run(body, (), out_shape=_sds((nq, tq, S), F32), grid=(nq,), in_specs=[],
                out_specs=pl.BlockSpec((1, tq, S), lambda i: (i, 0, 0)), name="dilated_attn_tables")


def _alibi_slope(h):
    return 2.0 ** (-8.0 / ATT_HEADS * (h + 1))


def _key_positions(Sk):
    kpos = lax.broadcasted_iota(jnp.int32, (Sk, LANES), 0)
    return (kpos >> 8).astype(F32).astype(MXU_DTYPE), (kpos & 255).astype(F32).astype(MXU_DTYPE)


def _score_operands(q2s, k2, pos_hi, pos_lo, lane, sub, h):
    own = (lane < ATT_HEAD_DIM) if sub == 0 else (lane >= ATT_HEAD_DIM)
    spare = ATT_HEAD_DIM if sub == 0 else 0
    slope = _alibi_slope(h)
    terms = jnp.where(lane == spare, slope * 256.0, jnp.where(lane == spare + 1, slope, 0.0)).astype(q2s.dtype)
    qa = jnp.where(own, q2s, terms)
    ka = jnp.where(lane == spare, pos_hi, jnp.where(lane == spare + 1, pos_lo, k2))
    return own, qa, ka


def _key_extents(S, tq):
    ext = min(ATT_KEY_EXTENT, S)
    return ext, ext // tq, S // ext


def _attn_fwd(tab, qn, kn, z, *, B, S, carry=None, name):
    T = B * S
    tq = tab.shape[1]
    nq = S // tq
    ext, per, n_ext = _key_extents(S, tq)
    scale = 1.0 / math.sqrt(ATT_HEAD_DIM)
    dn_t = (((1,), (1,)), ((), ()))

    def body(tab_ref, q_ref, k_ref, v_ref, y_ref, lse_ref, ycat_ref):
        i = pl.program_id(1)
        lane = lax.broadcasted_iota(jnp.int32, (1, LANES), 1)
        lo = lane < ATT_HEAD_DIM

        def visit(Sk):
            base = tab_ref[0, :, :Sk]
            pos_hi, pos_lo = _key_positions(Sk)
            lse_blk = jnp.zeros((tq, LANES), F32)
            for p in range(ATT_HEADS // 2):
                cs = slice(p * LANES, (p + 1) * LANES)
                q2s = q_ref[:, cs].astype(MXU_DTYPE) * scale
                k2 = k_ref[:Sk, cs].astype(MXU_DTYPE)
                v2 = v_ref[:Sk, cs].astype(MXU_DTYPE)
                outs = []
                for sub in range(2):
                    h = 2 * p + sub
                    _, qa, ka = _score_operands(q2s, k2, pos_hi, pos_lo, lane, sub, h)
                    s = lax.dot_general(qa, ka, dn_t, preferred_element_type=F32) + base
                    mx = jnp.max(s, axis=1, keepdims=True)
                    e = jnp.exp(s - mx)
                    l = jnp.sum(e, axis=1, keepdims=True)
                    outs.append(jnp.dot(e.astype(MXU_DTYPE), v2, preferred_element_type=F32) / l)
                    lse_blk = jnp.where(lane == h, mx + jnp.log(l), lse_blk)
                y = jnp.where(lo, outs[0], outs[1])
                y_ref[:, cs] = y
                ycat_ref[:, cs] = y.astype(MXU_DTYPE)
            lse_ref[...] = lse_blk

        for e in range(n_ext):
            pl.when(i // per == e)(functools.partial(visit, (e + 1) * ext))

    return _run(
        body, (tab, qn, kn, z), grid=(B, nq),
        out_shape=(_sds((T, ATT_W), F32), _sds((T, LANES), F32), _sds((T, ATT_W + HG_W), MXU_DTYPE)),
        in_specs=[pl.BlockSpec((1, tq, S), lambda b, i: (i, 0, 0)),
                  pl.BlockSpec((tq, ATT_W), lambda b, i: (b * nq + i, 0)),
                  pl.BlockSpec((S, ATT_W), lambda b, i: (b, 0)),
                  pl.BlockSpec((S, ATT_W), lambda b, i: (b, 2))],
        out_specs=(pl.BlockSpec((tq, ATT_W), lambda b, i: (b * nq + i, 0)),
                   pl.BlockSpec((tq, LANES), lambda b, i: (b * nq + i, 0)),
                   pl.BlockSpec((tq, ATT_W), lambda b, i: (b * nq + i, 0))),
        name=name, carry=carry)


def _attn_bwd(tab, qn, kn, z, y, lse, dycat, *, B, S, carry=None, name):
    T = B * S
    tq = tab.shape[1]
    nq = S // tq
    ext, per, n_ext = _key_extents(S, tq)
    scale = 1.0 / math.sqrt(ATT_HEAD_DIM)
    dn_t = (((1,), (1,)), ((), ()))
    dn_o = (((0,), (0,)), ((), ()))

    def body(tab_ref, q_ref, k_ref, v_ref, y_ref, lse_ref, dy_ref, dq_ref, dk_ref, dv_ref):
        i = pl.program_id(1)

        @pl.when(i == 0)
        def _():
            dk_ref[...] = jnp.zeros_like(dk_ref)
            dv_ref[...] = jnp.zeros_like(dv_ref)

        lane = lax.broadcasted_iota(jnp.int32, (1, LANES), 1)
        lo = lane < ATT_HEAD_DIM

        def visit(Sk):
            base = tab_ref[0, :, :Sk]
            pos_hi, pos_lo = _key_positions(Sk)
            lse_blk = lse_ref[...]
            for p in range(ATT_HEADS // 2):
                cs = slice(p * LANES, (p + 1) * LANES)
                q2 = q_ref[:, cs].astype(MXU_DTYPE)
                q2s = q2 * scale
                k2 = k_ref[:Sk, cs].astype(MXU_DTYPE)
                v2 = v_ref[:Sk, cs].astype(MXU_DTYPE)
                do2 = dy_ref[:, cs]
                doy = do2 * y_ref[:, cs]
                do2m = do2.astype(MXU_DTYPE)
                dqs, dks, dvs = [], [], []
                for sub in range(2):
                    h = 2 * p + sub
                    own, qa, ka = _score_operands(q2s, k2, pos_hi, pos_lo, lane, sub, h)
                    s = lax.dot_general(qa, ka, dn_t, preferred_element_type=F32) + base
                    lse_h = jnp.sum(jnp.where(lane == h, lse_blk, 0.0), axis=1, keepdims=True)
                    pr = jnp.exp(s - lse_h)
                    dsum = jnp.sum(jnp.where(own, doy, 0.0), axis=1, keepdims=True)
                    dom = jnp.where(own, do2m, jnp.zeros_like(do2m))
                    dp = lax.dot_general(dom, v2, dn_t, preferred_element_type=F32)
                    ds = (pr * (dp - dsum)).astype(MXU_DTYPE)
                    dqs.append(jnp.dot(ds, k2, preferred_element_type=F32))
                    dks.append(lax.dot_general(ds, q2, dn_o, preferred_element_type=F32))
                    dvs.append(lax.dot_general(pr.astype(MXU_DTYPE), do2m, dn_o, preferred_element_type=F32))
                dq_ref[:, cs] = jnp.where(lo, dqs[0], dqs[1]) * scale
                dk_ref[:Sk, cs] += jnp.where(lo, dks[0], dks[1]) * scale
                dv_ref[:Sk, cs] += jnp.where(lo, dvs[0], dvs[1])

        for e in range(n_ext):
            pl.when(i // per == e)(functools.partial(visit, (e + 1) * ext))

    qblk = pl.BlockSpec((tq, ATT_W), lambda b, i: (b * nq + i, 0))
    sblk = pl.BlockSpec((S, ATT_W), lambda b, i: (b, 0))
    return _run(
        body, (tab, qn, kn, z, y, lse, dycat), out_shape=(_sds((T, ATT_W), F32),) * 3, grid=(B, nq),
        in_specs=[pl.BlockSpec((1, tq, S), lambda b, i: (i, 0, 0)), qblk, sblk,
                  pl.BlockSpec((S, ATT_W), lambda b, i: (b, 2)), qblk,
                  pl.BlockSpec((tq, LANES), lambda b, i: (b * nq + i, 0)), qblk],
        out_specs=(qblk, sblk, sblk), name=name, carry=carry)


def _sigmoid_pair(x):
    en = jnp.exp(-jnp.abs(x))
    big = 1.0 / (1.0 + en)
    small = en * big
    pos = x >= 0
    return jnp.where(pos, big, small), jnp.where(pos, small, big)


def _chunk_scan(x, pos, reverse):
    n = x.shape[0]
    sh = 1
    while sh < HG_CHUNK:
        if reverse:
            x = x + jnp.where(pos < HG_CHUNK - sh, pltpu.roll(x, n - sh, 0), 0.0)
        else:
            x = x + jnp.where(pos >= sh, pltpu.roll(x, sh, 0), 0.0)
        sh *= 2
    return x


def _hgrn_pair_table(tb):
    def body(o_ref):
        t = lax.broadcasted_iota(jnp.int32, (tb, tb), 0)
        s = lax.broadcasted_iota(jnp.int32, (tb, tb), 1)
        o_ref[...] = jnp.where((t // HG_CHUNK) == (s // HG_CHUNK), (t // HG_SUB) - (s // HG_SUB), 0).astype(F32)

    return _pcall(body, out_shape=_sds((tb, tb), F32), name="hgrn_pair_table")()


def _hgrn_cross_decays(b, pos):
    n = b.shape[0]
    pos8 = pos % HG_SUB
    end = b
    sh = 1
    while sh < HG_SUB:
        end = jnp.where((pos8 & sh) == 0, pltpu.roll(end, n - sh, 0), end)
        sh *= 2
    x = jnp.where(pos >= HG_SUB, jnp.exp(jnp.minimum(b - pltpu.roll(end, HG_SUB, 0), 0.0)), 0.0)
    ys = []
    for j in range(1, HG_CHUNK // HG_SUB):
        end_j = end if j == 1 else pltpu.roll(end, n - HG_SUB * (j - 1), 0)
        ys.append(jnp.where(pos < HG_CHUNK - HG_SUB * j, jnp.exp(jnp.minimum(end_j - b, 0.0)), 0.0))
    return x, ys


def _hgrn_cross_scores(qt, kts, pair):
    n = qt.shape[0]
    kcat = jnp.concatenate([k.astype(MXU_DTYPE) for k in kts], axis=0)
    p = lax.dot_general(qt.astype(MXU_DTYPE), kcat, (((1,), (1,)), ((), ())), preferred_element_type=F32)
    a = jnp.zeros((n, n), F32)
    for j in range(1, len(kts) + 1):
        a = jnp.where(pair == j, p[:, (j - 1) * n:j * n], a)
    return a, kcat


def _hgrn_gates(qh, fh, lb):
    sq, _ = _sigmoid_pair(qh)
    sg, sgn = _sigmoid_pair(fh)
    f = lb + (1.0 - lb) * sg
    g = jnp.log(jnp.maximum(f, F_MIN))
    kk = (1.0 - lb) * sgn
    return qh * sq, sq, sg, sgn, f, g, kk


def _hgrn_fwd(z, lb, pair, *, B, S, name):
    T = B * S
    tb = _tile(S, 256, HG_CHUNK)
    nc = tb // HG_CHUNK
    nb = S // tb
    dn_t = (((1,), (1,)), ((), ()))
    dn_o = (((0,), (0,)), ((), ()))

    def zcol(first):
        return lambda h, b, i: (b * nb + i, first + h)

    def body(q_ref, f_ref, v_ref, lb_ref, pair_ref, o_ref, st_ref, state):
        i = pl.program_id(2)

        @pl.when(i == 0)
        def _():
            state[...] = jnp.zeros_like(state)

        q, _, _, _, _, g, kk = _hgrn_gates(q_ref[...], f_ref[...], lb_ref[...])
        v = v_ref[...]
        pos = lax.broadcasted_iota(jnp.int32, (tb, LANES), 0) % HG_CHUNK
        pos8 = pos % HG_SUB
        b = _chunk_scan(g, pos, False)
        o = jnp.sum(q * kk, axis=1, keepdims=True) * v
        for d in range(1, HG_SUB):
            qd = pltpu.roll(q, tb - d, 0)
            bd = pltpu.roll(b, tb - d, 0)
            e = jnp.where(pos8 < HG_SUB - d, jnp.exp(jnp.minimum(bd - b, 0.0)), 0.0)
            a = jnp.sum(qd * kk * e, axis=1, keepdims=True)
            o = o + pltpu.roll(a * v, d, 0)
        x, ys = _hgrn_cross_decays(b, pos)
        a_cross, _ = _hgrn_cross_scores(q * x, [kk * y for y in ys], pair_ref[...])
        o = o + jnp.dot(a_cross.astype(MXU_DTYPE), v.astype(MXU_DTYPE), preferred_element_type=F32)
        qe = q * jnp.exp(b)
        chunks = [slice(c * HG_CHUNK, (c + 1) * HG_CHUNK) for c in range(nc)]
        b_last = [jnp.sum(jnp.where(pos[rs] == HG_CHUNK - 1, b[rs], 0.0), axis=0, keepdims=True) for rs in chunks]
        own = [lax.dot_general(v[rs].astype(MXU_DTYPE), (kk[rs] * jnp.exp(bl - b[rs])).astype(MXU_DTYPE), dn_o,
                               preferred_element_type=F32) for rs, bl in zip(chunks, b_last)]
        st = state[...]
        before = []
        for c in range(nc):
            before.append(st)
            st = st * jnp.exp(b_last[c]) + own[c]
        state[...] = st
        for c, rs in enumerate(chunks):
            st_ref[c * LANES:(c + 1) * LANES, :] = before[c]
            o_ref[rs, :] = o[rs] + lax.dot_general(qe[rs].astype(MXU_DTYPE), before[c].astype(MXU_DTYPE), dn_t,
                                                   preferred_element_type=F32)

    blk = (tb, LANES)
    return _run(
        body, (z, z, z, lb, pair), out_shape=(_sds((T, HG_W), F32), _sds((T // HG_CHUNK * LANES, HG_W), F32)),
        grid=(HG_HEADS, B, nb),
        in_specs=[pl.BlockSpec(blk, zcol(12)), pl.BlockSpec(blk, zcol(16)), pl.BlockSpec(blk, zcol(20)),
                  pl.BlockSpec((1, LANES), lambda h, b, i: (0, h)), pl.BlockSpec((tb, tb), lambda h, b, i: (0, 0))],
        out_specs=(pl.BlockSpec(blk, lambda h, b, i: (b * nb + i, h)),
                   pl.BlockSpec((nc * LANES, LANES), lambda h, b, i: (b * nb + i, h))),
        scratch_shapes=[pltpu.VMEM((LANES, LANES), F32)], name=name)


def _hgrn_bwd(z, lb, pair, states, do, *, B, S, name):
    T = B * S
    tb = _tile(S, 256, HG_CHUNK)
    nc = tb // HG_CHUNK
    nb = S // tb
    dn_t = (((1,), (1,)), ((), ()))
    dn_o = (((0,), (0,)), ((), ()))

    def body(q_ref, f_ref, v_ref, lb_ref, pair_ref, st_ref, do_ref, dq_ref, df_ref, di_ref, dlb_ref, dstate):
        bi = pl.program_id(1)
        i = pl.program_id(2)

        @pl.when(i == 0)
        def _():
            dstate[...] = jnp.zeros_like(dstate)

        qh = q_ref[...]
        lbv = lb_ref[...]
        q, sq, sg, sgn, f, g, kk = _hgrn_gates(qh, f_ref[...], lbv)
        v = v_ref[...]
        dov = do_ref[...]
        pos = lax.broadcasted_iota(jnp.int32, (tb, LANES), 0) % HG_CHUNK
        pos8 = pos % HG_SUB
        b = _chunk_scan(g, pos, False)

        a0 = jnp.sum(q * kk, axis=1, keepdims=True)
        da0 = jnp.sum(dov * v, axis=1, keepdims=True)
        dv = a0 * dov
        dkk = da0 * q
        dq = da0 * kk
        db = jnp.zeros((tb, LANES), F32)
        for d in range(1, HG_SUB):
            qd = pltpu.roll(q, tb - d, 0)
            bd = pltpu.roll(b, tb - d, 0)
            dod = pltpu.roll(dov, tb - d, 0)
            e = jnp.where(pos8 < HG_SUB - d, jnp.exp(jnp.minimum(bd - b, 0.0)), 0.0)
            a = jnp.sum(qd * kk * e, axis=1, keepdims=True)
            da = jnp.sum(dod * v, axis=1, keepdims=True)
            dv = dv + a * dod
            xe = da * e
            dkk = dkk + xe * qd
            t1 = xe * kk
            wt = t1 * qd
            dq = dq + pltpu.roll(t1, d, 0)
            db = db + pltpu.roll(wt, d, 0) - wt

        n_cross = HG_CHUNK // HG_SUB - 1
        pair = pair_ref[...]
        x, ys = _hgrn_cross_decays(b, pos)
        qt = q * x
        kts = [kk * y for y in ys]
        a_cross, kcat = _hgrn_cross_scores(qt, kts, pair)
        dom = dov.astype(MXU_DTYPE)
        da_all = lax.dot_general(dom, v.astype(MXU_DTYPE), dn_t, preferred_element_type=F32)
        dv = dv + lax.dot_general(a_cross.astype(MXU_DTYPE), dom, dn_o, preferred_element_type=F32)
        gcat = jnp.concatenate([jnp.where(pair == j, da_all, 0.0).astype(MXU_DTYPE) for j in range(1, n_cross + 1)],
                               axis=1)
        dqt = jnp.dot(gcat, kcat, preferred_element_type=F32)
        dkcat = lax.dot_general(gcat, qt.astype(MXU_DTYPE), dn_o, preferred_element_type=F32)
        dq = dq + dqt * x
        u = dqt * qt
        db = db + u
        d_end = -pltpu.roll(u, tb - HG_SUB, 0)
        for j in range(1, n_cross + 1):
            dkt = dkcat[(j - 1) * tb:j * tb]
            dkk = dkk + dkt * ys[j - 1]
            w = dkt * kts[j - 1]
            db = db - w
            d_end = d_end + (w if j == 1 else pltpu.roll(w, HG_SUB * (j - 1), 0))
        sh = 1
        while sh < HG_SUB:
            d_end = d_end + jnp.where(pos8 >= sh, pltpu.roll(d_end, sh, 0), 0.0)
            sh *= 2
        db = db + jnp.where(pos8 == HG_SUB - 1, d_end, 0.0)

        eb = jnp.exp(b)
        qe = q * eb
        chunks = [slice(c * HG_CHUNK, (c + 1) * HG_CHUNK) for c in range(nc)]
        b_lasts = [jnp.sum(jnp.where(pos[rs] == HG_CHUNK - 1, b[rs], 0.0), axis=0, keepdims=True) for rs in chunks]
        own = [lax.dot_general(dom[rs], qe[rs].astype(MXU_DTYPE), dn_o, preferred_element_type=F32) for rs in chunks]
        dst = dstate[...]
        after = [None] * nc
        for c in range(nc - 1, -1, -1):
            after[c] = dst
            dst = dst * jnp.exp(b_lasts[c]) + own[c]
        dstate[...] = dst
        dq_c, dkk_c, dv_c, db_c = [None] * nc, [None] * nc, [None] * nc, [None] * nc
        for c, rs in enumerate(chunks):
            st0 = st_ref[c * LANES:(c + 1) * LANES, :]
            eb_last = jnp.exp(b_lasts[c])
            er = jnp.exp(b_lasts[c] - b[rs])
            ke = kk[rs] * er
            dstm = after[c].astype(MXU_DTYPE)
            dqe = jnp.dot(dom[rs], st0.astype(MXU_DTYPE), preferred_element_type=F32)
            dv_c[c] = dv[rs] + lax.dot_general(ke.astype(MXU_DTYPE), dstm, dn_t, preferred_element_type=F32)
            dke = jnp.dot(v[rs].astype(MXU_DTYPE), dstm, preferred_element_type=F32)
            d_eb_last = jnp.sum(after[c] * st0, axis=0, keepdims=True)
            dq_c[c] = dq[rs] + dqe * eb[rs]
            dkk_c[c] = dkk[rs] + dke * er
            dkeke = dke * ke
            db_last = jnp.sum(dkeke, axis=0, keepdims=True) + d_eb_last * eb_last
            db_c[c] = db[rs] + dqe * qe[rs] - dkeke + jnp.where(pos[rs] == HG_CHUNK - 1, db_last, 0.0)
        dq = jnp.concatenate(dq_c, axis=0)
        dkk = jnp.concatenate(dkk_c, axis=0)
        dv = jnp.concatenate(dv_c, axis=0)
        dg = _chunk_scan(jnp.concatenate(db_c, axis=0), pos, True)

        dfv = dg * jnp.where(f > F_MIN, 1.0 / f, 0.0)
        dsg = (dfv - dkk) * (1.0 - lbv)
        dq_ref[...] = dq * (sq * (1.0 + qh * (1.0 - sq)))
        df_ref[...] = dsg * sg * sgn
        di_ref[...] = dv
        part = jnp.sum(dfv * sgn - dkk * sgn, axis=0, keepdims=True)

        @pl.when((bi == 0) & (i == 0))
        def _():
            dlb_ref[...] = part

        @pl.when((bi > 0) | (i > 0))
        def _():
            dlb_ref[...] += part

    def rev(first):
        return lambda h, b, i: (b * nb + (nb - 1 - i), first + h)

    blk = (tb, LANES)
    oblk = pl.BlockSpec(blk, rev(0))
    return _run(
        body, (z, z, z, lb, pair, states, do), out_shape=(_sds((T, HG_W), F32),) * 3 + (_sds((1, HG_W), F32),),
        grid=(HG_HEADS, B, nb),
        in_specs=[pl.BlockSpec(blk, rev(12)), pl.BlockSpec(blk, rev(16)), pl.BlockSpec(blk, rev(20)),
                  pl.BlockSpec((1, LANES), lambda h, b, i: (0, h)), pl.BlockSpec((tb, tb), lambda h, b, i: (0, 0)),
                  pl.BlockSpec((nc * LANES, LANES), rev(0)), oblk],
        out_specs=(oblk, oblk, oblk, pl.BlockSpec((1, LANES), lambda h, b, i: (0, h))),
        scratch_shapes=[pltpu.VMEM((LANES, LANES), F32)], name=name)


def _hgout_fwd(o, z, g, ycat, *, name):
    T = o.shape[0]
    tm = _tile(T, 512, 8)

    def body(o_ref, gh_ref, g_ref, _, y_ref):
        ov = o_ref[...]
        gh = gh_ref[...]
        r = lax.rsqrt(_group_mean(ov * ov, HG_HEAD_DIM) + EPS)
        sg, _ = _sigmoid_pair(gh)
        y_ref[...] = (ov * r * g_ref[...] * (gh * sg)).astype(y_ref.dtype)

    blk = pl.BlockSpec((tm, HG_W), lambda i: (i, 0))
    return _run(body, (o, z, g, ycat), out_shape=_sds(ycat.shape, ycat.dtype), grid=(T // tm,),
                in_specs=[blk, pl.BlockSpec((tm, HG_W), lambda i: (i, 6)), pl.BlockSpec((1, HG_W), lambda i: (0, 0)),
                          pl.BlockSpec(memory_space=pl.ANY)],
                out_specs=pl.BlockSpec((tm, HG_W), lambda i: (i, ATT_W // HG_W)), name=name, aliases={3: 0})


def _hgout_bwd(o, z, g, dycat, *, name):
    T = o.shape[0]
    tm = _tile(T, 512, 8)

    def body(o_ref, gh_ref, g_ref, dy_ref, do_ref, dgh_ref, dg_ref):
        i = pl.program_id(0)
        ov = o_ref[...]
        gh = gh_ref[...]
        gv = g_ref[...]
        dy = dy_ref[...]
        r = lax.rsqrt(_group_mean(ov * ov, HG_HEAD_DIM) + EPS)
        xh = ov * r
        sg, _ = _sigmoid_pair(gh)
        dn = dy * (gh * sg)
        dgh_ref[...] = dy * xh * gv * (sg * (1.0 + gh * (1.0 - sg)))
        dng = dn * gv
        do_ref[...] = r * (dng - xh * _group_mean(dng * xh, HG_HEAD_DIM))
        part = jnp.sum(dn * xh, axis=0, keepdims=True)

        @pl.when(i == 0)
        def _():
            dg_ref[...] = part

        @pl.when(i > 0)
        def _():
            dg_ref[...] += part

    blk = pl.BlockSpec((tm, HG_W), lambda i: (i, 0))
    row = pl.BlockSpec((1, HG_W), lambda i: (0, 0))
    return _run(body, (o, z, g, dycat),
                out_shape=(_sds((T, HG_W), F32), _sds((T, HG_W), F32), _sds((1, HG_W), F32)), grid=(T // tm,),
                in_specs=[blk, pl.BlockSpec((tm, HG_W), lambda i: (i, 6)), row,
                          pl.BlockSpec((tm, HG_W), lambda i: (i, 1))],
                out_specs=(blk, blk, row), name=name)


def _lower_bounds_fwd(hg_lb):
    L = hg_lb.shape[0]

    def body(x_ref, o_ref):
        rows = [x_ref[l:l + 1, :] for l in range(L)]
        mx = functools.reduce(jnp.maximum, rows)
        es = [jnp.exp(r - mx) for r in rows]
        tot = functools.reduce(jnp.add, es)
        acc = jnp.zeros_like(tot)
        for l in range(L):
            acc = acc + es[l] / tot
            o_ref[l:l + 1, :] = acc - es[0] / tot

    return _pcall(body, out_shape=_sds(hg_lb.shape, F32), name="lower_bounds_fwd")(hg_lb)


def _lower_bounds_bwd(hg_lb, dlower):
    L = hg_lb.shape[0]

    def body(x_ref, d_ref, o_ref):
        rows = [x_ref[l:l + 1, :] for l in range(L)]
        ds = [d_ref[l:l + 1, :] for l in range(L)]
        mx = functools.reduce(jnp.maximum, rows)
        es = [jnp.exp(r - mx) for r in rows]
        tot = functools.reduce(jnp.add, es)
        ps = [e / tot for e in es]
        dps = []
        for j in range(L):
            t = functools.reduce(jnp.add, ds[j:])
            if j == 0:
                t = t - functools.reduce(jnp.add, ds)
            dps.append(t)
        inner = functools.reduce(jnp.add, [p * dp for p, dp in zip(ps, dps)])
        for j in range(L):
            o_ref[j:j + 1, :] = ps[j] * (dps[j] - inner)

    return _pcall(body, out_shape=_sds(hg_lb.shape, F32), name="lower_bounds_bwd")(hg_lb, dlower)


def _memattn_fwd(qm, km, kv, *, B, S, NM, name):
    T, D = qm.shape
    E = D // MEM_HEADS
    tq = _tile(S, 512, 8)
    nq = S // tq
    scale = 1.0 / math.sqrt(E)

    def body(q_ref, k_ref, v_ref, o_ref):
        for h in range(MEM_HEADS):
            cs = slice(h * E, (h + 1) * E)
            s = lax.dot_general(q_ref[:, cs].astype(MXU_DTYPE), k_ref[:, cs].astype(MXU_DTYPE),
                                (((1,), (1,)), ((), ())), preferred_element_type=F32) * scale
            e = jnp.exp(s - jnp.max(s, axis=1, keepdims=True))
            pr = e / jnp.sum(e, axis=1, keepdims=True)
            o_ref[:, cs] = jnp.dot(pr.astype(MXU_DTYPE), v_ref[:, cs].astype(MXU_DTYPE),
                                   preferred_element_type=F32).astype(o_ref.dtype)

    return _run(
        body, (qm, km, kv), out_shape=_sds((T, D), MXU_DTYPE), grid=(B, nq),
        in_specs=[pl.BlockSpec((tq, D), lambda b, i: (b * nq + i, 0)), pl.BlockSpec((NM, D), lambda b, i: (b, 0)),
                  pl.BlockSpec((NM, D), lambda b, i: (b, 1))],
        out_specs=pl.BlockSpec((tq, D), lambda b, i: (b * nq + i, 0)), name=name)


def _memattn_bwd(qm, km, kv, do, *, B, S, NM, name):
    T, D = qm.shape
    E = D // MEM_HEADS
    tq = _tile(S, 512, 8)
    nq = S // tq
    scale = 1.0 / math.sqrt(E)

    def body(q_ref, k_ref, v_ref, do_ref, dq_ref, dk_ref, dv_ref):
        i = pl.program_id(1)

        @pl.when(i == 0)
        def _():
            dk_ref[...] = jnp.zeros_like(dk_ref)
            dv_ref[...] = jnp.zeros_like(dv_ref)

        for h in range(MEM_HEADS):
            cs = slice(h * E, (h + 1) * E)
            qh = q_ref[:, cs].astype(MXU_DTYPE)
            kh = k_ref[:, cs].astype(MXU_DTYPE)
            vh = v_ref[:, cs].astype(MXU_DTYPE)
            doh = do_ref[:, cs].astype(MXU_DTYPE)
            s = lax.dot_general(qh, kh, (((1,), (1,)), ((), ())), preferred_element_type=F32) * scale
            e = jnp.exp(s - jnp.max(s, axis=1, keepdims=True))
            pr = e / jnp.sum(e, axis=1, keepdims=True)
            dp = lax.dot_general(doh, vh, (((1,), (1,)), ((), ())), preferred_element_type=F32)
            ds = (pr * (dp - jnp.sum(dp * pr, axis=1, keepdims=True))).astype(MXU_DTYPE)
            dq_ref[:, cs] = jnp.dot(ds, kh, preferred_element_type=F32) * scale
            dk_ref[:, cs] += lax.dot_general(ds, qh, (((0,), (0,)), ((), ())), preferred_element_type=F32) * scale
            dv_ref[:, cs] += lax.dot_general(pr.astype(MXU_DTYPE), doh, (((0,), (0,)), ((), ())),
                                             preferred_element_type=F32)

    qblk = pl.BlockSpec((tq, D), lambda b, i: (b * nq + i, 0))
    mblk = pl.BlockSpec((NM, D), lambda b, i: (b, 0))
    return _run(
        body, (qm, km, kv, do), out_shape=(_sds((T, D), F32), _sds((B * NM, D), F32), _sds((B * NM, D), F32)),
        grid=(B, nq), in_specs=[qblk, mblk, pl.BlockSpec((NM, D), lambda b, i: (b, 1)), qblk],
        out_specs=(qblk, mblk, mblk), name=name)


def _loss_head(y, tgt):
    T, D = y.shape
    tm = _tile(T, 512, 8)

    def body(y_ref, t_ref, dy_ref, dyl_ref, l_ref):
        i = pl.program_id(0)
        err = y_ref[...] - t_ref[...]
        dy_ref[...] = err * (1.0 / D)
        dyl_ref[...] = (err * (1.0 / D)).astype(MXU_DTYPE)
        part = jnp.sum(jnp.sum(err * err, axis=1, keepdims=True), axis=0, keepdims=True) * (0.5 / D)

        @pl.when(i == 0)
        def _():
            l_ref[...] = part

        @pl.when(i > 0)
        def _():
            l_ref[...] += part

    blk = pl.BlockSpec((tm, D), lambda i: (i, 0))
    return _run(body, (y, tgt), out_shape=(_sds((T, D), F32), _sds((T, D), MXU_DTYPE), _sds((1, 1), F32)),
                grid=(T // tm,), in_specs=[blk, blk], out_specs=(blk, blk, pl.BlockSpec((1, 1), lambda i: (0, 0))),
                name="loss_head")


def _adamw(w, g, m, v):
    m = ADAM_B1 * m + (1.0 - ADAM_B1) * g
    v = ADAM_B2 * v + (1.0 - ADAM_B2) * jnp.square(g)
    m_hat = m / (1.0 - ADAM_B1 ** ADAM_STEP)
    v_hat = v / (1.0 - ADAM_B2 ** ADAM_STEP)
    delta = -ADAM_LR * (m_hat / (jnp.sqrt(v_hat) + ADAM_EPS) + ADAM_WD * w)
    return delta, m, v


def _sum_adamw(recv, w, m, v, layer, so_far=None):
    L, r, c = w.shape
    tr = _tile(r, 256, 8)

    def body(r_ref, w_ref, m_ref, v_ref, *rest):
        g_ref, d_ref, nm_ref, nv_ref = rest[-4:]
        g = r_ref[0].astype(F32)
        for j in range(1, N_DEV):
            g = g + r_ref[j].astype(F32)
        g_ref[...] = g
        d_ref[...], nm_ref[...], nv_ref[...] = _adamw(w_ref[...], g, m_ref[...], v_ref[...])

    blk = pl.BlockSpec((None, tr, c), lambda i: (layer, i, 0))
    hbm = pl.BlockSpec(memory_space=pl.ANY)
    args, in_specs, aliases = (recv, w, m, v), [pl.BlockSpec((N_DEV, tr, c), lambda i: (0, i, 0)), blk, blk, blk], {}
    if so_far is not None:
        args, in_specs, aliases = args + tuple(so_far), in_specs + [hbm] * 4, {4 + k: k for k in range(4)}
    return _run(body, args, out_shape=(_sds((L, r, c), F32),) * 4, grid=(r // tr,), in_specs=in_specs,
                out_specs=(blk,) * 4, name="grad_sum_adamw", aliases=aliases)


def _small_sum_adamw(gp, w, m, v):
    NR, C = gp.shape
    MESH = pl.DeviceIdType.MESH

    def body(gp_ref, w_ref, m_ref, v_ref, g_ref, d_ref, nm_ref, nv_ref, gath, send_sems, recv_sems):
        x, y, c = _mesh_pos()
        me = 4 * x + 2 * y + c
        gath[me] = gp_ref[...]
        copies = []
        for k in range(1, N_DEV):
            peer = (_flip(x, k & 4), _flip(y, k & 2), _flip(c, k & 1))
            cp = pltpu.make_async_remote_copy(src_ref=gp_ref, dst_ref=gath.at[me], send_sem=send_sems.at[k - 1],
                                              recv_sem=recv_sems.at[k - 1], device_id=peer, device_id_type=MESH)
            cp.start()
            copies.append(cp)
        for cp in copies:
            cp.wait()
        g = gath[0]
        for j in range(1, N_DEV):
            g = g + gath[j]
        g_ref[...] = g
        d_ref[...], nm_ref[...], nv_ref[...] = _adamw(w_ref[...], g, m_ref[...], v_ref[...])

    vm = pl.BlockSpec(memory_space=pltpu.VMEM)
    return _pcall(body, out_shape=(_sds((NR, C), F32),) * 4, in_specs=[vm] * 4, out_specs=(vm,) * 4,
                  scratch_shapes=[pltpu.VMEM((N_DEV, NR, C), F32), pltpu.SemaphoreType.DMA((7,)),
                                  pltpu.SemaphoreType.DMA((7,))], name="small_sum_adamw")(gp, w, m, v)


def _gathered(gath, name):
    _, r, c = gath.shape
    return gath.transpose(1, 0, 2).reshape(r, N_DEV * c) if name in COL_SHARDED else gath.reshape(N_DEV * r, c)


def _scatter_blocks(grad, name):
    rows, cols = grad.shape
    if name in COL_SHARDED:
        return grad.reshape(rows, N_DEV, cols // N_DEV).transpose(1, 0, 2)
    return grad.reshape(N_DEV, rows // N_DEV, cols)


def _pack_small(d, prefix, loss=None):
    parts = []
    for n in SMALL:
        a = d[prefix + n].reshape(-1).astype(F32)
        pad = (-a.shape[0]) % LANES
        parts.append(jnp.pad(a, (0, pad)) if pad else a)
    tail = jnp.zeros((LANES,), F32)
    if loss is not None:
        tail = tail.at[0].set(loss)
    flat = jnp.concatenate(parts + [tail])
    pad = (-flat.shape[0]) % (8 * LANES)
    return jnp.pad(flat, (0, pad)).reshape(-1, LANES)


def _unpack_small(packed, shapes):
    flat = packed.reshape(-1)
    out, off = {}, 0
    for n in SMALL:
        size = math.prod(shapes[n])
        out[n] = flat[off:off + size].reshape(shapes[n])
        off += size + (-size) % LANES
    return out, flat[off]


def _row(a):
    return a.reshape(1, -1).astype(F32)


def _layer_fwd(x, memf, w_in, shards, next_in, sp, lb, tab, dims):
    B, S, NM = dims
    D = x.shape[1]
    W = {"w_in": w_in}
    h1 = _norm_fwd(x, sp["norm1_g"], E=D, W=D, out_dtype=MXU_DTYPE, name="norm1_fwd")
    z, gath = _matmul(h1, w_in, carry=("gather", [shards[n].astype(MXU_DTYPE) for n in GATHER_MIX]), name="in_proj")
    W.update({n: _gathered(t, n) for n, t in zip(GATHER_MIX, gath)})
    qn = _norm_fwd(z, sp["attn_qn_g"], E=ATT_HEAD_DIM, W=ATT_W, cb=0, out_dtype=MXU_DTYPE, name="qnorm_fwd")
    kn = _norm_fwd(z, sp["attn_kn_g"], E=ATT_HEAD_DIM, W=ATT_W, cb=1, out_dtype=MXU_DTYPE, name="knorm_fwd")
    (ya, lse, ycat), gath = _attn_fwd(tab["attn"], qn, kn, z, B=B, S=S, name="dilated_attn_fwd",
                                carry=("gather", [shards[n].astype(MXU_DTYPE) for n in GATHER_FF]))
    W.update({n: _gathered(t, n) for n, t in zip(GATHER_FF, gath)})
    o, states = _hgrn_fwd(z, lb, tab["hgrn"], B=B, S=S, name="hgrn_fwd")
    ycat = _hgout_fwd(o, z, sp["hg_onorm_g"], ycat, name="hgrn_out_fwd")
    x1 = _matmul(ycat, W["w_out"], epi="add", extra=x, name="out_proj")
    h2 = _norm_fwd(x1, sp["norm2_g"], E=D, W=D, out_dtype=MXU_DTYPE, name="norm2_fwd")
    qmp = _matmul(h2, W["w_mq"], name="mq_proj")
    qm = _norm_fwd(qmp, sp["mq_norm_g"], E=D // MEM_HEADS, W=D, out_dtype=MXU_DTYPE, name="mqnorm_fwd")
    mn = _norm_fwd(memf, sp["mem_norm_g"], E=D, W=D, out_dtype=MXU_DTYPE, name="memnorm_fwd")
    kv = _matmul(mn, W["w_mkv"], name="mkv_proj")
    km = _norm_fwd(kv, sp["mk_norm_g"], E=D // MEM_HEADS, W=D, cb=0, out_dtype=MXU_DTYPE, name="mknorm_fwd")
    om = _memattn_fwd(qm, km, kv, B=B, S=S, NM=NM, name="mem_attn_fwd")
    x2 = _matmul(om, W["w_mo"], epi="add", extra=x1, name="mo_proj")
    h3 = _norm_fwd(x2, sp["norm3_g"], E=D, W=D, out_dtype=MXU_DTYPE, name="norm3_fwd")
    if next_in is None:
        u, w_in_next = _matmul(h3, W["w_ff1"], epi="relu2", out_dtype=MXU_DTYPE, name="ff1"), None
    else:
        u, gath = _matmul(h3, W["w_ff1"], epi="relu2", out_dtype=MXU_DTYPE,
                          carry=("gather", [next_in.astype(MXU_DTYPE)]), name="ff1")
        w_in_next = _gathered(gath[0], "w_in")
    x3 = _matmul(u, W["w_ff2"], epi="add", extra=x2, name="ff2")
    saved = dict(x=x, h1=h1, z=z, qn=qn, kn=kn, ya=ya, lse=lse, o=o, states=states, ycat=ycat, x1=x1, h2=h2,
                 qmp=qmp, qm=qm, mn=mn, kv=kv, km=km, om=om, x2=x2, h3=h3, u=u)
    return x3, saved, W, w_in_next


def _layer_bwd(dx3, dx3_low, s, memf, W, sp, lb, tab, dims):
    B, S, NM = dims
    D = dx3.shape[1]
    E_M = D // MEM_HEADS
    gw, gs, recv = {}, {}, {}

    def blocks(names):
        return ("scatter", [_scatter_blocks(gw[n], n) for n in names])

    def received(names, got):
        recv.update(dict(zip(names, got)))

    da = _matmul(dx3_low, W["w_ff2"], tb=True, epi="relu2grad", extra=s["u"], out_dtype=MXU_DTYPE, name="ff2_dgrad")
    gw["w_ff2"] = _matmul(s["u"], dx3_low, ta=True, out_dtype=MXU_DTYPE, name="ff2_wgrad")
    dh3, got = _matmul(da, W["w_ff1"], tb=True, out_dtype=MXU_DTYPE, carry=blocks(SCATTER_A), name="ff1_dgrad")
    received(SCATTER_A, got)
    gw["w_ff1"] = _matmul(s["h3"], da, ta=True, out_dtype=MXU_DTYPE, name="ff1_wgrad")
    dx2, dx2_low, gs["norm3_g"] = _norm_bwd(s["x2"], sp["norm3_g"], dh3, E=D, W=D, res=dx3, low=True, name="norm3_bwd")
    dom = _matmul(dx2_low, W["w_mo"], tb=True, out_dtype=MXU_DTYPE, name="mo_dgrad")
    gw["w_mo"] = _matmul(s["om"], dx2_low, ta=True, out_dtype=MXU_DTYPE, name="mo_wgrad")
    dqm, dkm, dvm = _memattn_bwd(s["qm"], s["km"], s["kv"], dom, B=B, S=S, NM=NM, name="mem_attn_bwd")
    dqmp, gs["mq_norm_g"] = _norm_bwd(s["qmp"], sp["mq_norm_g"], dqm, E=E_M, W=D, fold=E_M, out_dtype=MXU_DTYPE,
                                      name="mqnorm_bwd")
    dkmp, gs["mk_norm_g"] = _norm_bwd(s["kv"], sp["mk_norm_g"], dkm, E=E_M, W=D, cb=0, fold=E_M,
                                      out_dtype=MXU_DTYPE, name="mknorm_bwd")
    dkv = jnp.concatenate([dkmp, dvm.astype(MXU_DTYPE)], axis=1)
    dh2 = _matmul(dqmp, W["w_mq"], tb=True, out_dtype=MXU_DTYPE, name="mq_dgrad")
    gw["w_mq"] = _matmul(s["h2"], dqmp, ta=True, out_dtype=MXU_DTYPE, name="mq_wgrad")
    dmn = _matmul(dkv, W["w_mkv"], tb=True, out_dtype=MXU_DTYPE, name="mkv_dgrad")
    gw["w_mkv"] = _matmul(s["mn"], dkv, ta=True, out_dtype=MXU_DTYPE, name="mkv_wgrad")
    _, gs["mem_norm_g"] = _norm_bwd(memf, sp["mem_norm_g"], dmn, E=D, W=D, name="memnorm_bwd")
    dx1, dx1_low, gs["norm2_g"] = _norm_bwd(s["x1"], sp["norm2_g"], dh2, E=D, W=D, res=dx2, low=True, name="norm2_bwd")
    dycat = _matmul(dx1_low, W["w_out"], tb=True, out_dtype=MXU_DTYPE, name="out_dgrad")
    gw["w_out"] = _matmul(s["ycat"], dx1_low, ta=True, out_dtype=MXU_DTYPE, name="out_wgrad")
    do_hg, dgh, gs["hg_onorm_g"] = _hgout_bwd(s["o"], s["z"], sp["hg_onorm_g"], dycat, name="hgrn_out_bwd")
    dqh, dfh, dih, dlb = _hgrn_bwd(s["z"], lb, tab["hgrn"], s["states"], do_hg, B=B, S=S, name="hgrn_bwd")
    (dqn, dkn, dv), got = _attn_bwd(tab["attn"], s["qn"], s["kn"], s["z"], s["ya"], s["lse"], dycat, B=B, S=S,
                                     carry=blocks(SCATTER_B), name="dilated_attn_bwd")
    received(SCATTER_B, got)
    dq, gs["attn_qn_g"] = _norm_bwd(s["z"], sp["attn_qn_g"], dqn, E=ATT_HEAD_DIM, W=ATT_W, cb=0, fold=ATT_HEAD_DIM,
                                    out_dtype=MXU_DTYPE, name="qnorm_bwd")
    dk, gs["attn_kn_g"] = _norm_bwd(s["z"], sp["attn_kn_g"], dkn, E=ATT_HEAD_DIM, W=ATT_W, cb=1, fold=ATT_HEAD_DIM,
                                    out_dtype=MXU_DTYPE, name="knorm_bwd")
    dz = jnp.concatenate([dq, dk] + [t.astype(MXU_DTYPE) for t in (dv, dqh, dfh, dih, dgh)], axis=1)
    gw["w_in"] = _matmul(s["h1"], dz, ta=True, out_dtype=MXU_DTYPE, name="in_wgrad")
    dh1, got = _matmul(dz, W["w_in"], tb=True, out_dtype=MXU_DTYPE, carry=blocks(SCATTER_C), name="in_dgrad")
    received(SCATTER_C, got)
    dx0, dx0_low, gs["norm1_g"] = _norm_bwd(s["x"], sp["norm1_g"], dh1, E=D, W=D, res=dx1, low=True, name="norm1_bwd")
    gs["attn_qn_g"] = gs["attn_qn_g"][:, :ATT_HEAD_DIM]
    gs["attn_kn_g"] = gs["attn_kn_g"][:, :ATT_HEAD_DIM]
    return dx0, dx0_low, recv, gs, dlb


def kernel(x, mem, norm1_g, w_in, attn_qn_g, attn_kn_g, hg_lb, hg_onorm_g, w_out, norm2_g, mem_norm_g, w_mq, w_mkv, mq_norm_g, mk_norm_g, w_mo, norm3_g, w_ff1, w_ff2, loss_target, m_norm1_g, m_w_in, m_attn_qn_g, m_attn_kn_g, m_hg_lb, m_hg_onorm_g, m_w_out, m_norm2_g, m_mem_norm_g, m_w_mq, m_w_mkv, m_mq_norm_g, m_mk_norm_g, m_w_mo, m_norm3_g, m_w_ff1, m_w_ff2, v_norm1_g, v_w_in, v_attn_qn_g, v_attn_kn_g, v_hg_lb, v_hg_onorm_g, v_w_out, v_norm2_g, v_mem_norm_g, v_w_mq, v_w_mkv, v_mq_norm_g, v_mk_norm_g, v_w_mo, v_norm3_g, v_w_ff1, v_w_ff2):
    given = dict(norm1_g=norm1_g, w_in=w_in, attn_qn_g=attn_qn_g, attn_kn_g=attn_kn_g, hg_lb=hg_lb, hg_onorm_g=hg_onorm_g, w_out=w_out, norm2_g=norm2_g, mem_norm_g=mem_norm_g, w_mq=w_mq, w_mkv=w_mkv, mq_norm_g=mq_norm_g, mk_norm_g=mk_norm_g, w_mo=w_mo, norm3_g=norm3_g, w_ff1=w_ff1, w_ff2=w_ff2, m_norm1_g=m_norm1_g, m_w_in=m_w_in, m_attn_qn_g=m_attn_qn_g, m_attn_kn_g=m_attn_kn_g, m_hg_lb=m_hg_lb, m_hg_onorm_g=m_hg_onorm_g, m_w_out=m_w_out, m_norm2_g=m_norm2_g, m_mem_norm_g=m_mem_norm_g, m_w_mq=m_w_mq, m_w_mkv=m_w_mkv, m_mq_norm_g=m_mq_norm_g, m_mk_norm_g=m_mk_norm_g, m_w_mo=m_w_mo, m_norm3_g=m_norm3_g, m_w_ff1=m_w_ff1, m_w_ff2=m_w_ff2, v_norm1_g=v_norm1_g, v_w_in=v_w_in, v_attn_qn_g=v_attn_qn_g, v_attn_kn_g=v_attn_kn_g, v_hg_lb=v_hg_lb, v_hg_onorm_g=v_hg_onorm_g, v_w_out=v_w_out, v_norm2_g=v_norm2_g, v_mem_norm_g=v_mem_norm_g, v_w_mq=v_w_mq, v_w_mkv=v_w_mkv, v_mq_norm_g=v_mq_norm_g, v_mk_norm_g=v_mk_norm_g, v_w_mo=v_w_mo, v_norm3_g=v_norm3_g, v_w_ff1=v_w_ff1, v_w_ff2=v_w_ff2)
    B, S, D = x.shape
    NM = mem.shape[1]
    L = w_in.shape[0]
    dims = (B, S, NM)
    small_shapes = {n: given[n].shape for n in SMALL}

    def shards(prefix, l):
        return {n: given[prefix + n][l] for n in BIG}

    lower = _lower_bounds_fwd(hg_lb)
    tab = {"attn": _attn_tables(S, _tile(S, ATT_Q_BLOCK, 8)), "hgrn": _hgrn_pair_table(_tile(S, 256, HG_CHUNK))}
    xf = x.reshape(B * S, D)
    memf = mem.reshape(B * NM, D)

    def small_params(l):
        sp = {"norm1_g": _row(norm1_g[l]), "norm2_g": _row(norm2_g[l]), "norm3_g": _row(norm3_g[l]),
              "mem_norm_g": _row(mem_norm_g[l]), "hg_onorm_g": _row(hg_onorm_g[l]),
              "attn_qn_g": _row(jnp.tile(attn_qn_g[l], ATT_HEADS)), "attn_kn_g": _row(jnp.tile(attn_kn_g[l], ATT_HEADS)),
              "mq_norm_g": _row(jnp.tile(mq_norm_g[l], MEM_HEADS)), "mk_norm_g": _row(jnp.tile(mk_norm_g[l], MEM_HEADS))}
        return sp, _row(lower[l])

    w_in_full = _gathered(_all_gather([given["w_in"][0].astype(MXU_DTYPE)])[0], "w_in")
    saved, weights = [], []
    h = xf
    for l in range(L):
        sp, lb = small_params(l)
        next_in = given["w_in"][l + 1] if l + 1 < L else None
        h, s, W, w_in_full = _layer_fwd(h, memf, w_in_full, shards("", l), next_in, sp, lb, tab, dims)
        saved.append(s)
        weights.append(W)

    dh, dh_low, loss_part = _loss_head(h, loss_target.reshape(B * S, D))

    recv_layers, gs_layers, dlb_layers = [None] * L, [None] * L, [None] * L
    for l in range(L - 1, -1, -1):
        sp, lb = small_params(l)
        dh, dh_low, recv_layers[l], gs_layers[l], dlb_layers[l] = _layer_bwd(dh, dh_low, saved[l], memf, weights[l],
                                                                              sp, lb, tab, dims)
    grad_x = dh.reshape(B, S, D)

    big_out = {}
    for n in BIG:
        outs = None
        for l in range(L):
            outs = _sum_adamw(recv_layers[l][n], given[n], given["m_" + n], given["v_" + n], l, outs)
        big_out[n] = outs

    gs = {n: jnp.stack([gs_layers[l][n].reshape(small_shapes[n][1:]) for l in range(L)]) for n in SMALL if n != "hg_lb"}
    gs["hg_lb"] = _lower_bounds_bwd(hg_lb, jnp.concatenate(dlb_layers, axis=0))
    packed = _small_sum_adamw(_pack_small(gs, "", loss_part[0, 0]), _pack_small(given, ""), _pack_small(given, "m_"),
                              _pack_small(given, "v_"))
    small_out, loss = [], None
    for t in packed:
        d, tail = _unpack_small(t, small_shapes)
        small_out.append(d)
        loss = tail if loss is None else loss

    outs = [loss, grad_x]
    for k in range(4):
        outs += [small_out[k][n] if n in small_shapes else big_out[n][k] for n in WEIGHTS]
    return tuple(outs)
```

```python
import functools
import math

import jax
import jax.numpy as jnp
from jax import lax
from jax.experimental import pallas as pl
from jax.experimental.pallas import tpu as pltpu

F32 = jnp.float32
MXU_DTYPE = jnp.bfloat16
VMEM_LIMIT = 48 * 1024 * 1024

N_DEV = 8
EPS = 1e-6
NEG = -1e30
F_MIN = 1e-12
ATT_HEADS = 8
ATT_HEAD_DIM = 64
ATT_W = ATT_HEADS * ATT_HEAD_DIM
DILATIONS = (1, 4, 16)
DIL_STEPS = 128
ATT_Q_BLOCK = 256
ATT_KEY_EXTENT = 512
HG_HEADS = 4
HG_HEAD_DIM = 128
HG_W = HG_HEADS * HG_HEAD_DIM
HG_CHUNK = 32
HG_SUB = 8
MEM_HEADS = 4
LANES = 128

ADAM_LR = 0.001
ADAM_B1 = 0.9
ADAM_B2 = 0.999
ADAM_EPS = 1e-08
ADAM_WD = 0.01
ADAM_STEP = 10

BIG = ("w_in", "w_out", "w_mq", "w_mkv", "w_mo", "w_ff1", "w_ff2")
COL_SHARDED = ("w_in", "w_mkv", "w_ff1")
SMALL = ("norm1_g", "attn_qn_g", "attn_kn_g", "hg_lb", "hg_onorm_g", "norm2_g", "mem_norm_g",
         "mq_norm_g", "mk_norm_g", "norm3_g")
WEIGHTS = ("norm1_g", "w_in", "attn_qn_g", "attn_kn_g", "hg_lb", "hg_onorm_g", "w_out", "norm2_g",
           "mem_norm_g", "w_mq", "w_mkv", "mq_norm_g", "mk_norm_g", "w_mo", "norm3_g", "w_ff1", "w_ff2")
GATHER_IN = ("w_in",)
GATHER_MIX = ("w_out", "w_mq", "w_mkv", "w_mo")
GATHER_FF = ("w_ff1", "w_ff2")
SCATTER_A = ("w_ff2",)
SCATTER_B = ("w_ff1", "w_mo", "w_mq", "w_mkv", "w_out")
SCATTER_C = ("w_in",)


def _pcall(body, **kw):
    return pl.pallas_call(body, **kw)


def _params():
    return pltpu.CompilerParams(vmem_limit_bytes=VMEM_LIMIT)


def _tile(n, pref, mult):
    t = (min(n, pref) // mult) * mult
    while t >= mult:
        if n % t == 0:
            return t
        t -= mult
    return n


def _sds(shape, dtype):
    return jax.ShapeDtypeStruct(shape, dtype)


def _mesh_pos():
    return lax.axis_index("x"), lax.axis_index("y"), lax.axis_index("c")


def _flip(v, bit):
    return 1 - v if bit else v


def _gather_hooks(x_refs, out_refs, send_sems, recv_sems, local_sems):
    MESH = pl.DeviceIdType.MESH
    n = len(x_refs)
    x, y, c = _mesh_pos()
    me, sibling = (x, y, c), (x, y, 1 - c)
    chips = [(1 - x, y), (x, 1 - y), (1 - x, 1 - y)]

    def rows(a, px, py, pc):
        return out_refs[a].at[4 * px + 2 * py + pc]

    def copy(a, k, block, to, own=False):
        return pltpu.make_async_remote_copy(
            src_ref=x_refs[a] if own else rows(a, *block), dst_ref=rows(a, *block), send_sem=send_sems.at[n * k + a],
            recv_sem=recv_sems.at[n * k + a], device_id=to, device_id_type=MESH)

    mine = [pltpu.make_async_copy(x_refs[a], rows(a, *me), local_sems.at[a]) for a in range(n)]
    first = [copy(a, 0, me, sibling, own=True) for a in range(n)]
    first += [copy(a, 1 + j, me, (*chip, c), own=True) for j, chip in enumerate(chips) for a in range(n)]
    passed = [[copy(a, 4 + j, (*chip, c), sibling) for a in range(n)] for j, chip in enumerate(chips)]

    def start():
        for cp in mine + first:
            cp.start()

    def forward():
        for j, chip in enumerate(chips):
            for a in range(n):
                copy(a, 1 + j, (*chip, c), me).wait_recv()
                passed[j][a].start()

    def finish():
        for a in range(n):
            copy(a, 0, sibling, me).wait_recv()
        for j, chip in enumerate(chips):
            for a in range(n):
                copy(a, 4 + j, (*chip, 1 - c), me).wait_recv()
        for cp in first + [cp for group in passed for cp in group]:
            cp.wait_send()
        for cp in mine:
            cp.wait()

    return start, forward, finish


def _scatter_hooks(s_refs, r_refs, send_sems, recv_sems, local_sems):
    MESH = pl.DeviceIdType.MESH
    n = len(s_refs)
    x, y, c = _mesh_pos()
    me = 4 * x + 2 * y + c
    mine = [pltpu.make_async_copy(s_refs[a].at[me], r_refs[a].at[me], local_sems.at[a]) for a in range(n)]
    copies = []
    for m in range(1, N_DEV):
        px, py, pc = _flip(x, m & 4), _flip(y, m & 2), _flip(c, m & 1)
        for a in range(n):
            copies.append(pltpu.make_async_remote_copy(
                src_ref=s_refs[a].at[4 * px + 2 * py + pc], dst_ref=r_refs[a].at[me],
                send_sem=send_sems.at[n * (m - 1) + a], recv_sem=recv_sems.at[n * (m - 1) + a],
                device_id=(px, py, pc), device_id_type=MESH))

    def start():
        for cp in mine + copies:
            cp.start()

    def finish():
        for cp in copies:
            cp.wait()
        for cp in mine:
            cp.wait()

    return start, None, finish


def _exchange_sems(n):
    return [pltpu.SemaphoreType.DMA(((N_DEV - 1) * n,)), pltpu.SemaphoreType.DMA(((N_DEV - 1) * n,)),
            pltpu.SemaphoreType.DMA((n,))]


def _exchange_out(kind, xs):
    return _sds((N_DEV,) + xs.shape, xs.dtype) if kind == "gather" else _sds(xs.shape, xs.dtype)


def _run(body, args, *, out_shape, grid, in_specs, out_specs, scratch_shapes=(), name, carry=None, aliases=None):
    single = not isinstance(out_shape, (tuple, list))
    outs = (out_shape,) if single else tuple(out_shape)
    ospecs = (out_specs,) if single else tuple(out_specs)
    aliases = dict(aliases or {})
    if carry is None:
        res = _pcall(body, out_shape=outs, grid=grid, in_specs=list(in_specs), out_specs=ospecs,
                     scratch_shapes=list(scratch_shapes), input_output_aliases=aliases, compiler_params=_params(),
                     name=name)(*args)
        return res[0] if single else tuple(res)
    kind, xs = carry
    n_in, n_out, n_scr, n_x = len(args), len(outs), len(scratch_shapes), len(xs)
    total = math.prod(grid)

    def wrapped(*refs):
        refs = list(refs)
        ins, x_refs = refs[:n_in], refs[n_in:n_in + n_x]
        o, xo_refs = refs[n_in + n_x:n_in + n_x + n_out], refs[n_in + n_x + n_out:n_in + 2 * n_x + n_out]
        scr = refs[n_in + 2 * n_x + n_out:n_in + 2 * n_x + n_out + n_scr]
        sems = refs[n_in + 2 * n_x + n_out + n_scr:]
        step = pl.program_id(0)
        for ax in range(1, len(grid)):
            step = step * grid[ax] + pl.program_id(ax)
        start, forward, finish = (_gather_hooks if kind == "gather" else _scatter_hooks)(x_refs, xo_refs, *sems)
        pl.when(step == 0)(start)
        body(*ins, *o, *scr)
        if forward is not None:
            pl.when(step == (3 * total) // 4)(forward)
        pl.when(step == total - 1)(finish)

    hbm = pl.BlockSpec(memory_space=pl.ANY)
    res = _pcall(wrapped, out_shape=outs + tuple(_exchange_out(kind, t) for t in xs), grid=grid,
                 in_specs=list(in_specs) + [hbm] * n_x, out_specs=ospecs + (hbm,) * n_x,
                 scratch_shapes=list(scratch_shapes) + _exchange_sems(n_x), input_output_aliases=aliases,
                 compiler_params=_params(), name=name)(*args, *xs)
    main = tuple(res[:n_out])
    return (main[0] if single else main), list(res[n_out:])


def _all_gather(xs):
    n = len(xs)

    def body(*refs):
        start, forward, finish = _gather_hooks(refs[:n], refs[n:2 * n], *refs[2 * n:])
        start()
        forward()
        finish()

    hbm = pl.BlockSpec(memory_space=pl.ANY)
    return _pcall(body, out_shape=tuple(_exchange_out("gather", t) for t in xs), in_specs=[hbm] * n,
                  out_specs=(hbm,) * n, scratch_shapes=_exchange_sems(n), name="weights_all_gather")(*xs)


def _matmul(a, b, *, ta=False, tb=False, epi=None, extra=None, out_dtype=F32, carry=None, name):
    M, K = (a.shape[1], a.shape[0]) if ta else a.shape
    N = b.shape[0] if tb else b.shape[1]
    tm = _tile(M, 1024, 8 if not ta else LANES)
    tn = _tile(N, 1024, LANES)
    tk = _tile(K, 1024 if ta else 2048, LANES if not ta else 8)
    nk = K // tk
    dims = (((0 if ta else 1,), (1 if tb else 0,)), ((), ()))

    def body(a_ref, b_ref, *rest):
        e_ref = rest[0] if extra is not None else None
        o_ref = rest[1] if extra is not None else rest[0]

        def finish(r):
            if epi == "add":
                r = r + e_ref[...]
            elif epi == "relu2":
                r = jnp.square(jnp.maximum(r, 0.0))
            elif epi == "relu2grad":
                r = r * (2.0 * jnp.sqrt(e_ref[...].astype(F32)))
            o_ref[...] = r.astype(out_dtype)

        part = lax.dot_general(a_ref[...].astype(MXU_DTYPE), b_ref[...].astype(MXU_DTYPE), dims,
                               preferred_element_type=F32)
        if nk == 1:
            finish(part)
        else:
            acc_ref = rest[-1]
            k = pl.program_id(2)

            @pl.when(k == 0)
            def _():
                acc_ref[...] = part

            @pl.when(k > 0)
            def _():
                acc_ref[...] += part

            @pl.when(k == nk - 1)
            def _():
                finish(acc_ref[...])

    a_spec = pl.BlockSpec((tk, tm), lambda i, j, k: (k, i)) if ta else pl.BlockSpec((tm, tk), lambda i, j, k: (i, k))
    b_spec = pl.BlockSpec((tn, tk), lambda i, j, k: (j, k)) if tb else pl.BlockSpec((tk, tn), lambda i, j, k: (k, j))
    o_spec = pl.BlockSpec((tm, tn), lambda i, j, k: (i, j))
    in_specs = [a_spec, b_spec] + ([o_spec] if extra is not None else [])
    args = (a, b) + ((extra,) if extra is not None else ())
    return _run(body, args, out_shape=_sds((M, N), out_dtype), grid=(M // tm, N // tn, nk), in_specs=in_specs,
                out_specs=o_spec, scratch_shapes=[pltpu.VMEM((tm, tn), F32)] if nk > 1 else [], name=name, carry=carry)


def _group_mean(v, E):
    rows, W = v.shape
    if E == W:
        return jnp.mean(v, axis=-1, keepdims=True)
    pieces = []
    if E % LANES == 0:
        for g0 in range(0, W, E):
            m = jnp.mean(v[:, g0:g0 + E], axis=-1, keepdims=True)
            pieces.append(jnp.broadcast_to(m, (rows, E)))
    else:
        lane = lax.broadcasted_iota(jnp.int32, (rows, LANES), 1)
        for c0 in range(0, W, LANES):
            vc = v[:, c0:c0 + LANES]
            acc = jnp.zeros((rows, LANES), F32)
            for s0 in range(0, LANES, E):
                msk = (lane >= s0) & (lane < s0 + E)
                m = jnp.sum(jnp.where(msk, vc, 0.0), axis=-1, keepdims=True) * (1.0 / E)
                acc = jnp.where(msk, m, acc)
            pieces.append(acc)
    return jnp.concatenate(pieces, axis=-1)


def _fold_groups(t, E):
    W = t.shape[1]
    step = max(E, LANES)
    acc = t[:, 0:step]
    for c0 in range(step, W, step):
        acc = acc + t[:, c0:c0 + step]
    sh = LANES // 2
    while sh >= E:
        acc = acc + pltpu.roll(acc, sh, 1)
        sh //= 2
    return acc


def _norm_fwd(x, g, *, E, W, cb=0, out_dtype, name):
    M = x.shape[0]
    tm = _tile(M, 512, 8)

    def body(x_ref, g_ref, o_ref):
        xv = x_ref[...]
        r = lax.rsqrt(_group_mean(xv * xv, E) + EPS)
        o_ref[...] = (xv * r * g_ref[...]).astype(out_dtype)

    return _run(body, (x, g), out_shape=_sds((M, W), out_dtype), grid=(M // tm,),
                in_specs=[pl.BlockSpec((tm, W), lambda i: (i, cb)), pl.BlockSpec((1, W), lambda i: (0, 0))],
                out_specs=pl.BlockSpec((tm, W), lambda i: (i, 0)), name=name)


def _norm_bwd(x, g, dy, *, E, W, cb=0, res=None, fold=None, out_dtype=F32, low=False, name):
    M = x.shape[0]
    tm = _tile(M, 512, 8)
    n = M // tm
    gw = W if fold is None else max(fold, LANES)

    def body(x_ref, g_ref, dy_ref, *rest):
        rest = list(rest)
        r_ref = rest.pop(0) if res is not None else None
        dx_ref = rest.pop(0)
        dxl_ref = rest.pop(0) if low else None
        dg_ref, acc_ref = rest
        i = pl.program_id(0)
        xv = x_ref[...]
        r = lax.rsqrt(_group_mean(xv * xv, E) + EPS)
        xh = xv * r
        dyv = dy_ref[...].astype(F32)
        dyg = dyv * g_ref[...]
        dx = r * (dyg - xh * _group_mean(dyg * xh, E))
        if res is not None:
            dx = dx + r_ref[...]
        dx_ref[...] = dx.astype(out_dtype)
        if low:
            dxl_ref[...] = dx.astype(MXU_DTYPE)
        part = jnp.sum(dyv * xh, axis=0, keepdims=True)

        @pl.when(i == 0)
        def _():
            acc_ref[...] = part

        @pl.when(i > 0)
        def _():
            acc_ref[...] += part

        @pl.when(i == n - 1)
        def _():
            t = acc_ref[...]
            dg_ref[...] = t if fold is None else _fold_groups(t, fold)

    blk = pl.BlockSpec((tm, W), lambda i: (i, 0))
    in_specs = [pl.BlockSpec((tm, W), lambda i: (i, cb)), pl.BlockSpec((1, W), lambda i: (0, 0)), blk]
    args = [x, g, dy]
    if res is not None:
        in_specs.append(blk)
        args.append(res)
    lows = ((_sds((M, W), MXU_DTYPE),), (blk,)) if low else ((), ())
    return _run(body, tuple(args), out_shape=(_sds((M, W), out_dtype),) + lows[0] + (_sds((1, gw), F32),), grid=(n,),
                in_specs=in_specs, out_specs=(blk,) + lows[1] + (pl.BlockSpec((1, gw), lambda i: (0, 0)),),
                scratch_shapes=[pltpu.VMEM((1, W), F32)], name=name)


def _attn_tables(S, tq):
    nq = S // tq

    def body(o_ref):
        i = pl.program_id(0)
        d = (i * tq + lax.broadcasted_iota(jnp.int32, (tq, S), 0)) - lax.broadcasted_iota(jnp.int32, (tq, S), 1)
        cnt = jnp.zeros((tq, S), jnp.int32)
        for dil in DILATIONS:
            hit = (d <= DIL_STEPS * dil) if dil == 1 else (((d & (dil - 1)) == 0) & (d <= DIL_STEPS * dil))
            cnt = cnt + hit.astype(jnp.int32)
        ok = (d >= 0) & (cnt > 0)
        logm = jnp.where(cnt == 3, math.log(3.0), jnp.where(cnt == 2, math.log(2.0), 0.0))
        o_ref[0] = jnp.where(ok, logm, NEG).astype(F32)

    return _run(body, (), out_shape=_sds((nq, tq, S), F32), grid=(nq,), in_specs=[],
                out_specs=pl.BlockSpec((1, tq, S), lambda i: (i, 0, 0)), name="dilated_attn_tables")


def _alibi_slope(h):
    return 2.0 ** (-8.0 / ATT_HEADS * (h + 1))


def _key_positions(Sk):
    kpos = lax.broadcasted_iota(jnp.int32, (Sk, LANES), 0)
    return (kpos >> 8).astype(F32).astype(MXU_DTYPE), (kpos & 255).astype(F32).astype(MXU_DTYPE)


def _score_operands(q2s, k2, pos_hi, pos_lo, lane, sub, h):
    own = (lane < ATT_HEAD_DIM) if sub == 0 else (lane >= ATT_HEAD_DIM)
    spare = ATT_HEAD_DIM if sub == 0 else 0
    slope = _alibi_slope(h)
    terms = jnp.where(lane == spare, slope * 256.0, jnp.where(lane == spare + 1, slope, 0.0)).astype(q2s.dtype)
    qa = jnp.where(own, q2s, terms)
    ka = jnp.where(lane == spare, pos_hi, jnp.where(lane == spare + 1, pos_lo, k2))
    return own, qa, ka


def _key_extents(S, tq):
    ext = min(ATT_KEY_EXTENT, S)
    return ext, ext // tq, S // ext


def _attn_fwd(tab, qn, kn, z, *, B, S, carry=None, name):
    T = B * S
    tq = tab.shape[1]
    nq = S // tq
    ext, per, n_ext = _key_extents(S, tq)
    scale = 1.0 / math.sqrt(ATT_HEAD_DIM)
    dn_t = (((1,), (1,)), ((), ()))

    def body(tab_ref, q_ref, k_ref, v_ref, y_ref, lse_ref, ycat_ref):
        i = pl.program_id(1)
        lane = lax.broadcasted_iota(jnp.int32, (1, LANES), 1)
        lo = lane < ATT_HEAD_DIM

        def visit(Sk):
            base = tab_ref[0, :, :Sk]
            pos_hi, pos_lo = _key_positions(Sk)
            lse_blk = jnp.zeros((tq, LANES), F32)
            for p in range(ATT_HEADS // 2):
                cs = slice(p * LANES, (p + 1) * LANES)
                q2s = q_ref[:, cs].astype(MXU_DTYPE) * scale
                k2 = k_ref[:Sk, cs].astype(MXU_DTYPE)
                v2 = v_ref[:Sk, cs].astype(MXU_DTYPE)
                outs = []
                for sub in range(2):
                    h = 2 * p + sub
                    _, qa, ka = _score_operands(q2s, k2, pos_hi, pos_lo, lane, sub, h)
                    s = lax.dot_general(qa, ka, dn_t, preferred_element_type=F32) + base
                    mx = jnp.max(s, axis=1, keepdims=True)
                    e = jnp.exp(s - mx)
                    l = jnp.sum(e, axis=1, keepdims=True)
                    outs.append(jnp.dot(e.astype(MXU_DTYPE), v2, preferred_element_type=F32) / l)
                    lse_blk = jnp.where(lane == h, mx + jnp.log(l), lse_blk)
                y = jnp.where(lo, outs[0], outs[1])
                y_ref[:, cs] = y
                ycat_ref[:, cs] = y.astype(MXU_DTYPE)
            lse_ref[...] = lse_blk

        for e in range(n_ext):
            pl.when(i // per == e)(functools.partial(visit, (e + 1) * ext))

    return _run(
        body, (tab, qn, kn, z), grid=(B, nq),
        out_shape=(_sds((T, ATT_W), F32), _sds((T, LANES), F32), _sds((T, ATT_W + HG_W), MXU_DTYPE)),
        in_specs=[pl.BlockSpec((1, tq, S), lambda b, i: (i, 0, 0)),
                  pl.BlockSpec((tq, ATT_W), lambda b, i: (b * nq + i, 0)),
                  pl.BlockSpec((S, ATT_W), lambda b, i: (b, 0)),
                  pl.BlockSpec((S, ATT_W), lambda b, i: (b, 2))],
        out_specs=(pl.BlockSpec((tq, ATT_W), lambda b, i: (b * nq + i, 0)),
                   pl.BlockSpec((tq, LANES), lambda b, i: (b * nq + i, 0)),
                   pl.BlockSpec((tq, ATT_W), lambda b, i: (b * nq + i, 0))),
        name=name, carry=carry)


def _attn_bwd(tab, qn, kn, z, y, lse, dycat, *, B, S, carry=None, name):
    T = B * S
    tq = tab.shape[1]
    nq = S // tq
    ext, per, n_ext = _key_extents(S, tq)
    scale = 1.0 / math.sqrt(ATT_HEAD_DIM)
    dn_t = (((1,), (1,)), ((), ()))
    dn_o = (((0,), (0,)), ((), ()))

    def body(tab_ref, q_ref, k_ref, v_ref, y_ref, lse_ref, dy_ref, dq_ref, dk_ref, dv_ref):
        i = pl.program_id(1)

        @pl.when(i == 0)
        def _():
            dk_ref[...] = jnp.zeros_like(dk_ref)
            dv_ref[...] = jnp.zeros_like(dv_ref)

        lane = lax.broadcasted_iota(jnp.int32, (1, LANES), 1)
        lo = lane < ATT_HEAD_DIM

        def visit(Sk):
            base = tab_ref[0, :, :Sk]
            pos_hi, pos_lo = _key_positions(Sk)
            lse_blk = lse_ref[...]
            for p in range(ATT_HEADS // 2):
                cs = slice(p * LANES, (p + 1) * LANES)
                q2 = q_ref[:, cs].astype(MXU_DTYPE)
                q2s = q2 * scale
                k2 = k_ref[:Sk, cs].astype(MXU_DTYPE)
                v2 = v_ref[:Sk, cs].astype(MXU_DTYPE)
                do2 = dy_ref[:, cs]
                doy = do2 * y_ref[:, cs]
                do2m = do2.astype(MXU_DTYPE)
                dqs, dks, dvs = [], [], []
                for sub in range(2):
                    h = 2 * p + sub
                    own, qa, ka = _score_operands(q2s, k2, pos_hi, pos_lo, lane, sub, h)
                    s = lax.dot_general(qa, ka, dn_t, preferred_element_type=F32) + base
                    lse_h = jnp.sum(jnp.where(lane == h, lse_blk, 0.0), axis=1, keepdims=True)
                    pr = jnp.exp(s - lse_h)
                    dsum = jnp.sum(jnp.where(own, doy, 0.0), axis=1, keepdims=True)
                    dom = jnp.where(own, do2m, jnp.zeros_like(do2m))
                    dp = lax.dot_general(dom, v2, dn_t, preferred_element_type=F32)
                    ds = (pr * (dp - dsum)).astype(MXU_DTYPE)
                    dqs.append(jnp.dot(ds, k2, preferred_element_type=F32))
                    dks.append(lax.dot_general(ds, q2, dn_o, preferred_element_type=F32))
                    dvs.append(lax.dot_general(pr.astype(MXU_DTYPE), do2m, dn_o, preferred_element_type=F32))
                dq_ref[:, cs] = jnp.where(lo, dqs[0], dqs[1]) * scale
                dk_ref[:Sk, cs] += jnp.where(lo, dks[0], dks[1]) * scale
                dv_ref[:Sk, cs] += jnp.where(lo, dvs[0], dvs[1])

        for e in range(n_ext):
            pl.when(i // per == e)(functools.partial(visit, (e + 1) * ext))

    qblk = pl.BlockSpec((tq, ATT_W), lambda b, i: (b * nq + i, 0))
    sblk = pl.BlockSpec((S, ATT_W), lambda b, i: (b, 0))
    return _run(
        body, (tab, qn, kn, z, y, lse, dycat), out_shape=(_sds((T, ATT_W), F32),) * 3, grid=(B, nq),
        in_specs=[pl.BlockSpec((1, tq, S), lambda b, i: (i, 0, 0)), qblk, sblk,
                  pl.BlockSpec((S, ATT_W), lambda b, i: (b, 2)), qblk,
                  pl.BlockSpec((tq, LANES), lambda b, i: (b * nq + i, 0)), qblk],
        out_specs=(qblk, sblk, sblk), name=name, carry=carry)


def _sigmoid_pair(x):
    en = jnp.exp(-jnp.abs(x))
    big = 1.0 / (1.0 + en)
    small = en * big
    pos = x >= 0
    return jnp.where(pos, big, small), jnp.where(pos, small, big)


def _chunk_scan(x, pos, reverse):
    n = x.shape[0]
    sh = 1
    while sh < HG_CHUNK:
        if reverse:
            x = x + jnp.where(pos < HG_CHUNK - sh, pltpu.roll(x, n - sh, 0), 0.0)
        else:
            x = x + jnp.where(pos >= sh, pltpu.roll(x, sh, 0), 0.0)
        sh *= 2
    return x


def _hgrn_pair_table(tb):
    def body(o_ref):
        t = lax.broadcasted_iota(jnp.int32, (tb, tb), 0)
        s = lax.broadcasted_iota(jnp.int32, (tb, tb), 1)
        o_ref[...] = jnp.where((t // HG_CHUNK) == (s // HG_CHUNK), (t // HG_SUB) - (s // HG_SUB), 0).astype(F32)

    return _pcall(body, out_shape=_sds((tb, tb), F32), name="hgrn_pair_table")()


def _hgrn_cross_decays(b, pos):
    n = b.shape[0]
    pos8 = pos % HG_SUB
    end = b
    sh = 1
    while sh < HG_SUB:
        end = jnp.where((pos8 & sh) == 0, pltpu.roll(end, n - sh, 0), end)
        sh *= 2
    x = jnp.where(pos >= HG_SUB, jnp.exp(jnp.minimum(b - pltpu.roll(end, HG_SUB, 0), 0.0)), 0.0)
    ys = []
    for j in range(1, HG_CHUNK // HG_SUB):
        end_j = end if j == 1 else pltpu.roll(end, n - HG_SUB * (j - 1), 0)
        ys.append(jnp.where(pos < HG_CHUNK - HG_SUB * j, jnp.exp(jnp.minimum(end_j - b, 0.0)), 0.0))
    return x, ys


def _hgrn_cross_scores(qt, kts, pair):
    n = qt.shape[0]
    kcat = jnp.concatenate([k.astype(MXU_DTYPE) for k in kts], axis=0)
    p = lax.dot_general(qt.astype(MXU_DTYPE), kcat, (((1,), (1,)), ((), ())), preferred_element_type=F32)
    a = jnp.zeros((n, n), F32)
    for j in range(1, len(kts) + 1):
        a = jnp.where(pair == j, p[:, (j - 1) * n:j * n], a)
    return a, kcat


def _hgrn_gates(qh, fh, lb):
    sq, _ = _sigmoid_pair(qh)
    sg, sgn = _sigmoid_pair(fh)
    f = lb + (1.0 - lb) * sg
    g = jnp.log(jnp.maximum(f, F_MIN))
    kk = (1.0 - lb) * sgn
    return qh * sq, sq, sg, sgn, f, g, kk


def _hgrn_fwd(z, lb, pair, *, B, S, name):
    T = B * S
    tb = _tile(S, 256, HG_CHUNK)
    nc = tb // HG_CHUNK
    nb = S // tb
    dn_t = (((1,), (1,)), ((), ()))
    dn_o = (((0,), (0,)), ((), ()))

    def zcol(first):
        return lambda h, b, i: (b * nb + i, first + h)

    def body(q_ref, f_ref, v_ref, lb_ref, pair_ref, o_ref, st_ref, state):
        i = pl.program_id(2)

        @pl.when(i == 0)
        def _():
            state[...] = jnp.zeros_like(state)

        q, _, _, _, _, g, kk = _hgrn_gates(q_ref[...], f_ref[...], lb_ref[...])
        v = v_ref[...]
        pos = lax.broadcasted_iota(jnp.int32, (tb, LANES), 0) % HG_CHUNK
        pos8 = pos % HG_SUB
        b = _chunk_scan(g, pos, False)
        o = jnp.sum(q * kk, axis=1, keepdims=True) * v
        for d in range(1, HG_SUB):
            qd = pltpu.roll(q, tb - d, 0)
            bd = pltpu.roll(b, tb - d, 0)
            e = jnp.where(pos8 < HG_SUB - d, jnp.exp(jnp.minimum(bd - b, 0.0)), 0.0)
            a = jnp.sum(qd * kk * e, axis=1, keepdims=True)
            o = o + pltpu.roll(a * v, d, 0)
        x, ys = _hgrn_cross_decays(b, pos)
        a_cross, _ = _hgrn_cross_scores(q * x, [kk * y for y in ys], pair_ref[...])
        o = o + jnp.dot(a_cross.astype(MXU_DTYPE), v.astype(MXU_DTYPE), preferred_element_type=F32)
        qe = q * jnp.exp(b)
        chunks = [slice(c * HG_CHUNK, (c + 1) * HG_CHUNK) for c in range(nc)]
        b_last = [jnp.sum(jnp.where(pos[rs] == HG_CHUNK - 1, b[rs], 0.0), axis=0, keepdims=True) for rs in chunks]
        own = [lax.dot_general(v[rs].astype(MXU_DTYPE), (kk[rs] * jnp.exp(bl - b[rs])).astype(MXU_DTYPE), dn_o,
                               preferred_element_type=F32) for rs, bl in zip(chunks, b_last)]
        st = state[...]
        before = []
        for c in range(nc):
            before.append(st)
            st = st * jnp.exp(b_last[c]) + own[c]
        state[...] = st
        for c, rs in enumerate(chunks):
            st_ref[c * LANES:(c + 1) * LANES, :] = before[c]
            o_ref[rs, :] = o[rs] + lax.dot_general(qe[rs].astype(MXU_DTYPE), before[c].astype(MXU_DTYPE), dn_t,
                                                   preferred_element_type=F32)

    blk = (tb, LANES)
    return _run(
        body, (z, z, z, lb, pair), out_shape=(_sds((T, HG_W), F32), _sds((T // HG_CHUNK * LANES, HG_W), F32)),
        grid=(HG_HEADS, B, nb),
        in_specs=[pl.BlockSpec(blk, zcol(12)), pl.BlockSpec(blk, zcol(16)), pl.BlockSpec(blk, zcol(20)),
                  pl.BlockSpec((1, LANES), lambda h, b, i: (0, h)), pl.BlockSpec((tb, tb), lambda h, b, i: (0, 0))],
        out_specs=(pl.BlockSpec(blk, lambda h, b, i: (b * nb + i, h)),
                   pl.BlockSpec((nc * LANES, LANES), lambda h, b, i: (b * nb + i, h))),
        scratch_shapes=[pltpu.VMEM((LANES, LANES), F32)], name=name)


def _hgrn_bwd(z, lb, pair, states, do, *, B, S, name):
    T = B * S
    tb = _tile(S, 256, HG_CHUNK)
    nc = tb // HG_CHUNK
    nb = S // tb
    dn_t = (((1,), (1,)), ((), ()))
    dn_o = (((0,), (0,)), ((), ()))

    def body(q_ref, f_ref, v_ref, lb_ref, pair_ref, st_ref, do_ref, dq_ref, df_ref, di_ref, dlb_ref, dstate):
        bi = pl.program_id(1)
        i = pl.program_id(2)

        @pl.when(i == 0)
        def _():
            dstate[...] = jnp.zeros_like(dstate)

        qh = q_ref[...]
        lbv = lb_ref[...]
        q, sq, sg, sgn, f, g, kk = _hgrn_gates(qh, f_ref[...], lbv)
        v = v_ref[...]
        dov = do_ref[...]
        pos = lax.broadcasted_iota(jnp.int32, (tb, LANES), 0) % HG_CHUNK
        pos8 = pos % HG_SUB
        b = _chunk_scan(g, pos, False)

        a0 = jnp.sum(q * kk, axis=1, keepdims=True)
        da0 = jnp.sum(dov * v, axis=1, keepdims=True)
        dv = a0 * dov
        dkk = da0 * q
        dq = da0 * kk
        db = jnp.zeros((tb, LANES), F32)
        for d in range(1, HG_SUB):
            qd = pltpu.roll(q, tb - d, 0)
            bd = pltpu.roll(b, tb - d, 0)
            dod = pltpu.roll(dov, tb - d, 0)
            e = jnp.where(pos8 < HG_SUB - d, jnp.exp(jnp.minimum(bd - b, 0.0)), 0.0)
            a = jnp.sum(qd * kk * e, axis=1, keepdims=True)
            da = jnp.sum(dod * v, axis=1, keepdims=True)
            dv = dv + a * dod
            xe = da * e
            dkk = dkk + xe * qd
            t1 = xe * kk
            wt = t1 * qd
            dq = dq + pltpu.roll(t1, d, 0)
            db = db + pltpu.roll(wt, d, 0) - wt

        n_cross = HG_CHUNK // HG_SUB - 1
        pair = pair_ref[...]
        x, ys = _hgrn_cross_decays(b, pos)
        qt = q * x
        kts = [kk * y for y in ys]
        a_cross, kcat = _hgrn_cross_scores(qt, kts, pair)
        dom = dov.astype(MXU_DTYPE)
        da_all = lax.dot_general(dom, v.astype(MXU_DTYPE), dn_t, preferred_element_type=F32)
        dv = dv + lax.dot_general(a_cross.astype(MXU_DTYPE), dom, dn_o, preferred_element_type=F32)
        gcat = jnp.concatenate([jnp.where(pair == j, da_all, 0.0).astype(MXU_DTYPE) for j in range(1, n_cross + 1)],
                               axis=1)
        dqt = jnp.dot(gcat, kcat, preferred_element_type=F32)
        dkcat = lax.dot_general(gcat, qt.astype(MXU_DTYPE), dn_o, preferred_element_type=F32)
        dq = dq + dqt * x
        u = dqt * qt
        db = db + u
        d_end = -pltpu.roll(u, tb - HG_SUB, 0)
        for j in range(1, n_cross + 1):
            dkt = dkcat[(j - 1) * tb:j * tb]
            dkk = dkk + dkt * ys[j - 1]
            w = dkt * kts[j - 1]
            db = db - w
            d_end = d_end + (w if j == 1 else pltpu.roll(w, HG_SUB * (j - 1), 0))
        sh = 1
        while sh < HG_SUB:
            d_end = d_end + jnp.where(pos8 >= sh, pltpu.roll(d_end, sh, 0), 0.0)
            sh *= 2
        db = db + jnp.where(pos8 == HG_SUB - 1, d_end, 0.0)

        eb = jnp.exp(b)
        qe = q * eb
        chunks = [slice(c * HG_CHUNK, (c + 1) * HG_CHUNK) for c in range(nc)]
        b_lasts = [jnp.sum(jnp.where(pos[rs] == HG_CHUNK - 1, b[rs], 0.0), axis=0, keepdims=True) for rs in chunks]
        own = [lax.dot_general(dom[rs], qe[rs].astype(MXU_DTYPE), dn_o, preferred_element_type=F32) for rs in chunks]
        dst = dstate[...]
        after = [None] * nc
        for c in range(nc - 1, -1, -1):
            after[c] = dst
            dst = dst * jnp.exp(b_lasts[c]) + own[c]
        dstate[...] = dst
        dq_c, dkk_c, dv_c, db_c = [None] * nc, [None] * nc, [None] * nc, [None] * nc
        for c, rs in enumerate(chunks):
            st0 = st_ref[c * LANES:(c + 1) * LANES, :]
            eb_last = jnp.exp(b_lasts[c])
            er = jnp.exp(b_lasts[c] - b[rs])
            ke = kk[rs] * er
            dstm = after[c].astype(MXU_DTYPE)
            dqe = jnp.dot(dom[rs], st0.astype(MXU_DTYPE), preferred_element_type=F32)
            dv_c[c] = dv[rs] + lax.dot_general(ke.astype(MXU_DTYPE), dstm, dn_t, preferred_element_type=F32)
            dke = jnp.dot(v[rs].astype(MXU_DTYPE), dstm, preferred_element_type=F32)
            d_eb_last = jnp.sum(after[c] * st0, axis=0, keepdims=True)
            dq_c[c] = dq[rs] + dqe * eb[rs]
            dkk_c[c] = dkk[rs] + dke * er
            dkeke = dke * ke
            db_last = jnp.sum(dkeke, axis=0, keepdims=True) + d_eb_last * eb_last
            db_c[c] = db[rs] + dqe * qe[rs] - dkeke + jnp.where(pos[rs] == HG_CHUNK - 1, db_last, 0.0)
        dq = jnp.concatenate(dq_c, axis=0)
        dkk = jnp.concatenate(dkk_c, axis=0)
        dv = jnp.concatenate(dv_c, axis=0)
        dg = _chunk_scan(jnp.concatenate(db_c, axis=0), pos, True)

        dfv = dg * jnp.where(f > F_MIN, 1.0 / f, 0.0)
        dsg = (dfv - dkk) * (1.0 - lbv)
        dq_ref[...] = (dq * (sq * (1.0 + qh * (1.0 - sq)))).astype(dq_ref.dtype)
        df_ref[...] = (dsg * sg * sgn).astype(df_ref.dtype)
        di_ref[...] = dv.astype(di_ref.dtype)
        part = jnp.sum(dfv * sgn - dkk * sgn, axis=0, keepdims=True)

        @pl.when((bi == 0) & (i == 0))
        def _():
            dlb_ref[...] = part

        @pl.when((bi > 0) | (i > 0))
        def _():
            dlb_ref[...] += part

    def rev(first):
        return lambda h, b, i: (b * nb + (nb - 1 - i), first + h)

    blk = (tb, LANES)
    oblk = pl.BlockSpec(blk, rev(0))
    return _run(
        body, (z, z, z, lb, pair, states, do), out_shape=(_sds((T, HG_W), MXU_DTYPE),) * 3 + (_sds((1, HG_W), F32),),
        grid=(HG_HEADS, B, nb),
        in_specs=[pl.BlockSpec(blk, rev(12)), pl.BlockSpec(blk, rev(16)), pl.BlockSpec(blk, rev(20)),
                  pl.BlockSpec((1, LANES), lambda h, b, i: (0, h)), pl.BlockSpec((tb, tb), lambda h, b, i: (0, 0)),
                  pl.BlockSpec((nc * LANES, LANES), rev(0)), oblk],
        out_specs=(oblk, oblk, oblk, pl.BlockSpec((1, LANES), lambda h, b, i: (0, h))),
        scratch_shapes=[pltpu.VMEM((LANES, LANES), F32)], name=name)


def _hgout_fwd(o, z, g, ycat, *, name):
    T = o.shape[0]
    tm = _tile(T, 512, 8)

    def body(o_ref, gh_ref, g_ref, _, y_ref):
        ov = o_ref[...]
        gh = gh_ref[...]
        r = lax.rsqrt(_group_mean(ov * ov, HG_HEAD_DIM) + EPS)
        sg, _ = _sigmoid_pair(gh)
        y_ref[...] = (ov * r * g_ref[...] * (gh * sg)).astype(y_ref.dtype)

    blk = pl.BlockSpec((tm, HG_W), lambda i: (i, 0))
    return _run(body, (o, z, g, ycat), out_shape=_sds(ycat.shape, ycat.dtype), grid=(T // tm,),
                in_specs=[blk, pl.BlockSpec((tm, HG_W), lambda i: (i, 6)), pl.BlockSpec((1, HG_W), lambda i: (0, 0)),
                          pl.BlockSpec(memory_space=pl.ANY)],
                out_specs=pl.BlockSpec((tm, HG_W), lambda i: (i, ATT_W // HG_W)), name=name, aliases={3: 0})


def _hgout_bwd(o, z, g, dycat, *, name):
    T = o.shape[0]
    tm = _tile(T, 512, 8)

    def body(o_ref, gh_ref, g_ref, dy_ref, do_ref, dgh_ref, dg_ref):
        i = pl.program_id(0)
        ov = o_ref[...]
        gh = gh_ref[...]
        gv = g_ref[...]
        dy = dy_ref[...]
        r = lax.rsqrt(_group_mean(ov * ov, HG_HEAD_DIM) + EPS)
        xh = ov * r
        sg, _ = _sigmoid_pair(gh)
        dn = dy * (gh * sg)
        dgh_ref[...] = (dy * xh * gv * (sg * (1.0 + gh * (1.0 - sg)))).astype(dgh_ref.dtype)
        dng = dn * gv
        do_ref[...] = r * (dng - xh * _group_mean(dng * xh, HG_HEAD_DIM))
        part = jnp.sum(dn * xh, axis=0, keepdims=True)

        @pl.when(i == 0)
        def _():
            dg_ref[...] = part

        @pl.when(i > 0)
        def _():
            dg_ref[...] += part

    blk = pl.BlockSpec((tm, HG_W), lambda i: (i, 0))
    row = pl.BlockSpec((1, HG_W), lambda i: (0, 0))
    return _run(body, (o, z, g, dycat),
                out_shape=(_sds((T, HG_W), F32), _sds((T, HG_W), MXU_DTYPE), _sds((1, HG_W), F32)), grid=(T // tm,),
                in_specs=[blk, pl.BlockSpec((tm, HG_W), lambda i: (i, 6)), row,
                          pl.BlockSpec((tm, HG_W), lambda i: (i, 1))],
                out_specs=(blk, blk, row), name=name)


def _lower_bounds_fwd(hg_lb):
    L = hg_lb.shape[0]

    def body(x_ref, o_ref):
        rows = [x_ref[l:l + 1, :] for l in range(L)]
        mx = functools.reduce(jnp.maximum, rows)
        es = [jnp.exp(r - mx) for r in rows]
        tot = functools.reduce(jnp.add, es)
        acc = jnp.zeros_like(tot)
        for l in range(L):
            acc = acc + es[l] / tot
            o_ref[l:l + 1, :] = acc - es[0] / tot

    return _pcall(body, out_shape=_sds(hg_lb.shape, F32), name="lower_bounds_fwd")(hg_lb)


def _lower_bounds_bwd(hg_lb, dlower):
    L = hg_lb.shape[0]

    def body(x_ref, d_ref, o_ref):
        rows = [x_ref[l:l + 1, :] for l in range(L)]
        ds = [d_ref[l:l + 1, :] for l in range(L)]
        mx = functools.reduce(jnp.maximum, rows)
        es = [jnp.exp(r - mx) for r in rows]
        tot = functools.reduce(jnp.add, es)
        ps = [e / tot for e in es]
        dps = []
        for j in range(L):
            t = functools.reduce(jnp.add, ds[j:])
            if j == 0:
                t = t - functools.reduce(jnp.add, ds)
            dps.append(t)
        inner = functools.reduce(jnp.add, [p * dp for p, dp in zip(ps, dps)])
        for j in range(L):
            o_ref[j:j + 1, :] = ps[j] * (dps[j] - inner)

    return _pcall(body, out_shape=_sds(hg_lb.shape, F32), name="lower_bounds_bwd")(hg_lb, dlower)


def _memattn_fwd(qm, km, kv, *, B, S, NM, name):
    T, D = qm.shape
    E = D // MEM_HEADS
    tq = _tile(S, 512, 8)
    nq = S // tq
    scale = 1.0 / math.sqrt(E)

    def body(q_ref, k_ref, v_ref, o_ref):
        for h in range(MEM_HEADS):
            cs = slice(h * E, (h + 1) * E)
            s = lax.dot_general(q_ref[:, cs].astype(MXU_DTYPE), k_ref[:, cs].astype(MXU_DTYPE),
                                (((1,), (1,)), ((), ())), preferred_element_type=F32) * scale
            e = jnp.exp(s - jnp.max(s, axis=1, keepdims=True))
            pr = e / jnp.sum(e, axis=1, keepdims=True)
            o_ref[:, cs] = jnp.dot(pr.astype(MXU_DTYPE), v_ref[:, cs].astype(MXU_DTYPE),
                                   preferred_element_type=F32).astype(o_ref.dtype)

    return _run(
        body, (qm, km, kv), out_shape=_sds((T, D), MXU_DTYPE), grid=(B, nq),
        in_specs=[pl.BlockSpec((tq, D), lambda b, i: (b * nq + i, 0)), pl.BlockSpec((NM, D), lambda b, i: (b, 0)),
                  pl.BlockSpec((NM, D), lambda b, i: (b, 1))],
        out_specs=pl.BlockSpec((tq, D), lambda b, i: (b * nq + i, 0)), name=name)


def _memattn_bwd(qm, km, kv, do, *, B, S, NM, name):
    T, D = qm.shape
    E = D // MEM_HEADS
    tq = _tile(S, 512, 8)
    nq = S // tq
    scale = 1.0 / math.sqrt(E)

    def body(q_ref, k_ref, v_ref, do_ref, dq_ref, dk_ref, dv_ref):
        i = pl.program_id(1)

        @pl.when(i == 0)
        def _():
            dk_ref[...] = jnp.zeros_like(dk_ref)
            dv_ref[...] = jnp.zeros_like(dv_ref)

        for h in range(MEM_HEADS):
            cs = slice(h * E, (h + 1) * E)
            qh = q_ref[:, cs].astype(MXU_DTYPE)
            kh = k_ref[:, cs].astype(MXU_DTYPE)
            vh = v_ref[:, cs].astype(MXU_DTYPE)
            doh = do_ref[:, cs].astype(MXU_DTYPE)
            s = lax.dot_general(qh, kh, (((1,), (1,)), ((), ())), preferred_element_type=F32) * scale
            e = jnp.exp(s - jnp.max(s, axis=1, keepdims=True))
            pr = e / jnp.sum(e, axis=1, keepdims=True)
            dp = lax.dot_general(doh, vh, (((1,), (1,)), ((), ())), preferred_element_type=F32)
            ds = (pr * (dp - jnp.sum(dp * pr, axis=1, keepdims=True))).astype(MXU_DTYPE)
            dq_ref[:, cs] = jnp.dot(ds, kh, preferred_element_type=F32) * scale
            dk_ref[:, cs] += lax.dot_general(ds, qh, (((0,), (0,)), ((), ())), preferred_element_type=F32) * scale
            dv_ref[:, cs] += lax.dot_general(pr.astype(MXU_DTYPE), doh, (((0,), (0,)), ((), ())),
                                             preferred_element_type=F32)

    qblk = pl.BlockSpec((tq, D), lambda b, i: (b * nq + i, 0))
    mblk = pl.BlockSpec((NM, D), lambda b, i: (b, 0))
    return _run(
        body, (qm, km, kv, do), out_shape=(_sds((T, D), F32), _sds((B * NM, D), F32), _sds((B * NM, D), F32)),
        grid=(B, nq), in_specs=[qblk, mblk, pl.BlockSpec((NM, D), lambda b, i: (b, 1)), qblk],
        out_specs=(qblk, mblk, mblk), name=name)


def _loss_head(y, tgt):
    T, D = y.shape
    tm = _tile(T, 512, 8)

    def body(y_ref, t_ref, dy_ref, dyl_ref, l_ref):
        i = pl.program_id(0)
        err = y_ref[...] - t_ref[...]
        dy_ref[...] = err * (1.0 / D)
        dyl_ref[...] = (err * (1.0 / D)).astype(MXU_DTYPE)
        part = jnp.sum(jnp.sum(err * err, axis=1, keepdims=True), axis=0, keepdims=True) * (0.5 / D)

        @pl.when(i == 0)
        def _():
            l_ref[...] = part

        @pl.when(i > 0)
        def _():
            l_ref[...] += part

    blk = pl.BlockSpec((tm, D), lambda i: (i, 0))
    return _run(body, (y, tgt), out_shape=(_sds((T, D), F32), _sds((T, D), MXU_DTYPE), _sds((1, 1), F32)),
                grid=(T // tm,), in_specs=[blk, blk], out_specs=(blk, blk, pl.BlockSpec((1, 1), lambda i: (0, 0))),
                name="loss_head")


def _adamw(w, g, m, v):
    m = ADAM_B1 * m + (1.0 - ADAM_B1) * g
    v = ADAM_B2 * v + (1.0 - ADAM_B2) * jnp.square(g)
    m_hat = m / (1.0 - ADAM_B1 ** ADAM_STEP)
    v_hat = v / (1.0 - ADAM_B2 ** ADAM_STEP)
    delta = -ADAM_LR * (m_hat / (jnp.sqrt(v_hat) + ADAM_EPS) + ADAM_WD * w)
    return delta, m, v


def _sum_adamw(recv, w, m, v, layer, so_far=None):
    L, r, c = w.shape
    tr = _tile(r, 256, 8)

    def body(r_ref, w_ref, m_ref, v_ref, *rest):
        g_ref, d_ref, nm_ref, nv_ref = rest[-4:]
        g = r_ref[0].astype(F32)
        for j in range(1, N_DEV):
            g = g + r_ref[j].astype(F32)
        g_ref[...] = g
        d_ref[...], nm_ref[...], nv_ref[...] = _adamw(w_ref[...], g, m_ref[...], v_ref[...])

    blk = pl.BlockSpec((None, tr, c), lambda i: (layer, i, 0))
    hbm = pl.BlockSpec(memory_space=pl.ANY)
    args, in_specs, aliases = (recv, w, m, v), [pl.BlockSpec((N_DEV, tr, c), lambda i: (0, i, 0)), blk, blk, blk], {}
    if so_far is not None:
        args, in_specs, aliases = args + tuple(so_far), in_specs + [hbm] * 4, {4 + k: k for k in range(4)}
    return _run(body, args, out_shape=(_sds((L, r, c), F32),) * 4, grid=(r // tr,), in_specs=in_specs,
                out_specs=(blk,) * 4, name="grad_sum_adamw", aliases=aliases)


def _small_sum_adamw(gp, w, m, v):
    NR, C = gp.shape
    MESH = pl.DeviceIdType.MESH

    def body(gp_ref, w_ref, m_ref, v_ref, g_ref, d_ref, nm_ref, nv_ref, gath, send_sems, recv_sems):
        x, y, c = _mesh_pos()
        me = 4 * x + 2 * y + c
        gath[me] = gp_ref[...]
        copies = []
        for k in range(1, N_DEV):
            peer = (_flip(x, k & 4), _flip(y, k & 2), _flip(c, k & 1))
            cp = pltpu.make_async_remote_copy(src_ref=gp_ref, dst_ref=gath.at[me], send_sem=send_sems.at[k - 1],
                                              recv_sem=recv_sems.at[k - 1], device_id=peer, device_id_type=MESH)
            cp.start()
            copies.append(cp)
        for cp in copies:
            cp.wait()
        g = gath[0]
        for j in range(1, N_DEV):
            g = g + gath[j]
        g_ref[...] = g
        d_ref[...], nm_ref[...], nv_ref[...] = _adamw(w_ref[...], g, m_ref[...], v_ref[...])

    vm = pl.BlockSpec(memory_space=pltpu.VMEM)
    return _pcall(body, out_shape=(_sds((NR, C), F32),) * 4, in_specs=[vm] * 4, out_specs=(vm,) * 4,
                  scratch_shapes=[pltpu.VMEM((N_DEV, NR, C), F32), pltpu.SemaphoreType.DMA((7,)),
                                  pltpu.SemaphoreType.DMA((7,))], name="small_sum_adamw")(gp, w, m, v)


def _gathered(gath, name):
    _, r, c = gath.shape
    return gath.transpose(1, 0, 2).reshape(r, N_DEV * c) if name in COL_SHARDED else gath.reshape(N_DEV * r, c)


def _scatter_blocks(grad, name):
    rows, cols = grad.shape
    if name in COL_SHARDED:
        return grad.reshape(rows, N_DEV, cols // N_DEV).transpose(1, 0, 2)
    return grad.reshape(N_DEV, rows // N_DEV, cols)


def _pack_small(d, prefix, loss=None):
    parts = []
    for n in SMALL:
        a = d[prefix + n].reshape(-1).astype(F32)
        pad = (-a.shape[0]) % LANES
        parts.append(jnp.pad(a, (0, pad)) if pad else a)
    tail = jnp.zeros((LANES,), F32)
    if loss is not None:
        tail = tail.at[0].set(loss)
    flat = jnp.concatenate(parts + [tail])
    pad = (-flat.shape[0]) % (8 * LANES)
    return jnp.pad(flat, (0, pad)).reshape(-1, LANES)


def _unpack_small(packed, shapes):
    flat = packed.reshape(-1)
    out, off = {}, 0
    for n in SMALL:
        size = math.prod(shapes[n])
        out[n] = flat[off:off + size].reshape(shapes[n])
        off += size + (-size) % LANES
    return out, flat[off]


def _row(a):
    return a.reshape(1, -1).astype(F32)


def _layer_fwd(x, memf, w_in, shards, next_in, sp, lb, tab, dims):
    B, S, NM = dims
    D = x.shape[1]
    W = {"w_in": w_in}
    h1 = _norm_fwd(x, sp["norm1_g"], E=D, W=D, out_dtype=MXU_DTYPE, name="norm1_fwd")
    z, gath = _matmul(h1, w_in, carry=("gather", [shards[n].astype(MXU_DTYPE) for n in GATHER_MIX]), name="in_proj")
    W.update({n: _gathered(t, n) for n, t in zip(GATHER_MIX, gath)})
    qn = _norm_fwd(z, sp["attn_qn_g"], E=ATT_HEAD_DIM, W=ATT_W, cb=0, out_dtype=MXU_DTYPE, name="qnorm_fwd")
    kn = _norm_fwd(z, sp["attn_kn_g"], E=ATT_HEAD_DIM, W=ATT_W, cb=1, out_dtype=MXU_DTYPE, name="knorm_fwd")
    (ya, lse, ycat), gath = _attn_fwd(tab["attn"], qn, kn, z, B=B, S=S, name="dilated_attn_fwd",
                                carry=("gather", [shards[n].astype(MXU_DTYPE) for n in GATHER_FF]))
    W.update({n: _gathered(t, n) for n, t in zip(GATHER_FF, gath)})
    o, states = _hgrn_fwd(z, lb, tab["hgrn"], B=B, S=S, name="hgrn_fwd")
    ycat = _hgout_fwd(o, z, sp["hg_onorm_g"], ycat, name="hgrn_out_fwd")
    x1 = _matmul(ycat, W["w_out"], epi="add", extra=x, name="out_proj")
    h2 = _norm_fwd(x1, sp["norm2_g"], E=D, W=D, out_dtype=MXU_DTYPE, name="norm2_fwd")
    qmp = _matmul(h2, W["w_mq"], name="mq_proj")
    qm = _norm_fwd(qmp, sp["mq_norm_g"], E=D // MEM_HEADS, W=D, out_dtype=MXU_DTYPE, name="mqnorm_fwd")
    mn = _norm_fwd(memf, sp["mem_norm_g"], E=D, W=D, out_dtype=MXU_DTYPE, name="memnorm_fwd")
    kv = _matmul(mn, W["w_mkv"], name="mkv_proj")
    km = _norm_fwd(kv, sp["mk_norm_g"], E=D // MEM_HEADS, W=D, cb=0, out_dtype=MXU_DTYPE, name="mknorm_fwd")
    om = _memattn_fwd(qm, km, kv, B=B, S=S, NM=NM, name="mem_attn_fwd")
    x2 = _matmul(om, W["w_mo"], epi="add", extra=x1, name="mo_proj")
    h3 = _norm_fwd(x2, sp["norm3_g"], E=D, W=D, out_dtype=MXU_DTYPE, name="norm3_fwd")
    if next_in is None:
        u, w_in_next = _matmul(h3, W["w_ff1"], epi="relu2", out_dtype=MXU_DTYPE, name="ff1"), None
    else:
        u, gath = _matmul(h3, W["w_ff1"], epi="relu2", out_dtype=MXU_DTYPE,
                          carry=("gather", [next_in.astype(MXU_DTYPE)]), name="ff1")
        w_in_next = _gathered(gath[0], "w_in")
    x3 = _matmul(u, W["w_ff2"], epi="add", extra=x2, name="ff2")
    saved = dict(x=x, h1=h1, z=z, qn=qn, kn=kn, ya=ya, lse=lse, o=o, states=states, ycat=ycat, x1=x1, h2=h2,
                 qmp=qmp, qm=qm, mn=mn, kv=kv, km=km, om=om, x2=x2, h3=h3, u=u)
    return x3, saved, W, w_in_next


def _layer_bwd(dx3, dx3_low, s, memf, W, sp, lb, tab, dims):
    B, S, NM = dims
    D = dx3.shape[1]
    E_M = D // MEM_HEADS
    gw, gs, recv = {}, {}, {}

    def blocks(names):
        return ("scatter", [_scatter_blocks(gw[n], n) for n in names])

    def received(names, got):
        recv.update(dict(zip(names, got)))

    da = _matmul(dx3_low, W["w_ff2"], tb=True, epi="relu2grad", extra=s["u"], out_dtype=MXU_DTYPE, name="ff2_dgrad")
    gw["w_ff2"] = _matmul(s["u"], dx3_low, ta=True, out_dtype=MXU_DTYPE, name="ff2_wgrad")
    dh3, got = _matmul(da, W["w_ff1"], tb=True, out_dtype=MXU_DTYPE, carry=blocks(SCATTER_A), name="ff1_dgrad")
    received(SCATTER_A, got)
    gw["w_ff1"] = _matmul(s["h3"], da, ta=True, out_dtype=MXU_DTYPE, name="ff1_wgrad")
    dx2, dx2_low, gs["norm3_g"] = _norm_bwd(s["x2"], sp["norm3_g"], dh3, E=D, W=D, res=dx3, low=True, name="norm3_bwd")
    dom = _matmul(dx2_low, W["w_mo"], tb=True, out_dtype=MXU_DTYPE, name="mo_dgrad")
    gw["w_mo"] = _matmul(s["om"], dx2_low, ta=True, out_dtype=MXU_DTYPE, name="mo_wgrad")
    dqm, dkm, dvm = _memattn_bwd(s["qm"], s["km"], s["kv"], dom, B=B, S=S, NM=NM, name="mem_attn_bwd")
    dqmp, gs["mq_norm_g"] = _norm_bwd(s["qmp"], sp["mq_norm_g"], dqm, E=E_M, W=D, fold=E_M, out_dtype=MXU_DTYPE,
                                      name="mqnorm_bwd")
    dkmp, gs["mk_norm_g"] = _norm_bwd(s["kv"], sp["mk_norm_g"], dkm, E=E_M, W=D, cb=0, fold=E_M,
                                      out_dtype=MXU_DTYPE, name="mknorm_bwd")
    dkv = jnp.concatenate([dkmp, dvm.astype(MXU_DTYPE)], axis=1)
    dh2 = _matmul(dqmp, W["w_mq"], tb=True, out_dtype=MXU_DTYPE, name="mq_dgrad")
    gw["w_mq"] = _matmul(s["h2"], dqmp, ta=True, out_dtype=MXU_DTYPE, name="mq_wgrad")
    dmn = _matmul(dkv, W["w_mkv"], tb=True, out_dtype=MXU_DTYPE, name="mkv_dgrad")
    gw["w_mkv"] = _matmul(s["mn"], dkv, ta=True, out_dtype=MXU_DTYPE, name="mkv_wgrad")
    _, gs["mem_norm_g"] = _norm_bwd(memf, sp["mem_norm_g"], dmn, E=D, W=D, name="memnorm_bwd")
    dx1, dx1_low, gs["norm2_g"] = _norm_bwd(s["x1"], sp["norm2_g"], dh2, E=D, W=D, res=dx2, low=True, name="norm2_bwd")
    dycat = _matmul(dx1_low, W["w_out"], tb=True, out_dtype=MXU_DTYPE, name="out_dgrad")
    gw["w_out"] = _matmul(s["ycat"], dx1_low, ta=True, out_dtype=MXU_DTYPE, name="out_wgrad")
    do_hg, dgh, gs["hg_onorm_g"] = _hgout_bwd(s["o"], s["z"], sp["hg_onorm_g"], dycat, name="hgrn_out_bwd")
    dqh, dfh, dih, dlb = _hgrn_bwd(s["z"], lb, tab["hgrn"], s["states"], do_hg, B=B, S=S, name="hgrn_bwd")
    (dqn, dkn, dv), got = _attn_bwd(tab["attn"], s["qn"], s["kn"], s["z"], s["ya"], s["lse"], dycat, B=B, S=S,
                                     carry=blocks(SCATTER_B), name="dilated_attn_bwd")
    received(SCATTER_B, got)
    dq, gs["attn_qn_g"] = _norm_bwd(s["z"], sp["attn_qn_g"], dqn, E=ATT_HEAD_DIM, W=ATT_W, cb=0, fold=ATT_HEAD_DIM,
                                    out_dtype=MXU_DTYPE, name="qnorm_bwd")
    dk, gs["attn_kn_g"] = _norm_bwd(s["z"], sp["attn_kn_g"], dkn, E=ATT_HEAD_DIM, W=ATT_W, cb=1, fold=ATT_HEAD_DIM,
                                    out_dtype=MXU_DTYPE, name="knorm_bwd")
    dz = jnp.concatenate([dq, dk] + [t.astype(MXU_DTYPE) for t in (dv, dqh, dfh, dih, dgh)], axis=1)
    gw["w_in"] = _matmul(s["h1"], dz, ta=True, out_dtype=MXU_DTYPE, name="in_wgrad")
    dh1, got = _matmul(dz, W["w_in"], tb=True, out_dtype=MXU_DTYPE, carry=blocks(SCATTER_C), name="in_dgrad")
    received(SCATTER_C, got)
    dx0, dx0_low, gs["norm1_g"] = _norm_bwd(s["x"], sp["norm1_g"], dh1, E=D, W=D, res=dx1, low=True, name="norm1_bwd")
    gs["attn_qn_g"] = gs["attn_qn_g"][:, :ATT_HEAD_DIM]
    gs["attn_kn_g"] = gs["attn_kn_g"][:, :ATT_HEAD_DIM]
    return dx0, dx0_low, recv, gs, dlb


def kernel(x, mem, norm1_g, w_in, attn_qn_g, attn_kn_g, hg_lb, hg_onorm_g, w_out, norm2_g, mem_norm_g, w_mq, w_mkv, mq_norm_g, mk_norm_g, w_mo, norm3_g, w_ff1, w_ff2, loss_target, m_norm1_g, m_w_in, m_attn_qn_g, m_attn_kn_g, m_hg_lb, m_hg_onorm_g, m_w_out, m_norm2_g, m_mem_norm_g, m_w_mq, m_w_mkv, m_mq_norm_g, m_mk_norm_g, m_w_mo, m_norm3_g, m_w_ff1, m_w_ff2, v_norm1_g, v_w_in, v_attn_qn_g, v_attn_kn_g, v_hg_lb, v_hg_onorm_g, v_w_out, v_norm2_g, v_mem_norm_g, v_w_mq, v_w_mkv, v_mq_norm_g, v_mk_norm_g, v_w_mo, v_norm3_g, v_w_ff1, v_w_ff2):
    given = dict(norm1_g=norm1_g, w_in=w_in, attn_qn_g=attn_qn_g, attn_kn_g=attn_kn_g, hg_lb=hg_lb, hg_onorm_g=hg_onorm_g, w_out=w_out, norm2_g=norm2_g, mem_norm_g=mem_norm_g, w_mq=w_mq, w_mkv=w_mkv, mq_norm_g=mq_norm_g, mk_norm_g=mk_norm_g, w_mo=w_mo, norm3_g=norm3_g, w_ff1=w_ff1, w_ff2=w_ff2, m_norm1_g=m_norm1_g, m_w_in=m_w_in, m_attn_qn_g=m_attn_qn_g, m_attn_kn_g=m_attn_kn_g, m_hg_lb=m_hg_lb, m_hg_onorm_g=m_hg_onorm_g, m_w_out=m_w_out, m_norm2_g=m_norm2_g, m_mem_norm_g=m_mem_norm_g, m_w_mq=m_w_mq, m_w_mkv=m_w_mkv, m_mq_norm_g=m_mq_norm_g, m_mk_norm_g=m_mk_norm_g, m_w_mo=m_w_mo, m_norm3_g=m_norm3_g, m_w_ff1=m_w_ff1, m_w_ff2=m_w_ff2, v_norm1_g=v_norm1_g, v_w_in=v_w_in, v_attn_qn_g=v_attn_qn_g, v_attn_kn_g=v_attn_kn_g, v_hg_lb=v_hg_lb, v_hg_onorm_g=v_hg_onorm_g, v_w_out=v_w_out, v_norm2_g=v_norm2_g, v_mem_norm_g=v_mem_norm_g, v_w_mq=v_w_mq, v_w_mkv=v_w_mkv, v_mq_norm_g=v_mq_norm_g, v_mk_norm_g=v_mk_norm_g, v_w_mo=v_w_mo, v_norm3_g=v_norm3_g, v_w_ff1=v_w_ff1, v_w_ff2=v_w_ff2)
    B, S, D = x.shape
    NM = mem.shape[1]
    L = w_in.shape[0]
    dims = (B, S, NM)
    small_shapes = {n: given[n].shape for n in SMALL}

    def shards(prefix, l):
        return {n: given[prefix + n][l] for n in BIG}

    lower = _lower_bounds_fwd(hg_lb)
    tab = {"attn": _attn_tables(S, _tile(S, ATT_Q_BLOCK, 8)), "hgrn": _hgrn_pair_table(_tile(S, 256, HG_CHUNK))}
    xf = x.reshape(B * S, D)
    memf = mem.reshape(B * NM, D)

    def small_params(l):
        sp = {"norm1_g": _row(norm1_g[l]), "norm2_g": _row(norm2_g[l]), "norm3_g": _row(norm3_g[l]),
              "mem_norm_g": _row(mem_norm_g[l]), "hg_onorm_g": _row(hg_onorm_g[l]),
              "attn_qn_g": _row(jnp.tile(attn_qn_g[l], ATT_HEADS)), "attn_kn_g": _row(jnp.tile(attn_kn_g[l], ATT_HEADS)),
              "mq_norm_g": _row(jnp.tile(mq_norm_g[l], MEM_HEADS)), "mk_norm_g": _row(jnp.tile(mk_norm_g[l], MEM_HEADS))}
        return sp, _row(lower[l])

    w_in_full = _gathered(_all_gather([given["w_in"][0].astype(MXU_DTYPE)])[0], "w_in")
    saved, weights = [], []
    h = xf
    for l in range(L):
        sp, lb = small_params(l)
        next_in = given["w_in"][l + 1] if l + 1 < L else None
        h, s, W, w_in_full = _layer_fwd(h, memf, w_in_full, shards("", l), next_in, sp, lb, tab, dims)
        saved.append(s)
        weights.append(W)

    dh, dh_low, loss_part = _loss_head(h, loss_target.reshape(B * S, D))

    recv_layers, gs_layers, dlb_layers = [None] * L, [None] * L, [None] * L
    for l in range(L - 1, -1, -1):
        sp, lb = small_params(l)
        dh, dh_low, recv_layers[l], gs_layers[l], dlb_layers[l] = _layer_bwd(dh, dh_low, saved[l], memf, weights[l],
                                                                              sp, lb, tab, dims)
    grad_x = dh.reshape(B, S, D)

    big_out = {}
    for n in BIG:
        outs = None
        for l in range(L):
            outs = _sum_adamw(recv_layers[l][n], given[n], given["m_" + n], given["v_" + n], l, outs)
        big_out[n] = outs

    gs = {n: jnp.stack([gs_layers[l][n].reshape(small_shapes[n][1:]) for l in range(L)]) for n in SMALL if n != "hg_lb"}
    gs["hg_lb"] = _lower_bounds_bwd(hg_lb, jnp.concatenate(dlb_layers, axis=0))
    packed = _small_sum_adamw(_pack_small(gs, "", loss_part[0, 0]), _pack_small(given, ""), _pack_small(given, "m_"),
                              _pack_small(given, "v_"))
    small_out, loss = [], None
    for t in packed:
        d, tail = _unpack_small(t, small_shapes)
        small_out.append(d)
        loss = tail if loss is None else loss

    outs = [loss, grad_x]
    for k in range(4):
        outs += [small_out[k][n] if n in small_shapes else big_out[n][k] for n in WEIGHTS]
    return tuple(outs)
```

```python
import functools
import math

import jax
import jax.numpy as jnp
from jax import lax
from jax.experimental import pallas as pl
from jax.experimental.pallas import tpu as pltpu

F32 = jnp.float32
MXU_DTYPE = jnp.bfloat16
VMEM_LIMIT = 48 * 1024 * 1024

N_DEV = 8
EPS = 1e-6
NEG = -1e30
F_MIN = 1e-12
ATT_HEADS = 8
ATT_HEAD_DIM = 64
ATT_W = ATT_HEADS * ATT_HEAD_DIM
DILATIONS = (1, 4, 16)
DIL_STEPS = 128
ATT_Q_BLOCK = 256
ATT_KEY_EXTENT = 512
HG_HEADS = 4
HG_HEAD_DIM = 128
HG_W = HG_HEADS * HG_HEAD_DIM
HG_CHUNK = 32
HG_SUB = 8
MEM_HEADS = 4
LANES = 128

ADAM_LR = 0.001
ADAM_B1 = 0.9
ADAM_B2 = 0.999
ADAM_EPS = 1e-08
ADAM_WD = 0.01
ADAM_STEP = 10

BIG = ("w_in", "w_out", "w_mq", "w_mkv", "w_mo", "w_ff1", "w_ff2")
COL_SHARDED = ("w_in", "w_mkv", "w_ff1")
SMALL = ("norm1_g", "attn_qn_g", "attn_kn_g", "hg_lb", "hg_onorm_g", "norm2_g", "mem_norm_g",
         "mq_norm_g", "mk_norm_g", "norm3_g")
WEIGHTS = ("norm1_g", "w_in", "attn_qn_g", "attn_kn_g", "hg_lb", "hg_onorm_g", "w_out", "norm2_g",
           "mem_norm_g", "w_mq", "w_mkv", "mq_norm_g", "mk_norm_g", "w_mo", "norm3_g", "w_ff1", "w_ff2")
GATHER_IN = ("w_in",)
GATHER_MIX = ("w_out", "w_mq", "w_mkv", "w_mo")
GATHER_FF = ("w_ff1", "w_ff2")
SCATTER_A = ("w_ff2",)
SCATTER_B = ("w_ff1", "w_mo", "w_mq", "w_mkv", "w_out")
SCATTER_C = ("w_in",)


def _pcall(body, **kw):
    return pl.pallas_call(body, **kw)


def _params():
    return pltpu.CompilerParams(vmem_limit_bytes=VMEM_LIMIT)


def _tile(n, pref, mult):
    t = (min(n, pref) // mult) * mult
    while t >= mult:
        if n % t == 0:
            return t
        t -= mult
    return n


def _sds(shape, dtype):
    return jax.ShapeDtypeStruct(shape, dtype)


def _mesh_pos():
    return lax.axis_index("x"), lax.axis_index("y"), lax.axis_index("c")


def _flip(v, bit):
    return 1 - v if bit else v


def _gather_hooks(x_refs, out_refs, send_sems, recv_sems, local_sems):
    MESH = pl.DeviceIdType.MESH
    n = len(x_refs)
    x, y, c = _mesh_pos()
    me, sibling = (x, y, c), (x, y, 1 - c)
    chips = [(1 - x, y), (x, 1 - y), (1 - x, 1 - y)]

    def rows(a, px, py, pc):
        return out_refs[a].at[4 * px + 2 * py + pc]

    def copy(a, k, block, to, own=False):
        return pltpu.make_async_remote_copy(
            src_ref=x_refs[a] if own else rows(a, *block), dst_ref=rows(a, *block), send_sem=send_sems.at[n * k + a],
            recv_sem=recv_sems.at[n * k + a], device_id=to, device_id_type=MESH)

    mine = [pltpu.make_async_copy(x_refs[a], rows(a, *me), local_sems.at[a]) for a in range(n)]
    first = [copy(a, 0, me, sibling, own=True) for a in range(n)]
    first += [copy(a, 1 + j, me, (*chip, c), own=True) for j, chip in enumerate(chips) for a in range(n)]
    passed = [[copy(a, 4 + j, (*chip, c), sibling) for a in range(n)] for j, chip in enumerate(chips)]

    def start():
        for cp in mine + first:
            cp.start()

    def forward():
        for j, chip in enumerate(chips):
            for a in range(n):
                copy(a, 1 + j, (*chip, c), me).wait_recv()
                passed[j][a].start()

    def finish():
        for a in range(n):
            copy(a, 0, sibling, me).wait_recv()
        for j, chip in enumerate(chips):
            for a in range(n):
                copy(a, 4 + j, (*chip, 1 - c), me).wait_recv()
        for cp in first + [cp for group in passed for cp in group]:
            cp.wait_send()
        for cp in mine:
            cp.wait()

    return start, forward, finish


def _scatter_hooks(s_refs, r_refs, send_sems, recv_sems, local_sems):
    MESH = pl.DeviceIdType.MESH
    n = len(s_refs)
    x, y, c = _mesh_pos()
    me = 4 * x + 2 * y + c
    mine = [pltpu.make_async_copy(s_refs[a].at[me], r_refs[a].at[me], local_sems.at[a]) for a in range(n)]
    copies = []
    for m in range(1, N_DEV):
        px, py, pc = _flip(x, m & 4), _flip(y, m & 2), _flip(c, m & 1)
        for a in range(n):
            copies.append(pltpu.make_async_remote_copy(
                src_ref=s_refs[a].at[4 * px + 2 * py + pc], dst_ref=r_refs[a].at[me],
                send_sem=send_sems.at[n * (m - 1) + a], recv_sem=recv_sems.at[n * (m - 1) + a],
                device_id=(px, py, pc), device_id_type=MESH))

    def start():
        for cp in mine + copies:
            cp.start()

    def finish():
        for cp in copies:
            cp.wait()
        for cp in mine:
            cp.wait()

    return start, None, finish


def _exchange_sems(n):
    return [pltpu.SemaphoreType.DMA(((N_DEV - 1) * n,)), pltpu.SemaphoreType.DMA(((N_DEV - 1) * n,)),
            pltpu.SemaphoreType.DMA((n,))]


def _exchange_out(kind, xs):
    return _sds((N_DEV,) + xs.shape, xs.dtype) if kind == "gather" else _sds(xs.shape, xs.dtype)


def _run(body, args, *, out_shape, grid, in_specs, out_specs, scratch_shapes=(), name, carry=None, aliases=None):
    single = not isinstance(out_shape, (tuple, list))
    outs = (out_shape,) if single else tuple(out_shape)
    ospecs = (out_specs,) if single else tuple(out_specs)
    aliases = dict(aliases or {})
    if carry is None:
        res = _pcall(body, out_shape=outs, grid=grid, in_specs=list(in_specs), out_specs=ospecs,
                     scratch_shapes=list(scratch_shapes), input_output_aliases=aliases, compiler_params=_params(),
                     name=name)(*args)
        return res[0] if single else tuple(res)
    kind, xs = carry
    n_in, n_out, n_scr, n_x = len(args), len(outs), len(scratch_shapes), len(xs)
    total = math.prod(grid)

    def wrapped(*refs):
        refs = list(refs)
        ins, x_refs = refs[:n_in], refs[n_in:n_in + n_x]
        o, xo_refs = refs[n_in + n_x:n_in + n_x + n_out], refs[n_in + n_x + n_out:n_in + 2 * n_x + n_out]
        scr = refs[n_in + 2 * n_x + n_out:n_in + 2 * n_x + n_out + n_scr]
        sems = refs[n_in + 2 * n_x + n_out + n_scr:]
        step = pl.program_id(0)
        for ax in range(1, len(grid)):
            step = step * grid[ax] + pl.program_id(ax)
        start, forward, finish = (_gather_hooks if kind == "gather" else _scatter_hooks)(x_refs, xo_refs, *sems)
        pl.when(step == 0)(start)
        body(*ins, *o, *scr)
        if forward is not None:
            pl.when(step == (3 * total) // 4)(forward)
        pl.when(step == total - 1)(finish)

    hbm = pl.BlockSpec(memory_space=pl.ANY)
    res = _pcall(wrapped, out_shape=outs + tuple(_exchange_out(kind, t) for t in xs), grid=grid,
                 in_specs=list(in_specs) + [hbm] * n_x, out_specs=ospecs + (hbm,) * n_x,
                 scratch_shapes=list(scratch_shapes) + _exchange_sems(n_x), input_output_aliases=aliases,
                 compiler_params=_params(), name=name)(*args, *xs)
    main = tuple(res[:n_out])
    return (main[0] if single else main), list(res[n_out:])


def _all_gather(xs):
    n = len(xs)

    def body(*refs):
        start, forward, finish = _gather_hooks(refs[:n], refs[n:2 * n], *refs[2 * n:])
        start()
        forward()
        finish()

    hbm = pl.BlockSpec(memory_space=pl.ANY)
    return _pcall(body, out_shape=tuple(_exchange_out("gather", t) for t in xs), in_specs=[hbm] * n,
                  out_specs=(hbm,) * n, scratch_shapes=_exchange_sems(n), name="weights_all_gather")(*xs)


def _matmul(a, b, *, ta=False, tb=False, epi=None, extra=None, out_dtype=F32, carry=None, name):
    M, K = (a.shape[1], a.shape[0]) if ta else a.shape
    N = b.shape[0] if tb else b.shape[1]
    tm = _tile(M, 1024, 8 if not ta else LANES)
    tn = _tile(N, 1024, LANES)
    tk = _tile(K, 1024 if ta else 2048, LANES if not ta else 8)
    nk = K // tk
    dims = (((0 if ta else 1,), (1 if tb else 0,)), ((), ()))

    def body(a_ref, b_ref, *rest):
        e_ref = rest[0] if extra is not None else None
        o_ref = rest[1] if extra is not None else rest[0]

        def finish(r):
            if epi == "add":
                r = r + e_ref[...]
            elif epi == "relu2":
                r = jnp.square(jnp.maximum(r, 0.0))
            elif epi == "relu2grad":
                r = r * (2.0 * jnp.sqrt(e_ref[...].astype(F32)))
            o_ref[...] = r.astype(out_dtype)

        part = lax.dot_general(a_ref[...].astype(MXU_DTYPE), b_ref[...].astype(MXU_DTYPE), dims,
                               preferred_element_type=F32)
        if nk == 1:
            finish(part)
        else:
            acc_ref = rest[-1]
            k = pl.program_id(2)

            @pl.when(k == 0)
            def _():
                acc_ref[...] = part

            @pl.when(k > 0)
            def _():
                acc_ref[...] += part

            @pl.when(k == nk - 1)
            def _():
                finish(acc_ref[...])

    a_spec = pl.BlockSpec((tk, tm), lambda i, j, k: (k, i)) if ta else pl.BlockSpec((tm, tk), lambda i, j, k: (i, k))
    b_spec = pl.BlockSpec((tn, tk), lambda i, j, k: (j, k)) if tb else pl.BlockSpec((tk, tn), lambda i, j, k: (k, j))
    o_spec = pl.BlockSpec((tm, tn), lambda i, j, k: (i, j))
    in_specs = [a_spec, b_spec] + ([o_spec] if extra is not None else [])
    args = (a, b) + ((extra,) if extra is not None else ())
    return _run(body, args, out_shape=_sds((M, N), out_dtype), grid=(M // tm, N // tn, nk), in_specs=in_specs,
                out_specs=o_spec, scratch_shapes=[pltpu.VMEM((tm, tn), F32)] if nk > 1 else [], name=name, carry=carry)


def _group_mean(v, E):
    rows, W = v.shape
    if E == W:
        return jnp.mean(v, axis=-1, keepdims=True)
    pieces = []
    if E % LANES == 0:
        for g0 in range(0, W, E):
            m = jnp.mean(v[:, g0:g0 + E], axis=-1, keepdims=True)
            pieces.append(jnp.broadcast_to(m, (rows, E)))
    else:
        lane = lax.broadcasted_iota(jnp.int32, (rows, LANES), 1)
        for c0 in range(0, W, LANES):
            vc = v[:, c0:c0 + LANES]
            acc = jnp.zeros((rows, LANES), F32)
            for s0 in range(0, LANES, E):
                msk = (lane >= s0) & (lane < s0 + E)
                m = jnp.sum(jnp.where(msk, vc, 0.0), axis=-1, keepdims=True) * (1.0 / E)
                acc = jnp.where(msk, m, acc)
            pieces.append(acc)
    return jnp.concatenate(pieces, axis=-1)


def _fold_groups(t, E):
    W = t.shape[1]
    step = max(E, LANES)
    acc = t[:, 0:step]
    for c0 in range(step, W, step):
        acc = acc + t[:, c0:c0 + step]
    sh = LANES // 2
    while sh >= E:
        acc = acc + pltpu.roll(acc, sh, 1)
        sh //= 2
    return acc


def _norm_fwd(x, g, *, E, W, cb=0, out_dtype, name):
    M = x.shape[0]
    tm = _tile(M, 512, 8)

    def body(x_ref, g_ref, o_ref):
        xv = x_ref[...]
        r = lax.rsqrt(_group_mean(xv * xv, E) + EPS)
        o_ref[...] = (xv * r * g_ref[...]).astype(out_dtype)

    return _run(body, (x, g), out_shape=_sds((M, W), out_dtype), grid=(M // tm,),
                in_specs=[pl.BlockSpec((tm, W), lambda i: (i, cb)), pl.BlockSpec((1, W), lambda i: (0, 0))],
                out_specs=pl.BlockSpec((tm, W), lambda i: (i, 0)), name=name)


def _norm_bwd(x, g, dy, *, E, W, cb=0, res=None, fold=None, out_dtype=F32, low=False, into=None, name):
    M = x.shape[0]
    tm = _tile(M, 512, 8)
    n = M // tm
    gw = W if fold is None else max(fold, LANES)

    def body(x_ref, g_ref, dy_ref, *rest):
        rest = list(rest)
        r_ref = rest.pop(0) if res is not None else None
        if into is not None:
            rest.pop(0)
        dx_ref = rest.pop(0)
        dxl_ref = rest.pop(0) if low else None
        dg_ref, acc_ref = rest
        i = pl.program_id(0)
        xv = x_ref[...]
        r = lax.rsqrt(_group_mean(xv * xv, E) + EPS)
        xh = xv * r
        dyv = dy_ref[...].astype(F32)
        dyg = dyv * g_ref[...]
        dx = r * (dyg - xh * _group_mean(dyg * xh, E))
        if res is not None:
            dx = dx + r_ref[...]
        dx_ref[...] = dx.astype(dx_ref.dtype)
        if low:
            dxl_ref[...] = dx.astype(MXU_DTYPE)
        part = jnp.sum(dyv * xh, axis=0, keepdims=True)

        @pl.when(i == 0)
        def _():
            acc_ref[...] = part

        @pl.when(i > 0)
        def _():
            acc_ref[...] += part

        @pl.when(i == n - 1)
        def _():
            t = acc_ref[...]
            dg_ref[...] = t if fold is None else _fold_groups(t, fold)

    blk = pl.BlockSpec((tm, W), lambda i: (i, 0))
    in_specs = [pl.BlockSpec((tm, W), lambda i: (i, cb)), pl.BlockSpec((1, W), lambda i: (0, 0)), blk]
    args = [x, g, dy]
    if res is not None:
        in_specs.append(blk)
        args.append(res)
    lows = ((_sds((M, W), MXU_DTYPE),), (blk,)) if low else ((), ())
    dx_out, dx_spec, aliases = _sds((M, W), out_dtype), blk, None
    if into is not None:
        aliases = {len(args): 0}
        args.append(into)
        in_specs.append(pl.BlockSpec(memory_space=pl.ANY))
        dx_out, dx_spec = _sds(into.shape, into.dtype), pl.BlockSpec((tm, W), lambda i: (i, cb))
    return _run(body, tuple(args), out_shape=(dx_out,) + lows[0] + (_sds((1, gw), F32),), grid=(n,),
                in_specs=in_specs, out_specs=(dx_spec,) + lows[1] + (pl.BlockSpec((1, gw), lambda i: (0, 0)),),
                scratch_shapes=[pltpu.VMEM((1, W), F32)], name=name, aliases=aliases)


def _attn_tables(S, tq):
    nq = S // tq

    def body(o_ref):
        i = pl.program_id(0)
        d = (i * tq + lax.broadcasted_iota(jnp.int32, (tq, S), 0)) - lax.broadcasted_iota(jnp.int32, (tq, S), 1)
        cnt = jnp.zeros((tq, S), jnp.int32)
        for dil in DILATIONS:
            hit = (d <= DIL_STEPS * dil) if dil == 1 else (((d & (dil - 1)) == 0) & (d <= DIL_STEPS * dil))
            cnt = cnt + hit.astype(jnp.int32)
        ok = (d >= 0) & (cnt > 0)
        logm = jnp.where(cnt == 3, math.log(3.0), jnp.where(cnt == 2, math.log(2.0), 0.0))
        o_ref[0] = jnp.where(ok, logm, NEG).astype(F32)

    return _run(body, (), out_shape=_sds((nq, tq, S), F32), grid=(nq,), in_specs=[],
                out_specs=pl.BlockSpec((1, tq, S), lambda i: (i, 0, 0)), name="dilated_attn_tables")


def _alibi_slope(h):
    return 2.0 ** (-8.0 / ATT_HEADS * (h + 1))


def _key_positions(Sk):
    kpos = lax.broadcasted_iota(jnp.int32, (Sk, LANES), 0)
    return (kpos >> 8).astype(F32).astype(MXU_DTYPE), (kpos & 255).astype(F32).astype(MXU_DTYPE)


def _score_operands(q2s, k2, pos_hi, pos_lo, lane, sub, h):
    own = (lane < ATT_HEAD_DIM) if sub == 0 else (lane >= ATT_HEAD_DIM)
    spare = ATT_HEAD_DIM if sub == 0 else 0
    slope = _alibi_slope(h)
    terms = jnp.where(lane == spare, slope * 256.0, jnp.where(lane == spare + 1, slope, 0.0)).astype(q2s.dtype)
    qa = jnp.where(own, q2s, terms)
    ka = jnp.where(lane == spare, pos_hi, jnp.where(lane == spare + 1, pos_lo, k2))
    return own, qa, ka


def _key_extents(S, tq):
    ext = min(ATT_KEY_EXTENT, S)
    return ext, ext // tq, S // ext


def _attn_fwd(tab, qn, kn, z, *, B, S, carry=None, name):
    T = B * S
    tq = tab.shape[1]
    nq = S // tq
    ext, per, n_ext = _key_extents(S, tq)
    scale = 1.0 / math.sqrt(ATT_HEAD_DIM)
    dn_t = (((1,), (1,)), ((), ()))

    def body(tab_ref, q_ref, k_ref, v_ref, y_ref, lse_ref, ycat_ref):
        i = pl.program_id(1)
        lane = lax.broadcasted_iota(jnp.int32, (1, LANES), 1)
        lo = lane < ATT_HEAD_DIM

        def visit(Sk):
            base = tab_ref[0, :, :Sk]
            pos_hi, pos_lo = _key_positions(Sk)
            lse_blk = jnp.zeros((tq, LANES), F32)
            for p in range(ATT_HEADS // 2):
                cs = slice(p * LANES, (p + 1) * LANES)
                q2s = q_ref[:, cs].astype(MXU_DTYPE) * scale
                k2 = k_ref[:Sk, cs].astype(MXU_DTYPE)
                v2 = v_ref[:Sk, cs].astype(MXU_DTYPE)
                outs = []
                for sub in range(2):
                    h = 2 * p + sub
                    _, qa, ka = _score_operands(q2s, k2, pos_hi, pos_lo, lane, sub, h)
                    s = lax.dot_general(qa, ka, dn_t, preferred_element_type=F32) + base
                    mx = jnp.max(s, axis=1, keepdims=True)
                    e = jnp.exp(s - mx)
                    l = jnp.sum(e, axis=1, keepdims=True)
                    outs.append(jnp.dot(e.astype(MXU_DTYPE), v2, preferred_element_type=F32) / l)
                    lse_blk = jnp.where(lane == h, mx + jnp.log(l), lse_blk)
                y = jnp.where(lo, outs[0], outs[1])
                y_ref[:, cs] = y
                ycat_ref[:, cs] = y.astype(MXU_DTYPE)
            lse_ref[...] = lse_blk

        for e in range(n_ext):
            pl.when(i // per == e)(functools.partial(visit, (e + 1) * ext))

    return _run(
        body, (tab, qn, kn, z), grid=(B, nq),
        out_shape=(_sds((T, ATT_W), F32), _sds((T, LANES), F32), _sds((T, ATT_W + HG_W), MXU_DTYPE)),
        in_specs=[pl.BlockSpec((1, tq, S), lambda b, i: (i, 0, 0)),
                  pl.BlockSpec((tq, ATT_W), lambda b, i: (b * nq + i, 0)),
                  pl.BlockSpec((S, ATT_W), lambda b, i: (b, 0)),
                  pl.BlockSpec((S, ATT_W), lambda b, i: (b, 2))],
        out_specs=(pl.BlockSpec((tq, ATT_W), lambda b, i: (b * nq + i, 0)),
                   pl.BlockSpec((tq, LANES), lambda b, i: (b * nq + i, 0)),
                   pl.BlockSpec((tq, ATT_W), lambda b, i: (b * nq + i, 0))),
        name=name, carry=carry)


def _attn_bwd(tab, qn, kn, z, y, lse, dycat, dz, *, B, S, carry=None, name):
    T = B * S
    tq = tab.shape[1]
    nq = S // tq
    ext, per, n_ext = _key_extents(S, tq)
    scale = 1.0 / math.sqrt(ATT_HEAD_DIM)
    dn_t = (((1,), (1,)), ((), ()))
    dn_o = (((0,), (0,)), ((), ()))

    def body(tab_ref, q_ref, k_ref, v_ref, y_ref, lse_ref, dy_ref, _, dq_ref, dk_ref, dz_ref, dv_ref):
        i = pl.program_id(1)

        @pl.when(i == 0)
        def _():
            dk_ref[...] = jnp.zeros_like(dk_ref)
            dv_ref[...] = jnp.zeros_like(dv_ref)

        lane = lax.broadcasted_iota(jnp.int32, (1, LANES), 1)
        lo = lane < ATT_HEAD_DIM

        def visit(Sk):
            base = tab_ref[0, :, :Sk]
            pos_hi, pos_lo = _key_positions(Sk)
            lse_blk = lse_ref[...]
            for p in range(ATT_HEADS // 2):
                cs = slice(p * LANES, (p + 1) * LANES)
                q2 = q_ref[:, cs].astype(MXU_DTYPE)
                q2s = q2 * scale
                k2 = k_ref[:Sk, cs].astype(MXU_DTYPE)
                v2 = v_ref[:Sk, cs].astype(MXU_DTYPE)
                do2 = dy_ref[:, cs]
                doy = do2 * y_ref[:, cs]
                do2m = do2.astype(MXU_DTYPE)
                dqs, dks, dvs = [], [], []
                for sub in range(2):
                    h = 2 * p + sub
                    own, qa, ka = _score_operands(q2s, k2, pos_hi, pos_lo, lane, sub, h)
                    s = lax.dot_general(qa, ka, dn_t, preferred_element_type=F32) + base
                    lse_h = jnp.sum(jnp.where(lane == h, lse_blk, 0.0), axis=1, keepdims=True)
                    pr = jnp.exp(s - lse_h)
                    dsum = jnp.sum(jnp.where(own, doy, 0.0), axis=1, keepdims=True)
                    dom = jnp.where(own, do2m, jnp.zeros_like(do2m))
                    dp = lax.dot_general(dom, v2, dn_t, preferred_element_type=F32)
                    ds = (pr * (dp - dsum)).astype(MXU_DTYPE)
                    dqs.append(jnp.dot(ds, k2, preferred_element_type=F32))
                    dks.append(lax.dot_general(ds, q2, dn_o, preferred_element_type=F32))
                    dvs.append(lax.dot_general(pr.astype(MXU_DTYPE), do2m, dn_o, preferred_element_type=F32))
                dq_ref[:, cs] = jnp.where(lo, dqs[0], dqs[1]) * scale
                dk_ref[:Sk, cs] += jnp.where(lo, dks[0], dks[1]) * scale
                dv_ref[:Sk, cs] += jnp.where(lo, dvs[0], dvs[1])

        for e in range(n_ext):
            pl.when(i // per == e)(functools.partial(visit, (e + 1) * ext))

        @pl.when(i == nq - 1)
        def _():
            dz_ref[...] = dv_ref[...].astype(dz_ref.dtype)

    qblk = pl.BlockSpec((tq, ATT_W), lambda b, i: (b * nq + i, 0))
    sblk = pl.BlockSpec((S, ATT_W), lambda b, i: (b, 0))
    vblk = pl.BlockSpec((S, ATT_W), lambda b, i: (b, 2))
    return _run(
        body, (tab, qn, kn, z, y, lse, dycat, dz), grid=(B, nq),
        out_shape=(_sds((T, ATT_W), F32), _sds((T, ATT_W), F32), _sds(dz.shape, dz.dtype)),
        in_specs=[pl.BlockSpec((1, tq, S), lambda b, i: (i, 0, 0)), qblk, sblk, vblk, qblk,
                  pl.BlockSpec((tq, LANES), lambda b, i: (b * nq + i, 0)), qblk, pl.BlockSpec(memory_space=pl.ANY)],
        out_specs=(qblk, sblk, vblk), scratch_shapes=[pltpu.VMEM((S, ATT_W), F32)], name=name, carry=carry,
        aliases={7: 2})


def _sigmoid_pair(x):
    en = jnp.exp(-jnp.abs(x))
    big = 1.0 / (1.0 + en)
    small = en * big
    pos = x >= 0
    return jnp.where(pos, big, small), jnp.where(pos, small, big)


def _chunk_scan(x, pos, reverse):
    n = x.shape[0]
    sh = 1
    while sh < HG_CHUNK:
        if reverse:
            x = x + jnp.where(pos < HG_CHUNK - sh, pltpu.roll(x, n - sh, 0), 0.0)
        else:
            x = x + jnp.where(pos >= sh, pltpu.roll(x, sh, 0), 0.0)
        sh *= 2
    return x


def _hgrn_pair_table(tb):
    def body(o_ref):
        t = lax.broadcasted_iota(jnp.int32, (tb, tb), 0)
        s = lax.broadcasted_iota(jnp.int32, (tb, tb), 1)
        o_ref[...] = jnp.where((t // HG_CHUNK) == (s // HG_CHUNK), (t // HG_SUB) - (s // HG_SUB), 0).astype(F32)

    return _pcall(body, out_shape=_sds((tb, tb), F32), name="hgrn_pair_table")()


def _hgrn_cross_decays(b, pos):
    n = b.shape[0]
    pos8 = pos % HG_SUB
    end = b
    sh = 1
    while sh < HG_SUB:
        end = jnp.where((pos8 & sh) == 0, pltpu.roll(end, n - sh, 0), end)
        sh *= 2
    x = jnp.where(pos >= HG_SUB, jnp.exp(jnp.minimum(b - pltpu.roll(end, HG_SUB, 0), 0.0)), 0.0)
    ys = []
    for j in range(1, HG_CHUNK // HG_SUB):
        end_j = end if j == 1 else pltpu.roll(end, n - HG_SUB * (j - 1), 0)
        ys.append(jnp.where(pos < HG_CHUNK - HG_SUB * j, jnp.exp(jnp.minimum(end_j - b, 0.0)), 0.0))
    return x, ys


def _hgrn_cross_scores(qt, kts, pair):
    n = qt.shape[0]
    kcat = jnp.concatenate([k.astype(MXU_DTYPE) for k in kts], axis=0)
    p = lax.dot_general(qt.astype(MXU_DTYPE), kcat, (((1,), (1,)), ((), ())), preferred_element_type=F32)
    a = jnp.zeros((n, n), F32)
    for j in range(1, len(kts) + 1):
        a = jnp.where(pair == j, p[:, (j - 1) * n:j * n], a)
    return a, kcat


def _hgrn_gates(qh, fh, lb):
    sq, _ = _sigmoid_pair(qh)
    sg, sgn = _sigmoid_pair(fh)
    f = lb + (1.0 - lb) * sg
    g = jnp.log(jnp.maximum(f, F_MIN))
    kk = (1.0 - lb) * sgn
    return qh * sq, sq, sg, sgn, f, g, kk


def _hgrn_fwd(z, lb, pair, *, B, S, name):
    T = B * S
    tb = _tile(S, 256, HG_CHUNK)
    nc = tb // HG_CHUNK
    nb = S // tb
    dn_t = (((1,), (1,)), ((), ()))
    dn_o = (((0,), (0,)), ((), ()))

    def zcol(first):
        return lambda h, b, i: (b * nb + i, first + h)

    def body(q_ref, f_ref, v_ref, lb_ref, pair_ref, o_ref, st_ref, state):
        i = pl.program_id(2)

        @pl.when(i == 0)
        def _():
            state[...] = jnp.zeros_like(state)

        q, _, _, _, _, g, kk = _hgrn_gates(q_ref[...], f_ref[...], lb_ref[...])
        v = v_ref[...]
        pos = lax.broadcasted_iota(jnp.int32, (tb, LANES), 0) % HG_CHUNK
        pos8 = pos % HG_SUB
        b = _chunk_scan(g, pos, False)
        o = jnp.sum(q * kk, axis=1, keepdims=True) * v
        for d in range(1, HG_SUB):
            qd = pltpu.roll(q, tb - d, 0)
            bd = pltpu.roll(b, tb - d, 0)
            e = jnp.where(pos8 < HG_SUB - d, jnp.exp(jnp.minimum(bd - b, 0.0)), 0.0)
            a = jnp.sum(qd * kk * e, axis=1, keepdims=True)
            o = o + pltpu.roll(a * v, d, 0)
        x, ys = _hgrn_cross_decays(b, pos)
        a_cross, _ = _hgrn_cross_scores(q * x, [kk * y for y in ys], pair_ref[...])
        o = o + jnp.dot(a_cross.astype(MXU_DTYPE), v.astype(MXU_DTYPE), preferred_element_type=F32)
        qe = q * jnp.exp(b)
        chunks = [slice(c * HG_CHUNK, (c + 1) * HG_CHUNK) for c in range(nc)]
        b_last = [jnp.sum(jnp.where(pos[rs] == HG_CHUNK - 1, b[rs], 0.0), axis=0, keepdims=True) for rs in chunks]
        own = [lax.dot_general(v[rs].astype(MXU_DTYPE), (kk[rs] * jnp.exp(bl - b[rs])).astype(MXU_DTYPE), dn_o,
                               preferred_element_type=F32) for rs, bl in zip(chunks, b_last)]
        st = state[...]
        before = []
        for c in range(nc):
            before.append(st)
            st = st * jnp.exp(b_last[c]) + own[c]
        state[...] = st
        for c, rs in enumerate(chunks):
            st_ref[c * LANES:(c + 1) * LANES, :] = before[c]
            o_ref[rs, :] = o[rs] + lax.dot_general(qe[rs].astype(MXU_DTYPE), before[c].astype(MXU_DTYPE), dn_t,
                                                   preferred_element_type=F32)

    blk = (tb, LANES)
    return _run(
        body, (z, z, z, lb, pair), out_shape=(_sds((T, HG_W), F32), _sds((T // HG_CHUNK * LANES, HG_W), F32)),
        grid=(HG_HEADS, B, nb),
        in_specs=[pl.BlockSpec(blk, zcol(12)), pl.BlockSpec(blk, zcol(16)), pl.BlockSpec(blk, zcol(20)),
                  pl.BlockSpec((1, LANES), lambda h, b, i: (0, h)), pl.BlockSpec((tb, tb), lambda h, b, i: (0, 0))],
        out_specs=(pl.BlockSpec(blk, lambda h, b, i: (b * nb + i, h)),
                   pl.BlockSpec((nc * LANES, LANES), lambda h, b, i: (b * nb + i, h))),
        scratch_shapes=[pltpu.VMEM((LANES, LANES), F32)], name=name)


def _hgrn_bwd(z, lb, pair, states, do, *, B, S, name):
    T = B * S
    tb = _tile(S, 256, HG_CHUNK)
    nc = tb // HG_CHUNK
    nb = S // tb
    dn_t = (((1,), (1,)), ((), ()))
    dn_o = (((0,), (0,)), ((), ()))

    def body(q_ref, f_ref, v_ref, lb_ref, pair_ref, st_ref, do_ref, dq_ref, df_ref, di_ref, dlb_ref, dstate):
        bi = pl.program_id(1)
        i = pl.program_id(2)

        @pl.when(i == 0)
        def _():
            dstate[...] = jnp.zeros_like(dstate)

        qh = q_ref[...]
        lbv = lb_ref[...]
        q, sq, sg, sgn, f, g, kk = _hgrn_gates(qh, f_ref[...], lbv)
        v = v_ref[...]
        dov = do_ref[...]
        pos = lax.broadcasted_iota(jnp.int32, (tb, LANES), 0) % HG_CHUNK
        pos8 = pos % HG_SUB
        b = _chunk_scan(g, pos, False)

        a0 = jnp.sum(q * kk, axis=1, keepdims=True)
        da0 = jnp.sum(dov * v, axis=1, keepdims=True)
        dv = a0 * dov
        dkk = da0 * q
        dq = da0 * kk
        db = jnp.zeros((tb, LANES), F32)
        for d in range(1, HG_SUB):
            qd = pltpu.roll(q, tb - d, 0)
            bd = pltpu.roll(b, tb - d, 0)
            dod = pltpu.roll(dov, tb - d, 0)
            e = jnp.where(pos8 < HG_SUB - d, jnp.exp(jnp.minimum(bd - b, 0.0)), 0.0)
            a = jnp.sum(qd * kk * e, axis=1, keepdims=True)
            da = jnp.sum(dod * v, axis=1, keepdims=True)
            dv = dv + a * dod
            xe = da * e
            dkk = dkk + xe * qd
            t1 = xe * kk
            wt = t1 * qd
            dq = dq + pltpu.roll(t1, d, 0)
            db = db + pltpu.roll(wt, d, 0) - wt

        n_cross = HG_CHUNK // HG_SUB - 1
        pair = pair_ref[...]
        x, ys = _hgrn_cross_decays(b, pos)
        qt = q * x
        kts = [kk * y for y in ys]
        a_cross, kcat = _hgrn_cross_scores(qt, kts, pair)
        dom = dov.astype(MXU_DTYPE)
        da_all = lax.dot_general(dom, v.astype(MXU_DTYPE), dn_t, preferred_element_type=F32)
        dv = dv + lax.dot_general(a_cross.astype(MXU_DTYPE), dom, dn_o, preferred_element_type=F32)
        gcat = jnp.concatenate([jnp.where(pair == j, da_all, 0.0).astype(MXU_DTYPE) for j in range(1, n_cross + 1)],
                               axis=1)
        dqt = jnp.dot(gcat, kcat, preferred_element_type=F32)
        dkcat = lax.dot_general(gcat, qt.astype(MXU_DTYPE), dn_o, preferred_element_type=F32)
        dq = dq + dqt * x
        u = dqt * qt
        db = db + u
        d_end = -pltpu.roll(u, tb - HG_SUB, 0)
        for j in range(1, n_cross + 1):
            dkt = dkcat[(j - 1) * tb:j * tb]
            dkk = dkk + dkt * ys[j - 1]
            w = dkt * kts[j - 1]
            db = db - w
            d_end = d_end + (w if j == 1 else pltpu.roll(w, HG_SUB * (j - 1), 0))
        sh = 1
        while sh < HG_SUB:
            d_end = d_end + jnp.where(pos8 >= sh, pltpu.roll(d_end, sh, 0), 0.0)
            sh *= 2
        db = db + jnp.where(pos8 == HG_SUB - 1, d_end, 0.0)

        eb = jnp.exp(b)
        qe = q * eb
        chunks = [slice(c * HG_CHUNK, (c + 1) * HG_CHUNK) for c in range(nc)]
        b_lasts = [jnp.sum(jnp.where(pos[rs] == HG_CHUNK - 1, b[rs], 0.0), axis=0, keepdims=True) for rs in chunks]
        own = [lax.dot_general(dom[rs], qe[rs].astype(MXU_DTYPE), dn_o, preferred_element_type=F32) for rs in chunks]
        dst = dstate[...]
        after = [None] * nc
        for c in range(nc - 1, -1, -1):
            after[c] = dst
            dst = dst * jnp.exp(b_lasts[c]) + own[c]
        dstate[...] = dst
        dq_c, dkk_c, dv_c, db_c = [None] * nc, [None] * nc, [None] * nc, [None] * nc
        for c, rs in enumerate(chunks):
            st0 = st_ref[c * LANES:(c + 1) * LANES, :]
            eb_last = jnp.exp(b_lasts[c])
            er = jnp.exp(b_lasts[c] - b[rs])
            ke = kk[rs] * er
            dstm = after[c].astype(MXU_DTYPE)
            dqe = jnp.dot(dom[rs], st0.astype(MXU_DTYPE), preferred_element_type=F32)
            dv_c[c] = dv[rs] + lax.dot_general(ke.astype(MXU_DTYPE), dstm, dn_t, preferred_element_type=F32)
            dke = jnp.dot(v[rs].astype(MXU_DTYPE), dstm, preferred_element_type=F32)
            d_eb_last = jnp.sum(after[c] * st0, axis=0, keepdims=True)
            dq_c[c] = dq[rs] + dqe * eb[rs]
            dkk_c[c] = dkk[rs] + dke * er
            dkeke = dke * ke
            db_last = jnp.sum(dkeke, axis=0, keepdims=True) + d_eb_last * eb_last
            db_c[c] = db[rs] + dqe * qe[rs] - dkeke + jnp.where(pos[rs] == HG_CHUNK - 1, db_last, 0.0)
        dq = jnp.concatenate(dq_c, axis=0)
        dkk = jnp.concatenate(dkk_c, axis=0)
        dv = jnp.concatenate(dv_c, axis=0)
        dg = _chunk_scan(jnp.concatenate(db_c, axis=0), pos, True)

        dfv = dg * jnp.where(f > F_MIN, 1.0 / f, 0.0)
        dsg = (dfv - dkk) * (1.0 - lbv)
        dq_ref[...] = (dq * (sq * (1.0 + qh * (1.0 - sq)))).astype(dq_ref.dtype)
        df_ref[...] = (dsg * sg * sgn).astype(df_ref.dtype)
        di_ref[...] = dv.astype(di_ref.dtype)
        part = jnp.sum(dfv * sgn - dkk * sgn, axis=0, keepdims=True)

        @pl.when((bi == 0) & (i == 0))
        def _():
            dlb_ref[...] = part

        @pl.when((bi > 0) | (i > 0))
        def _():
            dlb_ref[...] += part

    def rev(first):
        return lambda h, b, i: (b * nb + (nb - 1 - i), first + h)

    blk = (tb, LANES)
    oblk = pl.BlockSpec(blk, rev(0))
    return _run(
        body, (z, z, z, lb, pair, states, do), out_shape=(_sds((T, HG_W), MXU_DTYPE),) * 3 + (_sds((1, HG_W), F32),),
        grid=(HG_HEADS, B, nb),
        in_specs=[pl.BlockSpec(blk, rev(12)), pl.BlockSpec(blk, rev(16)), pl.BlockSpec(blk, rev(20)),
                  pl.BlockSpec((1, LANES), lambda h, b, i: (0, h)), pl.BlockSpec((tb, tb), lambda h, b, i: (0, 0)),
                  pl.BlockSpec((nc * LANES, LANES), rev(0)), oblk],
        out_specs=(oblk, oblk, oblk, pl.BlockSpec((1, LANES), lambda h, b, i: (0, h))),
        scratch_shapes=[pltpu.VMEM((LANES, LANES), F32)], name=name)


def _hgout_fwd(o, z, g, ycat, *, name):
    T = o.shape[0]
    tm = _tile(T, 512, 8)

    def body(o_ref, gh_ref, g_ref, _, y_ref):
        ov = o_ref[...]
        gh = gh_ref[...]
        r = lax.rsqrt(_group_mean(ov * ov, HG_HEAD_DIM) + EPS)
        sg, _ = _sigmoid_pair(gh)
        y_ref[...] = (ov * r * g_ref[...] * (gh * sg)).astype(y_ref.dtype)

    blk = pl.BlockSpec((tm, HG_W), lambda i: (i, 0))
    return _run(body, (o, z, g, ycat), out_shape=_sds(ycat.shape, ycat.dtype), grid=(T // tm,),
                in_specs=[blk, pl.BlockSpec((tm, HG_W), lambda i: (i, 6)), pl.BlockSpec((1, HG_W), lambda i: (0, 0)),
                          pl.BlockSpec(memory_space=pl.ANY)],
                out_specs=pl.BlockSpec((tm, HG_W), lambda i: (i, ATT_W // HG_W)), name=name, aliases={3: 0})


def _hgout_bwd(o, z, g, dycat, *, name):
    T = o.shape[0]
    tm = _tile(T, 512, 8)

    def body(o_ref, gh_ref, g_ref, dy_ref, do_ref, dgh_ref, dg_ref):
        i = pl.program_id(0)
        ov = o_ref[...]
        gh = gh_ref[...]
        gv = g_ref[...]
        dy = dy_ref[...]
        r = lax.rsqrt(_group_mean(ov * ov, HG_HEAD_DIM) + EPS)
        xh = ov * r
        sg, _ = _sigmoid_pair(gh)
        dn = dy * (gh * sg)
        dgh_ref[...] = (dy * xh * gv * (sg * (1.0 + gh * (1.0 - sg)))).astype(dgh_ref.dtype)
        dng = dn * gv
        do_ref[...] = r * (dng - xh * _group_mean(dng * xh, HG_HEAD_DIM))
        part = jnp.sum(dn * xh, axis=0, keepdims=True)

        @pl.when(i == 0)
        def _():
            dg_ref[...] = part

        @pl.when(i > 0)
        def _():
            dg_ref[...] += part

    blk = pl.BlockSpec((tm, HG_W), lambda i: (i, 0))
    row = pl.BlockSpec((1, HG_W), lambda i: (0, 0))
    return _run(body, (o, z, g, dycat),
                out_shape=(_sds((T, HG_W), F32), _sds(z.shape, MXU_DTYPE), _sds((1, HG_W), F32)), grid=(T // tm,),
                in_specs=[blk, pl.BlockSpec((tm, HG_W), lambda i: (i, 6)), row,
                          pl.BlockSpec((tm, HG_W), lambda i: (i, 1))],
                out_specs=(blk, pl.BlockSpec((tm, HG_W), lambda i: (i, 6)), row), name=name)


def _lower_bounds_fwd(hg_lb):
    L = hg_lb.shape[0]

    def body(x_ref, o_ref):
        rows = [x_ref[l:l + 1, :] for l in range(L)]
        mx = functools.reduce(jnp.maximum, rows)
        es = [jnp.exp(r - mx) for r in rows]
        tot = functools.reduce(jnp.add, es)
        acc = jnp.zeros_like(tot)
        for l in range(L):
            acc = acc + es[l] / tot
            o_ref[l:l + 1, :] = acc - es[0] / tot

    return _pcall(body, out_shape=_sds(hg_lb.shape, F32), name="lower_bounds_fwd")(hg_lb)


def _lower_bounds_bwd(hg_lb, dlower):
    L = hg_lb.shape[0]

    def body(x_ref, d_ref, o_ref):
        rows = [x_ref[l:l + 1, :] for l in range(L)]
        ds = [d_ref[l:l + 1, :] for l in range(L)]
        mx = functools.reduce(jnp.maximum, rows)
        es = [jnp.exp(r - mx) for r in rows]
        tot = functools.reduce(jnp.add, es)
        ps = [e / tot for e in es]
        dps = []
        for j in range(L):
            t = functools.reduce(jnp.add, ds[j:])
            if j == 0:
                t = t - functools.reduce(jnp.add, ds)
            dps.append(t)
        inner = functools.reduce(jnp.add, [p * dp for p, dp in zip(ps, dps)])
        for j in range(L):
            o_ref[j:j + 1, :] = ps[j] * (dps[j] - inner)

    return _pcall(body, out_shape=_sds(hg_lb.shape, F32), name="lower_bounds_bwd")(hg_lb, dlower)


def _memattn_fwd(qm, km, kv, *, B, S, NM, name):
    T, D = qm.shape
    E = D // MEM_HEADS
    tq = _tile(S, 512, 8)
    nq = S // tq
    scale = 1.0 / math.sqrt(E)

    def body(q_ref, k_ref, v_ref, o_ref):
        for h in range(MEM_HEADS):
            cs = slice(h * E, (h + 1) * E)
            s = lax.dot_general(q_ref[:, cs].astype(MXU_DTYPE), k_ref[:, cs].astype(MXU_DTYPE),
                                (((1,), (1,)), ((), ())), preferred_element_type=F32) * scale
            e = jnp.exp(s - jnp.max(s, axis=1, keepdims=True))
            pr = e / jnp.sum(e, axis=1, keepdims=True)
            o_ref[:, cs] = jnp.dot(pr.astype(MXU_DTYPE), v_ref[:, cs].astype(MXU_DTYPE),
                                   preferred_element_type=F32).astype(o_ref.dtype)

    return _run(
        body, (qm, km, kv), out_shape=_sds((T, D), MXU_DTYPE), grid=(B, nq),
        in_specs=[pl.BlockSpec((tq, D), lambda b, i: (b * nq + i, 0)), pl.BlockSpec((NM, D), lambda b, i: (b, 0)),
                  pl.BlockSpec((NM, D), lambda b, i: (b, 1))],
        out_specs=pl.BlockSpec((tq, D), lambda b, i: (b * nq + i, 0)), name=name)


def _memattn_bwd(qm, km, kv, do, *, B, S, NM, name):
    T, D = qm.shape
    E = D // MEM_HEADS
    tq = _tile(S, 512, 8)
    nq = S // tq
    scale = 1.0 / math.sqrt(E)

    def body(q_ref, k_ref, v_ref, do_ref, dq_ref, dk_ref, dv_ref):
        i = pl.program_id(1)

        @pl.when(i == 0)
        def _():
            dk_ref[...] = jnp.zeros_like(dk_ref)
            dv_ref[...] = jnp.zeros_like(dv_ref)

        for h in range(MEM_HEADS):
            cs = slice(h * E, (h + 1) * E)
            qh = q_ref[:, cs].astype(MXU_DTYPE)
            kh = k_ref[:, cs].astype(MXU_DTYPE)
            vh = v_ref[:, cs].astype(MXU_DTYPE)
            doh = do_ref[:, cs].astype(MXU_DTYPE)
            s = lax.dot_general(qh, kh, (((1,), (1,)), ((), ())), preferred_element_type=F32) * scale
            e = jnp.exp(s - jnp.max(s, axis=1, keepdims=True))
            pr = e / jnp.sum(e, axis=1, keepdims=True)
            dp = lax.dot_general(doh, vh, (((1,), (1,)), ((), ())), preferred_element_type=F32)
            ds = (pr * (dp - jnp.sum(dp * pr, axis=1, keepdims=True))).astype(MXU_DTYPE)
            dq_ref[:, cs] = jnp.dot(ds, kh, preferred_element_type=F32) * scale
            dk_ref[:, cs] += lax.dot_general(ds, qh, (((0,), (0,)), ((), ())), preferred_element_type=F32) * scale
            dv_ref[:, cs] += lax.dot_general(pr.astype(MXU_DTYPE), doh, (((0,), (0,)), ((), ())),
                                             preferred_element_type=F32)

    qblk = pl.BlockSpec((tq, D), lambda b, i: (b * nq + i, 0))
    mblk = pl.BlockSpec((NM, D), lambda b, i: (b, 0))
    return _run(
        body, (qm, km, kv, do), out_shape=(_sds((T, D), F32), _sds((B * NM, D), F32), _sds((B * NM, D), F32)),
        grid=(B, nq), in_specs=[qblk, mblk, pl.BlockSpec((NM, D), lambda b, i: (b, 1)), qblk],
        out_specs=(qblk, mblk, mblk), name=name)


def _loss_head(y, tgt):
    T, D = y.shape
    tm = _tile(T, 512, 8)

    def body(y_ref, t_ref, dy_ref, dyl_ref, l_ref):
        i = pl.program_id(0)
        err = y_ref[...] - t_ref[...]
        dy_ref[...] = err * (1.0 / D)
        dyl_ref[...] = (err * (1.0 / D)).astype(MXU_DTYPE)
        part = jnp.sum(jnp.sum(err * err, axis=1, keepdims=True), axis=0, keepdims=True) * (0.5 / D)

        @pl.when(i == 0)
        def _():
            l_ref[...] = part

        @pl.when(i > 0)
        def _():
            l_ref[...] += part

    blk = pl.BlockSpec((tm, D), lambda i: (i, 0))
    return _run(body, (y, tgt), out_shape=(_sds((T, D), F32), _sds((T, D), MXU_DTYPE), _sds((1, 1), F32)),
                grid=(T // tm,), in_specs=[blk, blk], out_specs=(blk, blk, pl.BlockSpec((1, 1), lambda i: (0, 0))),
                name="loss_head")


def _adamw(w, g, m, v):
    m = ADAM_B1 * m + (1.0 - ADAM_B1) * g
    v = ADAM_B2 * v + (1.0 - ADAM_B2) * jnp.square(g)
    m_hat = m / (1.0 - ADAM_B1 ** ADAM_STEP)
    v_hat = v / (1.0 - ADAM_B2 ** ADAM_STEP)
    delta = -ADAM_LR * (m_hat / (jnp.sqrt(v_hat) + ADAM_EPS) + ADAM_WD * w)
    return delta, m, v


def _sum_adamw(recv, w, m, v, layer, so_far=None):
    L, r, c = w.shape
    tr = _tile(r, 256, 8)

    def body(r_ref, w_ref, m_ref, v_ref, *rest):
        g_ref, d_ref, nm_ref, nv_ref = rest[-4:]
        g = r_ref[0].astype(F32)
        for j in range(1, N_DEV):
            g = g + r_ref[j].astype(F32)
        g_ref[...] = g
        d_ref[...], nm_ref[...], nv_ref[...] = _adamw(w_ref[...], g, m_ref[...], v_ref[...])

    blk = pl.BlockSpec((None, tr, c), lambda i: (layer, i, 0))
    hbm = pl.BlockSpec(memory_space=pl.ANY)
    args, in_specs, aliases = (recv, w, m, v), [pl.BlockSpec((N_DEV, tr, c), lambda i: (0, i, 0)), blk, blk, blk], {}
    if so_far is not None:
        args, in_specs, aliases = args + tuple(so_far), in_specs + [hbm] * 4, {4 + k: k for k in range(4)}
    return _run(body, args, out_shape=(_sds((L, r, c), F32),) * 4, grid=(r // tr,), in_specs=in_specs,
                out_specs=(blk,) * 4, name="grad_sum_adamw", aliases=aliases)


def _small_sum_adamw(gp, w, m, v):
    NR, C = gp.shape
    MESH = pl.DeviceIdType.MESH

    def body(gp_ref, w_ref, m_ref, v_ref, g_ref, d_ref, nm_ref, nv_ref, gath, send_sems, recv_sems):
        x, y, c = _mesh_pos()
        me = 4 * x + 2 * y + c
        gath[me] = gp_ref[...]
        copies = []
        for k in range(1, N_DEV):
            peer = (_flip(x, k & 4), _flip(y, k & 2), _flip(c, k & 1))
            cp = pltpu.make_async_remote_copy(src_ref=gp_ref, dst_ref=gath.at[me], send_sem=send_sems.at[k - 1],
                                              recv_sem=recv_sems.at[k - 1], device_id=peer, device_id_type=MESH)
            cp.start()
            copies.append(cp)
        for cp in copies:
            cp.wait()
        g = gath[0]
        for j in range(1, N_DEV):
            g = g + gath[j]
        g_ref[...] = g
        d_ref[...], nm_ref[...], nv_ref[...] = _adamw(w_ref[...], g, m_ref[...], v_ref[...])

    vm = pl.BlockSpec(memory_space=pltpu.VMEM)
    return _pcall(body, out_shape=(_sds((NR, C), F32),) * 4, in_specs=[vm] * 4, out_specs=(vm,) * 4,
                  scratch_shapes=[pltpu.VMEM((N_DEV, NR, C), F32), pltpu.SemaphoreType.DMA((7,)),
                                  pltpu.SemaphoreType.DMA((7,))], name="small_sum_adamw")(gp, w, m, v)


def _gathered(gath, name):
    _, r, c = gath.shape
    return gath.transpose(1, 0, 2).reshape(r, N_DEV * c) if name in COL_SHARDED else gath.reshape(N_DEV * r, c)


def _scatter_blocks(grad, name):
    rows, cols = grad.shape
    if name in COL_SHARDED:
        return grad.reshape(rows, N_DEV, cols // N_DEV).transpose(1, 0, 2)
    return grad.reshape(N_DEV, rows // N_DEV, cols)


def _pack_small(d, prefix, loss=None):
    parts = []
    for n in SMALL:
        a = d[prefix + n].reshape(-1).astype(F32)
        pad = (-a.shape[0]) % LANES
        parts.append(jnp.pad(a, (0, pad)) if pad else a)
    tail = jnp.zeros((LANES,), F32)
    if loss is not None:
        tail = tail.at[0].set(loss)
    flat = jnp.concatenate(parts + [tail])
    pad = (-flat.shape[0]) % (8 * LANES)
    return jnp.pad(flat, (0, pad)).reshape(-1, LANES)


def _unpack_small(packed, shapes):
    flat = packed.reshape(-1)
    out, off = {}, 0
    for n in SMALL:
        size = math.prod(shapes[n])
        out[n] = flat[off:off + size].reshape(shapes[n])
        off += size + (-size) % LANES
    return out, flat[off]


def _row(a):
    return a.reshape(1, -1).astype(F32)


def _layer_fwd(x, memf, w_in, shards, next_in, sp, lb, tab, dims):
    B, S, NM = dims
    D = x.shape[1]
    W = {"w_in": w_in}
    h1 = _norm_fwd(x, sp["norm1_g"], E=D, W=D, out_dtype=MXU_DTYPE, name="norm1_fwd")
    z, gath = _matmul(h1, w_in, carry=("gather", [shards[n].astype(MXU_DTYPE) for n in GATHER_MIX]), name="in_proj")
    W.update({n: _gathered(t, n) for n, t in zip(GATHER_MIX, gath)})
    qn = _norm_fwd(z, sp["attn_qn_g"], E=ATT_HEAD_DIM, W=ATT_W, cb=0, out_dtype=MXU_DTYPE, name="qnorm_fwd")
    kn = _norm_fwd(z, sp["attn_kn_g"], E=ATT_HEAD_DIM, W=ATT_W, cb=1, out_dtype=MXU_DTYPE, name="knorm_fwd")
    (ya, lse, ycat), gath = _attn_fwd(tab["attn"], qn, kn, z, B=B, S=S, name="dilated_attn_fwd",
                                carry=("gather", [shards[n].astype(MXU_DTYPE) for n in GATHER_FF]))
    W.update({n: _gathered(t, n) for n, t in zip(GATHER_FF, gath)})
    o, states = _hgrn_fwd(z, lb, tab["hgrn"], B=B, S=S, name="hgrn_fwd")
    ycat = _hgout_fwd(o, z, sp["hg_onorm_g"], ycat, name="hgrn_out_fwd")
    x1 = _matmul(ycat, W["w_out"], epi="add", extra=x, name="out_proj")
    h2 = _norm_fwd(x1, sp["norm2_g"], E=D, W=D, out_dtype=MXU_DTYPE, name="norm2_fwd")
    qmp = _matmul(h2, W["w_mq"], name="mq_proj")
    qm = _norm_fwd(qmp, sp["mq_norm_g"], E=D // MEM_HEADS, W=D, out_dtype=MXU_DTYPE, name="mqnorm_fwd")
    mn = _norm_fwd(memf, sp["mem_norm_g"], E=D, W=D, out_dtype=MXU_DTYPE, name="memnorm_fwd")
    kv = _matmul(mn, W["w_mkv"], name="mkv_proj")
    km = _norm_fwd(kv, sp["mk_norm_g"], E=D // MEM_HEADS, W=D, cb=0, out_dtype=MXU_DTYPE, name="mknorm_fwd")
    om = _memattn_fwd(qm, km, kv, B=B, S=S, NM=NM, name="mem_attn_fwd")
    x2 = _matmul(om, W["w_mo"], epi="add", extra=x1, name="mo_proj")
    h3 = _norm_fwd(x2, sp["norm3_g"], E=D, W=D, out_dtype=MXU_DTYPE, name="norm3_fwd")
    if next_in is None:
        u, w_in_next = _matmul(h3, W["w_ff1"], epi="relu2", out_dtype=MXU_DTYPE, name="ff1"), None
    else:
        u, gath = _matmul(h3, W["w_ff1"], epi="relu2", out_dtype=MXU_DTYPE,
                          carry=("gather", [next_in.astype(MXU_DTYPE)]), name="ff1")
        w_in_next = _gathered(gath[0], "w_in")
    x3 = _matmul(u, W["w_ff2"], epi="add", extra=x2, name="ff2")
    saved = dict(x=x, h1=h1, z=z, qn=qn, kn=kn, ya=ya, lse=lse, o=o, states=states, ycat=ycat, x1=x1, h2=h2,
                 qmp=qmp, qm=qm, mn=mn, kv=kv, km=km, om=om, x2=x2, h3=h3, u=u)
    return x3, saved, W, w_in_next


def _layer_bwd(dx3, dx3_low, s, memf, W, sp, lb, tab, dims):
    B, S, NM = dims
    D = dx3.shape[1]
    E_M = D // MEM_HEADS
    gw, gs, recv = {}, {}, {}

    def blocks(names):
        return ("scatter", [_scatter_blocks(gw[n], n) for n in names])

    def received(names, got):
        recv.update(dict(zip(names, got)))

    da = _matmul(dx3_low, W["w_ff2"], tb=True, epi="relu2grad", extra=s["u"], out_dtype=MXU_DTYPE, name="ff2_dgrad")
    gw["w_ff2"] = _matmul(s["u"], dx3_low, ta=True, out_dtype=MXU_DTYPE, name="ff2_wgrad")
    dh3, got = _matmul(da, W["w_ff1"], tb=True, out_dtype=MXU_DTYPE, carry=blocks(SCATTER_A), name="ff1_dgrad")
    received(SCATTER_A, got)
    gw["w_ff1"] = _matmul(s["h3"], da, ta=True, out_dtype=MXU_DTYPE, name="ff1_wgrad")
    dx2, dx2_low, gs["norm3_g"] = _norm_bwd(s["x2"], sp["norm3_g"], dh3, E=D, W=D, res=dx3, low=True, name="norm3_bwd")
    dom = _matmul(dx2_low, W["w_mo"], tb=True, out_dtype=MXU_DTYPE, name="mo_dgrad")
    gw["w_mo"] = _matmul(s["om"], dx2_low, ta=True, out_dtype=MXU_DTYPE, name="mo_wgrad")
    dqm, dkm, dvm = _memattn_bwd(s["qm"], s["km"], s["kv"], dom, B=B, S=S, NM=NM, name="mem_attn_bwd")
    dqmp, gs["mq_norm_g"] = _norm_bwd(s["qmp"], sp["mq_norm_g"], dqm, E=E_M, W=D, fold=E_M, out_dtype=MXU_DTYPE,
                                      name="mqnorm_bwd")
    dkmp, gs["mk_norm_g"] = _norm_bwd(s["kv"], sp["mk_norm_g"], dkm, E=E_M, W=D, cb=0, fold=E_M,
                                      out_dtype=MXU_DTYPE, name="mknorm_bwd")
    dkv = jnp.concatenate([dkmp, dvm.astype(MXU_DTYPE)], axis=1)
    dh2 = _matmul(dqmp, W["w_mq"], tb=True, out_dtype=MXU_DTYPE, name="mq_dgrad")
    gw["w_mq"] = _matmul(s["h2"], dqmp, ta=True, out_dtype=MXU_DTYPE, name="mq_wgrad")
    dmn = _matmul(dkv, W["w_mkv"], tb=True, out_dtype=MXU_DTYPE, name="mkv_dgrad")
    gw["w_mkv"] = _matmul(s["mn"], dkv, ta=True, out_dtype=MXU_DTYPE, name="mkv_wgrad")
    _, gs["mem_norm_g"] = _norm_bwd(memf, sp["mem_norm_g"], dmn, E=D, W=D, name="memnorm_bwd")
    dx1, dx1_low, gs["norm2_g"] = _norm_bwd(s["x1"], sp["norm2_g"], dh2, E=D, W=D, res=dx2, low=True, name="norm2_bwd")
    dycat = _matmul(dx1_low, W["w_out"], tb=True, out_dtype=MXU_DTYPE, name="out_dgrad")
    gw["w_out"] = _matmul(s["ycat"], dx1_low, ta=True, out_dtype=MXU_DTYPE, name="out_wgrad")
    do_hg, dz, gs["hg_onorm_g"] = _hgout_bwd(s["o"], s["z"], sp["hg_onorm_g"], dycat, name="hgrn_out_bwd")
    dqh, dfh, dih, dlb = _hgrn_bwd(s["z"], lb, tab["hgrn"], s["states"], do_hg, B=B, S=S, name="hgrn_bwd")
    (dqn, dkn, dz), got = _attn_bwd(tab["attn"], s["qn"], s["kn"], s["z"], s["ya"], s["lse"], dycat, dz, B=B, S=S,
                                     carry=blocks(SCATTER_B), name="dilated_attn_bwd")
    received(SCATTER_B, got)
    dz, gs["attn_qn_g"] = _norm_bwd(s["z"], sp["attn_qn_g"], dqn, E=ATT_HEAD_DIM, W=ATT_W, cb=0, fold=ATT_HEAD_DIM,
                                    into=dz, name="qnorm_bwd")
    dz, gs["attn_kn_g"] = _norm_bwd(s["z"], sp["attn_kn_g"], dkn, E=ATT_HEAD_DIM, W=ATT_W, cb=1, fold=ATT_HEAD_DIM,
                                    into=dz, name="knorm_bwd")
    for block, t in ((3, dqh), (4, dfh), (5, dih)):
        dz = lax.dynamic_update_slice(dz, t, (0, block * HG_W))
    gw["w_in"] = _matmul(s["h1"], dz, ta=True, out_dtype=MXU_DTYPE, name="in_wgrad")
    dh1, got = _matmul(dz, W["w_in"], tb=True, out_dtype=MXU_DTYPE, carry=blocks(SCATTER_C), name="in_dgrad")
    received(SCATTER_C, got)
    dx0, dx0_low, gs["norm1_g"] = _norm_bwd(s["x"], sp["norm1_g"], dh1, E=D, W=D, res=dx1, low=True, name="norm1_bwd")
    gs["attn_qn_g"] = gs["attn_qn_g"][:, :ATT_HEAD_DIM]
    gs["attn_kn_g"] = gs["attn_kn_g"][:, :ATT_HEAD_DIM]
    return dx0, dx0_low, recv, gs, dlb


def kernel(x, mem, norm1_g, w_in, attn_qn_g, attn_kn_g, hg_lb, hg_onorm_g, w_out, norm2_g, mem_norm_g, w_mq, w_mkv, mq_norm_g, mk_norm_g, w_mo, norm3_g, w_ff1, w_ff2, loss_target, m_norm1_g, m_w_in, m_attn_qn_g, m_attn_kn_g, m_hg_lb, m_hg_onorm_g, m_w_out, m_norm2_g, m_mem_norm_g, m_w_mq, m_w_mkv, m_mq_norm_g, m_mk_norm_g, m_w_mo, m_norm3_g, m_w_ff1, m_w_ff2, v_norm1_g, v_w_in, v_attn_qn_g, v_attn_kn_g, v_hg_lb, v_hg_onorm_g, v_w_out, v_norm2_g, v_mem_norm_g, v_w_mq, v_w_mkv, v_mq_norm_g, v_mk_norm_g, v_w_mo, v_norm3_g, v_w_ff1, v_w_ff2):
    given = dict(norm1_g=norm1_g, w_in=w_in, attn_qn_g=attn_qn_g, attn_kn_g=attn_kn_g, hg_lb=hg_lb, hg_onorm_g=hg_onorm_g, w_out=w_out, norm2_g=norm2_g, mem_norm_g=mem_norm_g, w_mq=w_mq, w_mkv=w_mkv, mq_norm_g=mq_norm_g, mk_norm_g=mk_norm_g, w_mo=w_mo, norm3_g=norm3_g, w_ff1=w_ff1, w_ff2=w_ff2, m_norm1_g=m_norm1_g, m_w_in=m_w_in, m_attn_qn_g=m_attn_qn_g, m_attn_kn_g=m_attn_kn_g, m_hg_lb=m_hg_lb, m_hg_onorm_g=m_hg_onorm_g, m_w_out=m_w_out, m_norm2_g=m_norm2_g, m_mem_norm_g=m_mem_norm_g, m_w_mq=m_w_mq, m_w_mkv=m_w_mkv, m_mq_norm_g=m_mq_norm_g, m_mk_norm_g=m_mk_norm_g, m_w_mo=m_w_mo, m_norm3_g=m_norm3_g, m_w_ff1=m_w_ff1, m_w_ff2=m_w_ff2, v_norm1_g=v_norm1_g, v_w_in=v_w_in, v_attn_qn_g=v_attn_qn_g, v_attn_kn_g=v_attn_kn_g, v_hg_lb=v_hg_lb, v_hg_onorm_g=v_hg_onorm_g, v_w_out=v_w_out, v_norm2_g=v_norm2_g, v_mem_norm_g=v_mem_norm_g, v_w_mq=v_w_mq, v_w_mkv=v_w_mkv, v_mq_norm_g=v_mq_norm_g, v_mk_norm_g=v_mk_norm_g, v_w_mo=v_w_mo, v_norm3_g=v_norm3_g, v_w_ff1=v_w_ff1, v_w_ff2=v_w_ff2)
    B, S, D = x.shape
    NM = mem.shape[1]
    L = w_in.shape[0]
    dims = (B, S, NM)
    small_shapes = {n: given[n].shape for n in SMALL}

    def shards(prefix, l):
        return {n: given[prefix + n][l] for n in BIG}

    lower = _lower_bounds_fwd(hg_lb)
    tab = {"attn": _attn_tables(S, _tile(S, ATT_Q_BLOCK, 8)), "hgrn": _hgrn_pair_table(_tile(S, 256, HG_CHUNK))}
    xf = x.reshape(B * S, D)
    memf = mem.reshape(B * NM, D)

    def small_params(l):
        sp = {"norm1_g": _row(norm1_g[l]), "norm2_g": _row(norm2_g[l]), "norm3_g": _row(norm3_g[l]),
              "mem_norm_g": _row(mem_norm_g[l]), "hg_onorm_g": _row(hg_onorm_g[l]),
              "attn_qn_g": _row(jnp.tile(attn_qn_g[l], ATT_HEADS)), "attn_kn_g": _row(jnp.tile(attn_kn_g[l], ATT_HEADS)),
              "mq_norm_g": _row(jnp.tile(mq_norm_g[l], MEM_HEADS)), "mk_norm_g": _row(jnp.tile(mk_norm_g[l], MEM_HEADS))}
        return sp, _row(lower[l])

    w_in_full = _gathered(_all_gather([given["w_in"][0].astype(MXU_DTYPE)])[0], "w_in")
    saved, weights = [], []
    h = xf
    for l in range(L):
        sp, lb = small_params(l)
        next_in = given["w_in"][l + 1] if l + 1 < L else None
        h, s, W, w_in_full = _layer_fwd(h, memf, w_in_full, shards("", l), next_in, sp, lb, tab, dims)
        saved.append(s)
        weights.append(W)

    dh, dh_low, loss_part = _loss_head(h, loss_target.reshape(B * S, D))

    recv_layers, gs_layers, dlb_layers = [None] * L, [None] * L, [None] * L
    for l in range(L - 1, -1, -1):
        sp, lb = small_params(l)
        dh, dh_low, recv_layers[l], gs_layers[l], dlb_layers[l] = _layer_bwd(dh, dh_low, saved[l], memf, weights[l],
                                                                              sp, lb, tab, dims)
    grad_x = dh.reshape(B, S, D)

    big_out = {}
    for n in BIG:
        outs = None
        for l in range(L):
            outs = _sum_adamw(recv_layers[l][n], given[n], given["m_" + n], given["v_" + n], l, outs)
        big_out[n] = outs

    gs = {n: jnp.stack([gs_layers[l][n].reshape(small_shapes[n][1:]) for l in range(L)]) for n in SMALL if n != "hg_lb"}
    gs["hg_lb"] = _lower_bounds_bwd(hg_lb, jnp.concatenate(dlb_layers, axis=0))
    packed = _small_sum_adamw(_pack_small(gs, "", loss_part[0, 0]), _pack_small(given, ""), _pack_small(given, "m_"),
                              _pack_small(given, "v_"))
    small_out, loss = [], None
    for t in packed:
        d, tail = _unpack_small(t, small_shapes)
        small_out.append(d)
        loss = tail if loss is None else loss

    outs = [loss, grad_x]
    for k in range(4):
        outs += [small_out[k][n] if n in small_shapes else big_out[n][k] for n in WEIGHTS]
    return tuple(outs)
```

```python
import functools
import math

import jax
import jax.numpy as jnp
from jax import lax
from jax.experimental import pallas as pl
from jax.experimental.pallas import tpu as pltpu

F32 = jnp.float32
MXU_DTYPE = jnp.bfloat16
VMEM_LIMIT = 48 * 1024 * 1024

N_DEV = 8
EPS = 1e-6
NEG = -1e30
F_MIN = 1e-12
ATT_HEADS = 8
ATT_HEAD_DIM = 64
ATT_W = ATT_HEADS * ATT_HEAD_DIM
DILATIONS = (1, 4, 16)
DIL_STEPS = 128
ATT_Q_BLOCK = 256
ATT_KEY_EXTENT = 512
HG_HEADS = 4
HG_HEAD_DIM = 128
HG_W = HG_HEADS * HG_HEAD_DIM
HG_CHUNK = 32
HG_SUB = 8
MEM_HEADS = 4
LANES = 128
SUBLANES = 8
ROW_BLOCK = 512
MM_BLOCK = 1024
MM_K_BLOCK = 2048
HG_BLOCK = 256
ADAM_BLOCK = 256

ADAM_LR = 0.001
ADAM_B1 = 0.9
ADAM_B2 = 0.999
ADAM_EPS = 1e-08
ADAM_WD = 0.01
ADAM_STEP = 10

BIG = ("w_in", "w_out", "w_mq", "w_mkv", "w_mo", "w_ff1", "w_ff2")
COL_SHARDED = ("w_in", "w_mkv", "w_ff1")
SMALL = ("norm1_g", "attn_qn_g", "attn_kn_g", "hg_lb", "hg_onorm_g", "norm2_g", "mem_norm_g",
         "mq_norm_g", "mk_norm_g", "norm3_g")
WEIGHTS = ("norm1_g", "w_in", "attn_qn_g", "attn_kn_g", "hg_lb", "hg_onorm_g", "w_out", "norm2_g",
           "mem_norm_g", "w_mq", "w_mkv", "mq_norm_g", "mk_norm_g", "w_mo", "norm3_g", "w_ff1", "w_ff2")
GATHER_MIX = ("w_out", "w_mq", "w_mkv", "w_mo")
GATHER_FF = ("w_ff1", "w_ff2")
SCATTER_A = ("w_ff2",)
SCATTER_B = ("w_ff1", "w_mo", "w_mq", "w_mkv", "w_out")
SCATTER_C = ("w_in",)


def _pcall(body, **kw):
    return pl.pallas_call(body, **kw)


def _params():
    return pltpu.CompilerParams(vmem_limit_bytes=VMEM_LIMIT)


def _tile(n, pref, mult):
    t = (min(n, pref) // mult) * mult
    while t >= mult:
        if n % t == 0:
            return t
        t -= mult
    return n


def _sds(shape, dtype):
    return jax.ShapeDtypeStruct(shape, dtype)


def _mesh_pos():
    return lax.axis_index("x"), lax.axis_index("y"), lax.axis_index("c")


def _flip(v, bit):
    return 1 - v if bit else v


def _gather_hooks(x_refs, out_refs, send_sems, recv_sems, local_sems):
    MESH = pl.DeviceIdType.MESH
    n = len(x_refs)
    x, y, c = _mesh_pos()
    me, sibling = (x, y, c), (x, y, 1 - c)
    chips = [(1 - x, y), (x, 1 - y), (1 - x, 1 - y)]

    def rows(a, px, py, pc):
        return out_refs[a].at[4 * px + 2 * py + pc]

    def copy(a, k, block, to, own=False):
        return pltpu.make_async_remote_copy(
            src_ref=x_refs[a] if own else rows(a, *block), dst_ref=rows(a, *block), send_sem=send_sems.at[n * k + a],
            recv_sem=recv_sems.at[n * k + a], device_id=to, device_id_type=MESH)

    mine = [pltpu.make_async_copy(x_refs[a], rows(a, *me), local_sems.at[a]) for a in range(n)]
    first = [copy(a, 0, me, sibling, own=True) for a in range(n)]
    first += [copy(a, 1 + j, me, (*chip, c), own=True) for j, chip in enumerate(chips) for a in range(n)]
    passed = [[copy(a, 4 + j, (*chip, c), sibling) for a in range(n)] for j, chip in enumerate(chips)]

    def start():
        for cp in mine + first:
            cp.start()

    def forward():
        for j, chip in enumerate(chips):
            for a in range(n):
                copy(a, 1 + j, (*chip, c), me).wait_recv()
                passed[j][a].start()

    def finish():
        for a in range(n):
            copy(a, 0, sibling, me).wait_recv()
        for j, chip in enumerate(chips):
            for a in range(n):
                copy(a, 4 + j, (*chip, 1 - c), me).wait_recv()
        for cp in first + [cp for group in passed for cp in group]:
            cp.wait_send()
        for cp in mine:
            cp.wait()

    return start, forward, finish


def _scatter_hooks(s_refs, r_refs, send_sems, recv_sems, local_sems):
    MESH = pl.DeviceIdType.MESH
    n = len(s_refs)
    x, y, c = _mesh_pos()
    me = 4 * x + 2 * y + c
    mine = [pltpu.make_async_copy(s_refs[a].at[me], r_refs[a].at[me], local_sems.at[a]) for a in range(n)]
    copies = []
    for m in range(1, N_DEV):
        px, py, pc = _flip(x, m & 4), _flip(y, m & 2), _flip(c, m & 1)
        for a in range(n):
            copies.append(pltpu.make_async_remote_copy(
                src_ref=s_refs[a].at[4 * px + 2 * py + pc], dst_ref=r_refs[a].at[me],
                send_sem=send_sems.at[n * (m - 1) + a], recv_sem=recv_sems.at[n * (m - 1) + a],
                device_id=(px, py, pc), device_id_type=MESH))

    def start():
        for cp in mine + copies:
            cp.start()

    def finish():
        for cp in copies:
            cp.wait()
        for cp in mine:
            cp.wait()

    return start, None, finish


def _exchange_sems(n):
    return [pltpu.SemaphoreType.DMA(((N_DEV - 1) * n,)), pltpu.SemaphoreType.DMA(((N_DEV - 1) * n,)),
            pltpu.SemaphoreType.DMA((n,))]


def _exchange_out(kind, xs):
    return _sds((N_DEV,) + xs.shape, xs.dtype) if kind == "gather" else _sds(xs.shape, xs.dtype)


def _run(body, args, *, out_shape, grid, in_specs, out_specs, scratch_shapes=(), name, carry=None, aliases=None):
    single = not isinstance(out_shape, (tuple, list))
    outs = (out_shape,) if single else tuple(out_shape)
    ospecs = (out_specs,) if single else tuple(out_specs)
    aliases = dict(aliases or {})
    if carry is None:
        res = _pcall(body, out_shape=outs, grid=grid, in_specs=list(in_specs), out_specs=ospecs,
                     scratch_shapes=list(scratch_shapes), input_output_aliases=aliases, compiler_params=_params(),
                     name=name)(*args)
        return res[0] if single else tuple(res)
    kind, xs = carry
    n_in, n_out, n_scr, n_x = len(args), len(outs), len(scratch_shapes), len(xs)
    total = math.prod(grid)

    def wrapped(*refs):
        refs = list(refs)
        ins, x_refs = refs[:n_in], refs[n_in:n_in + n_x]
        o, xo_refs = refs[n_in + n_x:n_in + n_x + n_out], refs[n_in + n_x + n_out:n_in + 2 * n_x + n_out]
        scr = refs[n_in + 2 * n_x + n_out:n_in + 2 * n_x + n_out + n_scr]
        sems = refs[n_in + 2 * n_x + n_out + n_scr:]
        step = pl.program_id(0)
        for ax in range(1, len(grid)):
            step = step * grid[ax] + pl.program_id(ax)
        start, forward, finish = (_gather_hooks if kind == "gather" else _scatter_hooks)(x_refs, xo_refs, *sems)
        pl.when(step == 0)(start)
        body(*ins, *o, *scr)
        if forward is not None:
            pl.when(step == (3 * total) // 4)(forward)
        pl.when(step == total - 1)(finish)

    hbm = pl.BlockSpec(memory_space=pl.ANY)
    res = _pcall(wrapped, out_shape=outs + tuple(_exchange_out(kind, t) for t in xs), grid=grid,
                 in_specs=list(in_specs) + [hbm] * n_x, out_specs=ospecs + (hbm,) * n_x,
                 scratch_shapes=list(scratch_shapes) + _exchange_sems(n_x), input_output_aliases=aliases,
                 compiler_params=_params(), name=name)(*args, *xs)
    main = tuple(res[:n_out])
    return (main[0] if single else main), list(res[n_out:])


def _matmul(a, b, *, ta=False, tb=False, epi=None, extra=None, out_dtype=F32, carry=None, name):
    M, K = (a.shape[1], a.shape[0]) if ta else a.shape
    N = b.shape[0] if tb else b.shape[1]
    tm = _tile(M, MM_BLOCK, SUBLANES if not ta else LANES)
    tn = _tile(N, MM_BLOCK, LANES)
    tk = _tile(K, MM_K_BLOCK // 2 if ta else MM_K_BLOCK, LANES if not ta else SUBLANES)
    nk = K // tk
    dims = (((0 if ta else 1,), (1 if tb else 0,)), ((), ()))

    def body(a_ref, b_ref, *rest):
        e_ref = rest[0] if extra is not None else None
        o_ref = rest[1] if extra is not None else rest[0]

        def finish(r):
            if epi == "add":
                r = r + e_ref[...]
            elif epi == "relu2":
                r = jnp.square(jnp.maximum(r, 0.0))
            elif epi == "relu2grad":
                r = r * (2.0 * jnp.sqrt(e_ref[...].astype(F32)))
            o_ref[...] = r.astype(out_dtype)

        part = lax.dot_general(a_ref[...].astype(MXU_DTYPE), b_ref[...].astype(MXU_DTYPE), dims,
                               preferred_element_type=F32)
        if nk == 1:
            finish(part)
        else:
            acc_ref = rest[-1]
            k = pl.program_id(2)

            @pl.when(k == 0)
            def _():
                acc_ref[...] = part

            @pl.when(k > 0)
            def _():
                acc_ref[...] += part

            @pl.when(k == nk - 1)
            def _():
                finish(acc_ref[...])

    a_spec = pl.BlockSpec((tk, tm), lambda i, j, k: (k, i)) if ta else pl.BlockSpec((tm, tk), lambda i, j, k: (i, k))
    b_spec = pl.BlockSpec((tn, tk), lambda i, j, k: (j, k)) if tb else pl.BlockSpec((tk, tn), lambda i, j, k: (k, j))
    o_spec = pl.BlockSpec((tm, tn), lambda i, j, k: (i, j))
    in_specs = [a_spec, b_spec] + ([o_spec] if extra is not None else [])
    args = (a, b) + ((extra,) if extra is not None else ())
    return _run(body, args, out_shape=_sds((M, N), out_dtype), grid=(M // tm, N // tn, nk), in_specs=in_specs,
                out_specs=o_spec, scratch_shapes=[pltpu.VMEM((tm, tn), F32)] if nk > 1 else [], name=name, carry=carry)


def _group_mean(v, E):
    rows, W = v.shape
    if E == W:
        return jnp.mean(v, axis=-1, keepdims=True)
    pieces = []
    if E % LANES == 0:
        for g0 in range(0, W, E):
            m = jnp.mean(v[:, g0:g0 + E], axis=-1, keepdims=True)
            pieces.append(jnp.broadcast_to(m, (rows, E)))
    else:
        lane = lax.broadcasted_iota(jnp.int32, (rows, LANES), 1)
        for c0 in range(0, W, LANES):
            vc = v[:, c0:c0 + LANES]
            acc = jnp.zeros((rows, LANES), F32)
            for s0 in range(0, LANES, E):
                msk = (lane >= s0) & (lane < s0 + E)
                m = jnp.sum(jnp.where(msk, vc, 0.0), axis=-1, keepdims=True) * (1.0 / E)
                acc = jnp.where(msk, m, acc)
            pieces.append(acc)
    return jnp.concatenate(pieces, axis=-1)


def _fold_groups(t, E):
    W = t.shape[1]
    step = max(E, LANES)
    acc = t[:, 0:step]
    for c0 in range(step, W, step):
        acc = acc + t[:, c0:c0 + step]
    sh = LANES // 2
    while sh >= E:
        acc = acc + pltpu.roll(acc, sh, 1)
        sh //= 2
    return acc


def _norm_fwd(x, g, *, E, W, cb=0, out_dtype, carry=None, name):
    M = x.shape[0]
    tm = _tile(M, ROW_BLOCK, SUBLANES)

    def body(x_ref, g_ref, o_ref):
        xv = x_ref[...]
        r = lax.rsqrt(_group_mean(xv * xv, E) + EPS)
        o_ref[...] = (xv * r * g_ref[...]).astype(out_dtype)

    return _run(body, (x, g), out_shape=_sds((M, W), out_dtype), grid=(M // tm,),
                in_specs=[pl.BlockSpec((tm, W), lambda i: (i, cb)), pl.BlockSpec((1, W), lambda i: (0, 0))],
                out_specs=pl.BlockSpec((tm, W), lambda i: (i, 0)), name=name, carry=carry)


def _norm_bwd(x, g, dy, *, E, W, cb=0, res=None, fold=None, out_dtype=F32, low=False, into=None, name):
    M = x.shape[0]
    tm = _tile(M, ROW_BLOCK, SUBLANES)
    n = M // tm
    gw = W if fold is None else max(fold, LANES)

    def body(x_ref, g_ref, dy_ref, *rest):
        rest = list(rest)
        r_ref = rest.pop(0) if res is not None else None
        if into is not None:
            rest.pop(0)
        dx_ref = rest.pop(0)
        dxl_ref = rest.pop(0) if low else None
        dg_ref, acc_ref = rest
        i = pl.program_id(0)
        xv = x_ref[...]
        r = lax.rsqrt(_group_mean(xv * xv, E) + EPS)
        xh = xv * r
        dyv = dy_ref[...].astype(F32)
        dyg = dyv * g_ref[...]
        dx = r * (dyg - xh * _group_mean(dyg * xh, E))
        if res is not None:
            dx = dx + r_ref[...]
        dx_ref[...] = dx.astype(dx_ref.dtype)
        if low:
            dxl_ref[...] = dx.astype(MXU_DTYPE)
        part = jnp.sum(dyv * xh, axis=0, keepdims=True)

        @pl.when(i == 0)
        def _():
            acc_ref[...] = part

        @pl.when(i > 0)
        def _():
            acc_ref[...] += part

        @pl.when(i == n - 1)
        def _():
            t = acc_ref[...]
            dg_ref[...] = t if fold is None else _fold_groups(t, fold)

    blk = pl.BlockSpec((tm, W), lambda i: (i, 0))
    in_specs = [pl.BlockSpec((tm, W), lambda i: (i, cb)), pl.BlockSpec((1, W), lambda i: (0, 0)), blk]
    args = [x, g, dy]
    if res is not None:
        in_specs.append(blk)
        args.append(res)
    lows = ((_sds((M, W), MXU_DTYPE),), (blk,)) if low else ((), ())
    dx_out, dx_spec, aliases = _sds((M, W), out_dtype), blk, None
    if into is not None:
        aliases = {len(args): 0}
        args.append(into)
        in_specs.append(pl.BlockSpec(memory_space=pl.ANY))
        dx_out, dx_spec = _sds(into.shape, into.dtype), pl.BlockSpec((tm, W), lambda i: (i, cb))
    return _run(body, tuple(args), out_shape=(dx_out,) + lows[0] + (_sds((1, gw), F32),), grid=(n,),
                in_specs=in_specs, out_specs=(dx_spec,) + lows[1] + (pl.BlockSpec((1, gw), lambda i: (0, 0)),),
                scratch_shapes=[pltpu.VMEM((1, W), F32)], name=name, aliases=aliases)


def _attn_tables(S, tq):
    nq = S // tq

    def body(o_ref):
        i = pl.program_id(0)
        d = (i * tq + lax.broadcasted_iota(jnp.int32, (tq, S), 0)) - lax.broadcasted_iota(jnp.int32, (tq, S), 1)
        cnt = jnp.zeros((tq, S), jnp.int32)
        for dil in DILATIONS:
            hit = (d <= DIL_STEPS * dil) if dil == 1 else (((d & (dil - 1)) == 0) & (d <= DIL_STEPS * dil))
            cnt = cnt + hit.astype(jnp.int32)
        ok = (d >= 0) & (cnt > 0)
        logm = jnp.where(cnt == 3, math.log(3.0), jnp.where(cnt == 2, math.log(2.0), 0.0))
        o_ref[0] = jnp.where(ok, logm, NEG).astype(F32)

    return _run(body, (), out_shape=_sds((nq, tq, S), F32), grid=(nq,), in_specs=[],
                out_specs=pl.BlockSpec((1, tq, S), lambda i: (i, 0, 0)), name="dilated_attn_tables")


def _alibi_slope(h):
    return 2.0 ** (-8.0 / ATT_HEADS * (h + 1))


def _key_positions(Sk):
    kpos = lax.broadcasted_iota(jnp.int32, (Sk, LANES), 0)
    return (kpos >> 8).astype(F32).astype(MXU_DTYPE), (kpos & 255).astype(F32).astype(MXU_DTYPE)


def _score_operands(q2s, k2, pos_hi, pos_lo, lane, sub, h):
    own = (lane < ATT_HEAD_DIM) if sub == 0 else (lane >= ATT_HEAD_DIM)
    spare = ATT_HEAD_DIM if sub == 0 else 0
    slope = _alibi_slope(h)
    terms = jnp.where(lane == spare, slope * 256.0, jnp.where(lane == spare + 1, slope, 0.0)).astype(q2s.dtype)
    qa = jnp.where(own, q2s, terms)
    ka = jnp.where(lane == spare, pos_hi, jnp.where(lane == spare + 1, pos_lo, k2))
    return own, qa, ka


def _key_extents(S, tq):
    ext = min(ATT_KEY_EXTENT, S)
    return ext, ext // tq, S // ext


def _attn_fwd(tab, qn, kn, z, *, B, S, carry=None, name):
    T = B * S
    tq = tab.shape[1]
    nq = S // tq
    ext, per, n_ext = _key_extents(S, tq)
    scale = 1.0 / math.sqrt(ATT_HEAD_DIM)
    dn_t = (((1,), (1,)), ((), ()))

    def body(tab_ref, q_ref, k_ref, v_ref, y_ref, lse_ref, ycat_ref):
        i = pl.program_id(1)
        lane = lax.broadcasted_iota(jnp.int32, (1, LANES), 1)
        lo = lane < ATT_HEAD_DIM

        def visit(Sk):
            base = tab_ref[0, :, :Sk]
            pos_hi, pos_lo = _key_positions(Sk)
            lse_blk = jnp.zeros((tq, LANES), F32)
            for p in range(ATT_HEADS // 2):
                cs = slice(p * LANES, (p + 1) * LANES)
                q2s = q_ref[:, cs].astype(MXU_DTYPE) * scale
                k2 = k_ref[:Sk, cs].astype(MXU_DTYPE)
                v2 = v_ref[:Sk, cs].astype(MXU_DTYPE)
                outs = []
                for sub in range(2):
                    h = 2 * p + sub
                    _, qa, ka = _score_operands(q2s, k2, pos_hi, pos_lo, lane, sub, h)
                    s = lax.dot_general(qa, ka, dn_t, preferred_element_type=F32) + base
                    mx = jnp.max(s, axis=1, keepdims=True)
                    e = jnp.exp(s - mx)
                    l = jnp.sum(e, axis=1, keepdims=True)
                    outs.append(jnp.dot(e.astype(MXU_DTYPE), v2, preferred_element_type=F32) / l)
                    lse_blk = jnp.where(lane == h, mx + jnp.log(l), lse_blk)
                y = jnp.where(lo, outs[0], outs[1])
                y_ref[:, cs] = y
                ycat_ref[:, cs] = y.astype(MXU_DTYPE)
            lse_ref[...] = lse_blk

        for e in range(n_ext):
            pl.when(i // per == e)(functools.partial(visit, (e + 1) * ext))

    return _run(
        body, (tab, qn, kn, z), grid=(B, nq),
        out_shape=(_sds((T, ATT_W), F32), _sds((T, LANES), F32), _sds((T, ATT_W + HG_W), MXU_DTYPE)),
        in_specs=[pl.BlockSpec((1, tq, S), lambda b, i: (i, 0, 0)),
                  pl.BlockSpec((tq, ATT_W), lambda b, i: (b * nq + i, 0)),
                  pl.BlockSpec((S, ATT_W), lambda b, i: (b, 0)),
                  pl.BlockSpec((S, ATT_W), lambda b, i: (b, 2))],
        out_specs=(pl.BlockSpec((tq, ATT_W), lambda b, i: (b * nq + i, 0)),
                   pl.BlockSpec((tq, LANES), lambda b, i: (b * nq + i, 0)),
                   pl.BlockSpec((tq, ATT_W), lambda b, i: (b * nq + i, 0))),
        name=name, carry=carry)


def _attn_bwd(tab, qn, kn, z, y, lse, dycat, dz, *, B, S, carry=None, name):
    T = B * S
    tq = tab.shape[1]
    nq = S // tq
    ext, per, n_ext = _key_extents(S, tq)
    scale = 1.0 / math.sqrt(ATT_HEAD_DIM)
    dn_t = (((1,), (1,)), ((), ()))
    dn_o = (((0,), (0,)), ((), ()))

    def body(tab_ref, q_ref, k_ref, v_ref, y_ref, lse_ref, dy_ref, _, dq_ref, dk_ref, dz_ref, dv_ref):
        i = pl.program_id(1)

        @pl.when(i == 0)
        def _():
            dk_ref[...] = jnp.zeros_like(dk_ref)
            dv_ref[...] = jnp.zeros_like(dv_ref)

        lane = lax.broadcasted_iota(jnp.int32, (1, LANES), 1)
        lo = lane < ATT_HEAD_DIM

        def visit(Sk):
            base = tab_ref[0, :, :Sk]
            pos_hi, pos_lo = _key_positions(Sk)
            lse_blk = lse_ref[...]
            for p in range(ATT_HEADS // 2):
                cs = slice(p * LANES, (p + 1) * LANES)
                q2 = q_ref[:, cs].astype(MXU_DTYPE)
                q2s = q2 * scale
                k2 = k_ref[:Sk, cs].astype(MXU_DTYPE)
                v2 = v_ref[:Sk, cs].astype(MXU_DTYPE)
                do2 = dy_ref[:, cs]
                doy = do2 * y_ref[:, cs]
                do2m = do2.astype(MXU_DTYPE)
                dqs, dks, dvs = [], [], []
                for sub in range(2):
                    h = 2 * p + sub
                    own, qa, ka = _score_operands(q2s, k2, pos_hi, pos_lo, lane, sub, h)
                    s = lax.dot_general(qa, ka, dn_t, preferred_element_type=F32) + base
                    lse_h = jnp.sum(jnp.where(lane == h, lse_blk, 0.0), axis=1, keepdims=True)
                    pr = jnp.exp(s - lse_h)
                    dsum = jnp.sum(jnp.where(own, doy, 0.0), axis=1, keepdims=True)
                    dom = jnp.where(own, do2m, jnp.zeros_like(do2m))
                    dp = lax.dot_general(dom, v2, dn_t, preferred_element_type=F32)
                    ds = (pr * (dp - dsum)).astype(MXU_DTYPE)
                    dqs.append(jnp.dot(ds, k2, preferred_element_type=F32))
                    dks.append(lax.dot_general(ds, q2, dn_o, preferred_element_type=F32))
                    dvs.append(lax.dot_general(pr.astype(MXU_DTYPE), do2m, dn_o, preferred_element_type=F32))
                dq_ref[:, cs] = jnp.where(lo, dqs[0], dqs[1]) * scale
                dk_ref[:Sk, cs] += jnp.where(lo, dks[0], dks[1]) * scale
                dv_ref[:Sk, cs] += jnp.where(lo, dvs[0], dvs[1])

        for e in range(n_ext):
            pl.when(i // per == e)(functools.partial(visit, (e + 1) * ext))

        @pl.when(i == nq - 1)
        def _():
            dz_ref[...] = dv_ref[...].astype(dz_ref.dtype)

    qblk = pl.BlockSpec((tq, ATT_W), lambda b, i: (b * nq + i, 0))
    sblk = pl.BlockSpec((S, ATT_W), lambda b, i: (b, 0))
    vblk = pl.BlockSpec((S, ATT_W), lambda b, i: (b, 2))
    return _run(
        body, (tab, qn, kn, z, y, lse, dycat, dz), grid=(B, nq),
        out_shape=(_sds((T, ATT_W), F32), _sds((T, ATT_W), F32), _sds(dz.shape, dz.dtype)),
        in_specs=[pl.BlockSpec((1, tq, S), lambda b, i: (i, 0, 0)), qblk, sblk, vblk, qblk,
                  pl.BlockSpec((tq, LANES), lambda b, i: (b * nq + i, 0)), qblk, pl.BlockSpec(memory_space=pl.ANY)],
        out_specs=(qblk, sblk, vblk), scratch_shapes=[pltpu.VMEM((S, ATT_W), F32)], name=name, carry=carry,
        aliases={7: 2})


def _sigmoid_pair(x):
    en = jnp.exp(-jnp.abs(x))
    big = 1.0 / (1.0 + en)
    small = en * big
    pos = x >= 0
    return jnp.where(pos, big, small), jnp.where(pos, small, big)


def _chunk_scan(x, pos, reverse):
    n = x.shape[0]
    sh = 1
    while sh < HG_CHUNK:
        if reverse:
            x = x + jnp.where(pos < HG_CHUNK - sh, pltpu.roll(x, n - sh, 0), 0.0)
        else:
            x = x + jnp.where(pos >= sh, pltpu.roll(x, sh, 0), 0.0)
        sh *= 2
    return x


def _hgrn_pair_table(tb):
    def body(o_ref):
        t = lax.broadcasted_iota(jnp.int32, (tb, tb), 0)
        s = lax.broadcasted_iota(jnp.int32, (tb, tb), 1)
        o_ref[...] = jnp.where((t // HG_CHUNK) == (s // HG_CHUNK), (t // HG_SUB) - (s // HG_SUB), 0).astype(F32)

    return _pcall(body, out_shape=_sds((tb, tb), F32), name="hgrn_pair_table")()


def _hgrn_cross_decays(b, pos):
    n = b.shape[0]
    pos8 = pos % HG_SUB
    end = b
    sh = 1
    while sh < HG_SUB:
        end = jnp.where((pos8 & sh) == 0, pltpu.roll(end, n - sh, 0), end)
        sh *= 2
    x = jnp.where(pos >= HG_SUB, jnp.exp(jnp.minimum(b - pltpu.roll(end, HG_SUB, 0), 0.0)), 0.0)
    ys = []
    for j in range(1, HG_CHUNK // HG_SUB):
        end_j = end if j == 1 else pltpu.roll(end, n - HG_SUB * (j - 1), 0)
        ys.append(jnp.where(pos < HG_CHUNK - HG_SUB * j, jnp.exp(jnp.minimum(end_j - b, 0.0)), 0.0))
    return x, ys


def _hgrn_cross_scores(qt, kts, pair):
    n = qt.shape[0]
    kcat = jnp.concatenate([k.astype(MXU_DTYPE) for k in kts], axis=0)
    p = lax.dot_general(qt.astype(MXU_DTYPE), kcat, (((1,), (1,)), ((), ())), preferred_element_type=F32)
    a = jnp.zeros((n, n), F32)
    for j in range(1, len(kts) + 1):
        a = jnp.where(pair == j, p[:, (j - 1) * n:j * n], a)
    return a, kcat


def _hgrn_gates(qh, fh, lb):
    sq, _ = _sigmoid_pair(qh)
    sg, sgn = _sigmoid_pair(fh)
    f = lb + (1.0 - lb) * sg
    g = jnp.log(jnp.maximum(f, F_MIN))
    kk = (1.0 - lb) * sgn
    return qh * sq, sq, sg, sgn, f, g, kk


def _hgrn_fwd(z, lb, pair, *, B, S, name):
    T = B * S
    tb = _tile(S, HG_BLOCK, HG_CHUNK)
    nc = tb // HG_CHUNK
    nb = S // tb
    dn_t = (((1,), (1,)), ((), ()))
    dn_o = (((0,), (0,)), ((), ()))

    def zcol(first):
        return lambda h, b, i: (b * nb + i, first + h)

    def body(q_ref, f_ref, v_ref, lb_ref, pair_ref, o_ref, st_ref, state):
        i = pl.program_id(2)

        @pl.when(i == 0)
        def _():
            state[...] = jnp.zeros_like(state)

        q, _, _, _, _, g, kk = _hgrn_gates(q_ref[...], f_ref[...], lb_ref[...])
        v = v_ref[...]
        pos = lax.broadcasted_iota(jnp.int32, (tb, LANES), 0) % HG_CHUNK
        pos8 = pos % HG_SUB
        b = _chunk_scan(g, pos, False)
        o = jnp.sum(q * kk, axis=1, keepdims=True) * v
        for d in range(1, HG_SUB):
            qd = pltpu.roll(q, tb - d, 0)
            bd = pltpu.roll(b, tb - d, 0)
            e = jnp.where(pos8 < HG_SUB - d, jnp.exp(jnp.minimum(bd - b, 0.0)), 0.0)
            a = jnp.sum(qd * kk * e, axis=1, keepdims=True)
            o = o + pltpu.roll(a * v, d, 0)
        x, ys = _hgrn_cross_decays(b, pos)
        a_cross, _ = _hgrn_cross_scores(q * x, [kk * y for y in ys], pair_ref[...])
        o = o + jnp.dot(a_cross.astype(MXU_DTYPE), v.astype(MXU_DTYPE), preferred_element_type=F32)
        qe = q * jnp.exp(b)
        chunks = [slice(c * HG_CHUNK, (c + 1) * HG_CHUNK) for c in range(nc)]
        b_last = [jnp.sum(jnp.where(pos[rs] == HG_CHUNK - 1, b[rs], 0.0), axis=0, keepdims=True) for rs in chunks]
        own = [lax.dot_general(v[rs].astype(MXU_DTYPE), (kk[rs] * jnp.exp(bl - b[rs])).astype(MXU_DTYPE), dn_o,
                               preferred_element_type=F32) for rs, bl in zip(chunks, b_last)]
        st = state[...]
        before = []
        for c in range(nc):
            before.append(st)
            st = st * jnp.exp(b_last[c]) + own[c]
        state[...] = st
        for c, rs in enumerate(chunks):
            st_ref[c * LANES:(c + 1) * LANES, :] = before[c]
            o_ref[rs, :] = o[rs] + lax.dot_general(qe[rs].astype(MXU_DTYPE), before[c].astype(MXU_DTYPE), dn_t,
                                                   preferred_element_type=F32)

    blk = (tb, LANES)
    return _run(
        body, (z, z, z, lb, pair), out_shape=(_sds((T, HG_W), F32), _sds((T // HG_CHUNK * LANES, HG_W), F32)),
        grid=(HG_HEADS, B, nb),
        in_specs=[pl.BlockSpec(blk, zcol(12)), pl.BlockSpec(blk, zcol(16)), pl.BlockSpec(blk, zcol(20)),
                  pl.BlockSpec((1, LANES), lambda h, b, i: (0, h)), pl.BlockSpec((tb, tb), lambda h, b, i: (0, 0))],
        out_specs=(pl.BlockSpec(blk, lambda h, b, i: (b * nb + i, h)),
                   pl.BlockSpec((nc * LANES, LANES), lambda h, b, i: (b * nb + i, h))),
        scratch_shapes=[pltpu.VMEM((LANES, LANES), F32)], name=name)


def _hgrn_bwd(z, lb, pair, states, do, *, B, S, name):
    T = B * S
    tb = _tile(S, HG_BLOCK, HG_CHUNK)
    nc = tb // HG_CHUNK
    nb = S // tb
    dn_t = (((1,), (1,)), ((), ()))
    dn_o = (((0,), (0,)), ((), ()))

    def body(q_ref, f_ref, v_ref, lb_ref, pair_ref, st_ref, do_ref, dq_ref, df_ref, di_ref, dlb_ref, dstate):
        bi = pl.program_id(1)
        i = pl.program_id(2)

        @pl.when(i == 0)
        def _():
            dstate[...] = jnp.zeros_like(dstate)

        qh = q_ref[...]
        lbv = lb_ref[...]
        q, sq, sg, sgn, f, g, kk = _hgrn_gates(qh, f_ref[...], lbv)
        v = v_ref[...]
        dov = do_ref[...]
        pos = lax.broadcasted_iota(jnp.int32, (tb, LANES), 0) % HG_CHUNK
        pos8 = pos % HG_SUB
        b = _chunk_scan(g, pos, False)

        a0 = jnp.sum(q * kk, axis=1, keepdims=True)
        da0 = jnp.sum(dov * v, axis=1, keepdims=True)
        dv = a0 * dov
        dkk = da0 * q
        dq = da0 * kk
        db = jnp.zeros((tb, LANES), F32)
        for d in range(1, HG_SUB):
            qd = pltpu.roll(q, tb - d, 0)
            bd = pltpu.roll(b, tb - d, 0)
            dod = pltpu.roll(dov, tb - d, 0)
            e = jnp.where(pos8 < HG_SUB - d, jnp.exp(jnp.minimum(bd - b, 0.0)), 0.0)
            a = jnp.sum(qd * kk * e, axis=1, keepdims=True)
            da = jnp.sum(dod * v, axis=1, keepdims=True)
            dv = dv + a * dod
            xe = da * e
            dkk = dkk + xe * qd
            t1 = xe * kk
            wt = t1 * qd
            dq = dq + pltpu.roll(t1, d, 0)
            db = db + pltpu.roll(wt, d, 0) - wt

        n_cross = HG_CHUNK // HG_SUB - 1
        pair = pair_ref[...]
        x, ys = _hgrn_cross_decays(b, pos)
        qt = q * x
        kts = [kk * y for y in ys]
        a_cross, kcat = _hgrn_cross_scores(qt, kts, pair)
        dom = dov.astype(MXU_DTYPE)
        da_all = lax.dot_general(dom, v.astype(MXU_DTYPE), dn_t, preferred_element_type=F32)
        dv = dv + lax.dot_general(a_cross.astype(MXU_DTYPE), dom, dn_o, preferred_element_type=F32)
        gcat = jnp.concatenate([jnp.where(pair == j, da_all, 0.0).astype(MXU_DTYPE) for j in range(1, n_cross + 1)],
                               axis=1)
        dqt = jnp.dot(gcat, kcat, preferred_element_type=F32)
        dkcat = lax.dot_general(gcat, qt.astype(MXU_DTYPE), dn_o, preferred_element_type=F32)
        dq = dq + dqt * x
        u = dqt * qt
        db = db + u
        d_end = -pltpu.roll(u, tb - HG_SUB, 0)
        for j in range(1, n_cross + 1):
            dkt = dkcat[(j - 1) * tb:j * tb]
            dkk = dkk + dkt * ys[j - 1]
            w = dkt * kts[j - 1]
            db = db - w
            d_end = d_end + (w if j == 1 else pltpu.roll(w, HG_SUB * (j - 1), 0))
        sh = 1
        while sh < HG_SUB:
            d_end = d_end + jnp.where(pos8 >= sh, pltpu.roll(d_end, sh, 0), 0.0)
            sh *= 2
        db = db + jnp.where(pos8 == HG_SUB - 1, d_end, 0.0)

        eb = jnp.exp(b)
        qe = q * eb
        chunks = [slice(c * HG_CHUNK, (c + 1) * HG_CHUNK) for c in range(nc)]
        b_lasts = [jnp.sum(jnp.where(pos[rs] == HG_CHUNK - 1, b[rs], 0.0), axis=0, keepdims=True) for rs in chunks]
        own = [lax.dot_general(dom[rs], qe[rs].astype(MXU_DTYPE), dn_o, preferred_element_type=F32) for rs in chunks]
        dst = dstate[...]
        after = [None] * nc
        for c in range(nc - 1, -1, -1):
            after[c] = dst
            dst = dst * jnp.exp(b_lasts[c]) + own[c]
        dstate[...] = dst
        dq_c, dkk_c, dv_c, db_c = [None] * nc, [None] * nc, [None] * nc, [None] * nc
        for c, rs in enumerate(chunks):
            st0 = st_ref[c * LANES:(c + 1) * LANES, :]
            eb_last = jnp.exp(b_lasts[c])
            er = jnp.exp(b_lasts[c] - b[rs])
            ke = kk[rs] * er
            dstm = after[c].astype(MXU_DTYPE)
            dqe = jnp.dot(dom[rs], st0.astype(MXU_DTYPE), preferred_element_type=F32)
            dv_c[c] = dv[rs] + lax.dot_general(ke.astype(MXU_DTYPE), dstm, dn_t, preferred_element_type=F32)
            dke = jnp.dot(v[rs].astype(MXU_DTYPE), dstm, preferred_element_type=F32)
            d_eb_last = jnp.sum(after[c] * st0, axis=0, keepdims=True)
            dq_c[c] = dq[rs] + dqe * eb[rs]
            dkk_c[c] = dkk[rs] + dke * er
            dkeke = dke * ke
            db_last = jnp.sum(dkeke, axis=0, keepdims=True) + d_eb_last * eb_last
            db_c[c] = db[rs] + dqe * qe[rs] - dkeke + jnp.where(pos[rs] == HG_CHUNK - 1, db_last, 0.0)
        dq = jnp.concatenate(dq_c, axis=0)
        dkk = jnp.concatenate(dkk_c, axis=0)
        dv = jnp.concatenate(dv_c, axis=0)
        dg = _chunk_scan(jnp.concatenate(db_c, axis=0), pos, True)

        dfv = dg * jnp.where(f > F_MIN, 1.0 / f, 0.0)
        dsg = (dfv - dkk) * (1.0 - lbv)
        dq_ref[...] = (dq * (sq * (1.0 + qh * (1.0 - sq)))).astype(dq_ref.dtype)
        df_ref[...] = (dsg * sg * sgn).astype(df_ref.dtype)
        di_ref[...] = dv.astype(di_ref.dtype)
        part = jnp.sum(dfv * sgn - dkk * sgn, axis=0, keepdims=True)

        @pl.when((bi == 0) & (i == 0))
        def _():
            dlb_ref[...] = part

        @pl.when((bi > 0) | (i > 0))
        def _():
            dlb_ref[...] += part

    def rev(first):
        return lambda h, b, i: (b * nb + (nb - 1 - i), first + h)

    blk = (tb, LANES)
    oblk = pl.BlockSpec(blk, rev(0))
    return _run(
        body, (z, z, z, lb, pair, states, do), out_shape=(_sds((T, HG_W), MXU_DTYPE),) * 3 + (_sds((1, HG_W), F32),),
        grid=(HG_HEADS, B, nb),
        in_specs=[pl.BlockSpec(blk, rev(12)), pl.BlockSpec(blk, rev(16)), pl.BlockSpec(blk, rev(20)),
                  pl.BlockSpec((1, LANES), lambda h, b, i: (0, h)), pl.BlockSpec((tb, tb), lambda h, b, i: (0, 0)),
                  pl.BlockSpec((nc * LANES, LANES), rev(0)), oblk],
        out_specs=(oblk, oblk, oblk, pl.BlockSpec((1, LANES), lambda h, b, i: (0, h))),
        scratch_shapes=[pltpu.VMEM((LANES, LANES), F32)], name=name)


def _hgout_fwd(o, z, g, ycat, *, name):
    T = o.shape[0]
    tm = _tile(T, ROW_BLOCK, SUBLANES)

    def body(o_ref, gh_ref, g_ref, _, y_ref):
        ov = o_ref[...]
        gh = gh_ref[...]
        r = lax.rsqrt(_group_mean(ov * ov, HG_HEAD_DIM) + EPS)
        sg, _ = _sigmoid_pair(gh)
        y_ref[...] = (ov * r * g_ref[...] * (gh * sg)).astype(y_ref.dtype)

    blk = pl.BlockSpec((tm, HG_W), lambda i: (i, 0))
    return _run(body, (o, z, g, ycat), out_shape=_sds(ycat.shape, ycat.dtype), grid=(T // tm,),
                in_specs=[blk, pl.BlockSpec((tm, HG_W), lambda i: (i, 6)), pl.BlockSpec((1, HG_W), lambda i: (0, 0)),
                          pl.BlockSpec(memory_space=pl.ANY)],
                out_specs=pl.BlockSpec((tm, HG_W), lambda i: (i, ATT_W // HG_W)), name=name, aliases={3: 0})


def _hgout_bwd(o, z, g, dycat, *, name):
    T = o.shape[0]
    tm = _tile(T, ROW_BLOCK, SUBLANES)

    def body(o_ref, gh_ref, g_ref, dy_ref, do_ref, dgh_ref, dg_ref):
        i = pl.program_id(0)
        ov = o_ref[...]
        gh = gh_ref[...]
        gv = g_ref[...]
        dy = dy_ref[...]
        r = lax.rsqrt(_group_mean(ov * ov, HG_HEAD_DIM) + EPS)
        xh = ov * r
        sg, _ = _sigmoid_pair(gh)
        dn = dy * (gh * sg)
        dgh_ref[...] = (dy * xh * gv * (sg * (1.0 + gh * (1.0 - sg)))).astype(dgh_ref.dtype)
        dng = dn * gv
        do_ref[...] = r * (dng - xh * _group_mean(dng * xh, HG_HEAD_DIM))
        part = jnp.sum(dn * xh, axis=0, keepdims=True)

        @pl.when(i == 0)
        def _():
            dg_ref[...] = part

        @pl.when(i > 0)
        def _():
            dg_ref[...] += part

    blk = pl.BlockSpec((tm, HG_W), lambda i: (i, 0))
    row = pl.BlockSpec((1, HG_W), lambda i: (0, 0))
    return _run(body, (o, z, g, dycat),
                out_shape=(_sds((T, HG_W), F32), _sds(z.shape, MXU_DTYPE), _sds((1, HG_W), F32)), grid=(T // tm,),
                in_specs=[blk, pl.BlockSpec((tm, HG_W), lambda i: (i, 6)), row,
                          pl.BlockSpec((tm, HG_W), lambda i: (i, 1))],
                out_specs=(blk, pl.BlockSpec((tm, HG_W), lambda i: (i, 6)), row), name=name)


def _lower_bounds_fwd(hg_lb):
    L = hg_lb.shape[0]

    def body(x_ref, o_ref):
        rows = [x_ref[l:l + 1, :] for l in range(L)]
        mx = functools.reduce(jnp.maximum, rows)
        es = [jnp.exp(r - mx) for r in rows]
        tot = functools.reduce(jnp.add, es)
        acc = jnp.zeros_like(tot)
        for l in range(L):
            acc = acc + es[l] / tot
            o_ref[l:l + 1, :] = acc - es[0] / tot

    return _pcall(body, out_shape=_sds(hg_lb.shape, F32), name="lower_bounds_fwd")(hg_lb)


def _lower_bounds_bwd(hg_lb, dlower):
    L = hg_lb.shape[0]

    def body(x_ref, d_ref, o_ref):
        rows = [x_ref[l:l + 1, :] for l in range(L)]
        ds = [d_ref[l:l + 1, :] for l in range(L)]
        mx = functools.reduce(jnp.maximum, rows)
        es = [jnp.exp(r - mx) for r in rows]
        tot = functools.reduce(jnp.add, es)
        ps = [e / tot for e in es]
        dps = []
        for j in range(L):
            t = functools.reduce(jnp.add, ds[j:])
            if j == 0:
                t = t - functools.reduce(jnp.add, ds)
            dps.append(t)
        inner = functools.reduce(jnp.add, [p * dp for p, dp in zip(ps, dps)])
        for j in range(L):
            o_ref[j:j + 1, :] = ps[j] * (dps[j] - inner)

    return _pcall(body, out_shape=_sds(hg_lb.shape, F32), name="lower_bounds_bwd")(hg_lb, dlower)


def _memattn_fwd(qm, km, kv, *, B, S, NM, name):
    T, D = qm.shape
    E = D // MEM_HEADS
    tq = _tile(S, ROW_BLOCK, SUBLANES)
    nq = S // tq
    scale = 1.0 / math.sqrt(E)

    def body(q_ref, k_ref, v_ref, o_ref):
        for h in range(MEM_HEADS):
            cs = slice(h * E, (h + 1) * E)
            s = lax.dot_general(q_ref[:, cs].astype(MXU_DTYPE), k_ref[:, cs].astype(MXU_DTYPE),
                                (((1,), (1,)), ((), ())), preferred_element_type=F32) * scale
            e = jnp.exp(s - jnp.max(s, axis=1, keepdims=True))
            pr = e / jnp.sum(e, axis=1, keepdims=True)
            o_ref[:, cs] = jnp.dot(pr.astype(MXU_DTYPE), v_ref[:, cs].astype(MXU_DTYPE),
                                   preferred_element_type=F32).astype(o_ref.dtype)

    return _run(
        body, (qm, km, kv), out_shape=_sds((T, D), MXU_DTYPE), grid=(B, nq),
        in_specs=[pl.BlockSpec((tq, D), lambda b, i: (b * nq + i, 0)), pl.BlockSpec((NM, D), lambda b, i: (b, 0)),
                  pl.BlockSpec((NM, D), lambda b, i: (b, 1))],
        out_specs=pl.BlockSpec((tq, D), lambda b, i: (b * nq + i, 0)), name=name)


def _memattn_bwd(qm, km, kv, do, *, B, S, NM, name):
    T, D = qm.shape
    E = D // MEM_HEADS
    tq = _tile(S, ROW_BLOCK, SUBLANES)
    nq = S // tq
    scale = 1.0 / math.sqrt(E)

    def body(q_ref, k_ref, v_ref, do_ref, dq_ref, dk_ref, dv_ref):
        i = pl.program_id(1)

        @pl.when(i == 0)
        def _():
            dk_ref[...] = jnp.zeros_like(dk_ref)
            dv_ref[...] = jnp.zeros_like(dv_ref)

        for h in range(MEM_HEADS):
            cs = slice(h * E, (h + 1) * E)
            qh = q_ref[:, cs].astype(MXU_DTYPE)
            kh = k_ref[:, cs].astype(MXU_DTYPE)
            vh = v_ref[:, cs].astype(MXU_DTYPE)
            doh = do_ref[:, cs].astype(MXU_DTYPE)
            s = lax.dot_general(qh, kh, (((1,), (1,)), ((), ())), preferred_element_type=F32) * scale
            e = jnp.exp(s - jnp.max(s, axis=1, keepdims=True))
            pr = e / jnp.sum(e, axis=1, keepdims=True)
            dp = lax.dot_general(doh, vh, (((1,), (1,)), ((), ())), preferred_element_type=F32)
            ds = (pr * (dp - jnp.sum(dp * pr, axis=1, keepdims=True))).astype(MXU_DTYPE)
            dq_ref[:, cs] = jnp.dot(ds, kh, preferred_element_type=F32) * scale
            dk_ref[:, cs] += lax.dot_general(ds, qh, (((0,), (0,)), ((), ())), preferred_element_type=F32) * scale
            dv_ref[:, cs] += lax.dot_general(pr.astype(MXU_DTYPE), doh, (((0,), (0,)), ((), ())),
                                             preferred_element_type=F32)

    qblk = pl.BlockSpec((tq, D), lambda b, i: (b * nq + i, 0))
    mblk = pl.BlockSpec((NM, D), lambda b, i: (b, 0))
    return _run(
        body, (qm, km, kv, do), out_shape=(_sds((T, D), F32), _sds((B * NM, D), F32), _sds((B * NM, D), F32)),
        grid=(B, nq), in_specs=[qblk, mblk, pl.BlockSpec((NM, D), lambda b, i: (b, 1)), qblk],
        out_specs=(qblk, mblk, mblk), name=name)


def _loss_head(y, tgt):
    T, D = y.shape
    tm = _tile(T, ROW_BLOCK, SUBLANES)

    def body(y_ref, t_ref, dy_ref, dyl_ref, l_ref):
        i = pl.program_id(0)
        err = y_ref[...] - t_ref[...]
        dy_ref[...] = err * (1.0 / D)
        dyl_ref[...] = (err * (1.0 / D)).astype(MXU_DTYPE)
        part = jnp.sum(jnp.sum(err * err, axis=1, keepdims=True), axis=0, keepdims=True) * (0.5 / D)

        @pl.when(i == 0)
        def _():
            l_ref[...] = part

        @pl.when(i > 0)
        def _():
            l_ref[...] += part

    blk = pl.BlockSpec((tm, D), lambda i: (i, 0))
    return _run(body, (y, tgt), out_shape=(_sds((T, D), F32), _sds((T, D), MXU_DTYPE), _sds((1, 1), F32)),
                grid=(T // tm,), in_specs=[blk, blk], out_specs=(blk, blk, pl.BlockSpec((1, 1), lambda i: (0, 0))),
                name="loss_head")


def _adamw(w, g, m, v):
    m = ADAM_B1 * m + (1.0 - ADAM_B1) * g
    v = ADAM_B2 * v + (1.0 - ADAM_B2) * jnp.square(g)
    m_hat = m / (1.0 - ADAM_B1 ** ADAM_STEP)
    v_hat = v / (1.0 - ADAM_B2 ** ADAM_STEP)
    delta = -ADAM_LR * (m_hat / (jnp.sqrt(v_hat) + ADAM_EPS) + ADAM_WD * w)
    return delta, m, v


def _sum_adamw(recv, w, m, v, layer, so_far=None):
    L, r, c = w.shape
    tr = _tile(r, ADAM_BLOCK, SUBLANES)

    def body(r_ref, w_ref, m_ref, v_ref, *rest):
        g_ref, d_ref, nm_ref, nv_ref = rest[-4:]
        g = r_ref[0].astype(F32)
        for j in range(1, N_DEV):
            g = g + r_ref[j].astype(F32)
        g_ref[...] = g
        d_ref[...], nm_ref[...], nv_ref[...] = _adamw(w_ref[...], g, m_ref[...], v_ref[...])

    blk = pl.BlockSpec((None, tr, c), lambda i: (layer, i, 0))
    hbm = pl.BlockSpec(memory_space=pl.ANY)
    args, in_specs, aliases = (recv, w, m, v), [pl.BlockSpec((N_DEV, tr, c), lambda i: (0, i, 0)), blk, blk, blk], {}
    if so_far is not None:
        args, in_specs, aliases = args + tuple(so_far), in_specs + [hbm] * 4, {4 + k: k for k in range(4)}
    return _run(body, args, out_shape=(_sds((L, r, c), F32),) * 4, grid=(r // tr,), in_specs=in_specs,
                out_specs=(blk,) * 4, name="grad_sum_adamw", aliases=aliases)


def _small_sum_adamw(gp, w, m, v):
    NR, C = gp.shape
    MESH = pl.DeviceIdType.MESH

    def body(gp_ref, w_ref, m_ref, v_ref, g_ref, d_ref, nm_ref, nv_ref, gath, send_sems, recv_sems):
        x, y, c = _mesh_pos()
        me = 4 * x + 2 * y + c
        gath[me] = gp_ref[...]
        copies = []
        for k in range(1, N_DEV):
            peer = (_flip(x, k & 4), _flip(y, k & 2), _flip(c, k & 1))
            cp = pltpu.make_async_remote_copy(src_ref=gp_ref, dst_ref=gath.at[me], send_sem=send_sems.at[k - 1],
                                              recv_sem=recv_sems.at[k - 1], device_id=peer, device_id_type=MESH)
            cp.start()
            copies.append(cp)
        for cp in copies:
            cp.wait()
        g = gath[0]
        for j in range(1, N_DEV):
            g = g + gath[j]
        g_ref[...] = g
        d_ref[...], nm_ref[...], nv_ref[...] = _adamw(w_ref[...], g, m_ref[...], v_ref[...])

    vm = pl.BlockSpec(memory_space=pltpu.VMEM)
    return _pcall(body, out_shape=(_sds((NR, C), F32),) * 4, in_specs=[vm] * 4, out_specs=(vm,) * 4,
                  scratch_shapes=[pltpu.VMEM((N_DEV, NR, C), F32), pltpu.SemaphoreType.DMA((7,)),
                                  pltpu.SemaphoreType.DMA((7,))], name="small_sum_adamw")(gp, w, m, v)


def _gathered(gath, name):
    _, r, c = gath.shape
    return gath.transpose(1, 0, 2).reshape(r, N_DEV * c) if name in COL_SHARDED else gath.reshape(N_DEV * r, c)


def _scatter_blocks(grad, name):
    rows, cols = grad.shape
    if name in COL_SHARDED:
        return grad.reshape(rows, N_DEV, cols // N_DEV).transpose(1, 0, 2)
    return grad.reshape(N_DEV, rows // N_DEV, cols)


def _pack_small(d, prefix, loss=None):
    parts = []
    for n in SMALL:
        a = d[prefix + n].reshape(-1).astype(F32)
        pad = (-a.shape[0]) % LANES
        parts.append(jnp.pad(a, (0, pad)) if pad else a)
    tail = jnp.zeros((LANES,), F32)
    if loss is not None:
        tail = tail.at[0].set(loss)
    flat = jnp.concatenate(parts + [tail])
    pad = (-flat.shape[0]) % (8 * LANES)
    return jnp.pad(flat, (0, pad)).reshape(-1, LANES)


def _unpack_small(packed, shapes):
    flat = packed.reshape(-1)
    out, off = {}, 0
    for n in SMALL:
        size = math.prod(shapes[n])
        out[n] = flat[off:off + size].reshape(shapes[n])
        off += size + (-size) % LANES
    return out, flat[off]


def _row(a):
    return a.reshape(1, -1).astype(F32)


def _layer_fwd(x, memf, w_in, shards, next_in, sp, lb, tab, dims):
    B, S, NM = dims
    D = x.shape[1]
    if w_in is None:
        h1, gath = _norm_fwd(x, sp["norm1_g"], E=D, W=D, out_dtype=MXU_DTYPE, name="norm1_fwd",
                             carry=("gather", [shards["w_in"].astype(MXU_DTYPE)]))
        w_in = _gathered(gath[0], "w_in")
    else:
        h1 = _norm_fwd(x, sp["norm1_g"], E=D, W=D, out_dtype=MXU_DTYPE, name="norm1_fwd")
    W = {"w_in": w_in}
    z, gath = _matmul(h1, w_in, carry=("gather", [shards[n].astype(MXU_DTYPE) for n in GATHER_MIX]), name="in_proj")
    W.update({n: _gathered(t, n) for n, t in zip(GATHER_MIX, gath)})
    qn = _norm_fwd(z, sp["attn_qn_g"], E=ATT_HEAD_DIM, W=ATT_W, cb=0, out_dtype=MXU_DTYPE, name="qnorm_fwd")
    kn = _norm_fwd(z, sp["attn_kn_g"], E=ATT_HEAD_DIM, W=ATT_W, cb=1, out_dtype=MXU_DTYPE, name="knorm_fwd")
    (ya, lse, ycat), gath = _attn_fwd(tab["attn"], qn, kn, z, B=B, S=S, name="dilated_attn_fwd",
                                carry=("gather", [shards[n].astype(MXU_DTYPE) for n in GATHER_FF]))
    W.update({n: _gathered(t, n) for n, t in zip(GATHER_FF, gath)})
    o, states = _hgrn_fwd(z, lb, tab["hgrn"], B=B, S=S, name="hgrn_fwd")
    ycat = _hgout_fwd(o, z, sp["hg_onorm_g"], ycat, name="hgrn_out_fwd")
    x1 = _matmul(ycat, W["w_out"], epi="add", extra=x, name="out_proj")
    h2 = _norm_fwd(x1, sp["norm2_g"], E=D, W=D, out_dtype=MXU_DTYPE, name="norm2_fwd")
    qmp = _matmul(h2, W["w_mq"], name="mq_proj")
    qm = _norm_fwd(qmp, sp["mq_norm_g"], E=D // MEM_HEADS, W=D, out_dtype=MXU_DTYPE, name="mqnorm_fwd")
    mn = _norm_fwd(memf, sp["mem_norm_g"], E=D, W=D, out_dtype=MXU_DTYPE, name="memnorm_fwd")
    kv = _matmul(mn, W["w_mkv"], name="mkv_proj")
    km = _norm_fwd(kv, sp["mk_norm_g"], E=D // MEM_HEADS, W=D, cb=0, out_dtype=MXU_DTYPE, name="mknorm_fwd")
    om = _memattn_fwd(qm, km, kv, B=B, S=S, NM=NM, name="mem_attn_fwd")
    x2 = _matmul(om, W["w_mo"], epi="add", extra=x1, name="mo_proj")
    h3 = _norm_fwd(x2, sp["norm3_g"], E=D, W=D, out_dtype=MXU_DTYPE, name="norm3_fwd")
    if next_in is None:
        u, w_in_next = _matmul(h3, W["w_ff1"], epi="relu2", out_dtype=MXU_DTYPE, name="ff1"), None
    else:
        u, gath = _matmul(h3, W["w_ff1"], epi="relu2", out_dtype=MXU_DTYPE,
                          carry=("gather", [next_in.astype(MXU_DTYPE)]), name="ff1")
        w_in_next = _gathered(gath[0], "w_in")
    x3 = _matmul(u, W["w_ff2"], epi="add", extra=x2, name="ff2")
    saved = dict(x=x, h1=h1, z=z, qn=qn, kn=kn, ya=ya, lse=lse, o=o, states=states, ycat=ycat, x1=x1, h2=h2,
                 qmp=qmp, qm=qm, mn=mn, kv=kv, km=km, om=om, x2=x2, h3=h3, u=u)
    return x3, saved, W, w_in_next


def _layer_bwd(dx3, dx3_low, s, memf, W, sp, lb, tab, dims):
    B, S, NM = dims
    D = dx3.shape[1]
    E_M = D // MEM_HEADS
    gw, gs, recv = {}, {}, {}

    def blocks(names):
        return ("scatter", [_scatter_blocks(gw[n], n) for n in names])

    def received(names, got):
        recv.update(dict(zip(names, got)))

    da = _matmul(dx3_low, W["w_ff2"], tb=True, epi="relu2grad", extra=s["u"], out_dtype=MXU_DTYPE, name="ff2_dgrad")
    gw["w_ff2"] = _matmul(s["u"], dx3_low, ta=True, out_dtype=MXU_DTYPE, name="ff2_wgrad")
    dh3, got = _matmul(da, W["w_ff1"], tb=True, out_dtype=MXU_DTYPE, carry=blocks(SCATTER_A), name="ff1_dgrad")
    received(SCATTER_A, got)
    gw["w_ff1"] = _matmul(s["h3"], da, ta=True, out_dtype=MXU_DTYPE, name="ff1_wgrad")
    dx2, dx2_low, gs["norm3_g"] = _norm_bwd(s["x2"], sp["norm3_g"], dh3, E=D, W=D, res=dx3, low=True, name="norm3_bwd")
    dom = _matmul(dx2_low, W["w_mo"], tb=True, out_dtype=MXU_DTYPE, name="mo_dgrad")
    gw["w_mo"] = _matmul(s["om"], dx2_low, ta=True, out_dtype=MXU_DTYPE, name="mo_wgrad")
    dqm, dkm, dvm = _memattn_bwd(s["qm"], s["km"], s["kv"], dom, B=B, S=S, NM=NM, name="mem_attn_bwd")
    dqmp, gs["mq_norm_g"] = _norm_bwd(s["qmp"], sp["mq_norm_g"], dqm, E=E_M, W=D, fold=E_M, out_dtype=MXU_DTYPE,
                                      name="mqnorm_bwd")
    dkmp, gs["mk_norm_g"] = _norm_bwd(s["kv"], sp["mk_norm_g"], dkm, E=E_M, W=D, cb=0, fold=E_M,
                                      out_dtype=MXU_DTYPE, name="mknorm_bwd")
    dkv = jnp.concatenate([dkmp, dvm.astype(MXU_DTYPE)], axis=1)
    dh2 = _matmul(dqmp, W["w_mq"], tb=True, out_dtype=MXU_DTYPE, name="mq_dgrad")
    gw["w_mq"] = _matmul(s["h2"], dqmp, ta=True, out_dtype=MXU_DTYPE, name="mq_wgrad")
    dmn = _matmul(dkv, W["w_mkv"], tb=True, out_dtype=MXU_DTYPE, name="mkv_dgrad")
    gw["w_mkv"] = _matmul(s["mn"], dkv, ta=True, out_dtype=MXU_DTYPE, name="mkv_wgrad")
    _, gs["mem_norm_g"] = _norm_bwd(memf, sp["mem_norm_g"], dmn, E=D, W=D, name="memnorm_bwd")
    dx1, dx1_low, gs["norm2_g"] = _norm_bwd(s["x1"], sp["norm2_g"], dh2, E=D, W=D, res=dx2, low=True, name="norm2_bwd")
    dycat = _matmul(dx1_low, W["w_out"], tb=True, out_dtype=MXU_DTYPE, name="out_dgrad")
    gw["w_out"] = _matmul(s["ycat"], dx1_low, ta=True, out_dtype=MXU_DTYPE, name="out_wgrad")
    do_hg, dz, gs["hg_onorm_g"] = _hgout_bwd(s["o"], s["z"], sp["hg_onorm_g"], dycat, name="hgrn_out_bwd")
    dqh, dfh, dih, dlb = _hgrn_bwd(s["z"], lb, tab["hgrn"], s["states"], do_hg, B=B, S=S, name="hgrn_bwd")
    (dqn, dkn, dz), got = _attn_bwd(tab["attn"], s["qn"], s["kn"], s["z"], s["ya"], s["lse"], dycat, dz, B=B, S=S,
                                     carry=blocks(SCATTER_B), name="dilated_attn_bwd")
    received(SCATTER_B, got)
    dz, gs["attn_qn_g"] = _norm_bwd(s["z"], sp["attn_qn_g"], dqn, E=ATT_HEAD_DIM, W=ATT_W, cb=0, fold=ATT_HEAD_DIM,
                                    into=dz, name="qnorm_bwd")
    dz, gs["attn_kn_g"] = _norm_bwd(s["z"], sp["attn_kn_g"], dkn, E=ATT_HEAD_DIM, W=ATT_W, cb=1, fold=ATT_HEAD_DIM,
                                    into=dz, name="knorm_bwd")
    for block, t in ((3, dqh), (4, dfh), (5, dih)):
        dz = lax.dynamic_update_slice(dz, t, (0, block * HG_W))
    gw["w_in"] = _matmul(s["h1"], dz, ta=True, out_dtype=MXU_DTYPE, name="in_wgrad")
    dh1, got = _matmul(dz, W["w_in"], tb=True, out_dtype=MXU_DTYPE, carry=blocks(SCATTER_C), name="in_dgrad")
    received(SCATTER_C, got)
    dx0, dx0_low, gs["norm1_g"] = _norm_bwd(s["x"], sp["norm1_g"], dh1, E=D, W=D, res=dx1, low=True, name="norm1_bwd")
    gs["attn_qn_g"] = gs["attn_qn_g"][:, :ATT_HEAD_DIM]
    gs["attn_kn_g"] = gs["attn_kn_g"][:, :ATT_HEAD_DIM]
    return dx0, dx0_low, recv, gs, dlb


def kernel(x, mem, norm1_g, w_in, attn_qn_g, attn_kn_g, hg_lb, hg_onorm_g, w_out, norm2_g, mem_norm_g, w_mq, w_mkv, mq_norm_g, mk_norm_g, w_mo, norm3_g, w_ff1, w_ff2, loss_target, m_norm1_g, m_w_in, m_attn_qn_g, m_attn_kn_g, m_hg_lb, m_hg_onorm_g, m_w_out, m_norm2_g, m_mem_norm_g, m_w_mq, m_w_mkv, m_mq_norm_g, m_mk_norm_g, m_w_mo, m_norm3_g, m_w_ff1, m_w_ff2, v_norm1_g, v_w_in, v_attn_qn_g, v_attn_kn_g, v_hg_lb, v_hg_onorm_g, v_w_out, v_norm2_g, v_mem_norm_g, v_w_mq, v_w_mkv, v_mq_norm_g, v_mk_norm_g, v_w_mo, v_norm3_g, v_w_ff1, v_w_ff2):
    given = dict(norm1_g=norm1_g, w_in=w_in, attn_qn_g=attn_qn_g, attn_kn_g=attn_kn_g, hg_lb=hg_lb, hg_onorm_g=hg_onorm_g, w_out=w_out, norm2_g=norm2_g, mem_norm_g=mem_norm_g, w_mq=w_mq, w_mkv=w_mkv, mq_norm_g=mq_norm_g, mk_norm_g=mk_norm_g, w_mo=w_mo, norm3_g=norm3_g, w_ff1=w_ff1, w_ff2=w_ff2, m_norm1_g=m_norm1_g, m_w_in=m_w_in, m_attn_qn_g=m_attn_qn_g, m_attn_kn_g=m_attn_kn_g, m_hg_lb=m_hg_lb, m_hg_onorm_g=m_hg_onorm_g, m_w_out=m_w_out, m_norm2_g=m_norm2_g, m_mem_norm_g=m_mem_norm_g, m_w_mq=m_w_mq, m_w_mkv=m_w_mkv, m_mq_norm_g=m_mq_norm_g, m_mk_norm_g=m_mk_norm_g, m_w_mo=m_w_mo, m_norm3_g=m_norm3_g, m_w_ff1=m_w_ff1, m_w_ff2=m_w_ff2, v_norm1_g=v_norm1_g, v_w_in=v_w_in, v_attn_qn_g=v_attn_qn_g, v_attn_kn_g=v_attn_kn_g, v_hg_lb=v_hg_lb, v_hg_onorm_g=v_hg_onorm_g, v_w_out=v_w_out, v_norm2_g=v_norm2_g, v_mem_norm_g=v_mem_norm_g, v_w_mq=v_w_mq, v_w_mkv=v_w_mkv, v_mq_norm_g=v_mq_norm_g, v_mk_norm_g=v_mk_norm_g, v_w_mo=v_w_mo, v_norm3_g=v_norm3_g, v_w_ff1=v_w_ff1, v_w_ff2=v_w_ff2)
    B, S, D = x.shape
    NM = mem.shape[1]
    L = w_in.shape[0]
    dims = (B, S, NM)
    small_shapes = {n: given[n].shape for n in SMALL}

    def shards(prefix, l):
        return {n: given[prefix + n][l] for n in BIG}

    lower = _lower_bounds_fwd(hg_lb)
    tab = {"attn": _attn_tables(S, _tile(S, ATT_Q_BLOCK, SUBLANES)), "hgrn": _hgrn_pair_table(_tile(S, HG_BLOCK, HG_CHUNK))}
    xf = x.reshape(B * S, D)
    memf = mem.reshape(B * NM, D)

    def small_params(l):
        sp = {"norm1_g": _row(norm1_g[l]), "norm2_g": _row(norm2_g[l]), "norm3_g": _row(norm3_g[l]),
              "mem_norm_g": _row(mem_norm_g[l]), "hg_onorm_g": _row(hg_onorm_g[l]),
              "attn_qn_g": _row(jnp.tile(attn_qn_g[l], ATT_HEADS)), "attn_kn_g": _row(jnp.tile(attn_kn_g[l], ATT_HEADS)),
              "mq_norm_g": _row(jnp.tile(mq_norm_g[l], MEM_HEADS)), "mk_norm_g": _row(jnp.tile(mk_norm_g[l], MEM_HEADS))}
        return sp, _row(lower[l])

    w_in_full = None
    saved, weights = [], []
    h = xf
    for l in range(L):
        sp, lb = small_params(l)
        next_in = given["w_in"][l + 1] if l + 1 < L else None
        h, s, W, w_in_full = _layer_fwd(h, memf, w_in_full, shards("", l), next_in, sp, lb, tab, dims)
        saved.append(s)
        weights.append(W)

    dh, dh_low, loss_part = _loss_head(h, loss_target.reshape(B * S, D))

    recv_layers, gs_layers, dlb_layers = [None] * L, [None] * L, [None] * L
    for l in range(L - 1, -1, -1):
        sp, lb = small_params(l)
        dh, dh_low, recv_layers[l], gs_layers[l], dlb_layers[l] = _layer_bwd(dh, dh_low, saved[l], memf, weights[l],
                                                                              sp, lb, tab, dims)
    grad_x = dh.reshape(B, S, D)

    big_out = {}
    for n in BIG:
        outs = None
        for l in range(L):
            outs = _sum_adamw(recv_layers[l][n], given[n], given["m_" + n], given["v_" + n], l, outs)
        big_out[n] = outs

    gs = {n: jnp.stack([gs_layers[l][n].reshape(small_shapes[n][1:]) for l in range(L)]) for n in SMALL if n != "hg_lb"}
    gs["hg_lb"] = _lower_bounds_bwd(hg_lb, jnp.concatenate(dlb_layers, axis=0))
    packed = _small_sum_adamw(_pack_small(gs, "", loss_part[0, 0]), _pack_small(given, ""), _pack_small(given, "m_"),
                              _pack_small(given, "v_"))
    small_out, loss = [], None
    for t in packed:
        d, tail = _unpack_small(t, small_shapes)
        small_out.append(d)
        loss = tail if loss is None else loss

    outs = [loss, grad_x]
    for k in range(4):
        outs += [small_out[k][n] if n in small_shapes else big_out[n][k] for n in WEIGHTS]
    return tuple(outs)
```

```python
import functools
import math

import jax
import jax.numpy as jnp
from jax import lax
from jax.experimental import pallas as pl
from jax.experimental.pallas import tpu as pltpu

F32 = jnp.float32
MXU_DTYPE = jnp.bfloat16
VMEM_LIMIT = 48 * 1024 * 1024

N_DEV = 8
EPS = 1e-6
NEG = -1e30
F_MIN = 1e-12
ATT_HEADS = 8
ATT_HEAD_DIM = 64
ATT_W = ATT_HEADS * ATT_HEAD_DIM
DILATIONS = (1, 4, 16)
DIL_STEPS = 128
ATT_Q_BLOCK = 256
ATT_KEY_EXTENT = 512
HG_HEADS = 4
HG_HEAD_DIM = 128
HG_W = HG_HEADS * HG_HEAD_DIM
HG_CHUNK = 32
HG_SUB = 8
MEM_HEADS = 4
LANES = 128
SUBLANES = 8
ROW_BLOCK = 1024
MM_BLOCK = 1024
MM_K_BLOCK = 2048
HG_BLOCK = 256
ADAM_BLOCK = 256

ADAM_LR = 0.001
ADAM_B1 = 0.9
ADAM_B2 = 0.999
ADAM_EPS = 1e-08
ADAM_WD = 0.01
ADAM_STEP = 10

BIG = ("w_in", "w_out", "w_mq", "w_mkv", "w_mo", "w_ff1", "w_ff2")
COL_SHARDED = ("w_in", "w_mkv", "w_ff1")
SMALL = ("norm1_g", "attn_qn_g", "attn_kn_g", "hg_lb", "hg_onorm_g", "norm2_g", "mem_norm_g",
         "mq_norm_g", "mk_norm_g", "norm3_g")
WEIGHTS = ("norm1_g", "w_in", "attn_qn_g", "attn_kn_g", "hg_lb", "hg_onorm_g", "w_out", "norm2_g",
           "mem_norm_g", "w_mq", "w_mkv", "mq_norm_g", "mk_norm_g", "w_mo", "norm3_g", "w_ff1", "w_ff2")
GATHER_MIX = ("w_out", "w_mq", "w_mkv", "w_mo")
GATHER_FF = ("w_ff1", "w_ff2")
SCATTER_A = ("w_ff2",)
SCATTER_B = ("w_ff1", "w_mo", "w_mq", "w_mkv", "w_out")
SCATTER_C = ("w_in",)


def _pcall(body, **kw):
    return pl.pallas_call(body, **kw)


def _params():
    return pltpu.CompilerParams(vmem_limit_bytes=VMEM_LIMIT)


def _tile(n, pref, mult):
    t = (min(n, pref) // mult) * mult
    while t >= mult:
        if n % t == 0:
            return t
        t -= mult
    return n


def _sds(shape, dtype):
    return jax.ShapeDtypeStruct(shape, dtype)


def _mesh_pos():
    return lax.axis_index("x"), lax.axis_index("y"), lax.axis_index("c")


def _flip(v, bit):
    return 1 - v if bit else v


def _gather_hooks(x_refs, out_refs, send_sems, recv_sems, local_sems):
    MESH = pl.DeviceIdType.MESH
    n = len(x_refs)
    x, y, c = _mesh_pos()
    me, sibling = (x, y, c), (x, y, 1 - c)
    chips = [(1 - x, y), (x, 1 - y), (1 - x, 1 - y)]

    def rows(a, px, py, pc):
        return out_refs[a].at[4 * px + 2 * py + pc]

    def copy(a, k, block, to, own=False):
        return pltpu.make_async_remote_copy(
            src_ref=x_refs[a] if own else rows(a, *block), dst_ref=rows(a, *block), send_sem=send_sems.at[n * k + a],
            recv_sem=recv_sems.at[n * k + a], device_id=to, device_id_type=MESH)

    mine = [pltpu.make_async_copy(x_refs[a], rows(a, *me), local_sems.at[a]) for a in range(n)]
    first = [copy(a, 0, me, sibling, own=True) for a in range(n)]
    first += [copy(a, 1 + j, me, (*chip, c), own=True) for j, chip in enumerate(chips) for a in range(n)]
    passed = [[copy(a, 4 + j, (*chip, c), sibling) for a in range(n)] for j, chip in enumerate(chips)]

    def start():
        for cp in mine + first:
            cp.start()

    def forward():
        for j, chip in enumerate(chips):
            for a in range(n):
                copy(a, 1 + j, (*chip, c), me).wait_recv()
                passed[j][a].start()

    def finish():
        for a in range(n):
            copy(a, 0, sibling, me).wait_recv()
        for j, chip in enumerate(chips):
            for a in range(n):
                copy(a, 4 + j, (*chip, 1 - c), me).wait_recv()
        for cp in first + [cp for group in passed for cp in group]:
            cp.wait_send()
        for cp in mine:
            cp.wait()

    return start, forward, finish


def _scatter_hooks(s_refs, r_refs, send_sems, recv_sems, local_sems):
    MESH = pl.DeviceIdType.MESH
    n = len(s_refs)
    x, y, c = _mesh_pos()
    me = 4 * x + 2 * y + c
    mine = [pltpu.make_async_copy(s_refs[a].at[me], r_refs[a].at[me], local_sems.at[a]) for a in range(n)]
    copies = []
    for m in range(1, N_DEV):
        px, py, pc = _flip(x, m & 4), _flip(y, m & 2), _flip(c, m & 1)
        for a in range(n):
            copies.append(pltpu.make_async_remote_copy(
                src_ref=s_refs[a].at[4 * px + 2 * py + pc], dst_ref=r_refs[a].at[me],
                send_sem=send_sems.at[n * (m - 1) + a], recv_sem=recv_sems.at[n * (m - 1) + a],
                device_id=(px, py, pc), device_id_type=MESH))

    def start():
        for cp in mine + copies:
            cp.start()

    def finish():
        for cp in copies:
            cp.wait()
        for cp in mine:
            cp.wait()

    return start, None, finish


def _exchange_sems(n):
    return [pltpu.SemaphoreType.DMA(((N_DEV - 1) * n,)), pltpu.SemaphoreType.DMA(((N_DEV - 1) * n,)),
            pltpu.SemaphoreType.DMA((n,))]


def _exchange_out(kind, xs):
    return _sds((N_DEV,) + xs.shape, xs.dtype) if kind == "gather" else _sds(xs.shape, xs.dtype)


def _run(body, args, *, out_shape, grid, in_specs, out_specs, scratch_shapes=(), name, carry=None, aliases=None):
    single = not isinstance(out_shape, (tuple, list))
    outs = (out_shape,) if single else tuple(out_shape)
    ospecs = (out_specs,) if single else tuple(out_specs)
    aliases = dict(aliases or {})
    if carry is None:
        res = _pcall(body, out_shape=outs, grid=grid, in_specs=list(in_specs), out_specs=ospecs,
                     scratch_shapes=list(scratch_shapes), input_output_aliases=aliases, compiler_params=_params(),
                     name=name)(*args)
        return res[0] if single else tuple(res)
    kind, xs = carry
    n_in, n_out, n_scr, n_x = len(args), len(outs), len(scratch_shapes), len(xs)
    total = math.prod(grid)

    def wrapped(*refs):
        refs = list(refs)
        ins, x_refs = refs[:n_in], refs[n_in:n_in + n_x]
        o, xo_refs = refs[n_in + n_x:n_in + n_x + n_out], refs[n_in + n_x + n_out:n_in + 2 * n_x + n_out]
        scr = refs[n_in + 2 * n_x + n_out:n_in + 2 * n_x + n_out + n_scr]
        sems = refs[n_in + 2 * n_x + n_out + n_scr:]
        step = pl.program_id(0)
        for ax in range(1, len(grid)):
            step = step * grid[ax] + pl.program_id(ax)
        start, forward, finish = (_gather_hooks if kind == "gather" else _scatter_hooks)(x_refs, xo_refs, *sems)
        pl.when(step == 0)(start)
        body(*ins, *o, *scr)
        if forward is not None:
            pl.when(step == (3 * total) // 4)(forward)
        pl.when(step == total - 1)(finish)

    hbm = pl.BlockSpec(memory_space=pl.ANY)
    res = _pcall(wrapped, out_shape=outs + tuple(_exchange_out(kind, t) for t in xs), grid=grid,
                 in_specs=list(in_specs) + [hbm] * n_x, out_specs=ospecs + (hbm,) * n_x,
                 scratch_shapes=list(scratch_shapes) + _exchange_sems(n_x), input_output_aliases=aliases,
                 compiler_params=_params(), name=name)(*args, *xs)
    main = tuple(res[:n_out])
    return (main[0] if single else main), list(res[n_out:])


def _matmul(a, b, *, ta=False, tb=False, epi=None, extra=None, out_dtype=F32, carry=None, name):
    M, K = (a.shape[1], a.shape[0]) if ta else a.shape
    N = b.shape[0] if tb else b.shape[1]
    tm = _tile(M, MM_BLOCK, SUBLANES if not ta else LANES)
    tn = _tile(N, MM_BLOCK, LANES)
    tk = _tile(K, MM_K_BLOCK // 2 if ta else MM_K_BLOCK, LANES if not ta else SUBLANES)
    nk = K // tk
    dims = (((0 if ta else 1,), (1 if tb else 0,)), ((), ()))

    def body(a_ref, b_ref, *rest):
        e_ref = rest[0] if extra is not None else None
        o_ref = rest[1] if extra is not None else rest[0]

        def finish(r):
            if epi == "add":
                r = r + e_ref[...]
            elif epi == "relu2":
                r = jnp.square(jnp.maximum(r, 0.0))
            elif epi == "relu2grad":
                r = r * (2.0 * jnp.sqrt(e_ref[...].astype(F32)))
            o_ref[...] = r.astype(out_dtype)

        part = lax.dot_general(a_ref[...].astype(MXU_DTYPE), b_ref[...].astype(MXU_DTYPE), dims,
                               preferred_element_type=F32)
        if nk == 1:
            finish(part)
        else:
            acc_ref = rest[-1]
            k = pl.program_id(2)

            @pl.when(k == 0)
            def _():
                acc_ref[...] = part

            @pl.when(k > 0)
            def _():
                acc_ref[...] += part

            @pl.when(k == nk - 1)
            def _():
                finish(acc_ref[...])

    a_spec = pl.BlockSpec((tk, tm), lambda i, j, k: (k, i)) if ta else pl.BlockSpec((tm, tk), lambda i, j, k: (i, k))
    b_spec = pl.BlockSpec((tn, tk), lambda i, j, k: (j, k)) if tb else pl.BlockSpec((tk, tn), lambda i, j, k: (k, j))
    o_spec = pl.BlockSpec((tm, tn), lambda i, j, k: (i, j))
    in_specs = [a_spec, b_spec] + ([o_spec] if extra is not None else [])
    args = (a, b) + ((extra,) if extra is not None else ())
    return _run(body, args, out_shape=_sds((M, N), out_dtype), grid=(M // tm, N // tn, nk), in_specs=in_specs,
                out_specs=o_spec, scratch_shapes=[pltpu.VMEM((tm, tn), F32)] if nk > 1 else [], name=name, carry=carry)


def _group_mean(v, E):
    rows, W = v.shape
    if E == W:
        return jnp.mean(v, axis=-1, keepdims=True)
    pieces = []
    if E % LANES == 0:
        for g0 in range(0, W, E):
            m = jnp.mean(v[:, g0:g0 + E], axis=-1, keepdims=True)
            pieces.append(jnp.broadcast_to(m, (rows, E)))
    else:
        lane = lax.broadcasted_iota(jnp.int32, (rows, LANES), 1)
        for c0 in range(0, W, LANES):
            vc = v[:, c0:c0 + LANES]
            acc = jnp.zeros((rows, LANES), F32)
            for s0 in range(0, LANES, E):
                msk = (lane >= s0) & (lane < s0 + E)
                m = jnp.sum(jnp.where(msk, vc, 0.0), axis=-1, keepdims=True) * (1.0 / E)
                acc = jnp.where(msk, m, acc)
            pieces.append(acc)
    return jnp.concatenate(pieces, axis=-1)


def _fold_groups(t, E):
    W = t.shape[1]
    step = max(E, LANES)
    acc = t[:, 0:step]
    for c0 in range(step, W, step):
        acc = acc + t[:, c0:c0 + step]
    sh = LANES // 2
    while sh >= E:
        acc = acc + pltpu.roll(acc, sh, 1)
        sh //= 2
    return acc


def _norm_fwd(x, g, *, E, W, cb=0, out_dtype, carry=None, name):
    M = x.shape[0]
    tm = _tile(M, ROW_BLOCK, SUBLANES)

    def body(x_ref, g_ref, o_ref):
        xv = x_ref[...]
        r = lax.rsqrt(_group_mean(xv * xv, E) + EPS)
        o_ref[...] = (xv * r * g_ref[...]).astype(out_dtype)

    return _run(body, (x, g), out_shape=_sds((M, W), out_dtype), grid=(M // tm,),
                in_specs=[pl.BlockSpec((tm, W), lambda i: (i, cb)), pl.BlockSpec((1, W), lambda i: (0, 0))],
                out_specs=pl.BlockSpec((tm, W), lambda i: (i, 0)), name=name, carry=carry)


def _norm_bwd(x, g, dy, *, E, W, cb=0, res=None, fold=None, out_dtype=F32, low=False, into=None, name):
    M = x.shape[0]
    tm = _tile(M, ROW_BLOCK, SUBLANES)
    n = M // tm
    gw = W if fold is None else max(fold, LANES)

    def body(x_ref, g_ref, dy_ref, *rest):
        rest = list(rest)
        r_ref = rest.pop(0) if res is not None else None
        if into is not None:
            rest.pop(0)
        dx_ref = rest.pop(0)
        dxl_ref = rest.pop(0) if low else None
        dg_ref, acc_ref = rest
        i = pl.program_id(0)
        xv = x_ref[...]
        r = lax.rsqrt(_group_mean(xv * xv, E) + EPS)
        xh = xv * r
        dyv = dy_ref[...].astype(F32)
        dyg = dyv * g_ref[...]
        dx = r * (dyg - xh * _group_mean(dyg * xh, E))
        if res is not None:
            dx = dx + r_ref[...]
        dx_ref[...] = dx.astype(dx_ref.dtype)
        if low:
            dxl_ref[...] = dx.astype(MXU_DTYPE)
        part = jnp.sum(dyv * xh, axis=0, keepdims=True)

        @pl.when(i == 0)
        def _():
            acc_ref[...] = part

        @pl.when(i > 0)
        def _():
            acc_ref[...] += part

        @pl.when(i == n - 1)
        def _():
            t = acc_ref[...]
            dg_ref[...] = t if fold is None else _fold_groups(t, fold)

    blk = pl.BlockSpec((tm, W), lambda i: (i, 0))
    in_specs = [pl.BlockSpec((tm, W), lambda i: (i, cb)), pl.BlockSpec((1, W), lambda i: (0, 0)), blk]
    args = [x, g, dy]
    if res is not None:
        in_specs.append(blk)
        args.append(res)
    lows = ((_sds((M, W), MXU_DTYPE),), (blk,)) if low else ((), ())
    dx_out, dx_spec, aliases = _sds((M, W), out_dtype), blk, None
    if into is not None:
        aliases = {len(args): 0}
        args.append(into)
        in_specs.append(pl.BlockSpec(memory_space=pl.ANY))
        dx_out, dx_spec = _sds(into.shape, into.dtype), pl.BlockSpec((tm, W), lambda i: (i, cb))
    return _run(body, tuple(args), out_shape=(dx_out,) + lows[0] + (_sds((1, gw), F32),), grid=(n,),
                in_specs=in_specs, out_specs=(dx_spec,) + lows[1] + (pl.BlockSpec((1, gw), lambda i: (0, 0)),),
                scratch_shapes=[pltpu.VMEM((1, W), F32)], name=name, aliases=aliases)


def _attn_tables(S, tq):
    nq = S // tq

    def body(o_ref):
        i = pl.program_id(0)
        d = (i * tq + lax.broadcasted_iota(jnp.int32, (tq, S), 0)) - lax.broadcasted_iota(jnp.int32, (tq, S), 1)
        cnt = jnp.zeros((tq, S), jnp.int32)
        for dil in DILATIONS:
            hit = (d <= DIL_STEPS * dil) if dil == 1 else (((d & (dil - 1)) == 0) & (d <= DIL_STEPS * dil))
            cnt = cnt + hit.astype(jnp.int32)
        ok = (d >= 0) & (cnt > 0)
        logm = jnp.where(cnt == 3, math.log(3.0), jnp.where(cnt == 2, math.log(2.0), 0.0))
        o_ref[0] = jnp.where(ok, logm, NEG).astype(F32)

    return _run(body, (), out_shape=_sds((nq, tq, S), F32), grid=(nq,), in_specs=[],
                out_specs=pl.BlockSpec((1, tq, S), lambda i: (i, 0, 0)), name="dilated_attn_tables")


def _alibi_slope(h):
    return 2.0 ** (-8.0 / ATT_HEADS * (h + 1))


def _key_positions(Sk):
    kpos = lax.broadcasted_iota(jnp.int32, (Sk, LANES), 0)
    return (kpos >> 8).astype(F32).astype(MXU_DTYPE), (kpos & 255).astype(F32).astype(MXU_DTYPE)


def _score_operands(q2s, k2, pos_hi, pos_lo, lane, sub, h):
    own = (lane < ATT_HEAD_DIM) if sub == 0 else (lane >= ATT_HEAD_DIM)
    spare = ATT_HEAD_DIM if sub == 0 else 0
    slope = _alibi_slope(h)
    terms = jnp.where(lane == spare, slope * 256.0, jnp.where(lane == spare + 1, slope, 0.0)).astype(q2s.dtype)
    qa = jnp.where(own, q2s, terms)
    ka = jnp.where(lane == spare, pos_hi, jnp.where(lane == spare + 1, pos_lo, k2))
    return own, qa, ka


def _key_extents(S, tq):
    ext = min(ATT_KEY_EXTENT, S)
    return ext, ext // tq, S // ext


def _attn_fwd(tab, qn, kn, z, *, B, S, carry=None, name):
    T = B * S
    tq = tab.shape[1]
    nq = S // tq
    ext, per, n_ext = _key_extents(S, tq)
    scale = 1.0 / math.sqrt(ATT_HEAD_DIM)
    dn_t = (((1,), (1,)), ((), ()))

    def body(tab_ref, q_ref, k_ref, v_ref, y_ref, lse_ref, ycat_ref):
        i = pl.program_id(1)
        lane = lax.broadcasted_iota(jnp.int32, (1, LANES), 1)
        lo = lane < ATT_HEAD_DIM

        def visit(Sk):
            base = tab_ref[0, :, :Sk]
            pos_hi, pos_lo = _key_positions(Sk)
            lse_blk = jnp.zeros((tq, LANES), F32)
            for p in range(ATT_HEADS // 2):
                cs = slice(p * LANES, (p + 1) * LANES)
                q2s = q_ref[:, cs].astype(MXU_DTYPE) * scale
                k2 = k_ref[:Sk, cs].astype(MXU_DTYPE)
                v2 = v_ref[:Sk, cs].astype(MXU_DTYPE)
                outs = []
                for sub in range(2):
                    h = 2 * p + sub
                    _, qa, ka = _score_operands(q2s, k2, pos_hi, pos_lo, lane, sub, h)
                    s = lax.dot_general(qa, ka, dn_t, preferred_element_type=F32) + base
                    mx = jnp.max(s, axis=1, keepdims=True)
                    e = jnp.exp(s - mx)
                    l = jnp.sum(e, axis=1, keepdims=True)
                    outs.append(jnp.dot(e.astype(MXU_DTYPE), v2, preferred_element_type=F32) / l)
                    lse_blk = jnp.where(lane == h, mx + jnp.log(l), lse_blk)
                y = jnp.where(lo, outs[0], outs[1])
                y_ref[:, cs] = y
                ycat_ref[:, cs] = y.astype(MXU_DTYPE)
            lse_ref[...] = lse_blk

        for e in range(n_ext):
            pl.when(i // per == e)(functools.partial(visit, (e + 1) * ext))

    return _run(
        body, (tab, qn, kn, z), grid=(B, nq),
        out_shape=(_sds((T, ATT_W), F32), _sds((T, LANES), F32), _sds((T, ATT_W + HG_W), MXU_DTYPE)),
        in_specs=[pl.BlockSpec((1, tq, S), lambda b, i: (i, 0, 0)),
                  pl.BlockSpec((tq, ATT_W), lambda b, i: (b * nq + i, 0)),
                  pl.BlockSpec((S, ATT_W), lambda b, i: (b, 0)),
                  pl.BlockSpec((S, ATT_W), lambda b, i: (b, 2))],
        out_specs=(pl.BlockSpec((tq, ATT_W), lambda b, i: (b * nq + i, 0)),
                   pl.BlockSpec((tq, LANES), lambda b, i: (b * nq + i, 0)),
                   pl.BlockSpec((tq, ATT_W), lambda b, i: (b * nq + i, 0))),
        name=name, carry=carry)


def _attn_bwd(tab, qn, kn, z, y, lse, dycat, dz, *, B, S, carry=None, name):
    T = B * S
    tq = tab.shape[1]
    nq = S // tq
    ext, per, n_ext = _key_extents(S, tq)
    scale = 1.0 / math.sqrt(ATT_HEAD_DIM)
    dn_t = (((1,), (1,)), ((), ()))
    dn_o = (((0,), (0,)), ((), ()))

    def body(tab_ref, q_ref, k_ref, v_ref, y_ref, lse_ref, dy_ref, _, dq_ref, dk_ref, dz_ref, dv_ref):
        i = pl.program_id(1)

        @pl.when(i == 0)
        def _():
            dk_ref[...] = jnp.zeros_like(dk_ref)
            dv_ref[...] = jnp.zeros_like(dv_ref)

        lane = lax.broadcasted_iota(jnp.int32, (1, LANES), 1)
        lo = lane < ATT_HEAD_DIM

        def visit(Sk):
            base = tab_ref[0, :, :Sk]
            pos_hi, pos_lo = _key_positions(Sk)
            lse_blk = lse_ref[...]
            for p in range(ATT_HEADS // 2):
                cs = slice(p * LANES, (p + 1) * LANES)
                q2 = q_ref[:, cs].astype(MXU_DTYPE)
                q2s = q2 * scale
                k2 = k_ref[:Sk, cs].astype(MXU_DTYPE)
                v2 = v_ref[:Sk, cs].astype(MXU_DTYPE)
                do2 = dy_ref[:, cs]
                doy = do2 * y_ref[:, cs]
                do2m = do2.astype(MXU_DTYPE)
                dqs, dks, dvs = [], [], []
                for sub in range(2):
                    h = 2 * p + sub
                    own, qa, ka = _score_operands(q2s, k2, pos_hi, pos_lo, lane, sub, h)
                    s = lax.dot_general(qa, ka, dn_t, preferred_element_type=F32) + base
                    lse_h = jnp.sum(jnp.where(lane == h, lse_blk, 0.0), axis=1, keepdims=True)
                    pr = jnp.exp(s - lse_h)
                    dsum = jnp.sum(jnp.where(own, doy, 0.0), axis=1, keepdims=True)
                    dom = jnp.where(own, do2m, jnp.zeros_like(do2m))
                    dp = lax.dot_general(dom, v2, dn_t, preferred_element_type=F32)
                    ds = (pr * (dp - dsum)).astype(MXU_DTYPE)
                    dqs.append(jnp.dot(ds, k2, preferred_element_type=F32))
                    dks.append(lax.dot_general(ds, q2, dn_o, preferred_element_type=F32))
                    dvs.append(lax.dot_general(pr.astype(MXU_DTYPE), do2m, dn_o, preferred_element_type=F32))
                dq_ref[:, cs] = jnp.where(lo, dqs[0], dqs[1]) * scale
                dk_ref[:Sk, cs] += jnp.where(lo, dks[0], dks[1]) * scale
                dv_ref[:Sk, cs] += jnp.where(lo, dvs[0], dvs[1])

        for e in range(n_ext):
            pl.when(i // per == e)(functools.partial(visit, (e + 1) * ext))

        @pl.when(i == nq - 1)
        def _():
            dz_ref[...] = dv_ref[...].astype(dz_ref.dtype)

    qblk = pl.BlockSpec((tq, ATT_W), lambda b, i: (b * nq + i, 0))
    sblk = pl.BlockSpec((S, ATT_W), lambda b, i: (b, 0))
    vblk = pl.BlockSpec((S, ATT_W), lambda b, i: (b, 2))
    return _run(
        body, (tab, qn, kn, z, y, lse, dycat, dz), grid=(B, nq),
        out_shape=(_sds((T, ATT_W), F32), _sds((T, ATT_W), F32), _sds(dz.shape, dz.dtype)),
        in_specs=[pl.BlockSpec((1, tq, S), lambda b, i: (i, 0, 0)), qblk, sblk, vblk, qblk,
                  pl.BlockSpec((tq, LANES), lambda b, i: (b * nq + i, 0)), qblk, pl.BlockSpec(memory_space=pl.ANY)],
        out_specs=(qblk, sblk, vblk), scratch_shapes=[pltpu.VMEM((S, ATT_W), F32)], name=name, carry=carry,
        aliases={7: 2})


def _sigmoid_pair(x):
    en = jnp.exp(-jnp.abs(x))
    big = 1.0 / (1.0 + en)
    small = en * big
    pos = x >= 0
    return jnp.where(pos, big, small), jnp.where(pos, small, big)


def _chunk_scan(x, pos, reverse):
    n = x.shape[0]
    sh = 1
    while sh < HG_CHUNK:
        if reverse:
            x = x + jnp.where(pos < HG_CHUNK - sh, pltpu.roll(x, n - sh, 0), 0.0)
        else:
            x = x + jnp.where(pos >= sh, pltpu.roll(x, sh, 0), 0.0)
        sh *= 2
    return x


def _hgrn_pair_table(tb):
    def body(o_ref):
        t = lax.broadcasted_iota(jnp.int32, (tb, tb), 0)
        s = lax.broadcasted_iota(jnp.int32, (tb, tb), 1)
        o_ref[...] = jnp.where((t // HG_CHUNK) == (s // HG_CHUNK), (t // HG_SUB) - (s // HG_SUB), 0).astype(F32)

    return _pcall(body, out_shape=_sds((tb, tb), F32), name="hgrn_pair_table")()


def _hgrn_cross_decays(b, pos):
    n = b.shape[0]
    pos8 = pos % HG_SUB
    end = b
    sh = 1
    while sh < HG_SUB:
        end = jnp.where((pos8 & sh) == 0, pltpu.roll(end, n - sh, 0), end)
        sh *= 2
    x = jnp.where(pos >= HG_SUB, jnp.exp(jnp.minimum(b - pltpu.roll(end, HG_SUB, 0), 0.0)), 0.0)
    ys = []
    for j in range(1, HG_CHUNK // HG_SUB):
        end_j = end if j == 1 else pltpu.roll(end, n - HG_SUB * (j - 1), 0)
        ys.append(jnp.where(pos < HG_CHUNK - HG_SUB * j, jnp.exp(jnp.minimum(end_j - b, 0.0)), 0.0))
    return x, ys


def _hgrn_cross_scores(qt, kts, pair):
    n = qt.shape[0]
    kcat = jnp.concatenate([k.astype(MXU_DTYPE) for k in kts], axis=0)
    p = lax.dot_general(qt.astype(MXU_DTYPE), kcat, (((1,), (1,)), ((), ())), preferred_element_type=F32)
    a = jnp.zeros((n, n), F32)
    for j in range(1, len(kts) + 1):
        a = jnp.where(pair == j, p[:, (j - 1) * n:j * n], a)
    return a, kcat


def _hgrn_gates(qh, fh, lb):
    sq, _ = _sigmoid_pair(qh)
    sg, sgn = _sigmoid_pair(fh)
    f = lb + (1.0 - lb) * sg
    g = jnp.log(jnp.maximum(f, F_MIN))
    kk = (1.0 - lb) * sgn
    return qh * sq, sq, sg, sgn, f, g, kk


def _hgrn_fwd(z, lb, pair, *, B, S, name):
    T = B * S
    tb = _tile(S, HG_BLOCK, HG_CHUNK)
    nc = tb // HG_CHUNK
    nb = S // tb
    dn_t = (((1,), (1,)), ((), ()))
    dn_o = (((0,), (0,)), ((), ()))

    def zcol(first):
        return lambda h, b, i: (b * nb + i, first + h)

    def body(q_ref, f_ref, v_ref, lb_ref, pair_ref, o_ref, st_ref, state):
        i = pl.program_id(2)

        @pl.when(i == 0)
        def _():
            state[...] = jnp.zeros_like(state)

        q, _, _, _, _, g, kk = _hgrn_gates(q_ref[...], f_ref[...], lb_ref[...])
        v = v_ref[...]
        pos = lax.broadcasted_iota(jnp.int32, (tb, LANES), 0) % HG_CHUNK
        pos8 = pos % HG_SUB
        b = _chunk_scan(g, pos, False)
        o = jnp.sum(q * kk, axis=1, keepdims=True) * v
        for d in range(1, HG_SUB):
            qd = pltpu.roll(q, tb - d, 0)
            bd = pltpu.roll(b, tb - d, 0)
            e = jnp.where(pos8 < HG_SUB - d, jnp.exp(jnp.minimum(bd - b, 0.0)), 0.0)
            a = jnp.sum(qd * kk * e, axis=1, keepdims=True)
            o = o + pltpu.roll(a * v, d, 0)
        x, ys = _hgrn_cross_decays(b, pos)
        a_cross, _ = _hgrn_cross_scores(q * x, [kk * y for y in ys], pair_ref[...])
        o = o + jnp.dot(a_cross.astype(MXU_DTYPE), v.astype(MXU_DTYPE), preferred_element_type=F32)
        qe = q * jnp.exp(b)
        chunks = [slice(c * HG_CHUNK, (c + 1) * HG_CHUNK) for c in range(nc)]
        b_last = [jnp.sum(jnp.where(pos[rs] == HG_CHUNK - 1, b[rs], 0.0), axis=0, keepdims=True) for rs in chunks]
        own = [lax.dot_general(v[rs].astype(MXU_DTYPE), (kk[rs] * jnp.exp(bl - b[rs])).astype(MXU_DTYPE), dn_o,
                               preferred_element_type=F32) for rs, bl in zip(chunks, b_last)]
        st = state[...]
        before = []
        for c in range(nc):
            before.append(st)
            st = st * jnp.exp(b_last[c]) + own[c]
        state[...] = st
        for c, rs in enumerate(chunks):
            st_ref[c * LANES:(c + 1) * LANES, :] = before[c]
            o_ref[rs, :] = o[rs] + lax.dot_general(qe[rs].astype(MXU_DTYPE), before[c].astype(MXU_DTYPE), dn_t,
                                                   preferred_element_type=F32)

    blk = (tb, LANES)
    return _run(
        body, (z, z, z, lb, pair), out_shape=(_sds((T, HG_W), F32), _sds((T // HG_CHUNK * LANES, HG_W), F32)),
        grid=(HG_HEADS, B, nb),
        in_specs=[pl.BlockSpec(blk, zcol(12)), pl.BlockSpec(blk, zcol(16)), pl.BlockSpec(blk, zcol(20)),
                  pl.BlockSpec((1, LANES), lambda h, b, i: (0, h)), pl.BlockSpec((tb, tb), lambda h, b, i: (0, 0))],
        out_specs=(pl.BlockSpec(blk, lambda h, b, i: (b * nb + i, h)),
                   pl.BlockSpec((nc * LANES, LANES), lambda h, b, i: (b * nb + i, h))),
        scratch_shapes=[pltpu.VMEM((LANES, LANES), F32)], name=name)


def _hgrn_bwd(z, lb, pair, states, do, *, B, S, name):
    T = B * S
    tb = _tile(S, HG_BLOCK, HG_CHUNK)
    nc = tb // HG_CHUNK
    nb = S // tb
    dn_t = (((1,), (1,)), ((), ()))
    dn_o = (((0,), (0,)), ((), ()))

    def body(q_ref, f_ref, v_ref, lb_ref, pair_ref, st_ref, do_ref, dq_ref, df_ref, di_ref, dlb_ref, dstate):
        bi = pl.program_id(1)
        i = pl.program_id(2)

        @pl.when(i == 0)
        def _():
            dstate[...] = jnp.zeros_like(dstate)

        qh = q_ref[...]
        lbv = lb_ref[...]
        q, sq, sg, sgn, f, g, kk = _hgrn_gates(qh, f_ref[...], lbv)
        v = v_ref[...]
        dov = do_ref[...]
        pos = lax.broadcasted_iota(jnp.int32, (tb, LANES), 0) % HG_CHUNK
        pos8 = pos % HG_SUB
        b = _chunk_scan(g, pos, False)

        a0 = jnp.sum(q * kk, axis=1, keepdims=True)
        da0 = jnp.sum(dov * v, axis=1, keepdims=True)
        dv = a0 * dov
        dkk = da0 * q
        dq = da0 * kk
        db = jnp.zeros((tb, LANES), F32)
        for d in range(1, HG_SUB):
            qd = pltpu.roll(q, tb - d, 0)
            bd = pltpu.roll(b, tb - d, 0)
            dod = pltpu.roll(dov, tb - d, 0)
            e = jnp.where(pos8 < HG_SUB - d, jnp.exp(jnp.minimum(bd - b, 0.0)), 0.0)
            a = jnp.sum(qd * kk * e, axis=1, keepdims=True)
            da = jnp.sum(dod * v, axis=1, keepdims=True)
            dv = dv + a * dod
            xe = da * e
            dkk = dkk + xe * qd
            t1 = xe * kk
            wt = t1 * qd
            dq = dq + pltpu.roll(t1, d, 0)
            db = db + pltpu.roll(wt, d, 0) - wt

        n_cross = HG_CHUNK // HG_SUB - 1
        pair = pair_ref[...]
        x, ys = _hgrn_cross_decays(b, pos)
        qt = q * x
        kts = [kk * y for y in ys]
        a_cross, kcat = _hgrn_cross_scores(qt, kts, pair)
        dom = dov.astype(MXU_DTYPE)
        da_all = lax.dot_general(dom, v.astype(MXU_DTYPE), dn_t, preferred_element_type=F32)
        dv = dv + lax.dot_general(a_cross.astype(MXU_DTYPE), dom, dn_o, preferred_element_type=F32)
        gcat = jnp.concatenate([jnp.where(pair == j, da_all, 0.0).astype(MXU_DTYPE) for j in range(1, n_cross + 1)],
                               axis=1)
        dqt = jnp.dot(gcat, kcat, preferred_element_type=F32)
        dkcat = lax.dot_general(gcat, qt.astype(MXU_DTYPE), dn_o, preferred_element_type=F32)
        dq = dq + dqt * x
        u = dqt * qt
        db = db + u
        d_end = -pltpu.roll(u, tb - HG_SUB, 0)
        for j in range(1, n_cross + 1):
            dkt = dkcat[(j - 1) * tb:j * tb]
            dkk = dkk + dkt * ys[j - 1]
            w = dkt * kts[j - 1]
            db = db - w
            d_end = d_end + (w if j == 1 else pltpu.roll(w, HG_SUB * (j - 1), 0))
        sh = 1
        while sh < HG_SUB:
            d_end = d_end + jnp.where(pos8 >= sh, pltpu.roll(d_end, sh, 0), 0.0)
            sh *= 2
        db = db + jnp.where(pos8 == HG_SUB - 1, d_end, 0.0)

        eb = jnp.exp(b)
        qe = q * eb
        chunks = [slice(c * HG_CHUNK, (c + 1) * HG_CHUNK) for c in range(nc)]
        b_lasts = [jnp.sum(jnp.where(pos[rs] == HG_CHUNK - 1, b[rs], 0.0), axis=0, keepdims=True) for rs in chunks]
        own = [lax.dot_general(dom[rs], qe[rs].astype(MXU_DTYPE), dn_o, preferred_element_type=F32) for rs in chunks]
        dst = dstate[...]
        after = [None] * nc
        for c in range(nc - 1, -1, -1):
            after[c] = dst
            dst = dst * jnp.exp(b_lasts[c]) + own[c]
        dstate[...] = dst
        dq_c, dkk_c, dv_c, db_c = [None] * nc, [None] * nc, [None] * nc, [None] * nc
        for c, rs in enumerate(chunks):
            st0 = st_ref[c * LANES:(c + 1) * LANES, :]
            eb_last = jnp.exp(b_lasts[c])
            er = jnp.exp(b_lasts[c] - b[rs])
            ke = kk[rs] * er
            dstm = after[c].astype(MXU_DTYPE)
            dqe = jnp.dot(dom[rs], st0.astype(MXU_DTYPE), preferred_element_type=F32)
            dv_c[c] = dv[rs] + lax.dot_general(ke.astype(MXU_DTYPE), dstm, dn_t, preferred_element_type=F32)
            dke = jnp.dot(v[rs].astype(MXU_DTYPE), dstm, preferred_element_type=F32)
            d_eb_last = jnp.sum(after[c] * st0, axis=0, keepdims=True)
            dq_c[c] = dq[rs] + dqe * eb[rs]
            dkk_c[c] = dkk[rs] + dke * er
            dkeke = dke * ke
            db_last = jnp.sum(dkeke, axis=0, keepdims=True) + d_eb_last * eb_last
            db_c[c] = db[rs] + dqe * qe[rs] - dkeke + jnp.where(pos[rs] == HG_CHUNK - 1, db_last, 0.0)
        dq = jnp.concatenate(dq_c, axis=0)
        dkk = jnp.concatenate(dkk_c, axis=0)
        dv = jnp.concatenate(dv_c, axis=0)
        dg = _chunk_scan(jnp.concatenate(db_c, axis=0), pos, True)

        dfv = dg * jnp.where(f > F_MIN, 1.0 / f, 0.0)
        dsg = (dfv - dkk) * (1.0 - lbv)
        dq_ref[...] = (dq * (sq * (1.0 + qh * (1.0 - sq)))).astype(dq_ref.dtype)
        df_ref[...] = (dsg * sg * sgn).astype(df_ref.dtype)
        di_ref[...] = dv.astype(di_ref.dtype)
        part = jnp.sum(dfv * sgn - dkk * sgn, axis=0, keepdims=True)

        @pl.when((bi == 0) & (i == 0))
        def _():
            dlb_ref[...] = part

        @pl.when((bi > 0) | (i > 0))
        def _():
            dlb_ref[...] += part

    def rev(first):
        return lambda h, b, i: (b * nb + (nb - 1 - i), first + h)

    blk = (tb, LANES)
    oblk = pl.BlockSpec(blk, rev(0))
    return _run(
        body, (z, z, z, lb, pair, states, do), out_shape=(_sds((T, HG_W), MXU_DTYPE),) * 3 + (_sds((1, HG_W), F32),),
        grid=(HG_HEADS, B, nb),
        in_specs=[pl.BlockSpec(blk, rev(12)), pl.BlockSpec(blk, rev(16)), pl.BlockSpec(blk, rev(20)),
                  pl.BlockSpec((1, LANES), lambda h, b, i: (0, h)), pl.BlockSpec((tb, tb), lambda h, b, i: (0, 0)),
                  pl.BlockSpec((nc * LANES, LANES), rev(0)), oblk],
        out_specs=(oblk, oblk, oblk, pl.BlockSpec((1, LANES), lambda h, b, i: (0, h))),
        scratch_shapes=[pltpu.VMEM((LANES, LANES), F32)], name=name)


def _hgout_fwd(o, z, g, ycat, *, name):
    T = o.shape[0]
    tm = _tile(T, ROW_BLOCK, SUBLANES)

    def body(o_ref, gh_ref, g_ref, _, y_ref):
        ov = o_ref[...]
        gh = gh_ref[...]
        r = lax.rsqrt(_group_mean(ov * ov, HG_HEAD_DIM) + EPS)
        sg, _ = _sigmoid_pair(gh)
        y_ref[...] = (ov * r * g_ref[...] * (gh * sg)).astype(y_ref.dtype)

    blk = pl.BlockSpec((tm, HG_W), lambda i: (i, 0))
    return _run(body, (o, z, g, ycat), out_shape=_sds(ycat.shape, ycat.dtype), grid=(T // tm,),
                in_specs=[blk, pl.BlockSpec((tm, HG_W), lambda i: (i, 6)), pl.BlockSpec((1, HG_W), lambda i: (0, 0)),
                          pl.BlockSpec(memory_space=pl.ANY)],
                out_specs=pl.BlockSpec((tm, HG_W), lambda i: (i, ATT_W // HG_W)), name=name, aliases={3: 0})


def _hgout_bwd(o, z, g, dycat, *, name):
    T = o.shape[0]
    tm = _tile(T, ROW_BLOCK, SUBLANES)

    def body(o_ref, gh_ref, g_ref, dy_ref, do_ref, dgh_ref, dg_ref):
        i = pl.program_id(0)
        ov = o_ref[...]
        gh = gh_ref[...]
        gv = g_ref[...]
        dy = dy_ref[...]
        r = lax.rsqrt(_group_mean(ov * ov, HG_HEAD_DIM) + EPS)
        xh = ov * r
        sg, _ = _sigmoid_pair(gh)
        dn = dy * (gh * sg)
        dgh_ref[...] = (dy * xh * gv * (sg * (1.0 + gh * (1.0 - sg)))).astype(dgh_ref.dtype)
        dng = dn * gv
        do_ref[...] = r * (dng - xh * _group_mean(dng * xh, HG_HEAD_DIM))
        part = jnp.sum(dn * xh, axis=0, keepdims=True)

        @pl.when(i == 0)
        def _():
            dg_ref[...] = part

        @pl.when(i > 0)
        def _():
            dg_ref[...] += part

    blk = pl.BlockSpec((tm, HG_W), lambda i: (i, 0))
    row = pl.BlockSpec((1, HG_W), lambda i: (0, 0))
    return _run(body, (o, z, g, dycat),
                out_shape=(_sds((T, HG_W), F32), _sds(z.shape, MXU_DTYPE), _sds((1, HG_W), F32)), grid=(T // tm,),
                in_specs=[blk, pl.BlockSpec((tm, HG_W), lambda i: (i, 6)), row,
                          pl.BlockSpec((tm, HG_W), lambda i: (i, 1))],
                out_specs=(blk, pl.BlockSpec((tm, HG_W), lambda i: (i, 6)), row), name=name)


def _lower_bounds_fwd(hg_lb):
    L = hg_lb.shape[0]

    def body(x_ref, o_ref):
        rows = [x_ref[l:l + 1, :] for l in range(L)]
        mx = functools.reduce(jnp.maximum, rows)
        es = [jnp.exp(r - mx) for r in rows]
        tot = functools.reduce(jnp.add, es)
        acc = jnp.zeros_like(tot)
        for l in range(L):
            acc = acc + es[l] / tot
            o_ref[l:l + 1, :] = acc - es[0] / tot

    return _pcall(body, out_shape=_sds(hg_lb.shape, F32), name="lower_bounds_fwd")(hg_lb)


def _lower_bounds_bwd(hg_lb, dlower):
    L = hg_lb.shape[0]

    def body(x_ref, d_ref, o_ref):
        rows = [x_ref[l:l + 1, :] for l in range(L)]
        ds = [d_ref[l:l + 1, :] for l in range(L)]
        mx = functools.reduce(jnp.maximum, rows)
        es = [jnp.exp(r - mx) for r in rows]
        tot = functools.reduce(jnp.add, es)
        ps = [e / tot for e in es]
        dps = []
        for j in range(L):
            t = functools.reduce(jnp.add, ds[j:])
            if j == 0:
                t = t - functools.reduce(jnp.add, ds)
            dps.append(t)
        inner = functools.reduce(jnp.add, [p * dp for p, dp in zip(ps, dps)])
        for j in range(L):
            o_ref[j:j + 1, :] = ps[j] * (dps[j] - inner)

    return _pcall(body, out_shape=_sds(hg_lb.shape, F32), name="lower_bounds_bwd")(hg_lb, dlower)


def _memattn_fwd(qm, km, kv, *, B, S, NM, name):
    T, D = qm.shape
    E = D // MEM_HEADS
    tq = _tile(S, ROW_BLOCK, SUBLANES)
    nq = S // tq
    scale = 1.0 / math.sqrt(E)

    def body(q_ref, k_ref, v_ref, o_ref):
        for h in range(MEM_HEADS):
            cs = slice(h * E, (h + 1) * E)
            s = lax.dot_general(q_ref[:, cs].astype(MXU_DTYPE), k_ref[:, cs].astype(MXU_DTYPE),
                                (((1,), (1,)), ((), ())), preferred_element_type=F32) * scale
            e = jnp.exp(s - jnp.max(s, axis=1, keepdims=True))
            pr = e / jnp.sum(e, axis=1, keepdims=True)
            o_ref[:, cs] = jnp.dot(pr.astype(MXU_DTYPE), v_ref[:, cs].astype(MXU_DTYPE),
                                   preferred_element_type=F32).astype(o_ref.dtype)

    return _run(
        body, (qm, km, kv), out_shape=_sds((T, D), MXU_DTYPE), grid=(B, nq),
        in_specs=[pl.BlockSpec((tq, D), lambda b, i: (b * nq + i, 0)), pl.BlockSpec((NM, D), lambda b, i: (b, 0)),
                  pl.BlockSpec((NM, D), lambda b, i: (b, 1))],
        out_specs=pl.BlockSpec((tq, D), lambda b, i: (b * nq + i, 0)), name=name)


def _memattn_bwd(qm, km, kv, do, *, B, S, NM, name):
    T, D = qm.shape
    E = D // MEM_HEADS
    tq = _tile(S, ROW_BLOCK, SUBLANES)
    nq = S // tq
    scale = 1.0 / math.sqrt(E)

    def body(q_ref, k_ref, v_ref, do_ref, dq_ref, dk_ref, dv_ref):
        i = pl.program_id(1)

        @pl.when(i == 0)
        def _():
            dk_ref[...] = jnp.zeros_like(dk_ref)
            dv_ref[...] = jnp.zeros_like(dv_ref)

        for h in range(MEM_HEADS):
            cs = slice(h * E, (h + 1) * E)
            qh = q_ref[:, cs].astype(MXU_DTYPE)
            kh = k_ref[:, cs].astype(MXU_DTYPE)
            vh = v_ref[:, cs].astype(MXU_DTYPE)
            doh = do_ref[:, cs].astype(MXU_DTYPE)
            s = lax.dot_general(qh, kh, (((1,), (1,)), ((), ())), preferred_element_type=F32) * scale
            e = jnp.exp(s - jnp.max(s, axis=1, keepdims=True))
            pr = e / jnp.sum(e, axis=1, keepdims=True)
            dp = lax.dot_general(doh, vh, (((1,), (1,)), ((), ())), preferred_element_type=F32)
            ds = (pr * (dp - jnp.sum(dp * pr, axis=1, keepdims=True))).astype(MXU_DTYPE)
            dq_ref[:, cs] = jnp.dot(ds, kh, preferred_element_type=F32) * scale
            dk_ref[:, cs] += lax.dot_general(ds, qh, (((0,), (0,)), ((), ())), preferred_element_type=F32) * scale
            dv_ref[:, cs] += lax.dot_general(pr.astype(MXU_DTYPE), doh, (((0,), (0,)), ((), ())),
                                             preferred_element_type=F32)

    qblk = pl.BlockSpec((tq, D), lambda b, i: (b * nq + i, 0))
    mblk = pl.BlockSpec((NM, D), lambda b, i: (b, 0))
    return _run(
        body, (qm, km, kv, do), out_shape=(_sds((T, D), F32), _sds((B * NM, D), F32), _sds((B * NM, D), F32)),
        grid=(B, nq), in_specs=[qblk, mblk, pl.BlockSpec((NM, D), lambda b, i: (b, 1)), qblk],
        out_specs=(qblk, mblk, mblk), name=name)


def _loss_head(y, tgt):
    T, D = y.shape
    tm = _tile(T, ROW_BLOCK, SUBLANES)

    def body(y_ref, t_ref, dy_ref, dyl_ref, l_ref):
        i = pl.program_id(0)
        err = y_ref[...] - t_ref[...]
        dy_ref[...] = err * (1.0 / D)
        dyl_ref[...] = (err * (1.0 / D)).astype(MXU_DTYPE)
        part = jnp.sum(jnp.sum(err * err, axis=1, keepdims=True), axis=0, keepdims=True) * (0.5 / D)

        @pl.when(i == 0)
        def _():
            l_ref[...] = part

        @pl.when(i > 0)
        def _():
            l_ref[...] += part

    blk = pl.BlockSpec((tm, D), lambda i: (i, 0))
    return _run(body, (y, tgt), out_shape=(_sds((T, D), F32), _sds((T, D), MXU_DTYPE), _sds((1, 1), F32)),
                grid=(T // tm,), in_specs=[blk, blk], out_specs=(blk, blk, pl.BlockSpec((1, 1), lambda i: (0, 0))),
                name="loss_head")


def _adamw(w, g, m, v):
    m = ADAM_B1 * m + (1.0 - ADAM_B1) * g
    v = ADAM_B2 * v + (1.0 - ADAM_B2) * jnp.square(g)
    m_hat = m / (1.0 - ADAM_B1 ** ADAM_STEP)
    v_hat = v / (1.0 - ADAM_B2 ** ADAM_STEP)
    delta = -ADAM_LR * (m_hat / (jnp.sqrt(v_hat) + ADAM_EPS) + ADAM_WD * w)
    return delta, m, v


def _sum_adamw(recv, w, m, v, layer, so_far=None):
    L, r, c = w.shape
    tr = _tile(r, ADAM_BLOCK, SUBLANES)

    def body(r_ref, w_ref, m_ref, v_ref, *rest):
        g_ref, d_ref, nm_ref, nv_ref = rest[-4:]
        g = r_ref[0].astype(F32)
        for j in range(1, N_DEV):
            g = g + r_ref[j].astype(F32)
        g_ref[...] = g
        d_ref[...], nm_ref[...], nv_ref[...] = _adamw(w_ref[...], g, m_ref[...], v_ref[...])

    blk = pl.BlockSpec((None, tr, c), lambda i: (layer, i, 0))
    hbm = pl.BlockSpec(memory_space=pl.ANY)
    args, in_specs, aliases = (recv, w, m, v), [pl.BlockSpec((N_DEV, tr, c), lambda i: (0, i, 0)), blk, blk, blk], {}
    if so_far is not None:
        args, in_specs, aliases = args + tuple(so_far), in_specs + [hbm] * 4, {4 + k: k for k in range(4)}
    return _run(body, args, out_shape=(_sds((L, r, c), F32),) * 4, grid=(r // tr,), in_specs=in_specs,
                out_specs=(blk,) * 4, name="grad_sum_adamw", aliases=aliases)


def _small_sum_adamw(gp, w, m, v):
    NR, C = gp.shape
    MESH = pl.DeviceIdType.MESH

    def body(gp_ref, w_ref, m_ref, v_ref, g_ref, d_ref, nm_ref, nv_ref, gath, send_sems, recv_sems):
        x, y, c = _mesh_pos()
        me = 4 * x + 2 * y + c
        gath[me] = gp_ref[...]
        copies = []
        for k in range(1, N_DEV):
            peer = (_flip(x, k & 4), _flip(y, k & 2), _flip(c, k & 1))
            cp = pltpu.make_async_remote_copy(src_ref=gp_ref, dst_ref=gath.at[me], send_sem=send_sems.at[k - 1],
                                              recv_sem=recv_sems.at[k - 1], device_id=peer, device_id_type=MESH)
            cp.start()
            copies.append(cp)
        for cp in copies:
            cp.wait()
        g = gath[0]
        for j in range(1, N_DEV):
            g = g + gath[j]
        g_ref[...] = g
        d_ref[...], nm_ref[...], nv_ref[...] = _adamw(w_ref[...], g, m_ref[...], v_ref[...])

    vm = pl.BlockSpec(memory_space=pltpu.VMEM)
    return _pcall(body, out_shape=(_sds((NR, C), F32),) * 4, in_specs=[vm] * 4, out_specs=(vm,) * 4,
                  scratch_shapes=[pltpu.VMEM((N_DEV, NR, C), F32), pltpu.SemaphoreType.DMA((7,)),
                                  pltpu.SemaphoreType.DMA((7,))], name="small_sum_adamw")(gp, w, m, v)


def _gathered(gath, name):
    _, r, c = gath.shape
    return gath.transpose(1, 0, 2).reshape(r, N_DEV * c) if name in COL_SHARDED else gath.reshape(N_DEV * r, c)


def _scatter_blocks(grad, name):
    rows, cols = grad.shape
    if name in COL_SHARDED:
        return grad.reshape(rows, N_DEV, cols // N_DEV).transpose(1, 0, 2)
    return grad.reshape(N_DEV, rows // N_DEV, cols)


def _pack_small(d, prefix, loss=None):
    parts = []
    for n in SMALL:
        a = d[prefix + n].reshape(-1).astype(F32)
        pad = (-a.shape[0]) % LANES
        parts.append(jnp.pad(a, (0, pad)) if pad else a)
    tail = jnp.zeros((LANES,), F32)
    if loss is not None:
        tail = tail.at[0].set(loss)
    flat = jnp.concatenate(parts + [tail])
    pad = (-flat.shape[0]) % (8 * LANES)
    return jnp.pad(flat, (0, pad)).reshape(-1, LANES)


def _unpack_small(packed, shapes):
    flat = packed.reshape(-1)
    out, off = {}, 0
    for n in SMALL:
        size = math.prod(shapes[n])
        out[n] = flat[off:off + size].reshape(shapes[n])
        off += size + (-size) % LANES
    return out, flat[off]


def _row(a):
    return a.reshape(1, -1).astype(F32)


def _layer_fwd(x, memf, w_in, shards, next_in, sp, lb, tab, dims):
    B, S, NM = dims
    D = x.shape[1]
    if w_in is None:
        h1, gath = _norm_fwd(x, sp["norm1_g"], E=D, W=D, out_dtype=MXU_DTYPE, name="norm1_fwd",
                             carry=("gather", [shards["w_in"].astype(MXU_DTYPE)]))
        w_in = _gathered(gath[0], "w_in")
    else:
        h1 = _norm_fwd(x, sp["norm1_g"], E=D, W=D, out_dtype=MXU_DTYPE, name="norm1_fwd")
    W = {"w_in": w_in}
    z, gath = _matmul(h1, w_in, carry=("gather", [shards[n].astype(MXU_DTYPE) for n in GATHER_MIX]), name="in_proj")
    W.update({n: _gathered(t, n) for n, t in zip(GATHER_MIX, gath)})
    qn = _norm_fwd(z, sp["attn_qn_g"], E=ATT_HEAD_DIM, W=ATT_W, cb=0, out_dtype=MXU_DTYPE, name="qnorm_fwd")
    kn = _norm_fwd(z, sp["attn_kn_g"], E=ATT_HEAD_DIM, W=ATT_W, cb=1, out_dtype=MXU_DTYPE, name="knorm_fwd")
    (ya, lse, ycat), gath = _attn_fwd(tab["attn"], qn, kn, z, B=B, S=S, name="dilated_attn_fwd",
                                carry=("gather", [shards[n].astype(MXU_DTYPE) for n in GATHER_FF]))
    W.update({n: _gathered(t, n) for n, t in zip(GATHER_FF, gath)})
    o, states = _hgrn_fwd(z, lb, tab["hgrn"], B=B, S=S, name="hgrn_fwd")
    ycat = _hgout_fwd(o, z, sp["hg_onorm_g"], ycat, name="hgrn_out_fwd")
    x1 = _matmul(ycat, W["w_out"], epi="add", extra=x, name="out_proj")
    h2 = _norm_fwd(x1, sp["norm2_g"], E=D, W=D, out_dtype=MXU_DTYPE, name="norm2_fwd")
    qmp = _matmul(h2, W["w_mq"], name="mq_proj")
    qm = _norm_fwd(qmp, sp["mq_norm_g"], E=D // MEM_HEADS, W=D, out_dtype=MXU_DTYPE, name="mqnorm_fwd")
    mn = _norm_fwd(memf, sp["mem_norm_g"], E=D, W=D, out_dtype=MXU_DTYPE, name="memnorm_fwd")
    kv = _matmul(mn, W["w_mkv"], name="mkv_proj")
    km = _norm_fwd(kv, sp["mk_norm_g"], E=D // MEM_HEADS, W=D, cb=0, out_dtype=MXU_DTYPE, name="mknorm_fwd")
    om = _memattn_fwd(qm, km, kv, B=B, S=S, NM=NM, name="mem_attn_fwd")
    x2 = _matmul(om, W["w_mo"], epi="add", extra=x1, name="mo_proj")
    h3 = _norm_fwd(x2, sp["norm3_g"], E=D, W=D, out_dtype=MXU_DTYPE, name="norm3_fwd")
    if next_in is None:
        u, w_in_next = _matmul(h3, W["w_ff1"], epi="relu2", out_dtype=MXU_DTYPE, name="ff1"), None
    else:
        u, gath = _matmul(h3, W["w_ff1"], epi="relu2", out_dtype=MXU_DTYPE,
                          carry=("gather", [next_in.astype(MXU_DTYPE)]), name="ff1")
        w_in_next = _gathered(gath[0], "w_in")
    x3 = _matmul(u, W["w_ff2"], epi="add", extra=x2, name="ff2")
    saved = dict(x=x, h1=h1, z=z, qn=qn, kn=kn, ya=ya, lse=lse, o=o, states=states, ycat=ycat, x1=x1, h2=h2,
                 qmp=qmp, qm=qm, mn=mn, kv=kv, km=km, om=om, x2=x2, h3=h3, u=u)
    return x3, saved, W, w_in_next


def _layer_bwd(dx3, dx3_low, s, memf, W, sp, lb, tab, dims):
    B, S, NM = dims
    D = dx3.shape[1]
    E_M = D // MEM_HEADS
    gw, gs, recv = {}, {}, {}

    def blocks(names):
        return ("scatter", [_scatter_blocks(gw[n], n) for n in names])

    def received(names, got):
        recv.update(dict(zip(names, got)))

    da = _matmul(dx3_low, W["w_ff2"], tb=True, epi="relu2grad", extra=s["u"], out_dtype=MXU_DTYPE, name="ff2_dgrad")
    gw["w_ff2"] = _matmul(s["u"], dx3_low, ta=True, out_dtype=MXU_DTYPE, name="ff2_wgrad")
    dh3, got = _matmul(da, W["w_ff1"], tb=True, out_dtype=MXU_DTYPE, carry=blocks(SCATTER_A), name="ff1_dgrad")
    received(SCATTER_A, got)
    gw["w_ff1"] = _matmul(s["h3"], da, ta=True, out_dtype=MXU_DTYPE, name="ff1_wgrad")
    dx2, dx2_low, gs["norm3_g"] = _norm_bwd(s["x2"], sp["norm3_g"], dh3, E=D, W=D, res=dx3, low=True, name="norm3_bwd")
    dom = _matmul(dx2_low, W["w_mo"], tb=True, out_dtype=MXU_DTYPE, name="mo_dgrad")
    gw["w_mo"] = _matmul(s["om"], dx2_low, ta=True, out_dtype=MXU_DTYPE, name="mo_wgrad")
    dqm, dkm, dvm = _memattn_bwd(s["qm"], s["km"], s["kv"], dom, B=B, S=S, NM=NM, name="mem_attn_bwd")
    dqmp, gs["mq_norm_g"] = _norm_bwd(s["qmp"], sp["mq_norm_g"], dqm, E=E_M, W=D, fold=E_M, out_dtype=MXU_DTYPE,
                                      name="mqnorm_bwd")
    dkmp, gs["mk_norm_g"] = _norm_bwd(s["kv"], sp["mk_norm_g"], dkm, E=E_M, W=D, cb=0, fold=E_M,
                                      out_dtype=MXU_DTYPE, name="mknorm_bwd")
    dkv = jnp.concatenate([dkmp, dvm.astype(MXU_DTYPE)], axis=1)
    dh2 = _matmul(dqmp, W["w_mq"], tb=True, out_dtype=MXU_DTYPE, name="mq_dgrad")
    gw["w_mq"] = _matmul(s["h2"], dqmp, ta=True, out_dtype=MXU_DTYPE, name="mq_wgrad")
    dmn = _matmul(dkv, W["w_mkv"], tb=True, out_dtype=MXU_DTYPE, name="mkv_dgrad")
    gw["w_mkv"] = _matmul(s["mn"], dkv, ta=True, out_dtype=MXU_DTYPE, name="mkv_wgrad")
    _, gs["mem_norm_g"] = _norm_bwd(memf, sp["mem_norm_g"], dmn, E=D, W=D, name="memnorm_bwd")
    dx1, dx1_low, gs["norm2_g"] = _norm_bwd(s["x1"], sp["norm2_g"], dh2, E=D, W=D, res=dx2, low=True, name="norm2_bwd")
    dycat = _matmul(dx1_low, W["w_out"], tb=True, out_dtype=MXU_DTYPE, name="out_dgrad")
    gw["w_out"] = _matmul(s["ycat"], dx1_low, ta=True, out_dtype=MXU_DTYPE, name="out_wgrad")
    do_hg, dz, gs["hg_onorm_g"] = _hgout_bwd(s["o"], s["z"], sp["hg_onorm_g"], dycat, name="hgrn_out_bwd")
    dqh, dfh, dih, dlb = _hgrn_bwd(s["z"], lb, tab["hgrn"], s["states"], do_hg, B=B, S=S, name="hgrn_bwd")
    (dqn, dkn, dz), got = _attn_bwd(tab["attn"], s["qn"], s["kn"], s["z"], s["ya"], s["lse"], dycat, dz, B=B, S=S,
                                     carry=blocks(SCATTER_B), name="dilated_attn_bwd")
    received(SCATTER_B, got)
    dz, gs["attn_qn_g"] = _norm_bwd(s["z"], sp["attn_qn_g"], dqn, E=ATT_HEAD_DIM, W=ATT_W, cb=0, fold=ATT_HEAD_DIM,
                                    into=dz, name="qnorm_bwd")
    dz, gs["attn_kn_g"] = _norm_bwd(s["z"], sp["attn_kn_g"], dkn, E=ATT_HEAD_DIM, W=ATT_W, cb=1, fold=ATT_HEAD_DIM,
                                    into=dz, name="knorm_bwd")
    for block, t in ((3, dqh), (4, dfh), (5, dih)):
        dz = lax.dynamic_update_slice(dz, t, (0, block * HG_W))
    gw["w_in"] = _matmul(s["h1"], dz, ta=True, out_dtype=MXU_DTYPE, name="in_wgrad")
    dh1, got = _matmul(dz, W["w_in"], tb=True, out_dtype=MXU_DTYPE, carry=blocks(SCATTER_C), name="in_dgrad")
    received(SCATTER_C, got)
    dx0, dx0_low, gs["norm1_g"] = _norm_bwd(s["x"], sp["norm1_g"], dh1, E=D, W=D, res=dx1, low=True, name="norm1_bwd")
    gs["attn_qn_g"] = gs["attn_qn_g"][:, :ATT_HEAD_DIM]
    gs["attn_kn_g"] = gs["attn_kn_g"][:, :ATT_HEAD_DIM]
    return dx0, dx0_low, recv, gs, dlb


def kernel(x, mem, norm1_g, w_in, attn_qn_g, attn_kn_g, hg_lb, hg_onorm_g, w_out, norm2_g, mem_norm_g, w_mq, w_mkv, mq_norm_g, mk_norm_g, w_mo, norm3_g, w_ff1, w_ff2, loss_target, m_norm1_g, m_w_in, m_attn_qn_g, m_attn_kn_g, m_hg_lb, m_hg_onorm_g, m_w_out, m_norm2_g, m_mem_norm_g, m_w_mq, m_w_mkv, m_mq_norm_g, m_mk_norm_g, m_w_mo, m_norm3_g, m_w_ff1, m_w_ff2, v_norm1_g, v_w_in, v_attn_qn_g, v_attn_kn_g, v_hg_lb, v_hg_onorm_g, v_w_out, v_norm2_g, v_mem_norm_g, v_w_mq, v_w_mkv, v_mq_norm_g, v_mk_norm_g, v_w_mo, v_norm3_g, v_w_ff1, v_w_ff2):
    given = dict(norm1_g=norm1_g, w_in=w_in, attn_qn_g=attn_qn_g, attn_kn_g=attn_kn_g, hg_lb=hg_lb, hg_onorm_g=hg_onorm_g, w_out=w_out, norm2_g=norm2_g, mem_norm_g=mem_norm_g, w_mq=w_mq, w_mkv=w_mkv, mq_norm_g=mq_norm_g, mk_norm_g=mk_norm_g, w_mo=w_mo, norm3_g=norm3_g, w_ff1=w_ff1, w_ff2=w_ff2, m_norm1_g=m_norm1_g, m_w_in=m_w_in, m_attn_qn_g=m_attn_qn_g, m_attn_kn_g=m_attn_kn_g, m_hg_lb=m_hg_lb, m_hg_onorm_g=m_hg_onorm_g, m_w_out=m_w_out, m_norm2_g=m_norm2_g, m_mem_norm_g=m_mem_norm_g, m_w_mq=m_w_mq, m_w_mkv=m_w_mkv, m_mq_norm_g=m_mq_norm_g, m_mk_norm_g=m_mk_norm_g, m_w_mo=m_w_mo, m_norm3_g=m_norm3_g, m_w_ff1=m_w_ff1, m_w_ff2=m_w_ff2, v_norm1_g=v_norm1_g, v_w_in=v_w_in, v_attn_qn_g=v_attn_qn_g, v_attn_kn_g=v_attn_kn_g, v_hg_lb=v_hg_lb, v_hg_onorm_g=v_hg_onorm_g, v_w_out=v_w_out, v_norm2_g=v_norm2_g, v_mem_norm_g=v_mem_norm_g, v_w_mq=v_w_mq, v_w_mkv=v_w_mkv, v_mq_norm_g=v_mq_norm_g, v_mk_norm_g=v_mk_norm_g, v_w_mo=v_w_mo, v_norm3_g=v_norm3_g, v_w_ff1=v_w_ff1, v_w_ff2=v_w_ff2)
    B, S, D = x.shape
    NM = mem.shape[1]
    L = w_in.shape[0]
    dims = (B, S, NM)
    small_shapes = {n: given[n].shape for n in SMALL}

    def shards(prefix, l):
        return {n: given[prefix + n][l] for n in BIG}

    lower = _lower_bounds_fwd(hg_lb)
    tab = {"attn": _attn_tables(S, _tile(S, ATT_Q_BLOCK, SUBLANES)), "hgrn": _hgrn_pair_table(_tile(S, HG_BLOCK, HG_CHUNK))}
    xf = x.reshape(B * S, D)
    memf = mem.reshape(B * NM, D)

    def small_params(l):
        sp = {"norm1_g": _row(norm1_g[l]), "norm2_g": _row(norm2_g[l]), "norm3_g": _row(norm3_g[l]),
              "mem_norm_g": _row(mem_norm_g[l]), "hg_onorm_g": _row(hg_onorm_g[l]),
              "attn_qn_g": _row(jnp.tile(attn_qn_g[l], ATT_HEADS)), "attn_kn_g": _row(jnp.tile(attn_kn_g[l], ATT_HEADS)),
              "mq_norm_g": _row(jnp.tile(mq_norm_g[l], MEM_HEADS)), "mk_norm_g": _row(jnp.tile(mk_norm_g[l], MEM_HEADS))}
        return sp, _row(lower[l])

    w_in_full = None
    saved, weights = [], []
    h = xf
    for l in range(L):
        sp, lb = small_params(l)
        next_in = given["w_in"][l + 1] if l + 1 < L else None
        h, s, W, w_in_full = _layer_fwd(h, memf, w_in_full, shards("", l), next_in, sp, lb, tab, dims)
        saved.append(s)
        weights.append(W)

    dh, dh_low, loss_part = _loss_head(h, loss_target.reshape(B * S, D))

    recv_layers, gs_layers, dlb_layers = [None] * L, [None] * L, [None] * L
    for l in range(L - 1, -1, -1):
        sp, lb = small_params(l)
        dh, dh_low, recv_layers[l], gs_layers[l], dlb_layers[l] = _layer_bwd(dh, dh_low, saved[l], memf, weights[l],
                                                                              sp, lb, tab, dims)
    grad_x = dh.reshape(B, S, D)

    big_out = {}
    for n in BIG:
        outs = None
        for l in range(L):
            outs = _sum_adamw(recv_layers[l][n], given[n], given["m_" + n], given["v_" + n], l, outs)
        big_out[n] = outs

    gs = {n: jnp.stack([gs_layers[l][n].reshape(small_shapes[n][1:]) for l in range(L)]) for n in SMALL if n != "hg_lb"}
    gs["hg_lb"] = _lower_bounds_bwd(hg_lb, jnp.concatenate(dlb_layers, axis=0))
    packed = _small_sum_adamw(_pack_small(gs, "", loss_part[0, 0]), _pack_small(given, ""), _pack_small(given, "m_"),
                              _pack_small(given, "v_"))
    small_out, loss = [], None
    for t in packed:
        d, tail = _unpack_small(t, small_shapes)
        small_out.append(d)
        loss = tail if loss is None else loss

    outs = [loss, grad_x]
    for k in range(4):
        outs += [small_out[k][n] if n in small_shapes else big_out[n][k] for n in WEIGHTS]
    return tuple(outs)
```

```python
import functools
import math

import jax
import jax.numpy as jnp
from jax import lax
from jax.experimental import pallas as pl
from jax.experimental.pallas import tpu as pltpu

F32 = jnp.float32
MXU_DTYPE = jnp.bfloat16
VMEM_LIMIT = 48 * 1024 * 1024

N_DEV = 8
EPS = 1e-6
NEG = -1e30
F_MIN = 1e-12
ATT_HEADS = 8
ATT_HEAD_DIM = 64
ATT_W = ATT_HEADS * ATT_HEAD_DIM
DILATIONS = (1, 4, 16)
DIL_STEPS = 128
ATT_Q_BLOCK = 256
ATT_KEY_EXTENT = 512
HG_HEADS = 4
HG_HEAD_DIM = 128
HG_W = HG_HEADS * HG_HEAD_DIM
HG_CHUNK = 32
HG_SUB = 4
MEM_HEADS = 4
LANES = 128
SUBLANES = 8
ROW_BLOCK = 1024
MM_BLOCK = 1024
MM_K_BLOCK = 2048
HG_BLOCK = 256
ADAM_BLOCK = 256

ADAM_LR = 0.001
ADAM_B1 = 0.9
ADAM_B2 = 0.999
ADAM_EPS = 1e-08
ADAM_WD = 0.01
ADAM_STEP = 10

BIG = ("w_in", "w_out", "w_mq", "w_mkv", "w_mo", "w_ff1", "w_ff2")
COL_SHARDED = ("w_in", "w_mkv", "w_ff1")
SMALL = ("norm1_g", "attn_qn_g", "attn_kn_g", "hg_lb", "hg_onorm_g", "norm2_g", "mem_norm_g",
         "mq_norm_g", "mk_norm_g", "norm3_g")
WEIGHTS = ("norm1_g", "w_in", "attn_qn_g", "attn_kn_g", "hg_lb", "hg_onorm_g", "w_out", "norm2_g",
           "mem_norm_g", "w_mq", "w_mkv", "mq_norm_g", "mk_norm_g", "w_mo", "norm3_g", "w_ff1", "w_ff2")
GATHER_MIX = ("w_out", "w_mq", "w_mkv", "w_mo")
GATHER_FF = ("w_ff1", "w_ff2")
SCATTER_A = ("w_ff2",)
SCATTER_B = ("w_ff1", "w_mo", "w_mq", "w_mkv", "w_out")
SCATTER_C = ("w_in",)


def _pcall(body, **kw):
    return pl.pallas_call(body, **kw)


def _params():
    return pltpu.CompilerParams(vmem_limit_bytes=VMEM_LIMIT)


def _tile(n, pref, mult):
    t = (min(n, pref) // mult) * mult
    while t >= mult:
        if n % t == 0:
            return t
        t -= mult
    return n


def _sds(shape, dtype):
    return jax.ShapeDtypeStruct(shape, dtype)


def _mesh_pos():
    return lax.axis_index("x"), lax.axis_index("y"), lax.axis_index("c")


def _flip(v, bit):
    return 1 - v if bit else v


def _gather_hooks(x_refs, out_refs, send_sems, recv_sems, local_sems):
    MESH = pl.DeviceIdType.MESH
    n = len(x_refs)
    x, y, c = _mesh_pos()
    me, sibling = (x, y, c), (x, y, 1 - c)
    chips = [(1 - x, y), (x, 1 - y), (1 - x, 1 - y)]

    def rows(a, px, py, pc):
        return out_refs[a].at[4 * px + 2 * py + pc]

    def copy(a, k, block, to, own=False):
        return pltpu.make_async_remote_copy(
            src_ref=x_refs[a] if own else rows(a, *block), dst_ref=rows(a, *block), send_sem=send_sems.at[n * k + a],
            recv_sem=recv_sems.at[n * k + a], device_id=to, device_id_type=MESH)

    mine = [pltpu.make_async_copy(x_refs[a], rows(a, *me), local_sems.at[a]) for a in range(n)]
    first = [copy(a, 0, me, sibling, own=True) for a in range(n)]
    first += [copy(a, 1 + j, me, (*chip, c), own=True) for j, chip in enumerate(chips) for a in range(n)]
    passed = [[copy(a, 4 + j, (*chip, c), sibling) for a in range(n)] for j, chip in enumerate(chips)]

    def start():
        for cp in mine + first:
            cp.start()

    def forward():
        for j, chip in enumerate(chips):
            for a in range(n):
                copy(a, 1 + j, (*chip, c), me).wait_recv()
                passed[j][a].start()

    def finish():
        for a in range(n):
            copy(a, 0, sibling, me).wait_recv()
        for j, chip in enumerate(chips):
            for a in range(n):
                copy(a, 4 + j, (*chip, 1 - c), me).wait_recv()
        for cp in first + [cp for group in passed for cp in group]:
            cp.wait_send()
        for cp in mine:
            cp.wait()

    return start, forward, finish


def _scatter_hooks(s_refs, r_refs, send_sems, recv_sems, local_sems):
    MESH = pl.DeviceIdType.MESH
    n = len(s_refs)
    x, y, c = _mesh_pos()
    me = 4 * x + 2 * y + c
    mine = [pltpu.make_async_copy(s_refs[a].at[me], r_refs[a].at[me], local_sems.at[a]) for a in range(n)]
    copies = []
    for m in range(1, N_DEV):
        px, py, pc = _flip(x, m & 4), _flip(y, m & 2), _flip(c, m & 1)
        for a in range(n):
            copies.append(pltpu.make_async_remote_copy(
                src_ref=s_refs[a].at[4 * px + 2 * py + pc], dst_ref=r_refs[a].at[me],
                send_sem=send_sems.at[n * (m - 1) + a], recv_sem=recv_sems.at[n * (m - 1) + a],
                device_id=(px, py, pc), device_id_type=MESH))

    def start():
        for cp in mine + copies:
            cp.start()

    def finish():
        for cp in copies:
            cp.wait()
        for cp in mine:
            cp.wait()

    return start, None, finish


def _exchange_sems(n):
    return [pltpu.SemaphoreType.DMA(((N_DEV - 1) * n,)), pltpu.SemaphoreType.DMA(((N_DEV - 1) * n,)),
            pltpu.SemaphoreType.DMA((n,))]


def _exchange_out(kind, xs):
    return _sds((N_DEV,) + xs.shape, xs.dtype) if kind == "gather" else _sds(xs.shape, xs.dtype)


def _run(body, args, *, out_shape, grid, in_specs, out_specs, scratch_shapes=(), name, carry=None, aliases=None):
    single = not isinstance(out_shape, (tuple, list))
    outs = (out_shape,) if single else tuple(out_shape)
    ospecs = (out_specs,) if single else tuple(out_specs)
    aliases = dict(aliases or {})
    if carry is None:
        res = _pcall(body, out_shape=outs, grid=grid, in_specs=list(in_specs), out_specs=ospecs,
                     scratch_shapes=list(scratch_shapes), input_output_aliases=aliases, compiler_params=_params(),
                     name=name)(*args)
        return res[0] if single else tuple(res)
    kind, xs = carry
    n_in, n_out, n_scr, n_x = len(args), len(outs), len(scratch_shapes), len(xs)
    total = math.prod(grid)

    def wrapped(*refs):
        refs = list(refs)
        ins, x_refs = refs[:n_in], refs[n_in:n_in + n_x]
        o, xo_refs = refs[n_in + n_x:n_in + n_x + n_out], refs[n_in + n_x + n_out:n_in + 2 * n_x + n_out]
        scr = refs[n_in + 2 * n_x + n_out:n_in + 2 * n_x + n_out + n_scr]
        sems = refs[n_in + 2 * n_x + n_out + n_scr:]
        step = pl.program_id(0)
        for ax in range(1, len(grid)):
            step = step * grid[ax] + pl.program_id(ax)
        start, forward, finish = (_gather_hooks if kind == "gather" else _scatter_hooks)(x_refs, xo_refs, *sems)
        pl.when(step == 0)(start)
        body(*ins, *o, *scr)
        if forward is not None:
            pl.when(step == (3 * total) // 4)(forward)
        pl.when(step == total - 1)(finish)

    hbm = pl.BlockSpec(memory_space=pl.ANY)
    res = _pcall(wrapped, out_shape=outs + tuple(_exchange_out(kind, t) for t in xs), grid=grid,
                 in_specs=list(in_specs) + [hbm] * n_x, out_specs=ospecs + (hbm,) * n_x,
                 scratch_shapes=list(scratch_shapes) + _exchange_sems(n_x), input_output_aliases=aliases,
                 compiler_params=_params(), name=name)(*args, *xs)
    main = tuple(res[:n_out])
    return (main[0] if single else main), list(res[n_out:])


def _matmul(a, b, *, ta=False, tb=False, epi=None, extra=None, out_dtype=F32, carry=None, name):
    M, K = (a.shape[1], a.shape[0]) if ta else a.shape
    N = b.shape[0] if tb else b.shape[1]
    tm = _tile(M, MM_BLOCK, SUBLANES if not ta else LANES)
    tn = _tile(N, MM_BLOCK, LANES)
    tk = _tile(K, MM_K_BLOCK // 2 if ta else MM_K_BLOCK, LANES if not ta else SUBLANES)
    nk = K // tk
    dims = (((0 if ta else 1,), (1 if tb else 0,)), ((), ()))

    def body(a_ref, b_ref, *rest):
        e_ref = rest[0] if extra is not None else None
        o_ref = rest[1] if extra is not None else rest[0]

        def finish(r):
            if epi == "add":
                r = r + e_ref[...]
            elif epi == "relu2":
                r = jnp.square(jnp.maximum(r, 0.0))
            elif epi == "relu2grad":
                r = r * (2.0 * jnp.sqrt(e_ref[...].astype(F32)))
            o_ref[...] = r.astype(out_dtype)

        part = lax.dot_general(a_ref[...].astype(MXU_DTYPE), b_ref[...].astype(MXU_DTYPE), dims,
                               preferred_element_type=F32)
        if nk == 1:
            finish(part)
        else:
            acc_ref = rest[-1]
            k = pl.program_id(2)

            @pl.when(k == 0)
            def _():
                acc_ref[...] = part

            @pl.when(k > 0)
            def _():
                acc_ref[...] += part

            @pl.when(k == nk - 1)
            def _():
                finish(acc_ref[...])

    a_spec = pl.BlockSpec((tk, tm), lambda i, j, k: (k, i)) if ta else pl.BlockSpec((tm, tk), lambda i, j, k: (i, k))
    b_spec = pl.BlockSpec((tn, tk), lambda i, j, k: (j, k)) if tb else pl.BlockSpec((tk, tn), lambda i, j, k: (k, j))
    o_spec = pl.BlockSpec((tm, tn), lambda i, j, k: (i, j))
    in_specs = [a_spec, b_spec] + ([o_spec] if extra is not None else [])
    args = (a, b) + ((extra,) if extra is not None else ())
    return _run(body, args, out_shape=_sds((M, N), out_dtype), grid=(M // tm, N // tn, nk), in_specs=in_specs,
                out_specs=o_spec, scratch_shapes=[pltpu.VMEM((tm, tn), F32)] if nk > 1 else [], name=name, carry=carry)


def _group_mean(v, E):
    rows, W = v.shape
    if E == W:
        return jnp.mean(v, axis=-1, keepdims=True)
    pieces = []
    if E % LANES == 0:
        for g0 in range(0, W, E):
            m = jnp.mean(v[:, g0:g0 + E], axis=-1, keepdims=True)
            pieces.append(jnp.broadcast_to(m, (rows, E)))
    else:
        lane = lax.broadcasted_iota(jnp.int32, (rows, LANES), 1)
        for c0 in range(0, W, LANES):
            vc = v[:, c0:c0 + LANES]
            acc = jnp.zeros((rows, LANES), F32)
            for s0 in range(0, LANES, E):
                msk = (lane >= s0) & (lane < s0 + E)
                m = jnp.sum(jnp.where(msk, vc, 0.0), axis=-1, keepdims=True) * (1.0 / E)
                acc = jnp.where(msk, m, acc)
            pieces.append(acc)
    return jnp.concatenate(pieces, axis=-1)


def _fold_groups(t, E):
    W = t.shape[1]
    step = max(E, LANES)
    acc = t[:, 0:step]
    for c0 in range(step, W, step):
        acc = acc + t[:, c0:c0 + step]
    sh = LANES // 2
    while sh >= E:
        acc = acc + pltpu.roll(acc, sh, 1)
        sh //= 2
    return acc


def _norm_fwd(x, g, *, E, W, cb=0, out_dtype, carry=None, name):
    M = x.shape[0]
    tm = _tile(M, ROW_BLOCK, SUBLANES)

    def body(x_ref, g_ref, o_ref):
        xv = x_ref[...]
        r = lax.rsqrt(_group_mean(xv * xv, E) + EPS)
        o_ref[...] = (xv * r * g_ref[...]).astype(out_dtype)

    return _run(body, (x, g), out_shape=_sds((M, W), out_dtype), grid=(M // tm,),
                in_specs=[pl.BlockSpec((tm, W), lambda i: (i, cb)), pl.BlockSpec((1, W), lambda i: (0, 0))],
                out_specs=pl.BlockSpec((tm, W), lambda i: (i, 0)), name=name, carry=carry)


def _norm_bwd(x, g, dy, *, E, W, cb=0, res=None, fold=None, out_dtype=F32, low=False, into=None, name):
    M = x.shape[0]
    tm = _tile(M, ROW_BLOCK, SUBLANES)
    n = M // tm
    gw = W if fold is None else max(fold, LANES)

    def body(x_ref, g_ref, dy_ref, *rest):
        rest = list(rest)
        r_ref = rest.pop(0) if res is not None else None
        if into is not None:
            rest.pop(0)
        dx_ref = rest.pop(0)
        dxl_ref = rest.pop(0) if low else None
        dg_ref, acc_ref = rest
        i = pl.program_id(0)
        xv = x_ref[...]
        r = lax.rsqrt(_group_mean(xv * xv, E) + EPS)
        xh = xv * r
        dyv = dy_ref[...].astype(F32)
        dyg = dyv * g_ref[...]
        dx = r * (dyg - xh * _group_mean(dyg * xh, E))
        if res is not None:
            dx = dx + r_ref[...]
        dx_ref[...] = dx.astype(dx_ref.dtype)
        if low:
            dxl_ref[...] = dx.astype(MXU_DTYPE)
        part = jnp.sum(dyv * xh, axis=0, keepdims=True)

        @pl.when(i == 0)
        def _():
            acc_ref[...] = part

        @pl.when(i > 0)
        def _():
            acc_ref[...] += part

        @pl.when(i == n - 1)
        def _():
            t = acc_ref[...]
            dg_ref[...] = t if fold is None else _fold_groups(t, fold)

    blk = pl.BlockSpec((tm, W), lambda i: (i, 0))
    in_specs = [pl.BlockSpec((tm, W), lambda i: (i, cb)), pl.BlockSpec((1, W), lambda i: (0, 0)), blk]
    args = [x, g, dy]
    if res is not None:
        in_specs.append(blk)
        args.append(res)
    lows = ((_sds((M, W), MXU_DTYPE),), (blk,)) if low else ((), ())
    dx_out, dx_spec, aliases = _sds((M, W), out_dtype), blk, None
    if into is not None:
        aliases = {len(args): 0}
        args.append(into)
        in_specs.append(pl.BlockSpec(memory_space=pl.ANY))
        dx_out, dx_spec = _sds(into.shape, into.dtype), pl.BlockSpec((tm, W), lambda i: (i, cb))
    return _run(body, tuple(args), out_shape=(dx_out,) + lows[0] + (_sds((1, gw), F32),), grid=(n,),
                in_specs=in_specs, out_specs=(dx_spec,) + lows[1] + (pl.BlockSpec((1, gw), lambda i: (0, 0)),),
                scratch_shapes=[pltpu.VMEM((1, W), F32)], name=name, aliases=aliases)


def _attn_tables(S, tq):
    nq = S // tq

    def body(o_ref):
        i = pl.program_id(0)
        d = (i * tq + lax.broadcasted_iota(jnp.int32, (tq, S), 0)) - lax.broadcasted_iota(jnp.int32, (tq, S), 1)
        cnt = jnp.zeros((tq, S), jnp.int32)
        for dil in DILATIONS:
            hit = (d <= DIL_STEPS * dil) if dil == 1 else (((d & (dil - 1)) == 0) & (d <= DIL_STEPS * dil))
            cnt = cnt + hit.astype(jnp.int32)
        ok = (d >= 0) & (cnt > 0)
        logm = jnp.where(cnt == 3, math.log(3.0), jnp.where(cnt == 2, math.log(2.0), 0.0))
        o_ref[0] = jnp.where(ok, logm, NEG).astype(F32)

    return _run(body, (), out_shape=_sds((nq, tq, S), F32), grid=(nq,), in_specs=[],
                out_specs=pl.BlockSpec((1, tq, S), lambda i: (i, 0, 0)), name="dilated_attn_tables")


def _alibi_slope(h):
    return 2.0 ** (-8.0 / ATT_HEADS * (h + 1))


def _key_positions(Sk):
    kpos = lax.broadcasted_iota(jnp.int32, (Sk, LANES), 0)
    return (kpos >> 8).astype(F32).astype(MXU_DTYPE), (kpos & 255).astype(F32).astype(MXU_DTYPE)


def _score_operands(q2s, k2, pos_hi, pos_lo, lane, sub, h):
    own = (lane < ATT_HEAD_DIM) if sub == 0 else (lane >= ATT_HEAD_DIM)
    spare = ATT_HEAD_DIM if sub == 0 else 0
    slope = _alibi_slope(h)
    terms = jnp.where(lane == spare, slope * 256.0, jnp.where(lane == spare + 1, slope, 0.0)).astype(q2s.dtype)
    qa = jnp.where(own, q2s, terms)
    ka = jnp.where(lane == spare, pos_hi, jnp.where(lane == spare + 1, pos_lo, k2))
    return own, qa, ka


def _key_extents(S, tq):
    ext = min(ATT_KEY_EXTENT, S)
    return ext, ext // tq, S // ext


def _attn_fwd(tab, qn, kn, z, *, B, S, carry=None, name):
    T = B * S
    tq = tab.shape[1]
    nq = S // tq
    ext, per, n_ext = _key_extents(S, tq)
    scale = 1.0 / math.sqrt(ATT_HEAD_DIM)
    dn_t = (((1,), (1,)), ((), ()))

    def body(tab_ref, q_ref, k_ref, v_ref, y_ref, lse_ref, ycat_ref):
        i = pl.program_id(1)
        lane = lax.broadcasted_iota(jnp.int32, (1, LANES), 1)
        lo = lane < ATT_HEAD_DIM

        def visit(Sk):
            base = tab_ref[0, :, :Sk]
            pos_hi, pos_lo = _key_positions(Sk)
            lse_blk = jnp.zeros((tq, LANES), F32)
            for p in range(ATT_HEADS // 2):
                cs = slice(p * LANES, (p + 1) * LANES)
                q2s = q_ref[:, cs].astype(MXU_DTYPE) * scale
                k2 = k_ref[:Sk, cs].astype(MXU_DTYPE)
                v2 = v_ref[:Sk, cs].astype(MXU_DTYPE)
                outs = []
                for sub in range(2):
                    h = 2 * p + sub
                    _, qa, ka = _score_operands(q2s, k2, pos_hi, pos_lo, lane, sub, h)
                    s = lax.dot_general(qa, ka, dn_t, preferred_element_type=F32) + base
                    mx = jnp.max(s, axis=1, keepdims=True)
                    e = jnp.exp(s - mx)
                    l = jnp.sum(e, axis=1, keepdims=True)
                    outs.append(jnp.dot(e.astype(MXU_DTYPE), v2, preferred_element_type=F32) / l)
                    lse_blk = jnp.where(lane == h, mx + jnp.log(l), lse_blk)
                y = jnp.where(lo, outs[0], outs[1])
                y_ref[:, cs] = y
                ycat_ref[:, cs] = y.astype(MXU_DTYPE)
            lse_ref[...] = lse_blk

        for e in range(n_ext):
            pl.when(i // per == e)(functools.partial(visit, (e + 1) * ext))

    return _run(
        body, (tab, qn, kn, z), grid=(B, nq),
        out_shape=(_sds((T, ATT_W), F32), _sds((T, LANES), F32), _sds((T, ATT_W + HG_W), MXU_DTYPE)),
        in_specs=[pl.BlockSpec((1, tq, S), lambda b, i: (i, 0, 0)),
                  pl.BlockSpec((tq, ATT_W), lambda b, i: (b * nq + i, 0)),
                  pl.BlockSpec((S, ATT_W), lambda b, i: (b, 0)),
                  pl.BlockSpec((S, ATT_W), lambda b, i: (b, 2))],
        out_specs=(pl.BlockSpec((tq, ATT_W), lambda b, i: (b * nq + i, 0)),
                   pl.BlockSpec((tq, LANES), lambda b, i: (b * nq + i, 0)),
                   pl.BlockSpec((tq, ATT_W), lambda b, i: (b * nq + i, 0))),
        name=name, carry=carry)


def _attn_bwd(tab, qn, kn, z, y, lse, dycat, dz, *, B, S, carry=None, name):
    T = B * S
    tq = tab.shape[1]
    nq = S // tq
    ext, per, n_ext = _key_extents(S, tq)
    scale = 1.0 / math.sqrt(ATT_HEAD_DIM)
    dn_t = (((1,), (1,)), ((), ()))
    dn_o = (((0,), (0,)), ((), ()))

    def body(tab_ref, q_ref, k_ref, v_ref, y_ref, lse_ref, dy_ref, _, dq_ref, dk_ref, dz_ref, dv_ref):
        i = pl.program_id(1)

        @pl.when(i == 0)
        def _():
            dk_ref[...] = jnp.zeros_like(dk_ref)
            dv_ref[...] = jnp.zeros_like(dv_ref)

        lane = lax.broadcasted_iota(jnp.int32, (1, LANES), 1)
        lo = lane < ATT_HEAD_DIM

        def visit(Sk):
            base = tab_ref[0, :, :Sk]
            pos_hi, pos_lo = _key_positions(Sk)
            lse_blk = lse_ref[...]
            for p in range(ATT_HEADS // 2):
                cs = slice(p * LANES, (p + 1) * LANES)
                q2 = q_ref[:, cs].astype(MXU_DTYPE)
                q2s = q2 * scale
                k2 = k_ref[:Sk, cs].astype(MXU_DTYPE)
                v2 = v_ref[:Sk, cs].astype(MXU_DTYPE)
                do2 = dy_ref[:, cs]
                doy = do2 * y_ref[:, cs]
                do2m = do2.astype(MXU_DTYPE)
                dqs, dks, dvs = [], [], []
                for sub in range(2):
                    h = 2 * p + sub
                    own, qa, ka = _score_operands(q2s, k2, pos_hi, pos_lo, lane, sub, h)
                    s = lax.dot_general(qa, ka, dn_t, preferred_element_type=F32) + base
                    lse_h = jnp.sum(jnp.where(lane == h, lse_blk, 0.0), axis=1, keepdims=True)
                    pr = jnp.exp(s - lse_h)
                    dsum = jnp.sum(jnp.where(own, doy, 0.0), axis=1, keepdims=True)
                    dom = jnp.where(own, do2m, jnp.zeros_like(do2m))
                    dp = lax.dot_general(dom, v2, dn_t, preferred_element_type=F32)
                    ds = (pr * (dp - dsum)).astype(MXU_DTYPE)
                    dqs.append(jnp.dot(ds, k2, preferred_element_type=F32))
                    dks.append(lax.dot_general(ds, q2, dn_o, preferred_element_type=F32))
                    dvs.append(lax.dot_general(pr.astype(MXU_DTYPE), do2m, dn_o, preferred_element_type=F32))
                dq_ref[:, cs] = jnp.where(lo, dqs[0], dqs[1]) * scale
                dk_ref[:Sk, cs] += jnp.where(lo, dks[0], dks[1]) * scale
                dv_ref[:Sk, cs] += jnp.where(lo, dvs[0], dvs[1])

        for e in range(n_ext):
            pl.when(i // per == e)(functools.partial(visit, (e + 1) * ext))

        @pl.when(i == nq - 1)
        def _():
            dz_ref[...] = dv_ref[...].astype(dz_ref.dtype)

    qblk = pl.BlockSpec((tq, ATT_W), lambda b, i: (b * nq + i, 0))
    sblk = pl.BlockSpec((S, ATT_W), lambda b, i: (b, 0))
    vblk = pl.BlockSpec((S, ATT_W), lambda b, i: (b, 2))
    return _run(
        body, (tab, qn, kn, z, y, lse, dycat, dz), grid=(B, nq),
        out_shape=(_sds((T, ATT_W), F32), _sds((T, ATT_W), F32), _sds(dz.shape, dz.dtype)),
        in_specs=[pl.BlockSpec((1, tq, S), lambda b, i: (i, 0, 0)), qblk, sblk, vblk, qblk,
                  pl.BlockSpec((tq, LANES), lambda b, i: (b * nq + i, 0)), qblk, pl.BlockSpec(memory_space=pl.ANY)],
        out_specs=(qblk, sblk, vblk), scratch_shapes=[pltpu.VMEM((S, ATT_W), F32)], name=name, carry=carry,
        aliases={7: 2})


def _sigmoid_pair(x):
    en = jnp.exp(-jnp.abs(x))
    big = 1.0 / (1.0 + en)
    small = en * big
    pos = x >= 0
    return jnp.where(pos, big, small), jnp.where(pos, small, big)


def _chunk_scan(x, pos, reverse):
    n = x.shape[0]
    sh = 1
    while sh < HG_CHUNK:
        if reverse:
            x = x + jnp.where(pos < HG_CHUNK - sh, pltpu.roll(x, n - sh, 0), 0.0)
        else:
            x = x + jnp.where(pos >= sh, pltpu.roll(x, sh, 0), 0.0)
        sh *= 2
    return x


def _hgrn_pair_table(tb):
    def body(o_ref):
        t = lax.broadcasted_iota(jnp.int32, (tb, tb), 0)
        s = lax.broadcasted_iota(jnp.int32, (tb, tb), 1)
        o_ref[...] = jnp.where((t // HG_CHUNK) == (s // HG_CHUNK), (t // HG_SUB) - (s // HG_SUB), 0).astype(F32)

    return _pcall(body, out_shape=_sds((tb, tb), F32), name="hgrn_pair_table")()


def _hgrn_cross_decays(b, pos):
    n = b.shape[0]
    pos8 = pos % HG_SUB
    end = b
    sh = 1
    while sh < HG_SUB:
        end = jnp.where((pos8 & sh) == 0, pltpu.roll(end, n - sh, 0), end)
        sh *= 2
    x = jnp.where(pos >= HG_SUB, jnp.exp(jnp.minimum(b - pltpu.roll(end, HG_SUB, 0), 0.0)), 0.0)
    ys = []
    for j in range(1, HG_CHUNK // HG_SUB):
        end_j = end if j == 1 else pltpu.roll(end, n - HG_SUB * (j - 1), 0)
        ys.append(jnp.where(pos < HG_CHUNK - HG_SUB * j, jnp.exp(jnp.minimum(end_j - b, 0.0)), 0.0))
    return x, ys


def _hgrn_cross_scores(qt, kts, pair):
    n = qt.shape[0]
    kcat = jnp.concatenate([k.astype(MXU_DTYPE) for k in kts], axis=0)
    p = lax.dot_general(qt.astype(MXU_DTYPE), kcat, (((1,), (1,)), ((), ())), preferred_element_type=F32)
    a = jnp.zeros((n, n), F32)
    for j in range(1, len(kts) + 1):
        a = jnp.where(pair == j, p[:, (j - 1) * n:j * n], a)
    return a, kcat


def _hgrn_gates(qh, fh, lb):
    sq, _ = _sigmoid_pair(qh)
    sg, sgn = _sigmoid_pair(fh)
    f = lb + (1.0 - lb) * sg
    g = jnp.log(jnp.maximum(f, F_MIN))
    kk = (1.0 - lb) * sgn
    return qh * sq, sq, sg, sgn, f, g, kk


def _hgrn_fwd(z, lb, pair, *, B, S, name):
    T = B * S
    tb = _tile(S, HG_BLOCK, HG_CHUNK)
    nc = tb // HG_CHUNK
    nb = S // tb
    dn_t = (((1,), (1,)), ((), ()))
    dn_o = (((0,), (0,)), ((), ()))

    def zcol(first):
        return lambda h, b, i: (b * nb + i, first + h)

    def body(q_ref, f_ref, v_ref, lb_ref, pair_ref, o_ref, st_ref, state):
        i = pl.program_id(2)

        @pl.when(i == 0)
        def _():
            state[...] = jnp.zeros_like(state)

        q, _, _, _, _, g, kk = _hgrn_gates(q_ref[...], f_ref[...], lb_ref[...])
        v = v_ref[...]
        pos = lax.broadcasted_iota(jnp.int32, (tb, LANES), 0) % HG_CHUNK
        pos8 = pos % HG_SUB
        b = _chunk_scan(g, pos, False)
        o = jnp.sum(q * kk, axis=1, keepdims=True) * v
        for d in range(1, HG_SUB):
            qd = pltpu.roll(q, tb - d, 0)
            bd = pltpu.roll(b, tb - d, 0)
            e = jnp.where(pos8 < HG_SUB - d, jnp.exp(jnp.minimum(bd - b, 0.0)), 0.0)
            a = jnp.sum(qd * kk * e, axis=1, keepdims=True)
            o = o + pltpu.roll(a * v, d, 0)
        x, ys = _hgrn_cross_decays(b, pos)
        a_cross, _ = _hgrn_cross_scores(q * x, [kk * y for y in ys], pair_ref[...])
        o = o + jnp.dot(a_cross.astype(MXU_DTYPE), v.astype(MXU_DTYPE), preferred_element_type=F32)
        qe = q * jnp.exp(b)
        chunks = [slice(c * HG_CHUNK, (c + 1) * HG_CHUNK) for c in range(nc)]
        b_last = [jnp.sum(jnp.where(pos[rs] == HG_CHUNK - 1, b[rs], 0.0), axis=0, keepdims=True) for rs in chunks]
        own = [lax.dot_general(v[rs].astype(MXU_DTYPE), (kk[rs] * jnp.exp(bl - b[rs])).astype(MXU_DTYPE), dn_o,
                               preferred_element_type=F32) for rs, bl in zip(chunks, b_last)]
        st = state[...]
        before = []
        for c in range(nc):
            before.append(st)
            st = st * jnp.exp(b_last[c]) + own[c]
        state[...] = st
        for c, rs in enumerate(chunks):
            st_ref[c * LANES:(c + 1) * LANES, :] = before[c]
            o_ref[rs, :] = o[rs] + lax.dot_general(qe[rs].astype(MXU_DTYPE), before[c].astype(MXU_DTYPE), dn_t,
                                                   preferred_element_type=F32)

    blk = (tb, LANES)
    return _run(
        body, (z, z, z, lb, pair), out_shape=(_sds((T, HG_W), F32), _sds((T // HG_CHUNK * LANES, HG_W), F32)),
        grid=(HG_HEADS, B, nb),
        in_specs=[pl.BlockSpec(blk, zcol(12)), pl.BlockSpec(blk, zcol(16)), pl.BlockSpec(blk, zcol(20)),
                  pl.BlockSpec((1, LANES), lambda h, b, i: (0, h)), pl.BlockSpec((tb, tb), lambda h, b, i: (0, 0))],
        out_specs=(pl.BlockSpec(blk, lambda h, b, i: (b * nb + i, h)),
                   pl.BlockSpec((nc * LANES, LANES), lambda h, b, i: (b * nb + i, h))),
        scratch_shapes=[pltpu.VMEM((LANES, LANES), F32)], name=name)


def _hgrn_bwd(z, lb, pair, states, do, *, B, S, name):
    T = B * S
    tb = _tile(S, HG_BLOCK, HG_CHUNK)
    nc = tb // HG_CHUNK
    nb = S // tb
    dn_t = (((1,), (1,)), ((), ()))
    dn_o = (((0,), (0,)), ((), ()))

    def body(q_ref, f_ref, v_ref, lb_ref, pair_ref, st_ref, do_ref, dq_ref, df_ref, di_ref, dlb_ref, dstate):
        bi = pl.program_id(1)
        i = pl.program_id(2)

        @pl.when(i == 0)
        def _():
            dstate[...] = jnp.zeros_like(dstate)

        qh = q_ref[...]
        lbv = lb_ref[...]
        q, sq, sg, sgn, f, g, kk = _hgrn_gates(qh, f_ref[...], lbv)
        v = v_ref[...]
        dov = do_ref[...]
        pos = lax.broadcasted_iota(jnp.int32, (tb, LANES), 0) % HG_CHUNK
        pos8 = pos % HG_SUB
        b = _chunk_scan(g, pos, False)

        a0 = jnp.sum(q * kk, axis=1, keepdims=True)
        da0 = jnp.sum(dov * v, axis=1, keepdims=True)
        dv = a0 * dov
        dkk = da0 * q
        dq = da0 * kk
        db = jnp.zeros((tb, LANES), F32)
        for d in range(1, HG_SUB):
            qd = pltpu.roll(q, tb - d, 0)
            bd = pltpu.roll(b, tb - d, 0)
            dod = pltpu.roll(dov, tb - d, 0)
            e = jnp.where(pos8 < HG_SUB - d, jnp.exp(jnp.minimum(bd - b, 0.0)), 0.0)
            a = jnp.sum(qd * kk * e, axis=1, keepdims=True)
            da = jnp.sum(dod * v, axis=1, keepdims=True)
            dv = dv + a * dod
            xe = da * e
            dkk = dkk + xe * qd
            t1 = xe * kk
            wt = t1 * qd
            dq = dq + pltpu.roll(t1, d, 0)
            db = db + pltpu.roll(wt, d, 0) - wt

        n_cross = HG_CHUNK // HG_SUB - 1
        pair = pair_ref[...]
        x, ys = _hgrn_cross_decays(b, pos)
        qt = q * x
        kts = [kk * y for y in ys]
        a_cross, kcat = _hgrn_cross_scores(qt, kts, pair)
        dom = dov.astype(MXU_DTYPE)
        da_all = lax.dot_general(dom, v.astype(MXU_DTYPE), dn_t, preferred_element_type=F32)
        dv = dv + lax.dot_general(a_cross.astype(MXU_DTYPE), dom, dn_o, preferred_element_type=F32)
        gcat = jnp.concatenate([jnp.where(pair == j, da_all, 0.0).astype(MXU_DTYPE) for j in range(1, n_cross + 1)],
                               axis=1)
        dqt = jnp.dot(gcat, kcat, preferred_element_type=F32)
        dkcat = lax.dot_general(gcat, qt.astype(MXU_DTYPE), dn_o, preferred_element_type=F32)
        dq = dq + dqt * x
        u = dqt * qt
        db = db + u
        d_end = -pltpu.roll(u, tb - HG_SUB, 0)
        for j in range(1, n_cross + 1):
            dkt = dkcat[(j - 1) * tb:j * tb]
            dkk = dkk + dkt * ys[j - 1]
            w = dkt * kts[j - 1]
            db = db - w
            d_end = d_end + (w if j == 1 else pltpu.roll(w, HG_SUB * (j - 1), 0))
        sh = 1
        while sh < HG_SUB:
            d_end = d_end + jnp.where(pos8 >= sh, pltpu.roll(d_end, sh, 0), 0.0)
            sh *= 2
        db = db + jnp.where(pos8 == HG_SUB - 1, d_end, 0.0)

        eb = jnp.exp(b)
        qe = q * eb
        chunks = [slice(c * HG_CHUNK, (c + 1) * HG_CHUNK) for c in range(nc)]
        b_lasts = [jnp.sum(jnp.where(pos[rs] == HG_CHUNK - 1, b[rs], 0.0), axis=0, keepdims=True) for rs in chunks]
        own = [lax.dot_general(dom[rs], qe[rs].astype(MXU_DTYPE), dn_o, preferred_element_type=F32) for rs in chunks]
        dst = dstate[...]
        after = [None] * nc
        for c in range(nc - 1, -1, -1):
            after[c] = dst
            dst = dst * jnp.exp(b_lasts[c]) + own[c]
        dstate[...] = dst
        dq_c, dkk_c, dv_c, db_c = [None] * nc, [None] * nc, [None] * nc, [None] * nc
        for c, rs in enumerate(chunks):
            st0 = st_ref[c * LANES:(c + 1) * LANES, :]
            eb_last = jnp.exp(b_lasts[c])
            er = jnp.exp(b_lasts[c] - b[rs])
            ke = kk[rs] * er
            dstm = after[c].astype(MXU_DTYPE)
            dqe = jnp.dot(dom[rs], st0.astype(MXU_DTYPE), preferred_element_type=F32)
            dv_c[c] = dv[rs] + lax.dot_general(ke.astype(MXU_DTYPE), dstm, dn_t, preferred_element_type=F32)
            dke = jnp.dot(v[rs].astype(MXU_DTYPE), dstm, preferred_element_type=F32)
            d_eb_last = jnp.sum(after[c] * st0, axis=0, keepdims=True)
            dq_c[c] = dq[rs] + dqe * eb[rs]
            dkk_c[c] = dkk[rs] + dke * er
            dkeke = dke * ke
            db_last = jnp.sum(dkeke, axis=0, keepdims=True) + d_eb_last * eb_last
            db_c[c] = db[rs] + dqe * qe[rs] - dkeke + jnp.where(pos[rs] == HG_CHUNK - 1, db_last, 0.0)
        dq = jnp.concatenate(dq_c, axis=0)
        dkk = jnp.concatenate(dkk_c, axis=0)
        dv = jnp.concatenate(dv_c, axis=0)
        dg = _chunk_scan(jnp.concatenate(db_c, axis=0), pos, True)

        dfv = dg * jnp.where(f > F_MIN, 1.0 / f, 0.0)
        dsg = (dfv - dkk) * (1.0 - lbv)
        dq_ref[...] = (dq * (sq * (1.0 + qh * (1.0 - sq)))).astype(dq_ref.dtype)
        df_ref[...] = (dsg * sg * sgn).astype(df_ref.dtype)
        di_ref[...] = dv.astype(di_ref.dtype)
        part = jnp.sum(dfv * sgn - dkk * sgn, axis=0, keepdims=True)

        @pl.when((bi == 0) & (i == 0))
        def _():
            dlb_ref[...] = part

        @pl.when((bi > 0) | (i > 0))
        def _():
            dlb_ref[...] += part

    def rev(first):
        return lambda h, b, i: (b * nb + (nb - 1 - i), first + h)

    blk = (tb, LANES)
    oblk = pl.BlockSpec(blk, rev(0))
    return _run(
        body, (z, z, z, lb, pair, states, do), out_shape=(_sds((T, HG_W), MXU_DTYPE),) * 3 + (_sds((1, HG_W), F32),),
        grid=(HG_HEADS, B, nb),
        in_specs=[pl.BlockSpec(blk, rev(12)), pl.BlockSpec(blk, rev(16)), pl.BlockSpec(blk, rev(20)),
                  pl.BlockSpec((1, LANES), lambda h, b, i: (0, h)), pl.BlockSpec((tb, tb), lambda h, b, i: (0, 0)),
                  pl.BlockSpec((nc * LANES, LANES), rev(0)), oblk],
        out_specs=(oblk, oblk, oblk, pl.BlockSpec((1, LANES), lambda h, b, i: (0, h))),
        scratch_shapes=[pltpu.VMEM((LANES, LANES), F32)], name=name)


def _hgout_fwd(o, z, g, ycat, *, name):
    T = o.shape[0]
    tm = _tile(T, ROW_BLOCK, SUBLANES)

    def body(o_ref, gh_ref, g_ref, _, y_ref):
        ov = o_ref[...]
        gh = gh_ref[...]
        r = lax.rsqrt(_group_mean(ov * ov, HG_HEAD_DIM) + EPS)
        sg, _ = _sigmoid_pair(gh)
        y_ref[...] = (ov * r * g_ref[...] * (gh * sg)).astype(y_ref.dtype)

    blk = pl.BlockSpec((tm, HG_W), lambda i: (i, 0))
    return _run(body, (o, z, g, ycat), out_shape=_sds(ycat.shape, ycat.dtype), grid=(T // tm,),
                in_specs=[blk, pl.BlockSpec((tm, HG_W), lambda i: (i, 6)), pl.BlockSpec((1, HG_W), lambda i: (0, 0)),
                          pl.BlockSpec(memory_space=pl.ANY)],
                out_specs=pl.BlockSpec((tm, HG_W), lambda i: (i, ATT_W // HG_W)), name=name, aliases={3: 0})


def _hgout_bwd(o, z, g, dycat, *, name):
    T = o.shape[0]
    tm = _tile(T, ROW_BLOCK, SUBLANES)

    def body(o_ref, gh_ref, g_ref, dy_ref, do_ref, dgh_ref, dg_ref):
        i = pl.program_id(0)
        ov = o_ref[...]
        gh = gh_ref[...]
        gv = g_ref[...]
        dy = dy_ref[...]
        r = lax.rsqrt(_group_mean(ov * ov, HG_HEAD_DIM) + EPS)
        xh = ov * r
        sg, _ = _sigmoid_pair(gh)
        dn = dy * (gh * sg)
        dgh_ref[...] = (dy * xh * gv * (sg * (1.0 + gh * (1.0 - sg)))).astype(dgh_ref.dtype)
        dng = dn * gv
        do_ref[...] = r * (dng - xh * _group_mean(dng * xh, HG_HEAD_DIM))
        part = jnp.sum(dn * xh, axis=0, keepdims=True)

        @pl.when(i == 0)
        def _():
            dg_ref[...] = part

        @pl.when(i > 0)
        def _():
            dg_ref[...] += part

    blk = pl.BlockSpec((tm, HG_W), lambda i: (i, 0))
    row = pl.BlockSpec((1, HG_W), lambda i: (0, 0))
    return _run(body, (o, z, g, dycat),
                out_shape=(_sds((T, HG_W), F32), _sds(z.shape, MXU_DTYPE), _sds((1, HG_W), F32)), grid=(T // tm,),
                in_specs=[blk, pl.BlockSpec((tm, HG_W), lambda i: (i, 6)), row,
                          pl.BlockSpec((tm, HG_W), lambda i: (i, 1))],
                out_specs=(blk, pl.BlockSpec((tm, HG_W), lambda i: (i, 6)), row), name=name)


def _lower_bounds_fwd(hg_lb):
    L = hg_lb.shape[0]

    def body(x_ref, o_ref):
        rows = [x_ref[l:l + 1, :] for l in range(L)]
        mx = functools.reduce(jnp.maximum, rows)
        es = [jnp.exp(r - mx) for r in rows]
        tot = functools.reduce(jnp.add, es)
        acc = jnp.zeros_like(tot)
        for l in range(L):
            acc = acc + es[l] / tot
            o_ref[l:l + 1, :] = acc - es[0] / tot

    return _pcall(body, out_shape=_sds(hg_lb.shape, F32), name="lower_bounds_fwd")(hg_lb)


def _lower_bounds_bwd(hg_lb, dlower):
    L = hg_lb.shape[0]

    def body(x_ref, d_ref, o_ref):
        rows = [x_ref[l:l + 1, :] for l in range(L)]
        ds = [d_ref[l:l + 1, :] for l in range(L)]
        mx = functools.reduce(jnp.maximum, rows)
        es = [jnp.exp(r - mx) for r in rows]
        tot = functools.reduce(jnp.add, es)
        ps = [e / tot for e in es]
        dps = []
        for j in range(L):
            t = functools.reduce(jnp.add, ds[j:])
            if j == 0:
                t = t - functools.reduce(jnp.add, ds)
            dps.append(t)
        inner = functools.reduce(jnp.add, [p * dp for p, dp in zip(ps, dps)])
        for j in range(L):
            o_ref[j:j + 1, :] = ps[j] * (dps[j] - inner)

    return _pcall(body, out_shape=_sds(hg_lb.shape, F32), name="lower_bounds_bwd")(hg_lb, dlower)


def _memattn_fwd(qm, km, kv, *, B, S, NM, name):
    T, D = qm.shape
    E = D // MEM_HEADS
    tq = _tile(S, ROW_BLOCK, SUBLANES)
    nq = S // tq
    scale = 1.0 / math.sqrt(E)

    def body(q_ref, k_ref, v_ref, o_ref):
        for h in range(MEM_HEADS):
            cs = slice(h * E, (h + 1) * E)
            s = lax.dot_general(q_ref[:, cs].astype(MXU_DTYPE), k_ref[:, cs].astype(MXU_DTYPE),
                                (((1,), (1,)), ((), ())), preferred_element_type=F32) * scale
            e = jnp.exp(s - jnp.max(s, axis=1, keepdims=True))
            pr = e / jnp.sum(e, axis=1, keepdims=True)
            o_ref[:, cs] = jnp.dot(pr.astype(MXU_DTYPE), v_ref[:, cs].astype(MXU_DTYPE),
                                   preferred_element_type=F32).astype(o_ref.dtype)

    return _run(
        body, (qm, km, kv), out_shape=_sds((T, D), MXU_DTYPE), grid=(B, nq),
        in_specs=[pl.BlockSpec((tq, D), lambda b, i: (b * nq + i, 0)), pl.BlockSpec((NM, D), lambda b, i: (b, 0)),
                  pl.BlockSpec((NM, D), lambda b, i: (b, 1))],
        out_specs=pl.BlockSpec((tq, D), lambda b, i: (b * nq + i, 0)), name=name)


def _memattn_bwd(qm, km, kv, do, *, B, S, NM, name):
    T, D = qm.shape
    E = D // MEM_HEADS
    tq = _tile(S, ROW_BLOCK, SUBLANES)
    nq = S // tq
    scale = 1.0 / math.sqrt(E)

    def body(q_ref, k_ref, v_ref, do_ref, dq_ref, dk_ref, dv_ref):
        i = pl.program_id(1)

        @pl.when(i == 0)
        def _():
            dk_ref[...] = jnp.zeros_like(dk_ref)
            dv_ref[...] = jnp.zeros_like(dv_ref)

        for h in range(MEM_HEADS):
            cs = slice(h * E, (h + 1) * E)
            qh = q_ref[:, cs].astype(MXU_DTYPE)
            kh = k_ref[:, cs].astype(MXU_DTYPE)
            vh = v_ref[:, cs].astype(MXU_DTYPE)
            doh = do_ref[:, cs].astype(MXU_DTYPE)
            s = lax.dot_general(qh, kh, (((1,), (1,)), ((), ())), preferred_element_type=F32) * scale
            e = jnp.exp(s - jnp.max(s, axis=1, keepdims=True))
            pr = e / jnp.sum(e, axis=1, keepdims=True)
            dp = lax.dot_general(doh, vh, (((1,), (1,)), ((), ())), preferred_element_type=F32)
            ds = (pr * (dp - jnp.sum(dp * pr, axis=1, keepdims=True))).astype(MXU_DTYPE)
            dq_ref[:, cs] = jnp.dot(ds, kh, preferred_element_type=F32) * scale
            dk_ref[:, cs] += lax.dot_general(ds, qh, (((0,), (0,)), ((), ())), preferred_element_type=F32) * scale
            dv_ref[:, cs] += lax.dot_general(pr.astype(MXU_DTYPE), doh, (((0,), (0,)), ((), ())),
                                             preferred_element_type=F32)

    qblk = pl.BlockSpec((tq, D), lambda b, i: (b * nq + i, 0))
    mblk = pl.BlockSpec((NM, D), lambda b, i: (b, 0))
    return _run(
        body, (qm, km, kv, do), out_shape=(_sds((T, D), F32), _sds((B * NM, D), F32), _sds((B * NM, D), F32)),
        grid=(B, nq), in_specs=[qblk, mblk, pl.BlockSpec((NM, D), lambda b, i: (b, 1)), qblk],
        out_specs=(qblk, mblk, mblk), name=name)


def _loss_head(y, tgt):
    T, D = y.shape
    tm = _tile(T, ROW_BLOCK, SUBLANES)

    def body(y_ref, t_ref, dy_ref, dyl_ref, l_ref):
        i = pl.program_id(0)
        err = y_ref[...] - t_ref[...]
        dy_ref[...] = err * (1.0 / D)
        dyl_ref[...] = (err * (1.0 / D)).astype(MXU_DTYPE)
        part = jnp.sum(jnp.sum(err * err, axis=1, keepdims=True), axis=0, keepdims=True) * (0.5 / D)

        @pl.when(i == 0)
        def _():
            l_ref[...] = part

        @pl.when(i > 0)
        def _():
            l_ref[...] += part

    blk = pl.BlockSpec((tm, D), lambda i: (i, 0))
    return _run(body, (y, tgt), out_shape=(_sds((T, D), F32), _sds((T, D), MXU_DTYPE), _sds((1, 1), F32)),
                grid=(T // tm,), in_specs=[blk, blk], out_specs=(blk, blk, pl.BlockSpec((1, 1), lambda i: (0, 0))),
                name="loss_head")


def _adamw(w, g, m, v):
    m = ADAM_B1 * m + (1.0 - ADAM_B1) * g
    v = ADAM_B2 * v + (1.0 - ADAM_B2) * jnp.square(g)
    m_hat = m / (1.0 - ADAM_B1 ** ADAM_STEP)
    v_hat = v / (1.0 - ADAM_B2 ** ADAM_STEP)
    delta = -ADAM_LR * (m_hat / (jnp.sqrt(v_hat) + ADAM_EPS) + ADAM_WD * w)
    return delta, m, v


def _sum_adamw(recv, w, m, v, layer, so_far=None):
    L, r, c = w.shape
    tr = _tile(r, ADAM_BLOCK, SUBLANES)

    def body(r_ref, w_ref, m_ref, v_ref, *rest):
        g_ref, d_ref, nm_ref, nv_ref = rest[-4:]
        g = r_ref[0].astype(F32)
        for j in range(1, N_DEV):
            g = g + r_ref[j].astype(F32)
        g_ref[...] = g
        d_ref[...], nm_ref[...], nv_ref[...] = _adamw(w_ref[...], g, m_ref[...], v_ref[...])

    blk = pl.BlockSpec((None, tr, c), lambda i: (layer, i, 0))
    hbm = pl.BlockSpec(memory_space=pl.ANY)
    args, in_specs, aliases = (recv, w, m, v), [pl.BlockSpec((N_DEV, tr, c), lambda i: (0, i, 0)), blk, blk, blk], {}
    if so_far is not None:
        args, in_specs, aliases = args + tuple(so_far), in_specs + [hbm] * 4, {4 + k: k for k in range(4)}
    return _run(body, args, out_shape=(_sds((L, r, c), F32),) * 4, grid=(r // tr,), in_specs=in_specs,
                out_specs=(blk,) * 4, name="grad_sum_adamw", aliases=aliases)


def _small_sum_adamw(gp, w, m, v):
    NR, C = gp.shape
    MESH = pl.DeviceIdType.MESH

    def body(gp_ref, w_ref, m_ref, v_ref, g_ref, d_ref, nm_ref, nv_ref, gath, send_sems, recv_sems):
        x, y, c = _mesh_pos()
        me = 4 * x + 2 * y + c
        gath[me] = gp_ref[...]
        copies = []
        for k in range(1, N_DEV):
            peer = (_flip(x, k & 4), _flip(y, k & 2), _flip(c, k & 1))
            cp = pltpu.make_async_remote_copy(src_ref=gp_ref, dst_ref=gath.at[me], send_sem=send_sems.at[k - 1],
                                              recv_sem=recv_sems.at[k - 1], device_id=peer, device_id_type=MESH)
            cp.start()
            copies.append(cp)
        for cp in copies:
            cp.wait()
        g = gath[0]
        for j in range(1, N_DEV):
            g = g + gath[j]
        g_ref[...] = g
        d_ref[...], nm_ref[...], nv_ref[...] = _adamw(w_ref[...], g, m_ref[...], v_ref[...])

    vm = pl.BlockSpec(memory_space=pltpu.VMEM)
    return _pcall(body, out_shape=(_sds((NR, C), F32),) * 4, in_specs=[vm] * 4, out_specs=(vm,) * 4,
                  scratch_shapes=[pltpu.VMEM((N_DEV, NR, C), F32), pltpu.SemaphoreType.DMA((7,)),
                                  pltpu.SemaphoreType.DMA((7,))], name="small_sum_adamw")(gp, w, m, v)


def _gathered(gath, name):
    _, r, c = gath.shape
    return gath.transpose(1, 0, 2).reshape(r, N_DEV * c) if name in COL_SHARDED else gath.reshape(N_DEV * r, c)


def _scatter_blocks(grad, name):
    rows, cols = grad.shape
    if name in COL_SHARDED:
        return grad.reshape(rows, N_DEV, cols // N_DEV).transpose(1, 0, 2)
    return grad.reshape(N_DEV, rows // N_DEV, cols)


def _pack_small(d, prefix, loss=None):
    parts = []
    for n in SMALL:
        a = d[prefix + n].reshape(-1).astype(F32)
        pad = (-a.shape[0]) % LANES
        parts.append(jnp.pad(a, (0, pad)) if pad else a)
    tail = jnp.zeros((LANES,), F32)
    if loss is not None:
        tail = tail.at[0].set(loss)
    flat = jnp.concatenate(parts + [tail])
    pad = (-flat.shape[0]) % (8 * LANES)
    return jnp.pad(flat, (0, pad)).reshape(-1, LANES)


def _unpack_small(packed, shapes):
    flat = packed.reshape(-1)
    out, off = {}, 0
    for n in SMALL:
        size = math.prod(shapes[n])
        out[n] = flat[off:off + size].reshape(shapes[n])
        off += size + (-size) % LANES
    return out, flat[off]


def _row(a):
    return a.reshape(1, -1).astype(F32)


def _layer_fwd(x, memf, w_in, shards, next_in, sp, lb, tab, dims):
    B, S, NM = dims
    D = x.shape[1]
    if w_in is None:
        h1, gath = _norm_fwd(x, sp["norm1_g"], E=D, W=D, out_dtype=MXU_DTYPE, name="norm1_fwd",
                             carry=("gather", [shards["w_in"].astype(MXU_DTYPE)]))
        w_in = _gathered(gath[0], "w_in")
    else:
        h1 = _norm_fwd(x, sp["norm1_g"], E=D, W=D, out_dtype=MXU_DTYPE, name="norm1_fwd")
    W = {"w_in": w_in}
    z, gath = _matmul(h1, w_in, carry=("gather", [shards[n].astype(MXU_DTYPE) for n in GATHER_MIX]), name="in_proj")
    W.update({n: _gathered(t, n) for n, t in zip(GATHER_MIX, gath)})
    qn = _norm_fwd(z, sp["attn_qn_g"], E=ATT_HEAD_DIM, W=ATT_W, cb=0, out_dtype=MXU_DTYPE, name="qnorm_fwd")
    kn = _norm_fwd(z, sp["attn_kn_g"], E=ATT_HEAD_DIM, W=ATT_W, cb=1, out_dtype=MXU_DTYPE, name="knorm_fwd")
    (ya, lse, ycat), gath = _attn_fwd(tab["attn"], qn, kn, z, B=B, S=S, name="dilated_attn_fwd",
                                carry=("gather", [shards[n].astype(MXU_DTYPE) for n in GATHER_FF]))
    W.update({n: _gathered(t, n) for n, t in zip(GATHER_FF, gath)})
    o, states = _hgrn_fwd(z, lb, tab["hgrn"], B=B, S=S, name="hgrn_fwd")
    ycat = _hgout_fwd(o, z, sp["hg_onorm_g"], ycat, name="hgrn_out_fwd")
    x1 = _matmul(ycat, W["w_out"], epi="add", extra=x, name="out_proj")
    h2 = _norm_fwd(x1, sp["norm2_g"], E=D, W=D, out_dtype=MXU_DTYPE, name="norm2_fwd")
    qmp = _matmul(h2, W["w_mq"], name="mq_proj")
    qm = _norm_fwd(qmp, sp["mq_norm_g"], E=D // MEM_HEADS, W=D, out_dtype=MXU_DTYPE, name="mqnorm_fwd")
    mn = _norm_fwd(memf, sp["mem_norm_g"], E=D, W=D, out_dtype=MXU_DTYPE, name="memnorm_fwd")
    kv = _matmul(mn, W["w_mkv"], name="mkv_proj")
    km = _norm_fwd(kv, sp["mk_norm_g"], E=D // MEM_HEADS, W=D, cb=0, out_dtype=MXU_DTYPE, name="mknorm_fwd")
    om = _memattn_fwd(qm, km, kv, B=B, S=S, NM=NM, name="mem_attn_fwd")
    x2 = _matmul(om, W["w_mo"], epi="add", extra=x1, name="mo_proj")
    h3 = _norm_fwd(x2, sp["norm3_g"], E=D, W=D, out_dtype=MXU_DTYPE, name="norm3_fwd")
    if next_in is None:
        u, w_in_next = _matmul(h3, W["w_ff1"], epi="relu2", out_dtype=MXU_DTYPE, name="ff1"), None
    else:
        u, gath = _matmul(h3, W["w_ff1"], epi="relu2", out_dtype=MXU_DTYPE,
                          carry=("gather", [next_in.astype(MXU_DTYPE)]), name="ff1")
        w_in_next = _gathered(gath[0], "w_in")
    x3 = _matmul(u, W["w_ff2"], epi="add", extra=x2, name="ff2")
    saved = dict(x=x, h1=h1, z=z, qn=qn, kn=kn, ya=ya, lse=lse, o=o, states=states, ycat=ycat, x1=x1, h2=h2,
                 qmp=qmp, qm=qm, mn=mn, kv=kv, km=km, om=om, x2=x2, h3=h3, u=u)
    return x3, saved, W, w_in_next


def _layer_bwd(dx3, dx3_low, s, memf, W, sp, lb, tab, dims):
    B, S, NM = dims
    D = dx3.shape[1]
    E_M = D // MEM_HEADS
    gw, gs, recv = {}, {}, {}

    def blocks(names):
        return ("scatter", [_scatter_blocks(gw[n], n) for n in names])

    def received(names, got):
        recv.update(dict(zip(names, got)))

    da = _matmul(dx3_low, W["w_ff2"], tb=True, epi="relu2grad", extra=s["u"], out_dtype=MXU_DTYPE, name="ff2_dgrad")
    gw["w_ff2"] = _matmul(s["u"], dx3_low, ta=True, out_dtype=MXU_DTYPE, name="ff2_wgrad")
    dh3, got = _matmul(da, W["w_ff1"], tb=True, out_dtype=MXU_DTYPE, carry=blocks(SCATTER_A), name="ff1_dgrad")
    received(SCATTER_A, got)
    gw["w_ff1"] = _matmul(s["h3"], da, ta=True, out_dtype=MXU_DTYPE, name="ff1_wgrad")
    dx2, dx2_low, gs["norm3_g"] = _norm_bwd(s["x2"], sp["norm3_g"], dh3, E=D, W=D, res=dx3, low=True, name="norm3_bwd")
    dom = _matmul(dx2_low, W["w_mo"], tb=True, out_dtype=MXU_DTYPE, name="mo_dgrad")
    gw["w_mo"] = _matmul(s["om"], dx2_low, ta=True, out_dtype=MXU_DTYPE, name="mo_wgrad")
    dqm, dkm, dvm = _memattn_bwd(s["qm"], s["km"], s["kv"], dom, B=B, S=S, NM=NM, name="mem_attn_bwd")
    dqmp, gs["mq_norm_g"] = _norm_bwd(s["qmp"], sp["mq_norm_g"], dqm, E=E_M, W=D, fold=E_M, out_dtype=MXU_DTYPE,
                                      name="mqnorm_bwd")
    dkmp, gs["mk_norm_g"] = _norm_bwd(s["kv"], sp["mk_norm_g"], dkm, E=E_M, W=D, cb=0, fold=E_M,
                                      out_dtype=MXU_DTYPE, name="mknorm_bwd")
    dkv = jnp.concatenate([dkmp, dvm.astype(MXU_DTYPE)], axis=1)
    dh2 = _matmul(dqmp, W["w_mq"], tb=True, out_dtype=MXU_DTYPE, name="mq_dgrad")
    gw["w_mq"] = _matmul(s["h2"], dqmp, ta=True, out_dtype=MXU_DTYPE, name="mq_wgrad")
    dmn = _matmul(dkv, W["w_mkv"], tb=True, out_dtype=MXU_DTYPE, name="mkv_dgrad")
    gw["w_mkv"] = _matmul(s["mn"], dkv, ta=True, out_dtype=MXU_DTYPE, name="mkv_wgrad")
    _, gs["mem_norm_g"] = _norm_bwd(memf, sp["mem_norm_g"], dmn, E=D, W=D, name="memnorm_bwd")
    dx1, dx1_low, gs["norm2_g"] = _norm_bwd(s["x1"], sp["norm2_g"], dh2, E=D, W=D, res=dx2, low=True, name="norm2_bwd")
    dycat = _matmul(dx1_low, W["w_out"], tb=True, out_dtype=MXU_DTYPE, name="out_dgrad")
    gw["w_out"] = _matmul(s["ycat"], dx1_low, ta=True, out_dtype=MXU_DTYPE, name="out_wgrad")
    do_hg, dz, gs["hg_onorm_g"] = _hgout_bwd(s["o"], s["z"], sp["hg_onorm_g"], dycat, name="hgrn_out_bwd")
    dqh, dfh, dih, dlb = _hgrn_bwd(s["z"], lb, tab["hgrn"], s["states"], do_hg, B=B, S=S, name="hgrn_bwd")
    (dqn, dkn, dz), got = _attn_bwd(tab["attn"], s["qn"], s["kn"], s["z"], s["ya"], s["lse"], dycat, dz, B=B, S=S,
                                     carry=blocks(SCATTER_B), name="dilated_attn_bwd")
    received(SCATTER_B, got)
    dz, gs["attn_qn_g"] = _norm_bwd(s["z"], sp["attn_qn_g"], dqn, E=ATT_HEAD_DIM, W=ATT_W, cb=0, fold=ATT_HEAD_DIM,
                                    into=dz, name="qnorm_bwd")
    dz, gs["attn_kn_g"] = _norm_bwd(s["z"], sp["attn_kn_g"], dkn, E=ATT_HEAD_DIM, W=ATT_W, cb=1, fold=ATT_HEAD_DIM,
                                    into=dz, name="knorm_bwd")
    for block, t in ((3, dqh), (4, dfh), (5, dih)):
        dz = lax.dynamic_update_slice(dz, t, (0, block * HG_W))
    gw["w_in"] = _matmul(s["h1"], dz, ta=True, out_dtype=MXU_DTYPE, name="in_wgrad")
    dh1, got = _matmul(dz, W["w_in"], tb=True, out_dtype=MXU_DTYPE, carry=blocks(SCATTER_C), name="in_dgrad")
    received(SCATTER_C, got)
    dx0, dx0_low, gs["norm1_g"] = _norm_bwd(s["x"], sp["norm1_g"], dh1, E=D, W=D, res=dx1, low=True, name="norm1_bwd")
    gs["attn_qn_g"] = gs["attn_qn_g"][:, :ATT_HEAD_DIM]
    gs["attn_kn_g"] = gs["attn_kn_g"][:, :ATT_HEAD_DIM]
    return dx0, dx0_low, recv, gs, dlb


def kernel(x, mem, norm1_g, w_in, attn_qn_g, attn_kn_g, hg_lb, hg_onorm_g, w_out, norm2_g, mem_norm_g, w_mq, w_mkv, mq_norm_g, mk_norm_g, w_mo, norm3_g, w_ff1, w_ff2, loss_target, m_norm1_g, m_w_in, m_attn_qn_g, m_attn_kn_g, m_hg_lb, m_hg_onorm_g, m_w_out, m_norm2_g, m_mem_norm_g, m_w_mq, m_w_mkv, m_mq_norm_g, m_mk_norm_g, m_w_mo, m_norm3_g, m_w_ff1, m_w_ff2, v_norm1_g, v_w_in, v_attn_qn_g, v_attn_kn_g, v_hg_lb, v_hg_onorm_g, v_w_out, v_norm2_g, v_mem_norm_g, v_w_mq, v_w_mkv, v_mq_norm_g, v_mk_norm_g, v_w_mo, v_norm3_g, v_w_ff1, v_w_ff2):
    given = dict(norm1_g=norm1_g, w_in=w_in, attn_qn_g=attn_qn_g, attn_kn_g=attn_kn_g, hg_lb=hg_lb, hg_onorm_g=hg_onorm_g, w_out=w_out, norm2_g=norm2_g, mem_norm_g=mem_norm_g, w_mq=w_mq, w_mkv=w_mkv, mq_norm_g=mq_norm_g, mk_norm_g=mk_norm_g, w_mo=w_mo, norm3_g=norm3_g, w_ff1=w_ff1, w_ff2=w_ff2, m_norm1_g=m_norm1_g, m_w_in=m_w_in, m_attn_qn_g=m_attn_qn_g, m_attn_kn_g=m_attn_kn_g, m_hg_lb=m_hg_lb, m_hg_onorm_g=m_hg_onorm_g, m_w_out=m_w_out, m_norm2_g=m_norm2_g, m_mem_norm_g=m_mem_norm_g, m_w_mq=m_w_mq, m_w_mkv=m_w_mkv, m_mq_norm_g=m_mq_norm_g, m_mk_norm_g=m_mk_norm_g, m_w_mo=m_w_mo, m_norm3_g=m_norm3_g, m_w_ff1=m_w_ff1, m_w_ff2=m_w_ff2, v_norm1_g=v_norm1_g, v_w_in=v_w_in, v_attn_qn_g=v_attn_qn_g, v_attn_kn_g=v_attn_kn_g, v_hg_lb=v_hg_lb, v_hg_onorm_g=v_hg_onorm_g, v_w_out=v_w_out, v_norm2_g=v_norm2_g, v_mem_norm_g=v_mem_norm_g, v_w_mq=v_w_mq, v_w_mkv=v_w_mkv, v_mq_norm_g=v_mq_norm_g, v_mk_norm_g=v_mk_norm_g, v_w_mo=v_w_mo, v_norm3_g=v_norm3_g, v_w_ff1=v_w_ff1, v_w_ff2=v_w_ff2)
    B, S, D = x.shape
    NM = mem.shape[1]
    L = w_in.shape[0]
    dims = (B, S, NM)
    small_shapes = {n: given[n].shape for n in SMALL}

    def shards(prefix, l):
        return {n: given[prefix + n][l] for n in BIG}

    lower = _lower_bounds_fwd(hg_lb)
    tab = {"attn": _attn_tables(S, _tile(S, ATT_Q_BLOCK, SUBLANES)), "hgrn": _hgrn_pair_table(_tile(S, HG_BLOCK, HG_CHUNK))}
    xf = x.reshape(B * S, D)
    memf = mem.reshape(B * NM, D)

    def small_params(l):
        sp = {"norm1_g": _row(norm1_g[l]), "norm2_g": _row(norm2_g[l]), "norm3_g": _row(norm3_g[l]),
              "mem_norm_g": _row(mem_norm_g[l]), "hg_onorm_g": _row(hg_onorm_g[l]),
              "attn_qn_g": _row(jnp.tile(attn_qn_g[l], ATT_HEADS)), "attn_kn_g": _row(jnp.tile(attn_kn_g[l], ATT_HEADS)),
              "mq_norm_g": _row(jnp.tile(mq_norm_g[l], MEM_HEADS)), "mk_norm_g": _row(jnp.tile(mk_norm_g[l], MEM_HEADS))}
        return sp, _row(lower[l])

    w_in_full = None
    saved, weights = [], []
    h = xf
    for l in range(L):
        sp, lb = small_params(l)
        next_in = given["w_in"][l + 1] if l + 1 < L else None
        h, s, W, w_in_full = _layer_fwd(h, memf, w_in_full, shards("", l), next_in, sp, lb, tab, dims)
        saved.append(s)
        weights.append(W)

    dh, dh_low, loss_part = _loss_head(h, loss_target.reshape(B * S, D))

    recv_layers, gs_layers, dlb_layers = [None] * L, [None] * L, [None] * L
    for l in range(L - 1, -1, -1):
        sp, lb = small_params(l)
        dh, dh_low, recv_layers[l], gs_layers[l], dlb_layers[l] = _layer_bwd(dh, dh_low, saved[l], memf, weights[l],
                                                                              sp, lb, tab, dims)
    grad_x = dh.reshape(B, S, D)

    big_out = {}
    for n in BIG:
        outs = None
        for l in range(L):
            outs = _sum_adamw(recv_layers[l][n], given[n], given["m_" + n], given["v_" + n], l, outs)
        big_out[n] = outs

    gs = {n: jnp.stack([gs_layers[l][n].reshape(small_shapes[n][1:]) for l in range(L)]) for n in SMALL if n != "hg_lb"}
    gs["hg_lb"] = _lower_bounds_bwd(hg_lb, jnp.concatenate(dlb_layers, axis=0))
    packed = _small_sum_adamw(_pack_small(gs, "", loss_part[0, 0]), _pack_small(given, ""), _pack_small(given, "m_"),
                              _pack_small(given, "v_"))
    small_out, loss = [], None
    for t in packed:
        d, tail = _unpack_small(t, small_shapes)
        small_out.append(d)
        loss = tail if loss is None else loss

    outs = [loss, grad_x]
    for k in range(4):
        outs += [small_out[k][n] if n in small_shapes else big_out[n][k] for n in WEIGHTS]
    return tuple(outs)
```

```python
import functools
import math

import jax
import jax.numpy as jnp
from jax import lax
from jax.experimental import pallas as pl
from jax.experimental.pallas import tpu as pltpu

F32 = jnp.float32
MXU_DTYPE = jnp.bfloat16
VMEM_LIMIT = 48 * 1024 * 1024

N_DEV = 8
EPS = 1e-6
NEG = -1e30
F_MIN = 1e-12
ATT_HEADS = 8
ATT_HEAD_DIM = 64
ATT_W = ATT_HEADS * ATT_HEAD_DIM
DILATIONS = (1, 4, 16)
DIL_STEPS = 128
ATT_Q_BLOCK = 256
ATT_KEY_EXTENT = 512
HG_HEADS = 4
HG_HEAD_DIM = 128
HG_W = HG_HEADS * HG_HEAD_DIM
HG_CHUNK = 32
HG_SUB = 4
MEM_HEADS = 4
LANES = 128
SUBLANES = 8
ROW_BLOCK = 1024
MM_BLOCK = 1024
MM_K_BLOCK = 2048
HG_BLOCK = 256
ADAM_BLOCK = 256

ADAM_LR = 0.001
ADAM_B1 = 0.9
ADAM_B2 = 0.999
ADAM_EPS = 1e-08
ADAM_WD = 0.01
ADAM_STEP = 10

BIG = ("w_in", "w_out", "w_mq", "w_mkv", "w_mo", "w_ff1", "w_ff2")
COL_SHARDED = ("w_in", "w_mkv", "w_ff1")
SMALL = ("norm1_g", "attn_qn_g", "attn_kn_g", "hg_lb", "hg_onorm_g", "norm2_g", "mem_norm_g",
         "mq_norm_g", "mk_norm_g", "norm3_g")
WEIGHTS = ("norm1_g", "w_in", "attn_qn_g", "attn_kn_g", "hg_lb", "hg_onorm_g", "w_out", "norm2_g",
           "mem_norm_g", "w_mq", "w_mkv", "mq_norm_g", "mk_norm_g", "w_mo", "norm3_g", "w_ff1", "w_ff2")
GATHER_MIX = ("w_out", "w_mq", "w_mkv", "w_mo")
GATHER_FF = ("w_ff1", "w_ff2")
SCATTER_A = ("w_ff2",)
SCATTER_B = ("w_ff1", "w_mo", "w_mq", "w_mkv", "w_out")
SCATTER_C = ("w_in",)


def _pcall(body, **kw):
    return pl.pallas_call(body, **kw)


def _params():
    return pltpu.CompilerParams(vmem_limit_bytes=VMEM_LIMIT)


def _tile(n, pref, mult):
    t = (min(n, pref) // mult) * mult
    while t >= mult:
        if n % t == 0:
            return t
        t -= mult
    return n


def _sds(shape, dtype):
    return jax.ShapeDtypeStruct(shape, dtype)


def _mesh_pos():
    return lax.axis_index("x"), lax.axis_index("y"), lax.axis_index("c")


def _flip(v, bit):
    return 1 - v if bit else v


def _gather_hooks(x_refs, out_refs, send_sems, recv_sems, local_sems):
    MESH = pl.DeviceIdType.MESH
    n = len(x_refs)
    x, y, c = _mesh_pos()
    me, sibling = (x, y, c), (x, y, 1 - c)
    chips = [(1 - x, y), (x, 1 - y), (1 - x, 1 - y)]

    def rows(a, px, py, pc):
        return out_refs[a].at[4 * px + 2 * py + pc]

    def copy(a, k, block, to, own=False):
        return pltpu.make_async_remote_copy(
            src_ref=x_refs[a] if own else rows(a, *block), dst_ref=rows(a, *block), send_sem=send_sems.at[n * k + a],
            recv_sem=recv_sems.at[n * k + a], device_id=to, device_id_type=MESH)

    mine = [pltpu.make_async_copy(x_refs[a], rows(a, *me), local_sems.at[a]) for a in range(n)]
    first = [copy(a, 0, me, sibling, own=True) for a in range(n)]
    first += [copy(a, 1 + j, me, (*chip, c), own=True) for j, chip in enumerate(chips) for a in range(n)]
    passed = [[copy(a, 4 + j, (*chip, c), sibling) for a in range(n)] for j, chip in enumerate(chips)]

    def start():
        for cp in mine + first:
            cp.start()

    def forward():
        for j, chip in enumerate(chips):
            for a in range(n):
                copy(a, 1 + j, (*chip, c), me).wait_recv()
                passed[j][a].start()

    def finish():
        for a in range(n):
            copy(a, 0, sibling, me).wait_recv()
        for j, chip in enumerate(chips):
            for a in range(n):
                copy(a, 4 + j, (*chip, 1 - c), me).wait_recv()
        for cp in first + [cp for group in passed for cp in group]:
            cp.wait_send()
        for cp in mine:
            cp.wait()

    return start, forward, finish


def _scatter_hooks(s_refs, r_refs, send_sems, recv_sems, local_sems):
    MESH = pl.DeviceIdType.MESH
    n = len(s_refs)
    x, y, c = _mesh_pos()
    me = 4 * x + 2 * y + c
    mine = [pltpu.make_async_copy(s_refs[a].at[me], r_refs[a].at[me], local_sems.at[a]) for a in range(n)]
    copies = []
    for m in range(1, N_DEV):
        px, py, pc = _flip(x, m & 4), _flip(y, m & 2), _flip(c, m & 1)
        for a in range(n):
            copies.append(pltpu.make_async_remote_copy(
                src_ref=s_refs[a].at[4 * px + 2 * py + pc], dst_ref=r_refs[a].at[me],
                send_sem=send_sems.at[n * (m - 1) + a], recv_sem=recv_sems.at[n * (m - 1) + a],
                device_id=(px, py, pc), device_id_type=MESH))

    def start():
        for cp in mine + copies:
            cp.start()

    def finish():
        for cp in copies:
            cp.wait()
        for cp in mine:
            cp.wait()

    return start, None, finish


def _exchange_sems(n):
    return [pltpu.SemaphoreType.DMA(((N_DEV - 1) * n,)), pltpu.SemaphoreType.DMA(((N_DEV - 1) * n,)),
            pltpu.SemaphoreType.DMA((n,))]


def _exchange_out(kind, xs):
    return _sds((N_DEV,) + xs.shape, xs.dtype) if kind == "gather" else _sds(xs.shape, xs.dtype)


def _run(body, args, *, out_shape, grid, in_specs, out_specs, scratch_shapes=(), name, carry=None, aliases=None):
    single = not isinstance(out_shape, (tuple, list))
    outs = (out_shape,) if single else tuple(out_shape)
    ospecs = (out_specs,) if single else tuple(out_specs)
    aliases = dict(aliases or {})
    if carry is None:
        res = _pcall(body, out_shape=outs, grid=grid, in_specs=list(in_specs), out_specs=ospecs,
                     scratch_shapes=list(scratch_shapes), input_output_aliases=aliases, compiler_params=_params(),
                     name=name)(*args)
        return res[0] if single else tuple(res)
    kind, xs = carry
    n_in, n_out, n_scr, n_x = len(args), len(outs), len(scratch_shapes), len(xs)
    total = math.prod(grid)

    def wrapped(*refs):
        refs = list(refs)
        ins, x_refs = refs[:n_in], refs[n_in:n_in + n_x]
        o, xo_refs = refs[n_in + n_x:n_in + n_x + n_out], refs[n_in + n_x + n_out:n_in + 2 * n_x + n_out]
        scr = refs[n_in + 2 * n_x + n_out:n_in + 2 * n_x + n_out + n_scr]
        sems = refs[n_in + 2 * n_x + n_out + n_scr:]
        step = pl.program_id(0)
        for ax in range(1, len(grid)):
            step = step * grid[ax] + pl.program_id(ax)
        start, forward, finish = (_gather_hooks if kind == "gather" else _scatter_hooks)(x_refs, xo_refs, *sems)
        pl.when(step == 0)(start)
        body(*ins, *o, *scr)
        if forward is not None:
            pl.when(step == (3 * total) // 4)(forward)
        pl.when(step == total - 1)(finish)

    hbm = pl.BlockSpec(memory_space=pl.ANY)
    res = _pcall(wrapped, out_shape=outs + tuple(_exchange_out(kind, t) for t in xs), grid=grid,
                 in_specs=list(in_specs) + [hbm] * n_x, out_specs=ospecs + (hbm,) * n_x,
                 scratch_shapes=list(scratch_shapes) + _exchange_sems(n_x), input_output_aliases=aliases,
                 compiler_params=_params(), name=name)(*args, *xs)
    main = tuple(res[:n_out])
    return (main[0] if single else main), list(res[n_out:])


def _matmul(a, b, *, ta=False, tb=False, epi=None, extra=None, out_dtype=F32, carry=None, name):
    M, K = (a.shape[1], a.shape[0]) if ta else a.shape
    N = b.shape[0] if tb else b.shape[1]
    tm = _tile(M, MM_BLOCK, SUBLANES if not ta else LANES)
    tn = _tile(N, MM_BLOCK if ta else 2 * MM_BLOCK, LANES)
    tk = _tile(K, MM_K_BLOCK // 2 if ta else MM_K_BLOCK, LANES if not ta else SUBLANES)
    nk = K // tk
    dims = (((0 if ta else 1,), (1 if tb else 0,)), ((), ()))

    def body(a_ref, b_ref, *rest):
        e_ref = rest[0] if extra is not None else None
        o_ref = rest[1] if extra is not None else rest[0]

        def finish(r):
            if epi == "add":
                r = r + e_ref[...]
            elif epi == "relu2":
                r = jnp.square(jnp.maximum(r, 0.0))
            elif epi == "relu2grad":
                r = r * (2.0 * jnp.sqrt(e_ref[...].astype(F32)))
            o_ref[...] = r.astype(out_dtype)

        part = lax.dot_general(a_ref[...].astype(MXU_DTYPE), b_ref[...].astype(MXU_DTYPE), dims,
                               preferred_element_type=F32)
        if nk == 1:
            finish(part)
        else:
            acc_ref = rest[-1]
            k = pl.program_id(2)

            @pl.when(k == 0)
            def _():
                acc_ref[...] = part

            @pl.when(k > 0)
            def _():
                acc_ref[...] += part

            @pl.when(k == nk - 1)
            def _():
                finish(acc_ref[...])

    a_spec = pl.BlockSpec((tk, tm), lambda i, j, k: (k, i)) if ta else pl.BlockSpec((tm, tk), lambda i, j, k: (i, k))
    b_spec = pl.BlockSpec((tn, tk), lambda i, j, k: (j, k)) if tb else pl.BlockSpec((tk, tn), lambda i, j, k: (k, j))
    o_spec = pl.BlockSpec((tm, tn), lambda i, j, k: (i, j))
    in_specs = [a_spec, b_spec] + ([o_spec] if extra is not None else [])
    args = (a, b) + ((extra,) if extra is not None else ())
    return _run(body, args, out_shape=_sds((M, N), out_dtype), grid=(M // tm, N // tn, nk), in_specs=in_specs,
                out_specs=o_spec, scratch_shapes=[pltpu.VMEM((tm, tn), F32)] if nk > 1 else [], name=name, carry=carry)


def _group_mean(v, E):
    rows, W = v.shape
    if E == W:
        return jnp.mean(v, axis=-1, keepdims=True)
    pieces = []
    if E % LANES == 0:
        for g0 in range(0, W, E):
            m = jnp.mean(v[:, g0:g0 + E], axis=-1, keepdims=True)
            pieces.append(jnp.broadcast_to(m, (rows, E)))
    else:
        lane = lax.broadcasted_iota(jnp.int32, (rows, LANES), 1)
        for c0 in range(0, W, LANES):
            vc = v[:, c0:c0 + LANES]
            acc = jnp.zeros((rows, LANES), F32)
            for s0 in range(0, LANES, E):
                msk = (lane >= s0) & (lane < s0 + E)
                m = jnp.sum(jnp.where(msk, vc, 0.0), axis=-1, keepdims=True) * (1.0 / E)
                acc = jnp.where(msk, m, acc)
            pieces.append(acc)
    return jnp.concatenate(pieces, axis=-1)


def _fold_groups(t, E):
    W = t.shape[1]
    step = max(E, LANES)
    acc = t[:, 0:step]
    for c0 in range(step, W, step):
        acc = acc + t[:, c0:c0 + step]
    sh = LANES // 2
    while sh >= E:
        acc = acc + pltpu.roll(acc, sh, 1)
        sh //= 2
    return acc


def _norm_fwd(x, g, *, E, W, cb=0, out_dtype, carry=None, name):
    M = x.shape[0]
    tm = _tile(M, ROW_BLOCK, SUBLANES)

    def body(x_ref, g_ref, o_ref):
        xv = x_ref[...]
        r = lax.rsqrt(_group_mean(xv * xv, E) + EPS)
        o_ref[...] = (xv * r * g_ref[...]).astype(out_dtype)

    return _run(body, (x, g), out_shape=_sds((M, W), out_dtype), grid=(M // tm,),
                in_specs=[pl.BlockSpec((tm, W), lambda i: (i, cb)), pl.BlockSpec((1, W), lambda i: (0, 0))],
                out_specs=pl.BlockSpec((tm, W), lambda i: (i, 0)), name=name, carry=carry)


def _norm_bwd(x, g, dy, *, E, W, cb=0, res=None, fold=None, out_dtype=F32, low=False, into=None, name):
    M = x.shape[0]
    tm = _tile(M, ROW_BLOCK, SUBLANES)
    n = M // tm
    gw = W if fold is None else max(fold, LANES)

    def body(x_ref, g_ref, dy_ref, *rest):
        rest = list(rest)
        r_ref = rest.pop(0) if res is not None else None
        if into is not None:
            rest.pop(0)
        dx_ref = rest.pop(0)
        dxl_ref = rest.pop(0) if low else None
        dg_ref, acc_ref = rest
        i = pl.program_id(0)
        xv = x_ref[...]
        r = lax.rsqrt(_group_mean(xv * xv, E) + EPS)
        xh = xv * r
        dyv = dy_ref[...].astype(F32)
        dyg = dyv * g_ref[...]
        dx = r * (dyg - xh * _group_mean(dyg * xh, E))
        if res is not None:
            dx = dx + r_ref[...]
        dx_ref[...] = dx.astype(dx_ref.dtype)
        if low:
            dxl_ref[...] = dx.astype(MXU_DTYPE)
        part = jnp.sum(dyv * xh, axis=0, keepdims=True)

        @pl.when(i == 0)
        def _():
            acc_ref[...] = part

        @pl.when(i > 0)
        def _():
            acc_ref[...] += part

        @pl.when(i == n - 1)
        def _():
            t = acc_ref[...]
            dg_ref[...] = t if fold is None else _fold_groups(t, fold)

    blk = pl.BlockSpec((tm, W), lambda i: (i, 0))
    in_specs = [pl.BlockSpec((tm, W), lambda i: (i, cb)), pl.BlockSpec((1, W), lambda i: (0, 0)), blk]
    args = [x, g, dy]
    if res is not None:
        in_specs.append(blk)
        args.append(res)
    lows = ((_sds((M, W), MXU_DTYPE),), (blk,)) if low else ((), ())
    dx_out, dx_spec, aliases = _sds((M, W), out_dtype), blk, None
    if into is not None:
        aliases = {len(args): 0}
        args.append(into)
        in_specs.append(pl.BlockSpec(memory_space=pl.ANY))
        dx_out, dx_spec = _sds(into.shape, into.dtype), pl.BlockSpec((tm, W), lambda i: (i, cb))
    return _run(body, tuple(args), out_shape=(dx_out,) + lows[0] + (_sds((1, gw), F32),), grid=(n,),
                in_specs=in_specs, out_specs=(dx_spec,) + lows[1] + (pl.BlockSpec((1, gw), lambda i: (0, 0)),),
                scratch_shapes=[pltpu.VMEM((1, W), F32)], name=name, aliases=aliases)


def _attn_tables(S, tq):
    nq = S // tq

    def body(o_ref):
        i = pl.program_id(0)
        d = (i * tq + lax.broadcasted_iota(jnp.int32, (tq, S), 0)) - lax.broadcasted_iota(jnp.int32, (tq, S), 1)
        cnt = jnp.zeros((tq, S), jnp.int32)
        for dil in DILATIONS:
            hit = (d <= DIL_STEPS * dil) if dil == 1 else (((d & (dil - 1)) == 0) & (d <= DIL_STEPS * dil))
            cnt = cnt + hit.astype(jnp.int32)
        ok = (d >= 0) & (cnt > 0)
        logm = jnp.where(cnt == 3, math.log(3.0), jnp.where(cnt == 2, math.log(2.0), 0.0))
        o_ref[0] = jnp.where(ok, logm, NEG).astype(F32)

    return _run(body, (), out_shape=_sds((nq, tq, S), F32), grid=(nq,), in_specs=[],
                out_specs=pl.BlockSpec((1, tq, S), lambda i: (i, 0, 0)), name="dilated_attn_tables")


def _alibi_slope(h):
    return 2.0 ** (-8.0 / ATT_HEADS * (h + 1))


def _key_positions(Sk):
    kpos = lax.broadcasted_iota(jnp.int32, (Sk, LANES), 0)
    return (kpos >> 8).astype(F32).astype(MXU_DTYPE), (kpos & 255).astype(F32).astype(MXU_DTYPE)


def _score_operands(q2s, k2, pos_hi, pos_lo, lane, sub, h):
    own = (lane < ATT_HEAD_DIM) if sub == 0 else (lane >= ATT_HEAD_DIM)
    spare = ATT_HEAD_DIM if sub == 0 else 0
    slope = _alibi_slope(h)
    terms = jnp.where(lane == spare, slope * 256.0, jnp.where(lane == spare + 1, slope, 0.0)).astype(q2s.dtype)
    qa = jnp.where(own, q2s, terms)
    ka = jnp.where(lane == spare, pos_hi, jnp.where(lane == spare + 1, pos_lo, k2))
    return own, qa, ka


def _key_extents(S, tq):
    ext = min(ATT_KEY_EXTENT, S)
    return ext, ext // tq, S // ext


def _attn_fwd(tab, qn, kn, z, *, B, S, carry=None, name):
    T = B * S
    tq = tab.shape[1]
    nq = S // tq
    ext, per, n_ext = _key_extents(S, tq)
    scale = 1.0 / math.sqrt(ATT_HEAD_DIM)
    dn_t = (((1,), (1,)), ((), ()))

    def body(tab_ref, q_ref, k_ref, v_ref, y_ref, lse_ref, ycat_ref):
        i = pl.program_id(1)
        lane = lax.broadcasted_iota(jnp.int32, (1, LANES), 1)
        lo = lane < ATT_HEAD_DIM

        def visit(Sk):
            base = tab_ref[0, :, :Sk]
            pos_hi, pos_lo = _key_positions(Sk)
            lse_blk = jnp.zeros((tq, LANES), F32)
            for p in range(ATT_HEADS // 2):
                cs = slice(p * LANES, (p + 1) * LANES)
                q2s = q_ref[:, cs].astype(MXU_DTYPE) * scale
                k2 = k_ref[:Sk, cs].astype(MXU_DTYPE)
                v2 = v_ref[:Sk, cs].astype(MXU_DTYPE)
                outs = []
                for sub in range(2):
                    h = 2 * p + sub
                    _, qa, ka = _score_operands(q2s, k2, pos_hi, pos_lo, lane, sub, h)
                    s = lax.dot_general(qa, ka, dn_t, preferred_element_type=F32) + base
                    mx = jnp.max(s, axis=1, keepdims=True)
                    e = jnp.exp(s - mx)
                    l = jnp.sum(e, axis=1, keepdims=True)
                    outs.append(jnp.dot(e.astype(MXU_DTYPE), v2, preferred_element_type=F32) / l)
                    lse_blk = jnp.where(lane == h, mx + jnp.log(l), lse_blk)
                y = jnp.where(lo, outs[0], outs[1])
                y_ref[:, cs] = y
                ycat_ref[:, cs] = y.astype(MXU_DTYPE)
            lse_ref[...] = lse_blk

        for e in range(n_ext):
            pl.when(i // per == e)(functools.partial(visit, (e + 1) * ext))

    return _run(
        body, (tab, qn, kn, z), grid=(B, nq),
        out_shape=(_sds((T, ATT_W), F32), _sds((T, LANES), F32), _sds((T, ATT_W + HG_W), MXU_DTYPE)),
        in_specs=[pl.BlockSpec((1, tq, S), lambda b, i: (i, 0, 0)),
                  pl.BlockSpec((tq, ATT_W), lambda b, i: (b * nq + i, 0)),
                  pl.BlockSpec((S, ATT_W), lambda b, i: (b, 0)),
                  pl.BlockSpec((S, ATT_W), lambda b, i: (b, 2))],
        out_specs=(pl.BlockSpec((tq, ATT_W), lambda b, i: (b * nq + i, 0)),
                   pl.BlockSpec((tq, LANES), lambda b, i: (b * nq + i, 0)),
                   pl.BlockSpec((tq, ATT_W), lambda b, i: (b * nq + i, 0))),
        name=name, carry=carry)


def _attn_bwd(tab, qn, kn, z, y, lse, dycat, dz, *, B, S, carry=None, name):
    T = B * S
    tq = tab.shape[1]
    nq = S // tq
    ext, per, n_ext = _key_extents(S, tq)
    scale = 1.0 / math.sqrt(ATT_HEAD_DIM)
    dn_t = (((1,), (1,)), ((), ()))
    dn_o = (((0,), (0,)), ((), ()))

    def body(tab_ref, q_ref, k_ref, v_ref, y_ref, lse_ref, dy_ref, _, dq_ref, dk_ref, dz_ref, dv_ref):
        i = pl.program_id(1)

        @pl.when(i == 0)
        def _():
            dk_ref[...] = jnp.zeros_like(dk_ref)
            dv_ref[...] = jnp.zeros_like(dv_ref)

        lane = lax.broadcasted_iota(jnp.int32, (1, LANES), 1)
        lo = lane < ATT_HEAD_DIM

        def visit(Sk):
            base = tab_ref[0, :, :Sk]
            pos_hi, pos_lo = _key_positions(Sk)
            lse_blk = lse_ref[...]
            for p in range(ATT_HEADS // 2):
                cs = slice(p * LANES, (p + 1) * LANES)
                q2 = q_ref[:, cs].astype(MXU_DTYPE)
                q2s = q2 * scale
                k2 = k_ref[:Sk, cs].astype(MXU_DTYPE)
                v2 = v_ref[:Sk, cs].astype(MXU_DTYPE)
                do2 = dy_ref[:, cs]
                doy = do2 * y_ref[:, cs]
                do2m = do2.astype(MXU_DTYPE)
                dqs, dks, dvs = [], [], []
                for sub in range(2):
                    h = 2 * p + sub
                    own, qa, ka = _score_operands(q2s, k2, pos_hi, pos_lo, lane, sub, h)
                    s = lax.dot_general(qa, ka, dn_t, preferred_element_type=F32) + base
                    lse_h = jnp.sum(jnp.where(lane == h, lse_blk, 0.0), axis=1, keepdims=True)
                    pr = jnp.exp(s - lse_h)
                    dsum = jnp.sum(jnp.where(own, doy, 0.0), axis=1, keepdims=True)
                    dom = jnp.where(own, do2m, jnp.zeros_like(do2m))
                    dp = lax.dot_general(dom, v2, dn_t, preferred_element_type=F32)
                    ds = (pr * (dp - dsum)).astype(MXU_DTYPE)
                    dqs.append(jnp.dot(ds, k2, preferred_element_type=F32))
                    dks.append(lax.dot_general(ds, q2, dn_o, preferred_element_type=F32))
                    dvs.append(lax.dot_general(pr.astype(MXU_DTYPE), do2m, dn_o, preferred_element_type=F32))
                dq_ref[:, cs] = jnp.where(lo, dqs[0], dqs[1]) * scale
                dk_ref[:Sk, cs] += jnp.where(lo, dks[0], dks[1]) * scale
                dv_ref[:Sk, cs] += jnp.where(lo, dvs[0], dvs[1])

        for e in range(n_ext):
            pl.when(i // per == e)(functools.partial(visit, (e + 1) * ext))

        @pl.when(i == nq - 1)
        def _():
            dz_ref[...] = dv_ref[...].astype(dz_ref.dtype)

    qblk = pl.BlockSpec((tq, ATT_W), lambda b, i: (b * nq + i, 0))
    sblk = pl.BlockSpec((S, ATT_W), lambda b, i: (b, 0))
    vblk = pl.BlockSpec((S, ATT_W), lambda b, i: (b, 2))
    return _run(
        body, (tab, qn, kn, z, y, lse, dycat, dz), grid=(B, nq),
        out_shape=(_sds((T, ATT_W), F32), _sds((T, ATT_W), F32), _sds(dz.shape, dz.dtype)),
        in_specs=[pl.BlockSpec((1, tq, S), lambda b, i: (i, 0, 0)), qblk, sblk, vblk, qblk,
                  pl.BlockSpec((tq, LANES), lambda b, i: (b * nq + i, 0)), qblk, pl.BlockSpec(memory_space=pl.ANY)],
        out_specs=(qblk, sblk, vblk), scratch_shapes=[pltpu.VMEM((S, ATT_W), F32)], name=name, carry=carry,
        aliases={7: 2})


def _sigmoid_pair(x):
    en = jnp.exp(-jnp.abs(x))
    big = 1.0 / (1.0 + en)
    small = en * big
    pos = x >= 0
    return jnp.where(pos, big, small), jnp.where(pos, small, big)


def _chunk_scan(x, pos, reverse):
    n = x.shape[0]
    sh = 1
    while sh < HG_CHUNK:
        if reverse:
            x = x + jnp.where(pos < HG_CHUNK - sh, pltpu.roll(x, n - sh, 0), 0.0)
        else:
            x = x + jnp.where(pos >= sh, pltpu.roll(x, sh, 0), 0.0)
        sh *= 2
    return x


def _hgrn_pair_table(tb):
    def body(o_ref):
        t = lax.broadcasted_iota(jnp.int32, (tb, tb), 0)
        s = lax.broadcasted_iota(jnp.int32, (tb, tb), 1)
        o_ref[...] = jnp.where((t // HG_CHUNK) == (s // HG_CHUNK), (t // HG_SUB) - (s // HG_SUB), 0).astype(F32)

    return _pcall(body, out_shape=_sds((tb, tb), F32), name="hgrn_pair_table")()


def _hgrn_cross_decays(b, pos):
    n = b.shape[0]
    pos8 = pos % HG_SUB
    end = b
    sh = 1
    while sh < HG_SUB:
        end = jnp.where((pos8 & sh) == 0, pltpu.roll(end, n - sh, 0), end)
        sh *= 2
    x = jnp.where(pos >= HG_SUB, jnp.exp(jnp.minimum(b - pltpu.roll(end, HG_SUB, 0), 0.0)), 0.0)
    ys = []
    for j in range(1, HG_CHUNK // HG_SUB):
        end_j = end if j == 1 else pltpu.roll(end, n - HG_SUB * (j - 1), 0)
        ys.append(jnp.where(pos < HG_CHUNK - HG_SUB * j, jnp.exp(jnp.minimum(end_j - b, 0.0)), 0.0))
    return x, ys


def _hgrn_cross_scores(qt, kts, pair):
    n = qt.shape[0]
    kcat = jnp.concatenate([k.astype(MXU_DTYPE) for k in kts], axis=0)
    p = lax.dot_general(qt.astype(MXU_DTYPE), kcat, (((1,), (1,)), ((), ())), preferred_element_type=F32)
    a = jnp.zeros((n, n), F32)
    for j in range(1, len(kts) + 1):
        a = jnp.where(pair == j, p[:, (j - 1) * n:j * n], a)
    return a, kcat


def _hgrn_gates(qh, fh, lb):
    sq, _ = _sigmoid_pair(qh)
    sg, sgn = _sigmoid_pair(fh)
    f = lb + (1.0 - lb) * sg
    g = jnp.log(jnp.maximum(f, F_MIN))
    kk = (1.0 - lb) * sgn
    return qh * sq, sq, sg, sgn, f, g, kk


def _hgrn_fwd(z, lb, pair, *, B, S, name):
    T = B * S
    tb = _tile(S, HG_BLOCK, HG_CHUNK)
    nc = tb // HG_CHUNK
    nb = S // tb
    dn_t = (((1,), (1,)), ((), ()))
    dn_o = (((0,), (0,)), ((), ()))

    def zcol(first):
        return lambda h, b, i: (b * nb + i, first + h)

    def body(q_ref, f_ref, v_ref, lb_ref, pair_ref, o_ref, st_ref, state):
        i = pl.program_id(2)

        @pl.when(i == 0)
        def _():
            state[...] = jnp.zeros_like(state)

        q, _, _, _, _, g, kk = _hgrn_gates(q_ref[...], f_ref[...], lb_ref[...])
        v = v_ref[...]
        pos = lax.broadcasted_iota(jnp.int32, (tb, LANES), 0) % HG_CHUNK
        pos8 = pos % HG_SUB
        b = _chunk_scan(g, pos, False)
        o = jnp.sum(q * kk, axis=1, keepdims=True) * v
        for d in range(1, HG_SUB):
            qd = pltpu.roll(q, tb - d, 0)
            bd = pltpu.roll(b, tb - d, 0)
            e = jnp.where(pos8 < HG_SUB - d, jnp.exp(jnp.minimum(bd - b, 0.0)), 0.0)
            a = jnp.sum(qd * kk * e, axis=1, keepdims=True)
            o = o + pltpu.roll(a * v, d, 0)
        x, ys = _hgrn_cross_decays(b, pos)
        a_cross, _ = _hgrn_cross_scores(q * x, [kk * y for y in ys], pair_ref[...])
        o = o + jnp.dot(a_cross.astype(MXU_DTYPE), v.astype(MXU_DTYPE), preferred_element_type=F32)
        qe = q * jnp.exp(b)
        chunks = [slice(c * HG_CHUNK, (c + 1) * HG_CHUNK) for c in range(nc)]
        b_last = [jnp.sum(jnp.where(pos[rs] == HG_CHUNK - 1, b[rs], 0.0), axis=0, keepdims=True) for rs in chunks]
        own = [lax.dot_general(v[rs].astype(MXU_DTYPE), (kk[rs] * jnp.exp(bl - b[rs])).astype(MXU_DTYPE), dn_o,
                               preferred_element_type=F32) for rs, bl in zip(chunks, b_last)]
        st = state[...]
        before = []
        for c in range(nc):
            before.append(st)
            st = st * jnp.exp(b_last[c]) + own[c]
        state[...] = st
        for c, rs in enumerate(chunks):
            st_ref[c * LANES:(c + 1) * LANES, :] = before[c]
            o_ref[rs, :] = o[rs] + lax.dot_general(qe[rs].astype(MXU_DTYPE), before[c].astype(MXU_DTYPE), dn_t,
                                                   preferred_element_type=F32)

    blk = (tb, LANES)
    return _run(
        body, (z, z, z, lb, pair), out_shape=(_sds((T, HG_W), F32), _sds((T // HG_CHUNK * LANES, HG_W), F32)),
        grid=(HG_HEADS, B, nb),
        in_specs=[pl.BlockSpec(blk, zcol(12)), pl.BlockSpec(blk, zcol(16)), pl.BlockSpec(blk, zcol(20)),
                  pl.BlockSpec((1, LANES), lambda h, b, i: (0, h)), pl.BlockSpec((tb, tb), lambda h, b, i: (0, 0))],
        out_specs=(pl.BlockSpec(blk, lambda h, b, i: (b * nb + i, h)),
                   pl.BlockSpec((nc * LANES, LANES), lambda h, b, i: (b * nb + i, h))),
        scratch_shapes=[pltpu.VMEM((LANES, LANES), F32)], name=name)


def _hgrn_bwd(z, lb, pair, states, do, *, B, S, name):
    T = B * S
    tb = _tile(S, HG_BLOCK, HG_CHUNK)
    nc = tb // HG_CHUNK
    nb = S // tb
    dn_t = (((1,), (1,)), ((), ()))
    dn_o = (((0,), (0,)), ((), ()))

    def body(q_ref, f_ref, v_ref, lb_ref, pair_ref, st_ref, do_ref, dq_ref, df_ref, di_ref, dlb_ref, dstate):
        bi = pl.program_id(1)
        i = pl.program_id(2)

        @pl.when(i == 0)
        def _():
            dstate[...] = jnp.zeros_like(dstate)

        qh = q_ref[...]
        lbv = lb_ref[...]
        q, sq, sg, sgn, f, g, kk = _hgrn_gates(qh, f_ref[...], lbv)
        v = v_ref[...]
        dov = do_ref[...]
        pos = lax.broadcasted_iota(jnp.int32, (tb, LANES), 0) % HG_CHUNK
        pos8 = pos % HG_SUB
        b = _chunk_scan(g, pos, False)

        a0 = jnp.sum(q * kk, axis=1, keepdims=True)
        da0 = jnp.sum(dov * v, axis=1, keepdims=True)
        dv = a0 * dov
        dkk = da0 * q
        dq = da0 * kk
        db = jnp.zeros((tb, LANES), F32)
        for d in range(1, HG_SUB):
            qd = pltpu.roll(q, tb - d, 0)
            bd = pltpu.roll(b, tb - d, 0)
            dod = pltpu.roll(dov, tb - d, 0)
            e = jnp.where(pos8 < HG_SUB - d, jnp.exp(jnp.minimum(bd - b, 0.0)), 0.0)
            a = jnp.sum(qd * kk * e, axis=1, keepdims=True)
            da = jnp.sum(dod * v, axis=1, keepdims=True)
            dv = dv + a * dod
            xe = da * e
            dkk = dkk + xe * qd
            t1 = xe * kk
            wt = t1 * qd
            dq = dq + pltpu.roll(t1, d, 0)
            db = db + pltpu.roll(wt, d, 0) - wt

        n_cross = HG_CHUNK // HG_SUB - 1
        pair = pair_ref[...]
        x, ys = _hgrn_cross_decays(b, pos)
        qt = q * x
        kts = [kk * y for y in ys]
        a_cross, kcat = _hgrn_cross_scores(qt, kts, pair)
        dom = dov.astype(MXU_DTYPE)
        da_all = lax.dot_general(dom, v.astype(MXU_DTYPE), dn_t, preferred_element_type=F32)
        dv = dv + lax.dot_general(a_cross.astype(MXU_DTYPE), dom, dn_o, preferred_element_type=F32)
        gcat = jnp.concatenate([jnp.where(pair == j, da_all, 0.0).astype(MXU_DTYPE) for j in range(1, n_cross + 1)],
                               axis=1)
        dqt = jnp.dot(gcat, kcat, preferred_element_type=F32)
        dkcat = lax.dot_general(gcat, qt.astype(MXU_DTYPE), dn_o, preferred_element_type=F32)
        dq = dq + dqt * x
        u = dqt * qt
        db = db + u
        d_end = -pltpu.roll(u, tb - HG_SUB, 0)
        for j in range(1, n_cross + 1):
            dkt = dkcat[(j - 1) * tb:j * tb]
            dkk = dkk + dkt * ys[j - 1]
            w = dkt * kts[j - 1]
            db = db - w
            d_end = d_end + (w if j == 1 else pltpu.roll(w, HG_SUB * (j - 1), 0))
        sh = 1
        while sh < HG_SUB:
            d_end = d_end + jnp.where(pos8 >= sh, pltpu.roll(d_end, sh, 0), 0.0)
            sh *= 2
        db = db + jnp.where(pos8 == HG_SUB - 1, d_end, 0.0)

        eb = jnp.exp(b)
        qe = q * eb
        chunks = [slice(c * HG_CHUNK, (c + 1) * HG_CHUNK) for c in range(nc)]
        b_lasts = [jnp.sum(jnp.where(pos[rs] == HG_CHUNK - 1, b[rs], 0.0), axis=0, keepdims=True) for rs in chunks]
        own = [lax.dot_general(dom[rs], qe[rs].astype(MXU_DTYPE), dn_o, preferred_element_type=F32) for rs in chunks]
        dst = dstate[...]
        after = [None] * nc
        for c in range(nc - 1, -1, -1):
            after[c] = dst
            dst = dst * jnp.exp(b_lasts[c]) + own[c]
        dstate[...] = dst
        dq_c, dkk_c, dv_c, db_c = [None] * nc, [None] * nc, [None] * nc, [None] * nc
        for c, rs in enumerate(chunks):
            st0 = st_ref[c * LANES:(c + 1) * LANES, :]
            eb_last = jnp.exp(b_lasts[c])
            er = jnp.exp(b_lasts[c] - b[rs])
            ke = kk[rs] * er
            dstm = after[c].astype(MXU_DTYPE)
            dqe = jnp.dot(dom[rs], st0.astype(MXU_DTYPE), preferred_element_type=F32)
            dv_c[c] = dv[rs] + lax.dot_general(ke.astype(MXU_DTYPE), dstm, dn_t, preferred_element_type=F32)
            dke = jnp.dot(v[rs].astype(MXU_DTYPE), dstm, preferred_element_type=F32)
            d_eb_last = jnp.sum(after[c] * st0, axis=0, keepdims=True)
            dq_c[c] = dq[rs] + dqe * eb[rs]
            dkk_c[c] = dkk[rs] + dke * er
            dkeke = dke * ke
            db_last = jnp.sum(dkeke, axis=0, keepdims=True) + d_eb_last * eb_last
            db_c[c] = db[rs] + dqe * qe[rs] - dkeke + jnp.where(pos[rs] == HG_CHUNK - 1, db_last, 0.0)
        dq = jnp.concatenate(dq_c, axis=0)
        dkk = jnp.concatenate(dkk_c, axis=0)
        dv = jnp.concatenate(dv_c, axis=0)
        dg = _chunk_scan(jnp.concatenate(db_c, axis=0), pos, True)

        dfv = dg * jnp.where(f > F_MIN, 1.0 / f, 0.0)
        dsg = (dfv - dkk) * (1.0 - lbv)
        dq_ref[...] = (dq * (sq * (1.0 + qh * (1.0 - sq)))).astype(dq_ref.dtype)
        df_ref[...] = (dsg * sg * sgn).astype(df_ref.dtype)
        di_ref[...] = dv.astype(di_ref.dtype)
        part = jnp.sum(dfv * sgn - dkk * sgn, axis=0, keepdims=True)

        @pl.when((bi == 0) & (i == 0))
        def _():
            dlb_ref[...] = part

        @pl.when((bi > 0) | (i > 0))
        def _():
            dlb_ref[...] += part

    def rev(first):
        return lambda h, b, i: (b * nb + (nb - 1 - i), first + h)

    blk = (tb, LANES)
    oblk = pl.BlockSpec(blk, rev(0))
    return _run(
        body, (z, z, z, lb, pair, states, do), out_shape=(_sds((T, HG_W), MXU_DTYPE),) * 3 + (_sds((1, HG_W), F32),),
        grid=(HG_HEADS, B, nb),
        in_specs=[pl.BlockSpec(blk, rev(12)), pl.BlockSpec(blk, rev(16)), pl.BlockSpec(blk, rev(20)),
                  pl.BlockSpec((1, LANES), lambda h, b, i: (0, h)), pl.BlockSpec((tb, tb), lambda h, b, i: (0, 0)),
                  pl.BlockSpec((nc * LANES, LANES), rev(0)), oblk],
        out_specs=(oblk, oblk, oblk, pl.BlockSpec((1, LANES), lambda h, b, i: (0, h))),
        scratch_shapes=[pltpu.VMEM((LANES, LANES), F32)], name=name)


def _hgout_fwd(o, z, g, ycat, *, name):
    T = o.shape[0]
    tm = _tile(T, ROW_BLOCK, SUBLANES)

    def body(o_ref, gh_ref, g_ref, _, y_ref):
        ov = o_ref[...]
        gh = gh_ref[...]
        r = lax.rsqrt(_group_mean(ov * ov, HG_HEAD_DIM) + EPS)
        sg, _ = _sigmoid_pair(gh)
        y_ref[...] = (ov * r * g_ref[...] * (gh * sg)).astype(y_ref.dtype)

    blk = pl.BlockSpec((tm, HG_W), lambda i: (i, 0))
    return _run(body, (o, z, g, ycat), out_shape=_sds(ycat.shape, ycat.dtype), grid=(T // tm,),
                in_specs=[blk, pl.BlockSpec((tm, HG_W), lambda i: (i, 6)), pl.BlockSpec((1, HG_W), lambda i: (0, 0)),
                          pl.BlockSpec(memory_space=pl.ANY)],
                out_specs=pl.BlockSpec((tm, HG_W), lambda i: (i, ATT_W // HG_W)), name=name, aliases={3: 0})


def _hgout_bwd(o, z, g, dycat, *, name):
    T = o.shape[0]
    tm = _tile(T, ROW_BLOCK, SUBLANES)

    def body(o_ref, gh_ref, g_ref, dy_ref, do_ref, dgh_ref, dg_ref):
        i = pl.program_id(0)
        ov = o_ref[...]
        gh = gh_ref[...]
        gv = g_ref[...]
        dy = dy_ref[...]
        r = lax.rsqrt(_group_mean(ov * ov, HG_HEAD_DIM) + EPS)
        xh = ov * r
        sg, _ = _sigmoid_pair(gh)
        dn = dy * (gh * sg)
        dgh_ref[...] = (dy * xh * gv * (sg * (1.0 + gh * (1.0 - sg)))).astype(dgh_ref.dtype)
        dng = dn * gv
        do_ref[...] = r * (dng - xh * _group_mean(dng * xh, HG_HEAD_DIM))
        part = jnp.sum(dn * xh, axis=0, keepdims=True)

        @pl.when(i == 0)
        def _():
            dg_ref[...] = part

        @pl.when(i > 0)
        def _():
            dg_ref[...] += part

    blk = pl.BlockSpec((tm, HG_W), lambda i: (i, 0))
    row = pl.BlockSpec((1, HG_W), lambda i: (0, 0))
    return _run(body, (o, z, g, dycat),
                out_shape=(_sds((T, HG_W), F32), _sds(z.shape, MXU_DTYPE), _sds((1, HG_W), F32)), grid=(T // tm,),
                in_specs=[blk, pl.BlockSpec((tm, HG_W), lambda i: (i, 6)), row,
                          pl.BlockSpec((tm, HG_W), lambda i: (i, 1))],
                out_specs=(blk, pl.BlockSpec((tm, HG_W), lambda i: (i, 6)), row), name=name)


def _lower_bounds_fwd(hg_lb):
    L = hg_lb.shape[0]

    def body(x_ref, o_ref):
        rows = [x_ref[l:l + 1, :] for l in range(L)]
        mx = functools.reduce(jnp.maximum, rows)
        es = [jnp.exp(r - mx) for r in rows]
        tot = functools.reduce(jnp.add, es)
        acc = jnp.zeros_like(tot)
        for l in range(L):
            acc = acc + es[l] / tot
            o_ref[l:l + 1, :] = acc - es[0] / tot

    return _pcall(body, out_shape=_sds(hg_lb.shape, F32), name="lower_bounds_fwd")(hg_lb)


def _lower_bounds_bwd(hg_lb, dlower):
    L = hg_lb.shape[0]

    def body(x_ref, d_ref, o_ref):
        rows = [x_ref[l:l + 1, :] for l in range(L)]
        ds = [d_ref[l:l + 1, :] for l in range(L)]
        mx = functools.reduce(jnp.maximum, rows)
        es = [jnp.exp(r - mx) for r in rows]
        tot = functools.reduce(jnp.add, es)
        ps = [e / tot for e in es]
        dps = []
        for j in range(L):
            t = functools.reduce(jnp.add, ds[j:])
            if j == 0:
                t = t - functools.reduce(jnp.add, ds)
            dps.append(t)
        inner = functools.reduce(jnp.add, [p * dp for p, dp in zip(ps, dps)])
        for j in range(L):
            o_ref[j:j + 1, :] = ps[j] * (dps[j] - inner)

    return _pcall(body, out_shape=_sds(hg_lb.shape, F32), name="lower_bounds_bwd")(hg_lb, dlower)


def _memattn_fwd(qm, km, kv, *, B, S, NM, name):
    T, D = qm.shape
    E = D // MEM_HEADS
    tq = _tile(S, ROW_BLOCK, SUBLANES)
    nq = S // tq
    scale = 1.0 / math.sqrt(E)

    def body(q_ref, k_ref, v_ref, o_ref):
        for h in range(MEM_HEADS):
            cs = slice(h * E, (h + 1) * E)
            s = lax.dot_general(q_ref[:, cs].astype(MXU_DTYPE), k_ref[:, cs].astype(MXU_DTYPE),
                                (((1,), (1,)), ((), ())), preferred_element_type=F32) * scale
            e = jnp.exp(s - jnp.max(s, axis=1, keepdims=True))
            pr = e / jnp.sum(e, axis=1, keepdims=True)
            o_ref[:, cs] = jnp.dot(pr.astype(MXU_DTYPE), v_ref[:, cs].astype(MXU_DTYPE),
                                   preferred_element_type=F32).astype(o_ref.dtype)

    return _run(
        body, (qm, km, kv), out_shape=_sds((T, D), MXU_DTYPE), grid=(B, nq),
        in_specs=[pl.BlockSpec((tq, D), lambda b, i: (b * nq + i, 0)), pl.BlockSpec((NM, D), lambda b, i: (b, 0)),
                  pl.BlockSpec((NM, D), lambda b, i: (b, 1))],
        out_specs=pl.BlockSpec((tq, D), lambda b, i: (b * nq + i, 0)), name=name)


def _memattn_bwd(qm, km, kv, do, *, B, S, NM, name):
    T, D = qm.shape
    E = D // MEM_HEADS
    tq = _tile(S, ROW_BLOCK, SUBLANES)
    nq = S // tq
    scale = 1.0 / math.sqrt(E)

    def body(q_ref, k_ref, v_ref, do_ref, dq_ref, dk_ref, dv_ref):
        i = pl.program_id(1)

        @pl.when(i == 0)
        def _():
            dk_ref[...] = jnp.zeros_like(dk_ref)
            dv_ref[...] = jnp.zeros_like(dv_ref)

        for h in range(MEM_HEADS):
            cs = slice(h * E, (h + 1) * E)
            qh = q_ref[:, cs].astype(MXU_DTYPE)
            kh = k_ref[:, cs].astype(MXU_DTYPE)
            vh = v_ref[:, cs].astype(MXU_DTYPE)
            doh = do_ref[:, cs].astype(MXU_DTYPE)
            s = lax.dot_general(qh, kh, (((1,), (1,)), ((), ())), preferred_element_type=F32) * scale
            e = jnp.exp(s - jnp.max(s, axis=1, keepdims=True))
            pr = e / jnp.sum(e, axis=1, keepdims=True)
            dp = lax.dot_general(doh, vh, (((1,), (1,)), ((), ())), preferred_element_type=F32)
            ds = (pr * (dp - jnp.sum(dp * pr, axis=1, keepdims=True))).astype(MXU_DTYPE)
            dq_ref[:, cs] = jnp.dot(ds, kh, preferred_element_type=F32) * scale
            dk_ref[:, cs] += lax.dot_general(ds, qh, (((0,), (0,)), ((), ())), preferred_element_type=F32) * scale
            dv_ref[:, cs] += lax.dot_general(pr.astype(MXU_DTYPE), doh, (((0,), (0,)), ((), ())),
                                             preferred_element_type=F32)

    qblk = pl.BlockSpec((tq, D), lambda b, i: (b * nq + i, 0))
    mblk = pl.BlockSpec((NM, D), lambda b, i: (b, 0))
    return _run(
        body, (qm, km, kv, do), out_shape=(_sds((T, D), F32), _sds((B * NM, D), F32), _sds((B * NM, D), F32)),
        grid=(B, nq), in_specs=[qblk, mblk, pl.BlockSpec((NM, D), lambda b, i: (b, 1)), qblk],
        out_specs=(qblk, mblk, mblk), name=name)


def _loss_head(y, tgt):
    T, D = y.shape
    tm = _tile(T, ROW_BLOCK, SUBLANES)

    def body(y_ref, t_ref, dy_ref, dyl_ref, l_ref):
        i = pl.program_id(0)
        err = y_ref[...] - t_ref[...]
        dy_ref[...] = err * (1.0 / D)
        dyl_ref[...] = (err * (1.0 / D)).astype(MXU_DTYPE)
        part = jnp.sum(jnp.sum(err * err, axis=1, keepdims=True), axis=0, keepdims=True) * (0.5 / D)

        @pl.when(i == 0)
        def _():
            l_ref[...] = part

        @pl.when(i > 0)
        def _():
            l_ref[...] += part

    blk = pl.BlockSpec((tm, D), lambda i: (i, 0))
    return _run(body, (y, tgt), out_shape=(_sds((T, D), F32), _sds((T, D), MXU_DTYPE), _sds((1, 1), F32)),
                grid=(T // tm,), in_specs=[blk, blk], out_specs=(blk, blk, pl.BlockSpec((1, 1), lambda i: (0, 0))),
                name="loss_head")


def _adamw(w, g, m, v):
    m = ADAM_B1 * m + (1.0 - ADAM_B1) * g
    v = ADAM_B2 * v + (1.0 - ADAM_B2) * jnp.square(g)
    m_hat = m / (1.0 - ADAM_B1 ** ADAM_STEP)
    v_hat = v / (1.0 - ADAM_B2 ** ADAM_STEP)
    delta = -ADAM_LR * (m_hat / (jnp.sqrt(v_hat) + ADAM_EPS) + ADAM_WD * w)
    return delta, m, v


def _sum_adamw(recv, w, m, v, layer, so_far=None):
    L, r, c = w.shape
    tr = _tile(r, ADAM_BLOCK, SUBLANES)

    def body(r_ref, w_ref, m_ref, v_ref, *rest):
        g_ref, d_ref, nm_ref, nv_ref = rest[-4:]
        g = r_ref[0].astype(F32)
        for j in range(1, N_DEV):
            g = g + r_ref[j].astype(F32)
        g_ref[...] = g
        d_ref[...], nm_ref[...], nv_ref[...] = _adamw(w_ref[...], g, m_ref[...], v_ref[...])

    blk = pl.BlockSpec((None, tr, c), lambda i: (layer, i, 0))
    hbm = pl.BlockSpec(memory_space=pl.ANY)
    args, in_specs, aliases = (recv, w, m, v), [pl.BlockSpec((N_DEV, tr, c), lambda i: (0, i, 0)), blk, blk, blk], {}
    if so_far is not None:
        args, in_specs, aliases = args + tuple(so_far), in_specs + [hbm] * 4, {4 + k: k for k in range(4)}
    return _run(body, args, out_shape=(_sds((L, r, c), F32),) * 4, grid=(r // tr,), in_specs=in_specs,
                out_specs=(blk,) * 4, name="grad_sum_adamw", aliases=aliases)


def _small_sum_adamw(gp, w, m, v):
    NR, C = gp.shape
    MESH = pl.DeviceIdType.MESH

    def body(gp_ref, w_ref, m_ref, v_ref, g_ref, d_ref, nm_ref, nv_ref, gath, send_sems, recv_sems):
        x, y, c = _mesh_pos()
        me = 4 * x + 2 * y + c
        gath[me] = gp_ref[...]
        copies = []
        for k in range(1, N_DEV):
            peer = (_flip(x, k & 4), _flip(y, k & 2), _flip(c, k & 1))
            cp = pltpu.make_async_remote_copy(src_ref=gp_ref, dst_ref=gath.at[me], send_sem=send_sems.at[k - 1],
                                              recv_sem=recv_sems.at[k - 1], device_id=peer, device_id_type=MESH)
            cp.start()
            copies.append(cp)
        for cp in copies:
            cp.wait()
        g = gath[0]
        for j in range(1, N_DEV):
            g = g + gath[j]
        g_ref[...] = g
        d_ref[...], nm_ref[...], nv_ref[...] = _adamw(w_ref[...], g, m_ref[...], v_ref[...])

    vm = pl.BlockSpec(memory_space=pltpu.VMEM)
    return _pcall(body, out_shape=(_sds((NR, C), F32),) * 4, in_specs=[vm] * 4, out_specs=(vm,) * 4,
                  scratch_shapes=[pltpu.VMEM((N_DEV, NR, C), F32), pltpu.SemaphoreType.DMA((7,)),
                                  pltpu.SemaphoreType.DMA((7,))], name="small_sum_adamw")(gp, w, m, v)


def _gathered(gath, name):
    _, r, c = gath.shape
    return gath.transpose(1, 0, 2).reshape(r, N_DEV * c) if name in COL_SHARDED else gath.reshape(N_DEV * r, c)


def _scatter_blocks(grad, name):
    rows, cols = grad.shape
    if name in COL_SHARDED:
        return grad.reshape(rows, N_DEV, cols // N_DEV).transpose(1, 0, 2)
    return grad.reshape(N_DEV, rows // N_DEV, cols)


def _pack_small(d, prefix, loss=None):
    parts = []
    for n in SMALL:
        a = d[prefix + n].reshape(-1).astype(F32)
        pad = (-a.shape[0]) % LANES
        parts.append(jnp.pad(a, (0, pad)) if pad else a)
    tail = jnp.zeros((LANES,), F32)
    if loss is not None:
        tail = tail.at[0].set(loss)
    flat = jnp.concatenate(parts + [tail])
    pad = (-flat.shape[0]) % (8 * LANES)
    return jnp.pad(flat, (0, pad)).reshape(-1, LANES)


def _unpack_small(packed, shapes):
    flat = packed.reshape(-1)
    out, off = {}, 0
    for n in SMALL:
        size = math.prod(shapes[n])
        out[n] = flat[off:off + size].reshape(shapes[n])
        off += size + (-size) % LANES
    return out, flat[off]


def _row(a):
    return a.reshape(1, -1).astype(F32)


def _layer_fwd(x, memf, w_in, shards, next_in, sp, lb, tab, dims):
    B, S, NM = dims
    D = x.shape[1]
    if w_in is None:
        h1, gath = _norm_fwd(x, sp["norm1_g"], E=D, W=D, out_dtype=MXU_DTYPE, name="norm1_fwd",
                             carry=("gather", [shards["w_in"].astype(MXU_DTYPE)]))
        w_in = _gathered(gath[0], "w_in")
    else:
        h1 = _norm_fwd(x, sp["norm1_g"], E=D, W=D, out_dtype=MXU_DTYPE, name="norm1_fwd")
    W = {"w_in": w_in}
    z, gath = _matmul(h1, w_in, carry=("gather", [shards[n].astype(MXU_DTYPE) for n in GATHER_MIX]), name="in_proj")
    W.update({n: _gathered(t, n) for n, t in zip(GATHER_MIX, gath)})
    qn = _norm_fwd(z, sp["attn_qn_g"], E=ATT_HEAD_DIM, W=ATT_W, cb=0, out_dtype=MXU_DTYPE, name="qnorm_fwd")
    kn = _norm_fwd(z, sp["attn_kn_g"], E=ATT_HEAD_DIM, W=ATT_W, cb=1, out_dtype=MXU_DTYPE, name="knorm_fwd")
    (ya, lse, ycat), gath = _attn_fwd(tab["attn"], qn, kn, z, B=B, S=S, name="dilated_attn_fwd",
                                carry=("gather", [shards[n].astype(MXU_DTYPE) for n in GATHER_FF]))
    W.update({n: _gathered(t, n) for n, t in zip(GATHER_FF, gath)})
    o, states = _hgrn_fwd(z, lb, tab["hgrn"], B=B, S=S, name="hgrn_fwd")
    ycat = _hgout_fwd(o, z, sp["hg_onorm_g"], ycat, name="hgrn_out_fwd")
    x1 = _matmul(ycat, W["w_out"], epi="add", extra=x, name="out_proj")
    h2 = _norm_fwd(x1, sp["norm2_g"], E=D, W=D, out_dtype=MXU_DTYPE, name="norm2_fwd")
    qmp = _matmul(h2, W["w_mq"], name="mq_proj")
    qm = _norm_fwd(qmp, sp["mq_norm_g"], E=D // MEM_HEADS, W=D, out_dtype=MXU_DTYPE, name="mqnorm_fwd")
    mn = _norm_fwd(memf, sp["mem_norm_g"], E=D, W=D, out_dtype=MXU_DTYPE, name="memnorm_fwd")
    kv = _matmul(mn, W["w_mkv"], name="mkv_proj")
    km = _norm_fwd(kv, sp["mk_norm_g"], E=D // MEM_HEADS, W=D, cb=0, out_dtype=MXU_DTYPE, name="mknorm_fwd")
    om = _memattn_fwd(qm, km, kv, B=B, S=S, NM=NM, name="mem_attn_fwd")
    x2 = _matmul(om, W["w_mo"], epi="add", extra=x1, name="mo_proj")
    h3 = _norm_fwd(x2, sp["norm3_g"], E=D, W=D, out_dtype=MXU_DTYPE, name="norm3_fwd")
    if next_in is None:
        u, w_in_next = _matmul(h3, W["w_ff1"], epi="relu2", out_dtype=MXU_DTYPE, name="ff1"), None
    else:
        u, gath = _matmul(h3, W["w_ff1"], epi="relu2", out_dtype=MXU_DTYPE,
                          carry=("gather", [next_in.astype(MXU_DTYPE)]), name="ff1")
        w_in_next = _gathered(gath[0], "w_in")
    x3 = _matmul(u, W["w_ff2"], epi="add", extra=x2, name="ff2")
    saved = dict(x=x, h1=h1, z=z, qn=qn, kn=kn, ya=ya, lse=lse, o=o, states=states, ycat=ycat, x1=x1, h2=h2,
                 qmp=qmp, qm=qm, mn=mn, kv=kv, km=km, om=om, x2=x2, h3=h3, u=u)
    return x3, saved, W, w_in_next


def _layer_bwd(dx3, dx3_low, s, memf, W, sp, lb, tab, dims):
    B, S, NM = dims
    D = dx3.shape[1]
    E_M = D // MEM_HEADS
    gw, gs, recv = {}, {}, {}

    def blocks(names):
        return ("scatter", [_scatter_blocks(gw[n], n) for n in names])

    def received(names, got):
        recv.update(dict(zip(names, got)))

    da = _matmul(dx3_low, W["w_ff2"], tb=True, epi="relu2grad", extra=s["u"], out_dtype=MXU_DTYPE, name="ff2_dgrad")
    gw["w_ff2"] = _matmul(s["u"], dx3_low, ta=True, out_dtype=MXU_DTYPE, name="ff2_wgrad")
    dh3, got = _matmul(da, W["w_ff1"], tb=True, out_dtype=MXU_DTYPE, carry=blocks(SCATTER_A), name="ff1_dgrad")
    received(SCATTER_A, got)
    gw["w_ff1"] = _matmul(s["h3"], da, ta=True, out_dtype=MXU_DTYPE, name="ff1_wgrad")
    dx2, dx2_low, gs["norm3_g"] = _norm_bwd(s["x2"], sp["norm3_g"], dh3, E=D, W=D, res=dx3, low=True, name="norm3_bwd")
    dom = _matmul(dx2_low, W["w_mo"], tb=True, out_dtype=MXU_DTYPE, name="mo_dgrad")
    gw["w_mo"] = _matmul(s["om"], dx2_low, ta=True, out_dtype=MXU_DTYPE, name="mo_wgrad")
    dqm, dkm, dvm = _memattn_bwd(s["qm"], s["km"], s["kv"], dom, B=B, S=S, NM=NM, name="mem_attn_bwd")
    dqmp, gs["mq_norm_g"] = _norm_bwd(s["qmp"], sp["mq_norm_g"], dqm, E=E_M, W=D, fold=E_M, out_dtype=MXU_DTYPE,
                                      name="mqnorm_bwd")
    dkmp, gs["mk_norm_g"] = _norm_bwd(s["kv"], sp["mk_norm_g"], dkm, E=E_M, W=D, cb=0, fold=E_M,
                                      out_dtype=MXU_DTYPE, name="mknorm_bwd")
    dkv = jnp.concatenate([dkmp, dvm.astype(MXU_DTYPE)], axis=1)
    dh2 = _matmul(dqmp, W["w_mq"], tb=True, out_dtype=MXU_DTYPE, name="mq_dgrad")
    gw["w_mq"] = _matmul(s["h2"], dqmp, ta=True, out_dtype=MXU_DTYPE, name="mq_wgrad")
    dmn = _matmul(dkv, W["w_mkv"], tb=True, out_dtype=MXU_DTYPE, name="mkv_dgrad")
    gw["w_mkv"] = _matmul(s["mn"], dkv, ta=True, out_dtype=MXU_DTYPE, name="mkv_wgrad")
    _, gs["mem_norm_g"] = _norm_bwd(memf, sp["mem_norm_g"], dmn, E=D, W=D, name="memnorm_bwd")
    dx1, dx1_low, gs["norm2_g"] = _norm_bwd(s["x1"], sp["norm2_g"], dh2, E=D, W=D, res=dx2, low=True, name="norm2_bwd")
    dycat = _matmul(dx1_low, W["w_out"], tb=True, out_dtype=MXU_DTYPE, name="out_dgrad")
    gw["w_out"] = _matmul(s["ycat"], dx1_low, ta=True, out_dtype=MXU_DTYPE, name="out_wgrad")
    do_hg, dz, gs["hg_onorm_g"] = _hgout_bwd(s["o"], s["z"], sp["hg_onorm_g"], dycat, name="hgrn_out_bwd")
    dqh, dfh, dih, dlb = _hgrn_bwd(s["z"], lb, tab["hgrn"], s["states"], do_hg, B=B, S=S, name="hgrn_bwd")
    (dqn, dkn, dz), got = _attn_bwd(tab["attn"], s["qn"], s["kn"], s["z"], s["ya"], s["lse"], dycat, dz, B=B, S=S,
                                     carry=blocks(SCATTER_B), name="dilated_attn_bwd")
    received(SCATTER_B, got)
    dz, gs["attn_qn_g"] = _norm_bwd(s["z"], sp["attn_qn_g"], dqn, E=ATT_HEAD_DIM, W=ATT_W, cb=0, fold=ATT_HEAD_DIM,
                                    into=dz, name="qnorm_bwd")
    dz, gs["attn_kn_g"] = _norm_bwd(s["z"], sp["attn_kn_g"], dkn, E=ATT_HEAD_DIM, W=ATT_W, cb=1, fold=ATT_HEAD_DIM,
                                    into=dz, name="knorm_bwd")
    for block, t in ((3, dqh), (4, dfh), (5, dih)):
        dz = lax.dynamic_update_slice(dz, t, (0, block * HG_W))
    gw["w_in"] = _matmul(s["h1"], dz, ta=True, out_dtype=MXU_DTYPE, name="in_wgrad")
    dh1, got = _matmul(dz, W["w_in"], tb=True, out_dtype=MXU_DTYPE, carry=blocks(SCATTER_C), name="in_dgrad")
    received(SCATTER_C, got)
    dx0, dx0_low, gs["norm1_g"] = _norm_bwd(s["x"], sp["norm1_g"], dh1, E=D, W=D, res=dx1, low=True, name="norm1_bwd")
    gs["attn_qn_g"] = gs["attn_qn_g"][:, :ATT_HEAD_DIM]
    gs["attn_kn_g"] = gs["attn_kn_g"][:, :ATT_HEAD_DIM]
    return dx0, dx0_low, recv, gs, dlb


def kernel(x, mem, norm1_g, w_in, attn_qn_g, attn_kn_g, hg_lb, hg_onorm_g, w_out, norm2_g, mem_norm_g, w_mq, w_mkv, mq_norm_g, mk_norm_g, w_mo, norm3_g, w_ff1, w_ff2, loss_target, m_norm1_g, m_w_in, m_attn_qn_g, m_attn_kn_g, m_hg_lb, m_hg_onorm_g, m_w_out, m_norm2_g, m_mem_norm_g, m_w_mq, m_w_mkv, m_mq_norm_g, m_mk_norm_g, m_w_mo, m_norm3_g, m_w_ff1, m_w_ff2, v_norm1_g, v_w_in, v_attn_qn_g, v_attn_kn_g, v_hg_lb, v_hg_onorm_g, v_w_out, v_norm2_g, v_mem_norm_g, v_w_mq, v_w_mkv, v_mq_norm_g, v_mk_norm_g, v_w_mo, v_norm3_g, v_w_ff1, v_w_ff2):
    given = dict(norm1_g=norm1_g, w_in=w_in, attn_qn_g=attn_qn_g, attn_kn_g=attn_kn_g, hg_lb=hg_lb, hg_onorm_g=hg_onorm_g, w_out=w_out, norm2_g=norm2_g, mem_norm_g=mem_norm_g, w_mq=w_mq, w_mkv=w_mkv, mq_norm_g=mq_norm_g, mk_norm_g=mk_norm_g, w_mo=w_mo, norm3_g=norm3_g, w_ff1=w_ff1, w_ff2=w_ff2, m_norm1_g=m_norm1_g, m_w_in=m_w_in, m_attn_qn_g=m_attn_qn_g, m_attn_kn_g=m_attn_kn_g, m_hg_lb=m_hg_lb, m_hg_onorm_g=m_hg_onorm_g, m_w_out=m_w_out, m_norm2_g=m_norm2_g, m_mem_norm_g=m_mem_norm_g, m_w_mq=m_w_mq, m_w_mkv=m_w_mkv, m_mq_norm_g=m_mq_norm_g, m_mk_norm_g=m_mk_norm_g, m_w_mo=m_w_mo, m_norm3_g=m_norm3_g, m_w_ff1=m_w_ff1, m_w_ff2=m_w_ff2, v_norm1_g=v_norm1_g, v_w_in=v_w_in, v_attn_qn_g=v_attn_qn_g, v_attn_kn_g=v_attn_kn_g, v_hg_lb=v_hg_lb, v_hg_onorm_g=v_hg_onorm_g, v_w_out=v_w_out, v_norm2_g=v_norm2_g, v_mem_norm_g=v_mem_norm_g, v_w_mq=v_w_mq, v_w_mkv=v_w_mkv, v_mq_norm_g=v_mq_norm_g, v_mk_norm_g=v_mk_norm_g, v_w_mo=v_w_mo, v_norm3_g=v_norm3_g, v_w_ff1=v_w_ff1, v_w_ff2=v_w_ff2)
    B, S, D = x.shape
    NM = mem.shape[1]
    L = w_in.shape[0]
    dims = (B, S, NM)
    small_shapes = {n: given[n].shape for n in SMALL}

    def shards(prefix, l):
        return {n: given[prefix + n][l] for n in BIG}

    lower = _lower_bounds_fwd(hg_lb)
    tab = {"attn": _attn_tables(S, _tile(S, ATT_Q_BLOCK, SUBLANES)), "hgrn": _hgrn_pair_table(_tile(S, HG_BLOCK, HG_CHUNK))}
    xf = x.reshape(B * S, D)
    memf = mem.reshape(B * NM, D)

    def small_params(l):
        sp = {"norm1_g": _row(norm1_g[l]), "norm2_g": _row(norm2_g[l]), "norm3_g": _row(norm3_g[l]),
              "mem_norm_g": _row(mem_norm_g[l]), "hg_onorm_g": _row(hg_onorm_g[l]),
              "attn_qn_g": _row(jnp.tile(attn_qn_g[l], ATT_HEADS)), "attn_kn_g": _row(jnp.tile(attn_kn_g[l], ATT_HEADS)),
              "mq_norm_g": _row(jnp.tile(mq_norm_g[l], MEM_HEADS)), "mk_norm_g": _row(jnp.tile(mk_norm_g[l], MEM_HEADS))}
        return sp, _row(lower[l])

    w_in_full = None
    saved, weights = [], []
    h = xf
    for l in range(L):
        sp, lb = small_params(l)
        next_in = given["w_in"][l + 1] if l + 1 < L else None
        h, s, W, w_in_full = _layer_fwd(h, memf, w_in_full, shards("", l), next_in, sp, lb, tab, dims)
        saved.append(s)
        weights.append(W)

    dh, dh_low, loss_part = _loss_head(h, loss_target.reshape(B * S, D))

    recv_layers, gs_layers, dlb_layers = [None] * L, [None] * L, [None] * L
    for l in range(L - 1, -1, -1):
        sp, lb = small_params(l)
        dh, dh_low, recv_layers[l], gs_layers[l], dlb_layers[l] = _layer_bwd(dh, dh_low, saved[l], memf, weights[l],
                                                                              sp, lb, tab, dims)
    grad_x = dh.reshape(B, S, D)

    big_out = {}
    for n in BIG:
        outs = None
        for l in range(L):
            outs = _sum_adamw(recv_layers[l][n], given[n], given["m_" + n], given["v_" + n], l, outs)
        big_out[n] = outs

    gs = {n: jnp.stack([gs_layers[l][n].reshape(small_shapes[n][1:]) for l in range(L)]) for n in SMALL if n != "hg_lb"}
    gs["hg_lb"] = _lower_bounds_bwd(hg_lb, jnp.concatenate(dlb_layers, axis=0))
    packed = _small_sum_adamw(_pack_small(gs, "", loss_part[0, 0]), _pack_small(given, ""), _pack_small(given, "m_"),
                              _pack_small(given, "v_"))
    small_out, loss = [], None
    for t in packed:
        d, tail = _unpack_small(t, small_shapes)
        small_out.append(d)
        loss = tail if loss is None else loss

    outs = [loss, grad_x]
    for k in range(4):
        outs += [small_out[k][n] if n in small_shapes else big_out[n][k] for n in WEIGHTS]
    return tuple(outs)
```

```python
import functools
import math

import jax
import jax.numpy as jnp
from jax import lax
from jax.experimental import pallas as pl
from jax.experimental.pallas import tpu as pltpu

F32 = jnp.float32
MXU_DTYPE = jnp.bfloat16
VMEM_LIMIT = 48 * 1024 * 1024

N_DEV = 8
EPS = 1e-6
NEG = -1e30
F_MIN = 1e-12
ATT_HEADS = 8
ATT_HEAD_DIM = 64
ATT_W = ATT_HEADS * ATT_HEAD_DIM
DILATIONS = (1, 4, 16)
DIL_STEPS = 128
ATT_Q_BLOCK = 256
ATT_KEY_EXTENT = 512
HG_HEADS = 4
HG_HEAD_DIM = 128
HG_W = HG_HEADS * HG_HEAD_DIM
HG_CHUNK = 32
HG_SUB = 4
MEM_HEADS = 4
LANES = 128
SUBLANES = 8
ROW_BLOCK = 1024
MM_BLOCK = 1024
MM_K_BLOCK = 2048
HG_BLOCK = 256
ADAM_BLOCK = 256

ADAM_LR = 0.001
ADAM_B1 = 0.9
ADAM_B2 = 0.999
ADAM_EPS = 1e-08
ADAM_WD = 0.01
ADAM_STEP = 10

BIG = ("w_in", "w_out", "w_mq", "w_mkv", "w_mo", "w_ff1", "w_ff2")
COL_SHARDED = ("w_in", "w_mkv", "w_ff1")
SMALL = ("norm1_g", "attn_qn_g", "attn_kn_g", "hg_lb", "hg_onorm_g", "norm2_g", "mem_norm_g",
         "mq_norm_g", "mk_norm_g", "norm3_g")
WEIGHTS = ("norm1_g", "w_in", "attn_qn_g", "attn_kn_g", "hg_lb", "hg_onorm_g", "w_out", "norm2_g",
           "mem_norm_g", "w_mq", "w_mkv", "mq_norm_g", "mk_norm_g", "w_mo", "norm3_g", "w_ff1", "w_ff2")
GATHER_MIX = ("w_out", "w_mq", "w_mkv", "w_mo")
GATHER_FF = ("w_ff1", "w_ff2")
SCATTER_A = ("w_ff2",)
SCATTER_B = ("w_ff1", "w_mo", "w_mq", "w_mkv", "w_out")
SCATTER_C = ("w_in",)


def _pcall(body, **kw):
    return pl.pallas_call(body, **kw)


def _params():
    return pltpu.CompilerParams(vmem_limit_bytes=VMEM_LIMIT)


def _tile(n, pref, mult):
    t = (min(n, pref) // mult) * mult
    while t >= mult:
        if n % t == 0:
            return t
        t -= mult
    return n


def _sds(shape, dtype):
    return jax.ShapeDtypeStruct(shape, dtype)


def _mesh_pos():
    return lax.axis_index("x"), lax.axis_index("y"), lax.axis_index("c")


def _flip(v, bit):
    return 1 - v if bit else v


def _gather_hooks(x_refs, out_refs, send_sems, recv_sems, local_sems):
    MESH = pl.DeviceIdType.MESH
    n = len(x_refs)
    x, y, c = _mesh_pos()
    me, sibling = (x, y, c), (x, y, 1 - c)
    chips = [(1 - x, y), (x, 1 - y), (1 - x, 1 - y)]

    def rows(a, px, py, pc):
        return out_refs[a].at[4 * px + 2 * py + pc]

    def copy(a, k, block, to, own=False):
        return pltpu.make_async_remote_copy(
            src_ref=x_refs[a] if own else rows(a, *block), dst_ref=rows(a, *block), send_sem=send_sems.at[n * k + a],
            recv_sem=recv_sems.at[n * k + a], device_id=to, device_id_type=MESH)

    mine = [pltpu.make_async_copy(x_refs[a], rows(a, *me), local_sems.at[a]) for a in range(n)]
    first = [copy(a, 0, me, sibling, own=True) for a in range(n)]
    first += [copy(a, 1 + j, me, (*chip, c), own=True) for j, chip in enumerate(chips) for a in range(n)]
    passed = [[copy(a, 4 + j, (*chip, c), sibling) for a in range(n)] for j, chip in enumerate(chips)]

    def start():
        for cp in mine + first:
            cp.start()

    def forward():
        for j, chip in enumerate(chips):
            for a in range(n):
                copy(a, 1 + j, (*chip, c), me).wait_recv()
                passed[j][a].start()

    def finish():
        for a in range(n):
            copy(a, 0, sibling, me).wait_recv()
        for j, chip in enumerate(chips):
            for a in range(n):
                copy(a, 4 + j, (*chip, 1 - c), me).wait_recv()
        for cp in first + [cp for group in passed for cp in group]:
            cp.wait_send()
        for cp in mine:
            cp.wait()

    return start, forward, finish


def _scatter_hooks(s_refs, r_refs, send_sems, recv_sems, local_sems):
    MESH = pl.DeviceIdType.MESH
    n = len(s_refs)
    x, y, c = _mesh_pos()
    me = 4 * x + 2 * y + c
    mine = [pltpu.make_async_copy(s_refs[a].at[me], r_refs[a].at[me], local_sems.at[a]) for a in range(n)]
    copies = []
    for m in range(1, N_DEV):
        px, py, pc = _flip(x, m & 4), _flip(y, m & 2), _flip(c, m & 1)
        for a in range(n):
            copies.append(pltpu.make_async_remote_copy(
                src_ref=s_refs[a].at[4 * px + 2 * py + pc], dst_ref=r_refs[a].at[me],
                send_sem=send_sems.at[n * (m - 1) + a], recv_sem=recv_sems.at[n * (m - 1) + a],
                device_id=(px, py, pc), device_id_type=MESH))

    def start():
        for cp in mine + copies:
            cp.start()

    def finish():
        for cp in copies:
            cp.wait()
        for cp in mine:
            cp.wait()

    return start, None, finish


def _exchange_sems(n):
    return [pltpu.SemaphoreType.DMA(((N_DEV - 1) * n,)), pltpu.SemaphoreType.DMA(((N_DEV - 1) * n,)),
            pltpu.SemaphoreType.DMA((n,))]


def _exchange_out(kind, xs):
    return _sds((N_DEV,) + xs.shape, xs.dtype) if kind == "gather" else _sds(xs.shape, xs.dtype)


def _run(body, args, *, out_shape, grid, in_specs, out_specs, scratch_shapes=(), name, carry=None, aliases=None):
    single = not isinstance(out_shape, (tuple, list))
    outs = (out_shape,) if single else tuple(out_shape)
    ospecs = (out_specs,) if single else tuple(out_specs)
    aliases = dict(aliases or {})
    if carry is None:
        res = _pcall(body, out_shape=outs, grid=grid, in_specs=list(in_specs), out_specs=ospecs,
                     scratch_shapes=list(scratch_shapes), input_output_aliases=aliases, compiler_params=_params(),
                     name=name)(*args)
        return res[0] if single else tuple(res)
    kind, xs = carry
    n_in, n_out, n_scr, n_x = len(args), len(outs), len(scratch_shapes), len(xs)
    total = math.prod(grid)

    def wrapped(*refs):
        refs = list(refs)
        ins, x_refs = refs[:n_in], refs[n_in:n_in + n_x]
        o, xo_refs = refs[n_in + n_x:n_in + n_x + n_out], refs[n_in + n_x + n_out:n_in + 2 * n_x + n_out]
        scr = refs[n_in + 2 * n_x + n_out:n_in + 2 * n_x + n_out + n_scr]
        sems = refs[n_in + 2 * n_x + n_out + n_scr:]
        step = pl.program_id(0)
        for ax in range(1, len(grid)):
            step = step * grid[ax] + pl.program_id(ax)
        start, forward, finish = (_gather_hooks if kind == "gather" else _scatter_hooks)(x_refs, xo_refs, *sems)
        pl.when(step == 0)(start)
        body(*ins, *o, *scr)
        if forward is not None:
            pl.when(step == (3 * total) // 4)(forward)
        pl.when(step == total - 1)(finish)

    hbm = pl.BlockSpec(memory_space=pl.ANY)
    res = _pcall(wrapped, out_shape=outs + tuple(_exchange_out(kind, t) for t in xs), grid=grid,
                 in_specs=list(in_specs) + [hbm] * n_x, out_specs=ospecs + (hbm,) * n_x,
                 scratch_shapes=list(scratch_shapes) + _exchange_sems(n_x), input_output_aliases=aliases,
                 compiler_params=_params(), name=name)(*args, *xs)
    main = tuple(res[:n_out])
    return (main[0] if single else main), list(res[n_out:])


def _matmul(a, b, *, ta=False, tb=False, epi=None, extra=None, out_dtype=F32, carry=None, name):
    M, K = (a.shape[1], a.shape[0]) if ta else a.shape
    N = b.shape[0] if tb else b.shape[1]
    tm = _tile(M, 2 * MM_BLOCK if ta else MM_BLOCK, SUBLANES if not ta else LANES)
    tn = _tile(N, MM_BLOCK if ta else 2 * MM_BLOCK, LANES)
    tk = _tile(K, MM_K_BLOCK // 2 if ta else MM_K_BLOCK, LANES if not ta else SUBLANES)
    nk = K // tk
    dims = (((0 if ta else 1,), (1 if tb else 0,)), ((), ()))

    def body(a_ref, b_ref, *rest):
        e_ref = rest[0] if extra is not None else None
        o_ref = rest[1] if extra is not None else rest[0]

        def finish(r):
            if epi == "add":
                r = r + e_ref[...]
            elif epi == "relu2":
                r = jnp.square(jnp.maximum(r, 0.0))
            elif epi == "relu2grad":
                r = r * (2.0 * jnp.sqrt(e_ref[...].astype(F32)))
            o_ref[...] = r.astype(out_dtype)

        part = lax.dot_general(a_ref[...].astype(MXU_DTYPE), b_ref[...].astype(MXU_DTYPE), dims,
                               preferred_element_type=F32)
        if nk == 1:
            finish(part)
        else:
            acc_ref = rest[-1]
            k = pl.program_id(2)

            @pl.when(k == 0)
            def _():
                acc_ref[...] = part

            @pl.when(k > 0)
            def _():
                acc_ref[...] += part

            @pl.when(k == nk - 1)
            def _():
                finish(acc_ref[...])

    a_spec = pl.BlockSpec((tk, tm), lambda i, j, k: (k, i)) if ta else pl.BlockSpec((tm, tk), lambda i, j, k: (i, k))
    b_spec = pl.BlockSpec((tn, tk), lambda i, j, k: (j, k)) if tb else pl.BlockSpec((tk, tn), lambda i, j, k: (k, j))
    o_spec = pl.BlockSpec((tm, tn), lambda i, j, k: (i, j))
    in_specs = [a_spec, b_spec] + ([o_spec] if extra is not None else [])
    args = (a, b) + ((extra,) if extra is not None else ())
    return _run(body, args, out_shape=_sds((M, N), out_dtype), grid=(M // tm, N // tn, nk), in_specs=in_specs,
                out_specs=o_spec, scratch_shapes=[pltpu.VMEM((tm, tn), F32)] if nk > 1 else [], name=name, carry=carry)


def _group_mean(v, E):
    rows, W = v.shape
    if E == W:
        return jnp.mean(v, axis=-1, keepdims=True)
    pieces = []
    if E % LANES == 0:
        for g0 in range(0, W, E):
            m = jnp.mean(v[:, g0:g0 + E], axis=-1, keepdims=True)
            pieces.append(jnp.broadcast_to(m, (rows, E)))
    else:
        lane = lax.broadcasted_iota(jnp.int32, (rows, LANES), 1)
        for c0 in range(0, W, LANES):
            vc = v[:, c0:c0 + LANES]
            acc = jnp.zeros((rows, LANES), F32)
            for s0 in range(0, LANES, E):
                msk = (lane >= s0) & (lane < s0 + E)
                m = jnp.sum(jnp.where(msk, vc, 0.0), axis=-1, keepdims=True) * (1.0 / E)
                acc = jnp.where(msk, m, acc)
            pieces.append(acc)
    return jnp.concatenate(pieces, axis=-1)


def _fold_groups(t, E):
    W = t.shape[1]
    step = max(E, LANES)
    acc = t[:, 0:step]
    for c0 in range(step, W, step):
        acc = acc + t[:, c0:c0 + step]
    sh = LANES // 2
    while sh >= E:
        acc = acc + pltpu.roll(acc, sh, 1)
        sh //= 2
    return acc


def _norm_fwd(x, g, *, E, W, cb=0, out_dtype, carry=None, name):
    M = x.shape[0]
    tm = _tile(M, ROW_BLOCK, SUBLANES)

    def body(x_ref, g_ref, o_ref):
        xv = x_ref[...]
        r = lax.rsqrt(_group_mean(xv * xv, E) + EPS)
        o_ref[...] = (xv * r * g_ref[...]).astype(out_dtype)

    return _run(body, (x, g), out_shape=_sds((M, W), out_dtype), grid=(M // tm,),
                in_specs=[pl.BlockSpec((tm, W), lambda i: (i, cb)), pl.BlockSpec((1, W), lambda i: (0, 0))],
                out_specs=pl.BlockSpec((tm, W), lambda i: (i, 0)), name=name, carry=carry)


def _norm_bwd(x, g, dy, *, E, W, cb=0, res=None, fold=None, out_dtype=F32, low=False, into=None, name):
    M = x.shape[0]
    tm = _tile(M, ROW_BLOCK, SUBLANES)
    n = M // tm
    gw = W if fold is None else max(fold, LANES)

    def body(x_ref, g_ref, dy_ref, *rest):
        rest = list(rest)
        r_ref = rest.pop(0) if res is not None else None
        if into is not None:
            rest.pop(0)
        dx_ref = rest.pop(0)
        dxl_ref = rest.pop(0) if low else None
        dg_ref, acc_ref = rest
        i = pl.program_id(0)
        xv = x_ref[...]
        r = lax.rsqrt(_group_mean(xv * xv, E) + EPS)
        xh = xv * r
        dyv = dy_ref[...].astype(F32)
        dyg = dyv * g_ref[...]
        dx = r * (dyg - xh * _group_mean(dyg * xh, E))
        if res is not None:
            dx = dx + r_ref[...]
        dx_ref[...] = dx.astype(dx_ref.dtype)
        if low:
            dxl_ref[...] = dx.astype(MXU_DTYPE)
        part = jnp.sum(dyv * xh, axis=0, keepdims=True)

        @pl.when(i == 0)
        def _():
            acc_ref[...] = part

        @pl.when(i > 0)
        def _():
            acc_ref[...] += part

        @pl.when(i == n - 1)
        def _():
            t = acc_ref[...]
            dg_ref[...] = t if fold is None else _fold_groups(t, fold)

    blk = pl.BlockSpec((tm, W), lambda i: (i, 0))
    in_specs = [pl.BlockSpec((tm, W), lambda i: (i, cb)), pl.BlockSpec((1, W), lambda i: (0, 0)), blk]
    args = [x, g, dy]
    if res is not None:
        in_specs.append(blk)
        args.append(res)
    lows = ((_sds((M, W), MXU_DTYPE),), (blk,)) if low else ((), ())
    dx_out, dx_spec, aliases = _sds((M, W), out_dtype), blk, None
    if into is not None:
        aliases = {len(args): 0}
        args.append(into)
        in_specs.append(pl.BlockSpec(memory_space=pl.ANY))
        dx_out, dx_spec = _sds(into.shape, into.dtype), pl.BlockSpec((tm, W), lambda i: (i, cb))
    return _run(body, tuple(args), out_shape=(dx_out,) + lows[0] + (_sds((1, gw), F32),), grid=(n,),
                in_specs=in_specs, out_specs=(dx_spec,) + lows[1] + (pl.BlockSpec((1, gw), lambda i: (0, 0)),),
                scratch_shapes=[pltpu.VMEM((1, W), F32)], name=name, aliases=aliases)


def _attn_tables(S, tq):
    nq = S // tq

    def body(o_ref):
        i = pl.program_id(0)
        d = (i * tq + lax.broadcasted_iota(jnp.int32, (tq, S), 0)) - lax.broadcasted_iota(jnp.int32, (tq, S), 1)
        cnt = jnp.zeros((tq, S), jnp.int32)
        for dil in DILATIONS:
            hit = (d <= DIL_STEPS * dil) if dil == 1 else (((d & (dil - 1)) == 0) & (d <= DIL_STEPS * dil))
            cnt = cnt + hit.astype(jnp.int32)
        ok = (d >= 0) & (cnt > 0)
        logm = jnp.where(cnt == 3, math.log(3.0), jnp.where(cnt == 2, math.log(2.0), 0.0))
        o_ref[0] = jnp.where(ok, logm, NEG).astype(F32)

    return _run(body, (), out_shape=_sds((nq, tq, S), F32), grid=(nq,), in_specs=[],
                out_specs=pl.BlockSpec((1, tq, S), lambda i: (i, 0, 0)), name="dilated_attn_tables")


def _alibi_slope(h):
    return 2.0 ** (-8.0 / ATT_HEADS * (h + 1))


def _key_positions(Sk):
    kpos = lax.broadcasted_iota(jnp.int32, (Sk, LANES), 0)
    return (kpos >> 8).astype(F32).astype(MXU_DTYPE), (kpos & 255).astype(F32).astype(MXU_DTYPE)


def _score_operands(q2s, k2, pos_hi, pos_lo, lane, sub, h):
    own = (lane < ATT_HEAD_DIM) if sub == 0 else (lane >= ATT_HEAD_DIM)
    spare = ATT_HEAD_DIM if sub == 0 else 0
    slope = _alibi_slope(h)
    terms = jnp.where(lane == spare, slope * 256.0, jnp.where(lane == spare + 1, slope, 0.0)).astype(q2s.dtype)
    qa = jnp.where(own, q2s, terms)
    ka = jnp.where(lane == spare, pos_hi, jnp.where(lane == spare + 1, pos_lo, k2))
    return own, qa, ka


def _key_extents(S, tq):
    ext = min(ATT_KEY_EXTENT, S)
    return ext, ext // tq, S // ext


def _attn_fwd(tab, qn, kn, z, *, B, S, carry=None, name):
    T = B * S
    tq = tab.shape[1]
    nq = S // tq
    ext, per, n_ext = _key_extents(S, tq)
    scale = 1.0 / math.sqrt(ATT_HEAD_DIM)
    dn_t = (((1,), (1,)), ((), ()))

    def body(tab_ref, q_ref, k_ref, v_ref, y_ref, lse_ref, ycat_ref):
        i = pl.program_id(1)
        lane = lax.broadcasted_iota(jnp.int32, (1, LANES), 1)
        lo = lane < ATT_HEAD_DIM

        def visit(Sk):
            base = tab_ref[0, :, :Sk]
            pos_hi, pos_lo = _key_positions(Sk)
            lse_blk = jnp.zeros((tq, LANES), F32)
            for p in range(ATT_HEADS // 2):
                cs = slice(p * LANES, (p + 1) * LANES)
                q2s = q_ref[:, cs].astype(MXU_DTYPE) * scale
                k2 = k_ref[:Sk, cs].astype(MXU_DTYPE)
                v2 = v_ref[:Sk, cs].astype(MXU_DTYPE)
                outs = []
                for sub in range(2):
                    h = 2 * p + sub
                    _, qa, ka = _score_operands(q2s, k2, pos_hi, pos_lo, lane, sub, h)
                    s = lax.dot_general(qa, ka, dn_t, preferred_element_type=F32) + base
                    mx = jnp.max(s, axis=1, keepdims=True)
                    e = jnp.exp(s - mx)
                    l = jnp.sum(e, axis=1, keepdims=True)
                    outs.append(jnp.dot(e.astype(MXU_DTYPE), v2, preferred_element_type=F32) / l)
                    lse_blk = jnp.where(lane == h, mx + jnp.log(l), lse_blk)
                y = jnp.where(lo, outs[0], outs[1])
                y_ref[:, cs] = y
                ycat_ref[:, cs] = y.astype(MXU_DTYPE)
            lse_ref[...] = lse_blk

        for e in range(n_ext):
            pl.when(i // per == e)(functools.partial(visit, (e + 1) * ext))

    return _run(
        body, (tab, qn, kn, z), grid=(B, nq),
        out_shape=(_sds((T, ATT_W), F32), _sds((T, LANES), F32), _sds((T, ATT_W + HG_W), MXU_DTYPE)),
        in_specs=[pl.BlockSpec((1, tq, S), lambda b, i: (i, 0, 0)),
                  pl.BlockSpec((tq, ATT_W), lambda b, i: (b * nq + i, 0)),
                  pl.BlockSpec((S, ATT_W), lambda b, i: (b, 0)),
                  pl.BlockSpec((S, ATT_W), lambda b, i: (b, 2))],
        out_specs=(pl.BlockSpec((tq, ATT_W), lambda b, i: (b * nq + i, 0)),
                   pl.BlockSpec((tq, LANES), lambda b, i: (b * nq + i, 0)),
                   pl.BlockSpec((tq, ATT_W), lambda b, i: (b * nq + i, 0))),
        name=name, carry=carry)


def _attn_bwd(tab, qn, kn, z, y, lse, dycat, dz, *, B, S, carry=None, name):
    T = B * S
    tq = tab.shape[1]
    nq = S // tq
    ext, per, n_ext = _key_extents(S, tq)
    scale = 1.0 / math.sqrt(ATT_HEAD_DIM)
    dn_t = (((1,), (1,)), ((), ()))
    dn_o = (((0,), (0,)), ((), ()))

    def body(tab_ref, q_ref, k_ref, v_ref, y_ref, lse_ref, dy_ref, _, dq_ref, dk_ref, dz_ref, dv_ref):
        i = pl.program_id(1)

        @pl.when(i == 0)
        def _():
            dk_ref[...] = jnp.zeros_like(dk_ref)
            dv_ref[...] = jnp.zeros_like(dv_ref)

        lane = lax.broadcasted_iota(jnp.int32, (1, LANES), 1)
        lo = lane < ATT_HEAD_DIM

        def visit(Sk):
            base = tab_ref[0, :, :Sk]
            pos_hi, pos_lo = _key_positions(Sk)
            lse_blk = lse_ref[...]
            for p in range(ATT_HEADS // 2):
                cs = slice(p * LANES, (p + 1) * LANES)
                q2 = q_ref[:, cs].astype(MXU_DTYPE)
                q2s = q2 * scale
                k2 = k_ref[:Sk, cs].astype(MXU_DTYPE)
                v2 = v_ref[:Sk, cs].astype(MXU_DTYPE)
                do2 = dy_ref[:, cs]
                doy = do2 * y_ref[:, cs]
                do2m = do2.astype(MXU_DTYPE)
                dqs, dks, dvs = [], [], []
                for sub in range(2):
                    h = 2 * p + sub
                    own, qa, ka = _score_operands(q2s, k2, pos_hi, pos_lo, lane, sub, h)
                    s = lax.dot_general(qa, ka, dn_t, preferred_element_type=F32) + base
                    lse_h = jnp.sum(jnp.where(lane == h, lse_blk, 0.0), axis=1, keepdims=True)
                    pr = jnp.exp(s - lse_h)
                    dsum = jnp.sum(jnp.where(own, doy, 0.0), axis=1, keepdims=True)
                    dom = jnp.where(own, do2m, jnp.zeros_like(do2m))
                    dp = lax.dot_general(dom, v2, dn_t, preferred_element_type=F32)
                    ds = (pr * (dp - dsum)).astype(MXU_DTYPE)
                    dqs.append(jnp.dot(ds, k2, preferred_element_type=F32))
                    dks.append(lax.dot_general(ds, q2, dn_o, preferred_element_type=F32))
                    dvs.append(lax.dot_general(pr.astype(MXU_DTYPE), do2m, dn_o, preferred_element_type=F32))
                dq_ref[:, cs] = jnp.where(lo, dqs[0], dqs[1]) * scale
                dk_ref[:Sk, cs] += jnp.where(lo, dks[0], dks[1]) * scale
                dv_ref[:Sk, cs] += jnp.where(lo, dvs[0], dvs[1])

        for e in range(n_ext):
            pl.when(i // per == e)(functools.partial(visit, (e + 1) * ext))

        @pl.when(i == nq - 1)
        def _():
            dz_ref[...] = dv_ref[...].astype(dz_ref.dtype)

    qblk = pl.BlockSpec((tq, ATT_W), lambda b, i: (b * nq + i, 0))
    sblk = pl.BlockSpec((S, ATT_W), lambda b, i: (b, 0))
    vblk = pl.BlockSpec((S, ATT_W), lambda b, i: (b, 2))
    return _run(
        body, (tab, qn, kn, z, y, lse, dycat, dz), grid=(B, nq),
        out_shape=(_sds((T, ATT_W), F32), _sds((T, ATT_W), F32), _sds(dz.shape, dz.dtype)),
        in_specs=[pl.BlockSpec((1, tq, S), lambda b, i: (i, 0, 0)), qblk, sblk, vblk, qblk,
                  pl.BlockSpec((tq, LANES), lambda b, i: (b * nq + i, 0)), qblk, pl.BlockSpec(memory_space=pl.ANY)],
        out_specs=(qblk, sblk, vblk), scratch_shapes=[pltpu.VMEM((S, ATT_W), F32)], name=name, carry=carry,
        aliases={7: 2})


def _sigmoid_pair(x):
    en = jnp.exp(-jnp.abs(x))
    big = 1.0 / (1.0 + en)
    small = en * big
    pos = x >= 0
    return jnp.where(pos, big, small), jnp.where(pos, small, big)


def _chunk_scan(x, pos, reverse):
    n = x.shape[0]
    sh = 1
    while sh < HG_CHUNK:
        if reverse:
            x = x + jnp.where(pos < HG_CHUNK - sh, pltpu.roll(x, n - sh, 0), 0.0)
        else:
            x = x + jnp.where(pos >= sh, pltpu.roll(x, sh, 0), 0.0)
        sh *= 2
    return x


def _hgrn_pair_table(tb):
    def body(o_ref):
        t = lax.broadcasted_iota(jnp.int32, (tb, tb), 0)
        s = lax.broadcasted_iota(jnp.int32, (tb, tb), 1)
        o_ref[...] = jnp.where((t // HG_CHUNK) == (s // HG_CHUNK), (t // HG_SUB) - (s // HG_SUB), 0).astype(F32)

    return _pcall(body, out_shape=_sds((tb, tb), F32), name="hgrn_pair_table")()


def _hgrn_cross_decays(b, pos):
    n = b.shape[0]
    pos8 = pos % HG_SUB
    end = b
    sh = 1
    while sh < HG_SUB:
        end = jnp.where((pos8 & sh) == 0, pltpu.roll(end, n - sh, 0), end)
        sh *= 2
    x = jnp.where(pos >= HG_SUB, jnp.exp(jnp.minimum(b - pltpu.roll(end, HG_SUB, 0), 0.0)), 0.0)
    ys = []
    for j in range(1, HG_CHUNK // HG_SUB):
        end_j = end if j == 1 else pltpu.roll(end, n - HG_SUB * (j - 1), 0)
        ys.append(jnp.where(pos < HG_CHUNK - HG_SUB * j, jnp.exp(jnp.minimum(end_j - b, 0.0)), 0.0))
    return x, ys


def _hgrn_cross_scores(qt, kts, pair):
    n = qt.shape[0]
    kcat = jnp.concatenate([k.astype(MXU_DTYPE) for k in kts], axis=0)
    p = lax.dot_general(qt.astype(MXU_DTYPE), kcat, (((1,), (1,)), ((), ())), preferred_element_type=F32)
    a = jnp.zeros((n, n), F32)
    for j in range(1, len(kts) + 1):
        a = jnp.where(pair == j, p[:, (j - 1) * n:j * n], a)
    return a, kcat


def _hgrn_gates(qh, fh, lb):
    sq, _ = _sigmoid_pair(qh)
    sg, sgn = _sigmoid_pair(fh)
    f = lb + (1.0 - lb) * sg
    g = jnp.log(jnp.maximum(f, F_MIN))
    kk = (1.0 - lb) * sgn
    return qh * sq, sq, sg, sgn, f, g, kk


def _hgrn_fwd(z, lb, pair, *, B, S, name):
    T = B * S
    tb = _tile(S, HG_BLOCK, HG_CHUNK)
    nc = tb // HG_CHUNK
    nb = S // tb
    dn_t = (((1,), (1,)), ((), ()))
    dn_o = (((0,), (0,)), ((), ()))

    def zcol(first):
        return lambda h, b, i: (b * nb + i, first + h)

    def body(q_ref, f_ref, v_ref, lb_ref, pair_ref, o_ref, st_ref, state):
        i = pl.program_id(2)

        @pl.when(i == 0)
        def _():
            state[...] = jnp.zeros_like(state)

        q, _, _, _, _, g, kk = _hgrn_gates(q_ref[...], f_ref[...], lb_ref[...])
        v = v_ref[...]
        pos = lax.broadcasted_iota(jnp.int32, (tb, LANES), 0) % HG_CHUNK
        pos8 = pos % HG_SUB
        b = _chunk_scan(g, pos, False)
        o = jnp.sum(q * kk, axis=1, keepdims=True) * v
        for d in range(1, HG_SUB):
            qd = pltpu.roll(q, tb - d, 0)
            bd = pltpu.roll(b, tb - d, 0)
            e = jnp.where(pos8 < HG_SUB - d, jnp.exp(jnp.minimum(bd - b, 0.0)), 0.0)
            a = jnp.sum(qd * kk * e, axis=1, keepdims=True)
            o = o + pltpu.roll(a * v, d, 0)
        x, ys = _hgrn_cross_decays(b, pos)
        a_cross, _ = _hgrn_cross_scores(q * x, [kk * y for y in ys], pair_ref[...])
        o = o + jnp.dot(a_cross.astype(MXU_DTYPE), v.astype(MXU_DTYPE), preferred_element_type=F32)
        qe = q * jnp.exp(b)
        chunks = [slice(c * HG_CHUNK, (c + 1) * HG_CHUNK) for c in range(nc)]
        b_last = [jnp.sum(jnp.where(pos[rs] == HG_CHUNK - 1, b[rs], 0.0), axis=0, keepdims=True) for rs in chunks]
        own = [lax.dot_general(v[rs].astype(MXU_DTYPE), (kk[rs] * jnp.exp(bl - b[rs])).astype(MXU_DTYPE), dn_o,
                               preferred_element_type=F32) for rs, bl in zip(chunks, b_last)]
        st = state[...]
        before = []
        for c in range(nc):
            before.append(st)
            st = st * jnp.exp(b_last[c]) + own[c]
        state[...] = st
        for c, rs in enumerate(chunks):
            st_ref[c * LANES:(c + 1) * LANES, :] = before[c]
            o_ref[rs, :] = o[rs] + lax.dot_general(qe[rs].astype(MXU_DTYPE), before[c].astype(MXU_DTYPE), dn_t,
                                                   preferred_element_type=F32)

    blk = (tb, LANES)
    return _run(
        body, (z, z, z, lb, pair), out_shape=(_sds((T, HG_W), F32), _sds((T // HG_CHUNK * LANES, HG_W), F32)),
        grid=(HG_HEADS, B, nb),
        in_specs=[pl.BlockSpec(blk, zcol(12)), pl.BlockSpec(blk, zcol(16)), pl.BlockSpec(blk, zcol(20)),
                  pl.BlockSpec((1, LANES), lambda h, b, i: (0, h)), pl.BlockSpec((tb, tb), lambda h, b, i: (0, 0))],
        out_specs=(pl.BlockSpec(blk, lambda h, b, i: (b * nb + i, h)),
                   pl.BlockSpec((nc * LANES, LANES), lambda h, b, i: (b * nb + i, h))),
        scratch_shapes=[pltpu.VMEM((LANES, LANES), F32)], name=name)


def _hgrn_bwd(z, lb, pair, states, do, *, B, S, name):
    T = B * S
    tb = _tile(S, HG_BLOCK, HG_CHUNK)
    nc = tb // HG_CHUNK
    nb = S // tb
    dn_t = (((1,), (1,)), ((), ()))
    dn_o = (((0,), (0,)), ((), ()))

    def body(q_ref, f_ref, v_ref, lb_ref, pair_ref, st_ref, do_ref, dq_ref, df_ref, di_ref, dlb_ref, dstate):
        bi = pl.program_id(1)
        i = pl.program_id(2)

        @pl.when(i == 0)
        def _():
            dstate[...] = jnp.zeros_like(dstate)

        qh = q_ref[...]
        lbv = lb_ref[...]
        q, sq, sg, sgn, f, g, kk = _hgrn_gates(qh, f_ref[...], lbv)
        v = v_ref[...]
        dov = do_ref[...]
        pos = lax.broadcasted_iota(jnp.int32, (tb, LANES), 0) % HG_CHUNK
        pos8 = pos % HG_SUB
        b = _chunk_scan(g, pos, False)

        a0 = jnp.sum(q * kk, axis=1, keepdims=True)
        da0 = jnp.sum(dov * v, axis=1, keepdims=True)
        dv = a0 * dov
        dkk = da0 * q
        dq = da0 * kk
        db = jnp.zeros((tb, LANES), F32)
        for d in range(1, HG_SUB):
            qd = pltpu.roll(q, tb - d, 0)
            bd = pltpu.roll(b, tb - d, 0)
            dod = pltpu.roll(dov, tb - d, 0)
            e = jnp.where(pos8 < HG_SUB - d, jnp.exp(jnp.minimum(bd - b, 0.0)), 0.0)
            a = jnp.sum(qd * kk * e, axis=1, keepdims=True)
            da = jnp.sum(dod * v, axis=1, keepdims=True)
            dv = dv + a * dod
            xe = da * e
            dkk = dkk + xe * qd
            t1 = xe * kk
            wt = t1 * qd
            dq = dq + pltpu.roll(t1, d, 0)
            db = db + pltpu.roll(wt, d, 0) - wt

        n_cross = HG_CHUNK // HG_SUB - 1
        pair = pair_ref[...]
        x, ys = _hgrn_cross_decays(b, pos)
        qt = q * x
        kts = [kk * y for y in ys]
        a_cross, kcat = _hgrn_cross_scores(qt, kts, pair)
        dom = dov.astype(MXU_DTYPE)
        da_all = lax.dot_general(dom, v.astype(MXU_DTYPE), dn_t, preferred_element_type=F32)
        dv = dv + lax.dot_general(a_cross.astype(MXU_DTYPE), dom, dn_o, preferred_element_type=F32)
        gcat = jnp.concatenate([jnp.where(pair == j, da_all, 0.0).astype(MXU_DTYPE) for j in range(1, n_cross + 1)],
                               axis=1)
        dqt = jnp.dot(gcat, kcat, preferred_element_type=F32)
        dkcat = lax.dot_general(gcat, qt.astype(MXU_DTYPE), dn_o, preferred_element_type=F32)
        dq = dq + dqt * x
        u = dqt * qt
        db = db + u
        d_end = -pltpu.roll(u, tb - HG_SUB, 0)
        for j in range(1, n_cross + 1):
            dkt = dkcat[(j - 1) * tb:j * tb]
            dkk = dkk + dkt * ys[j - 1]
            w = dkt * kts[j - 1]
            db = db - w
            d_end = d_end + (w if j == 1 else pltpu.roll(w, HG_SUB * (j - 1), 0))
        sh = 1
        while sh < HG_SUB:
            d_end = d_end + jnp.where(pos8 >= sh, pltpu.roll(d_end, sh, 0), 0.0)
            sh *= 2
        db = db + jnp.where(pos8 == HG_SUB - 1, d_end, 0.0)

        eb = jnp.exp(b)
        qe = q * eb
        chunks = [slice(c * HG_CHUNK, (c + 1) * HG_CHUNK) for c in range(nc)]
        b_lasts = [jnp.sum(jnp.where(pos[rs] == HG_CHUNK - 1, b[rs], 0.0), axis=0, keepdims=True) for rs in chunks]
        own = [lax.dot_general(dom[rs], qe[rs].astype(MXU_DTYPE), dn_o, preferred_element_type=F32) for rs in chunks]
        dst = dstate[...]
        after = [None] * nc
        for c in range(nc - 1, -1, -1):
            after[c] = dst
            dst = dst * jnp.exp(b_lasts[c]) + own[c]
        dstate[...] = dst
        dq_c, dkk_c, dv_c, db_c = [None] * nc, [None] * nc, [None] * nc, [None] * nc
        for c, rs in enumerate(chunks):
            st0 = st_ref[c * LANES:(c + 1) * LANES, :]
            eb_last = jnp.exp(b_lasts[c])
            er = jnp.exp(b_lasts[c] - b[rs])
            ke = kk[rs] * er
            dstm = after[c].astype(MXU_DTYPE)
            dqe = jnp.dot(dom[rs], st0.astype(MXU_DTYPE), preferred_element_type=F32)
            dv_c[c] = dv[rs] + lax.dot_general(ke.astype(MXU_DTYPE), dstm, dn_t, preferred_element_type=F32)
            dke = jnp.dot(v[rs].astype(MXU_DTYPE), dstm, preferred_element_type=F32)
            d_eb_last = jnp.sum(after[c] * st0, axis=0, keepdims=True)
            dq_c[c] = dq[rs] + dqe * eb[rs]
            dkk_c[c] = dkk[rs] + dke * er
            dkeke = dke * ke
            db_last = jnp.sum(dkeke, axis=0, keepdims=True) + d_eb_last * eb_last
            db_c[c] = db[rs] + dqe * qe[rs] - dkeke + jnp.where(pos[rs] == HG_CHUNK - 1, db_last, 0.0)
        dq = jnp.concatenate(dq_c, axis=0)
        dkk = jnp.concatenate(dkk_c, axis=0)
        dv = jnp.concatenate(dv_c, axis=0)
        dg = _chunk_scan(jnp.concatenate(db_c, axis=0), pos, True)

        dfv = dg * jnp.where(f > F_MIN, 1.0 / f, 0.0)
        dsg = (dfv - dkk) * (1.0 - lbv)
        dq_ref[...] = (dq * (sq * (1.0 + qh * (1.0 - sq)))).astype(dq_ref.dtype)
        df_ref[...] = (dsg * sg * sgn).astype(df_ref.dtype)
        di_ref[...] = dv.astype(di_ref.dtype)
        part = jnp.sum(dfv * sgn - dkk * sgn, axis=0, keepdims=True)

        @pl.when((bi == 0) & (i == 0))
        def _():
            dlb_ref[...] = part

        @pl.when((bi > 0) | (i > 0))
        def _():
            dlb_ref[...] += part

    def rev(first):
        return lambda h, b, i: (b * nb + (nb - 1 - i), first + h)

    blk = (tb, LANES)
    oblk = pl.BlockSpec(blk, rev(0))
    return _run(
        body, (z, z, z, lb, pair, states, do), out_shape=(_sds((T, HG_W), MXU_DTYPE),) * 3 + (_sds((1, HG_W), F32),),
        grid=(HG_HEADS, B, nb),
        in_specs=[pl.BlockSpec(blk, rev(12)), pl.BlockSpec(blk, rev(16)), pl.BlockSpec(blk, rev(20)),
                  pl.BlockSpec((1, LANES), lambda h, b, i: (0, h)), pl.BlockSpec((tb, tb), lambda h, b, i: (0, 0)),
                  pl.BlockSpec((nc * LANES, LANES), rev(0)), oblk],
        out_specs=(oblk, oblk, oblk, pl.BlockSpec((1, LANES), lambda h, b, i: (0, h))),
        scratch_shapes=[pltpu.VMEM((LANES, LANES), F32)], name=name)


def _hgout_fwd(o, z, g, ycat, *, name):
    T = o.shape[0]
    tm = _tile(T, ROW_BLOCK, SUBLANES)

    def body(o_ref, gh_ref, g_ref, _, y_ref):
        ov = o_ref[...]
        gh = gh_ref[...]
        r = lax.rsqrt(_group_mean(ov * ov, HG_HEAD_DIM) + EPS)
        sg, _ = _sigmoid_pair(gh)
        y_ref[...] = (ov * r * g_ref[...] * (gh * sg)).astype(y_ref.dtype)

    blk = pl.BlockSpec((tm, HG_W), lambda i: (i, 0))
    return _run(body, (o, z, g, ycat), out_shape=_sds(ycat.shape, ycat.dtype), grid=(T // tm,),
                in_specs=[blk, pl.BlockSpec((tm, HG_W), lambda i: (i, 6)), pl.BlockSpec((1, HG_W), lambda i: (0, 0)),
                          pl.BlockSpec(memory_space=pl.ANY)],
                out_specs=pl.BlockSpec((tm, HG_W), lambda i: (i, ATT_W // HG_W)), name=name, aliases={3: 0})


def _hgout_bwd(o, z, g, dycat, *, name):
    T = o.shape[0]
    tm = _tile(T, ROW_BLOCK, SUBLANES)

    def body(o_ref, gh_ref, g_ref, dy_ref, do_ref, dgh_ref, dg_ref):
        i = pl.program_id(0)
        ov = o_ref[...]
        gh = gh_ref[...]
        gv = g_ref[...]
        dy = dy_ref[...]
        r = lax.rsqrt(_group_mean(ov * ov, HG_HEAD_DIM) + EPS)
        xh = ov * r
        sg, _ = _sigmoid_pair(gh)
        dn = dy * (gh * sg)
        dgh_ref[...] = (dy * xh * gv * (sg * (1.0 + gh * (1.0 - sg)))).astype(dgh_ref.dtype)
        dng = dn * gv
        do_ref[...] = r * (dng - xh * _group_mean(dng * xh, HG_HEAD_DIM))
        part = jnp.sum(dn * xh, axis=0, keepdims=True)

        @pl.when(i == 0)
        def _():
            dg_ref[...] = part

        @pl.when(i > 0)
        def _():
            dg_ref[...] += part

    blk = pl.BlockSpec((tm, HG_W), lambda i: (i, 0))
    row = pl.BlockSpec((1, HG_W), lambda i: (0, 0))
    return _run(body, (o, z, g, dycat),
                out_shape=(_sds((T, HG_W), F32), _sds(z.shape, MXU_DTYPE), _sds((1, HG_W), F32)), grid=(T // tm,),
                in_specs=[blk, pl.BlockSpec((tm, HG_W), lambda i: (i, 6)), row,
                          pl.BlockSpec((tm, HG_W), lambda i: (i, 1))],
                out_specs=(blk, pl.BlockSpec((tm, HG_W), lambda i: (i, 6)), row), name=name)


def _lower_bounds_fwd(hg_lb):
    L = hg_lb.shape[0]

    def body(x_ref, o_ref):
        rows = [x_ref[l:l + 1, :] for l in range(L)]
        mx = functools.reduce(jnp.maximum, rows)
        es = [jnp.exp(r - mx) for r in rows]
        tot = functools.reduce(jnp.add, es)
        acc = jnp.zeros_like(tot)
        for l in range(L):
            acc = acc + es[l] / tot
            o_ref[l:l + 1, :] = acc - es[0] / tot

    return _pcall(body, out_shape=_sds(hg_lb.shape, F32), name="lower_bounds_fwd")(hg_lb)


def _lower_bounds_bwd(hg_lb, dlower):
    L = hg_lb.shape[0]

    def body(x_ref, d_ref, o_ref):
        rows = [x_ref[l:l + 1, :] for l in range(L)]
        ds = [d_ref[l:l + 1, :] for l in range(L)]
        mx = functools.reduce(jnp.maximum, rows)
        es = [jnp.exp(r - mx) for r in rows]
        tot = functools.reduce(jnp.add, es)
        ps = [e / tot for e in es]
        dps = []
        for j in range(L):
            t = functools.reduce(jnp.add, ds[j:])
            if j == 0:
                t = t - functools.reduce(jnp.add, ds)
            dps.append(t)
        inner = functools.reduce(jnp.add, [p * dp for p, dp in zip(ps, dps)])
        for j in range(L):
            o_ref[j:j + 1, :] = ps[j] * (dps[j] - inner)

    return _pcall(body, out_shape=_sds(hg_lb.shape, F32), name="lower_bounds_bwd")(hg_lb, dlower)


def _memattn_fwd(qm, km, kv, *, B, S, NM, name):
    T, D = qm.shape
    E = D // MEM_HEADS
    tq = _tile(S, ROW_BLOCK, SUBLANES)
    nq = S // tq
    scale = 1.0 / math.sqrt(E)

    def body(q_ref, k_ref, v_ref, o_ref):
        for h in range(MEM_HEADS):
            cs = slice(h * E, (h + 1) * E)
            s = lax.dot_general(q_ref[:, cs].astype(MXU_DTYPE), k_ref[:, cs].astype(MXU_DTYPE),
                                (((1,), (1,)), ((), ())), preferred_element_type=F32) * scale
            e = jnp.exp(s - jnp.max(s, axis=1, keepdims=True))
            pr = e / jnp.sum(e, axis=1, keepdims=True)
            o_ref[:, cs] = jnp.dot(pr.astype(MXU_DTYPE), v_ref[:, cs].astype(MXU_DTYPE),
                                   preferred_element_type=F32).astype(o_ref.dtype)

    return _run(
        body, (qm, km, kv), out_shape=_sds((T, D), MXU_DTYPE), grid=(B, nq),
        in_specs=[pl.BlockSpec((tq, D), lambda b, i: (b * nq + i, 0)), pl.BlockSpec((NM, D), lambda b, i: (b, 0)),
                  pl.BlockSpec((NM, D), lambda b, i: (b, 1))],
        out_specs=pl.BlockSpec((tq, D), lambda b, i: (b * nq + i, 0)), name=name)


def _memattn_bwd(qm, km, kv, do, *, B, S, NM, name):
    T, D = qm.shape
    E = D // MEM_HEADS
    tq = _tile(S, ROW_BLOCK, SUBLANES)
    nq = S // tq
    scale = 1.0 / math.sqrt(E)

    def body(q_ref, k_ref, v_ref, do_ref, dq_ref, dk_ref, dv_ref):
        i = pl.program_id(1)

        @pl.when(i == 0)
        def _():
            dk_ref[...] = jnp.zeros_like(dk_ref)
            dv_ref[...] = jnp.zeros_like(dv_ref)

        for h in range(MEM_HEADS):
            cs = slice(h * E, (h + 1) * E)
            qh = q_ref[:, cs].astype(MXU_DTYPE)
            kh = k_ref[:, cs].astype(MXU_DTYPE)
            vh = v_ref[:, cs].astype(MXU_DTYPE)
            doh = do_ref[:, cs].astype(MXU_DTYPE)
            s = lax.dot_general(qh, kh, (((1,), (1,)), ((), ())), preferred_element_type=F32) * scale
            e = jnp.exp(s - jnp.max(s, axis=1, keepdims=True))
            pr = e / jnp.sum(e, axis=1, keepdims=True)
            dp = lax.dot_general(doh, vh, (((1,), (1,)), ((), ())), preferred_element_type=F32)
            ds = (pr * (dp - jnp.sum(dp * pr, axis=1, keepdims=True))).astype(MXU_DTYPE)
            dq_ref[:, cs] = jnp.dot(ds, kh, preferred_element_type=F32) * scale
            dk_ref[:, cs] += lax.dot_general(ds, qh, (((0,), (0,)), ((), ())), preferred_element_type=F32) * scale
            dv_ref[:, cs] += lax.dot_general(pr.astype(MXU_DTYPE), doh, (((0,), (0,)), ((), ())),
                                             preferred_element_type=F32)

    qblk = pl.BlockSpec((tq, D), lambda b, i: (b * nq + i, 0))
    mblk = pl.BlockSpec((NM, D), lambda b, i: (b, 0))
    return _run(
        body, (qm, km, kv, do), out_shape=(_sds((T, D), F32), _sds((B * NM, D), F32), _sds((B * NM, D), F32)),
        grid=(B, nq), in_specs=[qblk, mblk, pl.BlockSpec((NM, D), lambda b, i: (b, 1)), qblk],
        out_specs=(qblk, mblk, mblk), name=name)


def _loss_head(y, tgt):
    T, D = y.shape
    tm = _tile(T, ROW_BLOCK, SUBLANES)

    def body(y_ref, t_ref, dy_ref, dyl_ref, l_ref):
        i = pl.program_id(0)
        err = y_ref[...] - t_ref[...]
        dy_ref[...] = err * (1.0 / D)
        dyl_ref[...] = (err * (1.0 / D)).astype(MXU_DTYPE)
        part = jnp.sum(jnp.sum(err * err, axis=1, keepdims=True), axis=0, keepdims=True) * (0.5 / D)

        @pl.when(i == 0)
        def _():
            l_ref[...] = part

        @pl.when(i > 0)
        def _():
            l_ref[...] += part

    blk = pl.BlockSpec((tm, D), lambda i: (i, 0))
    return _run(body, (y, tgt), out_shape=(_sds((T, D), F32), _sds((T, D), MXU_DTYPE), _sds((1, 1), F32)),
                grid=(T // tm,), in_specs=[blk, blk], out_specs=(blk, blk, pl.BlockSpec((1, 1), lambda i: (0, 0))),
                name="loss_head")


def _adamw(w, g, m, v):
    m = ADAM_B1 * m + (1.0 - ADAM_B1) * g
    v = ADAM_B2 * v + (1.0 - ADAM_B2) * jnp.square(g)
    m_hat = m / (1.0 - ADAM_B1 ** ADAM_STEP)
    v_hat = v / (1.0 - ADAM_B2 ** ADAM_STEP)
    delta = -ADAM_LR * (m_hat / (jnp.sqrt(v_hat) + ADAM_EPS) + ADAM_WD * w)
    return delta, m, v


def _sum_adamw(recv, w, m, v, layer, so_far=None):
    L, r, c = w.shape
    tr = _tile(r, ADAM_BLOCK, SUBLANES)

    def body(r_ref, w_ref, m_ref, v_ref, *rest):
        g_ref, d_ref, nm_ref, nv_ref = rest[-4:]
        g = r_ref[0].astype(F32)
        for j in range(1, N_DEV):
            g = g + r_ref[j].astype(F32)
        g_ref[...] = g
        d_ref[...], nm_ref[...], nv_ref[...] = _adamw(w_ref[...], g, m_ref[...], v_ref[...])

    blk = pl.BlockSpec((None, tr, c), lambda i: (layer, i, 0))
    hbm = pl.BlockSpec(memory_space=pl.ANY)
    args, in_specs, aliases = (recv, w, m, v), [pl.BlockSpec((N_DEV, tr, c), lambda i: (0, i, 0)), blk, blk, blk], {}
    if so_far is not None:
        args, in_specs, aliases = args + tuple(so_far), in_specs + [hbm] * 4, {4 + k: k for k in range(4)}
    return _run(body, args, out_shape=(_sds((L, r, c), F32),) * 4, grid=(r // tr,), in_specs=in_specs,
                out_specs=(blk,) * 4, name="grad_sum_adamw", aliases=aliases)


def _small_sum_adamw(gp, w, m, v):
    NR, C = gp.shape
    MESH = pl.DeviceIdType.MESH

    def body(gp_ref, w_ref, m_ref, v_ref, g_ref, d_ref, nm_ref, nv_ref, gath, send_sems, recv_sems):
        x, y, c = _mesh_pos()
        me = 4 * x + 2 * y + c
        gath[me] = gp_ref[...]
        copies = []
        for k in range(1, N_DEV):
            peer = (_flip(x, k & 4), _flip(y, k & 2), _flip(c, k & 1))
            cp = pltpu.make_async_remote_copy(src_ref=gp_ref, dst_ref=gath.at[me], send_sem=send_sems.at[k - 1],
                                              recv_sem=recv_sems.at[k - 1], device_id=peer, device_id_type=MESH)
            cp.start()
            copies.append(cp)
        for cp in copies:
            cp.wait()
        g = gath[0]
        for j in range(1, N_DEV):
            g = g + gath[j]
        g_ref[...] = g
        d_ref[...], nm_ref[...], nv_ref[...] = _adamw(w_ref[...], g, m_ref[...], v_ref[...])

    vm = pl.BlockSpec(memory_space=pltpu.VMEM)
    return _pcall(body, out_shape=(_sds((NR, C), F32),) * 4, in_specs=[vm] * 4, out_specs=(vm,) * 4,
                  scratch_shapes=[pltpu.VMEM((N_DEV, NR, C), F32), pltpu.SemaphoreType.DMA((7,)),
                                  pltpu.SemaphoreType.DMA((7,))], name="small_sum_adamw")(gp, w, m, v)


def _gathered(gath, name):
    _, r, c = gath.shape
    return gath.transpose(1, 0, 2).reshape(r, N_DEV * c) if name in COL_SHARDED else gath.reshape(N_DEV * r, c)


def _scatter_blocks(grad, name):
    rows, cols = grad.shape
    if name in COL_SHARDED:
        return grad.reshape(rows, N_DEV, cols // N_DEV).transpose(1, 0, 2)
    return grad.reshape(N_DEV, rows // N_DEV, cols)


def _pack_small(d, prefix, loss=None):
    parts = []
    for n in SMALL:
        a = d[prefix + n].reshape(-1).astype(F32)
        pad = (-a.shape[0]) % LANES
        parts.append(jnp.pad(a, (0, pad)) if pad else a)
    tail = jnp.zeros((LANES,), F32)
    if loss is not None:
        tail = tail.at[0].set(loss)
    flat = jnp.concatenate(parts + [tail])
    pad = (-flat.shape[0]) % (8 * LANES)
    return jnp.pad(flat, (0, pad)).reshape(-1, LANES)


def _unpack_small(packed, shapes):
    flat = packed.reshape(-1)
    out, off = {}, 0
    for n in SMALL:
        size = math.prod(shapes[n])
        out[n] = flat[off:off + size].reshape(shapes[n])
        off += size + (-size) % LANES
    return out, flat[off]


def _row(a):
    return a.reshape(1, -1).astype(F32)


def _layer_fwd(x, memf, w_in, shards, next_in, sp, lb, tab, dims):
    B, S, NM = dims
    D = x.shape[1]
    if w_in is None:
        h1, gath = _norm_fwd(x, sp["norm1_g"], E=D, W=D, out_dtype=MXU_DTYPE, name="norm1_fwd",
                             carry=("gather", [shards["w_in"].astype(MXU_DTYPE)]))
        w_in = _gathered(gath[0], "w_in")
    else:
        h1 = _norm_fwd(x, sp["norm1_g"], E=D, W=D, out_dtype=MXU_DTYPE, name="norm1_fwd")
    W = {"w_in": w_in}
    z, gath = _matmul(h1, w_in, carry=("gather", [shards[n].astype(MXU_DTYPE) for n in GATHER_MIX]), name="in_proj")
    W.update({n: _gathered(t, n) for n, t in zip(GATHER_MIX, gath)})
    qn = _norm_fwd(z, sp["attn_qn_g"], E=ATT_HEAD_DIM, W=ATT_W, cb=0, out_dtype=MXU_DTYPE, name="qnorm_fwd")
    kn = _norm_fwd(z, sp["attn_kn_g"], E=ATT_HEAD_DIM, W=ATT_W, cb=1, out_dtype=MXU_DTYPE, name="knorm_fwd")
    (ya, lse, ycat), gath = _attn_fwd(tab["attn"], qn, kn, z, B=B, S=S, name="dilated_attn_fwd",
                                carry=("gather", [shards[n].astype(MXU_DTYPE) for n in GATHER_FF]))
    W.update({n: _gathered(t, n) for n, t in zip(GATHER_FF, gath)})
    o, states = _hgrn_fwd(z, lb, tab["hgrn"], B=B, S=S, name="hgrn_fwd")
    ycat = _hgout_fwd(o, z, sp["hg_onorm_g"], ycat, name="hgrn_out_fwd")
    x1 = _matmul(ycat, W["w_out"], epi="add", extra=x, name="out_proj")
    h2 = _norm_fwd(x1, sp["norm2_g"], E=D, W=D, out_dtype=MXU_DTYPE, name="norm2_fwd")
    qmp = _matmul(h2, W["w_mq"], name="mq_proj")
    qm = _norm_fwd(qmp, sp["mq_norm_g"], E=D // MEM_HEADS, W=D, out_dtype=MXU_DTYPE, name="mqnorm_fwd")
    mn = _norm_fwd(memf, sp["mem_norm_g"], E=D, W=D, out_dtype=MXU_DTYPE, name="memnorm_fwd")
    kv = _matmul(mn, W["w_mkv"], name="mkv_proj")
    km = _norm_fwd(kv, sp["mk_norm_g"], E=D // MEM_HEADS, W=D, cb=0, out_dtype=MXU_DTYPE, name="mknorm_fwd")
    om = _memattn_fwd(qm, km, kv, B=B, S=S, NM=NM, name="mem_attn_fwd")
    x2 = _matmul(om, W["w_mo"], epi="add", extra=x1, name="mo_proj")
    h3 = _norm_fwd(x2, sp["norm3_g"], E=D, W=D, out_dtype=MXU_DTYPE, name="norm3_fwd")
    if next_in is None:
        u, w_in_next = _matmul(h3, W["w_ff1"], epi="relu2", out_dtype=MXU_DTYPE, name="ff1"), None
    else:
        u, gath = _matmul(h3, W["w_ff1"], epi="relu2", out_dtype=MXU_DTYPE,
                          carry=("gather", [next_in.astype(MXU_DTYPE)]), name="ff1")
        w_in_next = _gathered(gath[0], "w_in")
    x3 = _matmul(u, W["w_ff2"], epi="add", extra=x2, name="ff2")
    saved = dict(x=x, h1=h1, z=z, qn=qn, kn=kn, ya=ya, lse=lse, o=o, states=states, ycat=ycat, x1=x1, h2=h2,
                 qmp=qmp, qm=qm, mn=mn, kv=kv, km=km, om=om, x2=x2, h3=h3, u=u)
    return x3, saved, W, w_in_next


def _layer_bwd(dx3, dx3_low, s, memf, W, sp, lb, tab, dims):
    B, S, NM = dims
    D = dx3.shape[1]
    E_M = D // MEM_HEADS
    gw, gs, recv = {}, {}, {}

    def blocks(names):
        return ("scatter", [_scatter_blocks(gw[n], n) for n in names])

    def received(names, got):
        recv.update(dict(zip(names, got)))

    da = _matmul(dx3_low, W["w_ff2"], tb=True, epi="relu2grad", extra=s["u"], out_dtype=MXU_DTYPE, name="ff2_dgrad")
    gw["w_ff2"] = _matmul(s["u"], dx3_low, ta=True, out_dtype=MXU_DTYPE, name="ff2_wgrad")
    dh3, got = _matmul(da, W["w_ff1"], tb=True, out_dtype=MXU_DTYPE, carry=blocks(SCATTER_A), name="ff1_dgrad")
    received(SCATTER_A, got)
    gw["w_ff1"] = _matmul(s["h3"], da, ta=True, out_dtype=MXU_DTYPE, name="ff1_wgrad")
    dx2, dx2_low, gs["norm3_g"] = _norm_bwd(s["x2"], sp["norm3_g"], dh3, E=D, W=D, res=dx3, low=True, name="norm3_bwd")
    dom = _matmul(dx2_low, W["w_mo"], tb=True, out_dtype=MXU_DTYPE, name="mo_dgrad")
    gw["w_mo"] = _matmul(s["om"], dx2_low, ta=True, out_dtype=MXU_DTYPE, name="mo_wgrad")
    dqm, dkm, dvm = _memattn_bwd(s["qm"], s["km"], s["kv"], dom, B=B, S=S, NM=NM, name="mem_attn_bwd")
    dqmp, gs["mq_norm_g"] = _norm_bwd(s["qmp"], sp["mq_norm_g"], dqm, E=E_M, W=D, fold=E_M, out_dtype=MXU_DTYPE,
                                      name="mqnorm_bwd")
    dkmp, gs["mk_norm_g"] = _norm_bwd(s["kv"], sp["mk_norm_g"], dkm, E=E_M, W=D, cb=0, fold=E_M,
                                      out_dtype=MXU_DTYPE, name="mknorm_bwd")
    dkv = jnp.concatenate([dkmp, dvm.astype(MXU_DTYPE)], axis=1)
    dh2 = _matmul(dqmp, W["w_mq"], tb=True, out_dtype=MXU_DTYPE, name="mq_dgrad")
    gw["w_mq"] = _matmul(s["h2"], dqmp, ta=True, out_dtype=MXU_DTYPE, name="mq_wgrad")
    dmn = _matmul(dkv, W["w_mkv"], tb=True, out_dtype=MXU_DTYPE, name="mkv_dgrad")
    gw["w_mkv"] = _matmul(s["mn"], dkv, ta=True, out_dtype=MXU_DTYPE, name="mkv_wgrad")
    _, gs["mem_norm_g"] = _norm_bwd(memf, sp["mem_norm_g"], dmn, E=D, W=D, name="memnorm_bwd")
    dx1, dx1_low, gs["norm2_g"] = _norm_bwd(s["x1"], sp["norm2_g"], dh2, E=D, W=D, res=dx2, low=True, name="norm2_bwd")
    dycat = _matmul(dx1_low, W["w_out"], tb=True, out_dtype=MXU_DTYPE, name="out_dgrad")
    gw["w_out"] = _matmul(s["ycat"], dx1_low, ta=True, out_dtype=MXU_DTYPE, name="out_wgrad")
    do_hg, dz, gs["hg_onorm_g"] = _hgout_bwd(s["o"], s["z"], sp["hg_onorm_g"], dycat, name="hgrn_out_bwd")
    dqh, dfh, dih, dlb = _hgrn_bwd(s["z"], lb, tab["hgrn"], s["states"], do_hg, B=B, S=S, name="hgrn_bwd")
    (dqn, dkn, dz), got = _attn_bwd(tab["attn"], s["qn"], s["kn"], s["z"], s["ya"], s["lse"], dycat, dz, B=B, S=S,
                                     carry=blocks(SCATTER_B), name="dilated_attn_bwd")
    received(SCATTER_B, got)
    dz, gs["attn_qn_g"] = _norm_bwd(s["z"], sp["attn_qn_g"], dqn, E=ATT_HEAD_DIM, W=ATT_W, cb=0, fold=ATT_HEAD_DIM,
                                    into=dz, name="qnorm_bwd")
    dz, gs["attn_kn_g"] = _norm_bwd(s["z"], sp["attn_kn_g"], dkn, E=ATT_HEAD_DIM, W=ATT_W, cb=1, fold=ATT_HEAD_DIM,
                                    into=dz, name="knorm_bwd")
    for block, t in ((3, dqh), (4, dfh), (5, dih)):
        dz = lax.dynamic_update_slice(dz, t, (0, block * HG_W))
    gw["w_in"] = _matmul(s["h1"], dz, ta=True, out_dtype=MXU_DTYPE, name="in_wgrad")
    dh1, got = _matmul(dz, W["w_in"], tb=True, out_dtype=MXU_DTYPE, carry=blocks(SCATTER_C), name="in_dgrad")
    received(SCATTER_C, got)
    dx0, dx0_low, gs["norm1_g"] = _norm_bwd(s["x"], sp["norm1_g"], dh1, E=D, W=D, res=dx1, low=True, name="norm1_bwd")
    gs["attn_qn_g"] = gs["attn_qn_g"][:, :ATT_HEAD_DIM]
    gs["attn_kn_g"] = gs["attn_kn_g"][:, :ATT_HEAD_DIM]
    return dx0, dx0_low, recv, gs, dlb


def kernel(x, mem, norm1_g, w_in, attn_qn_g, attn_kn_g, hg_lb, hg_onorm_g, w_out, norm2_g, mem_norm_g, w_mq, w_mkv, mq_norm_g, mk_norm_g, w_mo, norm3_g, w_ff1, w_ff2, loss_target, m_norm1_g, m_w_in, m_attn_qn_g, m_attn_kn_g, m_hg_lb, m_hg_onorm_g, m_w_out, m_norm2_g, m_mem_norm_g, m_w_mq, m_w_mkv, m_mq_norm_g, m_mk_norm_g, m_w_mo, m_norm3_g, m_w_ff1, m_w_ff2, v_norm1_g, v_w_in, v_attn_qn_g, v_attn_kn_g, v_hg_lb, v_hg_onorm_g, v_w_out, v_norm2_g, v_mem_norm_g, v_w_mq, v_w_mkv, v_mq_norm_g, v_mk_norm_g, v_w_mo, v_norm3_g, v_w_ff1, v_w_ff2):
    given = dict(norm1_g=norm1_g, w_in=w_in, attn_qn_g=attn_qn_g, attn_kn_g=attn_kn_g, hg_lb=hg_lb, hg_onorm_g=hg_onorm_g, w_out=w_out, norm2_g=norm2_g, mem_norm_g=mem_norm_g, w_mq=w_mq, w_mkv=w_mkv, mq_norm_g=mq_norm_g, mk_norm_g=mk_norm_g, w_mo=w_mo, norm3_g=norm3_g, w_ff1=w_ff1, w_ff2=w_ff2, m_norm1_g=m_norm1_g, m_w_in=m_w_in, m_attn_qn_g=m_attn_qn_g, m_attn_kn_g=m_attn_kn_g, m_hg_lb=m_hg_lb, m_hg_onorm_g=m_hg_onorm_g, m_w_out=m_w_out, m_norm2_g=m_norm2_g, m_mem_norm_g=m_mem_norm_g, m_w_mq=m_w_mq, m_w_mkv=m_w_mkv, m_mq_norm_g=m_mq_norm_g, m_mk_norm_g=m_mk_norm_g, m_w_mo=m_w_mo, m_norm3_g=m_norm3_g, m_w_ff1=m_w_ff1, m_w_ff2=m_w_ff2, v_norm1_g=v_norm1_g, v_w_in=v_w_in, v_attn_qn_g=v_attn_qn_g, v_attn_kn_g=v_attn_kn_g, v_hg_lb=v_hg_lb, v_hg_onorm_g=v_hg_onorm_g, v_w_out=v_w_out, v_norm2_g=v_norm2_g, v_mem_norm_g=v_mem_norm_g, v_w_mq=v_w_mq, v_w_mkv=v_w_mkv, v_mq_norm_g=v_mq_norm_g, v_mk_norm_g=v_mk_norm_g, v_w_mo=v_w_mo, v_norm3_g=v_norm3_g, v_w_ff1=v_w_ff1, v_w_ff2=v_w_ff2)
    B, S, D = x.shape
    NM = mem.shape[1]
    L = w_in.shape[0]
    dims = (B, S, NM)
    small_shapes = {n: given[n].shape for n in SMALL}

    def shards(prefix, l):
        return {n: given[prefix + n][l] for n in BIG}

    lower = _lower_bounds_fwd(hg_lb)
    tab = {"attn": _attn_tables(S, _tile(S, ATT_Q_BLOCK, SUBLANES)), "hgrn": _hgrn_pair_table(_tile(S, HG_BLOCK, HG_CHUNK))}
    xf = x.reshape(B * S, D)
    memf = mem.reshape(B * NM, D)

    def small_params(l):
        sp = {"norm1_g": _row(norm1_g[l]), "norm2_g": _row(norm2_g[l]), "norm3_g": _row(norm3_g[l]),
              "mem_norm_g": _row(mem_norm_g[l]), "hg_onorm_g": _row(hg_onorm_g[l]),
              "attn_qn_g": _row(jnp.tile(attn_qn_g[l], ATT_HEADS)), "attn_kn_g": _row(jnp.tile(attn_kn_g[l], ATT_HEADS)),
              "mq_norm_g": _row(jnp.tile(mq_norm_g[l], MEM_HEADS)), "mk_norm_g": _row(jnp.tile(mk_norm_g[l], MEM_HEADS))}
        return sp, _row(lower[l])

    w_in_full = None
    saved, weights = [], []
    h = xf
    for l in range(L):
        sp, lb = small_params(l)
        next_in = given["w_in"][l + 1] if l + 1 < L else None
        h, s, W, w_in_full = _layer_fwd(h, memf, w_in_full, shards("", l), next_in, sp, lb, tab, dims)
        saved.append(s)
        weights.append(W)

    dh, dh_low, loss_part = _loss_head(h, loss_target.reshape(B * S, D))

    recv_layers, gs_layers, dlb_layers = [None] * L, [None] * L, [None] * L
    for l in range(L - 1, -1, -1):
        sp, lb = small_params(l)
        dh, dh_low, recv_layers[l], gs_layers[l], dlb_layers[l] = _layer_bwd(dh, dh_low, saved[l], memf, weights[l],
                                                                              sp, lb, tab, dims)
    grad_x = dh.reshape(B, S, D)

    big_out = {}
    for n in BIG:
        outs = None
        for l in range(L):
            outs = _sum_adamw(recv_layers[l][n], given[n], given["m_" + n], given["v_" + n], l, outs)
        big_out[n] = outs

    gs = {n: jnp.stack([gs_layers[l][n].reshape(small_shapes[n][1:]) for l in range(L)]) for n in SMALL if n != "hg_lb"}
    gs["hg_lb"] = _lower_bounds_bwd(hg_lb, jnp.concatenate(dlb_layers, axis=0))
    packed = _small_sum_adamw(_pack_small(gs, "", loss_part[0, 0]), _pack_small(given, ""), _pack_small(given, "m_"),
                              _pack_small(given, "v_"))
    small_out, loss = [], None
    for t in packed:
        d, tail = _unpack_small(t, small_shapes)
        small_out.append(d)
        loss = tail if loss is None else loss

    outs = [loss, grad_x]
    for k in range(4):
        outs += [small_out[k][n] if n in small_shapes else big_out[n][k] for n in WEIGHTS]
    return tuple(outs)
```
